```python
import jax, jax.numpy as jnp
from jax import lax
import numpy as np

D_MODEL = 2048
BATCH = 2
SEQ = 4096
DEPTH = 2

HEAD_DIM = 128
MIX_WIDTH = D_MODEL
MLSTM_WIDTH = MIX_WIDTH // 4
MLSTM_HEADS = MLSTM_WIDTH // HEAD_DIM
MLSTM_CHUNK = 64
MOBA_WIDTH = MIX_WIDTH // 2
MOBA_HEADS = MOBA_WIDTH // HEAD_DIM
MOBA_BLOCK = 256
MOBA_TOPK = 3
MOBA_QBLOCK = 32
ROPE_THETA = 500000.0
ROPE_DIM = HEAD_DIM // 4
GMLP_WIDTH = MIX_WIDTH // 4
GMLP_GROUP_DIM = 128
GMLP_GROUPS = GMLP_WIDTH // GMLP_GROUP_DIM
GMLP_CHUNK = 128
N_BRANCHES = 3
D_FF = 4 * D_MODEL
PLE_DIM = 256
NORM_EPS = 1e-6

IN_SPLITS = (MLSTM_WIDTH, MLSTM_WIDTH, MLSTM_WIDTH, MLSTM_WIDTH, MLSTM_HEADS, MLSTM_HEADS,
             MOBA_WIDTH, MOBA_WIDTH, MOBA_WIDTH, GMLP_WIDTH, GMLP_WIDTH, N_BRANCHES * D_MODEL)
IN_COLS = sum(IN_SPLITS)
IN_OFFSETS = np.cumsum(IN_SPLITS)[:-1].tolist()

kernel_name = 'hybrid_mlstm_moba_gmlp_block'


def rmsnorm(x, g):
    xf = x.astype(jnp.float32)
    y = xf * lax.rsqrt(jnp.mean(xf * xf, axis=-1, keepdims=True) + NORM_EPS)
    return (y * g.astype(jnp.float32)).astype(x.dtype)


def layernorm(x, g, b):
    xf = x.astype(jnp.float32)
    mu = jnp.mean(xf, axis=-1, keepdims=True)
    xc = xf - mu
    y = xc * lax.rsqrt(jnp.mean(xc * xc, axis=-1, keepdims=True) + NORM_EPS)
    return (y * g.astype(jnp.float32) + b.astype(jnp.float32)).astype(x.dtype)


def to_heads(t, n):
    b, s, _ = t.shape
    return t.reshape(b, s, n, -1).transpose(0, 2, 1, 3)


def from_heads(t):
    b, h, s, d = t.shape
    return t.transpose(0, 2, 1, 3).reshape(b, s, h * d)


def partial_rotary(t):
    s = t.shape[2]
    half = ROPE_DIM // 2
    inv_freq = ROPE_THETA ** (-jnp.arange(0, ROPE_DIM, 2, dtype=jnp.float32) / ROPE_DIM)
    ang = jnp.arange(s, dtype=jnp.float32)[:, None] * inv_freq[None, :]
    cos = jnp.cos(ang).astype(t.dtype)
    sin = jnp.sin(ang).astype(t.dtype)
    t1 = t[..., :half]
    t2 = t[..., half:ROPE_DIM]
    return jnp.concatenate([t1 * cos - t2 * sin, t2 * cos + t1 * sin, t[..., ROPE_DIM:]], axis=-1)


def mlstm_chunkwise(q, k, v, i_pre, f_pre):
    b, h, s, d = q.shape
    f32 = jnp.float32
    L = MLSTM_CHUNK
    nc = s // L
    q = q.astype(f32)
    k = k.astype(f32) * (d ** -0.5)
    v = v.astype(f32)
    log_f = jax.nn.log_sigmoid(f_pre.astype(f32))
    i_g = i_pre.astype(f32)

    def chunks(a):
        return jnp.moveaxis(a.reshape((b, h, nc, L) + a.shape[3:]), 2, 0)

    xs = (chunks(q), chunks(k), chunks(v), chunks(log_f), chunks(i_g))
    causal = jnp.tril(jnp.ones((L, L), dtype=bool))

    def step(carry, inp):
        C, n, m = carry
        qc, kc, vc, lf, ig = inp
        g = jnp.cumsum(lf, axis=-1)
        log_w = jnp.where(causal, g[..., :, None] - g[..., None, :] + ig[..., None, :], -jnp.inf)
        log_a = g + m[..., None]
        m_row = jnp.maximum(jnp.max(log_w, axis=-1), log_a)
        qk = jnp.einsum('bhld,bhsd->bhls', qc, kc) * jnp.exp(log_w - m_row[..., None])
        a = jnp.exp(log_a - m_row)
        num = jnp.einsum('bhls,bhsd->bhld', qk, vc) + a[..., None] * jnp.einsum('bhvk,bhlk->bhlv', C, qc)
        den = jnp.sum(qk, axis=-1) + a * jnp.einsum('bhk,bhlk->bhl', n, qc)
        h_out = num / jnp.maximum(jnp.abs(den), jnp.exp(-m_row))[..., None]
        g_last = g[..., -1]
        log_u = g_last[..., None] - g + ig
        m_new = jnp.maximum(g_last + m, jnp.max(log_u, axis=-1))
        decay = jnp.exp(g_last + m - m_new)
        u = jnp.exp(log_u - m_new[..., None])
        C = decay[..., None, None] * C + jnp.einsum('bhl,bhlv,bhlk->bhvk', u, vc, kc)
        n = decay[..., None] * n + jnp.einsum('bhl,bhlk->bhk', u, kc)
        return (C, n, m_new), h_out

    init = (jnp.zeros((b, h, d, d), f32), jnp.zeros((b, h, d), f32), jnp.zeros((b, h), f32))
    _, hs = lax.scan(step, init, xs)
    return jnp.moveaxis(hs, 0, 2).reshape(b, h, s, d)


def moba_attention(q, k, v):
    b, h, s, d = q.shape
    blk = MOBA_BLOCK
    nb = -(-s // blk)
    pad = nb * blk - s
    kp = jnp.pad(k, ((0, 0), (0, 0), (0, pad), (0, 0))).reshape(b, h, nb, blk, d)
    vp = jnp.pad(v, ((0, 0), (0, 0), (0, pad), (0, 0))).reshape(b, h, nb, blk, d)
    k_mean = jnp.mean(kp.astype(jnp.float32), axis=3)
    q_blk = jnp.arange(s) // blk
    gate = jnp.einsum('bhsd,bhnd->bhsn', q.astype(jnp.float32), k_mean)
    fully_past = jnp.arange(nb)[None, :] < q_blk[:, None]
    gate = jnp.where(fully_past, gate, -jnp.inf)
    k_top = min(MOBA_TOPK, nb)
    _, sel = lax.top_k(gate, k_top)
    sel_valid = jnp.arange(k_top)[None, :] < q_blk[:, None]
    scale = d ** -0.5
    b_idx = jnp.arange(b)[:, None, None, None]
    h_idx = jnp.arange(h)[None, :, None, None]
    qb = MOBA_QBLOCK

    def one_block(s0):
        qc = lax.dynamic_slice_in_dim(q, s0, qb, axis=2)
        sel_c = lax.dynamic_slice_in_dim(sel, s0, qb, axis=2)
        valid_c = lax.dynamic_slice_in_dim(sel_valid, s0, qb, axis=0)
        k_sel = kp[b_idx, h_idx, sel_c]
        v_sel = vp[b_idx, h_idx, sel_c]
        own = s0 // blk
        k_own = lax.dynamic_index_in_dim(kp, own, axis=2, keepdims=False)
        v_own = lax.dynamic_index_in_dim(vp, own, axis=2, keepdims=False)
        s_sel = jnp.einsum('bhqd,bhqjkd->bhqjk', qc, k_sel).astype(jnp.float32) * scale
        s_sel = jnp.where(valid_c[None, None, :, :, None], s_sel, -jnp.inf).reshape(b, h, qb, k_top * blk)
        s_own = jnp.einsum('bhqd,bhkd->bhqk', qc, k_own).astype(jnp.float32) * scale
        key_pos = own * blk + jnp.arange(blk)
        q_pos = s0 + jnp.arange(qb)
        s_own = jnp.where(key_pos[None, :] <= q_pos[:, None], s_own, -jnp.inf)
        probs = jax.nn.softmax(jnp.concatenate([s_sel, s_own], axis=-1), axis=-1).astype(v.dtype)
        p_sel = probs[..., :k_top * blk].reshape(b, h, qb, k_top, blk)
        p_own = probs[..., k_top * blk:]
        return (jnp.einsum('bhqjk,bhqjkd->bhqd', p_sel, v_sel)
                + jnp.einsum('bhqk,bhkd->bhqd', p_own, v_own))

    outs = lax.map(one_block, jnp.arange(s // qb) * qb)
    return jnp.moveaxis(outs, 0, 2).reshape(b, h, s, d)


def spatial_gating(u, v, norm_g, norm_b, w_s, b_s):
    b, s, _ = v.shape
    nc = s // GMLP_CHUNK
    v = layernorm(v, norm_g, norm_b).reshape(b, nc, GMLP_CHUNK, GMLP_GROUPS, GMLP_GROUP_DIM)
    w = w_s * jnp.tril(jnp.ones((GMLP_CHUNK, GMLP_CHUNK), dtype=w_s.dtype))[None]
    mixed = jnp.einsum('gts,bcsge->bctge', w, v) + b_s.T[None, None, :, :, None]
    return u * mixed.reshape(b, s, GMLP_WIDTH)


def setup_inputs(seed: int = 0) -> dict:
    key = jax.random.key(seed)
    ks = jax.random.split(key, 24)

    def nrm(k, shape, scale):
        return jax.random.normal(k, shape, jnp.float32) * scale

    def gain(k, shape):
        return 1.0 + nrm(k, shape, 0.05)

    gate_b = jnp.concatenate([nrm(ks[3], (DEPTH, MLSTM_HEADS), 0.1),
                              3.0 + nrm(ks[4], (DEPTH, MLSTM_HEADS), 0.5)], axis=-1)
    return {
        'x': nrm(ks[0], (BATCH, SEQ, D_MODEL), 1.0),
        'p': nrm(ks[1], (DEPTH, BATCH, SEQ, PLE_DIM), 1.0),
        'norm_mix_g': gain(ks[2], (DEPTH, D_MODEL)),
        'w_in': nrm(ks[5], (DEPTH, D_MODEL, IN_COLS), D_MODEL ** -0.5),
        'mlstm_gate_b': gate_b,
        'mlstm_norm_g': gain(ks[6], (DEPTH, MLSTM_WIDTH)),
        'gmlp_norm_g': gain(ks[7], (DEPTH, GMLP_WIDTH)),
        'gmlp_norm_b': nrm(ks[8], (DEPTH, GMLP_WIDTH), 0.02),
        'gmlp_ws': nrm(ks[9], (DEPTH, GMLP_GROUPS, GMLP_CHUNK, GMLP_CHUNK), 0.5 * GMLP_CHUNK ** -0.5),
        'gmlp_bs': 1.0 + nrm(ks[10], (DEPTH, GMLP_GROUPS, GMLP_CHUNK), 0.1),
        'w_branch_a': nrm(ks[11], (DEPTH, MLSTM_WIDTH, D_MODEL), MLSTM_WIDTH ** -0.5),
        'w_branch_b': nrm(ks[12], (DEPTH, MOBA_WIDTH, D_MODEL), MOBA_WIDTH ** -0.5),
        'w_branch_c': nrm(ks[13], (DEPTH, GMLP_WIDTH, D_MODEL), GMLP_WIDTH ** -0.5),
        'w_out': nrm(ks[14], (DEPTH, D_MODEL, D_MODEL), D_MODEL ** -0.5),
        'norm_mlp_g': gain(ks[15], (DEPTH, D_MODEL)),
        'w_mlp_up': nrm(ks[16], (DEPTH, D_MODEL, D_FF), D_MODEL ** -0.5),
        'w_mlp_down': nrm(ks[17], (DEPTH, D_FF, D_MODEL), D_FF ** -0.5),
        'norm_ple_g': gain(ks[18], (DEPTH, D_MODEL)),
        'w_ple_gate': nrm(ks[19], (DEPTH, D_MODEL, D_MODEL), D_MODEL ** -0.5),
        'w_ple_proj': nrm(ks[20], (DEPTH, PLE_DIM, D_MODEL), PLE_DIM ** -0.5),
        'final_norm_g': gain(ks[21], (D_MODEL,)),
    }


def reference(x, p, norm_mix_g, w_in, mlstm_gate_b, mlstm_norm_g, gmlp_norm_g, gmlp_norm_b,
              gmlp_ws, gmlp_bs, w_branch_a, w_branch_b, w_branch_c, w_out, norm_mlp_g,
              w_mlp_up, w_mlp_down, norm_ple_g, w_ple_gate, w_ple_proj, final_norm_g):
    b, s, _ = x.shape
    for i in range(DEPTH):
        h = rmsnorm(x, norm_mix_g[i])
        z = h @ w_in[i]
        (qa, ka, va, oa, ia, fa, qb_, kb_, vb_, uc, vc, zg) = jnp.split(z, IN_OFFSETS, axis=-1)

        i_pre = (ia + mlstm_gate_b[i, :MLSTM_HEADS]).transpose(0, 2, 1)
        f_pre = (fa + mlstm_gate_b[i, MLSTM_HEADS:]).transpose(0, 2, 1)
        h_til = mlstm_chunkwise(to_heads(qa, MLSTM_HEADS), to_heads(ka, MLSTM_HEADS),
                                to_heads(va, MLSTM_HEADS), i_pre, f_pre).astype(x.dtype)
        h_til = from_heads(h_til).reshape(b, s, MLSTM_HEADS, HEAD_DIM)
        y_a = rmsnorm(h_til, jnp.ones((HEAD_DIM,), x.dtype)).reshape(b, s, MLSTM_WIDTH)
        y_a = jax.nn.sigmoid(oa) * (y_a * mlstm_norm_g[i])

        qh = partial_rotary(to_heads(qb_, MOBA_HEADS))
        kh = partial_rotary(to_heads(kb_, MOBA_HEADS))
        y_b = from_heads(moba_attention(qh, kh, to_heads(vb_, MOBA_HEADS)))

        y_c = spatial_gating(jax.nn.gelu(uc), jax.nn.gelu(vc), gmlp_norm_g[i], gmlp_norm_b[i],
                             gmlp_ws[i], gmlp_bs[i])

        gates = jax.nn.sigmoid(zg).reshape(b, s, N_BRANCHES, D_MODEL)
        merged = (gates[:, :, 0] * (y_a @ w_branch_a[i])
                  + gates[:, :, 1] * (y_b @ w_branch_b[i])
                  + gates[:, :, 2] * (y_c @ w_branch_c[i]))
        x = x + merged @ w_out[i]

        h2 = rmsnorm(x, norm_mlp_g[i])
        x = x + jnp.square(jax.nn.relu(h2 @ w_mlp_up[i])) @ w_mlp_down[i]

        g_ple = jax.nn.sigmoid(rmsnorm(x, norm_ple_g[i]) @ w_ple_gate[i])
        x = x + g_ple * (p[i] @ w_ple_proj[i])
    return rmsnorm(x, final_norm_g)
```

```python
import functools

import jax
import jax.numpy as jnp
import numpy as np
from jax import lax
from jax.experimental import pallas as pl
from jax.experimental.pallas import tpu as pltpu

F32 = jnp.float32
BF16 = jnp.bfloat16

HEAD_DIM = 128
MLSTM_HEADS = 4
MLSTM_WIDTH = MLSTM_HEADS * HEAD_DIM
MOBA_HEADS = 8
MOBA_WIDTH = MOBA_HEADS * HEAD_DIM
MOBA_BLOCK = 256
MOBA_TOPK = 3
ROPE_THETA = 500000.0
ROPE_DIM = HEAD_DIM // 4
GMLP_WIDTH = 512
GMLP_GROUPS = 4
GMLP_CHUNK = 128
N_BRANCHES = 3
NORM_EPS = 1e-6

LANES = 128
V7X_VMEM_BYTES = 64 * 1024 * 1024
VMEM_CEILING = V7X_VMEM_BYTES - 8 * 1024 * 1024

MLSTM_KERNEL_CHUNK = 128
GATE_LANES = LANES
MASK_BIG = 2.0 ** 100

NT_DIMS = (((1,), (1,)), ((), ()))
TN_DIMS = (((0,), (0,)), ((), ()))


def _nbytes(shape, dtype):
    return int(np.prod(shape)) * jnp.dtype(dtype).itemsize


def _params(semantics, blocks, scratch=()):
    need = 2 * sum(_nbytes(s, d) for s, d in blocks) + sum(_nbytes(s, d) for s, d in scratch)
    limit = min(VMEM_CEILING, need + need // 4 + 4 * 1024 * 1024)
    return pltpu.CompilerParams(dimension_semantics=semantics, vmem_limit_bytes=limit)


def _rmsnorm_kernel(x_ref, g_ref, o_ref):
    x = x_ref[...]
    y = x * lax.rsqrt(jnp.mean(x * x, axis=-1, keepdims=True) + NORM_EPS)
    o_ref[...] = (y * g_ref[...]).astype(o_ref.dtype)


def _rmsnorm(x, g, out_dtype, tm=512):
    m, d = x.shape
    return pl.pallas_call(
        _rmsnorm_kernel,
        grid=(m // tm,),
        in_specs=[pl.BlockSpec((tm, d), lambda i: (i, 0)),
                  pl.BlockSpec((1, d), lambda i: (0, 0))],
        out_specs=pl.BlockSpec((tm, d), lambda i: (i, 0)),
        out_shape=jax.ShapeDtypeStruct((m, d), out_dtype),
        compiler_params=_params(("arbitrary",), [((tm, d), F32), ((tm, d), out_dtype)]),
        name="rmsnorm",
    )(x, g.reshape(1, d))


def _inproj_kernel(h_ref, w_ref, wif_ref, cs_ref, z_ref, zif_ref):
    acc = jnp.dot(h_ref[...], w_ref[...], preferred_element_type=F32)
    z_ref[...] = acc * cs_ref[...]

    @pl.when(pl.program_id(1) == 0)
    def _():
        zif_ref[...] = jnp.dot(h_ref[...], wif_ref[...], preferred_element_type=F32)


def _inproj(h, w_main, w_if, colscale, tm=1024, tn=512):
    m, k = h.shape
    n = w_main.shape[1]
    blocks = [((tm, k), BF16), ((k, tn), BF16), ((k, GATE_LANES), BF16), ((1, tn), F32),
              ((tm, tn), F32), ((tm, GATE_LANES), F32)]
    return pl.pallas_call(
        _inproj_kernel,
        grid=(m // tm, n // tn),
        in_specs=[pl.BlockSpec((tm, k), lambda i, j: (i, 0)),
                  pl.BlockSpec((k, tn), lambda i, j: (0, j)),
                  pl.BlockSpec((k, GATE_LANES), lambda i, j: (0, 0)),
                  pl.BlockSpec((1, tn), lambda i, j: (0, j))],
        out_specs=[pl.BlockSpec((tm, tn), lambda i, j: (i, j)),
                   pl.BlockSpec((tm, GATE_LANES), lambda i, j: (i, 0))],
        out_shape=[jax.ShapeDtypeStruct((m, n), F32),
                   jax.ShapeDtypeStruct((m, GATE_LANES), F32)],
        compiler_params=_params(("arbitrary", "arbitrary"), blocks),
        name="in_proj",
    )(h, w_main, w_if, colscale)


def _log_sigmoid(x):
    return jnp.minimum(x, 0.0) - jnp.log1p(jnp.exp(-jnp.abs(x)))


def _mlstm_kernel(q_ref, k_ref, v_ref, o_ref, zif_ref, gb_ref, ng_ref, y_ref, c_sc, n_sc, m_sc):
    nbatch, chunk, _ = q_ref.shape
    heads = MLSTM_HEADS

    @pl.when(pl.program_id(0) == 0)
    def _():
        c_sc[...] = jnp.zeros_like(c_sc)
        n_sc[...] = jnp.zeros_like(n_sc)
        m_sc[...] = jnp.zeros_like(m_sc)

    row = lax.broadcasted_iota(jnp.int32, (chunk, chunk), 0)
    col = lax.broadcasted_iota(jnp.int32, (chunk, chunk), 1)
    causal = col <= row
    tril = jnp.where(causal, 1.0, 0.0).astype(F32)
    lane = lax.broadcasted_iota(jnp.int32, (chunk, GATE_LANES), 1)

    for b in range(nbatch):
        pre = zif_ref[b] + gb_ref[...]
        gates = jnp.where(lane < heads, pre, _log_sigmoid(pre))
        gcum = jnp.dot(tril, gates, preferred_element_type=F32,
                       precision=lax.Precision.HIGHEST)
        gates_t = gates.T
        gcum_t = gcum.T
        for h in range(heads):
            s = b * heads + h
            sl = slice(h * HEAD_DIM, (h + 1) * HEAD_DIM)
            q = q_ref[b, :, sl]
            k = k_ref[b, :, sl]
            v = v_ref[b, :, sl]
            qb = q.astype(BF16)
            kb = k.astype(BF16)
            vb = v.astype(BF16)
            g_t = gcum[:, heads + h:heads + h + 1]
            g_s = gcum_t[heads + h:heads + h + 1, :]
            i_s = gates_t[h:h + 1, :]
            i_t = gates[:, h:h + 1]
            m_prev = m_sc[s][:, 0:1]
            c_prev = c_sc[s]
            n_prev = n_sc[s]

            log_w = jnp.where(causal, g_t - g_s + i_s, -jnp.inf)
            log_a = g_t + m_prev
            m_row = jnp.maximum(jnp.max(log_w, axis=1, keepdims=True), log_a)
            qk = lax.dot_general(qb, kb, NT_DIMS, preferred_element_type=F32) * jnp.exp(log_w - m_row)
            a = jnp.exp(log_a - m_row)
            num = (jnp.dot(qk.astype(BF16), vb, preferred_element_type=F32)
                   + a * lax.dot_general(qb, c_prev.astype(BF16), NT_DIMS, preferred_element_type=F32))
            den = (jnp.sum(qk, axis=1, keepdims=True)
                   + a * jnp.sum(q * n_prev, axis=1, keepdims=True))
            h_out = num / jnp.maximum(jnp.abs(den), jnp.exp(-m_row))

            g_last = gcum[chunk - 1:chunk, heads + h:heads + h + 1]
            log_u = g_last - g_t + i_t
            m_new = jnp.maximum(g_last + m_prev, jnp.max(log_u, axis=0, keepdims=True))
            decay = jnp.exp(g_last + m_prev - m_new)
            u = jnp.exp(log_u - m_new)
            c_sc[s] = decay * c_prev + lax.dot_general((u * v).astype(BF16), kb, TN_DIMS,
                                                       preferred_element_type=F32)
            n_sc[s] = decay * n_prev + jnp.sum(u * k, axis=0, keepdims=True)
            m_sc[s] = jnp.broadcast_to(m_new, (1, LANES))

            yn = h_out * lax.rsqrt(jnp.mean(h_out * h_out, axis=-1, keepdims=True) + NORM_EPS)
            y_ref[b, :, sl] = (jax.nn.sigmoid(o_ref[b, :, sl]) * (yn * ng_ref[:, sl])).astype(y_ref.dtype)


def _mlstm(z3, zif3, gate_b, norm_g):
    nbatch, seq, _ = z3.shape
    chunk = MLSTM_KERNEL_CHUNK
    w = MLSTM_WIDTH
    streams = nbatch * MLSTM_HEADS
    gb = jnp.pad(gate_b, (0, GATE_LANES - gate_b.shape[0])).reshape(1, GATE_LANES)
    blocks = [((nbatch, chunk, w), F32)] * 4 + [((nbatch, chunk, GATE_LANES), F32),
                                               ((nbatch, chunk, w), BF16)]
    scratch = [((streams, HEAD_DIM, HEAD_DIM), F32), ((streams, 1, LANES), F32), ((streams, 1, LANES), F32)]

    def zcol(cb):
        return pl.BlockSpec((nbatch, chunk, w), lambda c: (0, c, cb))

    return pl.pallas_call(
        _mlstm_kernel,
        grid=(seq // chunk,),
        in_specs=[zcol(0), zcol(1), zcol(2), zcol(3),
                  pl.BlockSpec((nbatch, chunk, GATE_LANES), lambda c: (0, c, 0)),
                  pl.BlockSpec((1, GATE_LANES), lambda c: (0, 0)),
                  pl.BlockSpec((1, w), lambda c: (0, 0))],
        out_specs=pl.BlockSpec((nbatch, chunk, w), lambda c: (0, c, 0)),
        out_shape=jax.ShapeDtypeStruct((nbatch, seq, w), BF16),
        scratch_shapes=[pltpu.VMEM(s, d) for s, d in scratch],
        compiler_params=_params(("arbitrary",), blocks, scratch),
        name="mlstm",
    )(z3, z3, z3, z3, zif3, gb, norm_g.reshape(1, w))


def _rope_tables(seq):
    half = ROPE_DIM // 2
    inv_freq = ROPE_THETA ** (-jnp.arange(0, ROPE_DIM, 2, dtype=F32) / ROPE_DIM)
    ang = jnp.arange(seq, dtype=F32)[:, None] * inv_freq[None, :]
    cos = jnp.cos(ang)
    sin = jnp.sin(ang)
    ones = jnp.ones((seq, HEAD_DIM - ROPE_DIM), F32)
    cos_tab = jnp.concatenate([cos, cos, ones], axis=1)
    sin_tab = jnp.concatenate([-sin, sin, 0.0 * ones], axis=1)
    assert cos_tab.shape == (seq, HEAD_DIM) and half * 2 == ROPE_DIM
    return cos_tab, sin_tab


def _rotary(t, cos, sin):
    half = ROPE_DIM // 2
    lane = lax.broadcasted_iota(jnp.int32, t.shape, 1)
    upper = pltpu.roll(t, HEAD_DIM - half, axis=1)
    lower = pltpu.roll(t, half, axis=1)
    partner = jnp.where(lane < half, upper, lower)
    return jnp.where(lane < ROPE_DIM, t * cos + partner * sin, t)


def _moba_kernel(q_ref, k_ref, v_ref, cosq_ref, sinq_ref, cos_ref, sin_ref, o_ref, kb_sc, vb_sc, km_sc):
    blk = MOBA_BLOCK
    seq = k_ref.shape[1]
    nblk = seq // blk
    j = pl.program_id(2)
    scale = HEAD_DIM ** -0.5

    @pl.when(j == 0)
    def _():
        km_sc[...] = jnp.zeros_like(km_sc)
        for n in range(nblk):
            rows = slice(n * blk, (n + 1) * blk)
            kk = _rotary(k_ref[0, rows, :], cos_ref[rows, :], sin_ref[rows, :])
            kb_sc[rows, :] = kk.astype(BF16)
            km_sc[n:n + 1, :] = jnp.mean(kk, axis=0, keepdims=True)
            vb_sc[rows, :] = v_ref[0, rows, :].astype(BF16)

    qf = _rotary(q_ref[0], cosq_ref[...], sinq_ref[...])
    qb = qf.astype(BF16)

    lane = lax.broadcasted_iota(jnp.int32, (blk, LANES), 1)
    lane_f = lane.astype(F32)
    gate = lax.dot_general(qf, km_sc[...], NT_DIMS, preferred_element_type=F32,
                           precision=lax.Precision.HIGHEST)
    gate = jnp.where(lane < j, gate, -jnp.inf)
    sel_m1 = jnp.full((blk, LANES), -1.0, F32)
    for _ in range(MOBA_TOPK):
        mx = jnp.max(gate, axis=1, keepdims=True)
        first = jnp.min(jnp.where(gate == mx, lane_f, float(LANES)), axis=1, keepdims=True)
        first = jnp.where(mx > -jnp.inf, first, -1.0)
        hit = lane_f == first
        sel_m1 = jnp.where(hit, 0.0, sel_m1)
        gate = jnp.where(hit, -jnp.inf, gate)
    q_aug = jnp.concatenate([qb, sel_m1.astype(BF16)], axis=1)

    r0 = pl.multiple_of(j * blk, blk)
    row = lax.broadcasted_iota(jnp.int32, (blk, blk), 0)
    col = lax.broadcasted_iota(jnp.int32, (blk, blk), 1)
    s_own = lax.dot_general(qb, kb_sc[pl.ds(r0, blk), :], NT_DIMS, preferred_element_type=F32) * scale
    s_own = jnp.where(col <= row, s_own, -jnp.inf)
    m0 = jnp.max(s_own, axis=1, keepdims=True)
    p0 = jnp.exp(s_own - m0)
    l0 = jnp.sum(p0, axis=1, keepdims=True)
    acc0 = jnp.dot(p0.astype(BF16), vb_sc[pl.ds(r0, blk), :], preferred_element_type=F32)

    def past_block(n, carry):
        m_prev, l_prev, acc = carry
        c0 = pl.multiple_of(n * blk, blk)
        onehot = jnp.where(lane == n, MASK_BIG, 0.0).astype(BF16)
        k_aug = jnp.concatenate([kb_sc[pl.ds(c0, blk), :], onehot], axis=1)
        s = lax.dot_general(q_aug, k_aug, NT_DIMS, preferred_element_type=F32) * scale
        m_new = jnp.maximum(m_prev, jnp.max(s, axis=1, keepdims=True))
        alpha = jnp.exp(m_prev - m_new)
        p = jnp.exp(s - m_new)
        l_new = alpha * l_prev + jnp.sum(p, axis=1, keepdims=True)
        acc = alpha * acc + jnp.dot(p.astype(BF16), vb_sc[pl.ds(c0, blk), :], preferred_element_type=F32)
        return m_new, l_new, acc

    _, l_fin, acc = lax.fori_loop(0, j, past_block, (m0, l0, acc0))
    o_ref[0] = (acc / l_fin).astype(o_ref.dtype)


def _moba(z3, cos_tab, sin_tab, q_cb, k_cb, v_cb):
    nbatch, seq, _ = z3.shape
    blk = MOBA_BLOCK
    d = HEAD_DIM
    blocks = [((1, blk, d), F32), ((1, seq, d), F32), ((1, seq, d), F32),
              ((blk, d), F32), ((blk, d), F32), ((seq, d), F32), ((seq, d), F32), ((1, blk, d), BF16)]
    scratch = [((seq, d), BF16), ((seq, d), BF16), ((LANES, d), F32)]
    return pl.pallas_call(
        _moba_kernel,
        grid=(nbatch, MOBA_HEADS, seq // blk),
        in_specs=[pl.BlockSpec((1, blk, d), lambda b, h, j: (b, j, q_cb + h)),
                  pl.BlockSpec((1, seq, d), lambda b, h, j: (b, 0, k_cb + h)),
                  pl.BlockSpec((1, seq, d), lambda b, h, j: (b, 0, v_cb + h)),
                  pl.BlockSpec((blk, d), lambda b, h, j: (j, 0)),
                  pl.BlockSpec((blk, d), lambda b, h, j: (j, 0)),
                  pl.BlockSpec((seq, d), lambda b, h, j: (0, 0)),
                  pl.BlockSpec((seq, d), lambda b, h, j: (0, 0))],
        out_specs=pl.BlockSpec((1, blk, d), lambda b, h, j: (b, j, h)),
        out_shape=jax.ShapeDtypeStruct((nbatch, seq, MOBA_WIDTH), BF16),
        scratch_shapes=[pltpu.VMEM(s, dt) for s, dt in scratch],
        compiler_params=_params(("arbitrary", "arbitrary", "arbitrary"), blocks, scratch),
        name="moba",
    )(z3, z3, z3, cos_tab, sin_tab, cos_tab, sin_tab)


def _gelu_tanh(x):
    c = np.sqrt(2.0 / np.pi).astype(np.float32)
    return x * (0.5 * (1.0 + jnp.tanh(c * (x + 0.044715 * (x * x * x)))))


def _gmlp_kernel(u_ref, v_ref, lg_ref, lb_ref, ws_ref, bst_ref, y_ref):
    rows = u_ref.shape[1]
    t = GMLP_CHUNK
    gd = GMLP_WIDTH // GMLP_GROUPS
    v = _gelu_tanh(v_ref[0])
    mu = jnp.mean(v, axis=-1, keepdims=True)
    vc = v - mu
    vln = vc * lax.rsqrt(jnp.mean(vc * vc, axis=-1, keepdims=True) + NORM_EPS) * lg_ref[...] + lb_ref[...]
    vb = vln.astype(BF16)
    row = lax.broadcasted_iota(jnp.int32, (t, t), 0)
    col = lax.broadcasted_iota(jnp.int32, (t, t), 1)
    for g in range(GMLP_GROUPS):
        wg = jnp.where(col <= row, ws_ref[g], 0.0).astype(BF16)
        bias = bst_ref[:, g:g + 1]
        cols = slice(g * gd, (g + 1) * gd)
        for c in range(rows // t):
            rs = slice(c * t, (c + 1) * t)
            mixed = jnp.dot(wg, vb[rs, cols], preferred_element_type=F32) + bias
            y_ref[0, rs, cols] = (_gelu_tanh(u_ref[0, rs, cols]) * mixed).astype(y_ref.dtype)


def _gmlp(z3, u_cb, v_cb, ln_g, ln_b, ws, bs, rows=512):
    nbatch, seq, _ = z3.shape
    w = GMLP_WIDTH
    t = GMLP_CHUNK
    blocks = [((1, rows, w), F32), ((1, rows, w), F32), ((GMLP_GROUPS, t, t), F32), ((1, rows, w), BF16)]
    return pl.pallas_call(
        _gmlp_kernel,
        grid=(nbatch, seq // rows),
        in_specs=[pl.BlockSpec((1, rows, w), lambda b, c: (b, c, u_cb)),
                  pl.BlockSpec((1, rows, w), lambda b, c: (b, c, v_cb)),
                  pl.BlockSpec((1, w), lambda b, c: (0, 0)),
                  pl.BlockSpec((1, w), lambda b, c: (0, 0)),
                  pl.BlockSpec((GMLP_GROUPS, t, t), lambda b, c: (0, 0, 0)),
                  pl.BlockSpec((t, GMLP_GROUPS), lambda b, c: (0, 0))],
        out_specs=pl.BlockSpec((1, rows, w), lambda b, c: (b, c, 0)),
        out_shape=jax.ShapeDtypeStruct((nbatch, seq, w), BF16),
        compiler_params=_params(("arbitrary", "arbitrary"), blocks),
        name="gmlp",
    )(z3, z3, ln_g.reshape(1, w), ln_b.reshape(1, w), ws, bs.T)


def _merge_kernel(h_ref, wga_ref, wgb_ref, wgc_ref, ya_ref, yb_ref, yc_ref, wa_ref, wb_ref, wc_ref, o_ref):
    h = h_ref[...]

    def branch(wg_ref, y_ref, w_ref):
        gate = jax.nn.sigmoid(jnp.dot(h, wg_ref[...], preferred_element_type=F32))
        return gate * jnp.dot(y_ref[...], w_ref[...], preferred_element_type=F32)

    merged = branch(wga_ref, ya_ref, wa_ref) + branch(wgb_ref, yb_ref, wb_ref) + branch(wgc_ref, yc_ref, wc_ref)
    o_ref[...] = merged.astype(o_ref.dtype)


def _merge(h, w_gate, ya, yb, yc, wa, wb, wc, tm=1024, tn=512):
    m, d = h.shape
    nblk = d // tn
    ka, kb, kc = ya.shape[1], yb.shape[1], yc.shape[1]
    blocks = ([((tm, d), BF16)] + [((d, tn), BF16)] * 3
              + [((tm, ka), BF16), ((tm, kb), BF16), ((tm, kc), BF16)]
              + [((ka, tn), BF16), ((kb, tn), BF16), ((kc, tn), BF16), ((tm, tn), BF16)])

    def gate_spec(branch):
        return pl.BlockSpec((d, tn), lambda i, j: (0, branch * nblk + j))

    return pl.pallas_call(
        _merge_kernel,
        grid=(m // tm, nblk),
        in_specs=[pl.BlockSpec((tm, d), lambda i, j: (i, 0)),
                  gate_spec(0), gate_spec(1), gate_spec(2),
                  pl.BlockSpec((tm, ka), lambda i, j: (i, 0)),
                  pl.BlockSpec((tm, kb), lambda i, j: (i, 0)),
                  pl.BlockSpec((tm, kc), lambda i, j: (i, 0)),
                  pl.BlockSpec((ka, tn), lambda i, j: (0, j)),
                  pl.BlockSpec((kb, tn), lambda i, j: (0, j)),
                  pl.BlockSpec((kc, tn), lambda i, j: (0, j))],
        out_specs=pl.BlockSpec((tm, tn), lambda i, j: (i, j)),
        out_shape=jax.ShapeDtypeStruct((m, d), BF16),
        compiler_params=_params(("arbitrary", "arbitrary"), blocks),
        name="merge",
    )(h, w_gate, w_gate, w_gate, ya, yb, yc, wa, wb, wc)


def _residual_matmul_kernel(a_ref, w_ref, r_ref, o_ref, acc_ref):
    kk = pl.program_id(2)

    @pl.when(kk == 0)
    def _():
        acc_ref[...] = r_ref[...]

    acc_ref[...] += jnp.dot(a_ref[...], w_ref[...], preferred_element_type=F32)

    @pl.when(kk == pl.num_programs(2) - 1)
    def _():
        o_ref[...] = acc_ref[...]


def _residual_matmul(a, w, res, tm=1024, tn=1024, tk=2048):
    m, k = a.shape
    n = w.shape[1]
    blocks = [((tm, tk), BF16), ((tk, tn), BF16), ((tm, tn), F32), ((tm, tn), F32)]
    scratch = [((tm, tn), F32)]
    return pl.pallas_call(
        _residual_matmul_kernel,
        grid=(m // tm, n // tn, k // tk),
        in_specs=[pl.BlockSpec((tm, tk), lambda i, j, kk: (i, kk)),
                  pl.BlockSpec((tk, tn), lambda i, j, kk: (kk, j)),
                  pl.BlockSpec((tm, tn), lambda i, j, kk: (i, j))],
        out_specs=pl.BlockSpec((tm, tn), lambda i, j, kk: (i, j)),
        out_shape=jax.ShapeDtypeStruct((m, n), F32),
        scratch_shapes=[pltpu.VMEM(s, d) for s, d in scratch],
        compiler_params=_params(("arbitrary", "arbitrary", "arbitrary"), blocks, scratch),
        name="residual_matmul",
    )(a, w, res)


def _relu2_matmul_kernel(a_ref, w_ref, o_ref):
    up = jnp.maximum(jnp.dot(a_ref[...], w_ref[...], preferred_element_type=F32), 0.0)
    o_ref[...] = (up * up).astype(o_ref.dtype)


def _relu2_matmul(a, w, tm=1024, tn=1024):
    m, k = a.shape
    n = w.shape[1]
    blocks = [((tm, k), BF16), ((k, tn), BF16), ((tm, tn), BF16)]
    return pl.pallas_call(
        _relu2_matmul_kernel,
        grid=(m // tm, n // tn),
        in_specs=[pl.BlockSpec((tm, k), lambda i, j: (i, 0)),
                  pl.BlockSpec((k, tn), lambda i, j: (0, j))],
        out_specs=pl.BlockSpec((tm, tn), lambda i, j: (i, j)),
        out_shape=jax.ShapeDtypeStruct((m, n), BF16),
        compiler_params=_params(("arbitrary", "arbitrary"), blocks),
        name="mlp_up",
    )(a, w)


def _ple_kernel(h_ref, wg_ref, p_ref, wp_ref, r_ref, o_ref):
    gate = jax.nn.sigmoid(jnp.dot(h_ref[...], wg_ref[...], preferred_element_type=F32))
    emb = jnp.dot(p_ref[...].astype(BF16), wp_ref[...], preferred_element_type=F32)
    o_ref[...] = r_ref[...] + gate * emb


def _ple(h, w_gate, p, w_proj, res, tm=1024, tn=512):
    m, d = h.shape
    pd = p.shape[1]
    blocks = [((tm, d), BF16), ((d, tn), BF16), ((tm, pd), F32), ((pd, tn), BF16),
              ((tm, tn), F32), ((tm, tn), F32)]
    return pl.pallas_call(
        _ple_kernel,
        grid=(m // tm, d // tn),
        in_specs=[pl.BlockSpec((tm, d), lambda i, j: (i, 0)),
                  pl.BlockSpec((d, tn), lambda i, j: (0, j)),
                  pl.BlockSpec((tm, pd), lambda i, j: (i, 0)),
                  pl.BlockSpec((pd, tn), lambda i, j: (0, j)),
                  pl.BlockSpec((tm, tn), lambda i, j: (i, j))],
        out_specs=pl.BlockSpec((tm, tn), lambda i, j: (i, j)),
        out_shape=jax.ShapeDtypeStruct((m, d), F32),
        compiler_params=_params(("arbitrary", "arbitrary"), blocks),
        name="ple",
    )(h, w_gate, p, w_proj, res)


def kernel(x, p, norm_mix_g, w_in, mlstm_gate_b, mlstm_norm_g, gmlp_norm_g, gmlp_norm_b, gmlp_ws, gmlp_bs,
           w_branch_a, w_branch_b, w_branch_c, w_out, norm_mlp_g, w_mlp_up, w_mlp_down, norm_ple_g,
           w_ple_gate, w_ple_proj, final_norm_g):
    nbatch, seq, d = x.shape
    depth = w_in.shape[0]
    m = nbatch * seq
    assert d == MLSTM_WIDTH + MOBA_WIDTH + GMLP_WIDTH

    qkvo_a = 4 * MLSTM_WIDTH
    gates_if = 2 * MLSTM_HEADS
    main_b = qkvo_a + gates_if
    main_cols = 3 * MOBA_WIDTH + 2 * GMLP_WIDTH
    gate_off = main_b + main_cols
    assert w_in.shape[2] == gate_off + N_BRANCHES * d

    moba_q_cb = qkvo_a // HEAD_DIM
    moba_k_cb = moba_q_cb + MOBA_HEADS
    moba_v_cb = moba_k_cb + MOBA_HEADS
    gmlp_u_cb = (qkvo_a + 3 * MOBA_WIDTH) // GMLP_WIDTH
    gmlp_v_cb = gmlp_u_cb + 1

    z_cols = qkvo_a + main_cols
    colscale = jnp.ones((1, z_cols), F32).at[:, MLSTM_WIDTH:2 * MLSTM_WIDTH].set(HEAD_DIM ** -0.5)
    cos_tab, sin_tab = _rope_tables(seq)

    xf = x.reshape(m, d)
    for i in range(depth):
        w_i = w_in[i]
        w_main = jnp.concatenate([w_i[:, :qkvo_a], w_i[:, main_b:gate_off]], axis=1).astype(BF16)
        w_if = jnp.pad(w_i[:, qkvo_a:main_b], ((0, 0), (0, GATE_LANES - gates_if))).astype(BF16)
        w_gate = w_i[:, gate_off:].astype(BF16)

        h = _rmsnorm(xf, norm_mix_g[i], BF16)
        z, zif = _inproj(h, w_main, w_if, colscale)
        z3 = z.reshape(nbatch, seq, z_cols)
        zif3 = zif.reshape(nbatch, seq, GATE_LANES)

        ya = _mlstm(z3, zif3, mlstm_gate_b[i], mlstm_norm_g[i])
        yb = _moba(z3, cos_tab, sin_tab, moba_q_cb, moba_k_cb, moba_v_cb)
        yc = _gmlp(z3, gmlp_u_cb, gmlp_v_cb, gmlp_norm_g[i], gmlp_norm_b[i], gmlp_ws[i], gmlp_bs[i])

        merged = _merge(h, w_gate, ya.reshape(m, -1), yb.reshape(m, -1), yc.reshape(m, -1),
                        w_branch_a[i].astype(BF16), w_branch_b[i].astype(BF16), w_branch_c[i].astype(BF16))
        xf = _residual_matmul(merged, w_out[i].astype(BF16), xf)

        h2 = _rmsnorm(xf, norm_mlp_g[i], BF16)
        hidden = _relu2_matmul(h2, w_mlp_up[i].astype(BF16))
        xf = _residual_matmul(hidden, w_mlp_down[i].astype(BF16), xf)

        h3 = _rmsnorm(xf, norm_ple_g[i], BF16)
        xf = _ple(h3, w_ple_gate[i].astype(BF16), p[i].reshape(m, -1), w_ple_proj[i].astype(BF16), xf)

    return _rmsnorm(xf, final_norm_g, F32).reshape(nbatch, seq, d)
```

```python
import functools

import jax
import jax.numpy as jnp
import numpy as np
from jax import lax
from jax.experimental import pallas as pl
from jax.experimental.pallas import tpu as pltpu

F32 = jnp.float32
BF16 = jnp.bfloat16

HEAD_DIM = 128
MLSTM_HEADS = 4
MLSTM_WIDTH = MLSTM_HEADS * HEAD_DIM
MOBA_HEADS = 8
MOBA_WIDTH = MOBA_HEADS * HEAD_DIM
MOBA_BLOCK = 256
MOBA_TOPK = 3
ROPE_THETA = 500000.0
ROPE_DIM = HEAD_DIM // 4
GMLP_WIDTH = 512
GMLP_GROUPS = 4
GMLP_CHUNK = 128
N_BRANCHES = 3
NORM_EPS = 1e-6

LANES = 128
V7X_VMEM_BYTES = 64 * 1024 * 1024
VMEM_CEILING = V7X_VMEM_BYTES - 8 * 1024 * 1024

MLSTM_KERNEL_CHUNK = 128
GATE_LANES = LANES
MASK_BIG = 2.0 ** 100
MOBA_GROUP = 4

NT_DIMS = (((1,), (1,)), ((), ()))
TN_DIMS = (((0,), (0,)), ((), ()))


def _nbytes(shape, dtype):
    return int(np.prod(shape)) * jnp.dtype(dtype).itemsize


def _params(semantics, blocks, scratch=()):
    need = 2 * sum(_nbytes(s, d) for s, d in blocks) + sum(_nbytes(s, d) for s, d in scratch)
    limit = min(VMEM_CEILING, need + need // 4 + 4 * 1024 * 1024)
    return pltpu.CompilerParams(dimension_semantics=semantics, vmem_limit_bytes=limit)


def _rmsnorm_kernel(x_ref, g_ref, o_ref):
    x = x_ref[...]
    y = x * lax.rsqrt(jnp.mean(x * x, axis=-1, keepdims=True) + NORM_EPS)
    o_ref[...] = (y * g_ref[...]).astype(o_ref.dtype)


def _rmsnorm(x, g, out_dtype, tm=512):
    m, d = x.shape
    return pl.pallas_call(
        _rmsnorm_kernel,
        grid=(m // tm,),
        in_specs=[pl.BlockSpec((tm, d), lambda i: (i, 0)),
                  pl.BlockSpec((1, d), lambda i: (0, 0))],
        out_specs=pl.BlockSpec((tm, d), lambda i: (i, 0)),
        out_shape=jax.ShapeDtypeStruct((m, d), out_dtype),
        compiler_params=_params(("arbitrary",), [((tm, d), F32), ((tm, d), out_dtype)]),
        name="rmsnorm",
    )(x, g.reshape(1, d))


def _inproj_kernel(h_ref, w_ref, wif_ref, cs_ref, z_ref, zif_ref):
    acc = jnp.dot(h_ref[...], w_ref[...], preferred_element_type=F32)
    z_ref[...] = acc * cs_ref[...]

    @pl.when(pl.program_id(1) == 0)
    def _():
        zif_ref[...] = jnp.dot(h_ref[...], wif_ref[...], preferred_element_type=F32)


def _inproj(h, w_main, w_if, colscale, tm=1024, tn=512):
    m, k = h.shape
    n = w_main.shape[1]
    blocks = [((tm, k), BF16), ((k, tn), BF16), ((k, GATE_LANES), BF16), ((1, tn), F32),
              ((tm, tn), F32), ((tm, GATE_LANES), F32)]
    return pl.pallas_call(
        _inproj_kernel,
        grid=(m // tm, n // tn),
        in_specs=[pl.BlockSpec((tm, k), lambda i, j: (i, 0)),
                  pl.BlockSpec((k, tn), lambda i, j: (0, j)),
                  pl.BlockSpec((k, GATE_LANES), lambda i, j: (0, 0)),
                  pl.BlockSpec((1, tn), lambda i, j: (0, j))],
        out_specs=[pl.BlockSpec((tm, tn), lambda i, j: (i, j)),
                   pl.BlockSpec((tm, GATE_LANES), lambda i, j: (i, 0))],
        out_shape=[jax.ShapeDtypeStruct((m, n), F32),
                   jax.ShapeDtypeStruct((m, GATE_LANES), F32)],
        compiler_params=_params(("arbitrary", "arbitrary"), blocks),
        name="in_proj",
    )(h, w_main, w_if, colscale)


def _log_sigmoid(x):
    return jnp.minimum(x, 0.0) - jnp.log1p(jnp.exp(-jnp.abs(x)))


def _mlstm_kernel(q_ref, k_ref, v_ref, o_ref, zif_ref, gb_ref, ng_ref, y_ref, c_sc, n_sc, m_sc):
    nbatch, chunk, _ = q_ref.shape
    heads = MLSTM_HEADS

    @pl.when(pl.program_id(0) == 0)
    def _():
        c_sc[...] = jnp.zeros_like(c_sc)
        n_sc[...] = jnp.zeros_like(n_sc)
        m_sc[...] = jnp.zeros_like(m_sc)

    row = lax.broadcasted_iota(jnp.int32, (chunk, chunk), 0)
    col = lax.broadcasted_iota(jnp.int32, (chunk, chunk), 1)
    causal = col <= row
    tril = jnp.where(causal, 1.0, 0.0).astype(F32)
    lane = lax.broadcasted_iota(jnp.int32, (chunk, GATE_LANES), 1)

    for b in range(nbatch):
        pre = zif_ref[b] + gb_ref[...]
        gates = jnp.where(lane < heads, pre, _log_sigmoid(pre))
        gcum = jnp.dot(tril, gates, preferred_element_type=F32,
                       precision=lax.Precision.HIGHEST)
        gates_t = gates.T
        gcum_t = gcum.T
        for h in range(heads):
            s = b * heads + h
            sl = slice(h * HEAD_DIM, (h + 1) * HEAD_DIM)
            q = q_ref[b, :, sl]
            k = k_ref[b, :, sl]
            v = v_ref[b, :, sl]
            qb = q.astype(BF16)
            kb = k.astype(BF16)
            vb = v.astype(BF16)
            g_t = gcum[:, heads + h:heads + h + 1]
            g_s = gcum_t[heads + h:heads + h + 1, :]
            i_s = gates_t[h:h + 1, :]
            i_t = gates[:, h:h + 1]
            m_prev = m_sc[s][:, 0:1]
            c_prev = c_sc[s]
            n_prev = n_sc[s]

            log_w = jnp.where(causal, g_t - g_s + i_s, -jnp.inf)
            log_a = g_t + m_prev
            m_row = jnp.maximum(jnp.max(log_w, axis=1, keepdims=True), log_a)
            qk = lax.dot_general(qb, kb, NT_DIMS, preferred_element_type=F32) * jnp.exp(log_w - m_row)
            a = jnp.exp(log_a - m_row)
            num = (jnp.dot(qk.astype(BF16), vb, preferred_element_type=F32)
                   + a * lax.dot_general(qb, c_prev.astype(BF16), NT_DIMS, preferred_element_type=F32))
            den = (jnp.sum(qk, axis=1, keepdims=True)
                   + a * jnp.sum(q * n_prev, axis=1, keepdims=True))
            h_out = num / jnp.maximum(jnp.abs(den), jnp.exp(-m_row))

            g_last = gcum[chunk - 1:chunk, heads + h:heads + h + 1]
            log_u = g_last - g_t + i_t
            m_new = jnp.maximum(g_last + m_prev, jnp.max(log_u, axis=0, keepdims=True))
            decay = jnp.exp(g_last + m_prev - m_new)
            u = jnp.exp(log_u - m_new)
            c_sc[s] = decay * c_prev + lax.dot_general((u * v).astype(BF16), kb, TN_DIMS,
                                                       preferred_element_type=F32)
            n_sc[s] = decay * n_prev + jnp.sum(u * k, axis=0, keepdims=True)
            m_sc[s] = jnp.broadcast_to(m_new, (1, LANES))

            yn = h_out * lax.rsqrt(jnp.mean(h_out * h_out, axis=-1, keepdims=True) + NORM_EPS)
            y_ref[b, :, sl] = (jax.nn.sigmoid(o_ref[b, :, sl]) * (yn * ng_ref[:, sl])).astype(y_ref.dtype)


def _mlstm(z3, zif3, gate_b, norm_g):
    nbatch, seq, _ = z3.shape
    chunk = MLSTM_KERNEL_CHUNK
    w = MLSTM_WIDTH
    streams = nbatch * MLSTM_HEADS
    gb = jnp.pad(gate_b, (0, GATE_LANES - gate_b.shape[0])).reshape(1, GATE_LANES)
    blocks = [((nbatch, chunk, w), F32)] * 4 + [((nbatch, chunk, GATE_LANES), F32),
                                               ((nbatch, chunk, w), BF16)]
    scratch = [((streams, HEAD_DIM, HEAD_DIM), F32), ((streams, 1, LANES), F32), ((streams, 1, LANES), F32)]

    def zcol(cb):
        return pl.BlockSpec((nbatch, chunk, w), lambda c: (0, c, cb))

    return pl.pallas_call(
        _mlstm_kernel,
        grid=(seq // chunk,),
        in_specs=[zcol(0), zcol(1), zcol(2), zcol(3),
                  pl.BlockSpec((nbatch, chunk, GATE_LANES), lambda c: (0, c, 0)),
                  pl.BlockSpec((1, GATE_LANES), lambda c: (0, 0)),
                  pl.BlockSpec((1, w), lambda c: (0, 0))],
        out_specs=pl.BlockSpec((nbatch, chunk, w), lambda c: (0, c, 0)),
        out_shape=jax.ShapeDtypeStruct((nbatch, seq, w), BF16),
        scratch_shapes=[pltpu.VMEM(s, d) for s, d in scratch],
        compiler_params=_params(("arbitrary",), blocks, scratch),
        name="mlstm",
    )(z3, z3, z3, z3, zif3, gb, norm_g.reshape(1, w))


def _rope_tables(seq):
    half = ROPE_DIM // 2
    inv_freq = ROPE_THETA ** (-jnp.arange(0, ROPE_DIM, 2, dtype=F32) / ROPE_DIM)
    ang = jnp.arange(seq, dtype=F32)[:, None] * inv_freq[None, :]
    cos = jnp.cos(ang)
    sin = jnp.sin(ang)
    ones = jnp.ones((seq, HEAD_DIM - ROPE_DIM), F32)
    cos_tab = jnp.concatenate([cos, cos, ones], axis=1)
    sin_tab = jnp.concatenate([-sin, sin, 0.0 * ones], axis=1)
    assert cos_tab.shape == (seq, HEAD_DIM) and half * 2 == ROPE_DIM
    return cos_tab, sin_tab


def _rotary(t, cos, sin):
    half = ROPE_DIM // 2
    lane = lax.broadcasted_iota(jnp.int32, t.shape, 1)
    upper = pltpu.roll(t, HEAD_DIM - half, axis=1)
    lower = pltpu.roll(t, half, axis=1)
    partner = jnp.where(lane < half, upper, lower)
    return jnp.where(lane < ROPE_DIM, t * cos + partner * sin, t)


def _moba_prep_kernel(q_ref, k_ref, v_ref, cos_ref, sin_ref, qa_ref, kb_ref, vb_ref, km_sc):
    blk = q_ref.shape[1]
    d = HEAD_DIM
    j = pl.program_id(1)

    @pl.when(j == 0)
    def _():
        km_sc[...] = jnp.zeros_like(km_sc)

    cos = cos_ref[...]
    sin = sin_ref[...]
    lane = lax.broadcasted_iota(jnp.int32, (blk, LANES), 1)
    lane_f = lane.astype(F32)
    mean_row = lax.broadcasted_iota(jnp.int32, (LANES, d), 0)
    vb_ref[0] = v_ref[0].astype(BF16)
    for h in range(MOBA_HEADS):
        sl = slice(h * d, (h + 1) * d)
        qf = _rotary(q_ref[0, :, sl], cos, sin)
        kk = _rotary(k_ref[0, :, sl], cos, sin)
        kb_ref[0, :, sl] = kk.astype(BF16)

        gate = lax.dot_general(qf, km_sc[h], NT_DIMS, preferred_element_type=F32,
                               precision=lax.Precision.HIGHEST)
        gate = jnp.where(lane < j, gate, -jnp.inf)
        sel_m1 = jnp.full((blk, LANES), -1.0, F32)
        for _ in range(MOBA_TOPK):
            mx = jnp.max(gate, axis=1, keepdims=True)
            first = jnp.min(jnp.where(gate == mx, lane_f, float(LANES)), axis=1, keepdims=True)
            first = jnp.where(mx > -jnp.inf, first, -1.0)
            hit = lane_f == first
            sel_m1 = jnp.where(hit, 0.0, sel_m1)
            gate = jnp.where(hit, -jnp.inf, gate)
        qa_ref[0, :, 2 * h * d:(2 * h + 1) * d] = qf.astype(BF16)
        qa_ref[0, :, (2 * h + 1) * d:(2 * h + 2) * d] = sel_m1.astype(BF16)

        km_sc[h] = jnp.where(mean_row == j, jnp.mean(kk, axis=0, keepdims=True), km_sc[h])


def _moba_prep(z3, cos_tab, sin_tab, q_cb, k_cb, v_cb):
    nbatch, seq, _ = z3.shape
    blk = MOBA_BLOCK
    w = MOBA_WIDTH
    d = HEAD_DIM
    blocks = [((1, blk, w), F32)] * 3 + [((blk, d), F32)] * 2 + [((1, blk, 2 * w), BF16), ((1, blk, w), BF16),
                                                                  ((1, blk, w), BF16)]
    scratch = [((MOBA_HEADS, LANES, d), F32)]
    return pl.pallas_call(
        _moba_prep_kernel,
        grid=(nbatch, seq // blk),
        in_specs=[pl.BlockSpec((1, blk, w), lambda b, j: (b, j, q_cb)),
                  pl.BlockSpec((1, blk, w), lambda b, j: (b, j, k_cb)),
                  pl.BlockSpec((1, blk, w), lambda b, j: (b, j, v_cb)),
                  pl.BlockSpec((blk, d), lambda b, j: (j, 0)),
                  pl.BlockSpec((blk, d), lambda b, j: (j, 0))],
        out_specs=[pl.BlockSpec((1, blk, 2 * w), lambda b, j: (b, j, 0)),
                   pl.BlockSpec((1, blk, w), lambda b, j: (b, j, 0)),
                   pl.BlockSpec((1, blk, w), lambda b, j: (b, j, 0))],
        out_shape=[jax.ShapeDtypeStruct((nbatch, seq, 2 * w), BF16),
                   jax.ShapeDtypeStruct((nbatch, seq, w), BF16),
                   jax.ShapeDtypeStruct((nbatch, seq, w), BF16)],
        scratch_shapes=[pltpu.VMEM(s, dt) for s, dt in scratch],
        compiler_params=_params(("arbitrary", "arbitrary"), blocks, scratch),
        name="moba_prep",
    )(z3, z3, z3, cos_tab, sin_tab)


def _moba_attn_kernel(q_ref, k_ref, v_ref, o_ref):
    blk = q_ref.shape[1]
    d = HEAD_DIM
    nblk = k_ref.shape[1] // blk
    group = MOBA_GROUP
    j = pl.program_id(2)
    scale = d ** -0.5
    q_aug = q_ref[0]
    lane = lax.broadcasted_iota(jnp.int32, (blk, LANES), 1)

    def softmax_partial(s, v_rows):
        m = jnp.max(s, axis=1, keepdims=True)
        p = jnp.exp(s - m)
        return m, jnp.sum(p, axis=1, keepdims=True), jnp.dot(p.astype(BF16), v_rows, preferred_element_type=F32)

    def past_partial(n):
        c0 = pl.multiple_of(jnp.minimum(n, nblk - 1) * blk, blk)
        onehot = jnp.where(lane == n, MASK_BIG, 0.0).astype(BF16)
        k_aug = jnp.concatenate([k_ref[0, pl.ds(c0, blk), :], onehot], axis=1)
        s = lax.dot_general(q_aug, k_aug, NT_DIMS, preferred_element_type=F32) * scale
        return softmax_partial(s, v_ref[0, pl.ds(c0, blk), :])

    def merge(state, parts):
        m_run, l_run, acc_run = state
        m_new = m_run
        for m, _, _ in parts:
            m_new = jnp.maximum(m_new, m)
        w = jnp.exp(m_run - m_new)
        l_new = w * l_run
        acc = w * acc_run
        for m, l, a in parts:
            w = jnp.exp(m - m_new)
            l_new = l_new + w * l
            acc = acc + w * a
        return m_new, l_new, acc

    r0 = pl.multiple_of(j * blk, blk)
    row = lax.broadcasted_iota(jnp.int32, (blk, blk), 0)
    col = lax.broadcasted_iota(jnp.int32, (blk, blk), 1)
    s_own = lax.dot_general(q_aug[:, :d], k_ref[0, pl.ds(r0, blk), :], NT_DIMS,
                            preferred_element_type=F32) * scale
    s_own = jnp.where(col <= row, s_own, -jnp.inf)
    state = softmax_partial(s_own, v_ref[0, pl.ds(r0, blk), :])

    state = merge(state, [past_partial(n) for n in range(group)])
    ngroups = lax.div(j + (group - 1), group)

    def group_body(g, st):
        return merge(st, [past_partial(g * group + i) for i in range(group)])

    _, l_fin, acc = lax.fori_loop(1, ngroups, group_body, state)
    o_ref[0] = (acc / l_fin).astype(o_ref.dtype)


def _moba_attn(q_aug, kb, vb):
    nbatch, seq, w = kb.shape
    blk = MOBA_BLOCK
    d = HEAD_DIM
    blocks = [((1, blk, 2 * d), BF16), ((1, seq, d), BF16), ((1, seq, d), BF16), ((1, blk, d), BF16)]
    return pl.pallas_call(
        _moba_attn_kernel,
        grid=(nbatch, MOBA_HEADS, seq // blk),
        in_specs=[pl.BlockSpec((1, blk, 2 * d), lambda b, h, j: (b, j, h)),
                  pl.BlockSpec((1, seq, d), lambda b, h, j: (b, 0, h)),
                  pl.BlockSpec((1, seq, d), lambda b, h, j: (b, 0, h))],
        out_specs=pl.BlockSpec((1, blk, d), lambda b, h, j: (b, j, h)),
        out_shape=jax.ShapeDtypeStruct((nbatch, seq, w), BF16),
        compiler_params=_params(("arbitrary", "arbitrary", "arbitrary"), blocks),
        name="moba_attn",
    )(q_aug, kb, vb)


def _gelu_tanh(x):
    c = np.sqrt(2.0 / np.pi).astype(np.float32)
    return x * (0.5 * (1.0 + jnp.tanh(c * (x + 0.044715 * (x * x * x)))))


def _gmlp_kernel(u_ref, v_ref, lg_ref, lb_ref, ws_ref, bst_ref, y_ref):
    rows = u_ref.shape[1]
    t = GMLP_CHUNK
    gd = GMLP_WIDTH // GMLP_GROUPS
    v = _gelu_tanh(v_ref[0])
    mu = jnp.mean(v, axis=-1, keepdims=True)
    vc = v - mu
    vln = vc * lax.rsqrt(jnp.mean(vc * vc, axis=-1, keepdims=True) + NORM_EPS) * lg_ref[...] + lb_ref[...]
    vb = vln.astype(BF16)
    row = lax.broadcasted_iota(jnp.int32, (t, t), 0)
    col = lax.broadcasted_iota(jnp.int32, (t, t), 1)
    for g in range(GMLP_GROUPS):
        wg = jnp.where(col <= row, ws_ref[g], 0.0).astype(BF16)
        bias = bst_ref[:, g:g + 1]
        cols = slice(g * gd, (g + 1) * gd)
        for c in range(rows // t):
            rs = slice(c * t, (c + 1) * t)
            mixed = jnp.dot(wg, vb[rs, cols], preferred_element_type=F32) + bias
            y_ref[0, rs, cols] = (_gelu_tanh(u_ref[0, rs, cols]) * mixed).astype(y_ref.dtype)


def _gmlp(z3, u_cb, v_cb, ln_g, ln_b, ws, bs, rows=512):
    nbatch, seq, _ = z3.shape
    w = GMLP_WIDTH
    t = GMLP_CHUNK
    blocks = [((1, rows, w), F32), ((1, rows, w), F32), ((GMLP_GROUPS, t, t), F32), ((1, rows, w), BF16)]
    return pl.pallas_call(
        _gmlp_kernel,
        grid=(nbatch, seq // rows),
        in_specs=[pl.BlockSpec((1, rows, w), lambda b, c: (b, c, u_cb)),
                  pl.BlockSpec((1, rows, w), lambda b, c: (b, c, v_cb)),
                  pl.BlockSpec((1, w), lambda b, c: (0, 0)),
                  pl.BlockSpec((1, w), lambda b, c: (0, 0)),
                  pl.BlockSpec((GMLP_GROUPS, t, t), lambda b, c: (0, 0, 0)),
                  pl.BlockSpec((t, GMLP_GROUPS), lambda b, c: (0, 0))],
        out_specs=pl.BlockSpec((1, rows, w), lambda b, c: (b, c, 0)),
        out_shape=jax.ShapeDtypeStruct((nbatch, seq, w), BF16),
        compiler_params=_params(("arbitrary", "arbitrary"), blocks),
        name="gmlp",
    )(z3, z3, ln_g.reshape(1, w), ln_b.reshape(1, w), ws, bs.T)


def _merge_kernel(h_ref, wga_ref, wgb_ref, wgc_ref, ya_ref, yb_ref, yc_ref, wa_ref, wb_ref, wc_ref, o_ref):
    h = h_ref[...]

    def branch(wg_ref, y_ref, w_ref):
        gate = jax.nn.sigmoid(jnp.dot(h, wg_ref[...], preferred_element_type=F32))
        return gate * jnp.dot(y_ref[...], w_ref[...], preferred_element_type=F32)

    merged = branch(wga_ref, ya_ref, wa_ref) + branch(wgb_ref, yb_ref, wb_ref) + branch(wgc_ref, yc_ref, wc_ref)
    o_ref[...] = merged.astype(o_ref.dtype)


def _merge(h, w_gate, ya, yb, yc, wa, wb, wc, tm=1024, tn=512):
    m, d = h.shape
    nblk = d // tn
    ka, kb, kc = ya.shape[1], yb.shape[1], yc.shape[1]
    blocks = ([((tm, d), BF16)] + [((d, tn), BF16)] * 3
              + [((tm, ka), BF16), ((tm, kb), BF16), ((tm, kc), BF16)]
              + [((ka, tn), BF16), ((kb, tn), BF16), ((kc, tn), BF16), ((tm, tn), BF16)])

    def gate_spec(branch):
        return pl.BlockSpec((d, tn), lambda i, j: (0, branch * nblk + j))

    return pl.pallas_call(
        _merge_kernel,
        grid=(m // tm, nblk),
        in_specs=[pl.BlockSpec((tm, d), lambda i, j: (i, 0)),
                  gate_spec(0), gate_spec(1), gate_spec(2),
                  pl.BlockSpec((tm, ka), lambda i, j: (i, 0)),
                  pl.BlockSpec((tm, kb), lambda i, j: (i, 0)),
                  pl.BlockSpec((tm, kc), lambda i, j: (i, 0)),
                  pl.BlockSpec((ka, tn), lambda i, j: (0, j)),
                  pl.BlockSpec((kb, tn), lambda i, j: (0, j)),
                  pl.BlockSpec((kc, tn), lambda i, j: (0, j))],
        out_specs=pl.BlockSpec((tm, tn), lambda i, j: (i, j)),
        out_shape=jax.ShapeDtypeStruct((m, d), BF16),
        compiler_params=_params(("arbitrary", "arbitrary"), blocks),
        name="merge",
    )(h, w_gate, w_gate, w_gate, ya, yb, yc, wa, wb, wc)


def _residual_matmul_kernel(a_ref, w_ref, r_ref, o_ref, acc_ref):
    kk = pl.program_id(2)

    @pl.when(kk == 0)
    def _():
        acc_ref[...] = r_ref[...]

    acc_ref[...] += jnp.dot(a_ref[...], w_ref[...], preferred_element_type=F32)

    @pl.when(kk == pl.num_programs(2) - 1)
    def _():
        o_ref[...] = acc_ref[...]


def _residual_matmul(a, w, res, tm=1024, tn=1024, tk=2048):
    m, k = a.shape
    n = w.shape[1]
    blocks = [((tm, tk), BF16), ((tk, tn), BF16), ((tm, tn), F32), ((tm, tn), F32)]
    scratch = [((tm, tn), F32)]
    return pl.pallas_call(
        _residual_matmul_kernel,
        grid=(m // tm, n // tn, k // tk),
        in_specs=[pl.BlockSpec((tm, tk), lambda i, j, kk: (i, kk)),
                  pl.BlockSpec((tk, tn), lambda i, j, kk: (kk, j)),
                  pl.BlockSpec((tm, tn), lambda i, j, kk: (i, j))],
        out_specs=pl.BlockSpec((tm, tn), lambda i, j, kk: (i, j)),
        out_shape=jax.ShapeDtypeStruct((m, n), F32),
        scratch_shapes=[pltpu.VMEM(s, d) for s, d in scratch],
        compiler_params=_params(("arbitrary", "arbitrary", "arbitrary"), blocks, scratch),
        name="residual_matmul",
    )(a, w, res)


def _relu2_matmul_kernel(a_ref, w_ref, o_ref):
    up = jnp.maximum(jnp.dot(a_ref[...], w_ref[...], preferred_element_type=F32), 0.0)
    o_ref[...] = (up * up).astype(o_ref.dtype)


def _relu2_matmul(a, w, tm=1024, tn=1024):
    m, k = a.shape
    n = w.shape[1]
    blocks = [((tm, k), BF16), ((k, tn), BF16), ((tm, tn), BF16)]
    return pl.pallas_call(
        _relu2_matmul_kernel,
        grid=(m // tm, n // tn),
        in_specs=[pl.BlockSpec((tm, k), lambda i, j: (i, 0)),
                  pl.BlockSpec((k, tn), lambda i, j: (0, j))],
        out_specs=pl.BlockSpec((tm, tn), lambda i, j: (i, j)),
        out_shape=jax.ShapeDtypeStruct((m, n), BF16),
        compiler_params=_params(("arbitrary", "arbitrary"), blocks),
        name="mlp_up",
    )(a, w)


def _ple_kernel(h_ref, wg_ref, p_ref, wp_ref, r_ref, o_ref):
    gate = jax.nn.sigmoid(jnp.dot(h_ref[...], wg_ref[...], preferred_element_type=F32))
    emb = jnp.dot(p_ref[...].astype(BF16), wp_ref[...], preferred_element_type=F32)
    o_ref[...] = r_ref[...] + gate * emb


def _ple(h, w_gate, p, w_proj, res, tm=1024, tn=512):
    m, d = h.shape
    pd = p.shape[1]
    blocks = [((tm, d), BF16), ((d, tn), BF16), ((tm, pd), F32), ((pd, tn), BF16),
              ((tm, tn), F32), ((tm, tn), F32)]
    return pl.pallas_call(
        _ple_kernel,
        grid=(m // tm, d // tn),
        in_specs=[pl.BlockSpec((tm, d), lambda i, j: (i, 0)),
                  pl.BlockSpec((d, tn), lambda i, j: (0, j)),
                  pl.BlockSpec((tm, pd), lambda i, j: (i, 0)),
                  pl.BlockSpec((pd, tn), lambda i, j: (0, j)),
                  pl.BlockSpec((tm, tn), lambda i, j: (i, j))],
        out_specs=pl.BlockSpec((tm, tn), lambda i, j: (i, j)),
        out_shape=jax.ShapeDtypeStruct((m, d), F32),
        compiler_params=_params(("arbitrary", "arbitrary"), blocks),
        name="ple",
    )(h, w_gate, p, w_proj, res)


def kernel(x, p, norm_mix_g, w_in, mlstm_gate_b, mlstm_norm_g, gmlp_norm_g, gmlp_norm_b, gmlp_ws, gmlp_bs,
           w_branch_a, w_branch_b, w_branch_c, w_out, norm_mlp_g, w_mlp_up, w_mlp_down, norm_ple_g,
           w_ple_gate, w_ple_proj, final_norm_g):
    nbatch, seq, d = x.shape
    depth = w_in.shape[0]
    m = nbatch * seq
    assert d == MLSTM_WIDTH + MOBA_WIDTH + GMLP_WIDTH

    qkvo_a = 4 * MLSTM_WIDTH
    gates_if = 2 * MLSTM_HEADS
    main_b = qkvo_a + gates_if
    main_cols = 3 * MOBA_WIDTH + 2 * GMLP_WIDTH
    gate_off = main_b + main_cols
    assert w_in.shape[2] == gate_off + N_BRANCHES * d

    moba_q_cb = qkvo_a // MOBA_WIDTH
    moba_k_cb = moba_q_cb + 1
    moba_v_cb = moba_k_cb + 1
    gmlp_u_cb = (qkvo_a + 3 * MOBA_WIDTH) // GMLP_WIDTH
    gmlp_v_cb = gmlp_u_cb + 1

    z_cols = qkvo_a + main_cols
    colscale = jnp.ones((1, z_cols), F32).at[:, MLSTM_WIDTH:2 * MLSTM_WIDTH].set(HEAD_DIM ** -0.5)
    cos_tab, sin_tab = _rope_tables(seq)

    xf = x.reshape(m, d)
    for i in range(depth):
        w_i = w_in[i]
        w_main = jnp.concatenate([w_i[:, :qkvo_a], w_i[:, main_b:gate_off]], axis=1).astype(BF16)
        w_if = jnp.pad(w_i[:, qkvo_a:main_b], ((0, 0), (0, GATE_LANES - gates_if))).astype(BF16)
        w_gate = w_i[:, gate_off:].astype(BF16)

        h = _rmsnorm(xf, norm_mix_g[i], BF16)
        z, zif = _inproj(h, w_main, w_if, colscale)
        z3 = z.reshape(nbatch, seq, z_cols)
        zif3 = zif.reshape(nbatch, seq, GATE_LANES)

        ya = _mlstm(z3, zif3, mlstm_gate_b[i], mlstm_norm_g[i])
        yb = _moba_attn(*_moba_prep(z3, cos_tab, sin_tab, moba_q_cb, moba_k_cb, moba_v_cb))
        yc = _gmlp(z3, gmlp_u_cb, gmlp_v_cb, gmlp_norm_g[i], gmlp_norm_b[i], gmlp_ws[i], gmlp_bs[i])

        merged = _merge(h, w_gate, ya.reshape(m, -1), yb.reshape(m, -1), yc.reshape(m, -1),
                        w_branch_a[i].astype(BF16), w_branch_b[i].astype(BF16), w_branch_c[i].astype(BF16))
        xf = _residual_matmul(merged, w_out[i].astype(BF16), xf)

        h2 = _rmsnorm(xf, norm_mlp_g[i], BF16)
        hidden = _relu2_matmul(h2, w_mlp_up[i].astype(BF16))
        xf = _residual_matmul(hidden, w_mlp_down[i].astype(BF16), xf)

        h3 = _rmsnorm(xf, norm_ple_g[i], BF16)
        xf = _ple(h3, w_ple_gate[i].astype(BF16), p[i].reshape(m, -1), w_ple_proj[i].astype(BF16), xf)

    return _rmsnorm(xf, final_norm_g, F32).reshape(nbatch, seq, d)
```

```python
import functools

import jax
import jax.numpy as jnp
import numpy as np
from jax import lax
from jax.experimental import pallas as pl
from jax.experimental.pallas import tpu as pltpu

F32 = jnp.float32
BF16 = jnp.bfloat16

HEAD_DIM = 128
MLSTM_HEADS = 4
MLSTM_WIDTH = MLSTM_HEADS * HEAD_DIM
MOBA_HEADS = 8
MOBA_WIDTH = MOBA_HEADS * HEAD_DIM
MOBA_BLOCK = 256
MOBA_TOPK = 3
ROPE_THETA = 500000.0
ROPE_DIM = HEAD_DIM // 4
GMLP_WIDTH = 512
GMLP_GROUPS = 4
GMLP_CHUNK = 128
N_BRANCHES = 3
NORM_EPS = 1e-6

LANES = 128
V7X_VMEM_BYTES = 64 * 1024 * 1024
VMEM_CEILING = V7X_VMEM_BYTES - 8 * 1024 * 1024

MLSTM_KERNEL_CHUNK = 128
GATE_LANES = LANES
MASK_BIG = 2.0 ** 100
MOBA_GROUP = 4

NT_DIMS = (((1,), (1,)), ((), ()))
TN_DIMS = (((0,), (0,)), ((), ()))


def _nbytes(shape, dtype):
    return int(np.prod(shape)) * jnp.dtype(dtype).itemsize


def _params(semantics, blocks, scratch=()):
    need = 2 * sum(_nbytes(s, d) for s, d in blocks) + sum(_nbytes(s, d) for s, d in scratch)
    limit = min(VMEM_CEILING, need + need // 4 + 4 * 1024 * 1024)
    return pltpu.CompilerParams(dimension_semantics=semantics, vmem_limit_bytes=limit)


def _rmsnorm_kernel(x_ref, g_ref, o_ref):
    x = x_ref[...]
    y = x * lax.rsqrt(jnp.mean(x * x, axis=-1, keepdims=True) + NORM_EPS)
    o_ref[...] = (y * g_ref[...]).astype(o_ref.dtype)


def _rmsnorm(x, g, out_dtype, tm=512):
    m, d = x.shape
    return pl.pallas_call(
        _rmsnorm_kernel,
        grid=(m // tm,),
        in_specs=[pl.BlockSpec((tm, d), lambda i: (i, 0)),
                  pl.BlockSpec((1, d), lambda i: (0, 0))],
        out_specs=pl.BlockSpec((tm, d), lambda i: (i, 0)),
        out_shape=jax.ShapeDtypeStruct((m, d), out_dtype),
        compiler_params=_params(("arbitrary",), [((tm, d), F32), ((tm, d), out_dtype)]),
        name="rmsnorm",
    )(x, g.reshape(1, d))


def _inproj_kernel(h_ref, w_ref, wif_ref, cs_ref, z_ref, zif_ref):
    acc = jnp.dot(h_ref[...], w_ref[...], preferred_element_type=F32)
    z_ref[...] = acc * cs_ref[...]

    @pl.when(pl.program_id(1) == 0)
    def _():
        zif_ref[...] = jnp.dot(h_ref[...], wif_ref[...], preferred_element_type=F32)


def _inproj(h, w_main, w_if, colscale, tm=1024, tn=512):
    m, k = h.shape
    n = w_main.shape[1]
    blocks = [((tm, k), BF16), ((k, tn), BF16), ((k, GATE_LANES), BF16), ((1, tn), F32),
              ((tm, tn), F32), ((tm, GATE_LANES), F32)]
    return pl.pallas_call(
        _inproj_kernel,
        grid=(m // tm, n // tn),
        in_specs=[pl.BlockSpec((tm, k), lambda i, j: (i, 0)),
                  pl.BlockSpec((k, tn), lambda i, j: (0, j)),
                  pl.BlockSpec((k, GATE_LANES), lambda i, j: (0, 0)),
                  pl.BlockSpec((1, tn), lambda i, j: (0, j))],
        out_specs=[pl.BlockSpec((tm, tn), lambda i, j: (i, j)),
                   pl.BlockSpec((tm, GATE_LANES), lambda i, j: (i, 0))],
        out_shape=[jax.ShapeDtypeStruct((m, n), F32),
                   jax.ShapeDtypeStruct((m, GATE_LANES), F32)],
        compiler_params=_params(("arbitrary", "arbitrary"), blocks),
        name="in_proj",
    )(h, w_main, w_if, colscale)


def _log_sigmoid(x):
    return jnp.minimum(x, 0.0) - jnp.log1p(jnp.exp(-jnp.abs(x)))


def _mlstm_kernel(q_ref, k_ref, v_ref, o_ref, zif_ref, gb_ref, ng_ref, y_ref, c_sc, n_sc, m_sc):
    nbatch, chunk, _ = q_ref.shape
    heads = MLSTM_HEADS

    @pl.when(pl.program_id(0) == 0)
    def _():
        c_sc[...] = jnp.zeros_like(c_sc)
        n_sc[...] = jnp.zeros_like(n_sc)
        m_sc[...] = jnp.zeros_like(m_sc)

    row = lax.broadcasted_iota(jnp.int32, (chunk, chunk), 0)
    col = lax.broadcasted_iota(jnp.int32, (chunk, chunk), 1)
    causal = col <= row
    tril = jnp.where(causal, 1.0, 0.0).astype(F32)
    lane = lax.broadcasted_iota(jnp.int32, (chunk, GATE_LANES), 1)

    for b in range(nbatch):
        pre = zif_ref[b] + gb_ref[...]
        gates = jnp.where(lane < heads, pre, _log_sigmoid(pre))
        gcum = jnp.dot(tril, gates, preferred_element_type=F32,
                       precision=lax.Precision.HIGHEST)
        gates_t = gates.T
        gcum_t = gcum.T
        for h in range(heads):
            s = b * heads + h
            sl = slice(h * HEAD_DIM, (h + 1) * HEAD_DIM)
            q = q_ref[b, :, sl]
            k = k_ref[b, :, sl]
            v = v_ref[b, :, sl]
            qb = q.astype(BF16)
            kb = k.astype(BF16)
            vb = v.astype(BF16)
            g_t = gcum[:, heads + h:heads + h + 1]
            g_s = gcum_t[heads + h:heads + h + 1, :]
            i_s = gates_t[h:h + 1, :]
            i_t = gates[:, h:h + 1]
            m_prev = m_sc[s][:, 0:1]
            c_prev = c_sc[s]
            n_prev = n_sc[s]

            log_w = jnp.where(causal, g_t - g_s + i_s, -jnp.inf)
            log_a = g_t + m_prev
            m_row = jnp.maximum(jnp.max(log_w, axis=1, keepdims=True), log_a)
            qk = lax.dot_general(qb, kb, NT_DIMS, preferred_element_type=F32) * jnp.exp(log_w - m_row)
            a = jnp.exp(log_a - m_row)
            num = (jnp.dot(qk.astype(BF16), vb, preferred_element_type=F32)
                   + a * lax.dot_general(qb, c_prev.astype(BF16), NT_DIMS, preferred_element_type=F32))
            den = (jnp.sum(qk, axis=1, keepdims=True)
                   + a * jnp.sum(q * n_prev, axis=1, keepdims=True))
            h_out = num / jnp.maximum(jnp.abs(den), jnp.exp(-m_row))

            g_last = gcum[chunk - 1:chunk, heads + h:heads + h + 1]
            log_u = g_last - g_t + i_t
            m_new = jnp.maximum(g_last + m_prev, jnp.max(log_u, axis=0, keepdims=True))
            decay = jnp.exp(g_last + m_prev - m_new)
            u = jnp.exp(log_u - m_new)
            c_sc[s] = decay * c_prev + lax.dot_general((u * v).astype(BF16), kb, TN_DIMS,
                                                       preferred_element_type=F32)
            n_sc[s] = decay * n_prev + jnp.sum(u * k, axis=0, keepdims=True)
            m_sc[s] = jnp.broadcast_to(m_new, (1, LANES))

            yn = h_out * lax.rsqrt(jnp.mean(h_out * h_out, axis=-1, keepdims=True) + NORM_EPS)
            y_ref[b, :, sl] = (jax.nn.sigmoid(o_ref[b, :, sl]) * (yn * ng_ref[:, sl])).astype(y_ref.dtype)


def _mlstm(z3, zif3, gate_b, norm_g):
    nbatch, seq, _ = z3.shape
    chunk = MLSTM_KERNEL_CHUNK
    w = MLSTM_WIDTH
    streams = nbatch * MLSTM_HEADS
    gb = jnp.pad(gate_b, (0, GATE_LANES - gate_b.shape[0])).reshape(1, GATE_LANES)
    blocks = [((nbatch, chunk, w), F32)] * 4 + [((nbatch, chunk, GATE_LANES), F32),
                                               ((nbatch, chunk, w), BF16)]
    scratch = [((streams, HEAD_DIM, HEAD_DIM), F32), ((streams, 1, LANES), F32), ((streams, 1, LANES), F32)]

    def zcol(cb):
        return pl.BlockSpec((nbatch, chunk, w), lambda c: (0, c, cb))

    return pl.pallas_call(
        _mlstm_kernel,
        grid=(seq // chunk,),
        in_specs=[zcol(0), zcol(1), zcol(2), zcol(3),
                  pl.BlockSpec((nbatch, chunk, GATE_LANES), lambda c: (0, c, 0)),
                  pl.BlockSpec((1, GATE_LANES), lambda c: (0, 0)),
                  pl.BlockSpec((1, w), lambda c: (0, 0))],
        out_specs=pl.BlockSpec((nbatch, chunk, w), lambda c: (0, c, 0)),
        out_shape=jax.ShapeDtypeStruct((nbatch, seq, w), BF16),
        scratch_shapes=[pltpu.VMEM(s, d) for s, d in scratch],
        compiler_params=_params(("arbitrary",), blocks, scratch),
        name="mlstm",
    )(z3, z3, z3, z3, zif3, gb, norm_g.reshape(1, w))


def _rope_tables(seq):
    half = ROPE_DIM // 2
    inv_freq = ROPE_THETA ** (-jnp.arange(0, ROPE_DIM, 2, dtype=F32) / ROPE_DIM)
    ang = jnp.arange(seq, dtype=F32)[:, None] * inv_freq[None, :]
    cos = jnp.cos(ang)
    sin = jnp.sin(ang)
    ones = jnp.ones((seq, HEAD_DIM - ROPE_DIM), F32)
    cos_tab = jnp.concatenate([cos, cos, ones], axis=1)
    sin_tab = jnp.concatenate([-sin, sin, 0.0 * ones], axis=1)
    assert cos_tab.shape == (seq, HEAD_DIM) and half * 2 == ROPE_DIM
    return cos_tab, sin_tab


def _rotary(t, cos, sin):
    half = ROPE_DIM // 2
    lane = lax.broadcasted_iota(jnp.int32, t.shape, 1)
    upper = pltpu.roll(t, HEAD_DIM - half, axis=1)
    lower = pltpu.roll(t, half, axis=1)
    partner = jnp.where(lane < half, upper, lower)
    return jnp.where(lane < ROPE_DIM, t * cos + partner * sin, t)


def _moba_prep_kernel(q_ref, k_ref, v_ref, cos_ref, sin_ref, qa_ref, kb_ref, vb_ref, km_sc):
    blk = q_ref.shape[1]
    d = HEAD_DIM
    j = pl.program_id(1)

    @pl.when(j == 0)
    def _():
        km_sc[...] = jnp.zeros_like(km_sc)

    cos = cos_ref[...]
    sin = sin_ref[...]
    lane = lax.broadcasted_iota(jnp.int32, (blk, LANES), 1)
    lane_f = lane.astype(F32)
    mean_row = lax.broadcasted_iota(jnp.int32, (LANES, d), 0)
    vb_ref[0] = v_ref[0].astype(BF16)
    for h in range(MOBA_HEADS):
        sl = slice(h * d, (h + 1) * d)
        qf = _rotary(q_ref[0, :, sl], cos, sin)
        kk = _rotary(k_ref[0, :, sl], cos, sin)
        kb_ref[0, :, sl] = kk.astype(BF16)

        gate = lax.dot_general(qf, km_sc[h], NT_DIMS, preferred_element_type=F32,
                               precision=lax.Precision.HIGHEST)
        gate = jnp.where(lane < j, gate, -jnp.inf)
        sel_m1 = jnp.full((blk, LANES), -1.0, F32)
        for _ in range(MOBA_TOPK):
            mx = jnp.max(gate, axis=1, keepdims=True)
            first = jnp.min(jnp.where(gate == mx, lane_f, float(LANES)), axis=1, keepdims=True)
            first = jnp.where(mx > -jnp.inf, first, -1.0)
            hit = lane_f == first
            sel_m1 = jnp.where(hit, 0.0, sel_m1)
            gate = jnp.where(hit, -jnp.inf, gate)
        qa_ref[0, :, 2 * h * d:(2 * h + 1) * d] = qf.astype(BF16)
        qa_ref[0, :, (2 * h + 1) * d:(2 * h + 2) * d] = sel_m1.astype(BF16)

        km_sc[h] = jnp.where(mean_row == j, jnp.mean(kk, axis=0, keepdims=True), km_sc[h])


def _moba_prep(z3, cos_tab, sin_tab, q_cb, k_cb, v_cb):
    nbatch, seq, _ = z3.shape
    blk = MOBA_BLOCK
    w = MOBA_WIDTH
    d = HEAD_DIM
    blocks = [((1, blk, w), F32)] * 3 + [((blk, d), F32)] * 2 + [((1, blk, 2 * w), BF16), ((1, blk, w), BF16),
                                                                  ((1, blk, w), BF16)]
    scratch = [((MOBA_HEADS, LANES, d), F32)]
    return pl.pallas_call(
        _moba_prep_kernel,
        grid=(nbatch, seq // blk),
        in_specs=[pl.BlockSpec((1, blk, w), lambda b, j: (b, j, q_cb)),
                  pl.BlockSpec((1, blk, w), lambda b, j: (b, j, k_cb)),
                  pl.BlockSpec((1, blk, w), lambda b, j: (b, j, v_cb)),
                  pl.BlockSpec((blk, d), lambda b, j: (j, 0)),
                  pl.BlockSpec((blk, d), lambda b, j: (j, 0))],
        out_specs=[pl.BlockSpec((1, blk, 2 * w), lambda b, j: (b, j, 0)),
                   pl.BlockSpec((1, blk, w), lambda b, j: (b, j, 0)),
                   pl.BlockSpec((1, blk, w), lambda b, j: (b, j, 0))],
        out_shape=[jax.ShapeDtypeStruct((nbatch, seq, 2 * w), BF16),
                   jax.ShapeDtypeStruct((nbatch, seq, w), BF16),
                   jax.ShapeDtypeStruct((nbatch, seq, w), BF16)],
        scratch_shapes=[pltpu.VMEM(s, dt) for s, dt in scratch],
        compiler_params=_params(("arbitrary", "arbitrary"), blocks, scratch),
        name="moba_prep",
    )(z3, z3, z3, cos_tab, sin_tab)


def _moba_attn_kernel(q_ref, k_ref, v_ref, o_ref):
    blk = q_ref.shape[1]
    d = HEAD_DIM
    nblk = k_ref.shape[1] // blk
    group = MOBA_GROUP
    j = pl.program_id(2)
    scale = d ** -0.5
    q_aug = q_ref[0]
    lane = lax.broadcasted_iota(jnp.int32, (blk, LANES), 1)

    def softmax_partial(s, v_rows):
        m = jnp.max(s, axis=1, keepdims=True)
        p = jnp.exp(s - m)
        return m, jnp.sum(p, axis=1, keepdims=True), jnp.dot(p.astype(BF16), v_rows, preferred_element_type=F32)

    def past_partial(n):
        c0 = pl.multiple_of(jnp.minimum(n, nblk - 1) * blk, blk)
        onehot = jnp.where(lane == n, MASK_BIG, 0.0).astype(BF16)
        k_aug = jnp.concatenate([k_ref[0, pl.ds(c0, blk), :], onehot], axis=1)
        s = lax.dot_general(q_aug, k_aug, NT_DIMS, preferred_element_type=F32) * scale
        return softmax_partial(s, v_ref[0, pl.ds(c0, blk), :])

    def merge(state, parts):
        m_run, l_run, acc_run = state
        m_new = m_run
        for m, _, _ in parts:
            m_new = jnp.maximum(m_new, m)
        w = jnp.exp(m_run - m_new)
        l_new = w * l_run
        acc = w * acc_run
        for m, l, a in parts:
            w = jnp.exp(m - m_new)
            l_new = l_new + w * l
            acc = acc + w * a
        return m_new, l_new, acc

    r0 = pl.multiple_of(j * blk, blk)
    row = lax.broadcasted_iota(jnp.int32, (blk, blk), 0)
    col = lax.broadcasted_iota(jnp.int32, (blk, blk), 1)
    s_own = lax.dot_general(q_aug[:, :d], k_ref[0, pl.ds(r0, blk), :], NT_DIMS,
                            preferred_element_type=F32) * scale
    s_own = jnp.where(col <= row, s_own, -jnp.inf)
    state = softmax_partial(s_own, v_ref[0, pl.ds(r0, blk), :])

    state = merge(state, [past_partial(n) for n in range(group)])
    ngroups = lax.div(j + (group - 1), group)

    def group_body(g, st):
        return merge(st, [past_partial(g * group + i) for i in range(group)])

    _, l_fin, acc = lax.fori_loop(1, ngroups, group_body, state)
    o_ref[0] = (acc / l_fin).astype(o_ref.dtype)


def _moba_attn(q_aug, kb, vb):
    nbatch, seq, w = kb.shape
    blk = MOBA_BLOCK
    d = HEAD_DIM
    blocks = [((1, blk, 2 * d), BF16), ((1, seq, d), BF16), ((1, seq, d), BF16), ((1, blk, d), BF16)]
    return pl.pallas_call(
        _moba_attn_kernel,
        grid=(nbatch, MOBA_HEADS, seq // blk),
        in_specs=[pl.BlockSpec((1, blk, 2 * d), lambda b, h, j: (b, j, h)),
                  pl.BlockSpec((1, seq, d), lambda b, h, j: (b, 0, h)),
                  pl.BlockSpec((1, seq, d), lambda b, h, j: (b, 0, h))],
        out_specs=pl.BlockSpec((1, blk, d), lambda b, h, j: (b, j, h)),
        out_shape=jax.ShapeDtypeStruct((nbatch, seq, w), BF16),
        compiler_params=_params(("arbitrary", "arbitrary", "arbitrary"), blocks),
        name="moba_attn",
    )(q_aug, kb, vb)


def _gelu_tanh(x):
    c = np.sqrt(2.0 / np.pi).astype(np.float32)
    return x * (0.5 * (1.0 + jnp.tanh(c * (x + 0.044715 * (x * x * x)))))


def _gmlp_kernel(u_ref, v_ref, lg_ref, lb_ref, ws_ref, bst_ref, y_ref):
    rows = u_ref.shape[1]
    t = GMLP_CHUNK
    gd = GMLP_WIDTH // GMLP_GROUPS
    v = _gelu_tanh(v_ref[0])
    mu = jnp.mean(v, axis=-1, keepdims=True)
    vc = v - mu
    vln = vc * lax.rsqrt(jnp.mean(vc * vc, axis=-1, keepdims=True) + NORM_EPS) * lg_ref[...] + lb_ref[...]
    vb = vln.astype(BF16)
    row = lax.broadcasted_iota(jnp.int32, (t, t), 0)
    col = lax.broadcasted_iota(jnp.int32, (t, t), 1)
    for g in range(GMLP_GROUPS):
        wg = jnp.where(col <= row, ws_ref[g], 0.0).astype(BF16)
        bias = bst_ref[:, g:g + 1]
        cols = slice(g * gd, (g + 1) * gd)
        for c in range(rows // t):
            rs = slice(c * t, (c + 1) * t)
            mixed = jnp.dot(wg, vb[rs, cols], preferred_element_type=F32) + bias
            y_ref[0, rs, cols] = (_gelu_tanh(u_ref[0, rs, cols]) * mixed).astype(y_ref.dtype)


def _gmlp(z3, u_cb, v_cb, ln_g, ln_b, ws, bs, rows=512):
    nbatch, seq, _ = z3.shape
    w = GMLP_WIDTH
    t = GMLP_CHUNK
    blocks = [((1, rows, w), F32), ((1, rows, w), F32), ((GMLP_GROUPS, t, t), F32), ((1, rows, w), BF16)]
    return pl.pallas_call(
        _gmlp_kernel,
        grid=(nbatch, seq // rows),
        in_specs=[pl.BlockSpec((1, rows, w), lambda b, c: (b, c, u_cb)),
                  pl.BlockSpec((1, rows, w), lambda b, c: (b, c, v_cb)),
                  pl.BlockSpec((1, w), lambda b, c: (0, 0)),
                  pl.BlockSpec((1, w), lambda b, c: (0, 0)),
                  pl.BlockSpec((GMLP_GROUPS, t, t), lambda b, c: (0, 0, 0)),
                  pl.BlockSpec((t, GMLP_GROUPS), lambda b, c: (0, 0))],
        out_specs=pl.BlockSpec((1, rows, w), lambda b, c: (b, c, 0)),
        out_shape=jax.ShapeDtypeStruct((nbatch, seq, w), BF16),
        compiler_params=_params(("arbitrary", "arbitrary"), blocks),
        name="gmlp",
    )(z3, z3, ln_g.reshape(1, w), ln_b.reshape(1, w), ws, bs.T)


def _layer_weight_spec(layer, k, tn, index_of):
    return pl.BlockSpec((None, k, tn), lambda *idx: (layer,) + index_of(*idx))


def _merge_kernel(h_ref, wga_ref, wgb_ref, wgc_ref, ya_ref, yb_ref, yc_ref, wa_ref, wb_ref, wc_ref, o_ref):
    h = h_ref[...]

    def branch(wg_ref, y_ref, w_ref):
        gate = jax.nn.sigmoid(jnp.dot(h, wg_ref[...], preferred_element_type=F32))
        return gate * jnp.dot(y_ref[...], w_ref[...].astype(BF16), preferred_element_type=F32)

    merged = branch(wga_ref, ya_ref, wa_ref) + branch(wgb_ref, yb_ref, wb_ref) + branch(wgc_ref, yc_ref, wc_ref)
    o_ref[...] = merged.astype(o_ref.dtype)


def _merge(h, w_gate, ya, yb, yc, wa, wb, wc, layer, tm=1024, tn=512):
    m, d = h.shape
    nblk = d // tn
    ka, kb, kc = ya.shape[1], yb.shape[1], yc.shape[1]
    blocks = ([((tm, d), BF16)] + [((d, tn), BF16)] * 3
              + [((tm, ka), BF16), ((tm, kb), BF16), ((tm, kc), BF16)]
              + [((ka, tn), F32), ((kb, tn), F32), ((kc, tn), F32), ((tm, tn), BF16)])

    def gate_spec(branch):
        return pl.BlockSpec((d, tn), lambda i, j: (0, branch * nblk + j))

    def col(i, j):
        return (0, j)

    return pl.pallas_call(
        _merge_kernel,
        grid=(m // tm, nblk),
        in_specs=[pl.BlockSpec((tm, d), lambda i, j: (i, 0)),
                  gate_spec(0), gate_spec(1), gate_spec(2),
                  pl.BlockSpec((tm, ka), lambda i, j: (i, 0)),
                  pl.BlockSpec((tm, kb), lambda i, j: (i, 0)),
                  pl.BlockSpec((tm, kc), lambda i, j: (i, 0)),
                  _layer_weight_spec(layer, ka, tn, col),
                  _layer_weight_spec(layer, kb, tn, col),
                  _layer_weight_spec(layer, kc, tn, col)],
        out_specs=pl.BlockSpec((tm, tn), lambda i, j: (i, j)),
        out_shape=jax.ShapeDtypeStruct((m, d), BF16),
        compiler_params=_params(("arbitrary", "arbitrary"), blocks),
        name="merge",
    )(h, w_gate, w_gate, w_gate, ya, yb, yc, wa, wb, wc)


def _residual_matmul_kernel(a_ref, w_ref, r_ref, o_ref):
    @pl.when(pl.program_id(2) == 0)
    def _():
        o_ref[...] = r_ref[...]

    o_ref[...] += jnp.dot(a_ref[...], w_ref[...].astype(BF16), preferred_element_type=F32)


def _residual_matmul(a, w, layer, res, tm=1024, tn=1024, tk=1024):
    m, k = a.shape
    n = w.shape[2]
    blocks = [((tm, tk), BF16), ((tk, tn), F32), ((tm, tn), F32), ((tm, tn), F32)]
    return pl.pallas_call(
        _residual_matmul_kernel,
        grid=(m // tm, n // tn, k // tk),
        in_specs=[pl.BlockSpec((tm, tk), lambda i, j, kk: (i, kk)),
                  _layer_weight_spec(layer, tk, tn, lambda i, j, kk: (kk, j)),
                  pl.BlockSpec((tm, tn), lambda i, j, kk: (i, j))],
        out_specs=pl.BlockSpec((tm, tn), lambda i, j, kk: (i, j)),
        out_shape=jax.ShapeDtypeStruct((m, n), F32),
        compiler_params=_params(("arbitrary", "arbitrary", "arbitrary"), blocks),
        name="residual_matmul",
    )(a, w, res)


def _relu2_matmul_kernel(a_ref, w_ref, o_ref):
    up = jnp.maximum(jnp.dot(a_ref[...], w_ref[...].astype(BF16), preferred_element_type=F32), 0.0)
    o_ref[...] = (up * up).astype(o_ref.dtype)


def _relu2_matmul(a, w, layer, tm=1024, tn=1024):
    m, k = a.shape
    n = w.shape[2]
    blocks = [((tm, k), BF16), ((k, tn), F32), ((tm, tn), BF16)]
    return pl.pallas_call(
        _relu2_matmul_kernel,
        grid=(m // tm, n // tn),
        in_specs=[pl.BlockSpec((tm, k), lambda i, j: (i, 0)),
                  _layer_weight_spec(layer, k, tn, lambda i, j: (0, j))],
        out_specs=pl.BlockSpec((tm, tn), lambda i, j: (i, j)),
        out_shape=jax.ShapeDtypeStruct((m, n), BF16),
        compiler_params=_params(("arbitrary", "arbitrary"), blocks),
        name="mlp_up",
    )(a, w)


def _ple_kernel(h_ref, wg_ref, p_ref, wp_ref, r_ref, o_ref):
    gate = jax.nn.sigmoid(jnp.dot(h_ref[...], wg_ref[...].astype(BF16), preferred_element_type=F32))
    emb = jnp.dot(p_ref[...].astype(BF16), wp_ref[...].astype(BF16), preferred_element_type=F32)
    o_ref[...] = r_ref[...] + gate * emb


def _ple(h, w_gate, p, w_proj, layer, res, tm=1024, tn=512):
    m, d = h.shape
    pd = p.shape[2]
    blocks = [((tm, d), BF16), ((d, tn), F32), ((tm, pd), F32), ((pd, tn), F32),
              ((tm, tn), F32), ((tm, tn), F32)]
    return pl.pallas_call(
        _ple_kernel,
        grid=(m // tm, d // tn),
        in_specs=[pl.BlockSpec((tm, d), lambda i, j: (i, 0)),
                  _layer_weight_spec(layer, d, tn, lambda i, j: (0, j)),
                  pl.BlockSpec((None, tm, pd), lambda i, j: (layer, i, 0)),
                  _layer_weight_spec(layer, pd, tn, lambda i, j: (0, j)),
                  pl.BlockSpec((tm, tn), lambda i, j: (i, j))],
        out_specs=pl.BlockSpec((tm, tn), lambda i, j: (i, j)),
        out_shape=jax.ShapeDtypeStruct((m, d), F32),
        compiler_params=_params(("arbitrary", "arbitrary"), blocks),
        name="ple",
    )(h, w_gate, p, w_proj, res)


def kernel(x, p, norm_mix_g, w_in, mlstm_gate_b, mlstm_norm_g, gmlp_norm_g, gmlp_norm_b, gmlp_ws, gmlp_bs,
           w_branch_a, w_branch_b, w_branch_c, w_out, norm_mlp_g, w_mlp_up, w_mlp_down, norm_ple_g,
           w_ple_gate, w_ple_proj, final_norm_g):
    nbatch, seq, d = x.shape
    depth = w_in.shape[0]
    m = nbatch * seq
    assert d == MLSTM_WIDTH + MOBA_WIDTH + GMLP_WIDTH

    qkvo_a = 4 * MLSTM_WIDTH
    gates_if = 2 * MLSTM_HEADS
    main_b = qkvo_a + gates_if
    main_cols = 3 * MOBA_WIDTH + 2 * GMLP_WIDTH
    gate_off = main_b + main_cols
    assert w_in.shape[2] == gate_off + N_BRANCHES * d

    moba_q_cb = qkvo_a // MOBA_WIDTH
    moba_k_cb = moba_q_cb + 1
    moba_v_cb = moba_k_cb + 1
    gmlp_u_cb = (qkvo_a + 3 * MOBA_WIDTH) // GMLP_WIDTH
    gmlp_v_cb = gmlp_u_cb + 1

    z_cols = qkvo_a + main_cols
    colscale = jnp.ones((1, z_cols), F32).at[:, MLSTM_WIDTH:2 * MLSTM_WIDTH].set(HEAD_DIM ** -0.5)
    cos_tab, sin_tab = _rope_tables(seq)

    xf = x.reshape(m, d)
    p_flat = p.reshape(depth, m, p.shape[-1])
    for i in range(depth):
        w_i = w_in[i]
        w_main = jnp.concatenate([w_i[:, :qkvo_a], w_i[:, main_b:gate_off]], axis=1).astype(BF16)
        w_if = jnp.pad(w_i[:, qkvo_a:main_b], ((0, 0), (0, GATE_LANES - gates_if))).astype(BF16)
        w_gate = w_i[:, gate_off:].astype(BF16)

        h = _rmsnorm(xf, norm_mix_g[i], BF16)
        z, zif = _inproj(h, w_main, w_if, colscale)
        z3 = z.reshape(nbatch, seq, z_cols)
        zif3 = zif.reshape(nbatch, seq, GATE_LANES)

        ya = _mlstm(z3, zif3, mlstm_gate_b[i], mlstm_norm_g[i])
        yb = _moba_attn(*_moba_prep(z3, cos_tab, sin_tab, moba_q_cb, moba_k_cb, moba_v_cb))
        yc = _gmlp(z3, gmlp_u_cb, gmlp_v_cb, gmlp_norm_g[i], gmlp_norm_b[i], gmlp_ws[i], gmlp_bs[i])

        merged = _merge(h, w_gate, ya.reshape(m, -1), yb.reshape(m, -1), yc.reshape(m, -1),
                        w_branch_a, w_branch_b, w_branch_c, i)
        xf = _residual_matmul(merged, w_out, i, xf)

        h2 = _rmsnorm(xf, norm_mlp_g[i], BF16)
        hidden = _relu2_matmul(h2, w_mlp_up, i)
        xf = _residual_matmul(hidden, w_mlp_down, i, xf)

        h3 = _rmsnorm(xf, norm_ple_g[i], BF16)
        xf = _ple(h3, w_ple_gate, p_flat, w_ple_proj, i, xf)

    return _rmsnorm(xf, final_norm_g, F32).reshape(nbatch, seq, d)
```

```python
import functools

import jax
import jax.numpy as jnp
import numpy as np
from jax import lax
from jax.experimental import pallas as pl
from jax.experimental.pallas import tpu as pltpu

F32 = jnp.float32
BF16 = jnp.bfloat16

HEAD_DIM = 128
MLSTM_HEADS = 4
MLSTM_WIDTH = MLSTM_HEADS * HEAD_DIM
MOBA_HEADS = 8
MOBA_WIDTH = MOBA_HEADS * HEAD_DIM
MOBA_BLOCK = 256
MOBA_TOPK = 3
ROPE_THETA = 500000.0
ROPE_DIM = HEAD_DIM // 4
GMLP_WIDTH = 512
GMLP_GROUPS = 4
GMLP_CHUNK = 128
N_BRANCHES = 3
NORM_EPS = 1e-6

LANES = 128
V7X_VMEM_BYTES = 64 * 1024 * 1024
VMEM_CEILING = V7X_VMEM_BYTES - 8 * 1024 * 1024

MLSTM_KERNEL_CHUNK = 128
GATE_LANES = LANES
MASK_BIG = 2.0 ** 100
MOBA_GROUP = 4

NT_DIMS = (((1,), (1,)), ((), ()))
TN_DIMS = (((0,), (0,)), ((), ()))


def _nbytes(shape, dtype):
    return int(np.prod(shape)) * jnp.dtype(dtype).itemsize


def _params(semantics, blocks, scratch=()):
    need = 2 * sum(_nbytes(s, d) for s, d in blocks) + sum(_nbytes(s, d) for s, d in scratch)
    limit = min(VMEM_CEILING, need + need // 4 + 4 * 1024 * 1024)
    return pltpu.CompilerParams(dimension_semantics=semantics, vmem_limit_bytes=limit)


def _rmsnorm_kernel(x_ref, g_ref, o_ref):
    x = x_ref[...]
    y = x * lax.rsqrt(jnp.mean(x * x, axis=-1, keepdims=True) + NORM_EPS)
    o_ref[...] = (y * g_ref[...]).astype(o_ref.dtype)


def _rmsnorm(x, g, out_dtype, tm=512):
    m, d = x.shape
    return pl.pallas_call(
        _rmsnorm_kernel,
        grid=(m // tm,),
        in_specs=[pl.BlockSpec((tm, d), lambda i: (i, 0)),
                  pl.BlockSpec((1, d), lambda i: (0, 0))],
        out_specs=pl.BlockSpec((tm, d), lambda i: (i, 0)),
        out_shape=jax.ShapeDtypeStruct((m, d), out_dtype),
        compiler_params=_params(("arbitrary",), [((tm, d), F32), ((tm, d), out_dtype)]),
        name="rmsnorm",
    )(x, g.reshape(1, d))


def _inproj_kernel(h_ref, w_ref, wif_ref, cs_ref, z_ref, zif_ref):
    acc = jnp.dot(h_ref[...], w_ref[...], preferred_element_type=F32)
    z_ref[...] = acc * cs_ref[...]

    @pl.when(pl.program_id(1) == 0)
    def _():
        zif_ref[...] = jnp.dot(h_ref[...], wif_ref[...], preferred_element_type=F32)


def _inproj(h, w_main, w_if, colscale, tm=1024, tn=512):
    m, k = h.shape
    n = w_main.shape[1]
    blocks = [((tm, k), BF16), ((k, tn), BF16), ((k, GATE_LANES), BF16), ((1, tn), F32),
              ((tm, tn), F32), ((tm, GATE_LANES), F32)]
    return pl.pallas_call(
        _inproj_kernel,
        grid=(m // tm, n // tn),
        in_specs=[pl.BlockSpec((tm, k), lambda i, j: (i, 0)),
                  pl.BlockSpec((k, tn), lambda i, j: (0, j)),
                  pl.BlockSpec((k, GATE_LANES), lambda i, j: (0, 0)),
                  pl.BlockSpec((1, tn), lambda i, j: (0, j))],
        out_specs=[pl.BlockSpec((tm, tn), lambda i, j: (i, j)),
                   pl.BlockSpec((tm, GATE_LANES), lambda i, j: (i, 0))],
        out_shape=[jax.ShapeDtypeStruct((m, n), F32),
                   jax.ShapeDtypeStruct((m, GATE_LANES), F32)],
        compiler_params=_params(("arbitrary", "arbitrary"), blocks),
        name="in_proj",
    )(h, w_main, w_if, colscale)


def _log_sigmoid(x):
    return jnp.minimum(x, 0.0) - jnp.log1p(jnp.exp(-jnp.abs(x)))


def _mlstm_kernel(q_ref, k_ref, v_ref, o_ref, zif_ref, gb_ref, ng_ref, y_ref, c_sc, n_sc, m_sc):
    nbatch, chunk, _ = q_ref.shape
    heads = MLSTM_HEADS

    @pl.when(pl.program_id(0) == 0)
    def _():
        c_sc[...] = jnp.zeros_like(c_sc)
        n_sc[...] = jnp.zeros_like(n_sc)
        m_sc[...] = jnp.zeros_like(m_sc)

    row = lax.broadcasted_iota(jnp.int32, (chunk, chunk), 0)
    col = lax.broadcasted_iota(jnp.int32, (chunk, chunk), 1)
    causal = col <= row
    tril = jnp.where(causal, 1.0, 0.0).astype(F32)
    lane = lax.broadcasted_iota(jnp.int32, (chunk, GATE_LANES), 1)

    for b in range(nbatch):
        pre = zif_ref[b] + gb_ref[...]
        gates = jnp.where(lane < heads, pre, _log_sigmoid(pre))
        gcum = jnp.dot(tril, gates, preferred_element_type=F32,
                       precision=lax.Precision.HIGHEST)
        gates_t = gates.T
        gcum_t = gcum.T
        for h in range(heads):
            s = b * heads + h
            sl = slice(h * HEAD_DIM, (h + 1) * HEAD_DIM)
            q = q_ref[b, :, sl]
            k = k_ref[b, :, sl]
            v = v_ref[b, :, sl]
            qb = q.astype(BF16)
            kb = k.astype(BF16)
            vb = v.astype(BF16)
            g_t = gcum[:, heads + h:heads + h + 1]
            g_s = gcum_t[heads + h:heads + h + 1, :]
            i_s = gates_t[h:h + 1, :]
            i_t = gates[:, h:h + 1]
            m_prev = m_sc[s][:, 0:1]
            c_prev = c_sc[s]
            n_prev = n_sc[s]

            log_w = jnp.where(causal, g_t - g_s + i_s, -jnp.inf)
            log_a = g_t + m_prev
            m_row = jnp.maximum(jnp.max(log_w, axis=1, keepdims=True), log_a)
            qk = lax.dot_general(qb, kb, NT_DIMS, preferred_element_type=F32) * jnp.exp(log_w - m_row)
            a = jnp.exp(log_a - m_row)
            num = (jnp.dot(qk.astype(BF16), vb, preferred_element_type=F32)
                   + a * lax.dot_general(qb, c_prev.astype(BF16), NT_DIMS, preferred_element_type=F32))
            den = (jnp.sum(qk, axis=1, keepdims=True)
                   + a * jnp.sum(q * n_prev, axis=1, keepdims=True))
            h_out = num / jnp.maximum(jnp.abs(den), jnp.exp(-m_row))

            g_last = gcum[chunk - 1:chunk, heads + h:heads + h + 1]
            log_u = g_last - g_t + i_t
            m_new = jnp.maximum(g_last + m_prev, jnp.max(log_u, axis=0, keepdims=True))
            decay = jnp.exp(g_last + m_prev - m_new)
            u = jnp.exp(log_u - m_new)
            c_sc[s] = decay * c_prev + lax.dot_general((u * v).astype(BF16), kb, TN_DIMS,
                                                       preferred_element_type=F32)
            n_sc[s] = decay * n_prev + jnp.sum(u * k, axis=0, keepdims=True)
            m_sc[s] = jnp.broadcast_to(m_new, (1, LANES))

            yn = h_out * lax.rsqrt(jnp.mean(h_out * h_out, axis=-1, keepdims=True) + NORM_EPS)
            y_ref[b, :, sl] = (jax.nn.sigmoid(o_ref[b, :, sl]) * (yn * ng_ref[:, sl])).astype(y_ref.dtype)


def _mlstm(z3, zif3, gate_b, norm_g):
    nbatch, seq, _ = z3.shape
    chunk = MLSTM_KERNEL_CHUNK
    w = MLSTM_WIDTH
    streams = nbatch * MLSTM_HEADS
    gb = jnp.pad(gate_b, (0, GATE_LANES - gate_b.shape[0])).reshape(1, GATE_LANES)
    blocks = [((nbatch, chunk, w), F32)] * 4 + [((nbatch, chunk, GATE_LANES), F32),
                                               ((nbatch, chunk, w), BF16)]
    scratch = [((streams, HEAD_DIM, HEAD_DIM), F32), ((streams, 1, LANES), F32), ((streams, 1, LANES), F32)]

    def zcol(cb):
        return pl.BlockSpec((nbatch, chunk, w), lambda c: (0, c, cb))

    return pl.pallas_call(
        _mlstm_kernel,
        grid=(seq // chunk,),
        in_specs=[zcol(0), zcol(1), zcol(2), zcol(3),
                  pl.BlockSpec((nbatch, chunk, GATE_LANES), lambda c: (0, c, 0)),
                  pl.BlockSpec((1, GATE_LANES), lambda c: (0, 0)),
                  pl.BlockSpec((1, w), lambda c: (0, 0))],
        out_specs=pl.BlockSpec((nbatch, chunk, w), lambda c: (0, c, 0)),
        out_shape=jax.ShapeDtypeStruct((nbatch, seq, w), BF16),
        scratch_shapes=[pltpu.VMEM(s, d) for s, d in scratch],
        compiler_params=_params(("arbitrary",), blocks, scratch),
        name="mlstm",
    )(z3, z3, z3, z3, zif3, gb, norm_g.reshape(1, w))


def _rope_tables(seq):
    half = ROPE_DIM // 2
    inv_freq = ROPE_THETA ** (-jnp.arange(0, ROPE_DIM, 2, dtype=F32) / ROPE_DIM)
    ang = jnp.arange(seq, dtype=F32)[:, None] * inv_freq[None, :]
    cos = jnp.cos(ang)
    sin = jnp.sin(ang)
    ones = jnp.ones((seq, HEAD_DIM - ROPE_DIM), F32)
    cos_tab = jnp.concatenate([cos, cos, ones], axis=1)
    sin_tab = jnp.concatenate([-sin, sin, 0.0 * ones], axis=1)
    assert cos_tab.shape == (seq, HEAD_DIM) and half * 2 == ROPE_DIM
    return cos_tab, sin_tab


def _rotary(t, cos, sin):
    half = ROPE_DIM // 2
    lane = lax.broadcasted_iota(jnp.int32, t.shape, 1)
    upper = pltpu.roll(t, HEAD_DIM - half, axis=1)
    lower = pltpu.roll(t, half, axis=1)
    partner = jnp.where(lane < half, upper, lower)
    return jnp.where(lane < ROPE_DIM, t * cos + partner * sin, t)


def _moba_prep_kernel(q_ref, k_ref, v_ref, cos_ref, sin_ref, qa_ref, kb_ref, vb_ref, km_sc):
    blk = q_ref.shape[1]
    d = HEAD_DIM
    j = pl.program_id(1)

    @pl.when(j == 0)
    def _():
        km_sc[...] = jnp.zeros_like(km_sc)

    cos = cos_ref[...]
    sin = sin_ref[...]
    lane = lax.broadcasted_iota(jnp.int32, (blk, LANES), 1)
    lane_f = lane.astype(F32)
    mean_row = lax.broadcasted_iota(jnp.int32, (LANES, d), 0)
    vb_ref[0] = v_ref[0].astype(BF16)
    for h in range(MOBA_HEADS):
        sl = slice(h * d, (h + 1) * d)
        qf = _rotary(q_ref[0, :, sl], cos, sin)
        kk = _rotary(k_ref[0, :, sl], cos, sin)
        kb_ref[0, :, sl] = kk.astype(BF16)

        gate = lax.dot_general(qf, km_sc[h], NT_DIMS, preferred_element_type=F32,
                               precision=lax.Precision.HIGHEST)
        gate = jnp.where(lane < j, gate, -jnp.inf)
        sel_m1 = jnp.full((blk, LANES), -1.0, F32)
        for _ in range(MOBA_TOPK):
            mx = jnp.max(gate, axis=1, keepdims=True)
            first = jnp.min(jnp.where(gate == mx, lane_f, float(LANES)), axis=1, keepdims=True)
            first = jnp.where(mx > -jnp.inf, first, -1.0)
            hit = lane_f == first
            sel_m1 = jnp.where(hit, 0.0, sel_m1)
            gate = jnp.where(hit, -jnp.inf, gate)
        qa_ref[0, :, 2 * h * d:(2 * h + 1) * d] = qf.astype(BF16)
        qa_ref[0, :, (2 * h + 1) * d:(2 * h + 2) * d] = sel_m1.astype(BF16)

        km_sc[h] = jnp.where(mean_row == j, jnp.mean(kk, axis=0, keepdims=True), km_sc[h])


def _moba_prep(z3, cos_tab, sin_tab, q_cb, k_cb, v_cb):
    nbatch, seq, _ = z3.shape
    blk = MOBA_BLOCK
    w = MOBA_WIDTH
    d = HEAD_DIM
    blocks = [((1, blk, w), F32)] * 3 + [((blk, d), F32)] * 2 + [((1, blk, 2 * w), BF16), ((1, blk, w), BF16),
                                                                  ((1, blk, w), BF16)]
    scratch = [((MOBA_HEADS, LANES, d), F32)]
    return pl.pallas_call(
        _moba_prep_kernel,
        grid=(nbatch, seq // blk),
        in_specs=[pl.BlockSpec((1, blk, w), lambda b, j: (b, j, q_cb)),
                  pl.BlockSpec((1, blk, w), lambda b, j: (b, j, k_cb)),
                  pl.BlockSpec((1, blk, w), lambda b, j: (b, j, v_cb)),
                  pl.BlockSpec((blk, d), lambda b, j: (j, 0)),
                  pl.BlockSpec((blk, d), lambda b, j: (j, 0))],
        out_specs=[pl.BlockSpec((1, blk, 2 * w), lambda b, j: (b, j, 0)),
                   pl.BlockSpec((1, blk, w), lambda b, j: (b, j, 0)),
                   pl.BlockSpec((1, blk, w), lambda b, j: (b, j, 0))],
        out_shape=[jax.ShapeDtypeStruct((nbatch, seq, 2 * w), BF16),
                   jax.ShapeDtypeStruct((nbatch, seq, w), BF16),
                   jax.ShapeDtypeStruct((nbatch, seq, w), BF16)],
        scratch_shapes=[pltpu.VMEM(s, dt) for s, dt in scratch],
        compiler_params=_params(("arbitrary", "arbitrary"), blocks, scratch),
        name="moba_prep",
    )(z3, z3, z3, cos_tab, sin_tab)


def _moba_attn_kernel(q_ref, k_ref, v_ref, o_ref):
    blk = q_ref.shape[1]
    d = HEAD_DIM
    nblk = k_ref.shape[1] // blk
    group = MOBA_GROUP
    j = pl.program_id(2)
    scale = d ** -0.5
    q_aug = q_ref[0]
    lane = lax.broadcasted_iota(jnp.int32, (blk, LANES), 1)

    def softmax_partial(s, v_rows):
        m = jnp.max(s, axis=1, keepdims=True)
        p = jnp.exp(s - m)
        return m, jnp.sum(p, axis=1, keepdims=True), jnp.dot(p.astype(BF16), v_rows, preferred_element_type=F32)

    def past_partial(n):
        c0 = pl.multiple_of(jnp.minimum(n, nblk - 1) * blk, blk)
        onehot = jnp.where(lane == n, MASK_BIG, 0.0).astype(BF16)
        k_aug = jnp.concatenate([k_ref[0, pl.ds(c0, blk), :], onehot], axis=1)
        s = lax.dot_general(q_aug, k_aug, NT_DIMS, preferred_element_type=F32) * scale
        return softmax_partial(s, v_ref[0, pl.ds(c0, blk), :])

    def merge(state, parts):
        m_run, l_run, acc_run = state
        m_new = m_run
        for m, _, _ in parts:
            m_new = jnp.maximum(m_new, m)
        w = jnp.exp(m_run - m_new)
        l_new = w * l_run
        acc = w * acc_run
        for m, l, a in parts:
            w = jnp.exp(m - m_new)
            l_new = l_new + w * l
            acc = acc + w * a
        return m_new, l_new, acc

    r0 = pl.multiple_of(j * blk, blk)
    row = lax.broadcasted_iota(jnp.int32, (blk, blk), 0)
    col = lax.broadcasted_iota(jnp.int32, (blk, blk), 1)
    s_own = lax.dot_general(q_aug[:, :d], k_ref[0, pl.ds(r0, blk), :], NT_DIMS,
                            preferred_element_type=F32) * scale
    s_own = jnp.where(col <= row, s_own, -jnp.inf)
    state = softmax_partial(s_own, v_ref[0, pl.ds(r0, blk), :])

    state = merge(state, [past_partial(n) for n in range(group)])
    ngroups = lax.div(j + (group - 1), group)

    def group_body(g, st):
        return merge(st, [past_partial(g * group + i) for i in range(group)])

    _, l_fin, acc = lax.fori_loop(1, ngroups, group_body, state)
    o_ref[0] = (acc / l_fin).astype(o_ref.dtype)


def _moba_attn(q_aug, kb, vb):
    nbatch, seq, w = kb.shape
    blk = MOBA_BLOCK
    d = HEAD_DIM
    blocks = [((1, blk, 2 * d), BF16), ((1, seq, d), BF16), ((1, seq, d), BF16), ((1, blk, d), BF16)]
    return pl.pallas_call(
        _moba_attn_kernel,
        grid=(nbatch, MOBA_HEADS, seq // blk),
        in_specs=[pl.BlockSpec((1, blk, 2 * d), lambda b, h, j: (b, j, h)),
                  pl.BlockSpec((1, seq, d), lambda b, h, j: (b, 0, h)),
                  pl.BlockSpec((1, seq, d), lambda b, h, j: (b, 0, h))],
        out_specs=pl.BlockSpec((1, blk, d), lambda b, h, j: (b, j, h)),
        out_shape=jax.ShapeDtypeStruct((nbatch, seq, w), BF16),
        compiler_params=_params(("arbitrary", "arbitrary", "arbitrary"), blocks),
        name="moba_attn",
    )(q_aug, kb, vb)


def _rope_tables_t(seq):
    inv_freq = ROPE_THETA ** (-jnp.arange(0, ROPE_DIM, 2, dtype=F32) / ROPE_DIM)
    ang = jnp.arange(seq, dtype=F32)[:, None] * inv_freq[None, :]
    cos = jnp.cos(ang).T
    sin = jnp.sin(ang).T
    return jnp.concatenate([cos, cos], axis=0), jnp.concatenate([-sin, sin], axis=0)


def _moba_prep_t_kernel(q_ref, k_ref, v_ref, cos_ref, sin_ref, cost_ref, sint_ref, qa_ref, kb_ref, vt_ref, km_sc):
    blk = q_ref.shape[1]
    d = HEAD_DIM
    half = ROPE_DIM // 2
    nsel = km_sc.shape[1]
    j = pl.program_id(1)

    @pl.when(j == 0)
    def _():
        km_sc[...] = jnp.zeros_like(km_sc)

    cos = cos_ref[...]
    sin = sin_ref[...]
    cos_t = cost_ref[...]
    sin_t = sint_ref[...]
    blk_id = lax.broadcasted_iota(jnp.int32, (nsel, blk), 0)
    blk_id_f = blk_id.astype(F32)
    mean_row = lax.broadcasted_iota(jnp.int32, (nsel, d), 0)
    for h in range(MOBA_HEADS):
        sl = slice(h * d, (h + 1) * d)
        q_t = q_ref[0, :, sl].T
        top = q_t[:ROPE_DIM]
        partner = jnp.concatenate([top[half:], top[:half]], axis=0)
        q_t = jnp.concatenate([top * cos_t + partner * sin_t, q_t[ROPE_DIM:]], axis=0)

        gate = jnp.dot(km_sc[h], q_t, preferred_element_type=F32,
                       precision=lax.Precision.HIGHEST)
        gate = jnp.where(blk_id < j, gate, -jnp.inf)
        sel_m1 = jnp.full((nsel, blk), -1.0, F32)
        for _ in range(MOBA_TOPK):
            mx = jnp.max(gate, axis=0, keepdims=True)
            first = jnp.min(jnp.where(gate == mx, blk_id_f, float(nsel)), axis=0, keepdims=True)
            first = jnp.where(mx > -jnp.inf, first, -1.0)
            hit = blk_id_f == first
            sel_m1 = jnp.where(hit, 0.0, sel_m1)
            gate = jnp.where(hit, -jnp.inf, gate)
        qa_ref[0, h, 0, 0:d, :] = q_t.astype(BF16)
        qa_ref[0, h, 0, d:d + nsel, :] = sel_m1.astype(BF16)
        qa_ref[0, h, 0, d + nsel:, :] = jnp.full((d - nsel, blk), -1.0, BF16)

        kk = _rotary(k_ref[0, :, sl], cos, sin)
        kb_ref[0, :, sl] = kk.astype(BF16)
        km_sc[h] = jnp.where(mean_row == j, jnp.mean(kk, axis=0, keepdims=True), km_sc[h])

        vt_ref[0, h, 0] = v_ref[0, :, sl].T.astype(BF16)


def _moba_prep_t(z3, tables, q_cb, k_cb, v_cb):
    nbatch, seq, _ = z3.shape
    blk = MOBA_BLOCK
    nblk = seq // blk
    nsel = -(-nblk // 16) * 16
    assert nsel <= HEAD_DIM
    w = MOBA_WIDTH
    d = HEAD_DIM
    hh = MOBA_HEADS
    cos_tab, sin_tab, cos_t, sin_t = tables
    blocks = ([((1, blk, w), F32)] * 3 + [((blk, d), F32)] * 2 + [((ROPE_DIM, blk), F32)] * 2
              + [((hh, 2 * d, blk), BF16), ((1, blk, w), BF16), ((hh, d, blk), BF16)])
    scratch = [((hh, nsel, d), F32)]
    return pl.pallas_call(
        _moba_prep_t_kernel,
        grid=(nbatch, nblk),
        in_specs=[pl.BlockSpec((1, blk, w), lambda b, j: (b, j, q_cb)),
                  pl.BlockSpec((1, blk, w), lambda b, j: (b, j, k_cb)),
                  pl.BlockSpec((1, blk, w), lambda b, j: (b, j, v_cb)),
                  pl.BlockSpec((blk, d), lambda b, j: (j, 0)),
                  pl.BlockSpec((blk, d), lambda b, j: (j, 0)),
                  pl.BlockSpec((ROPE_DIM, blk), lambda b, j: (0, j)),
                  pl.BlockSpec((ROPE_DIM, blk), lambda b, j: (0, j))],
        out_specs=[pl.BlockSpec((1, hh, 1, 2 * d, blk), lambda b, j: (b, 0, j, 0, 0)),
                   pl.BlockSpec((1, blk, w), lambda b, j: (b, j, 0)),
                   pl.BlockSpec((1, hh, 1, d, blk), lambda b, j: (b, 0, j // MOBA_GROUP, 0, j % MOBA_GROUP))],
        out_shape=[jax.ShapeDtypeStruct((nbatch, hh, nblk, 2 * d, blk), BF16),
                   jax.ShapeDtypeStruct((nbatch, seq, w), BF16),
                   jax.ShapeDtypeStruct((nbatch, hh, nblk // MOBA_GROUP, d, MOBA_GROUP * blk), BF16)],
        scratch_shapes=[pltpu.VMEM(s, dt) for s, dt in scratch],
        compiler_params=_params(("arbitrary", "arbitrary"), blocks, scratch),
        name="moba_prep",
    )(z3, z3, z3, cos_tab, sin_tab, cos_t, sin_t)


def _moba_attn_t_kernel(q_ref, k_ref, mask_ref, vt_ref, vown_ref, o_ref, s_sc):
    blk = q_ref.shape[-1]
    d = HEAD_DIM
    gkeys = vt_ref.shape[-1]
    group = gkeys // blk
    last_group = vt_ref.shape[2] - 1
    j = pl.program_id(2)
    c_exp = (d ** -0.5) * np.log2(np.e).astype(np.float32)
    q_aug = q_ref[0, 0, 0]

    def group_scores(g, slot):
        c0 = pl.multiple_of(g * gkeys, gkeys)
        k_aug = jnp.concatenate([k_ref[0, pl.ds(c0, gkeys), :], mask_ref[pl.ds(c0, gkeys), :]], axis=1)
        s_t = jnp.dot(k_aug, q_aug, preferred_element_type=F32)
        s_sc[slot] = s_t
        return jnp.max(s_t, axis=0, keepdims=True)

    r0 = pl.multiple_of(j * blk, blk)
    key = lax.broadcasted_iota(jnp.int32, (blk, blk), 0)
    qry = lax.broadcasted_iota(jnp.int32, (blk, blk), 1)
    s_own = jnp.dot(k_ref[0, pl.ds(r0, blk), :], q_aug[:d], preferred_element_type=F32)
    s_own = jnp.where(key <= qry, s_own, -jnp.inf)
    m0 = jnp.max(s_own, axis=0, keepdims=True)
    p0 = jnp.exp2((s_own - m0) * c_exp)
    l0 = jnp.sum(p0, axis=0, keepdims=True)
    acc0 = jnp.dot(vown_ref[0, 0, 0], p0.astype(BF16), preferred_element_type=F32)

    def fold_group(g, state, slot):
        m_run, l_run, acc, s_max = state
        next_max = group_scores(jnp.minimum(g + 1, last_group), 1 - slot)
        m_new = jnp.maximum(m_run, s_max)
        alpha = jnp.exp2((m_run - m_new) * c_exp)
        p = jnp.exp2((s_sc[slot] - m_new) * c_exp)
        l_new = alpha * l_run + jnp.sum(p, axis=0, keepdims=True)
        acc = alpha * acc + jnp.dot(vt_ref[0, 0, g], p.astype(BF16), preferred_element_type=F32)
        return m_new, l_new, acc, next_max

    def group_body(g, state):
        return lax.cond(g % 2 == 0, lambda st: fold_group(g, st, 0), lambda st: fold_group(g, st, 1), state)

    ngroups = lax.div(j + (group - 1), group)
    _, l_fin, acc, _ = lax.fori_loop(0, ngroups, group_body, (m0, l0, acc0, group_scores(0, 0)))
    o_ref[0] = (acc / l_fin).T.astype(o_ref.dtype)


def _moba_mask_columns(seq):
    blk_id = jnp.arange(seq, dtype=jnp.int32)[:, None] // MOBA_BLOCK
    return jnp.where(jnp.arange(LANES, dtype=jnp.int32)[None, :] == blk_id, MASK_BIG, 0.0).astype(BF16)


def _moba_attn_t(q_aug, kb, vt, mask_cols):
    nbatch, seq, w = kb.shape
    blk = MOBA_BLOCK
    d = HEAD_DIM
    ngrp, gkeys = vt.shape[2], vt.shape[4]
    group = gkeys // blk
    blocks = [((2 * d, blk), BF16), ((1, seq, d), BF16), ((seq, LANES), BF16), ((ngrp, d, gkeys), BF16),
              ((d, blk), BF16), ((1, blk, d), BF16)]
    scratch = [((2, gkeys, blk), F32)]
    return pl.pallas_call(
        _moba_attn_t_kernel,
        grid=(nbatch, MOBA_HEADS, seq // blk),
        in_specs=[pl.BlockSpec((1, 1, 1, 2 * d, blk), lambda b, h, j: (b, h, j, 0, 0)),
                  pl.BlockSpec((1, seq, d), lambda b, h, j: (b, 0, h)),
                  pl.BlockSpec((seq, LANES), lambda b, h, j: (0, 0)),
                  pl.BlockSpec((1, 1, ngrp, d, gkeys), lambda b, h, j: (b, h, 0, 0, 0)),
                  pl.BlockSpec((1, 1, 1, d, blk), lambda b, h, j: (b, h, j // group, 0, j % group))],
        out_specs=pl.BlockSpec((1, blk, d), lambda b, h, j: (b, j, h)),
        out_shape=jax.ShapeDtypeStruct((nbatch, seq, w), BF16),
        scratch_shapes=[pltpu.VMEM(s, dt) for s, dt in scratch],
        compiler_params=_params(("arbitrary", "arbitrary", "arbitrary"), blocks, scratch),
        name="moba_attn",
    )(q_aug, kb, mask_cols, vt, vt)


def _gelu_tanh(x):
    c = np.sqrt(2.0 / np.pi).astype(np.float32)
    return x * (0.5 * (1.0 + jnp.tanh(c * (x + 0.044715 * (x * x * x)))))


def _gmlp_kernel(u_ref, v_ref, lg_ref, lb_ref, ws_ref, bst_ref, y_ref):
    rows = u_ref.shape[1]
    t = GMLP_CHUNK
    gd = GMLP_WIDTH // GMLP_GROUPS
    v = _gelu_tanh(v_ref[0])
    mu = jnp.mean(v, axis=-1, keepdims=True)
    vc = v - mu
    vln = vc * lax.rsqrt(jnp.mean(vc * vc, axis=-1, keepdims=True) + NORM_EPS) * lg_ref[...] + lb_ref[...]
    vb = vln.astype(BF16)
    row = lax.broadcasted_iota(jnp.int32, (t, t), 0)
    col = lax.broadcasted_iota(jnp.int32, (t, t), 1)
    for g in range(GMLP_GROUPS):
        wg = jnp.where(col <= row, ws_ref[g], 0.0).astype(BF16)
        bias = bst_ref[:, g:g + 1]
        cols = slice(g * gd, (g + 1) * gd)
        for c in range(rows // t):
            rs = slice(c * t, (c + 1) * t)
            mixed = jnp.dot(wg, vb[rs, cols], preferred_element_type=F32) + bias
            y_ref[0, rs, cols] = (_gelu_tanh(u_ref[0, rs, cols]) * mixed).astype(y_ref.dtype)


def _gmlp(z3, u_cb, v_cb, ln_g, ln_b, ws, bs, rows=512):
    nbatch, seq, _ = z3.shape
    w = GMLP_WIDTH
    t = GMLP_CHUNK
    blocks = [((1, rows, w), F32), ((1, rows, w), F32), ((GMLP_GROUPS, t, t), F32), ((1, rows, w), BF16)]
    return pl.pallas_call(
        _gmlp_kernel,
        grid=(nbatch, seq // rows),
        in_specs=[pl.BlockSpec((1, rows, w), lambda b, c: (b, c, u_cb)),
                  pl.BlockSpec((1, rows, w), lambda b, c: (b, c, v_cb)),
                  pl.BlockSpec((1, w), lambda b, c: (0, 0)),
                  pl.BlockSpec((1, w), lambda b, c: (0, 0)),
                  pl.BlockSpec((GMLP_GROUPS, t, t), lambda b, c: (0, 0, 0)),
                  pl.BlockSpec((t, GMLP_GROUPS), lambda b, c: (0, 0))],
        out_specs=pl.BlockSpec((1, rows, w), lambda b, c: (b, c, 0)),
        out_shape=jax.ShapeDtypeStruct((nbatch, seq, w), BF16),
        compiler_params=_params(("arbitrary", "arbitrary"), blocks),
        name="gmlp",
    )(z3, z3, ln_g.reshape(1, w), ln_b.reshape(1, w), ws, bs.T)


def _layer_weight_spec(layer, k, tn, index_of):
    return pl.BlockSpec((None, k, tn), lambda *idx: (layer,) + index_of(*idx))


def _merge_kernel(h_ref, wga_ref, wgb_ref, wgc_ref, ya_ref, yb_ref, yc_ref, wa_ref, wb_ref, wc_ref, o_ref):
    h = h_ref[...]

    def branch(wg_ref, y_ref, w_ref):
        gate = jax.nn.sigmoid(jnp.dot(h, wg_ref[...], preferred_element_type=F32))
        return gate * jnp.dot(y_ref[...], w_ref[...].astype(BF16), preferred_element_type=F32)

    merged = branch(wga_ref, ya_ref, wa_ref) + branch(wgb_ref, yb_ref, wb_ref) + branch(wgc_ref, yc_ref, wc_ref)
    o_ref[...] = merged.astype(o_ref.dtype)


def _merge(h, w_gate, ya, yb, yc, wa, wb, wc, layer, tm=1024, tn=512):
    m, d = h.shape
    nblk = d // tn
    ka, kb, kc = ya.shape[1], yb.shape[1], yc.shape[1]
    blocks = ([((tm, d), BF16)] + [((d, tn), BF16)] * 3
              + [((tm, ka), BF16), ((tm, kb), BF16), ((tm, kc), BF16)]
              + [((ka, tn), F32), ((kb, tn), F32), ((kc, tn), F32), ((tm, tn), BF16)])

    def gate_spec(branch):
        return pl.BlockSpec((d, tn), lambda i, j: (0, branch * nblk + j))

    def col(i, j):
        return (0, j)

    return pl.pallas_call(
        _merge_kernel,
        grid=(m // tm, nblk),
        in_specs=[pl.BlockSpec((tm, d), lambda i, j: (i, 0)),
                  gate_spec(0), gate_spec(1), gate_spec(2),
                  pl.BlockSpec((tm, ka), lambda i, j: (i, 0)),
                  pl.BlockSpec((tm, kb), lambda i, j: (i, 0)),
                  pl.BlockSpec((tm, kc), lambda i, j: (i, 0)),
                  _layer_weight_spec(layer, ka, tn, col),
                  _layer_weight_spec(layer, kb, tn, col),
                  _layer_weight_spec(layer, kc, tn, col)],
        out_specs=pl.BlockSpec((tm, tn), lambda i, j: (i, j)),
        out_shape=jax.ShapeDtypeStruct((m, d), BF16),
        compiler_params=_params(("arbitrary", "arbitrary"), blocks),
        name="merge",
    )(h, w_gate, w_gate, w_gate, ya, yb, yc, wa, wb, wc)


def _residual_matmul_kernel(a_ref, w_ref, r_ref, o_ref):
    @pl.when(pl.program_id(2) == 0)
    def _():
        o_ref[...] = r_ref[...]

    o_ref[...] += jnp.dot(a_ref[...], w_ref[...].astype(BF16), preferred_element_type=F32)


def _residual_matmul(a, w, layer, res, tm=1024, tn=1024, tk=1024):
    m, k = a.shape
    n = w.shape[2]
    blocks = [((tm, tk), BF16), ((tk, tn), F32), ((tm, tn), F32), ((tm, tn), F32)]
    return pl.pallas_call(
        _residual_matmul_kernel,
        grid=(m // tm, n // tn, k // tk),
        in_specs=[pl.BlockSpec((tm, tk), lambda i, j, kk: (i, kk)),
                  _layer_weight_spec(layer, tk, tn, lambda i, j, kk: (kk, j)),
                  pl.BlockSpec((tm, tn), lambda i, j, kk: (i, j))],
        out_specs=pl.BlockSpec((tm, tn), lambda i, j, kk: (i, j)),
        out_shape=jax.ShapeDtypeStruct((m, n), F32),
        compiler_params=_params(("arbitrary", "arbitrary", "arbitrary"), blocks),
        name="residual_matmul",
    )(a, w, res)


def _relu2_matmul_kernel(a_ref, w_ref, o_ref):
    up = jnp.maximum(jnp.dot(a_ref[...], w_ref[...].astype(BF16), preferred_element_type=F32), 0.0)
    o_ref[...] = (up * up).astype(o_ref.dtype)


def _relu2_matmul(a, w, layer, tm=1024, tn=1024):
    m, k = a.shape
    n = w.shape[2]
    blocks = [((tm, k), BF16), ((k, tn), F32), ((tm, tn), BF16)]
    return pl.pallas_call(
        _relu2_matmul_kernel,
        grid=(m // tm, n // tn),
        in_specs=[pl.BlockSpec((tm, k), lambda i, j: (i, 0)),
                  _layer_weight_spec(layer, k, tn, lambda i, j: (0, j))],
        out_specs=pl.BlockSpec((tm, tn), lambda i, j: (i, j)),
        out_shape=jax.ShapeDtypeStruct((m, n), BF16),
        compiler_params=_params(("arbitrary", "arbitrary"), blocks),
        name="mlp_up",
    )(a, w)


def _ple_kernel(h_ref, wg_ref, p_ref, wp_ref, r_ref, o_ref):
    gate = jax.nn.sigmoid(jnp.dot(h_ref[...], wg_ref[...].astype(BF16), preferred_element_type=F32))
    emb = jnp.dot(p_ref[...].astype(BF16), wp_ref[...].astype(BF16), preferred_element_type=F32)
    o_ref[...] = r_ref[...] + gate * emb


def _ple(h, w_gate, p, w_proj, layer, res, tm=1024, tn=512):
    m, d = h.shape
    pd = p.shape[2]
    blocks = [((tm, d), BF16), ((d, tn), F32), ((tm, pd), F32), ((pd, tn), F32),
              ((tm, tn), F32), ((tm, tn), F32)]
    return pl.pallas_call(
        _ple_kernel,
        grid=(m // tm, d // tn),
        in_specs=[pl.BlockSpec((tm, d), lambda i, j: (i, 0)),
                  _layer_weight_spec(layer, d, tn, lambda i, j: (0, j)),
                  pl.BlockSpec((None, tm, pd), lambda i, j: (layer, i, 0)),
                  _layer_weight_spec(layer, pd, tn, lambda i, j: (0, j)),
                  pl.BlockSpec((tm, tn), lambda i, j: (i, j))],
        out_specs=pl.BlockSpec((tm, tn), lambda i, j: (i, j)),
        out_shape=jax.ShapeDtypeStruct((m, d), F32),
        compiler_params=_params(("arbitrary", "arbitrary"), blocks),
        name="ple",
    )(h, w_gate, p, w_proj, res)


def kernel(x, p, norm_mix_g, w_in, mlstm_gate_b, mlstm_norm_g, gmlp_norm_g, gmlp_norm_b, gmlp_ws, gmlp_bs,
           w_branch_a, w_branch_b, w_branch_c, w_out, norm_mlp_g, w_mlp_up, w_mlp_down, norm_ple_g,
           w_ple_gate, w_ple_proj, final_norm_g):
    nbatch, seq, d = x.shape
    depth = w_in.shape[0]
    m = nbatch * seq
    assert d == MLSTM_WIDTH + MOBA_WIDTH + GMLP_WIDTH

    qkvo_a = 4 * MLSTM_WIDTH
    gates_if = 2 * MLSTM_HEADS
    main_b = qkvo_a + gates_if
    main_cols = 3 * MOBA_WIDTH + 2 * GMLP_WIDTH
    gate_off = main_b + main_cols
    assert w_in.shape[2] == gate_off + N_BRANCHES * d

    moba_q_cb = qkvo_a // MOBA_WIDTH
    moba_k_cb = moba_q_cb + 1
    moba_v_cb = moba_k_cb + 1
    gmlp_u_cb = (qkvo_a + 3 * MOBA_WIDTH) // GMLP_WIDTH
    gmlp_v_cb = gmlp_u_cb + 1

    z_cols = qkvo_a + main_cols
    colscale = jnp.ones((1, z_cols), F32).at[:, MLSTM_WIDTH:2 * MLSTM_WIDTH].set(HEAD_DIM ** -0.5)
    rope_tables = _rope_tables(seq) + _rope_tables_t(seq)
    mask_cols = _moba_mask_columns(seq)

    xf = x.reshape(m, d)
    p_flat = p.reshape(depth, m, p.shape[-1])
    for i in range(depth):
        w_i = w_in[i]
        w_main = jnp.concatenate([w_i[:, :qkvo_a], w_i[:, main_b:gate_off]], axis=1).astype(BF16)
        w_if = jnp.pad(w_i[:, qkvo_a:main_b], ((0, 0), (0, GATE_LANES - gates_if))).astype(BF16)
        w_gate = w_i[:, gate_off:].astype(BF16)

        h = _rmsnorm(xf, norm_mix_g[i], BF16)
        z, zif = _inproj(h, w_main, w_if, colscale)
        z3 = z.reshape(nbatch, seq, z_cols)
        zif3 = zif.reshape(nbatch, seq, GATE_LANES)

        ya = _mlstm(z3, zif3, mlstm_gate_b[i], mlstm_norm_g[i])
        yb = _moba_attn_t(*_moba_prep_t(z3, rope_tables, moba_q_cb, moba_k_cb, moba_v_cb), mask_cols)
        yc = _gmlp(z3, gmlp_u_cb, gmlp_v_cb, gmlp_norm_g[i], gmlp_norm_b[i], gmlp_ws[i], gmlp_bs[i])

        merged = _merge(h, w_gate, ya.reshape(m, -1), yb.reshape(m, -1), yc.reshape(m, -1),
                        w_branch_a, w_branch_b, w_branch_c, i)
        xf = _residual_matmul(merged, w_out, i, xf)

        h2 = _rmsnorm(xf, norm_mlp_g[i], BF16)
        hidden = _relu2_matmul(h2, w_mlp_up, i)
        xf = _residual_matmul(hidden, w_mlp_down, i, xf)

        h3 = _rmsnorm(xf, norm_ple_g[i], BF16)
        xf = _ple(h3, w_ple_gate, p_flat, w_ple_proj, i, xf)

    return _rmsnorm(xf, final_norm_g, F32).reshape(nbatch, seq, d)
```

```python
import functools

import jax
import jax.numpy as jnp
import numpy as np
from jax import lax
from jax.experimental import pallas as pl
from jax.experimental.pallas import tpu as pltpu

F32 = jnp.float32
BF16 = jnp.bfloat16

HEAD_DIM = 128
MLSTM_HEADS = 4
MLSTM_WIDTH = MLSTM_HEADS * HEAD_DIM
MOBA_HEADS = 8
MOBA_WIDTH = MOBA_HEADS * HEAD_DIM
MOBA_BLOCK = 256
MOBA_TOPK = 3
ROPE_THETA = 500000.0
ROPE_DIM = HEAD_DIM // 4
GMLP_WIDTH = 512
GMLP_GROUPS = 4
GMLP_CHUNK = 128
N_BRANCHES = 3
NORM_EPS = 1e-6

LANES = 128
V7X_VMEM_BYTES = 64 * 1024 * 1024
VMEM_CEILING = V7X_VMEM_BYTES - 8 * 1024 * 1024

MLSTM_KERNEL_CHUNK = 128
GATE_LANES = LANES
MASK_BIG = 2.0 ** 100
MOBA_GROUP = 4

NT_DIMS = (((1,), (1,)), ((), ()))
TN_DIMS = (((0,), (0,)), ((), ()))


def _tiles(n, t):
    count, rest = divmod(n, t)
    assert rest == 0 and count > 0, (n, t)
    return count


def _nbytes(shape, dtype):
    return int(np.prod(shape)) * jnp.dtype(dtype).itemsize


def _params(semantics, blocks, scratch=()):
    need = 2 * sum(_nbytes(s, d) for s, d in blocks) + sum(_nbytes(s, d) for s, d in scratch)
    limit = min(VMEM_CEILING, need + need // 4 + 4 * 1024 * 1024)
    return pltpu.CompilerParams(dimension_semantics=semantics, vmem_limit_bytes=limit)


def _rmsnorm_kernel(x_ref, g_ref, o_ref):
    x = x_ref[...]
    y = x * lax.rsqrt(jnp.mean(x * x, axis=-1, keepdims=True) + NORM_EPS)
    o_ref[...] = (y * g_ref[...]).astype(o_ref.dtype)


def _rmsnorm(x, g, out_dtype, tm=512):
    m, d = x.shape
    return pl.pallas_call(
        _rmsnorm_kernel,
        grid=(_tiles(m, tm),),
        in_specs=[pl.BlockSpec((tm, d), lambda i: (i, 0)),
                  pl.BlockSpec((1, d), lambda i: (0, 0))],
        out_specs=pl.BlockSpec((tm, d), lambda i: (i, 0)),
        out_shape=jax.ShapeDtypeStruct((m, d), out_dtype),
        compiler_params=_params(("arbitrary",), [((tm, d), F32), ((tm, d), out_dtype)]),
        name="rmsnorm",
    )(x, g.reshape(1, d))


def _win_split_kernel(w_ref, main_ref, if_ref, gate_ref, *, head, n_if, main_cols):
    rows = w_ref.shape[0]
    tail = main_cols - head
    main_ref[:, :head] = w_ref[:, :head].astype(BF16)
    main_ref[:, head:] = w_ref[:, head + n_if:head + n_if + tail].astype(BF16)
    lane = lax.broadcasted_iota(jnp.int32, (rows, GATE_LANES), 1)
    if_ref[...] = jnp.where(lane < n_if, w_ref[:, head:head + GATE_LANES], 0.0).astype(BF16)
    gate_ref[...] = w_ref[:, head + n_if + tail:].astype(BF16)


def _win_split(w_in, head, n_if, main_cols, rows=128):
    depth, k, cols = w_in.shape
    gate_cols = cols - main_cols - n_if
    blocks = [((rows, cols), F32), ((rows, main_cols), BF16), ((rows, GATE_LANES), BF16),
              ((rows, gate_cols), BF16)]

    def out_spec(width):
        return pl.BlockSpec((None, rows, width), lambda l, r: (l, r, 0))

    return pl.pallas_call(
        functools.partial(_win_split_kernel, head=head, n_if=n_if, main_cols=main_cols),
        grid=(depth, _tiles(k, rows)),
        in_specs=[pl.BlockSpec((None, rows, cols), lambda l, r: (l, r, 0))],
        out_specs=[out_spec(main_cols), out_spec(GATE_LANES), out_spec(gate_cols)],
        out_shape=[jax.ShapeDtypeStruct((depth, k, main_cols), BF16),
                   jax.ShapeDtypeStruct((depth, k, GATE_LANES), BF16),
                   jax.ShapeDtypeStruct((depth, k, gate_cols), BF16)],
        compiler_params=_params(("arbitrary", "arbitrary"), blocks),
        name="win_split",
    )(w_in)


def _inproj_kernel(h_ref, w_ref, wif_ref, cs_ref, z_ref, zif_ref):
    acc = jnp.dot(h_ref[...], w_ref[...], preferred_element_type=F32)
    z_ref[...] = acc * cs_ref[...]

    @pl.when(pl.program_id(1) == 0)
    def _():
        zif_ref[...] = jnp.dot(h_ref[...], wif_ref[...], preferred_element_type=F32)


def _inproj(h, w_main, w_if, layer, colscale, tm=1024, tn=512):
    m, k = h.shape
    n = w_main.shape[2]
    blocks = [((tm, k), BF16), ((k, tn), BF16), ((k, GATE_LANES), BF16), ((1, tn), F32),
              ((tm, tn), F32), ((tm, GATE_LANES), F32)]
    return pl.pallas_call(
        _inproj_kernel,
        grid=(_tiles(m, tm), _tiles(n, tn)),
        in_specs=[pl.BlockSpec((tm, k), lambda i, j: (i, 0)),
                  pl.BlockSpec((None, k, tn), lambda i, j: (layer, 0, j)),
                  pl.BlockSpec((None, k, GATE_LANES), lambda i, j: (layer, 0, 0)),
                  pl.BlockSpec((1, tn), lambda i, j: (0, j))],
        out_specs=[pl.BlockSpec((tm, tn), lambda i, j: (i, j)),
                   pl.BlockSpec((tm, GATE_LANES), lambda i, j: (i, 0))],
        out_shape=[jax.ShapeDtypeStruct((m, n), F32),
                   jax.ShapeDtypeStruct((m, GATE_LANES), F32)],
        compiler_params=_params(("arbitrary", "arbitrary"), blocks),
        name="in_proj",
    )(h, w_main, w_if, colscale)


def _log_sigmoid(x):
    return jnp.minimum(x, 0.0) - jnp.log1p(jnp.exp(-jnp.abs(x)))


def _mlstm_kernel(q_ref, k_ref, v_ref, o_ref, zif_ref, gb_ref, ng_ref, y_ref, c_sc, n_sc, m_sc):
    nbatch, chunk, _ = q_ref.shape
    heads = MLSTM_HEADS

    @pl.when(pl.program_id(0) == 0)
    def _():
        c_sc[...] = jnp.zeros_like(c_sc)
        n_sc[...] = jnp.zeros_like(n_sc)
        m_sc[...] = jnp.zeros_like(m_sc)

    row = lax.broadcasted_iota(jnp.int32, (chunk, chunk), 0)
    col = lax.broadcasted_iota(jnp.int32, (chunk, chunk), 1)
    causal = col <= row
    tril = jnp.where(causal, 1.0, 0.0).astype(F32)
    lane = lax.broadcasted_iota(jnp.int32, (chunk, GATE_LANES), 1)

    for b in range(nbatch):
        pre = zif_ref[b] + gb_ref[...]
        gates = jnp.where(lane < heads, pre, _log_sigmoid(pre))
        gcum = jnp.dot(tril, gates, preferred_element_type=F32,
                       precision=lax.Precision.HIGHEST)
        gates_t = gates.T
        gcum_t = gcum.T
        for h in range(heads):
            s = b * heads + h
            sl = slice(h * HEAD_DIM, (h + 1) * HEAD_DIM)
            q = q_ref[b, :, sl]
            k = k_ref[b, :, sl]
            v = v_ref[b, :, sl]
            qb = q.astype(BF16)
            kb = k.astype(BF16)
            vb = v.astype(BF16)
            g_t = gcum[:, heads + h:heads + h + 1]
            g_s = gcum_t[heads + h:heads + h + 1, :]
            i_s = gates_t[h:h + 1, :]
            i_t = gates[:, h:h + 1]
            m_prev = m_sc[s][:, 0:1]
            c_prev = c_sc[s]
            n_prev = n_sc[s]

            log_w = jnp.where(causal, g_t - g_s + i_s, -jnp.inf)
            log_a = g_t + m_prev
            m_row = jnp.maximum(jnp.max(log_w, axis=1, keepdims=True), log_a)
            qk = lax.dot_general(qb, kb, NT_DIMS, preferred_element_type=F32) * jnp.exp(log_w - m_row)
            a = jnp.exp(log_a - m_row)
            num = (jnp.dot(qk.astype(BF16), vb, preferred_element_type=F32)
                   + a * lax.dot_general(qb, c_prev.astype(BF16), NT_DIMS, preferred_element_type=F32))
            den = (jnp.sum(qk, axis=1, keepdims=True)
                   + a * jnp.sum(q * n_prev, axis=1, keepdims=True))
            h_out = num / jnp.maximum(jnp.abs(den), jnp.exp(-m_row))

            g_last = gcum[chunk - 1:chunk, heads + h:heads + h + 1]
            log_u = g_last - g_t + i_t
            m_new = jnp.maximum(g_last + m_prev, jnp.max(log_u, axis=0, keepdims=True))
            decay = jnp.exp(g_last + m_prev - m_new)
            u = jnp.exp(log_u - m_new)
            c_sc[s] = decay * c_prev + lax.dot_general((u * v).astype(BF16), kb, TN_DIMS,
                                                       preferred_element_type=F32)
            n_sc[s] = decay * n_prev + jnp.sum(u * k, axis=0, keepdims=True)
            m_sc[s] = jnp.broadcast_to(m_new, (1, LANES))

            yn = h_out * lax.rsqrt(jnp.mean(h_out * h_out, axis=-1, keepdims=True) + NORM_EPS)
            y_ref[b, :, sl] = (jax.nn.sigmoid(o_ref[b, :, sl]) * (yn * ng_ref[:, sl])).astype(y_ref.dtype)


def _mlstm(z3, zif3, gate_b, norm_g):
    nbatch, seq, _ = z3.shape
    chunk = MLSTM_KERNEL_CHUNK
    w = MLSTM_WIDTH
    streams = nbatch * MLSTM_HEADS
    gb = jnp.pad(gate_b, (0, GATE_LANES - gate_b.shape[0])).reshape(1, GATE_LANES)
    blocks = [((nbatch, chunk, w), F32)] * 4 + [((nbatch, chunk, GATE_LANES), F32),
                                               ((nbatch, chunk, w), BF16)]
    scratch = [((streams, HEAD_DIM, HEAD_DIM), F32), ((streams, 1, LANES), F32), ((streams, 1, LANES), F32)]

    def zcol(cb):
        return pl.BlockSpec((nbatch, chunk, w), lambda c: (0, c, cb))

    return pl.pallas_call(
        _mlstm_kernel,
        grid=(_tiles(seq, chunk),),
        in_specs=[zcol(0), zcol(1), zcol(2), zcol(3),
                  pl.BlockSpec((nbatch, chunk, GATE_LANES), lambda c: (0, c, 0)),
                  pl.BlockSpec((1, GATE_LANES), lambda c: (0, 0)),
                  pl.BlockSpec((1, w), lambda c: (0, 0))],
        out_specs=pl.BlockSpec((nbatch, chunk, w), lambda c: (0, c, 0)),
        out_shape=jax.ShapeDtypeStruct((nbatch, seq, w), BF16),
        scratch_shapes=[pltpu.VMEM(s, d) for s, d in scratch],
        compiler_params=_params(("arbitrary",), blocks, scratch),
        name="mlstm",
    )(z3, z3, z3, z3, zif3, gb, norm_g.reshape(1, w))


def _rope_tables(seq):
    half = ROPE_DIM // 2
    inv_freq = ROPE_THETA ** (-jnp.arange(0, ROPE_DIM, 2, dtype=F32) / ROPE_DIM)
    ang = jnp.arange(seq, dtype=F32)[:, None] * inv_freq[None, :]
    cos = jnp.cos(ang)
    sin = jnp.sin(ang)
    ones = jnp.ones((seq, HEAD_DIM - ROPE_DIM), F32)
    cos_tab = jnp.concatenate([cos, cos, ones], axis=1)
    sin_tab = jnp.concatenate([-sin, sin, 0.0 * ones], axis=1)
    assert cos_tab.shape == (seq, HEAD_DIM) and half * 2 == ROPE_DIM
    return cos_tab, sin_tab


def _rotary(t, cos, sin):
    half = ROPE_DIM // 2
    lane = lax.broadcasted_iota(jnp.int32, t.shape, 1)
    upper = pltpu.roll(t, HEAD_DIM - half, axis=1)
    lower = pltpu.roll(t, half, axis=1)
    partner = jnp.where(lane < half, upper, lower)
    return jnp.where(lane < ROPE_DIM, t * cos + partner * sin, t)


def _moba_prep_kernel(q_ref, k_ref, v_ref, cos_ref, sin_ref, qa_ref, kb_ref, vb_ref, km_sc):
    blk = q_ref.shape[1]
    d = HEAD_DIM
    j = pl.program_id(1)

    @pl.when(j == 0)
    def _():
        km_sc[...] = jnp.zeros_like(km_sc)

    cos = cos_ref[...]
    sin = sin_ref[...]
    lane = lax.broadcasted_iota(jnp.int32, (blk, LANES), 1)
    lane_f = lane.astype(F32)
    mean_row = lax.broadcasted_iota(jnp.int32, (LANES, d), 0)
    vb_ref[0] = v_ref[0].astype(BF16)
    for h in range(MOBA_HEADS):
        sl = slice(h * d, (h + 1) * d)
        qf = _rotary(q_ref[0, :, sl], cos, sin)
        kk = _rotary(k_ref[0, :, sl], cos, sin)
        kb_ref[0, :, sl] = kk.astype(BF16)

        gate = lax.dot_general(qf, km_sc[h], NT_DIMS, preferred_element_type=F32,
                               precision=lax.Precision.HIGHEST)
        gate = jnp.where(lane < j, gate, -jnp.inf)
        sel_m1 = jnp.full((blk, LANES), -1.0, F32)
        for _ in range(MOBA_TOPK):
            mx = jnp.max(gate, axis=1, keepdims=True)
            first = jnp.min(jnp.where(gate == mx, lane_f, float(LANES)), axis=1, keepdims=True)
            first = jnp.where(mx > -jnp.inf, first, -1.0)
            hit = lane_f == first
            sel_m1 = jnp.where(hit, 0.0, sel_m1)
            gate = jnp.where(hit, -jnp.inf, gate)
        qa_ref[0, :, 2 * h * d:(2 * h + 1) * d] = qf.astype(BF16)
        qa_ref[0, :, (2 * h + 1) * d:(2 * h + 2) * d] = sel_m1.astype(BF16)

        km_sc[h] = jnp.where(mean_row == j, jnp.mean(kk, axis=0, keepdims=True), km_sc[h])


def _moba_prep(z3, cos_tab, sin_tab, q_cb, k_cb, v_cb):
    nbatch, seq, _ = z3.shape
    blk = MOBA_BLOCK
    w = MOBA_WIDTH
    d = HEAD_DIM
    blocks = [((1, blk, w), F32)] * 3 + [((blk, d), F32)] * 2 + [((1, blk, 2 * w), BF16), ((1, blk, w), BF16),
                                                                  ((1, blk, w), BF16)]
    scratch = [((MOBA_HEADS, LANES, d), F32)]
    return pl.pallas_call(
        _moba_prep_kernel,
        grid=(nbatch, _tiles(seq, blk)),
        in_specs=[pl.BlockSpec((1, blk, w), lambda b, j: (b, j, q_cb)),
                  pl.BlockSpec((1, blk, w), lambda b, j: (b, j, k_cb)),
                  pl.BlockSpec((1, blk, w), lambda b, j: (b, j, v_cb)),
                  pl.BlockSpec((blk, d), lambda b, j: (j, 0)),
                  pl.BlockSpec((blk, d), lambda b, j: (j, 0))],
        out_specs=[pl.BlockSpec((1, blk, 2 * w), lambda b, j: (b, j, 0)),
                   pl.BlockSpec((1, blk, w), lambda b, j: (b, j, 0)),
                   pl.BlockSpec((1, blk, w), lambda b, j: (b, j, 0))],
        out_shape=[jax.ShapeDtypeStruct((nbatch, seq, 2 * w), BF16),
                   jax.ShapeDtypeStruct((nbatch, seq, w), BF16),
                   jax.ShapeDtypeStruct((nbatch, seq, w), BF16)],
        scratch_shapes=[pltpu.VMEM(s, dt) for s, dt in scratch],
        compiler_params=_params(("arbitrary", "arbitrary"), blocks, scratch),
        name="moba_prep",
    )(z3, z3, z3, cos_tab, sin_tab)


def _moba_attn_kernel(q_ref, k_ref, v_ref, o_ref):
    blk = q_ref.shape[1]
    d = HEAD_DIM
    nblk = k_ref.shape[1] // blk
    group = MOBA_GROUP
    j = pl.program_id(2)
    scale = d ** -0.5
    q_aug = q_ref[0]
    lane = lax.broadcasted_iota(jnp.int32, (blk, LANES), 1)

    def softmax_partial(s, v_rows):
        m = jnp.max(s, axis=1, keepdims=True)
        p = jnp.exp(s - m)
        return m, jnp.sum(p, axis=1, keepdims=True), jnp.dot(p.astype(BF16), v_rows, preferred_element_type=F32)

    def past_partial(n):
        c0 = pl.multiple_of(jnp.minimum(n, nblk - 1) * blk, blk)
        onehot = jnp.where(lane == n, MASK_BIG, 0.0).astype(BF16)
        k_aug = jnp.concatenate([k_ref[0, pl.ds(c0, blk), :], onehot], axis=1)
        s = lax.dot_general(q_aug, k_aug, NT_DIMS, preferred_element_type=F32) * scale
        return softmax_partial(s, v_ref[0, pl.ds(c0, blk), :])

    def merge(state, parts):
        m_run, l_run, acc_run = state
        m_new = m_run
        for m, _, _ in parts:
            m_new = jnp.maximum(m_new, m)
        w = jnp.exp(m_run - m_new)
        l_new = w * l_run
        acc = w * acc_run
        for m, l, a in parts:
            w = jnp.exp(m - m_new)
            l_new = l_new + w * l
            acc = acc + w * a
        return m_new, l_new, acc

    r0 = pl.multiple_of(j * blk, blk)
    row = lax.broadcasted_iota(jnp.int32, (blk, blk), 0)
    col = lax.broadcasted_iota(jnp.int32, (blk, blk), 1)
    s_own = lax.dot_general(q_aug[:, :d], k_ref[0, pl.ds(r0, blk), :], NT_DIMS,
                            preferred_element_type=F32) * scale
    s_own = jnp.where(col <= row, s_own, -jnp.inf)
    state = softmax_partial(s_own, v_ref[0, pl.ds(r0, blk), :])

    state = merge(state, [past_partial(n) for n in range(group)])
    ngroups = lax.div(j + (group - 1), group)

    def group_body(g, st):
        return merge(st, [past_partial(g * group + i) for i in range(group)])

    _, l_fin, acc = lax.fori_loop(1, ngroups, group_body, state)
    o_ref[0] = (acc / l_fin).astype(o_ref.dtype)


def _moba_attn(q_aug, kb, vb):
    nbatch, seq, w = kb.shape
    blk = MOBA_BLOCK
    d = HEAD_DIM
    blocks = [((1, blk, 2 * d), BF16), ((1, seq, d), BF16), ((1, seq, d), BF16), ((1, blk, d), BF16)]
    return pl.pallas_call(
        _moba_attn_kernel,
        grid=(nbatch, MOBA_HEADS, _tiles(seq, blk)),
        in_specs=[pl.BlockSpec((1, blk, 2 * d), lambda b, h, j: (b, j, h)),
                  pl.BlockSpec((1, seq, d), lambda b, h, j: (b, 0, h)),
                  pl.BlockSpec((1, seq, d), lambda b, h, j: (b, 0, h))],
        out_specs=pl.BlockSpec((1, blk, d), lambda b, h, j: (b, j, h)),
        out_shape=jax.ShapeDtypeStruct((nbatch, seq, w), BF16),
        compiler_params=_params(("arbitrary", "arbitrary", "arbitrary"), blocks),
        name="moba_attn",
    )(q_aug, kb, vb)


def _rope_tables_t(seq):
    inv_freq = ROPE_THETA ** (-jnp.arange(0, ROPE_DIM, 2, dtype=F32) / ROPE_DIM)
    ang = jnp.arange(seq, dtype=F32)[:, None] * inv_freq[None, :]
    cos = jnp.cos(ang).T
    sin = jnp.sin(ang).T
    return jnp.concatenate([cos, cos], axis=0), jnp.concatenate([-sin, sin], axis=0)


def _moba_prep_t_kernel(q_ref, k_ref, v_ref, cos_ref, sin_ref, cost_ref, sint_ref, qa_ref, kb_ref, vt_ref, km_sc):
    blk = q_ref.shape[1]
    d = HEAD_DIM
    half = ROPE_DIM // 2
    nsel = km_sc.shape[1]
    j = pl.program_id(1)

    @pl.when(j == 0)
    def _():
        km_sc[...] = jnp.zeros_like(km_sc)

    cos = cos_ref[...]
    sin = sin_ref[...]
    cos_t = cost_ref[...]
    sin_t = sint_ref[...]
    blk_id = lax.broadcasted_iota(jnp.int32, (nsel, blk), 0)
    blk_id_f = blk_id.astype(F32)
    mean_row = lax.broadcasted_iota(jnp.int32, (nsel, d), 0)
    for h in range(MOBA_HEADS):
        sl = slice(h * d, (h + 1) * d)
        q_t = q_ref[0, :, sl].T
        top = q_t[:ROPE_DIM]
        partner = jnp.concatenate([top[half:], top[:half]], axis=0)
        q_t = jnp.concatenate([top * cos_t + partner * sin_t, q_t[ROPE_DIM:]], axis=0)

        gate = jnp.dot(km_sc[h], q_t, preferred_element_type=F32,
                       precision=lax.Precision.HIGHEST)
        gate = jnp.where(blk_id < j, gate, -jnp.inf)
        sel_m1 = jnp.full((nsel, blk), -1.0, F32)
        for _ in range(MOBA_TOPK):
            mx = jnp.max(gate, axis=0, keepdims=True)
            first = jnp.min(jnp.where(gate == mx, blk_id_f, float(nsel)), axis=0, keepdims=True)
            first = jnp.where(mx > -jnp.inf, first, -1.0)
            hit = blk_id_f == first
            sel_m1 = jnp.where(hit, 0.0, sel_m1)
            gate = jnp.where(hit, -jnp.inf, gate)
        qa_ref[0, h, 0, 0:d, :] = q_t.astype(BF16)
        qa_ref[0, h, 0, d:d + nsel, :] = sel_m1.astype(BF16)
        qa_ref[0, h, 0, d + nsel:, :] = jnp.full((d - nsel, blk), -1.0, BF16)

        kk = _rotary(k_ref[0, :, sl], cos, sin)
        kb_ref[0, :, sl] = kk.astype(BF16)
        km_sc[h] = jnp.where(mean_row == j, jnp.mean(kk, axis=0, keepdims=True), km_sc[h])

        vt_ref[0, h, 0] = v_ref[0, :, sl].T.astype(BF16)


def _moba_prep_t(z3, tables, q_cb, k_cb, v_cb):
    nbatch, seq, _ = z3.shape
    blk = MOBA_BLOCK
    nblk = _tiles(seq, blk)
    nsel = -(-nblk // 16) * 16
    assert nsel <= HEAD_DIM
    w = MOBA_WIDTH
    d = HEAD_DIM
    hh = MOBA_HEADS
    cos_tab, sin_tab, cos_t, sin_t = tables
    blocks = ([((1, blk, w), F32)] * 3 + [((blk, d), F32)] * 2 + [((ROPE_DIM, blk), F32)] * 2
              + [((hh, 2 * d, blk), BF16), ((1, blk, w), BF16), ((hh, d, blk), BF16)])
    scratch = [((hh, nsel, d), F32)]
    return pl.pallas_call(
        _moba_prep_t_kernel,
        grid=(nbatch, nblk),
        in_specs=[pl.BlockSpec((1, blk, w), lambda b, j: (b, j, q_cb)),
                  pl.BlockSpec((1, blk, w), lambda b, j: (b, j, k_cb)),
                  pl.BlockSpec((1, blk, w), lambda b, j: (b, j, v_cb)),
                  pl.BlockSpec((blk, d), lambda b, j: (j, 0)),
                  pl.BlockSpec((blk, d), lambda b, j: (j, 0)),
                  pl.BlockSpec((ROPE_DIM, blk), lambda b, j: (0, j)),
                  pl.BlockSpec((ROPE_DIM, blk), lambda b, j: (0, j))],
        out_specs=[pl.BlockSpec((1, hh, 1, 2 * d, blk), lambda b, j: (b, 0, j, 0, 0)),
                   pl.BlockSpec((1, blk, w), lambda b, j: (b, j, 0)),
                   pl.BlockSpec((1, hh, 1, d, blk), lambda b, j: (b, 0, j // MOBA_GROUP, 0, j % MOBA_GROUP))],
        out_shape=[jax.ShapeDtypeStruct((nbatch, hh, nblk, 2 * d, blk), BF16),
                   jax.ShapeDtypeStruct((nbatch, seq, w), BF16),
                   jax.ShapeDtypeStruct((nbatch, hh, _tiles(nblk, MOBA_GROUP), d, MOBA_GROUP * blk), BF16)],
        scratch_shapes=[pltpu.VMEM(s, dt) for s, dt in scratch],
        compiler_params=_params(("arbitrary", "arbitrary"), blocks, scratch),
        name="moba_prep",
    )(z3, z3, z3, cos_tab, sin_tab, cos_t, sin_t)


def _moba_attn_t_kernel(q_ref, k_ref, mask_ref, vt_ref, vown_ref, o_ref, s_sc):
    blk = q_ref.shape[-1]
    d = HEAD_DIM
    gkeys = vt_ref.shape[-1]
    group = gkeys // blk
    last_group = vt_ref.shape[2] - 1
    j = pl.program_id(2)
    c_exp = (d ** -0.5) * np.log2(np.e).astype(np.float32)
    q_aug = q_ref[0, 0, 0]

    def group_scores(g, slot):
        c0 = pl.multiple_of(g * gkeys, gkeys)
        k_aug = jnp.concatenate([k_ref[0, pl.ds(c0, gkeys), :], mask_ref[pl.ds(c0, gkeys), :]], axis=1)
        s_t = jnp.dot(k_aug, q_aug, preferred_element_type=F32)
        s_sc[slot] = s_t
        return jnp.max(s_t, axis=0, keepdims=True)

    r0 = pl.multiple_of(j * blk, blk)
    key = lax.broadcasted_iota(jnp.int32, (blk, blk), 0)
    qry = lax.broadcasted_iota(jnp.int32, (blk, blk), 1)
    s_own = jnp.dot(k_ref[0, pl.ds(r0, blk), :], q_aug[:d], preferred_element_type=F32)
    s_own = jnp.where(key <= qry, s_own, -jnp.inf)
    m0 = jnp.max(s_own, axis=0, keepdims=True)
    p0 = jnp.exp2((s_own - m0) * c_exp)
    l0 = jnp.sum(p0, axis=0, keepdims=True)
    acc0 = jnp.dot(vown_ref[0, 0, 0], p0.astype(BF16), preferred_element_type=F32)

    def fold_group(g, state, slot):
        m_run, l_run, acc, s_max = state
        next_max = group_scores(jnp.minimum(g + 1, last_group), 1 - slot)
        m_new = jnp.maximum(m_run, s_max)
        alpha = jnp.exp2((m_run - m_new) * c_exp)
        p = jnp.exp2((s_sc[slot] - m_new) * c_exp)
        l_new = alpha * l_run + jnp.sum(p, axis=0, keepdims=True)
        acc = alpha * acc + jnp.dot(vt_ref[0, 0, g], p.astype(BF16), preferred_element_type=F32)
        return m_new, l_new, acc, next_max

    def group_body(g, state):
        return lax.cond(g % 2 == 0, lambda st: fold_group(g, st, 0), lambda st: fold_group(g, st, 1), state)

    ngroups = lax.div(j + (group - 1), group)
    _, l_fin, acc, _ = lax.fori_loop(0, ngroups, group_body, (m0, l0, acc0, group_scores(0, 0)))
    o_ref[0] = (acc / l_fin).T.astype(o_ref.dtype)


def _moba_mask_columns(seq):
    blk_id = jnp.arange(seq, dtype=jnp.int32)[:, None] // MOBA_BLOCK
    return jnp.where(jnp.arange(LANES, dtype=jnp.int32)[None, :] == blk_id, MASK_BIG, 0.0).astype(BF16)


def _moba_attn_t(q_aug, kb, vt, mask_cols):
    nbatch, seq, w = kb.shape
    blk = MOBA_BLOCK
    d = HEAD_DIM
    ngrp, gkeys = vt.shape[2], vt.shape[4]
    group = gkeys // blk
    blocks = [((2 * d, blk), BF16), ((1, seq, d), BF16), ((seq, LANES), BF16), ((ngrp, d, gkeys), BF16),
              ((d, blk), BF16), ((1, blk, d), BF16)]
    scratch = [((2, gkeys, blk), F32)]
    return pl.pallas_call(
        _moba_attn_t_kernel,
        grid=(nbatch, MOBA_HEADS, _tiles(seq, blk)),
        in_specs=[pl.BlockSpec((1, 1, 1, 2 * d, blk), lambda b, h, j: (b, h, j, 0, 0)),
                  pl.BlockSpec((1, seq, d), lambda b, h, j: (b, 0, h)),
                  pl.BlockSpec((seq, LANES), lambda b, h, j: (0, 0)),
                  pl.BlockSpec((1, 1, ngrp, d, gkeys), lambda b, h, j: (b, h, 0, 0, 0)),
                  pl.BlockSpec((1, 1, 1, d, blk), lambda b, h, j: (b, h, j // group, 0, j % group))],
        out_specs=pl.BlockSpec((1, blk, d), lambda b, h, j: (b, j, h)),
        out_shape=jax.ShapeDtypeStruct((nbatch, seq, w), BF16),
        scratch_shapes=[pltpu.VMEM(s, dt) for s, dt in scratch],
        compiler_params=_params(("arbitrary", "arbitrary", "arbitrary"), blocks, scratch),
        name="moba_attn",
    )(q_aug, kb, mask_cols, vt, vt)


def _gelu_tanh(x):
    c = np.sqrt(2.0 / np.pi).astype(np.float32)
    return x * (0.5 * (1.0 + jnp.tanh(c * (x + 0.044715 * (x * x * x)))))


def _gmlp_kernel(u_ref, v_ref, lg_ref, lb_ref, ws_ref, bst_ref, y_ref):
    rows = u_ref.shape[1]
    t = GMLP_CHUNK
    gd = GMLP_WIDTH // GMLP_GROUPS
    v = _gelu_tanh(v_ref[0])
    mu = jnp.mean(v, axis=-1, keepdims=True)
    vc = v - mu
    vln = vc * lax.rsqrt(jnp.mean(vc * vc, axis=-1, keepdims=True) + NORM_EPS) * lg_ref[...] + lb_ref[...]
    vb = vln.astype(BF16)
    row = lax.broadcasted_iota(jnp.int32, (t, t), 0)
    col = lax.broadcasted_iota(jnp.int32, (t, t), 1)
    for g in range(GMLP_GROUPS):
        wg = jnp.where(col <= row, ws_ref[g], 0.0).astype(BF16)
        bias = bst_ref[:, g:g + 1]
        cols = slice(g * gd, (g + 1) * gd)
        for c in range(rows // t):
            rs = slice(c * t, (c + 1) * t)
            mixed = jnp.dot(wg, vb[rs, cols], preferred_element_type=F32) + bias
            y_ref[0, rs, cols] = (_gelu_tanh(u_ref[0, rs, cols]) * mixed).astype(y_ref.dtype)


def _gmlp(z3, u_cb, v_cb, ln_g, ln_b, ws, bs, rows=512):
    nbatch, seq, _ = z3.shape
    w = GMLP_WIDTH
    t = GMLP_CHUNK
    blocks = [((1, rows, w), F32), ((1, rows, w), F32), ((GMLP_GROUPS, t, t), F32), ((1, rows, w), BF16)]
    return pl.pallas_call(
        _gmlp_kernel,
        grid=(nbatch, _tiles(seq, rows)),
        in_specs=[pl.BlockSpec((1, rows, w), lambda b, c: (b, c, u_cb)),
                  pl.BlockSpec((1, rows, w), lambda b, c: (b, c, v_cb)),
                  pl.BlockSpec((1, w), lambda b, c: (0, 0)),
                  pl.BlockSpec((1, w), lambda b, c: (0, 0)),
                  pl.BlockSpec((GMLP_GROUPS, t, t), lambda b, c: (0, 0, 0)),
                  pl.BlockSpec((t, GMLP_GROUPS), lambda b, c: (0, 0))],
        out_specs=pl.BlockSpec((1, rows, w), lambda b, c: (b, c, 0)),
        out_shape=jax.ShapeDtypeStruct((nbatch, seq, w), BF16),
        compiler_params=_params(("arbitrary", "arbitrary"), blocks),
        name="gmlp",
    )(z3, z3, ln_g.reshape(1, w), ln_b.reshape(1, w), ws, bs.T)


def _layer_weight_spec(layer, k, tn, index_of):
    return pl.BlockSpec((None, k, tn), lambda *idx: (layer,) + index_of(*idx))


def _merge_kernel(h_ref, wga_ref, wgb_ref, wgc_ref, ya_ref, yb_ref, yc_ref, wa_ref, wb_ref, wc_ref, o_ref):
    h = h_ref[...]

    def branch(wg_ref, y_ref, w_ref):
        gate = jax.nn.sigmoid(jnp.dot(h, wg_ref[...], preferred_element_type=F32))
        return gate * jnp.dot(y_ref[...], w_ref[...].astype(BF16), preferred_element_type=F32)

    merged = branch(wga_ref, ya_ref, wa_ref) + branch(wgb_ref, yb_ref, wb_ref) + branch(wgc_ref, yc_ref, wc_ref)
    o_ref[...] = merged.astype(o_ref.dtype)


def _merge(h, w_gate, ya, yb, yc, wa, wb, wc, layer, tm=1024, tn=512):
    m, d = h.shape
    nblk = _tiles(d, tn)
    ka, kb, kc = ya.shape[1], yb.shape[1], yc.shape[1]
    blocks = ([((tm, d), BF16)] + [((d, tn), BF16)] * 3
              + [((tm, ka), BF16), ((tm, kb), BF16), ((tm, kc), BF16)]
              + [((ka, tn), F32), ((kb, tn), F32), ((kc, tn), F32), ((tm, tn), BF16)])

    def gate_spec(branch):
        return pl.BlockSpec((None, d, tn), lambda i, j: (layer, 0, branch * nblk + j))

    def col(i, j):
        return (0, j)

    return pl.pallas_call(
        _merge_kernel,
        grid=(_tiles(m, tm), nblk),
        in_specs=[pl.BlockSpec((tm, d), lambda i, j: (i, 0)),
                  gate_spec(0), gate_spec(1), gate_spec(2),
                  pl.BlockSpec((tm, ka), lambda i, j: (i, 0)),
                  pl.BlockSpec((tm, kb), lambda i, j: (i, 0)),
                  pl.BlockSpec((tm, kc), lambda i, j: (i, 0)),
                  _layer_weight_spec(layer, ka, tn, col),
                  _layer_weight_spec(layer, kb, tn, col),
                  _layer_weight_spec(layer, kc, tn, col)],
        out_specs=pl.BlockSpec((tm, tn), lambda i, j: (i, j)),
        out_shape=jax.ShapeDtypeStruct((m, d), BF16),
        compiler_params=_params(("arbitrary", "arbitrary"), blocks),
        name="merge",
    )(h, w_gate, w_gate, w_gate, ya, yb, yc, wa, wb, wc)


def _residual_matmul_kernel(a_ref, w_ref, r_ref, o_ref):
    @pl.when(pl.program_id(2) == 0)
    def _():
        o_ref[...] = r_ref[...]

    o_ref[...] += jnp.dot(a_ref[...], w_ref[...].astype(BF16), preferred_element_type=F32)


def _residual_matmul(a, w, layer, res, tm=1024, tn=1024, tk=1024):
    m, k = a.shape
    n = w.shape[2]
    blocks = [((tm, tk), BF16), ((tk, tn), F32), ((tm, tn), F32), ((tm, tn), F32)]
    return pl.pallas_call(
        _residual_matmul_kernel,
        grid=(_tiles(m, tm), _tiles(n, tn), _tiles(k, tk)),
        in_specs=[pl.BlockSpec((tm, tk), lambda i, j, kk: (i, kk)),
                  _layer_weight_spec(layer, tk, tn, lambda i, j, kk: (kk, j)),
                  pl.BlockSpec((tm, tn), lambda i, j, kk: (i, j))],
        out_specs=pl.BlockSpec((tm, tn), lambda i, j, kk: (i, j)),
        out_shape=jax.ShapeDtypeStruct((m, n), F32),
        compiler_params=_params(("arbitrary", "arbitrary", "arbitrary"), blocks),
        name="residual_matmul",
    )(a, w, res)


def _relu2_matmul_kernel(a_ref, w_ref, o_ref):
    up = jnp.maximum(jnp.dot(a_ref[...], w_ref[...].astype(BF16), preferred_element_type=F32), 0.0)
    o_ref[...] = (up * up).astype(o_ref.dtype)


def _relu2_matmul(a, w, layer, tm=1024, tn=1024):
    m, k = a.shape
    n = w.shape[2]
    blocks = [((tm, k), BF16), ((k, tn), F32), ((tm, tn), BF16)]
    return pl.pallas_call(
        _relu2_matmul_kernel,
        grid=(_tiles(m, tm), _tiles(n, tn)),
        in_specs=[pl.BlockSpec((tm, k), lambda i, j: (i, 0)),
                  _layer_weight_spec(layer, k, tn, lambda i, j: (0, j))],
        out_specs=pl.BlockSpec((tm, tn), lambda i, j: (i, j)),
        out_shape=jax.ShapeDtypeStruct((m, n), BF16),
        compiler_params=_params(("arbitrary", "arbitrary"), blocks),
        name="mlp_up",
    )(a, w)


def _ple_kernel(h_ref, wg_ref, p_ref, wp_ref, r_ref, o_ref):
    gate = jax.nn.sigmoid(jnp.dot(h_ref[...], wg_ref[...].astype(BF16), preferred_element_type=F32))
    emb = jnp.dot(p_ref[...].astype(BF16), wp_ref[...].astype(BF16), preferred_element_type=F32)
    o_ref[...] = r_ref[...] + gate * emb


def _ple(h, w_gate, p, w_proj, layer, res, tm=1024, tn=512):
    m, d = h.shape
    pd = p.shape[2]
    blocks = [((tm, d), BF16), ((d, tn), F32), ((tm, pd), F32), ((pd, tn), F32),
              ((tm, tn), F32), ((tm, tn), F32)]
    return pl.pallas_call(
        _ple_kernel,
        grid=(_tiles(m, tm), _tiles(d, tn)),
        in_specs=[pl.BlockSpec((tm, d), lambda i, j: (i, 0)),
                  _layer_weight_spec(layer, d, tn, lambda i, j: (0, j)),
                  pl.BlockSpec((None, tm, pd), lambda i, j: (layer, i, 0)),
                  _layer_weight_spec(layer, pd, tn, lambda i, j: (0, j)),
                  pl.BlockSpec((tm, tn), lambda i, j: (i, j))],
        out_specs=pl.BlockSpec((tm, tn), lambda i, j: (i, j)),
        out_shape=jax.ShapeDtypeStruct((m, d), F32),
        compiler_params=_params(("arbitrary", "arbitrary"), blocks),
        name="ple",
    )(h, w_gate, p, w_proj, res)


def kernel(x, p, norm_mix_g, w_in, mlstm_gate_b, mlstm_norm_g, gmlp_norm_g, gmlp_norm_b, gmlp_ws, gmlp_bs,
           w_branch_a, w_branch_b, w_branch_c, w_out, norm_mlp_g, w_mlp_up, w_mlp_down, norm_ple_g,
           w_ple_gate, w_ple_proj, final_norm_g):
    nbatch, seq, d = x.shape
    depth = w_in.shape[0]
    m = nbatch * seq
    assert d == MLSTM_WIDTH + MOBA_WIDTH + GMLP_WIDTH

    qkvo_a = 4 * MLSTM_WIDTH
    gates_if = 2 * MLSTM_HEADS
    main_b = qkvo_a + gates_if
    main_cols = 3 * MOBA_WIDTH + 2 * GMLP_WIDTH
    gate_off = main_b + main_cols
    assert w_in.shape[2] == gate_off + N_BRANCHES * d

    moba_q_cb = qkvo_a // MOBA_WIDTH
    moba_k_cb = moba_q_cb + 1
    moba_v_cb = moba_k_cb + 1
    gmlp_u_cb = (qkvo_a + 3 * MOBA_WIDTH) // GMLP_WIDTH
    gmlp_v_cb = gmlp_u_cb + 1

    z_cols = qkvo_a + main_cols
    colscale = jnp.ones((1, z_cols), F32).at[:, MLSTM_WIDTH:2 * MLSTM_WIDTH].set(HEAD_DIM ** -0.5)
    rope_tables = _rope_tables(seq) + _rope_tables_t(seq)
    mask_cols = _moba_mask_columns(seq)

    xf = x.reshape(m, d)
    p_flat = p.reshape(depth, m, p.shape[-1])
    w_main, w_if, w_gate = _win_split(w_in, qkvo_a, gates_if, z_cols)
    for i in range(depth):
        h = _rmsnorm(xf, norm_mix_g[i], BF16)
        z, zif = _inproj(h, w_main, w_if, i, colscale)
        z3 = z.reshape(nbatch, seq, z_cols)
        zif3 = zif.reshape(nbatch, seq, GATE_LANES)

        ya = _mlstm(z3, zif3, mlstm_gate_b[i], mlstm_norm_g[i])
        yb = _moba_attn_t(*_moba_prep_t(z3, rope_tables, moba_q_cb, moba_k_cb, moba_v_cb), mask_cols)
        yc = _gmlp(z3, gmlp_u_cb, gmlp_v_cb, gmlp_norm_g[i], gmlp_norm_b[i], gmlp_ws[i], gmlp_bs[i])

        merged = _merge(h, w_gate, ya.reshape(m, -1), yb.reshape(m, -1), yc.reshape(m, -1),
                        w_branch_a, w_branch_b, w_branch_c, i)
        xf = _residual_matmul(merged, w_out, i, xf, tm=2048)

        h2 = _rmsnorm(xf, norm_mlp_g[i], BF16)
        hidden = _relu2_matmul(h2, w_mlp_up, i)
        xf = _residual_matmul(hidden, w_mlp_down, i, xf, tm=2048)

        h3 = _rmsnorm(xf, norm_ple_g[i], BF16)
        xf = _ple(h3, w_ple_gate, p_flat, w_ple_proj, i, xf, tm=2048)

    return _rmsnorm(xf, final_norm_g, F32).reshape(nbatch, seq, d)
```

```python
import functools

import jax
import jax.numpy as jnp
import numpy as np
from jax import lax
from jax.experimental import pallas as pl
from jax.experimental.pallas import tpu as pltpu

F32 = jnp.float32
BF16 = jnp.bfloat16

HEAD_DIM = 128
MLSTM_HEADS = 4
MLSTM_WIDTH = MLSTM_HEADS * HEAD_DIM
MOBA_HEADS = 8
MOBA_WIDTH = MOBA_HEADS * HEAD_DIM
MOBA_BLOCK = 256
MOBA_TOPK = 3
ROPE_THETA = 500000.0
ROPE_DIM = HEAD_DIM // 4
GMLP_WIDTH = 512
GMLP_GROUPS = 4
GMLP_CHUNK = 128
N_BRANCHES = 3
NORM_EPS = 1e-6

LANES = 128
V7X_VMEM_BYTES = 64 * 1024 * 1024
VMEM_CEILING = V7X_VMEM_BYTES - 8 * 1024 * 1024

MLSTM_KERNEL_CHUNK = 128
GATE_LANES = LANES
MASK_BIG = 2.0 ** 100
MOBA_GROUP = 4

NT_DIMS = (((1,), (1,)), ((), ()))
TN_DIMS = (((0,), (0,)), ((), ()))


def _tiles(n, t):
    count, rest = divmod(n, t)
    assert rest == 0 and count > 0, (n, t)
    return count


def _nbytes(shape, dtype):
    return int(np.prod(shape)) * jnp.dtype(dtype).itemsize


def _params(semantics, blocks, scratch=()):
    need = 2 * sum(_nbytes(s, d) for s, d in blocks) + sum(_nbytes(s, d) for s, d in scratch)
    limit = min(VMEM_CEILING, need + need // 4 + 4 * 1024 * 1024)
    return pltpu.CompilerParams(dimension_semantics=semantics, vmem_limit_bytes=limit)


def _rmsnorm_kernel(x_ref, g_ref, o_ref):
    x = x_ref[...]
    y = x * lax.rsqrt(jnp.mean(x * x, axis=-1, keepdims=True) + NORM_EPS)
    o_ref[...] = (y * g_ref[...]).astype(o_ref.dtype)


def _rmsnorm(x, g, out_dtype, tm=512):
    m, d = x.shape
    return pl.pallas_call(
        _rmsnorm_kernel,
        grid=(_tiles(m, tm),),
        in_specs=[pl.BlockSpec((tm, d), lambda i: (i, 0)),
                  pl.BlockSpec((1, d), lambda i: (0, 0))],
        out_specs=pl.BlockSpec((tm, d), lambda i: (i, 0)),
        out_shape=jax.ShapeDtypeStruct((m, d), out_dtype),
        compiler_params=_params(("arbitrary",), [((tm, d), F32), ((tm, d), out_dtype)]),
        name="rmsnorm",
    )(x, g.reshape(1, d))


def _win_split_kernel(wm_ref, wif_ref, wg_ref, main_ref, if_ref, gate_ref):
    main_ref[...] = wm_ref[0].T.astype(BF16)
    gate_ref[...] = wg_ref[0].T.astype(BF16)
    _, n_if, k = wif_ref.shape
    padded = jnp.concatenate([wif_ref[0], jnp.zeros((GATE_LANES - n_if, k), F32)], axis=0)
    if_ref[...] = padded.T.astype(BF16)


def _win_split(w_in, head, n_if, main_cols, tn=512):
    depth, k, cols = w_in.shape
    gate_cols = cols - main_cols - n_if
    assert gate_cols == main_cols and head % tn == 0
    w_t = jnp.swapaxes(w_in, 1, 2)
    head_steps = head // tn
    blocks = [((tn, k), F32), ((n_if, k), F32), ((tn, k), F32), ((k, tn), BF16), ((k, GATE_LANES), BF16),
              ((k, tn), BF16)]

    def main_rows(l, r):
        return (l, pl.multiple_of(r * tn + jnp.where(r >= head_steps, n_if, 0), n_if), 0)

    def window(rows, start_of):
        return pl.BlockSpec((pl.Element(1), pl.Element(rows), pl.Element(k)), start_of)

    return pl.pallas_call(
        _win_split_kernel,
        grid=(depth, _tiles(main_cols, tn)),
        in_specs=[window(tn, main_rows),
                  window(n_if, lambda l, r: (l, head, 0)),
                  window(tn, lambda l, r: (l, pl.multiple_of(main_cols + n_if + r * tn, n_if), 0))],
        out_specs=[pl.BlockSpec((None, k, tn), lambda l, r: (l, 0, r)),
                   pl.BlockSpec((None, k, GATE_LANES), lambda l, r: (l, 0, 0)),
                   pl.BlockSpec((None, k, tn), lambda l, r: (l, 0, r))],
        out_shape=[jax.ShapeDtypeStruct((depth, k, main_cols), BF16),
                   jax.ShapeDtypeStruct((depth, k, GATE_LANES), BF16),
                   jax.ShapeDtypeStruct((depth, k, gate_cols), BF16)],
        compiler_params=_params(("arbitrary", "arbitrary"), blocks),
        name="win_split",
    )(w_t, w_t, w_t)


def _inproj_kernel(h_ref, w_ref, wif_ref, cs_ref, z_ref, zif_ref):
    acc = jnp.dot(h_ref[...], w_ref[...], preferred_element_type=F32)
    z_ref[...] = acc * cs_ref[...]

    @pl.when(pl.program_id(1) == 0)
    def _():
        zif_ref[...] = jnp.dot(h_ref[...], wif_ref[...], preferred_element_type=F32)


def _inproj(h, w_main, w_if, layer, colscale, tm=1024, tn=512):
    m, k = h.shape
    n = w_main.shape[2]
    blocks = [((tm, k), BF16), ((k, tn), BF16), ((k, GATE_LANES), BF16), ((1, tn), F32),
              ((tm, tn), F32), ((tm, GATE_LANES), F32)]
    return pl.pallas_call(
        _inproj_kernel,
        grid=(_tiles(m, tm), _tiles(n, tn)),
        in_specs=[pl.BlockSpec((tm, k), lambda i, j: (i, 0)),
                  pl.BlockSpec((None, k, tn), lambda i, j: (layer, 0, j)),
                  pl.BlockSpec((None, k, GATE_LANES), lambda i, j: (layer, 0, 0)),
                  pl.BlockSpec((1, tn), lambda i, j: (0, j))],
        out_specs=[pl.BlockSpec((tm, tn), lambda i, j: (i, j)),
                   pl.BlockSpec((tm, GATE_LANES), lambda i, j: (i, 0))],
        out_shape=[jax.ShapeDtypeStruct((m, n), F32),
                   jax.ShapeDtypeStruct((m, GATE_LANES), F32)],
        compiler_params=_params(("arbitrary", "arbitrary"), blocks),
        name="in_proj",
    )(h, w_main, w_if, colscale)


def _log_sigmoid(x):
    return jnp.minimum(x, 0.0) - jnp.log1p(jnp.exp(-jnp.abs(x)))


def _mlstm_kernel(q_ref, k_ref, v_ref, o_ref, zif_ref, gb_ref, ng_ref, y_ref, c_sc, n_sc, m_sc):
    nbatch, chunk, _ = q_ref.shape
    heads = MLSTM_HEADS

    @pl.when(pl.program_id(0) == 0)
    def _():
        c_sc[...] = jnp.zeros_like(c_sc)
        n_sc[...] = jnp.zeros_like(n_sc)
        m_sc[...] = jnp.zeros_like(m_sc)

    row = lax.broadcasted_iota(jnp.int32, (chunk, chunk), 0)
    col = lax.broadcasted_iota(jnp.int32, (chunk, chunk), 1)
    causal = col <= row
    tril = jnp.where(causal, 1.0, 0.0).astype(F32)
    lane = lax.broadcasted_iota(jnp.int32, (chunk, GATE_LANES), 1)

    for b in range(nbatch):
        pre = zif_ref[b] + gb_ref[...]
        gates = jnp.where(lane < heads, pre, _log_sigmoid(pre))
        gcum = jnp.dot(tril, gates, preferred_element_type=F32,
                       precision=lax.Precision.HIGHEST)
        gates_t = gates.T
        gcum_t = gcum.T
        for h in range(heads):
            s = b * heads + h
            sl = slice(h * HEAD_DIM, (h + 1) * HEAD_DIM)
            q = q_ref[b, :, sl]
            k = k_ref[b, :, sl]
            v = v_ref[b, :, sl]
            qb = q.astype(BF16)
            kb = k.astype(BF16)
            vb = v.astype(BF16)
            g_t = gcum[:, heads + h:heads + h + 1]
            g_s = gcum_t[heads + h:heads + h + 1, :]
            i_s = gates_t[h:h + 1, :]
            i_t = gates[:, h:h + 1]
            m_prev = m_sc[s][:, 0:1]
            c_prev = c_sc[s]
            n_prev = n_sc[s]

            log_w = jnp.where(causal, g_t - g_s + i_s, -jnp.inf)
            log_a = g_t + m_prev
            m_row = jnp.maximum(jnp.max(log_w, axis=1, keepdims=True), log_a)
            qk = lax.dot_general(qb, kb, NT_DIMS, preferred_element_type=F32) * jnp.exp(log_w - m_row)
            a = jnp.exp(log_a - m_row)
            num = (jnp.dot(qk.astype(BF16), vb, preferred_element_type=F32)
                   + a * lax.dot_general(qb, c_prev.astype(BF16), NT_DIMS, preferred_element_type=F32))
            den = (jnp.sum(qk, axis=1, keepdims=True)
                   + a * jnp.sum(q * n_prev, axis=1, keepdims=True))
            h_out = num / jnp.maximum(jnp.abs(den), jnp.exp(-m_row))

            g_last = gcum[chunk - 1:chunk, heads + h:heads + h + 1]
            log_u = g_last - g_t + i_t
            m_new = jnp.maximum(g_last + m_prev, jnp.max(log_u, axis=0, keepdims=True))
            decay = jnp.exp(g_last + m_prev - m_new)
            u = jnp.exp(log_u - m_new)
            c_sc[s] = decay * c_prev + lax.dot_general((u * v).astype(BF16), kb, TN_DIMS,
                                                       preferred_element_type=F32)
            n_sc[s] = decay * n_prev + jnp.sum(u * k, axis=0, keepdims=True)
            m_sc[s] = jnp.broadcast_to(m_new, (1, LANES))

            yn = h_out * lax.rsqrt(jnp.mean(h_out * h_out, axis=-1, keepdims=True) + NORM_EPS)
            y_ref[b, :, sl] = (jax.nn.sigmoid(o_ref[b, :, sl]) * (yn * ng_ref[:, sl])).astype(y_ref.dtype)


def _mlstm(z3, zif3, gate_b, norm_g):
    nbatch, seq, _ = z3.shape
    chunk = MLSTM_KERNEL_CHUNK
    w = MLSTM_WIDTH
    streams = nbatch * MLSTM_HEADS
    gb = jnp.pad(gate_b, (0, GATE_LANES - gate_b.shape[0])).reshape(1, GATE_LANES)
    blocks = [((nbatch, chunk, w), F32)] * 4 + [((nbatch, chunk, GATE_LANES), F32),
                                               ((nbatch, chunk, w), BF16)]
    scratch = [((streams, HEAD_DIM, HEAD_DIM), F32), ((streams, 1, LANES), F32), ((streams, 1, LANES), F32)]

    def zcol(cb):
        return pl.BlockSpec((nbatch, chunk, w), lambda c: (0, c, cb))

    return pl.pallas_call(
        _mlstm_kernel,
        grid=(_tiles(seq, chunk),),
        in_specs=[zcol(0), zcol(1), zcol(2), zcol(3),
                  pl.BlockSpec((nbatch, chunk, GATE_LANES), lambda c: (0, c, 0)),
                  pl.BlockSpec((1, GATE_LANES), lambda c: (0, 0)),
                  pl.BlockSpec((1, w), lambda c: (0, 0))],
        out_specs=pl.BlockSpec((nbatch, chunk, w), lambda c: (0, c, 0)),
        out_shape=jax.ShapeDtypeStruct((nbatch, seq, w), BF16),
        scratch_shapes=[pltpu.VMEM(s, d) for s, d in scratch],
        compiler_params=_params(("arbitrary",), blocks, scratch),
        name="mlstm",
    )(z3, z3, z3, z3, zif3, gb, norm_g.reshape(1, w))


def _rope_tables(seq):
    half = ROPE_DIM // 2
    inv_freq = ROPE_THETA ** (-jnp.arange(0, ROPE_DIM, 2, dtype=F32) / ROPE_DIM)
    ang = jnp.arange(seq, dtype=F32)[:, None] * inv_freq[None, :]
    cos = jnp.cos(ang)
    sin = jnp.sin(ang)
    ones = jnp.ones((seq, HEAD_DIM - ROPE_DIM), F32)
    cos_tab = jnp.concatenate([cos, cos, ones], axis=1)
    sin_tab = jnp.concatenate([-sin, sin, 0.0 * ones], axis=1)
    assert cos_tab.shape == (seq, HEAD_DIM) and half * 2 == ROPE_DIM
    return cos_tab, sin_tab


def _rotary(t, cos, sin):
    half = ROPE_DIM // 2
    lane = lax.broadcasted_iota(jnp.int32, t.shape, 1)
    upper = pltpu.roll(t, HEAD_DIM - half, axis=1)
    lower = pltpu.roll(t, half, axis=1)
    partner = jnp.where(lane < half, upper, lower)
    return jnp.where(lane < ROPE_DIM, t * cos + partner * sin, t)


def _moba_prep_kernel(q_ref, k_ref, v_ref, cos_ref, sin_ref, qa_ref, kb_ref, vb_ref, km_sc):
    blk = q_ref.shape[1]
    d = HEAD_DIM
    j = pl.program_id(1)

    @pl.when(j == 0)
    def _():
        km_sc[...] = jnp.zeros_like(km_sc)

    cos = cos_ref[...]
    sin = sin_ref[...]
    lane = lax.broadcasted_iota(jnp.int32, (blk, LANES), 1)
    lane_f = lane.astype(F32)
    mean_row = lax.broadcasted_iota(jnp.int32, (LANES, d), 0)
    vb_ref[0] = v_ref[0].astype(BF16)
    for h in range(MOBA_HEADS):
        sl = slice(h * d, (h + 1) * d)
        qf = _rotary(q_ref[0, :, sl], cos, sin)
        kk = _rotary(k_ref[0, :, sl], cos, sin)
        kb_ref[0, :, sl] = kk.astype(BF16)

        gate = lax.dot_general(qf, km_sc[h], NT_DIMS, preferred_element_type=F32,
                               precision=lax.Precision.HIGHEST)
        gate = jnp.where(lane < j, gate, -jnp.inf)
        sel_m1 = jnp.full((blk, LANES), -1.0, F32)
        for _ in range(MOBA_TOPK):
            mx = jnp.max(gate, axis=1, keepdims=True)
            first = jnp.min(jnp.where(gate == mx, lane_f, float(LANES)), axis=1, keepdims=True)
            first = jnp.where(mx > -jnp.inf, first, -1.0)
            hit = lane_f == first
            sel_m1 = jnp.where(hit, 0.0, sel_m1)
            gate = jnp.where(hit, -jnp.inf, gate)
        qa_ref[0, :, 2 * h * d:(2 * h + 1) * d] = qf.astype(BF16)
        qa_ref[0, :, (2 * h + 1) * d:(2 * h + 2) * d] = sel_m1.astype(BF16)

        km_sc[h] = jnp.where(mean_row == j, jnp.mean(kk, axis=0, keepdims=True), km_sc[h])


def _moba_prep(z3, cos_tab, sin_tab, q_cb, k_cb, v_cb):
    nbatch, seq, _ = z3.shape
    blk = MOBA_BLOCK
    w = MOBA_WIDTH
    d = HEAD_DIM
    blocks = [((1, blk, w), F32)] * 3 + [((blk, d), F32)] * 2 + [((1, blk, 2 * w), BF16), ((1, blk, w), BF16),
                                                                  ((1, blk, w), BF16)]
    scratch = [((MOBA_HEADS, LANES, d), F32)]
    return pl.pallas_call(
        _moba_prep_kernel,
        grid=(nbatch, _tiles(seq, blk)),
        in_specs=[pl.BlockSpec((1, blk, w), lambda b, j: (b, j, q_cb)),
                  pl.BlockSpec((1, blk, w), lambda b, j: (b, j, k_cb)),
                  pl.BlockSpec((1, blk, w), lambda b, j: (b, j, v_cb)),
                  pl.BlockSpec((blk, d), lambda b, j: (j, 0)),
                  pl.BlockSpec((blk, d), lambda b, j: (j, 0))],
        out_specs=[pl.BlockSpec((1, blk, 2 * w), lambda b, j: (b, j, 0)),
                   pl.BlockSpec((1, blk, w), lambda b, j: (b, j, 0)),
                   pl.BlockSpec((1, blk, w), lambda b, j: (b, j, 0))],
        out_shape=[jax.ShapeDtypeStruct((nbatch, seq, 2 * w), BF16),
                   jax.ShapeDtypeStruct((nbatch, seq, w), BF16),
                   jax.ShapeDtypeStruct((nbatch, seq, w), BF16)],
        scratch_shapes=[pltpu.VMEM(s, dt) for s, dt in scratch],
        compiler_params=_params(("arbitrary", "arbitrary"), blocks, scratch),
        name="moba_prep",
    )(z3, z3, z3, cos_tab, sin_tab)


def _moba_attn_kernel(q_ref, k_ref, v_ref, o_ref):
    blk = q_ref.shape[1]
    d = HEAD_DIM
    nblk = k_ref.shape[1] // blk
    group = MOBA_GROUP
    j = pl.program_id(2)
    scale = d ** -0.5
    q_aug = q_ref[0]
    lane = lax.broadcasted_iota(jnp.int32, (blk, LANES), 1)

    def softmax_partial(s, v_rows):
        m = jnp.max(s, axis=1, keepdims=True)
        p = jnp.exp(s - m)
        return m, jnp.sum(p, axis=1, keepdims=True), jnp.dot(p.astype(BF16), v_rows, preferred_element_type=F32)

    def past_partial(n):
        c0 = pl.multiple_of(jnp.minimum(n, nblk - 1) * blk, blk)
        onehot = jnp.where(lane == n, MASK_BIG, 0.0).astype(BF16)
        k_aug = jnp.concatenate([k_ref[0, pl.ds(c0, blk), :], onehot], axis=1)
        s = lax.dot_general(q_aug, k_aug, NT_DIMS, preferred_element_type=F32) * scale
        return softmax_partial(s, v_ref[0, pl.ds(c0, blk), :])

    def merge(state, parts):
        m_run, l_run, acc_run = state
        m_new = m_run
        for m, _, _ in parts:
            m_new = jnp.maximum(m_new, m)
        w = jnp.exp(m_run - m_new)
        l_new = w * l_run
        acc = w * acc_run
        for m, l, a in parts:
            w = jnp.exp(m - m_new)
            l_new = l_new + w * l
            acc = acc + w * a
        return m_new, l_new, acc

    r0 = pl.multiple_of(j * blk, blk)
    row = lax.broadcasted_iota(jnp.int32, (blk, blk), 0)
    col = lax.broadcasted_iota(jnp.int32, (blk, blk), 1)
    s_own = lax.dot_general(q_aug[:, :d], k_ref[0, pl.ds(r0, blk), :], NT_DIMS,
                            preferred_element_type=F32) * scale
    s_own = jnp.where(col <= row, s_own, -jnp.inf)
    state = softmax_partial(s_own, v_ref[0, pl.ds(r0, blk), :])

    state = merge(state, [past_partial(n) for n in range(group)])
    ngroups = lax.div(j + (group - 1), group)

    def group_body(g, st):
        return merge(st, [past_partial(g * group + i) for i in range(group)])

    _, l_fin, acc = lax.fori_loop(1, ngroups, group_body, state)
    o_ref[0] = (acc / l_fin).astype(o_ref.dtype)


def _moba_attn(q_aug, kb, vb):
    nbatch, seq, w = kb.shape
    blk = MOBA_BLOCK
    d = HEAD_DIM
    blocks = [((1, blk, 2 * d), BF16), ((1, seq, d), BF16), ((1, seq, d), BF16), ((1, blk, d), BF16)]
    return pl.pallas_call(
        _moba_attn_kernel,
        grid=(nbatch, MOBA_HEADS, _tiles(seq, blk)),
        in_specs=[pl.BlockSpec((1, blk, 2 * d), lambda b, h, j: (b, j, h)),
                  pl.BlockSpec((1, seq, d), lambda b, h, j: (b, 0, h)),
                  pl.BlockSpec((1, seq, d), lambda b, h, j: (b, 0, h))],
        out_specs=pl.BlockSpec((1, blk, d), lambda b, h, j: (b, j, h)),
        out_shape=jax.ShapeDtypeStruct((nbatch, seq, w), BF16),
        compiler_params=_params(("arbitrary", "arbitrary", "arbitrary"), blocks),
        name="moba_attn",
    )(q_aug, kb, vb)


def _rope_tables_t(seq):
    inv_freq = ROPE_THETA ** (-jnp.arange(0, ROPE_DIM, 2, dtype=F32) / ROPE_DIM)
    ang = jnp.arange(seq, dtype=F32)[:, None] * inv_freq[None, :]
    cos = jnp.cos(ang).T
    sin = jnp.sin(ang).T
    return jnp.concatenate([cos, cos], axis=0), jnp.concatenate([-sin, sin], axis=0)


def _moba_prep_t_kernel(q_ref, k_ref, v_ref, cos_ref, sin_ref, cost_ref, sint_ref, qa_ref, kb_ref, vt_ref, km_sc):
    blk = q_ref.shape[1]
    d = HEAD_DIM
    half = ROPE_DIM // 2
    nsel = km_sc.shape[1]
    j = pl.program_id(1)

    @pl.when(j == 0)
    def _():
        km_sc[...] = jnp.zeros_like(km_sc)

    cos = cos_ref[...]
    sin = sin_ref[...]
    cos_t = cost_ref[...]
    sin_t = sint_ref[...]
    blk_id = lax.broadcasted_iota(jnp.int32, (nsel, blk), 0)
    blk_id_f = blk_id.astype(F32)
    mean_row = lax.broadcasted_iota(jnp.int32, (nsel, d), 0)
    for h in range(MOBA_HEADS):
        sl = slice(h * d, (h + 1) * d)
        q_t = q_ref[0, :, sl].T
        top = q_t[:ROPE_DIM]
        partner = jnp.concatenate([top[half:], top[:half]], axis=0)
        q_t = jnp.concatenate([top * cos_t + partner * sin_t, q_t[ROPE_DIM:]], axis=0)

        gate = jnp.dot(km_sc[h], q_t, preferred_element_type=F32,
                       precision=lax.Precision.HIGHEST)
        gate = jnp.where(blk_id < j, gate, -jnp.inf)
        sel_m1 = jnp.full((nsel, blk), -1.0, F32)
        for _ in range(MOBA_TOPK):
            mx = jnp.max(gate, axis=0, keepdims=True)
            first = jnp.min(jnp.where(gate == mx, blk_id_f, float(nsel)), axis=0, keepdims=True)
            first = jnp.where(mx > -jnp.inf, first, -1.0)
            hit = blk_id_f == first
            sel_m1 = jnp.where(hit, 0.0, sel_m1)
            gate = jnp.where(hit, -jnp.inf, gate)
        qa_ref[0, h, 0, 0:d, :] = q_t.astype(BF16)
        qa_ref[0, h, 0, d:d + nsel, :] = sel_m1.astype(BF16)
        qa_ref[0, h, 0, d + nsel:, :] = jnp.full((d - nsel, blk), -1.0, BF16)

        kk = _rotary(k_ref[0, :, sl], cos, sin)
        kb_ref[0, :, sl] = kk.astype(BF16)
        km_sc[h] = jnp.where(mean_row == j, jnp.mean(kk, axis=0, keepdims=True), km_sc[h])

        vt_ref[0, h, 0] = v_ref[0, :, sl].T.astype(BF16)


def _moba_prep_t(z3, tables, q_cb, k_cb, v_cb):
    nbatch, seq, _ = z3.shape
    blk = MOBA_BLOCK
    nblk = _tiles(seq, blk)
    nsel = -(-nblk // 16) * 16
    assert nsel <= HEAD_DIM
    w = MOBA_WIDTH
    d = HEAD_DIM
    hh = MOBA_HEADS
    cos_tab, sin_tab, cos_t, sin_t = tables
    blocks = ([((1, blk, w), F32)] * 3 + [((blk, d), F32)] * 2 + [((ROPE_DIM, blk), F32)] * 2
              + [((hh, 2 * d, blk), BF16), ((1, blk, w), BF16), ((hh, d, blk), BF16)])
    scratch = [((hh, nsel, d), F32)]
    return pl.pallas_call(
        _moba_prep_t_kernel,
        grid=(nbatch, nblk),
        in_specs=[pl.BlockSpec((1, blk, w), lambda b, j: (b, j, q_cb)),
                  pl.BlockSpec((1, blk, w), lambda b, j: (b, j, k_cb)),
                  pl.BlockSpec((1, blk, w), lambda b, j: (b, j, v_cb)),
                  pl.BlockSpec((blk, d), lambda b, j: (j, 0)),
                  pl.BlockSpec((blk, d), lambda b, j: (j, 0)),
                  pl.BlockSpec((ROPE_DIM, blk), lambda b, j: (0, j)),
                  pl.BlockSpec((ROPE_DIM, blk), lambda b, j: (0, j))],
        out_specs=[pl.BlockSpec((1, hh, 1, 2 * d, blk), lambda b, j: (b, 0, j, 0, 0)),
                   pl.BlockSpec((1, blk, w), lambda b, j: (b, j, 0)),
                   pl.BlockSpec((1, hh, 1, d, blk), lambda b, j: (b, 0, j // MOBA_GROUP, 0, j % MOBA_GROUP))],
        out_shape=[jax.ShapeDtypeStruct((nbatch, hh, nblk, 2 * d, blk), BF16),
                   jax.ShapeDtypeStruct((nbatch, seq, w), BF16),
                   jax.ShapeDtypeStruct((nbatch, hh, _tiles(nblk, MOBA_GROUP), d, MOBA_GROUP * blk), BF16)],
        scratch_shapes=[pltpu.VMEM(s, dt) for s, dt in scratch],
        compiler_params=_params(("arbitrary", "arbitrary"), blocks, scratch),
        name="moba_prep",
    )(z3, z3, z3, cos_tab, sin_tab, cos_t, sin_t)


def _moba_attn_t_kernel(q_ref, k_ref, mask_ref, vt_ref, vown_ref, o_ref, s_sc):
    blk = q_ref.shape[-1]
    d = HEAD_DIM
    gkeys = vt_ref.shape[-1]
    group = gkeys // blk
    last_group = vt_ref.shape[2] - 1
    j = pl.program_id(2)
    c_exp = (d ** -0.5) * np.log2(np.e).astype(np.float32)
    q_aug = q_ref[0, 0, 0]

    def group_scores(g, slot):
        c0 = pl.multiple_of(g * gkeys, gkeys)
        k_aug = jnp.concatenate([k_ref[0, pl.ds(c0, gkeys), :], mask_ref[pl.ds(c0, gkeys), :]], axis=1)
        s_t = jnp.dot(k_aug, q_aug, preferred_element_type=F32)
        s_sc[slot] = s_t
        return jnp.max(s_t, axis=0, keepdims=True)

    r0 = pl.multiple_of(j * blk, blk)
    key = lax.broadcasted_iota(jnp.int32, (blk, blk), 0)
    qry = lax.broadcasted_iota(jnp.int32, (blk, blk), 1)
    s_own = jnp.dot(k_ref[0, pl.ds(r0, blk), :], q_aug[:d], preferred_element_type=F32)
    s_own = jnp.where(key <= qry, s_own, -jnp.inf)
    m0 = jnp.max(s_own, axis=0, keepdims=True)
    p0 = jnp.exp2((s_own - m0) * c_exp)
    l0 = jnp.sum(p0, axis=0, keepdims=True)
    acc0 = jnp.dot(vown_ref[0, 0, 0], p0.astype(BF16), preferred_element_type=F32)

    def fold_group(g, state, slot):
        m_run, l_run, acc, s_max = state
        next_max = group_scores(jnp.minimum(g + 1, last_group), 1 - slot)
        m_new = jnp.maximum(m_run, s_max)
        alpha = jnp.exp2((m_run - m_new) * c_exp)
        p = jnp.exp2((s_sc[slot] - m_new) * c_exp)
        l_new = alpha * l_run + jnp.sum(p, axis=0, keepdims=True)
        acc = alpha * acc + jnp.dot(vt_ref[0, 0, g], p.astype(BF16), preferred_element_type=F32)
        return m_new, l_new, acc, next_max

    def group_body(g, state):
        return lax.cond(g % 2 == 0, lambda st: fold_group(g, st, 0), lambda st: fold_group(g, st, 1), state)

    ngroups = lax.div(j + (group - 1), group)
    _, l_fin, acc, _ = lax.fori_loop(0, ngroups, group_body, (m0, l0, acc0, group_scores(0, 0)))
    o_ref[0] = (acc / l_fin).T.astype(o_ref.dtype)


def _moba_mask_columns(seq):
    blk_id = jnp.arange(seq, dtype=jnp.int32)[:, None] // MOBA_BLOCK
    return jnp.where(jnp.arange(LANES, dtype=jnp.int32)[None, :] == blk_id, MASK_BIG, 0.0).astype(BF16)


def _moba_attn_t(q_aug, kb, vt, mask_cols):
    nbatch, seq, w = kb.shape
    blk = MOBA_BLOCK
    d = HEAD_DIM
    ngrp, gkeys = vt.shape[2], vt.shape[4]
    group = gkeys // blk
    blocks = [((2 * d, blk), BF16), ((1, seq, d), BF16), ((seq, LANES), BF16), ((ngrp, d, gkeys), BF16),
              ((d, blk), BF16), ((1, blk, d), BF16)]
    scratch = [((2, gkeys, blk), F32)]
    return pl.pallas_call(
        _moba_attn_t_kernel,
        grid=(nbatch, MOBA_HEADS, _tiles(seq, blk)),
        in_specs=[pl.BlockSpec((1, 1, 1, 2 * d, blk), lambda b, h, j: (b, h, j, 0, 0)),
                  pl.BlockSpec((1, seq, d), lambda b, h, j: (b, 0, h)),
                  pl.BlockSpec((seq, LANES), lambda b, h, j: (0, 0)),
                  pl.BlockSpec((1, 1, ngrp, d, gkeys), lambda b, h, j: (b, h, 0, 0, 0)),
                  pl.BlockSpec((1, 1, 1, d, blk), lambda b, h, j: (b, h, j // group, 0, j % group))],
        out_specs=pl.BlockSpec((1, blk, d), lambda b, h, j: (b, j, h)),
        out_shape=jax.ShapeDtypeStruct((nbatch, seq, w), BF16),
        scratch_shapes=[pltpu.VMEM(s, dt) for s, dt in scratch],
        compiler_params=_params(("arbitrary", "arbitrary", "arbitrary"), blocks, scratch),
        name="moba_attn",
    )(q_aug, kb, mask_cols, vt, vt)


def _gelu_tanh(x):
    c = np.sqrt(2.0 / np.pi).astype(np.float32)
    return x * (0.5 * (1.0 + jnp.tanh(c * (x + 0.044715 * (x * x * x)))))


def _gmlp_kernel(u_ref, v_ref, lg_ref, lb_ref, ws_ref, bst_ref, y_ref):
    rows = u_ref.shape[1]
    t = GMLP_CHUNK
    gd = GMLP_WIDTH // GMLP_GROUPS
    v = _gelu_tanh(v_ref[0])
    mu = jnp.mean(v, axis=-1, keepdims=True)
    vc = v - mu
    vln = vc * lax.rsqrt(jnp.mean(vc * vc, axis=-1, keepdims=True) + NORM_EPS) * lg_ref[...] + lb_ref[...]
    vb = vln.astype(BF16)
    row = lax.broadcasted_iota(jnp.int32, (t, t), 0)
    col = lax.broadcasted_iota(jnp.int32, (t, t), 1)
    for g in range(GMLP_GROUPS):
        wg = jnp.where(col <= row, ws_ref[g], 0.0).astype(BF16)
        bias = bst_ref[:, g:g + 1]
        cols = slice(g * gd, (g + 1) * gd)
        for c in range(rows // t):
            rs = slice(c * t, (c + 1) * t)
            mixed = jnp.dot(wg, vb[rs, cols], preferred_element_type=F32) + bias
            y_ref[0, rs, cols] = (_gelu_tanh(u_ref[0, rs, cols]) * mixed).astype(y_ref.dtype)


def _gmlp(z3, u_cb, v_cb, ln_g, ln_b, ws, bs, rows=512):
    nbatch, seq, _ = z3.shape
    w = GMLP_WIDTH
    t = GMLP_CHUNK
    blocks = [((1, rows, w), F32), ((1, rows, w), F32), ((GMLP_GROUPS, t, t), F32), ((1, rows, w), BF16)]
    return pl.pallas_call(
        _gmlp_kernel,
        grid=(nbatch, _tiles(seq, rows)),
        in_specs=[pl.BlockSpec((1, rows, w), lambda b, c: (b, c, u_cb)),
                  pl.BlockSpec((1, rows, w), lambda b, c: (b, c, v_cb)),
                  pl.BlockSpec((1, w), lambda b, c: (0, 0)),
                  pl.BlockSpec((1, w), lambda b, c: (0, 0)),
                  pl.BlockSpec((GMLP_GROUPS, t, t), lambda b, c: (0, 0, 0)),
                  pl.BlockSpec((t, GMLP_GROUPS), lambda b, c: (0, 0))],
        out_specs=pl.BlockSpec((1, rows, w), lambda b, c: (b, c, 0)),
        out_shape=jax.ShapeDtypeStruct((nbatch, seq, w), BF16),
        compiler_params=_params(("arbitrary", "arbitrary"), blocks),
        name="gmlp",
    )(z3, z3, ln_g.reshape(1, w), ln_b.reshape(1, w), ws, bs.T)


def _layer_weight_spec(layer, k, tn, index_of):
    return pl.BlockSpec((None, k, tn), lambda *idx: (layer,) + index_of(*idx))


def _merge_kernel(h_ref, wga_ref, wgb_ref, wgc_ref, ya_ref, yb_ref, yc_ref, wa_ref, wb_ref, wc_ref, o_ref):
    h = h_ref[...]

    def branch(wg_ref, y_ref, w_ref):
        gate = jax.nn.sigmoid(jnp.dot(h, wg_ref[...], preferred_element_type=F32))
        return gate * jnp.dot(y_ref[...], w_ref[...].astype(BF16), preferred_element_type=F32)

    merged = branch(wga_ref, ya_ref, wa_ref) + branch(wgb_ref, yb_ref, wb_ref) + branch(wgc_ref, yc_ref, wc_ref)
    o_ref[...] = merged.astype(o_ref.dtype)


def _merge(h, w_gate, ya, yb, yc, wa, wb, wc, layer, tm=1024, tn=512):
    m, d = h.shape
    nblk = _tiles(d, tn)
    ka, kb, kc = ya.shape[1], yb.shape[1], yc.shape[1]
    blocks = ([((tm, d), BF16)] + [((d, tn), BF16)] * 3
              + [((tm, ka), BF16), ((tm, kb), BF16), ((tm, kc), BF16)]
              + [((ka, tn), F32), ((kb, tn), F32), ((kc, tn), F32), ((tm, tn), BF16)])

    def gate_spec(branch):
        return pl.BlockSpec((None, d, tn), lambda i, j: (layer, 0, branch * nblk + j))

    def col(i, j):
        return (0, j)

    return pl.pallas_call(
        _merge_kernel,
        grid=(_tiles(m, tm), nblk),
        in_specs=[pl.BlockSpec((tm, d), lambda i, j: (i, 0)),
                  gate_spec(0), gate_spec(1), gate_spec(2),
                  pl.BlockSpec((tm, ka), lambda i, j: (i, 0)),
                  pl.BlockSpec((tm, kb), lambda i, j: (i, 0)),
                  pl.BlockSpec((tm, kc), lambda i, j: (i, 0)),
                  _layer_weight_spec(layer, ka, tn, col),
                  _layer_weight_spec(layer, kb, tn, col),
                  _layer_weight_spec(layer, kc, tn, col)],
        out_specs=pl.BlockSpec((tm, tn), lambda i, j: (i, j)),
        out_shape=jax.ShapeDtypeStruct((m, d), BF16),
        compiler_params=_params(("arbitrary", "arbitrary"), blocks),
        name="merge",
    )(h, w_gate, w_gate, w_gate, ya, yb, yc, wa, wb, wc)


def _residual_matmul_kernel(a_ref, w_ref, r_ref, o_ref):
    @pl.when(pl.program_id(2) == 0)
    def _():
        o_ref[...] = r_ref[...]

    o_ref[...] += jnp.dot(a_ref[...], w_ref[...].astype(BF16), preferred_element_type=F32)


def _residual_matmul(a, w, layer, res, tm=1024, tn=1024, tk=1024):
    m, k = a.shape
    n = w.shape[2]
    blocks = [((tm, tk), BF16), ((tk, tn), F32), ((tm, tn), F32), ((tm, tn), F32)]
    return pl.pallas_call(
        _residual_matmul_kernel,
        grid=(_tiles(m, tm), _tiles(n, tn), _tiles(k, tk)),
        in_specs=[pl.BlockSpec((tm, tk), lambda i, j, kk: (i, kk)),
                  _layer_weight_spec(layer, tk, tn, lambda i, j, kk: (kk, j)),
                  pl.BlockSpec((tm, tn), lambda i, j, kk: (i, j))],
        out_specs=pl.BlockSpec((tm, tn), lambda i, j, kk: (i, j)),
        out_shape=jax.ShapeDtypeStruct((m, n), F32),
        compiler_params=_params(("arbitrary", "arbitrary", "arbitrary"), blocks),
        name="residual_matmul",
    )(a, w, res)


def _relu2_matmul_kernel(a_ref, w_ref, o_ref):
    up = jnp.maximum(jnp.dot(a_ref[...], w_ref[...].astype(BF16), preferred_element_type=F32), 0.0)
    o_ref[...] = (up * up).astype(o_ref.dtype)


def _relu2_matmul(a, w, layer, tm=1024, tn=1024):
    m, k = a.shape
    n = w.shape[2]
    blocks = [((tm, k), BF16), ((k, tn), F32), ((tm, tn), BF16)]
    return pl.pallas_call(
        _relu2_matmul_kernel,
        grid=(_tiles(m, tm), _tiles(n, tn)),
        in_specs=[pl.BlockSpec((tm, k), lambda i, j: (i, 0)),
                  _layer_weight_spec(layer, k, tn, lambda i, j: (0, j))],
        out_specs=pl.BlockSpec((tm, tn), lambda i, j: (i, j)),
        out_shape=jax.ShapeDtypeStruct((m, n), BF16),
        compiler_params=_params(("arbitrary", "arbitrary"), blocks),
        name="mlp_up",
    )(a, w)


def _ple_kernel(h_ref, wg_ref, p_ref, wp_ref, r_ref, o_ref):
    gate = jax.nn.sigmoid(jnp.dot(h_ref[...], wg_ref[...].astype(BF16), preferred_element_type=F32))
    emb = jnp.dot(p_ref[...].astype(BF16), wp_ref[...].astype(BF16), preferred_element_type=F32)
    o_ref[...] = r_ref[...] + gate * emb


def _ple(h, w_gate, p, w_proj, layer, res, tm=1024, tn=512):
    m, d = h.shape
    pd = p.shape[2]
    blocks = [((tm, d), BF16), ((d, tn), F32), ((tm, pd), F32), ((pd, tn), F32),
              ((tm, tn), F32), ((tm, tn), F32)]
    return pl.pallas_call(
        _ple_kernel,
        grid=(_tiles(m, tm), _tiles(d, tn)),
        in_specs=[pl.BlockSpec((tm, d), lambda i, j: (i, 0)),
                  _layer_weight_spec(layer, d, tn, lambda i, j: (0, j)),
                  pl.BlockSpec((None, tm, pd), lambda i, j: (layer, i, 0)),
                  _layer_weight_spec(layer, pd, tn, lambda i, j: (0, j)),
                  pl.BlockSpec((tm, tn), lambda i, j: (i, j))],
        out_specs=pl.BlockSpec((tm, tn), lambda i, j: (i, j)),
        out_shape=jax.ShapeDtypeStruct((m, d), F32),
        compiler_params=_params(("arbitrary", "arbitrary"), blocks),
        name="ple",
    )(h, w_gate, p, w_proj, res)


def kernel(x, p, norm_mix_g, w_in, mlstm_gate_b, mlstm_norm_g, gmlp_norm_g, gmlp_norm_b, gmlp_ws, gmlp_bs,
           w_branch_a, w_branch_b, w_branch_c, w_out, norm_mlp_g, w_mlp_up, w_mlp_down, norm_ple_g,
           w_ple_gate, w_ple_proj, final_norm_g):
    nbatch, seq, d = x.shape
    depth = w_in.shape[0]
    m = nbatch * seq
    assert d == MLSTM_WIDTH + MOBA_WIDTH + GMLP_WIDTH

    qkvo_a = 4 * MLSTM_WIDTH
    gates_if = 2 * MLSTM_HEADS
    main_b = qkvo_a + gates_if
    main_cols = 3 * MOBA_WIDTH + 2 * GMLP_WIDTH
    gate_off = main_b + main_cols
    assert w_in.shape[2] == gate_off + N_BRANCHES * d

    moba_q_cb = qkvo_a // MOBA_WIDTH
    moba_k_cb = moba_q_cb + 1
    moba_v_cb = moba_k_cb + 1
    gmlp_u_cb = (qkvo_a + 3 * MOBA_WIDTH) // GMLP_WIDTH
    gmlp_v_cb = gmlp_u_cb + 1

    z_cols = qkvo_a + main_cols
    colscale = jnp.ones((1, z_cols), F32).at[:, MLSTM_WIDTH:2 * MLSTM_WIDTH].set(HEAD_DIM ** -0.5)
    rope_tables = _rope_tables(seq) + _rope_tables_t(seq)
    mask_cols = _moba_mask_columns(seq)

    xf = x.reshape(m, d)
    p_flat = p.reshape(depth, m, p.shape[-1])
    w_main, w_if, w_gate = _win_split(w_in, qkvo_a, gates_if, z_cols)
    for i in range(depth):
        h = _rmsnorm(xf, norm_mix_g[i], BF16)
        z, zif = _inproj(h, w_main, w_if, i, colscale)
        z3 = z.reshape(nbatch, seq, z_cols)
        zif3 = zif.reshape(nbatch, seq, GATE_LANES)

        ya = _mlstm(z3, zif3, mlstm_gate_b[i], mlstm_norm_g[i])
        yb = _moba_attn_t(*_moba_prep_t(z3, rope_tables, moba_q_cb, moba_k_cb, moba_v_cb), mask_cols)
        yc = _gmlp(z3, gmlp_u_cb, gmlp_v_cb, gmlp_norm_g[i], gmlp_norm_b[i], gmlp_ws[i], gmlp_bs[i])

        merged = _merge(h, w_gate, ya.reshape(m, -1), yb.reshape(m, -1), yc.reshape(m, -1),
                        w_branch_a, w_branch_b, w_branch_c, i)
        xf = _residual_matmul(merged, w_out, i, xf, tm=2048)

        h2 = _rmsnorm(xf, norm_mlp_g[i], BF16)
        hidden = _relu2_matmul(h2, w_mlp_up, i)
        xf = _residual_matmul(hidden, w_mlp_down, i, xf, tm=2048)

        h3 = _rmsnorm(xf, norm_ple_g[i], BF16)
        xf = _ple(h3, w_ple_gate, p_flat, w_ple_proj, i, xf, tm=2048)

    return _rmsnorm(xf, final_norm_g, F32).reshape(nbatch, seq, d)
```

```python
import functools

import jax
import jax.numpy as jnp
import numpy as np
from jax import lax
from jax.experimental import pallas as pl
from jax.experimental.pallas import tpu as pltpu

F32 = jnp.float32
BF16 = jnp.bfloat16

HEAD_DIM = 128
MLSTM_HEADS = 4
MLSTM_WIDTH = MLSTM_HEADS * HEAD_DIM
MOBA_HEADS = 8
MOBA_WIDTH = MOBA_HEADS * HEAD_DIM
MOBA_BLOCK = 256
MOBA_TOPK = 3
ROPE_THETA = 500000.0
ROPE_DIM = HEAD_DIM // 4
GMLP_WIDTH = 512
GMLP_GROUPS = 4
GMLP_CHUNK = 128
N_BRANCHES = 3
NORM_EPS = 1e-6

LANES = 128
V7X_VMEM_BYTES = 64 * 1024 * 1024
VMEM_CEILING = V7X_VMEM_BYTES - 8 * 1024 * 1024

MLSTM_KERNEL_CHUNK = 128
GATE_LANES = LANES
MASK_BIG = 2.0 ** 100
MOBA_GROUP = 4

NT_DIMS = (((1,), (1,)), ((), ()))
TN_DIMS = (((0,), (0,)), ((), ()))


def _tiles(n, t):
    count, rest = divmod(n, t)
    assert rest == 0 and count > 0, (n, t)
    return count


def _nbytes(shape, dtype):
    return int(np.prod(shape)) * jnp.dtype(dtype).itemsize


def _params(semantics, blocks, scratch=()):
    need = 2 * sum(_nbytes(s, d) for s, d in blocks) + sum(_nbytes(s, d) for s, d in scratch)
    limit = min(VMEM_CEILING, need + need // 4 + 4 * 1024 * 1024)
    return pltpu.CompilerParams(dimension_semantics=semantics, vmem_limit_bytes=limit)


def _rmsnorm_kernel(x_ref, g_ref, o_ref):
    x = x_ref[...]
    y = x * lax.rsqrt(jnp.mean(x * x, axis=-1, keepdims=True) + NORM_EPS)
    o_ref[...] = (y * g_ref[...]).astype(o_ref.dtype)


def _rmsnorm(x, g, out_dtype, tm=512):
    m, d = x.shape
    return pl.pallas_call(
        _rmsnorm_kernel,
        grid=(_tiles(m, tm),),
        in_specs=[pl.BlockSpec((tm, d), lambda i: (i, 0)),
                  pl.BlockSpec((1, d), lambda i: (0, 0))],
        out_specs=pl.BlockSpec((tm, d), lambda i: (i, 0)),
        out_shape=jax.ShapeDtypeStruct((m, d), out_dtype),
        compiler_params=_params(("arbitrary",), [((tm, d), F32), ((tm, d), out_dtype)]),
        name="rmsnorm",
    )(x, g.reshape(1, d))


def _win_split_kernel(wm_ref, wif_ref, wg_ref, main_ref, if_ref, gate_ref):
    main_ref[...] = wm_ref[0].T.astype(BF16)
    gate_ref[...] = wg_ref[0].T.astype(BF16)
    _, n_if, k = wif_ref.shape
    padded = jnp.concatenate([wif_ref[0], jnp.zeros((GATE_LANES - n_if, k), F32)], axis=0)
    if_ref[...] = padded.T.astype(BF16)


def _win_split(w_in, head, n_if, main_cols, tn=512):
    depth, k, cols = w_in.shape
    gate_cols = cols - main_cols - n_if
    assert gate_cols == main_cols and head % tn == 0
    w_t = jnp.swapaxes(w_in, 1, 2)
    head_steps = head // tn
    blocks = [((tn, k), F32), ((n_if, k), F32), ((tn, k), F32), ((k, tn), BF16), ((k, GATE_LANES), BF16),
              ((k, tn), BF16)]

    def main_rows(l, r):
        return (l, pl.multiple_of(r * tn + jnp.where(r >= head_steps, n_if, 0), n_if), 0)

    def window(rows, start_of):
        return pl.BlockSpec((pl.Element(1), pl.Element(rows), pl.Element(k)), start_of)

    return pl.pallas_call(
        _win_split_kernel,
        grid=(depth, _tiles(main_cols, tn)),
        in_specs=[window(tn, main_rows),
                  window(n_if, lambda l, r: (l, head, 0)),
                  window(tn, lambda l, r: (l, pl.multiple_of(main_cols + n_if + r * tn, n_if), 0))],
        out_specs=[pl.BlockSpec((None, k, tn), lambda l, r: (l, 0, r)),
                   pl.BlockSpec((None, k, GATE_LANES), lambda l, r: (l, 0, 0)),
                   pl.BlockSpec((None, k, tn), lambda l, r: (l, 0, r))],
        out_shape=[jax.ShapeDtypeStruct((depth, k, main_cols), BF16),
                   jax.ShapeDtypeStruct((depth, k, GATE_LANES), BF16),
                   jax.ShapeDtypeStruct((depth, k, gate_cols), BF16)],
        compiler_params=_params(("arbitrary", "arbitrary"), blocks),
        name="win_split",
    )(w_t, w_t, w_t)


def _inproj_kernel(h_ref, w_ref, wif_ref, cs_ref, z_ref, zif_ref):
    acc = jnp.dot(h_ref[...], w_ref[...], preferred_element_type=F32)
    z_ref[...] = acc * cs_ref[...]

    @pl.when(pl.program_id(1) == 0)
    def _():
        zif_ref[...] = jnp.dot(h_ref[...], wif_ref[...], preferred_element_type=F32)


def _inproj(h, w_main, w_if, layer, colscale, tm=1024, tn=512):
    m, k = h.shape
    n = w_main.shape[2]
    blocks = [((tm, k), BF16), ((k, tn), BF16), ((k, GATE_LANES), BF16), ((1, tn), F32),
              ((tm, tn), F32), ((tm, GATE_LANES), F32)]
    return pl.pallas_call(
        _inproj_kernel,
        grid=(_tiles(m, tm), _tiles(n, tn)),
        in_specs=[pl.BlockSpec((tm, k), lambda i, j: (i, 0)),
                  pl.BlockSpec((None, k, tn), lambda i, j: (layer, 0, j)),
                  pl.BlockSpec((None, k, GATE_LANES), lambda i, j: (layer, 0, 0)),
                  pl.BlockSpec((1, tn), lambda i, j: (0, j))],
        out_specs=[pl.BlockSpec((tm, tn), lambda i, j: (i, j)),
                   pl.BlockSpec((tm, GATE_LANES), lambda i, j: (i, 0))],
        out_shape=[jax.ShapeDtypeStruct((m, n), F32),
                   jax.ShapeDtypeStruct((m, GATE_LANES), F32)],
        compiler_params=_params(("arbitrary", "arbitrary"), blocks),
        name="in_proj",
    )(h, w_main, w_if, colscale)


def _log_sigmoid(x):
    return jnp.minimum(x, 0.0) - jnp.log1p(jnp.exp(-jnp.abs(x)))


def _mlstm_kernel(q_ref, k_ref, v_ref, o_ref, zif_ref, gb_ref, ng_ref, y_ref, c_sc, n_sc, m_sc):
    nbatch, chunk, _ = q_ref.shape
    heads = MLSTM_HEADS

    @pl.when(pl.program_id(0) == 0)
    def _():
        c_sc[...] = jnp.zeros_like(c_sc)
        n_sc[...] = jnp.zeros_like(n_sc)
        m_sc[...] = jnp.zeros_like(m_sc)

    row = lax.broadcasted_iota(jnp.int32, (chunk, chunk), 0)
    col = lax.broadcasted_iota(jnp.int32, (chunk, chunk), 1)
    causal = col <= row
    tril = jnp.where(causal, 1.0, 0.0).astype(F32)
    lane = lax.broadcasted_iota(jnp.int32, (chunk, GATE_LANES), 1)

    for b in range(nbatch):
        pre = zif_ref[b] + gb_ref[...]
        gates = jnp.where(lane < heads, pre, _log_sigmoid(pre))
        gcum = jnp.dot(tril, gates, preferred_element_type=F32,
                       precision=lax.Precision.HIGHEST)
        gates_t = gates.T
        gcum_t = gcum.T
        for h in range(heads):
            s = b * heads + h
            sl = slice(h * HEAD_DIM, (h + 1) * HEAD_DIM)
            q = q_ref[b, :, sl]
            k = k_ref[b, :, sl]
            v = v_ref[b, :, sl]
            qb = q.astype(BF16)
            kb = k.astype(BF16)
            vb = v.astype(BF16)
            g_t = gcum[:, heads + h:heads + h + 1]
            g_s = gcum_t[heads + h:heads + h + 1, :]
            i_s = gates_t[h:h + 1, :]
            i_t = gates[:, h:h + 1]
            m_prev = m_sc[s][:, 0:1]
            c_prev = c_sc[s]
            n_prev = n_sc[s]

            log_w = jnp.where(causal, g_t - g_s + i_s, -jnp.inf)
            log_a = g_t + m_prev
            m_row = jnp.maximum(jnp.max(log_w, axis=1, keepdims=True), log_a)
            qk = lax.dot_general(qb, kb, NT_DIMS, preferred_element_type=F32) * jnp.exp(log_w - m_row)
            a = jnp.exp(log_a - m_row)
            num = (jnp.dot(qk.astype(BF16), vb, preferred_element_type=F32)
                   + a * lax.dot_general(qb, c_prev.astype(BF16), NT_DIMS, preferred_element_type=F32))
            den = (jnp.sum(qk, axis=1, keepdims=True)
                   + a * jnp.sum(q * n_prev, axis=1, keepdims=True))
            h_out = num / jnp.maximum(jnp.abs(den), jnp.exp(-m_row))

            g_last = gcum[chunk - 1:chunk, heads + h:heads + h + 1]
            log_u = g_last - g_t + i_t
            m_new = jnp.maximum(g_last + m_prev, jnp.max(log_u, axis=0, keepdims=True))
            decay = jnp.exp(g_last + m_prev - m_new)
            u = jnp.exp(log_u - m_new)
            c_sc[s] = decay * c_prev + lax.dot_general((u * v).astype(BF16), kb, TN_DIMS,
                                                       preferred_element_type=F32)
            n_sc[s] = decay * n_prev + jnp.sum(u * k, axis=0, keepdims=True)
            m_sc[s] = jnp.broadcast_to(m_new, (1, LANES))

            yn = h_out * lax.rsqrt(jnp.mean(h_out * h_out, axis=-1, keepdims=True) + NORM_EPS)
            y_ref[b, :, sl] = (jax.nn.sigmoid(o_ref[b, :, sl]) * (yn * ng_ref[:, sl])).astype(y_ref.dtype)


def _mlstm(z3, zif3, gate_b, norm_g):
    nbatch, seq, _ = z3.shape
    chunk = MLSTM_KERNEL_CHUNK
    w = MLSTM_WIDTH
    streams = nbatch * MLSTM_HEADS
    gb = jnp.pad(gate_b, (0, GATE_LANES - gate_b.shape[0])).reshape(1, GATE_LANES)
    blocks = [((nbatch, chunk, w), F32)] * 4 + [((nbatch, chunk, GATE_LANES), F32),
                                               ((nbatch, chunk, w), BF16)]
    scratch = [((streams, HEAD_DIM, HEAD_DIM), F32), ((streams, 1, LANES), F32), ((streams, 1, LANES), F32)]

    def zcol(cb):
        return pl.BlockSpec((nbatch, chunk, w), lambda c: (0, c, cb))

    return pl.pallas_call(
        _mlstm_kernel,
        grid=(_tiles(seq, chunk),),
        in_specs=[zcol(0), zcol(1), zcol(2), zcol(3),
                  pl.BlockSpec((nbatch, chunk, GATE_LANES), lambda c: (0, c, 0)),
                  pl.BlockSpec((1, GATE_LANES), lambda c: (0, 0)),
                  pl.BlockSpec((1, w), lambda c: (0, 0))],
        out_specs=pl.BlockSpec((nbatch, chunk, w), lambda c: (0, c, 0)),
        out_shape=jax.ShapeDtypeStruct((nbatch, seq, w), BF16),
        scratch_shapes=[pltpu.VMEM(s, d) for s, d in scratch],
        compiler_params=_params(("arbitrary",), blocks, scratch),
        name="mlstm",
    )(z3, z3, z3, z3, zif3, gb, norm_g.reshape(1, w))


def _rope_tables(seq):
    half = ROPE_DIM // 2
    inv_freq = ROPE_THETA ** (-jnp.arange(0, ROPE_DIM, 2, dtype=F32) / ROPE_DIM)
    ang = jnp.arange(seq, dtype=F32)[:, None] * inv_freq[None, :]
    cos = jnp.cos(ang)
    sin = jnp.sin(ang)
    ones = jnp.ones((seq, HEAD_DIM - ROPE_DIM), F32)
    cos_tab = jnp.concatenate([cos, cos, ones], axis=1)
    sin_tab = jnp.concatenate([-sin, sin, 0.0 * ones], axis=1)
    assert cos_tab.shape == (seq, HEAD_DIM) and half * 2 == ROPE_DIM
    return cos_tab, sin_tab


def _rotary(t, cos, sin):
    half = ROPE_DIM // 2
    lane = lax.broadcasted_iota(jnp.int32, t.shape, 1)
    upper = pltpu.roll(t, HEAD_DIM - half, axis=1)
    lower = pltpu.roll(t, half, axis=1)
    partner = jnp.where(lane < half, upper, lower)
    return jnp.where(lane < ROPE_DIM, t * cos + partner * sin, t)


def _moba_prep_kernel(q_ref, k_ref, v_ref, cos_ref, sin_ref, qa_ref, kb_ref, vb_ref, km_sc):
    blk = q_ref.shape[1]
    d = HEAD_DIM
    j = pl.program_id(1)

    @pl.when(j == 0)
    def _():
        km_sc[...] = jnp.zeros_like(km_sc)

    cos = cos_ref[...]
    sin = sin_ref[...]
    lane = lax.broadcasted_iota(jnp.int32, (blk, LANES), 1)
    lane_f = lane.astype(F32)
    mean_row = lax.broadcasted_iota(jnp.int32, (LANES, d), 0)
    vb_ref[0] = v_ref[0].astype(BF16)
    for h in range(MOBA_HEADS):
        sl = slice(h * d, (h + 1) * d)
        qf = _rotary(q_ref[0, :, sl], cos, sin)
        kk = _rotary(k_ref[0, :, sl], cos, sin)
        kb_ref[0, :, sl] = kk.astype(BF16)

        gate = lax.dot_general(qf, km_sc[h], NT_DIMS, preferred_element_type=F32,
                               precision=lax.Precision.HIGHEST)
        gate = jnp.where(lane < j, gate, -jnp.inf)
        sel_m1 = jnp.full((blk, LANES), -1.0, F32)
        for _ in range(MOBA_TOPK):
            mx = jnp.max(gate, axis=1, keepdims=True)
            first = jnp.min(jnp.where(gate == mx, lane_f, float(LANES)), axis=1, keepdims=True)
            first = jnp.where(mx > -jnp.inf, first, -1.0)
            hit = lane_f == first
            sel_m1 = jnp.where(hit, 0.0, sel_m1)
            gate = jnp.where(hit, -jnp.inf, gate)
        qa_ref[0, :, 2 * h * d:(2 * h + 1) * d] = qf.astype(BF16)
        qa_ref[0, :, (2 * h + 1) * d:(2 * h + 2) * d] = sel_m1.astype(BF16)

        km_sc[h] = jnp.where(mean_row == j, jnp.mean(kk, axis=0, keepdims=True), km_sc[h])


def _moba_prep(z3, cos_tab, sin_tab, q_cb, k_cb, v_cb):
    nbatch, seq, _ = z3.shape
    blk = MOBA_BLOCK
    w = MOBA_WIDTH
    d = HEAD_DIM
    blocks = [((1, blk, w), F32)] * 3 + [((blk, d), F32)] * 2 + [((1, blk, 2 * w), BF16), ((1, blk, w), BF16),
                                                                  ((1, blk, w), BF16)]
    scratch = [((MOBA_HEADS, LANES, d), F32)]
    return pl.pallas_call(
        _moba_prep_kernel,
        grid=(nbatch, _tiles(seq, blk)),
        in_specs=[pl.BlockSpec((1, blk, w), lambda b, j: (b, j, q_cb)),
                  pl.BlockSpec((1, blk, w), lambda b, j: (b, j, k_cb)),
                  pl.BlockSpec((1, blk, w), lambda b, j: (b, j, v_cb)),
                  pl.BlockSpec((blk, d), lambda b, j: (j, 0)),
                  pl.BlockSpec((blk, d), lambda b, j: (j, 0))],
        out_specs=[pl.BlockSpec((1, blk, 2 * w), lambda b, j: (b, j, 0)),
                   pl.BlockSpec((1, blk, w), lambda b, j: (b, j, 0)),
                   pl.BlockSpec((1, blk, w), lambda b, j: (b, j, 0))],
        out_shape=[jax.ShapeDtypeStruct((nbatch, seq, 2 * w), BF16),
                   jax.ShapeDtypeStruct((nbatch, seq, w), BF16),
                   jax.ShapeDtypeStruct((nbatch, seq, w), BF16)],
        scratch_shapes=[pltpu.VMEM(s, dt) for s, dt in scratch],
        compiler_params=_params(("arbitrary", "arbitrary"), blocks, scratch),
        name="moba_prep",
    )(z3, z3, z3, cos_tab, sin_tab)


def _moba_attn_kernel(q_ref, k_ref, v_ref, o_ref):
    blk = q_ref.shape[1]
    d = HEAD_DIM
    nblk = k_ref.shape[1] // blk
    group = MOBA_GROUP
    j = pl.program_id(2)
    scale = d ** -0.5
    q_aug = q_ref[0]
    lane = lax.broadcasted_iota(jnp.int32, (blk, LANES), 1)

    def softmax_partial(s, v_rows):
        m = jnp.max(s, axis=1, keepdims=True)
        p = jnp.exp(s - m)
        return m, jnp.sum(p, axis=1, keepdims=True), jnp.dot(p.astype(BF16), v_rows, preferred_element_type=F32)

    def past_partial(n):
        c0 = pl.multiple_of(jnp.minimum(n, nblk - 1) * blk, blk)
        onehot = jnp.where(lane == n, MASK_BIG, 0.0).astype(BF16)
        k_aug = jnp.concatenate([k_ref[0, pl.ds(c0, blk), :], onehot], axis=1)
        s = lax.dot_general(q_aug, k_aug, NT_DIMS, preferred_element_type=F32) * scale
        return softmax_partial(s, v_ref[0, pl.ds(c0, blk), :])

    def merge(state, parts):
        m_run, l_run, acc_run = state
        m_new = m_run
        for m, _, _ in parts:
            m_new = jnp.maximum(m_new, m)
        w = jnp.exp(m_run - m_new)
        l_new = w * l_run
        acc = w * acc_run
        for m, l, a in parts:
            w = jnp.exp(m - m_new)
            l_new = l_new + w * l
            acc = acc + w * a
        return m_new, l_new, acc

    r0 = pl.multiple_of(j * blk, blk)
    row = lax.broadcasted_iota(jnp.int32, (blk, blk), 0)
    col = lax.broadcasted_iota(jnp.int32, (blk, blk), 1)
    s_own = lax.dot_general(q_aug[:, :d], k_ref[0, pl.ds(r0, blk), :], NT_DIMS,
                            preferred_element_type=F32) * scale
    s_own = jnp.where(col <= row, s_own, -jnp.inf)
    state = softmax_partial(s_own, v_ref[0, pl.ds(r0, blk), :])

    state = merge(state, [past_partial(n) for n in range(group)])
    ngroups = lax.div(j + (group - 1), group)

    def group_body(g, st):
        return merge(st, [past_partial(g * group + i) for i in range(group)])

    _, l_fin, acc = lax.fori_loop(1, ngroups, group_body, state)
    o_ref[0] = (acc / l_fin).astype(o_ref.dtype)


def _moba_attn(q_aug, kb, vb):
    nbatch, seq, w = kb.shape
    blk = MOBA_BLOCK
    d = HEAD_DIM
    blocks = [((1, blk, 2 * d), BF16), ((1, seq, d), BF16), ((1, seq, d), BF16), ((1, blk, d), BF16)]
    return pl.pallas_call(
        _moba_attn_kernel,
        grid=(nbatch, MOBA_HEADS, _tiles(seq, blk)),
        in_specs=[pl.BlockSpec((1, blk, 2 * d), lambda b, h, j: (b, j, h)),
                  pl.BlockSpec((1, seq, d), lambda b, h, j: (b, 0, h)),
                  pl.BlockSpec((1, seq, d), lambda b, h, j: (b, 0, h))],
        out_specs=pl.BlockSpec((1, blk, d), lambda b, h, j: (b, j, h)),
        out_shape=jax.ShapeDtypeStruct((nbatch, seq, w), BF16),
        compiler_params=_params(("arbitrary", "arbitrary", "arbitrary"), blocks),
        name="moba_attn",
    )(q_aug, kb, vb)


def _rope_tables_t(seq):
    inv_freq = ROPE_THETA ** (-jnp.arange(0, ROPE_DIM, 2, dtype=F32) / ROPE_DIM)
    ang = jnp.arange(seq, dtype=F32)[:, None] * inv_freq[None, :]
    cos = jnp.cos(ang).T
    sin = jnp.sin(ang).T
    return jnp.concatenate([cos, cos], axis=0), jnp.concatenate([-sin, sin], axis=0)


def _moba_prep_t_kernel(q_ref, k_ref, v_ref, cos_ref, sin_ref, cost_ref, sint_ref, qa_ref, kb_ref, vt_ref, km_sc):
    blk = q_ref.shape[1]
    d = HEAD_DIM
    half = ROPE_DIM // 2
    nsel = km_sc.shape[1]
    j = pl.program_id(1)

    @pl.when(j == 0)
    def _():
        km_sc[...] = jnp.zeros_like(km_sc)

    cos = cos_ref[...]
    sin = sin_ref[...]
    cos_t = cost_ref[...]
    sin_t = sint_ref[...]
    blk_id = lax.broadcasted_iota(jnp.int32, (nsel, blk), 0)
    blk_id_f = blk_id.astype(F32)
    mean_row = lax.broadcasted_iota(jnp.int32, (nsel, d), 0)
    for h in range(MOBA_HEADS):
        sl = slice(h * d, (h + 1) * d)
        q_t = q_ref[0, :, sl].T
        top = q_t[:ROPE_DIM]
        partner = jnp.concatenate([top[half:], top[:half]], axis=0)
        q_t = jnp.concatenate([top * cos_t + partner * sin_t, q_t[ROPE_DIM:]], axis=0)

        gate = jnp.dot(km_sc[h], q_t, preferred_element_type=F32,
                       precision=lax.Precision.HIGHEST)
        gate = jnp.where(blk_id < j, gate, -jnp.inf)
        sel_m1 = jnp.full((nsel, blk), -1.0, F32)
        for _ in range(MOBA_TOPK):
            mx = jnp.max(gate, axis=0, keepdims=True)
            first = jnp.min(jnp.where(gate == mx, blk_id_f, float(nsel)), axis=0, keepdims=True)
            first = jnp.where(mx > -jnp.inf, first, -1.0)
            hit = blk_id_f == first
            sel_m1 = jnp.where(hit, 0.0, sel_m1)
            gate = jnp.where(hit, -jnp.inf, gate)
        qa_ref[0, h, 0, 0:d, :] = q_t.astype(BF16)
        qa_ref[0, h, 0, d:d + nsel, :] = sel_m1.astype(BF16)
        qa_ref[0, h, 0, d + nsel:, :] = jnp.full((d - nsel, blk), -1.0, BF16)

        kk = _rotary(k_ref[0, :, sl], cos, sin)
        kb_ref[0, :, sl] = kk.astype(BF16)
        km_sc[h] = jnp.where(mean_row == j, jnp.mean(kk, axis=0, keepdims=True), km_sc[h])

        vt_ref[0, h, 0] = v_ref[0, :, sl].T.astype(BF16)


def _moba_prep_t(z3, tables, q_cb, k_cb, v_cb):
    nbatch, seq, _ = z3.shape
    blk = MOBA_BLOCK
    nblk = _tiles(seq, blk)
    nsel = -(-nblk // 16) * 16
    assert nsel <= HEAD_DIM
    w = MOBA_WIDTH
    d = HEAD_DIM
    hh = MOBA_HEADS
    cos_tab, sin_tab, cos_t, sin_t = tables
    blocks = ([((1, blk, w), F32)] * 3 + [((blk, d), F32)] * 2 + [((ROPE_DIM, blk), F32)] * 2
              + [((hh, 2 * d, blk), BF16), ((1, blk, w), BF16), ((hh, d, blk), BF16)])
    scratch = [((hh, nsel, d), F32)]
    return pl.pallas_call(
        _moba_prep_t_kernel,
        grid=(nbatch, nblk),
        in_specs=[pl.BlockSpec((1, blk, w), lambda b, j: (b, j, q_cb)),
                  pl.BlockSpec((1, blk, w), lambda b, j: (b, j, k_cb)),
                  pl.BlockSpec((1, blk, w), lambda b, j: (b, j, v_cb)),
                  pl.BlockSpec((blk, d), lambda b, j: (j, 0)),
                  pl.BlockSpec((blk, d), lambda b, j: (j, 0)),
                  pl.BlockSpec((ROPE_DIM, blk), lambda b, j: (0, j)),
                  pl.BlockSpec((ROPE_DIM, blk), lambda b, j: (0, j))],
        out_specs=[pl.BlockSpec((1, hh, 1, 2 * d, blk), lambda b, j: (b, 0, j, 0, 0)),
                   pl.BlockSpec((1, blk, w), lambda b, j: (b, j, 0)),
                   pl.BlockSpec((1, hh, 1, d, blk), lambda b, j: (b, 0, j, 0, 0))],
        out_shape=[jax.ShapeDtypeStruct((nbatch, hh, nblk, 2 * d, blk), BF16),
                   jax.ShapeDtypeStruct((nbatch, seq, w), BF16),
                   jax.ShapeDtypeStruct((nbatch, hh, nblk, d, blk), BF16)],
        scratch_shapes=[pltpu.VMEM(s, dt) for s, dt in scratch],
        compiler_params=_params(("arbitrary", "arbitrary"), blocks, scratch),
        name="moba_prep",
    )(z3, z3, z3, cos_tab, sin_tab, cos_t, sin_t)


def _moba_attn_t_kernel(q_ref, k_ref, mask_ref, vt_ref, vown_ref, o_ref, s_sc):
    blk = q_ref.shape[-1]
    d = HEAD_DIM
    gkeys = vt_ref.shape[-1]
    group = gkeys // blk
    last_group = vt_ref.shape[2] - 1
    j = pl.program_id(2)
    c_exp = (d ** -0.5) * np.log2(np.e).astype(np.float32)
    q_aug = q_ref[0, 0, 0]

    def group_scores(g, slot):
        c0 = pl.multiple_of(g * gkeys, gkeys)
        k_aug = jnp.concatenate([k_ref[0, pl.ds(c0, gkeys), :], mask_ref[pl.ds(c0, gkeys), :]], axis=1)
        s_t = jnp.dot(k_aug, q_aug, preferred_element_type=F32)
        s_sc[slot] = s_t
        return jnp.max(s_t, axis=0, keepdims=True)

    r0 = pl.multiple_of(j * blk, blk)
    key = lax.broadcasted_iota(jnp.int32, (blk, blk), 0)
    qry = lax.broadcasted_iota(jnp.int32, (blk, blk), 1)
    s_own = jnp.dot(k_ref[0, pl.ds(r0, blk), :], q_aug[:d], preferred_element_type=F32)
    s_own = jnp.where(key <= qry, s_own, -jnp.inf)
    m0 = jnp.max(s_own, axis=0, keepdims=True)
    p0 = jnp.exp2((s_own - m0) * c_exp)
    l0 = jnp.sum(p0, axis=0, keepdims=True)
    acc0 = jnp.dot(vown_ref[0, 0, 0], p0.astype(BF16), preferred_element_type=F32)

    def fold_group(g, state, slot):
        m_run, l_run, acc, s_max = state
        next_max = group_scores(jnp.minimum(g + 1, last_group), 1 - slot)
        m_new = jnp.maximum(m_run, s_max)
        alpha = jnp.exp2((m_run - m_new) * c_exp)
        p = jnp.exp2((s_sc[slot] - m_new) * c_exp)
        l_new = alpha * l_run + jnp.sum(p, axis=0, keepdims=True)
        acc = alpha * acc + jnp.dot(vt_ref[0, 0, g], p.astype(BF16), preferred_element_type=F32)
        return m_new, l_new, acc, next_max

    def group_body(g, state):
        return lax.cond(g % 2 == 0, lambda st: fold_group(g, st, 0), lambda st: fold_group(g, st, 1), state)

    ngroups = lax.div(j + (group - 1), group)
    _, l_fin, acc, _ = lax.fori_loop(0, ngroups, group_body, (m0, l0, acc0, group_scores(0, 0)))
    o_ref[0] = (acc / l_fin).T.astype(o_ref.dtype)


def _moba_mask_columns(seq):
    blk_id = jnp.arange(seq, dtype=jnp.int32)[:, None] // MOBA_BLOCK
    return jnp.where(jnp.arange(LANES, dtype=jnp.int32)[None, :] == blk_id, MASK_BIG, 0.0).astype(BF16)


def _moba_attn_t(q_aug, kb, vt, mask_cols):
    nbatch, seq, w = kb.shape
    blk = MOBA_BLOCK
    d = HEAD_DIM
    ngrp, gkeys = vt.shape[2], vt.shape[4]
    group = gkeys // blk
    blocks = [((2 * d, blk), BF16), ((1, seq, d), BF16), ((seq, LANES), BF16), ((ngrp, d, gkeys), BF16),
              ((d, blk), BF16), ((1, blk, d), BF16)]
    scratch = [((2, gkeys, blk), F32)]
    return pl.pallas_call(
        _moba_attn_t_kernel,
        grid=(nbatch, MOBA_HEADS, _tiles(seq, blk)),
        in_specs=[pl.BlockSpec((1, 1, 1, 2 * d, blk), lambda b, h, j: (b, h, j, 0, 0)),
                  pl.BlockSpec((1, seq, d), lambda b, h, j: (b, 0, h)),
                  pl.BlockSpec((seq, LANES), lambda b, h, j: (0, 0)),
                  pl.BlockSpec((1, 1, ngrp, d, gkeys), lambda b, h, j: (b, h, 0, 0, 0)),
                  pl.BlockSpec((1, 1, 1, d, blk), lambda b, h, j: (b, h, j // group, 0, j % group))],
        out_specs=pl.BlockSpec((1, blk, d), lambda b, h, j: (b, j, h)),
        out_shape=jax.ShapeDtypeStruct((nbatch, seq, w), BF16),
        scratch_shapes=[pltpu.VMEM(s, dt) for s, dt in scratch],
        compiler_params=_params(("arbitrary", "arbitrary", "arbitrary"), blocks, scratch),
        name="moba_attn",
    )(q_aug, kb, mask_cols, vt, vt)


def _pipelined_pairs(count, first, step):
    def pair(t, carry):
        return step(2 * t + 1, 1, step(2 * t, 0, carry))

    carry = lax.fori_loop(0, count // 2, pair, first)
    if count % 2:
        carry = step(count - 1, 0, carry)
    return carry


def _moba_flat_kernel(tile_ref, group_ref, q_ref, k_ref, mask_ref, vt_ref, o_ref,
                      s_sc, acc_sc, m_sc, l_sc):
    nblk, d, blk = vt_ref.shape[2:]
    group = MOBA_GROUP
    gkeys = group * blk
    n_items = tile_ref.shape[0]
    c_exp = (d ** -0.5) * np.log2(np.e).astype(np.float32)
    key = lax.broadcasted_iota(jnp.int32, (blk, blk), 0)
    qry = lax.broadcasted_iota(jnp.int32, (blk, blk), 1)

    def weighted_values(first_blk, p):
        acc = None
        for i in range(p.shape[0] // blk):
            part = jnp.dot(vt_ref[0, 0, first_blk + i], p[i * blk:(i + 1) * blk].astype(BF16),
                           preferred_element_type=F32)
            acc = part if acc is None else acc + part
        return acc

    def own_scores(j, slot):
        r0 = pl.multiple_of(j * blk, blk)
        s_t = jnp.dot(k_ref[0, pl.ds(r0, blk), :], q_ref[0, 0, j, :d, :], preferred_element_type=F32)
        s_t = jnp.where(key <= qry, s_t, -jnp.inf)
        s_sc[slot, :blk, :] = s_t
        return jnp.max(s_t, axis=0, keepdims=True)

    def fold_own(j, slot, s_max):
        next_max = own_scores(jnp.minimum(j + 1, nblk - 1), 1 - slot)
        p = jnp.exp2((s_sc[slot, :blk, :] - s_max) * c_exp)
        m_sc[j] = s_max
        l_sc[j] = jnp.sum(p, axis=0, keepdims=True)
        acc_sc[j] = weighted_values(j, p)
        return next_max

    _pipelined_pairs(nblk, own_scores(0, 0), fold_own)

    def group_scores(i, slot):
        c0 = pl.multiple_of(group_ref[i] * gkeys, gkeys)
        k_aug = jnp.concatenate([k_ref[0, pl.ds(c0, gkeys), :], mask_ref[pl.ds(c0, gkeys), :]], axis=1)
        s_t = jnp.dot(k_aug, q_ref[0, 0, tile_ref[i]], preferred_element_type=F32)
        s_sc[slot] = s_t
        return jnp.max(s_t, axis=0, keepdims=True)

    def fold_group(i, slot, s_max):
        next_max = group_scores(jnp.minimum(i + 1, n_items - 1), 1 - slot)
        j = tile_ref[i]
        m_run = m_sc[j]
        m_new = jnp.maximum(m_run, s_max)
        alpha = jnp.exp2((m_run - m_new) * c_exp)
        p = jnp.exp2((s_sc[slot] - m_new) * c_exp)
        m_sc[j] = m_new
        l_sc[j] = alpha * l_sc[j] + jnp.sum(p, axis=0, keepdims=True)
        acc_sc[j] = alpha * acc_sc[j] + weighted_values(group_ref[i] * group, p)
        return next_max

    _pipelined_pairs(n_items, group_scores(0, 0), fold_group)

    def finish(j, carry):
        r0 = pl.multiple_of(j * blk, blk)
        o_ref[0, pl.ds(r0, blk), :] = (acc_sc[j] / l_sc[j]).T.astype(o_ref.dtype)
        return carry

    lax.fori_loop(0, nblk, finish, 0)


def _moba_flat(q_aug, kb, vt, mask_cols):
    nbatch, seq, w = kb.shape
    blk = MOBA_BLOCK
    d = HEAD_DIM
    nblk = vt.shape[2]
    group = MOBA_GROUP
    items = [(j, g) for j in range(nblk) for g in range(-(-j // group))]
    assert items and _tiles(nblk, group)
    item_tile = jnp.asarray([j for j, _ in items], jnp.int32)
    item_group = jnp.asarray([g for _, g in items], jnp.int32)
    blocks = [((nblk, 2 * d, blk), BF16), ((1, seq, d), BF16), ((seq, LANES), BF16), ((nblk, d, blk), BF16),
              ((1, seq, d), BF16)]
    scratch = [((2, group * blk, blk), F32), ((nblk, d, blk), F32), ((nblk, 1, blk), F32), ((nblk, 1, blk), F32)]
    grid_spec = pltpu.PrefetchScalarGridSpec(
        num_scalar_prefetch=2,
        grid=(nbatch, MOBA_HEADS),
        in_specs=[pl.BlockSpec((1, 1, nblk, 2 * d, blk), lambda b, h, *_: (b, h, 0, 0, 0)),
                  pl.BlockSpec((1, seq, d), lambda b, h, *_: (b, 0, h)),
                  pl.BlockSpec((seq, LANES), lambda b, h, *_: (0, 0)),
                  pl.BlockSpec((1, 1, nblk, d, blk), lambda b, h, *_: (b, h, 0, 0, 0))],
        out_specs=pl.BlockSpec((1, seq, d), lambda b, h, *_: (b, 0, h)),
        scratch_shapes=[pltpu.VMEM(s, dt) for s, dt in scratch])
    return pl.pallas_call(
        _moba_flat_kernel,
        grid_spec=grid_spec,
        out_shape=jax.ShapeDtypeStruct((nbatch, seq, w), BF16),
        compiler_params=_params(("arbitrary", "arbitrary"), blocks, scratch),
        name="moba_attn",
    )(item_tile, item_group, q_aug, kb, mask_cols, vt)


def _gelu_tanh(x):
    c = np.sqrt(2.0 / np.pi).astype(np.float32)
    return x * (0.5 * (1.0 + jnp.tanh(c * (x + 0.044715 * (x * x * x)))))


def _gmlp_kernel(u_ref, v_ref, lg_ref, lb_ref, ws_ref, bst_ref, y_ref):
    rows = u_ref.shape[1]
    t = GMLP_CHUNK
    gd = GMLP_WIDTH // GMLP_GROUPS
    v = _gelu_tanh(v_ref[0])
    mu = jnp.mean(v, axis=-1, keepdims=True)
    vc = v - mu
    vln = vc * lax.rsqrt(jnp.mean(vc * vc, axis=-1, keepdims=True) + NORM_EPS) * lg_ref[...] + lb_ref[...]
    vb = vln.astype(BF16)
    row = lax.broadcasted_iota(jnp.int32, (t, t), 0)
    col = lax.broadcasted_iota(jnp.int32, (t, t), 1)
    for g in range(GMLP_GROUPS):
        wg = jnp.where(col <= row, ws_ref[g], 0.0).astype(BF16)
        bias = bst_ref[:, g:g + 1]
        cols = slice(g * gd, (g + 1) * gd)
        for c in range(rows // t):
            rs = slice(c * t, (c + 1) * t)
            mixed = jnp.dot(wg, vb[rs, cols], preferred_element_type=F32) + bias
            y_ref[0, rs, cols] = (_gelu_tanh(u_ref[0, rs, cols]) * mixed).astype(y_ref.dtype)


def _gmlp(z3, u_cb, v_cb, ln_g, ln_b, ws, bs, rows=512):
    nbatch, seq, _ = z3.shape
    w = GMLP_WIDTH
    t = GMLP_CHUNK
    blocks = [((1, rows, w), F32), ((1, rows, w), F32), ((GMLP_GROUPS, t, t), F32), ((1, rows, w), BF16)]
    return pl.pallas_call(
        _gmlp_kernel,
        grid=(nbatch, _tiles(seq, rows)),
        in_specs=[pl.BlockSpec((1, rows, w), lambda b, c: (b, c, u_cb)),
                  pl.BlockSpec((1, rows, w), lambda b, c: (b, c, v_cb)),
                  pl.BlockSpec((1, w), lambda b, c: (0, 0)),
                  pl.BlockSpec((1, w), lambda b, c: (0, 0)),
                  pl.BlockSpec((GMLP_GROUPS, t, t), lambda b, c: (0, 0, 0)),
                  pl.BlockSpec((t, GMLP_GROUPS), lambda b, c: (0, 0))],
        out_specs=pl.BlockSpec((1, rows, w), lambda b, c: (b, c, 0)),
        out_shape=jax.ShapeDtypeStruct((nbatch, seq, w), BF16),
        compiler_params=_params(("arbitrary", "arbitrary"), blocks),
        name="gmlp",
    )(z3, z3, ln_g.reshape(1, w), ln_b.reshape(1, w), ws, bs.T)


def _layer_weight_spec(layer, k, tn, index_of):
    return pl.BlockSpec((None, k, tn), lambda *idx: (layer,) + index_of(*idx))


def _merge_kernel(h_ref, wga_ref, wgb_ref, wgc_ref, ya_ref, yb_ref, yc_ref, wa_ref, wb_ref, wc_ref, o_ref):
    h = h_ref[...]

    def branch(wg_ref, y_ref, w_ref):
        gate = jax.nn.sigmoid(jnp.dot(h, wg_ref[...], preferred_element_type=F32))
        return gate * jnp.dot(y_ref[...], w_ref[...].astype(BF16), preferred_element_type=F32)

    merged = branch(wga_ref, ya_ref, wa_ref) + branch(wgb_ref, yb_ref, wb_ref) + branch(wgc_ref, yc_ref, wc_ref)
    o_ref[...] = merged.astype(o_ref.dtype)


def _merge(h, w_gate, ya, yb, yc, wa, wb, wc, layer, tm=1024, tn=512):
    m, d = h.shape
    nblk = _tiles(d, tn)
    ka, kb, kc = ya.shape[1], yb.shape[1], yc.shape[1]
    blocks = ([((tm, d), BF16)] + [((d, tn), BF16)] * 3
              + [((tm, ka), BF16), ((tm, kb), BF16), ((tm, kc), BF16)]
              + [((ka, tn), F32), ((kb, tn), F32), ((kc, tn), F32), ((tm, tn), BF16)])

    def gate_spec(branch):
        return pl.BlockSpec((None, d, tn), lambda i, j: (layer, 0, branch * nblk + j))

    def col(i, j):
        return (0, j)

    return pl.pallas_call(
        _merge_kernel,
        grid=(_tiles(m, tm), nblk),
        in_specs=[pl.BlockSpec((tm, d), lambda i, j: (i, 0)),
                  gate_spec(0), gate_spec(1), gate_spec(2),
                  pl.BlockSpec((tm, ka), lambda i, j: (i, 0)),
                  pl.BlockSpec((tm, kb), lambda i, j: (i, 0)),
                  pl.BlockSpec((tm, kc), lambda i, j: (i, 0)),
                  _layer_weight_spec(layer, ka, tn, col),
                  _layer_weight_spec(layer, kb, tn, col),
                  _layer_weight_spec(layer, kc, tn, col)],
        out_specs=pl.BlockSpec((tm, tn), lambda i, j: (i, j)),
        out_shape=jax.ShapeDtypeStruct((m, d), BF16),
        compiler_params=_params(("arbitrary", "arbitrary"), blocks),
        name="merge",
    )(h, w_gate, w_gate, w_gate, ya, yb, yc, wa, wb, wc)


def _residual_matmul_kernel(a_ref, w_ref, r_ref, o_ref):
    @pl.when(pl.program_id(2) == 0)
    def _():
        o_ref[...] = r_ref[...]

    o_ref[...] += jnp.dot(a_ref[...], w_ref[...].astype(BF16), preferred_element_type=F32)


def _residual_matmul(a, w, layer, res, tm=1024, tn=1024, tk=1024):
    m, k = a.shape
    n = w.shape[2]
    blocks = [((tm, tk), BF16), ((tk, tn), F32), ((tm, tn), F32), ((tm, tn), F32)]
    return pl.pallas_call(
        _residual_matmul_kernel,
        grid=(_tiles(m, tm), _tiles(n, tn), _tiles(k, tk)),
        in_specs=[pl.BlockSpec((tm, tk), lambda i, j, kk: (i, kk)),
                  _layer_weight_spec(layer, tk, tn, lambda i, j, kk: (kk, j)),
                  pl.BlockSpec((tm, tn), lambda i, j, kk: (i, j))],
        out_specs=pl.BlockSpec((tm, tn), lambda i, j, kk: (i, j)),
        out_shape=jax.ShapeDtypeStruct((m, n), F32),
        compiler_params=_params(("arbitrary", "arbitrary", "arbitrary"), blocks),
        name="residual_matmul",
    )(a, w, res)


def _relu2_matmul_kernel(a_ref, w_ref, o_ref):
    up = jnp.maximum(jnp.dot(a_ref[...], w_ref[...].astype(BF16), preferred_element_type=F32), 0.0)
    o_ref[...] = (up * up).astype(o_ref.dtype)


def _relu2_matmul(a, w, layer, tm=1024, tn=1024):
    m, k = a.shape
    n = w.shape[2]
    blocks = [((tm, k), BF16), ((k, tn), F32), ((tm, tn), BF16)]
    return pl.pallas_call(
        _relu2_matmul_kernel,
        grid=(_tiles(m, tm), _tiles(n, tn)),
        in_specs=[pl.BlockSpec((tm, k), lambda i, j: (i, 0)),
                  _layer_weight_spec(layer, k, tn, lambda i, j: (0, j))],
        out_specs=pl.BlockSpec((tm, tn), lambda i, j: (i, j)),
        out_shape=jax.ShapeDtypeStruct((m, n), BF16),
        compiler_params=_params(("arbitrary", "arbitrary"), blocks),
        name="mlp_up",
    )(a, w)


def _ple_kernel(h_ref, wg_ref, p_ref, wp_ref, r_ref, o_ref):
    gate = jax.nn.sigmoid(jnp.dot(h_ref[...], wg_ref[...].astype(BF16), preferred_element_type=F32))
    emb = jnp.dot(p_ref[...].astype(BF16), wp_ref[...].astype(BF16), preferred_element_type=F32)
    o_ref[...] = r_ref[...] + gate * emb


def _ple(h, w_gate, p, w_proj, layer, res, tm=1024, tn=512):
    m, d = h.shape
    pd = p.shape[2]
    blocks = [((tm, d), BF16), ((d, tn), F32), ((tm, pd), F32), ((pd, tn), F32),
              ((tm, tn), F32), ((tm, tn), F32)]
    return pl.pallas_call(
        _ple_kernel,
        grid=(_tiles(m, tm), _tiles(d, tn)),
        in_specs=[pl.BlockSpec((tm, d), lambda i, j: (i, 0)),
                  _layer_weight_spec(layer, d, tn, lambda i, j: (0, j)),
                  pl.BlockSpec((None, tm, pd), lambda i, j: (layer, i, 0)),
                  _layer_weight_spec(layer, pd, tn, lambda i, j: (0, j)),
                  pl.BlockSpec((tm, tn), lambda i, j: (i, j))],
        out_specs=pl.BlockSpec((tm, tn), lambda i, j: (i, j)),
        out_shape=jax.ShapeDtypeStruct((m, d), F32),
        compiler_params=_params(("arbitrary", "arbitrary"), blocks),
        name="ple",
    )(h, w_gate, p, w_proj, res)


def kernel(x, p, norm_mix_g, w_in, mlstm_gate_b, mlstm_norm_g, gmlp_norm_g, gmlp_norm_b, gmlp_ws, gmlp_bs,
           w_branch_a, w_branch_b, w_branch_c, w_out, norm_mlp_g, w_mlp_up, w_mlp_down, norm_ple_g,
           w_ple_gate, w_ple_proj, final_norm_g):
    nbatch, seq, d = x.shape
    depth = w_in.shape[0]
    m = nbatch * seq
    assert d == MLSTM_WIDTH + MOBA_WIDTH + GMLP_WIDTH

    qkvo_a = 4 * MLSTM_WIDTH
    gates_if = 2 * MLSTM_HEADS
    main_b = qkvo_a + gates_if
    main_cols = 3 * MOBA_WIDTH + 2 * GMLP_WIDTH
    gate_off = main_b + main_cols
    assert w_in.shape[2] == gate_off + N_BRANCHES * d

    moba_q_cb = qkvo_a // MOBA_WIDTH
    moba_k_cb = moba_q_cb + 1
    moba_v_cb = moba_k_cb + 1
    gmlp_u_cb = (qkvo_a + 3 * MOBA_WIDTH) // GMLP_WIDTH
    gmlp_v_cb = gmlp_u_cb + 1

    z_cols = qkvo_a + main_cols
    colscale = jnp.ones((1, z_cols), F32).at[:, MLSTM_WIDTH:2 * MLSTM_WIDTH].set(HEAD_DIM ** -0.5)
    rope_tables = _rope_tables(seq) + _rope_tables_t(seq)
    mask_cols = _moba_mask_columns(seq)

    xf = x.reshape(m, d)
    p_flat = p.reshape(depth, m, p.shape[-1])
    w_main, w_if, w_gate = _win_split(w_in, qkvo_a, gates_if, z_cols)
    for i in range(depth):
        h = _rmsnorm(xf, norm_mix_g[i], BF16)
        z, zif = _inproj(h, w_main, w_if, i, colscale)
        z3 = z.reshape(nbatch, seq, z_cols)
        zif3 = zif.reshape(nbatch, seq, GATE_LANES)

        ya = _mlstm(z3, zif3, mlstm_gate_b[i], mlstm_norm_g[i])
        yb = _moba_flat(*_moba_prep_t(z3, rope_tables, moba_q_cb, moba_k_cb, moba_v_cb), mask_cols)
        yc = _gmlp(z3, gmlp_u_cb, gmlp_v_cb, gmlp_norm_g[i], gmlp_norm_b[i], gmlp_ws[i], gmlp_bs[i])

        merged = _merge(h, w_gate, ya.reshape(m, -1), yb.reshape(m, -1), yc.reshape(m, -1),
                        w_branch_a, w_branch_b, w_branch_c, i)
        xf = _residual_matmul(merged, w_out, i, xf, tm=2048)

        h2 = _rmsnorm(xf, norm_mlp_g[i], BF16)
        hidden = _relu2_matmul(h2, w_mlp_up, i)
        xf = _residual_matmul(hidden, w_mlp_down, i, xf, tm=2048)

        h3 = _rmsnorm(xf, norm_ple_g[i], BF16)
        xf = _ple(h3, w_ple_gate, p_flat, w_ple_proj, i, xf, tm=2048)

    return _rmsnorm(xf, final_norm_g, F32).reshape(nbatch, seq, d)
```

```python
import functools

import jax
import jax.numpy as jnp
import numpy as np
from jax import lax
from jax.experimental import pallas as pl
from jax.experimental.pallas import tpu as pltpu

F32 = jnp.float32
BF16 = jnp.bfloat16

HEAD_DIM = 128
MLSTM_HEADS = 4
MLSTM_WIDTH = MLSTM_HEADS * HEAD_DIM
MOBA_HEADS = 8
MOBA_WIDTH = MOBA_HEADS * HEAD_DIM
MOBA_BLOCK = 256
MOBA_TOPK = 3
ROPE_THETA = 500000.0
ROPE_DIM = HEAD_DIM // 4
GMLP_WIDTH = 512
GMLP_GROUPS = 4
GMLP_CHUNK = 128
N_BRANCHES = 3
NORM_EPS = 1e-6

LANES = 128
V7X_VMEM_BYTES = 64 * 1024 * 1024
VMEM_CEILING = V7X_VMEM_BYTES - 8 * 1024 * 1024

MLSTM_KERNEL_CHUNK = 128
STATE_ROWS = 16
GATE_LANES = LANES
MASK_BIG = 2.0 ** 100
MOBA_GROUP = 4

NT_DIMS = (((1,), (1,)), ((), ()))
TN_DIMS = (((0,), (0,)), ((), ()))


def _tiles(n, t):
    count, rest = divmod(n, t)
    assert rest == 0 and count > 0, (n, t)
    return count


def _nbytes(shape, dtype):
    return int(np.prod(shape)) * jnp.dtype(dtype).itemsize


def _params(semantics, blocks, scratch=()):
    need = 2 * sum(_nbytes(s, d) for s, d in blocks) + sum(_nbytes(s, d) for s, d in scratch)
    limit = min(VMEM_CEILING, need + need // 4 + 4 * 1024 * 1024)
    return pltpu.CompilerParams(dimension_semantics=semantics, vmem_limit_bytes=limit)


def _rmsnorm_kernel(x_ref, g_ref, o_ref):
    x = x_ref[...]
    y = x * lax.rsqrt(jnp.mean(x * x, axis=-1, keepdims=True) + NORM_EPS)
    o_ref[...] = (y * g_ref[...]).astype(o_ref.dtype)


def _rmsnorm(x, g, out_dtype, tm=512):
    m, d = x.shape
    return pl.pallas_call(
        _rmsnorm_kernel,
        grid=(_tiles(m, tm),),
        in_specs=[pl.BlockSpec((tm, d), lambda i: (i, 0)),
                  pl.BlockSpec((1, d), lambda i: (0, 0))],
        out_specs=pl.BlockSpec((tm, d), lambda i: (i, 0)),
        out_shape=jax.ShapeDtypeStruct((m, d), out_dtype),
        compiler_params=_params(("arbitrary",), [((tm, d), F32), ((tm, d), out_dtype)]),
        name="rmsnorm",
    )(x, g.reshape(1, d))


def _win_split_kernel(wm_ref, wif_ref, wg_ref, main_ref, if_ref, gate_ref):
    main_ref[...] = wm_ref[0].T.astype(BF16)
    gate_ref[...] = wg_ref[0].T.astype(BF16)
    _, n_if, k = wif_ref.shape
    padded = jnp.concatenate([wif_ref[0], jnp.zeros((GATE_LANES - n_if, k), F32)], axis=0)
    if_ref[...] = padded.T.astype(BF16)


def _win_split(w_in, head, n_if, main_cols, tn=512):
    depth, k, cols = w_in.shape
    gate_cols = cols - main_cols - n_if
    assert gate_cols == main_cols and head % tn == 0
    w_t = jnp.swapaxes(w_in, 1, 2)
    head_steps = head // tn
    blocks = [((tn, k), F32), ((n_if, k), F32), ((tn, k), F32), ((k, tn), BF16), ((k, GATE_LANES), BF16),
              ((k, tn), BF16)]

    def main_rows(l, r):
        return (l, pl.multiple_of(r * tn + jnp.where(r >= head_steps, n_if, 0), n_if), 0)

    def window(rows, start_of):
        return pl.BlockSpec((pl.Element(1), pl.Element(rows), pl.Element(k)), start_of)

    return pl.pallas_call(
        _win_split_kernel,
        grid=(depth, _tiles(main_cols, tn)),
        in_specs=[window(tn, main_rows),
                  window(n_if, lambda l, r: (l, head, 0)),
                  window(tn, lambda l, r: (l, pl.multiple_of(main_cols + n_if + r * tn, n_if), 0))],
        out_specs=[pl.BlockSpec((None, k, tn), lambda l, r: (l, 0, r)),
                   pl.BlockSpec((None, k, GATE_LANES), lambda l, r: (l, 0, 0)),
                   pl.BlockSpec((None, k, tn), lambda l, r: (l, 0, r))],
        out_shape=[jax.ShapeDtypeStruct((depth, k, main_cols), BF16),
                   jax.ShapeDtypeStruct((depth, k, GATE_LANES), BF16),
                   jax.ShapeDtypeStruct((depth, k, gate_cols), BF16)],
        compiler_params=_params(("arbitrary", "arbitrary"), blocks),
        name="win_split",
    )(w_t, w_t, w_t)


def _inproj_kernel(h_ref, w_ref, wif_ref, cs_ref, z_ref, zif_ref):
    acc = jnp.dot(h_ref[...], w_ref[...], preferred_element_type=F32)
    z_ref[...] = acc * cs_ref[...]

    @pl.when(pl.program_id(1) == 0)
    def _():
        zif_ref[...] = jnp.dot(h_ref[...], wif_ref[...], preferred_element_type=F32)


def _inproj(h, w_main, w_if, layer, colscale, tm=1024, tn=512):
    m, k = h.shape
    n = w_main.shape[2]
    blocks = [((tm, k), BF16), ((k, tn), BF16), ((k, GATE_LANES), BF16), ((1, tn), F32),
              ((tm, tn), F32), ((tm, GATE_LANES), F32)]
    return pl.pallas_call(
        _inproj_kernel,
        grid=(_tiles(m, tm), _tiles(n, tn)),
        in_specs=[pl.BlockSpec((tm, k), lambda i, j: (i, 0)),
                  pl.BlockSpec((None, k, tn), lambda i, j: (layer, 0, j)),
                  pl.BlockSpec((None, k, GATE_LANES), lambda i, j: (layer, 0, 0)),
                  pl.BlockSpec((1, tn), lambda i, j: (0, j))],
        out_specs=[pl.BlockSpec((tm, tn), lambda i, j: (i, j)),
                   pl.BlockSpec((tm, GATE_LANES), lambda i, j: (i, 0))],
        out_shape=[jax.ShapeDtypeStruct((m, n), F32),
                   jax.ShapeDtypeStruct((m, GATE_LANES), F32)],
        compiler_params=_params(("arbitrary", "arbitrary"), blocks),
        name="in_proj",
    )(h, w_main, w_if, colscale)


def _log_sigmoid(x):
    return jnp.minimum(x, 0.0) - jnp.log1p(jnp.exp(-jnp.abs(x)))


def _exact_tril_matmul(tril, x):
    tril = tril.astype(BF16)
    hi = x.astype(BF16)
    rest = x - hi.astype(F32)
    mid = rest.astype(BF16)
    lo = (rest - mid.astype(F32)).astype(BF16)
    return (jnp.dot(tril, hi, preferred_element_type=F32) + jnp.dot(tril, mid, preferred_element_type=F32)
            + jnp.dot(tril, lo, preferred_element_type=F32))


def _mlstm_kernel(q_ref, k_ref, v_ref, o_ref, zif_ref, gb_ref, ng_ref, y_ref, c_sc, n_sc, m_sc):
    nbatch, chunk, _ = q_ref.shape
    heads = MLSTM_HEADS

    @pl.when(pl.program_id(0) == 0)
    def _():
        c_sc[...] = jnp.zeros_like(c_sc)
        n_sc[...] = jnp.zeros_like(n_sc)
        m_sc[...] = jnp.zeros_like(m_sc)

    row = lax.broadcasted_iota(jnp.int32, (chunk, chunk), 0)
    col = lax.broadcasted_iota(jnp.int32, (chunk, chunk), 1)
    causal_t = row <= col
    tril = jnp.where(col <= row, 1.0, 0.0).astype(F32)
    lane = lax.broadcasted_iota(jnp.int32, (chunk, GATE_LANES), 1)

    for b in range(nbatch):
        pre = zif_ref[b] + gb_ref[...]
        gates = jnp.where(lane < heads, pre, _log_sigmoid(pre))
        gcum = _exact_tril_matmul(tril, gates)
        gates_t = gates.T
        gcum_t = gcum.T
        for h in range(heads):
            s = b * heads + h
            sl = slice(h * HEAD_DIM, (h + 1) * HEAD_DIM)
            q = q_ref[b, :, sl]
            k = k_ref[b, :, sl]
            v = v_ref[b, :, sl]
            q_t = q.T
            qb = q.astype(BF16)
            kb = k.astype(BF16)
            q_tb = q_t.astype(BF16)
            v_t = v.T
            g_row = gcum_t[heads + h:heads + h + 1, :]
            i_row = gates_t[h:h + 1, :]
            ig_col = gates[:, h:h + 1] - gcum[:, heads + h:heads + h + 1]
            m_prev = m_sc[s][:, 0:1]
            c_prev = c_sc[s]
            n_prev = n_sc[s]

            log_w = jnp.where(causal_t, g_row + ig_col, -jnp.inf)
            log_a = g_row + m_prev
            m_row = jnp.maximum(jnp.max(log_w, axis=0, keepdims=True), log_a)
            qk = lax.dot_general(kb, qb, NT_DIMS, preferred_element_type=F32) * jnp.exp(log_w - m_row)
            a = jnp.exp(log_a - m_row)
            num = (jnp.dot(v_t.astype(BF16), qk.astype(BF16), preferred_element_type=F32)
                   + a * jnp.dot(c_prev.astype(BF16), q_tb, preferred_element_type=F32))
            n_dot_q = jnp.dot(n_prev.astype(BF16), q_tb, preferred_element_type=F32)[0:1]
            den = jnp.sum(qk, axis=0, keepdims=True) + a * n_dot_q
            h_out = num / jnp.maximum(jnp.abs(den), jnp.exp(-m_row))

            g_last = g_row[:, chunk - 1:chunk]
            log_u = g_last - g_row + i_row
            m_new = jnp.maximum(g_last + m_prev, jnp.max(log_u, axis=1, keepdims=True))
            decay = jnp.exp(g_last + m_prev - m_new)
            u = jnp.exp(log_u - m_new)
            c_sc[s] = decay * c_prev + jnp.dot((v_t * u).astype(BF16), kb, preferred_element_type=F32)
            u_rows = jnp.broadcast_to(u, (STATE_ROWS, chunk)).astype(BF16)
            n_sc[s] = decay * n_prev + jnp.dot(u_rows, kb, preferred_element_type=F32)
            m_sc[s] = jnp.broadcast_to(m_new, (1, LANES))

            yn = (h_out * lax.rsqrt(jnp.mean(h_out * h_out, axis=0, keepdims=True) + NORM_EPS)).T
            y_ref[b, :, sl] = (jax.nn.sigmoid(o_ref[b, :, sl]) * (yn * ng_ref[:, sl])).astype(y_ref.dtype)


def _mlstm(z3, zif3, gate_b, norm_g):
    nbatch, seq, _ = z3.shape
    chunk = MLSTM_KERNEL_CHUNK
    w = MLSTM_WIDTH
    streams = nbatch * MLSTM_HEADS
    gb = jnp.pad(gate_b, (0, GATE_LANES - gate_b.shape[0])).reshape(1, GATE_LANES)
    blocks = [((nbatch, chunk, w), F32)] * 4 + [((nbatch, chunk, GATE_LANES), F32),
                                               ((nbatch, chunk, w), BF16)]
    scratch = [((streams, HEAD_DIM, HEAD_DIM), F32), ((streams, STATE_ROWS, LANES), F32),
               ((streams, 1, LANES), F32)]

    def zcol(cb):
        return pl.BlockSpec((nbatch, chunk, w), lambda c: (0, c, cb))

    return pl.pallas_call(
        _mlstm_kernel,
        grid=(_tiles(seq, chunk),),
        in_specs=[zcol(0), zcol(1), zcol(2), zcol(3),
                  pl.BlockSpec((nbatch, chunk, GATE_LANES), lambda c: (0, c, 0)),
                  pl.BlockSpec((1, GATE_LANES), lambda c: (0, 0)),
                  pl.BlockSpec((1, w), lambda c: (0, 0))],
        out_specs=pl.BlockSpec((nbatch, chunk, w), lambda c: (0, c, 0)),
        out_shape=jax.ShapeDtypeStruct((nbatch, seq, w), BF16),
        scratch_shapes=[pltpu.VMEM(s, d) for s, d in scratch],
        compiler_params=_params(("arbitrary",), blocks, scratch),
        name="mlstm",
    )(z3, z3, z3, z3, zif3, gb, norm_g.reshape(1, w))


def _rope_tables(seq):
    half = ROPE_DIM // 2
    inv_freq = ROPE_THETA ** (-jnp.arange(0, ROPE_DIM, 2, dtype=F32) / ROPE_DIM)
    ang = jnp.arange(seq, dtype=F32)[:, None] * inv_freq[None, :]
    cos = jnp.cos(ang)
    sin = jnp.sin(ang)
    ones = jnp.ones((seq, HEAD_DIM - ROPE_DIM), F32)
    cos_tab = jnp.concatenate([cos, cos, ones], axis=1)
    sin_tab = jnp.concatenate([-sin, sin, 0.0 * ones], axis=1)
    assert cos_tab.shape == (seq, HEAD_DIM) and half * 2 == ROPE_DIM
    return cos_tab, sin_tab


def _rotary(t, cos, sin):
    half = ROPE_DIM // 2
    lane = lax.broadcasted_iota(jnp.int32, t.shape, 1)
    upper = pltpu.roll(t, HEAD_DIM - half, axis=1)
    lower = pltpu.roll(t, half, axis=1)
    partner = jnp.where(lane < half, upper, lower)
    return jnp.where(lane < ROPE_DIM, t * cos + partner * sin, t)


def _moba_prep_kernel(q_ref, k_ref, v_ref, cos_ref, sin_ref, qa_ref, kb_ref, vb_ref, km_sc):
    blk = q_ref.shape[1]
    d = HEAD_DIM
    j = pl.program_id(1)

    @pl.when(j == 0)
    def _():
        km_sc[...] = jnp.zeros_like(km_sc)

    cos = cos_ref[...]
    sin = sin_ref[...]
    lane = lax.broadcasted_iota(jnp.int32, (blk, LANES), 1)
    lane_f = lane.astype(F32)
    mean_row = lax.broadcasted_iota(jnp.int32, (LANES, d), 0)
    vb_ref[0] = v_ref[0].astype(BF16)
    for h in range(MOBA_HEADS):
        sl = slice(h * d, (h + 1) * d)
        qf = _rotary(q_ref[0, :, sl], cos, sin)
        kk = _rotary(k_ref[0, :, sl], cos, sin)
        kb_ref[0, :, sl] = kk.astype(BF16)

        gate = lax.dot_general(qf, km_sc[h], NT_DIMS, preferred_element_type=F32,
                               precision=lax.Precision.HIGHEST)
        gate = jnp.where(lane < j, gate, -jnp.inf)
        sel_m1 = jnp.full((blk, LANES), -1.0, F32)
        for _ in range(MOBA_TOPK):
            mx = jnp.max(gate, axis=1, keepdims=True)
            first = jnp.min(jnp.where(gate == mx, lane_f, float(LANES)), axis=1, keepdims=True)
            first = jnp.where(mx > -jnp.inf, first, -1.0)
            hit = lane_f == first
            sel_m1 = jnp.where(hit, 0.0, sel_m1)
            gate = jnp.where(hit, -jnp.inf, gate)
        qa_ref[0, :, 2 * h * d:(2 * h + 1) * d] = qf.astype(BF16)
        qa_ref[0, :, (2 * h + 1) * d:(2 * h + 2) * d] = sel_m1.astype(BF16)

        km_sc[h] = jnp.where(mean_row == j, jnp.mean(kk, axis=0, keepdims=True), km_sc[h])


def _moba_prep(z3, cos_tab, sin_tab, q_cb, k_cb, v_cb):
    nbatch, seq, _ = z3.shape
    blk = MOBA_BLOCK
    w = MOBA_WIDTH
    d = HEAD_DIM
    blocks = [((1, blk, w), F32)] * 3 + [((blk, d), F32)] * 2 + [((1, blk, 2 * w), BF16), ((1, blk, w), BF16),
                                                                  ((1, blk, w), BF16)]
    scratch = [((MOBA_HEADS, LANES, d), F32)]
    return pl.pallas_call(
        _moba_prep_kernel,
        grid=(nbatch, _tiles(seq, blk)),
        in_specs=[pl.BlockSpec((1, blk, w), lambda b, j: (b, j, q_cb)),
                  pl.BlockSpec((1, blk, w), lambda b, j: (b, j, k_cb)),
                  pl.BlockSpec((1, blk, w), lambda b, j: (b, j, v_cb)),
                  pl.BlockSpec((blk, d), lambda b, j: (j, 0)),
                  pl.BlockSpec((blk, d), lambda b, j: (j, 0))],
        out_specs=[pl.BlockSpec((1, blk, 2 * w), lambda b, j: (b, j, 0)),
                   pl.BlockSpec((1, blk, w), lambda b, j: (b, j, 0)),
                   pl.BlockSpec((1, blk, w), lambda b, j: (b, j, 0))],
        out_shape=[jax.ShapeDtypeStruct((nbatch, seq, 2 * w), BF16),
                   jax.ShapeDtypeStruct((nbatch, seq, w), BF16),
                   jax.ShapeDtypeStruct((nbatch, seq, w), BF16)],
        scratch_shapes=[pltpu.VMEM(s, dt) for s, dt in scratch],
        compiler_params=_params(("arbitrary", "arbitrary"), blocks, scratch),
        name="moba_prep",
    )(z3, z3, z3, cos_tab, sin_tab)


def _moba_attn_kernel(q_ref, k_ref, v_ref, o_ref):
    blk = q_ref.shape[1]
    d = HEAD_DIM
    nblk = k_ref.shape[1] // blk
    group = MOBA_GROUP
    j = pl.program_id(2)
    scale = d ** -0.5
    q_aug = q_ref[0]
    lane = lax.broadcasted_iota(jnp.int32, (blk, LANES), 1)

    def softmax_partial(s, v_rows):
        m = jnp.max(s, axis=1, keepdims=True)
        p = jnp.exp(s - m)
        return m, jnp.sum(p, axis=1, keepdims=True), jnp.dot(p.astype(BF16), v_rows, preferred_element_type=F32)

    def past_partial(n):
        c0 = pl.multiple_of(jnp.minimum(n, nblk - 1) * blk, blk)
        onehot = jnp.where(lane == n, MASK_BIG, 0.0).astype(BF16)
        k_aug = jnp.concatenate([k_ref[0, pl.ds(c0, blk), :], onehot], axis=1)
        s = lax.dot_general(q_aug, k_aug, NT_DIMS, preferred_element_type=F32) * scale
        return softmax_partial(s, v_ref[0, pl.ds(c0, blk), :])

    def merge(state, parts):
        m_run, l_run, acc_run = state
        m_new = m_run
        for m, _, _ in parts:
            m_new = jnp.maximum(m_new, m)
        w = jnp.exp(m_run - m_new)
        l_new = w * l_run
        acc = w * acc_run
        for m, l, a in parts:
            w = jnp.exp(m - m_new)
            l_new = l_new + w * l
            acc = acc + w * a
        return m_new, l_new, acc

    r0 = pl.multiple_of(j * blk, blk)
    row = lax.broadcasted_iota(jnp.int32, (blk, blk), 0)
    col = lax.broadcasted_iota(jnp.int32, (blk, blk), 1)
    s_own = lax.dot_general(q_aug[:, :d], k_ref[0, pl.ds(r0, blk), :], NT_DIMS,
                            preferred_element_type=F32) * scale
    s_own = jnp.where(col <= row, s_own, -jnp.inf)
    state = softmax_partial(s_own, v_ref[0, pl.ds(r0, blk), :])

    state = merge(state, [past_partial(n) for n in range(group)])
    ngroups = lax.div(j + (group - 1), group)

    def group_body(g, st):
        return merge(st, [past_partial(g * group + i) for i in range(group)])

    _, l_fin, acc = lax.fori_loop(1, ngroups, group_body, state)
    o_ref[0] = (acc / l_fin).astype(o_ref.dtype)


def _moba_attn(q_aug, kb, vb):
    nbatch, seq, w = kb.shape
    blk = MOBA_BLOCK
    d = HEAD_DIM
    blocks = [((1, blk, 2 * d), BF16), ((1, seq, d), BF16), ((1, seq, d), BF16), ((1, blk, d), BF16)]
    return pl.pallas_call(
        _moba_attn_kernel,
        grid=(nbatch, MOBA_HEADS, _tiles(seq, blk)),
        in_specs=[pl.BlockSpec((1, blk, 2 * d), lambda b, h, j: (b, j, h)),
                  pl.BlockSpec((1, seq, d), lambda b, h, j: (b, 0, h)),
                  pl.BlockSpec((1, seq, d), lambda b, h, j: (b, 0, h))],
        out_specs=pl.BlockSpec((1, blk, d), lambda b, h, j: (b, j, h)),
        out_shape=jax.ShapeDtypeStruct((nbatch, seq, w), BF16),
        compiler_params=_params(("arbitrary", "arbitrary", "arbitrary"), blocks),
        name="moba_attn",
    )(q_aug, kb, vb)


def _rope_tables_t(seq):
    inv_freq = ROPE_THETA ** (-jnp.arange(0, ROPE_DIM, 2, dtype=F32) / ROPE_DIM)
    ang = jnp.arange(seq, dtype=F32)[:, None] * inv_freq[None, :]
    cos = jnp.cos(ang).T
    sin = jnp.sin(ang).T
    return jnp.concatenate([cos, cos], axis=0), jnp.concatenate([-sin, sin], axis=0)


def _moba_prep_t_kernel(q_ref, k_ref, v_ref, cos_ref, sin_ref, cost_ref, sint_ref, qa_ref, kb_ref, vt_ref, km_sc):
    blk = q_ref.shape[1]
    d = HEAD_DIM
    half = ROPE_DIM // 2
    nsel = km_sc.shape[1]
    j = pl.program_id(1)

    @pl.when(j == 0)
    def _():
        km_sc[...] = jnp.zeros_like(km_sc)

    cos = cos_ref[...]
    sin = sin_ref[...]
    cos_t = cost_ref[...]
    sin_t = sint_ref[...]
    blk_id = lax.broadcasted_iota(jnp.int32, (nsel, blk), 0)
    blk_id_f = blk_id.astype(F32)
    mean_row = lax.broadcasted_iota(jnp.int32, (nsel, d), 0)
    for h in range(MOBA_HEADS):
        sl = slice(h * d, (h + 1) * d)
        q_t = q_ref[0, :, sl].T
        top = q_t[:ROPE_DIM]
        partner = jnp.concatenate([top[half:], top[:half]], axis=0)
        q_t = jnp.concatenate([top * cos_t + partner * sin_t, q_t[ROPE_DIM:]], axis=0)

        gate = jnp.dot(km_sc[h], q_t, preferred_element_type=F32,
                       precision=lax.Precision.HIGHEST)
        gate = jnp.where(blk_id < j, gate, -jnp.inf)
        sel_m1 = jnp.full((nsel, blk), -1.0, F32)
        for _ in range(MOBA_TOPK):
            mx = jnp.max(gate, axis=0, keepdims=True)
            first = jnp.min(jnp.where(gate == mx, blk_id_f, float(nsel)), axis=0, keepdims=True)
            first = jnp.where(mx > -jnp.inf, first, -1.0)
            hit = blk_id_f == first
            sel_m1 = jnp.where(hit, 0.0, sel_m1)
            gate = jnp.where(hit, -jnp.inf, gate)
        qa_ref[0, h, 0, 0:d, :] = q_t.astype(BF16)
        qa_ref[0, h, 0, d:d + nsel, :] = sel_m1.astype(BF16)
        qa_ref[0, h, 0, d + nsel:, :] = jnp.full((d - nsel, blk), -1.0, BF16)

        kk = _rotary(k_ref[0, :, sl], cos, sin)
        kb_ref[0, :, sl] = kk.astype(BF16)
        km_sc[h] = jnp.where(mean_row == j, jnp.mean(kk, axis=0, keepdims=True), km_sc[h])

        vt_ref[0, h, 0] = v_ref[0, :, sl].T.astype(BF16)


def _moba_prep_t(z3, tables, q_cb, k_cb, v_cb):
    nbatch, seq, _ = z3.shape
    blk = MOBA_BLOCK
    nblk = _tiles(seq, blk)
    nsel = -(-nblk // 16) * 16
    assert nsel <= HEAD_DIM
    w = MOBA_WIDTH
    d = HEAD_DIM
    hh = MOBA_HEADS
    cos_tab, sin_tab, cos_t, sin_t = tables
    blocks = ([((1, blk, w), F32)] * 3 + [((blk, d), F32)] * 2 + [((ROPE_DIM, blk), F32)] * 2
              + [((hh, 2 * d, blk), BF16), ((1, blk, w), BF16), ((hh, d, blk), BF16)])
    scratch = [((hh, nsel, d), F32)]
    return pl.pallas_call(
        _moba_prep_t_kernel,
        grid=(nbatch, nblk),
        in_specs=[pl.BlockSpec((1, blk, w), lambda b, j: (b, j, q_cb)),
                  pl.BlockSpec((1, blk, w), lambda b, j: (b, j, k_cb)),
                  pl.BlockSpec((1, blk, w), lambda b, j: (b, j, v_cb)),
                  pl.BlockSpec((blk, d), lambda b, j: (j, 0)),
                  pl.BlockSpec((blk, d), lambda b, j: (j, 0)),
                  pl.BlockSpec((ROPE_DIM, blk), lambda b, j: (0, j)),
                  pl.BlockSpec((ROPE_DIM, blk), lambda b, j: (0, j))],
        out_specs=[pl.BlockSpec((1, hh, 1, 2 * d, blk), lambda b, j: (b, 0, j, 0, 0)),
                   pl.BlockSpec((1, blk, w), lambda b, j: (b, j, 0)),
                   pl.BlockSpec((1, hh, 1, d, blk), lambda b, j: (b, 0, j, 0, 0))],
        out_shape=[jax.ShapeDtypeStruct((nbatch, hh, nblk, 2 * d, blk), BF16),
                   jax.ShapeDtypeStruct((nbatch, seq, w), BF16),
                   jax.ShapeDtypeStruct((nbatch, hh, nblk, d, blk), BF16)],
        scratch_shapes=[pltpu.VMEM(s, dt) for s, dt in scratch],
        compiler_params=_params(("arbitrary", "arbitrary"), blocks, scratch),
        name="moba_prep",
    )(z3, z3, z3, cos_tab, sin_tab, cos_t, sin_t)


def _moba_attn_t_kernel(q_ref, k_ref, mask_ref, vt_ref, vown_ref, o_ref, s_sc):
    blk = q_ref.shape[-1]
    d = HEAD_DIM
    gkeys = vt_ref.shape[-1]
    group = gkeys // blk
    last_group = vt_ref.shape[2] - 1
    j = pl.program_id(2)
    c_exp = (d ** -0.5) * np.log2(np.e).astype(np.float32)
    q_aug = q_ref[0, 0, 0]

    def group_scores(g, slot):
        c0 = pl.multiple_of(g * gkeys, gkeys)
        k_aug = jnp.concatenate([k_ref[0, pl.ds(c0, gkeys), :], mask_ref[pl.ds(c0, gkeys), :]], axis=1)
        s_t = jnp.dot(k_aug, q_aug, preferred_element_type=F32)
        s_sc[slot] = s_t
        return jnp.max(s_t, axis=0, keepdims=True)

    r0 = pl.multiple_of(j * blk, blk)
    key = lax.broadcasted_iota(jnp.int32, (blk, blk), 0)
    qry = lax.broadcasted_iota(jnp.int32, (blk, blk), 1)
    s_own = jnp.dot(k_ref[0, pl.ds(r0, blk), :], q_aug[:d], preferred_element_type=F32)
    s_own = jnp.where(key <= qry, s_own, -jnp.inf)
    m0 = jnp.max(s_own, axis=0, keepdims=True)
    p0 = jnp.exp2((s_own - m0) * c_exp)
    l0 = jnp.sum(p0, axis=0, keepdims=True)
    acc0 = jnp.dot(vown_ref[0, 0, 0], p0.astype(BF16), preferred_element_type=F32)

    def fold_group(g, state, slot):
        m_run, l_run, acc, s_max = state
        next_max = group_scores(jnp.minimum(g + 1, last_group), 1 - slot)
        m_new = jnp.maximum(m_run, s_max)
        alpha = jnp.exp2((m_run - m_new) * c_exp)
        p = jnp.exp2((s_sc[slot] - m_new) * c_exp)
        l_new = alpha * l_run + jnp.sum(p, axis=0, keepdims=True)
        acc = alpha * acc + jnp.dot(vt_ref[0, 0, g], p.astype(BF16), preferred_element_type=F32)
        return m_new, l_new, acc, next_max

    def group_body(g, state):
        return lax.cond(g % 2 == 0, lambda st: fold_group(g, st, 0), lambda st: fold_group(g, st, 1), state)

    ngroups = lax.div(j + (group - 1), group)
    _, l_fin, acc, _ = lax.fori_loop(0, ngroups, group_body, (m0, l0, acc0, group_scores(0, 0)))
    o_ref[0] = (acc / l_fin).T.astype(o_ref.dtype)


def _moba_mask_columns(seq):
    blk_id = jnp.arange(seq, dtype=jnp.int32)[:, None] // MOBA_BLOCK
    return jnp.where(jnp.arange(LANES, dtype=jnp.int32)[None, :] == blk_id, MASK_BIG, 0.0).astype(BF16)


def _moba_attn_t(q_aug, kb, vt, mask_cols):
    nbatch, seq, w = kb.shape
    blk = MOBA_BLOCK
    d = HEAD_DIM
    ngrp, gkeys = vt.shape[2], vt.shape[4]
    group = gkeys // blk
    blocks = [((2 * d, blk), BF16), ((1, seq, d), BF16), ((seq, LANES), BF16), ((ngrp, d, gkeys), BF16),
              ((d, blk), BF16), ((1, blk, d), BF16)]
    scratch = [((2, gkeys, blk), F32)]
    return pl.pallas_call(
        _moba_attn_t_kernel,
        grid=(nbatch, MOBA_HEADS, _tiles(seq, blk)),
        in_specs=[pl.BlockSpec((1, 1, 1, 2 * d, blk), lambda b, h, j: (b, h, j, 0, 0)),
                  pl.BlockSpec((1, seq, d), lambda b, h, j: (b, 0, h)),
                  pl.BlockSpec((seq, LANES), lambda b, h, j: (0, 0)),
                  pl.BlockSpec((1, 1, ngrp, d, gkeys), lambda b, h, j: (b, h, 0, 0, 0)),
                  pl.BlockSpec((1, 1, 1, d, blk), lambda b, h, j: (b, h, j // group, 0, j % group))],
        out_specs=pl.BlockSpec((1, blk, d), lambda b, h, j: (b, j, h)),
        out_shape=jax.ShapeDtypeStruct((nbatch, seq, w), BF16),
        scratch_shapes=[pltpu.VMEM(s, dt) for s, dt in scratch],
        compiler_params=_params(("arbitrary", "arbitrary", "arbitrary"), blocks, scratch),
        name="moba_attn",
    )(q_aug, kb, mask_cols, vt, vt)


def _pipelined_pairs(count, first, step):
    def pair(t, carry):
        return step(2 * t + 1, 1, step(2 * t, 0, carry))

    carry = lax.fori_loop(0, count // 2, pair, first)
    if count % 2:
        carry = step(count - 1, 0, carry)
    return carry


def _moba_flat_kernel(tile_ref, group_ref, q_ref, k_ref, mask_ref, vt_ref, o_ref,
                      s_sc, acc_sc, m_sc, l_sc):
    nblk, d, blk = vt_ref.shape[2:]
    group = MOBA_GROUP
    gkeys = group * blk
    n_items = tile_ref.shape[0]
    c_exp = (d ** -0.5) * np.log2(np.e).astype(np.float32)
    key = lax.broadcasted_iota(jnp.int32, (blk, blk), 0)
    qry = lax.broadcasted_iota(jnp.int32, (blk, blk), 1)

    def weighted_values(first_blk, p):
        acc = None
        for i in range(p.shape[0] // blk):
            part = jnp.dot(vt_ref[0, 0, first_blk + i], p[i * blk:(i + 1) * blk].astype(BF16),
                           preferred_element_type=F32)
            acc = part if acc is None else acc + part
        return acc

    def own_scores(j, slot):
        r0 = pl.multiple_of(j * blk, blk)
        s_t = jnp.dot(k_ref[0, pl.ds(r0, blk), :], q_ref[0, 0, j, :d, :], preferred_element_type=F32)
        s_t = jnp.where(key <= qry, s_t, -jnp.inf)
        s_sc[slot, :blk, :] = s_t
        return jnp.max(s_t, axis=0, keepdims=True)

    def fold_own(j, slot, s_max):
        next_max = own_scores(jnp.minimum(j + 1, nblk - 1), 1 - slot)
        p = jnp.exp2((s_sc[slot, :blk, :] - s_max) * c_exp)
        m_sc[j] = s_max
        l_sc[j] = jnp.sum(p, axis=0, keepdims=True)
        acc_sc[j] = weighted_values(j, p)
        return next_max

    _pipelined_pairs(nblk, own_scores(0, 0), fold_own)

    def group_scores(i, slot):
        c0 = pl.multiple_of(group_ref[i] * gkeys, gkeys)
        k_aug = jnp.concatenate([k_ref[0, pl.ds(c0, gkeys), :], mask_ref[pl.ds(c0, gkeys), :]], axis=1)
        s_t = jnp.dot(k_aug, q_ref[0, 0, tile_ref[i]], preferred_element_type=F32)
        s_sc[slot] = s_t
        return jnp.max(s_t, axis=0, keepdims=True)

    def fold_group(i, slot, s_max):
        next_max = group_scores(jnp.minimum(i + 1, n_items - 1), 1 - slot)
        j = tile_ref[i]
        m_run = m_sc[j]
        m_new = jnp.maximum(m_run, s_max)
        alpha = jnp.exp2((m_run - m_new) * c_exp)
        p = jnp.exp2((s_sc[slot] - m_new) * c_exp)
        m_sc[j] = m_new
        l_sc[j] = alpha * l_sc[j] + jnp.sum(p, axis=0, keepdims=True)
        acc_sc[j] = alpha * acc_sc[j] + weighted_values(group_ref[i] * group, p)
        return next_max

    _pipelined_pairs(n_items, group_scores(0, 0), fold_group)

    def finish(j, carry):
        r0 = pl.multiple_of(j * blk, blk)
        o_ref[0, pl.ds(r0, blk), :] = (acc_sc[j] / l_sc[j]).T.astype(o_ref.dtype)
        return carry

    lax.fori_loop(0, nblk, finish, 0)


def _moba_flat(q_aug, kb, vt, mask_cols):
    nbatch, seq, w = kb.shape
    blk = MOBA_BLOCK
    d = HEAD_DIM
    nblk = vt.shape[2]
    group = MOBA_GROUP
    items = [(j, g) for j in range(nblk) for g in range(-(-j // group))]
    assert items and _tiles(nblk, group)
    item_tile = jnp.asarray([j for j, _ in items], jnp.int32)
    item_group = jnp.asarray([g for _, g in items], jnp.int32)
    blocks = [((nblk, 2 * d, blk), BF16), ((1, seq, d), BF16), ((seq, LANES), BF16), ((nblk, d, blk), BF16),
              ((1, seq, d), BF16)]
    scratch = [((2, group * blk, blk), F32), ((nblk, d, blk), F32), ((nblk, 1, blk), F32), ((nblk, 1, blk), F32)]
    grid_spec = pltpu.PrefetchScalarGridSpec(
        num_scalar_prefetch=2,
        grid=(nbatch, MOBA_HEADS),
        in_specs=[pl.BlockSpec((1, 1, nblk, 2 * d, blk), lambda b, h, *_: (b, h, 0, 0, 0)),
                  pl.BlockSpec((1, seq, d), lambda b, h, *_: (b, 0, h)),
                  pl.BlockSpec((seq, LANES), lambda b, h, *_: (0, 0)),
                  pl.BlockSpec((1, 1, nblk, d, blk), lambda b, h, *_: (b, h, 0, 0, 0))],
        out_specs=pl.BlockSpec((1, seq, d), lambda b, h, *_: (b, 0, h)),
        scratch_shapes=[pltpu.VMEM(s, dt) for s, dt in scratch])
    return pl.pallas_call(
        _moba_flat_kernel,
        grid_spec=grid_spec,
        out_shape=jax.ShapeDtypeStruct((nbatch, seq, w), BF16),
        compiler_params=_params(("arbitrary", "arbitrary"), blocks, scratch),
        name="moba_attn",
    )(item_tile, item_group, q_aug, kb, mask_cols, vt)


def _gelu_tanh(x):
    c = np.sqrt(2.0 / np.pi).astype(np.float32)
    return x * (0.5 * (1.0 + jnp.tanh(c * (x + 0.044715 * (x * x * x)))))


def _gmlp_kernel(u_ref, v_ref, lg_ref, lb_ref, ws_ref, bst_ref, y_ref):
    rows = u_ref.shape[1]
    t = GMLP_CHUNK
    gd = GMLP_WIDTH // GMLP_GROUPS
    v = _gelu_tanh(v_ref[0])
    mu = jnp.mean(v, axis=-1, keepdims=True)
    vc = v - mu
    vln = vc * lax.rsqrt(jnp.mean(vc * vc, axis=-1, keepdims=True) + NORM_EPS) * lg_ref[...] + lb_ref[...]
    vb = vln.astype(BF16)
    row = lax.broadcasted_iota(jnp.int32, (t, t), 0)
    col = lax.broadcasted_iota(jnp.int32, (t, t), 1)
    for g in range(GMLP_GROUPS):
        wg = jnp.where(col <= row, ws_ref[g], 0.0).astype(BF16)
        bias = bst_ref[:, g:g + 1]
        cols = slice(g * gd, (g + 1) * gd)
        for c in range(rows // t):
            rs = slice(c * t, (c + 1) * t)
            mixed = jnp.dot(wg, vb[rs, cols], preferred_element_type=F32) + bias
            y_ref[0, rs, cols] = (_gelu_tanh(u_ref[0, rs, cols]) * mixed).astype(y_ref.dtype)


def _gmlp(z3, u_cb, v_cb, ln_g, ln_b, ws, bs, rows=512):
    nbatch, seq, _ = z3.shape
    w = GMLP_WIDTH
    t = GMLP_CHUNK
    blocks = [((1, rows, w), F32), ((1, rows, w), F32), ((GMLP_GROUPS, t, t), F32), ((1, rows, w), BF16)]
    return pl.pallas_call(
        _gmlp_kernel,
        grid=(nbatch, _tiles(seq, rows)),
        in_specs=[pl.BlockSpec((1, rows, w), lambda b, c: (b, c, u_cb)),
                  pl.BlockSpec((1, rows, w), lambda b, c: (b, c, v_cb)),
                  pl.BlockSpec((1, w), lambda b, c: (0, 0)),
                  pl.BlockSpec((1, w), lambda b, c: (0, 0)),
                  pl.BlockSpec((GMLP_GROUPS, t, t), lambda b, c: (0, 0, 0)),
                  pl.BlockSpec((t, GMLP_GROUPS), lambda b, c: (0, 0))],
        out_specs=pl.BlockSpec((1, rows, w), lambda b, c: (b, c, 0)),
        out_shape=jax.ShapeDtypeStruct((nbatch, seq, w), BF16),
        compiler_params=_params(("arbitrary", "arbitrary"), blocks),
        name="gmlp",
    )(z3, z3, ln_g.reshape(1, w), ln_b.reshape(1, w), ws, bs.T)


def _layer_weight_spec(layer, k, tn, index_of):
    return pl.BlockSpec((None, k, tn), lambda *idx: (layer,) + index_of(*idx))


def _merge_kernel(h_ref, wga_ref, wgb_ref, wgc_ref, ya_ref, yb_ref, yc_ref, wa_ref, wb_ref, wc_ref, o_ref):
    h = h_ref[...]

    def branch(wg_ref, y_ref, w_ref):
        gate = jax.nn.sigmoid(jnp.dot(h, wg_ref[...], preferred_element_type=F32))
        return gate * jnp.dot(y_ref[...], w_ref[...].astype(BF16), preferred_element_type=F32)

    merged = branch(wga_ref, ya_ref, wa_ref) + branch(wgb_ref, yb_ref, wb_ref) + branch(wgc_ref, yc_ref, wc_ref)
    o_ref[...] = merged.astype(o_ref.dtype)


def _merge(h, w_gate, ya, yb, yc, wa, wb, wc, layer, tm=1024, tn=512):
    m, d = h.shape
    nblk = _tiles(d, tn)
    ka, kb, kc = ya.shape[1], yb.shape[1], yc.shape[1]
    blocks = ([((tm, d), BF16)] + [((d, tn), BF16)] * 3
              + [((tm, ka), BF16), ((tm, kb), BF16), ((tm, kc), BF16)]
              + [((ka, tn), F32), ((kb, tn), F32), ((kc, tn), F32), ((tm, tn), BF16)])

    def gate_spec(branch):
        return pl.BlockSpec((None, d, tn), lambda i, j: (layer, 0, branch * nblk + j))

    def col(i, j):
        return (0, j)

    return pl.pallas_call(
        _merge_kernel,
        grid=(_tiles(m, tm), nblk),
        in_specs=[pl.BlockSpec((tm, d), lambda i, j: (i, 0)),
                  gate_spec(0), gate_spec(1), gate_spec(2),
                  pl.BlockSpec((tm, ka), lambda i, j: (i, 0)),
                  pl.BlockSpec((tm, kb), lambda i, j: (i, 0)),
                  pl.BlockSpec((tm, kc), lambda i, j: (i, 0)),
                  _layer_weight_spec(layer, ka, tn, col),
                  _layer_weight_spec(layer, kb, tn, col),
                  _layer_weight_spec(layer, kc, tn, col)],
        out_specs=pl.BlockSpec((tm, tn), lambda i, j: (i, j)),
        out_shape=jax.ShapeDtypeStruct((m, d), BF16),
        compiler_params=_params(("arbitrary", "arbitrary"), blocks),
        name="merge",
    )(h, w_gate, w_gate, w_gate, ya, yb, yc, wa, wb, wc)


def _residual_matmul_kernel(a_ref, w_ref, r_ref, o_ref):
    @pl.when(pl.program_id(2) == 0)
    def _():
        o_ref[...] = r_ref[...]

    o_ref[...] += jnp.dot(a_ref[...], w_ref[...].astype(BF16), preferred_element_type=F32)


def _residual_matmul(a, w, layer, res, tm=1024, tn=1024, tk=1024):
    m, k = a.shape
    n = w.shape[2]
    blocks = [((tm, tk), BF16), ((tk, tn), F32), ((tm, tn), F32), ((tm, tn), F32)]
    return pl.pallas_call(
        _residual_matmul_kernel,
        grid=(_tiles(m, tm), _tiles(n, tn), _tiles(k, tk)),
        in_specs=[pl.BlockSpec((tm, tk), lambda i, j, kk: (i, kk)),
                  _layer_weight_spec(layer, tk, tn, lambda i, j, kk: (kk, j)),
                  pl.BlockSpec((tm, tn), lambda i, j, kk: (i, j))],
        out_specs=pl.BlockSpec((tm, tn), lambda i, j, kk: (i, j)),
        out_shape=jax.ShapeDtypeStruct((m, n), F32),
        compiler_params=_params(("arbitrary", "arbitrary", "arbitrary"), blocks),
        name="residual_matmul",
    )(a, w, res)


def _relu2_matmul_kernel(a_ref, w_ref, o_ref):
    up = jnp.maximum(jnp.dot(a_ref[...], w_ref[...].astype(BF16), preferred_element_type=F32), 0.0)
    o_ref[...] = (up * up).astype(o_ref.dtype)


def _relu2_matmul(a, w, layer, tm=1024, tn=1024):
    m, k = a.shape
    n = w.shape[2]
    blocks = [((tm, k), BF16), ((k, tn), F32), ((tm, tn), BF16)]
    return pl.pallas_call(
        _relu2_matmul_kernel,
        grid=(_tiles(m, tm), _tiles(n, tn)),
        in_specs=[pl.BlockSpec((tm, k), lambda i, j: (i, 0)),
                  _layer_weight_spec(layer, k, tn, lambda i, j: (0, j))],
        out_specs=pl.BlockSpec((tm, tn), lambda i, j: (i, j)),
        out_shape=jax.ShapeDtypeStruct((m, n), BF16),
        compiler_params=_params(("arbitrary", "arbitrary"), blocks),
        name="mlp_up",
    )(a, w)


def _ple_kernel(h_ref, wg_ref, p_ref, wp_ref, r_ref, o_ref):
    gate = jax.nn.sigmoid(jnp.dot(h_ref[...], wg_ref[...].astype(BF16), preferred_element_type=F32))
    emb = jnp.dot(p_ref[...].astype(BF16), wp_ref[...].astype(BF16), preferred_element_type=F32)
    o_ref[...] = r_ref[...] + gate * emb


def _ple(h, w_gate, p, w_proj, layer, res, tm=1024, tn=512):
    m, d = h.shape
    pd = p.shape[2]
    blocks = [((tm, d), BF16), ((d, tn), F32), ((tm, pd), F32), ((pd, tn), F32),
              ((tm, tn), F32), ((tm, tn), F32)]
    return pl.pallas_call(
        _ple_kernel,
        grid=(_tiles(m, tm), _tiles(d, tn)),
        in_specs=[pl.BlockSpec((tm, d), lambda i, j: (i, 0)),
                  _layer_weight_spec(layer, d, tn, lambda i, j: (0, j)),
                  pl.BlockSpec((None, tm, pd), lambda i, j: (layer, i, 0)),
                  _layer_weight_spec(layer, pd, tn, lambda i, j: (0, j)),
                  pl.BlockSpec((tm, tn), lambda i, j: (i, j))],
        out_specs=pl.BlockSpec((tm, tn), lambda i, j: (i, j)),
        out_shape=jax.ShapeDtypeStruct((m, d), F32),
        compiler_params=_params(("arbitrary", "arbitrary"), blocks),
        name="ple",
    )(h, w_gate, p, w_proj, res)


def kernel(x, p, norm_mix_g, w_in, mlstm_gate_b, mlstm_norm_g, gmlp_norm_g, gmlp_norm_b, gmlp_ws, gmlp_bs,
           w_branch_a, w_branch_b, w_branch_c, w_out, norm_mlp_g, w_mlp_up, w_mlp_down, norm_ple_g,
           w_ple_gate, w_ple_proj, final_norm_g):
    nbatch, seq, d = x.shape
    depth = w_in.shape[0]
    m = nbatch * seq
    assert d == MLSTM_WIDTH + MOBA_WIDTH + GMLP_WIDTH

    qkvo_a = 4 * MLSTM_WIDTH
    gates_if = 2 * MLSTM_HEADS
    main_b = qkvo_a + gates_if
    main_cols = 3 * MOBA_WIDTH + 2 * GMLP_WIDTH
    gate_off = main_b + main_cols
    assert w_in.shape[2] == gate_off + N_BRANCHES * d

    moba_q_cb = qkvo_a // MOBA_WIDTH
    moba_k_cb = moba_q_cb + 1
    moba_v_cb = moba_k_cb + 1
    gmlp_u_cb = (qkvo_a + 3 * MOBA_WIDTH) // GMLP_WIDTH
    gmlp_v_cb = gmlp_u_cb + 1

    z_cols = qkvo_a + main_cols
    colscale = jnp.ones((1, z_cols), F32).at[:, MLSTM_WIDTH:2 * MLSTM_WIDTH].set(HEAD_DIM ** -0.5)
    rope_tables = _rope_tables(seq) + _rope_tables_t(seq)
    mask_cols = _moba_mask_columns(seq)

    xf = x.reshape(m, d)
    p_flat = p.reshape(depth, m, p.shape[-1])
    w_main, w_if, w_gate = _win_split(w_in, qkvo_a, gates_if, z_cols)
    for i in range(depth):
        h = _rmsnorm(xf, norm_mix_g[i], BF16)
        z, zif = _inproj(h, w_main, w_if, i, colscale)
        z3 = z.reshape(nbatch, seq, z_cols)
        zif3 = zif.reshape(nbatch, seq, GATE_LANES)

        ya = _mlstm(z3, zif3, mlstm_gate_b[i], mlstm_norm_g[i])
        yb = _moba_flat(*_moba_prep_t(z3, rope_tables, moba_q_cb, moba_k_cb, moba_v_cb), mask_cols)
        yc = _gmlp(z3, gmlp_u_cb, gmlp_v_cb, gmlp_norm_g[i], gmlp_norm_b[i], gmlp_ws[i], gmlp_bs[i])

        merged = _merge(h, w_gate, ya.reshape(m, -1), yb.reshape(m, -1), yc.reshape(m, -1),
                        w_branch_a, w_branch_b, w_branch_c, i)
        xf = _residual_matmul(merged, w_out, i, xf, tm=2048)

        h2 = _rmsnorm(xf, norm_mlp_g[i], BF16)
        hidden = _relu2_matmul(h2, w_mlp_up, i)
        xf = _residual_matmul(hidden, w_mlp_down, i, xf, tm=2048)

        h3 = _rmsnorm(xf, norm_ple_g[i], BF16)
        xf = _ple(h3, w_ple_gate, p_flat, w_ple_proj, i, xf, tm=2048)

    return _rmsnorm(xf, final_norm_g, F32).reshape(nbatch, seq, d)
```

```python
import functools

import jax
import jax.numpy as jnp
import numpy as np
from jax import lax
from jax.experimental import pallas as pl
from jax.experimental.pallas import tpu as pltpu

F32 = jnp.float32
BF16 = jnp.bfloat16

HEAD_DIM = 128
MLSTM_HEADS = 4
MLSTM_WIDTH = MLSTM_HEADS * HEAD_DIM
MOBA_HEADS = 8
MOBA_WIDTH = MOBA_HEADS * HEAD_DIM
MOBA_BLOCK = 256
MOBA_TOPK = 3
ROPE_THETA = 500000.0
ROPE_DIM = HEAD_DIM // 4
GMLP_WIDTH = 512
GMLP_GROUPS = 4
GMLP_CHUNK = 128
N_BRANCHES = 3
NORM_EPS = 1e-6

LANES = 128
V7X_VMEM_BYTES = 64 * 1024 * 1024
VMEM_CEILING = V7X_VMEM_BYTES - 8 * 1024 * 1024

MLSTM_KERNEL_CHUNK = 128
STATE_ROWS = 16
GATE_LANES = LANES
MASK_BIG = 2.0 ** 100
MOBA_GROUP = 4

NT_DIMS = (((1,), (1,)), ((), ()))
TN_DIMS = (((0,), (0,)), ((), ()))


def _tiles(n, t):
    count, rest = divmod(n, t)
    assert rest == 0 and count > 0, (n, t)
    return count


def _nbytes(shape, dtype):
    return int(np.prod(shape)) * jnp.dtype(dtype).itemsize


def _params(semantics, blocks, scratch=()):
    need = 2 * sum(_nbytes(s, d) for s, d in blocks) + sum(_nbytes(s, d) for s, d in scratch)
    limit = min(VMEM_CEILING, need + need // 4 + 4 * 1024 * 1024)
    return pltpu.CompilerParams(dimension_semantics=semantics, vmem_limit_bytes=limit)


def _rmsnorm_kernel(x_ref, g_ref, o_ref):
    x = x_ref[...]
    y = x * lax.rsqrt(jnp.mean(x * x, axis=-1, keepdims=True) + NORM_EPS)
    o_ref[...] = (y * g_ref[...]).astype(o_ref.dtype)


def _rmsnorm(x, g, out_dtype, tm=512):
    m, d = x.shape
    return pl.pallas_call(
        _rmsnorm_kernel,
        grid=(_tiles(m, tm),),
        in_specs=[pl.BlockSpec((tm, d), lambda i: (i, 0)),
                  pl.BlockSpec((1, d), lambda i: (0, 0))],
        out_specs=pl.BlockSpec((tm, d), lambda i: (i, 0)),
        out_shape=jax.ShapeDtypeStruct((m, d), out_dtype),
        compiler_params=_params(("arbitrary",), [((tm, d), F32), ((tm, d), out_dtype)]),
        name="rmsnorm",
    )(x, g.reshape(1, d))


def _win_split_kernel(wm_ref, wif_ref, wg_ref, main_ref, if_ref, gate_ref):
    main_ref[...] = wm_ref[0].T.astype(BF16)
    gate_ref[...] = wg_ref[0].T.astype(BF16)
    _, n_if, k = wif_ref.shape
    padded = jnp.concatenate([wif_ref[0], jnp.zeros((GATE_LANES - n_if, k), F32)], axis=0)
    if_ref[...] = padded.T.astype(BF16)


def _win_split(w_in, head, n_if, main_cols, tn=512):
    depth, k, cols = w_in.shape
    gate_cols = cols - main_cols - n_if
    assert gate_cols == main_cols and head % tn == 0
    w_t = jnp.swapaxes(w_in, 1, 2)
    head_steps = head // tn
    blocks = [((tn, k), F32), ((n_if, k), F32), ((tn, k), F32), ((k, tn), BF16), ((k, GATE_LANES), BF16),
              ((k, tn), BF16)]

    def main_rows(l, r):
        return (l, pl.multiple_of(r * tn + jnp.where(r >= head_steps, n_if, 0), n_if), 0)

    def window(rows, start_of):
        return pl.BlockSpec((pl.Element(1), pl.Element(rows), pl.Element(k)), start_of)

    return pl.pallas_call(
        _win_split_kernel,
        grid=(depth, _tiles(main_cols, tn)),
        in_specs=[window(tn, main_rows),
                  window(n_if, lambda l, r: (l, head, 0)),
                  window(tn, lambda l, r: (l, pl.multiple_of(main_cols + n_if + r * tn, n_if), 0))],
        out_specs=[pl.BlockSpec((None, k, tn), lambda l, r: (l, 0, r)),
                   pl.BlockSpec((None, k, GATE_LANES), lambda l, r: (l, 0, 0)),
                   pl.BlockSpec((None, k, tn), lambda l, r: (l, 0, r))],
        out_shape=[jax.ShapeDtypeStruct((depth, k, main_cols), BF16),
                   jax.ShapeDtypeStruct((depth, k, GATE_LANES), BF16),
                   jax.ShapeDtypeStruct((depth, k, gate_cols), BF16)],
        compiler_params=_params(("arbitrary", "arbitrary"), blocks),
        name="win_split",
    )(w_t, w_t, w_t)


def _rmsnorm_rows(x_ref, g_ref, h_ref, rows=256):
    for r0 in range(0, x_ref.shape[0], rows):
        x = x_ref[r0:r0 + rows, :]
        y = x * lax.rsqrt(jnp.mean(x * x, axis=-1, keepdims=True) + NORM_EPS)
        h_ref[r0:r0 + rows, :] = (y * g_ref[...]).astype(h_ref.dtype)


def _inproj_kernel(x_ref, g_ref, w_ref, wif_ref, cs_ref, z_ref, zif_ref, h_ref):
    @pl.when(pl.program_id(1) == 0)
    def _():
        _rmsnorm_rows(x_ref, g_ref, h_ref)
        zif_ref[...] = jnp.dot(h_ref[...], wif_ref[...], preferred_element_type=F32)

    acc = jnp.dot(h_ref[...], w_ref[...], preferred_element_type=F32)
    z_ref[...] = acc * cs_ref[...]


def _inproj(x, g, w_main, w_if, layer, colscale, tm=1024, tn=1024):
    m, k = x.shape
    n = w_main.shape[2]
    blocks = [((tm, k), F32), ((k, tn), BF16), ((k, GATE_LANES), BF16), ((1, tn), F32),
              ((tm, tn), F32), ((tm, GATE_LANES), F32), ((tm, k), BF16)]
    return pl.pallas_call(
        _inproj_kernel,
        grid=(_tiles(m, tm), _tiles(n, tn)),
        in_specs=[pl.BlockSpec((tm, k), lambda i, j: (i, 0)),
                  pl.BlockSpec((1, k), lambda i, j: (0, 0)),
                  pl.BlockSpec((None, k, tn), lambda i, j: (layer, 0, j)),
                  pl.BlockSpec((None, k, GATE_LANES), lambda i, j: (layer, 0, 0)),
                  pl.BlockSpec((1, tn), lambda i, j: (0, j))],
        out_specs=[pl.BlockSpec((tm, tn), lambda i, j: (i, j)),
                   pl.BlockSpec((tm, GATE_LANES), lambda i, j: (i, 0)),
                   pl.BlockSpec((tm, k), lambda i, j: (i, 0))],
        out_shape=[jax.ShapeDtypeStruct((m, n), F32),
                   jax.ShapeDtypeStruct((m, GATE_LANES), F32),
                   jax.ShapeDtypeStruct((m, k), BF16)],
        compiler_params=_params(("arbitrary", "arbitrary"), blocks),
        name="in_proj",
    )(x, g.reshape(1, k), w_main, w_if, colscale)


def _log_sigmoid(x):
    return jnp.minimum(x, 0.0) - jnp.log1p(jnp.exp(-jnp.abs(x)))


def _exact_tril_matmul(tril, x):
    tril = tril.astype(BF16)
    hi = x.astype(BF16)
    rest = x - hi.astype(F32)
    mid = rest.astype(BF16)
    lo = (rest - mid.astype(F32)).astype(BF16)
    return (jnp.dot(tril, hi, preferred_element_type=F32) + jnp.dot(tril, mid, preferred_element_type=F32)
            + jnp.dot(tril, lo, preferred_element_type=F32))


def _mlstm_kernel(q_ref, k_ref, v_ref, o_ref, zif_ref, gb_ref, ng_ref, y_ref, c_sc, n_sc, m_sc):
    nbatch, chunk, _ = q_ref.shape
    heads = MLSTM_HEADS

    @pl.when(pl.program_id(0) == 0)
    def _():
        c_sc[...] = jnp.zeros_like(c_sc)
        n_sc[...] = jnp.zeros_like(n_sc)
        m_sc[...] = jnp.zeros_like(m_sc)

    row = lax.broadcasted_iota(jnp.int32, (chunk, chunk), 0)
    col = lax.broadcasted_iota(jnp.int32, (chunk, chunk), 1)
    causal_t = row <= col
    tril = jnp.where(col <= row, 1.0, 0.0).astype(F32)
    lane = lax.broadcasted_iota(jnp.int32, (chunk, GATE_LANES), 1)

    for b in range(nbatch):
        pre = zif_ref[b] + gb_ref[...]
        gates = jnp.where(lane < heads, pre, _log_sigmoid(pre))
        gcum = _exact_tril_matmul(tril, gates)
        gates_t = gates.T
        gcum_t = gcum.T
        for h in range(heads):
            s = b * heads + h
            sl = slice(h * HEAD_DIM, (h + 1) * HEAD_DIM)
            q = q_ref[b, :, sl]
            k = k_ref[b, :, sl]
            v = v_ref[b, :, sl]
            q_t = q.T
            qb = q.astype(BF16)
            kb = k.astype(BF16)
            q_tb = q_t.astype(BF16)
            v_t = v.T
            g_row = gcum_t[heads + h:heads + h + 1, :]
            i_row = gates_t[h:h + 1, :]
            ig_col = gates[:, h:h + 1] - gcum[:, heads + h:heads + h + 1]
            m_prev = m_sc[s][:, 0:1]
            c_prev = c_sc[s]
            n_prev = n_sc[s]

            log_w = jnp.where(causal_t, g_row + ig_col, -jnp.inf)
            log_a = g_row + m_prev
            m_row = jnp.maximum(jnp.max(log_w, axis=0, keepdims=True), log_a)
            qk = lax.dot_general(kb, qb, NT_DIMS, preferred_element_type=F32) * jnp.exp(log_w - m_row)
            a = jnp.exp(log_a - m_row)
            num = (jnp.dot(v_t.astype(BF16), qk.astype(BF16), preferred_element_type=F32)
                   + a * jnp.dot(c_prev.astype(BF16), q_tb, preferred_element_type=F32))
            n_dot_q = jnp.dot(n_prev.astype(BF16), q_tb, preferred_element_type=F32)[0:1]
            den = jnp.sum(qk, axis=0, keepdims=True) + a * n_dot_q
            h_out = num / jnp.maximum(jnp.abs(den), jnp.exp(-m_row))

            g_last = g_row[:, chunk - 1:chunk]
            log_u = g_last - g_row + i_row
            m_new = jnp.maximum(g_last + m_prev, jnp.max(log_u, axis=1, keepdims=True))
            decay = jnp.exp(g_last + m_prev - m_new)
            u = jnp.exp(log_u - m_new)
            c_sc[s] = decay * c_prev + jnp.dot((v_t * u).astype(BF16), kb, preferred_element_type=F32)
            u_rows = jnp.broadcast_to(u, (STATE_ROWS, chunk)).astype(BF16)
            n_sc[s] = decay * n_prev + jnp.dot(u_rows, kb, preferred_element_type=F32)
            m_sc[s] = jnp.broadcast_to(m_new, (1, LANES))

            yn = (h_out * lax.rsqrt(jnp.mean(h_out * h_out, axis=0, keepdims=True) + NORM_EPS)).T
            y_ref[b, :, sl] = (jax.nn.sigmoid(o_ref[b, :, sl]) * (yn * ng_ref[:, sl])).astype(y_ref.dtype)


def _mlstm(z3, zif3, gate_b, norm_g):
    nbatch, seq, _ = z3.shape
    chunk = MLSTM_KERNEL_CHUNK
    w = MLSTM_WIDTH
    streams = nbatch * MLSTM_HEADS
    gb = jnp.pad(gate_b, (0, GATE_LANES - gate_b.shape[0])).reshape(1, GATE_LANES)
    blocks = [((nbatch, chunk, w), F32)] * 4 + [((nbatch, chunk, GATE_LANES), F32),
                                               ((nbatch, chunk, w), BF16)]
    scratch = [((streams, HEAD_DIM, HEAD_DIM), F32), ((streams, STATE_ROWS, LANES), F32),
               ((streams, 1, LANES), F32)]

    def zcol(cb):
        return pl.BlockSpec((nbatch, chunk, w), lambda c: (0, c, cb))

    return pl.pallas_call(
        _mlstm_kernel,
        grid=(_tiles(seq, chunk),),
        in_specs=[zcol(0), zcol(1), zcol(2), zcol(3),
                  pl.BlockSpec((nbatch, chunk, GATE_LANES), lambda c: (0, c, 0)),
                  pl.BlockSpec((1, GATE_LANES), lambda c: (0, 0)),
                  pl.BlockSpec((1, w), lambda c: (0, 0))],
        out_specs=pl.BlockSpec((nbatch, chunk, w), lambda c: (0, c, 0)),
        out_shape=jax.ShapeDtypeStruct((nbatch, seq, w), BF16),
        scratch_shapes=[pltpu.VMEM(s, d) for s, d in scratch],
        compiler_params=_params(("arbitrary",), blocks, scratch),
        name="mlstm",
    )(z3, z3, z3, z3, zif3, gb, norm_g.reshape(1, w))


def _rope_tables(seq):
    half = ROPE_DIM // 2
    inv_freq = ROPE_THETA ** (-jnp.arange(0, ROPE_DIM, 2, dtype=F32) / ROPE_DIM)
    ang = jnp.arange(seq, dtype=F32)[:, None] * inv_freq[None, :]
    cos = jnp.cos(ang)
    sin = jnp.sin(ang)
    ones = jnp.ones((seq, HEAD_DIM - ROPE_DIM), F32)
    cos_tab = jnp.concatenate([cos, cos, ones], axis=1)
    sin_tab = jnp.concatenate([-sin, sin, 0.0 * ones], axis=1)
    assert cos_tab.shape == (seq, HEAD_DIM) and half * 2 == ROPE_DIM
    return cos_tab, sin_tab


def _rotary(t, cos, sin):
    half = ROPE_DIM // 2
    lane = lax.broadcasted_iota(jnp.int32, t.shape, 1)
    upper = pltpu.roll(t, HEAD_DIM - half, axis=1)
    lower = pltpu.roll(t, half, axis=1)
    partner = jnp.where(lane < half, upper, lower)
    return jnp.where(lane < ROPE_DIM, t * cos + partner * sin, t)


def _moba_prep_kernel(q_ref, k_ref, v_ref, cos_ref, sin_ref, qa_ref, kb_ref, vb_ref, km_sc):
    blk = q_ref.shape[1]
    d = HEAD_DIM
    j = pl.program_id(1)

    @pl.when(j == 0)
    def _():
        km_sc[...] = jnp.zeros_like(km_sc)

    cos = cos_ref[...]
    sin = sin_ref[...]
    lane = lax.broadcasted_iota(jnp.int32, (blk, LANES), 1)
    lane_f = lane.astype(F32)
    mean_row = lax.broadcasted_iota(jnp.int32, (LANES, d), 0)
    vb_ref[0] = v_ref[0].astype(BF16)
    for h in range(MOBA_HEADS):
        sl = slice(h * d, (h + 1) * d)
        qf = _rotary(q_ref[0, :, sl], cos, sin)
        kk = _rotary(k_ref[0, :, sl], cos, sin)
        kb_ref[0, :, sl] = kk.astype(BF16)

        gate = lax.dot_general(qf, km_sc[h], NT_DIMS, preferred_element_type=F32,
                               precision=lax.Precision.HIGHEST)
        gate = jnp.where(lane < j, gate, -jnp.inf)
        sel_m1 = jnp.full((blk, LANES), -1.0, F32)
        for _ in range(MOBA_TOPK):
            mx = jnp.max(gate, axis=1, keepdims=True)
            first = jnp.min(jnp.where(gate == mx, lane_f, float(LANES)), axis=1, keepdims=True)
            first = jnp.where(mx > -jnp.inf, first, -1.0)
            hit = lane_f == first
            sel_m1 = jnp.where(hit, 0.0, sel_m1)
            gate = jnp.where(hit, -jnp.inf, gate)
        qa_ref[0, :, 2 * h * d:(2 * h + 1) * d] = qf.astype(BF16)
        qa_ref[0, :, (2 * h + 1) * d:(2 * h + 2) * d] = sel_m1.astype(BF16)

        km_sc[h] = jnp.where(mean_row == j, jnp.mean(kk, axis=0, keepdims=True), km_sc[h])


def _moba_prep(z3, cos_tab, sin_tab, q_cb, k_cb, v_cb):
    nbatch, seq, _ = z3.shape
    blk = MOBA_BLOCK
    w = MOBA_WIDTH
    d = HEAD_DIM
    blocks = [((1, blk, w), F32)] * 3 + [((blk, d), F32)] * 2 + [((1, blk, 2 * w), BF16), ((1, blk, w), BF16),
                                                                  ((1, blk, w), BF16)]
    scratch = [((MOBA_HEADS, LANES, d), F32)]
    return pl.pallas_call(
        _moba_prep_kernel,
        grid=(nbatch, _tiles(seq, blk)),
        in_specs=[pl.BlockSpec((1, blk, w), lambda b, j: (b, j, q_cb)),
                  pl.BlockSpec((1, blk, w), lambda b, j: (b, j, k_cb)),
                  pl.BlockSpec((1, blk, w), lambda b, j: (b, j, v_cb)),
                  pl.BlockSpec((blk, d), lambda b, j: (j, 0)),
                  pl.BlockSpec((blk, d), lambda b, j: (j, 0))],
        out_specs=[pl.BlockSpec((1, blk, 2 * w), lambda b, j: (b, j, 0)),
                   pl.BlockSpec((1, blk, w), lambda b, j: (b, j, 0)),
                   pl.BlockSpec((1, blk, w), lambda b, j: (b, j, 0))],
        out_shape=[jax.ShapeDtypeStruct((nbatch, seq, 2 * w), BF16),
                   jax.ShapeDtypeStruct((nbatch, seq, w), BF16),
                   jax.ShapeDtypeStruct((nbatch, seq, w), BF16)],
        scratch_shapes=[pltpu.VMEM(s, dt) for s, dt in scratch],
        compiler_params=_params(("arbitrary", "arbitrary"), blocks, scratch),
        name="moba_prep",
    )(z3, z3, z3, cos_tab, sin_tab)


def _moba_attn_kernel(q_ref, k_ref, v_ref, o_ref):
    blk = q_ref.shape[1]
    d = HEAD_DIM
    nblk = k_ref.shape[1] // blk
    group = MOBA_GROUP
    j = pl.program_id(2)
    scale = d ** -0.5
    q_aug = q_ref[0]
    lane = lax.broadcasted_iota(jnp.int32, (blk, LANES), 1)

    def softmax_partial(s, v_rows):
        m = jnp.max(s, axis=1, keepdims=True)
        p = jnp.exp(s - m)
        return m, jnp.sum(p, axis=1, keepdims=True), jnp.dot(p.astype(BF16), v_rows, preferred_element_type=F32)

    def past_partial(n):
        c0 = pl.multiple_of(jnp.minimum(n, nblk - 1) * blk, blk)
        onehot = jnp.where(lane == n, MASK_BIG, 0.0).astype(BF16)
        k_aug = jnp.concatenate([k_ref[0, pl.ds(c0, blk), :], onehot], axis=1)
        s = lax.dot_general(q_aug, k_aug, NT_DIMS, preferred_element_type=F32) * scale
        return softmax_partial(s, v_ref[0, pl.ds(c0, blk), :])

    def merge(state, parts):
        m_run, l_run, acc_run = state
        m_new = m_run
        for m, _, _ in parts:
            m_new = jnp.maximum(m_new, m)
        w = jnp.exp(m_run - m_new)
        l_new = w * l_run
        acc = w * acc_run
        for m, l, a in parts:
            w = jnp.exp(m - m_new)
            l_new = l_new + w * l
            acc = acc + w * a
        return m_new, l_new, acc

    r0 = pl.multiple_of(j * blk, blk)
    row = lax.broadcasted_iota(jnp.int32, (blk, blk), 0)
    col = lax.broadcasted_iota(jnp.int32, (blk, blk), 1)
    s_own = lax.dot_general(q_aug[:, :d], k_ref[0, pl.ds(r0, blk), :], NT_DIMS,
                            preferred_element_type=F32) * scale
    s_own = jnp.where(col <= row, s_own, -jnp.inf)
    state = softmax_partial(s_own, v_ref[0, pl.ds(r0, blk), :])

    state = merge(state, [past_partial(n) for n in range(group)])
    ngroups = lax.div(j + (group - 1), group)

    def group_body(g, st):
        return merge(st, [past_partial(g * group + i) for i in range(group)])

    _, l_fin, acc = lax.fori_loop(1, ngroups, group_body, state)
    o_ref[0] = (acc / l_fin).astype(o_ref.dtype)


def _moba_attn(q_aug, kb, vb):
    nbatch, seq, w = kb.shape
    blk = MOBA_BLOCK
    d = HEAD_DIM
    blocks = [((1, blk, 2 * d), BF16), ((1, seq, d), BF16), ((1, seq, d), BF16), ((1, blk, d), BF16)]
    return pl.pallas_call(
        _moba_attn_kernel,
        grid=(nbatch, MOBA_HEADS, _tiles(seq, blk)),
        in_specs=[pl.BlockSpec((1, blk, 2 * d), lambda b, h, j: (b, j, h)),
                  pl.BlockSpec((1, seq, d), lambda b, h, j: (b, 0, h)),
                  pl.BlockSpec((1, seq, d), lambda b, h, j: (b, 0, h))],
        out_specs=pl.BlockSpec((1, blk, d), lambda b, h, j: (b, j, h)),
        out_shape=jax.ShapeDtypeStruct((nbatch, seq, w), BF16),
        compiler_params=_params(("arbitrary", "arbitrary", "arbitrary"), blocks),
        name="moba_attn",
    )(q_aug, kb, vb)


def _rope_tables_t(seq):
    inv_freq = ROPE_THETA ** (-jnp.arange(0, ROPE_DIM, 2, dtype=F32) / ROPE_DIM)
    ang = jnp.arange(seq, dtype=F32)[:, None] * inv_freq[None, :]
    cos = jnp.cos(ang).T
    sin = jnp.sin(ang).T
    return jnp.concatenate([cos, cos], axis=0), jnp.concatenate([-sin, sin], axis=0)


def _moba_prep_t_kernel(q_ref, k_ref, v_ref, cos_ref, sin_ref, cost_ref, sint_ref, qa_ref, kb_ref, vt_ref, km_sc):
    blk = q_ref.shape[1]
    d = HEAD_DIM
    half = ROPE_DIM // 2
    nsel = km_sc.shape[1]
    j = pl.program_id(1)

    @pl.when(j == 0)
    def _():
        km_sc[...] = jnp.zeros_like(km_sc)

    cos = cos_ref[...]
    sin = sin_ref[...]
    cos_t = cost_ref[...]
    sin_t = sint_ref[...]
    blk_id = lax.broadcasted_iota(jnp.int32, (nsel, blk), 0)
    blk_id_f = blk_id.astype(F32)
    mean_row = lax.broadcasted_iota(jnp.int32, (nsel, d), 0)
    for h in range(MOBA_HEADS):
        sl = slice(h * d, (h + 1) * d)
        q_t = q_ref[0, :, sl].T
        top = q_t[:ROPE_DIM]
        partner = jnp.concatenate([top[half:], top[:half]], axis=0)
        q_t = jnp.concatenate([top * cos_t + partner * sin_t, q_t[ROPE_DIM:]], axis=0)

        gate = jnp.dot(km_sc[h], q_t, preferred_element_type=F32,
                       precision=lax.Precision.HIGHEST)
        gate = jnp.where(blk_id < j, gate, -jnp.inf)
        sel_m1 = jnp.full((nsel, blk), -1.0, F32)
        for _ in range(MOBA_TOPK):
            mx = jnp.max(gate, axis=0, keepdims=True)
            first = jnp.min(jnp.where(gate == mx, blk_id_f, float(nsel)), axis=0, keepdims=True)
            first = jnp.where(mx > -jnp.inf, first, -1.0)
            hit = blk_id_f == first
            sel_m1 = jnp.where(hit, 0.0, sel_m1)
            gate = jnp.where(hit, -jnp.inf, gate)
        qa_ref[0, h, 0, 0:d, :] = q_t.astype(BF16)
        qa_ref[0, h, 0, d:d + nsel, :] = sel_m1.astype(BF16)
        qa_ref[0, h, 0, d + nsel:, :] = jnp.full((d - nsel, blk), -1.0, BF16)

        kk = _rotary(k_ref[0, :, sl], cos, sin)
        kb_ref[0, :, sl] = kk.astype(BF16)
        km_sc[h] = jnp.where(mean_row == j, jnp.mean(kk, axis=0, keepdims=True), km_sc[h])

        vt_ref[0, h, 0] = v_ref[0, :, sl].T.astype(BF16)


def _moba_prep_t(z3, tables, q_cb, k_cb, v_cb):
    nbatch, seq, _ = z3.shape
    blk = MOBA_BLOCK
    nblk = _tiles(seq, blk)
    nsel = -(-nblk // 16) * 16
    assert nsel <= HEAD_DIM
    w = MOBA_WIDTH
    d = HEAD_DIM
    hh = MOBA_HEADS
    cos_tab, sin_tab, cos_t, sin_t = tables
    blocks = ([((1, blk, w), F32)] * 3 + [((blk, d), F32)] * 2 + [((ROPE_DIM, blk), F32)] * 2
              + [((hh, 2 * d, blk), BF16), ((1, blk, w), BF16), ((hh, d, blk), BF16)])
    scratch = [((hh, nsel, d), F32)]
    return pl.pallas_call(
        _moba_prep_t_kernel,
        grid=(nbatch, nblk),
        in_specs=[pl.BlockSpec((1, blk, w), lambda b, j: (b, j, q_cb)),
                  pl.BlockSpec((1, blk, w), lambda b, j: (b, j, k_cb)),
                  pl.BlockSpec((1, blk, w), lambda b, j: (b, j, v_cb)),
                  pl.BlockSpec((blk, d), lambda b, j: (j, 0)),
                  pl.BlockSpec((blk, d), lambda b, j: (j, 0)),
                  pl.BlockSpec((ROPE_DIM, blk), lambda b, j: (0, j)),
                  pl.BlockSpec((ROPE_DIM, blk), lambda b, j: (0, j))],
        out_specs=[pl.BlockSpec((1, hh, 1, 2 * d, blk), lambda b, j: (b, 0, j, 0, 0)),
                   pl.BlockSpec((1, blk, w), lambda b, j: (b, j, 0)),
                   pl.BlockSpec((1, hh, 1, d, blk), lambda b, j: (b, 0, j, 0, 0))],
        out_shape=[jax.ShapeDtypeStruct((nbatch, hh, nblk, 2 * d, blk), BF16),
                   jax.ShapeDtypeStruct((nbatch, seq, w), BF16),
                   jax.ShapeDtypeStruct((nbatch, hh, nblk, d, blk), BF16)],
        scratch_shapes=[pltpu.VMEM(s, dt) for s, dt in scratch],
        compiler_params=_params(("arbitrary", "arbitrary"), blocks, scratch),
        name="moba_prep",
    )(z3, z3, z3, cos_tab, sin_tab, cos_t, sin_t)


def _moba_attn_t_kernel(q_ref, k_ref, mask_ref, vt_ref, vown_ref, o_ref, s_sc):
    blk = q_ref.shape[-1]
    d = HEAD_DIM
    gkeys = vt_ref.shape[-1]
    group = gkeys // blk
    last_group = vt_ref.shape[2] - 1
    j = pl.program_id(2)
    c_exp = (d ** -0.5) * np.log2(np.e).astype(np.float32)
    q_aug = q_ref[0, 0, 0]

    def group_scores(g, slot):
        c0 = pl.multiple_of(g * gkeys, gkeys)
        k_aug = jnp.concatenate([k_ref[0, pl.ds(c0, gkeys), :], mask_ref[pl.ds(c0, gkeys), :]], axis=1)
        s_t = jnp.dot(k_aug, q_aug, preferred_element_type=F32)
        s_sc[slot] = s_t
        return jnp.max(s_t, axis=0, keepdims=True)

    r0 = pl.multiple_of(j * blk, blk)
    key = lax.broadcasted_iota(jnp.int32, (blk, blk), 0)
    qry = lax.broadcasted_iota(jnp.int32, (blk, blk), 1)
    s_own = jnp.dot(k_ref[0, pl.ds(r0, blk), :], q_aug[:d], preferred_element_type=F32)
    s_own = jnp.where(key <= qry, s_own, -jnp.inf)
    m0 = jnp.max(s_own, axis=0, keepdims=True)
    p0 = jnp.exp2((s_own - m0) * c_exp)
    l0 = jnp.sum(p0, axis=0, keepdims=True)
    acc0 = jnp.dot(vown_ref[0, 0, 0], p0.astype(BF16), preferred_element_type=F32)

    def fold_group(g, state, slot):
        m_run, l_run, acc, s_max = state
        next_max = group_scores(jnp.minimum(g + 1, last_group), 1 - slot)
        m_new = jnp.maximum(m_run, s_max)
        alpha = jnp.exp2((m_run - m_new) * c_exp)
        p = jnp.exp2((s_sc[slot] - m_new) * c_exp)
        l_new = alpha * l_run + jnp.sum(p, axis=0, keepdims=True)
        acc = alpha * acc + jnp.dot(vt_ref[0, 0, g], p.astype(BF16), preferred_element_type=F32)
        return m_new, l_new, acc, next_max

    def group_body(g, state):
        return lax.cond(g % 2 == 0, lambda st: fold_group(g, st, 0), lambda st: fold_group(g, st, 1), state)

    ngroups = lax.div(j + (group - 1), group)
    _, l_fin, acc, _ = lax.fori_loop(0, ngroups, group_body, (m0, l0, acc0, group_scores(0, 0)))
    o_ref[0] = (acc / l_fin).T.astype(o_ref.dtype)


def _moba_mask_columns(seq):
    blk_id = jnp.arange(seq, dtype=jnp.int32)[:, None] // MOBA_BLOCK
    return jnp.where(jnp.arange(LANES, dtype=jnp.int32)[None, :] == blk_id, MASK_BIG, 0.0).astype(BF16)


def _moba_attn_t(q_aug, kb, vt, mask_cols):
    nbatch, seq, w = kb.shape
    blk = MOBA_BLOCK
    d = HEAD_DIM
    ngrp, gkeys = vt.shape[2], vt.shape[4]
    group = gkeys // blk
    blocks = [((2 * d, blk), BF16), ((1, seq, d), BF16), ((seq, LANES), BF16), ((ngrp, d, gkeys), BF16),
              ((d, blk), BF16), ((1, blk, d), BF16)]
    scratch = [((2, gkeys, blk), F32)]
    return pl.pallas_call(
        _moba_attn_t_kernel,
        grid=(nbatch, MOBA_HEADS, _tiles(seq, blk)),
        in_specs=[pl.BlockSpec((1, 1, 1, 2 * d, blk), lambda b, h, j: (b, h, j, 0, 0)),
                  pl.BlockSpec((1, seq, d), lambda b, h, j: (b, 0, h)),
                  pl.BlockSpec((seq, LANES), lambda b, h, j: (0, 0)),
                  pl.BlockSpec((1, 1, ngrp, d, gkeys), lambda b, h, j: (b, h, 0, 0, 0)),
                  pl.BlockSpec((1, 1, 1, d, blk), lambda b, h, j: (b, h, j // group, 0, j % group))],
        out_specs=pl.BlockSpec((1, blk, d), lambda b, h, j: (b, j, h)),
        out_shape=jax.ShapeDtypeStruct((nbatch, seq, w), BF16),
        scratch_shapes=[pltpu.VMEM(s, dt) for s, dt in scratch],
        compiler_params=_params(("arbitrary", "arbitrary", "arbitrary"), blocks, scratch),
        name="moba_attn",
    )(q_aug, kb, mask_cols, vt, vt)


def _pipelined_pairs(count, first, step):
    def pair(t, carry):
        return step(2 * t + 1, 1, step(2 * t, 0, carry))

    carry = lax.fori_loop(0, count // 2, pair, first)
    if count % 2:
        carry = step(count - 1, 0, carry)
    return carry


def _moba_flat_kernel(tile_ref, group_ref, q_ref, k_ref, mask_ref, vt_ref, o_ref,
                      s_sc, acc_sc, m_sc, l_sc):
    nblk, d, blk = vt_ref.shape[2:]
    group = MOBA_GROUP
    gkeys = group * blk
    n_items = tile_ref.shape[0]
    c_exp = (d ** -0.5) * np.log2(np.e).astype(np.float32)
    key = lax.broadcasted_iota(jnp.int32, (blk, blk), 0)
    qry = lax.broadcasted_iota(jnp.int32, (blk, blk), 1)

    def weighted_values(first_blk, p):
        acc = None
        for i in range(p.shape[0] // blk):
            part = jnp.dot(vt_ref[0, 0, first_blk + i], p[i * blk:(i + 1) * blk].astype(BF16),
                           preferred_element_type=F32)
            acc = part if acc is None else acc + part
        return acc

    def own_scores(j, slot):
        r0 = pl.multiple_of(j * blk, blk)
        s_t = jnp.dot(k_ref[0, pl.ds(r0, blk), :], q_ref[0, 0, j, :d, :], preferred_element_type=F32)
        s_t = jnp.where(key <= qry, s_t, -jnp.inf)
        s_sc[slot, :blk, :] = s_t
        return jnp.max(s_t, axis=0, keepdims=True)

    def fold_own(j, slot, s_max):
        next_max = own_scores(jnp.minimum(j + 1, nblk - 1), 1 - slot)
        p = jnp.exp2((s_sc[slot, :blk, :] - s_max) * c_exp)
        m_sc[j] = s_max
        l_sc[j] = jnp.sum(p, axis=0, keepdims=True)
        acc_sc[j] = weighted_values(j, p)
        return next_max

    _pipelined_pairs(nblk, own_scores(0, 0), fold_own)

    def group_scores(i, slot):
        c0 = pl.multiple_of(group_ref[i] * gkeys, gkeys)
        k_aug = jnp.concatenate([k_ref[0, pl.ds(c0, gkeys), :], mask_ref[pl.ds(c0, gkeys), :]], axis=1)
        s_t = jnp.dot(k_aug, q_ref[0, 0, tile_ref[i]], preferred_element_type=F32)
        s_sc[slot] = s_t
        return jnp.max(s_t, axis=0, keepdims=True)

    def fold_group(i, slot, s_max):
        next_max = group_scores(jnp.minimum(i + 1, n_items - 1), 1 - slot)
        j = tile_ref[i]
        m_run = m_sc[j]
        m_new = jnp.maximum(m_run, s_max)
        alpha = jnp.exp2((m_run - m_new) * c_exp)
        p = jnp.exp2((s_sc[slot] - m_new) * c_exp)
        m_sc[j] = m_new
        l_sc[j] = alpha * l_sc[j] + jnp.sum(p, axis=0, keepdims=True)
        acc_sc[j] = alpha * acc_sc[j] + weighted_values(group_ref[i] * group, p)
        return next_max

    _pipelined_pairs(n_items, group_scores(0, 0), fold_group)

    def finish(j, carry):
        r0 = pl.multiple_of(j * blk, blk)
        o_ref[0, pl.ds(r0, blk), :] = (acc_sc[j] / l_sc[j]).T.astype(o_ref.dtype)
        return carry

    lax.fori_loop(0, nblk, finish, 0)


def _moba_flat(q_aug, kb, vt, mask_cols):
    nbatch, seq, w = kb.shape
    blk = MOBA_BLOCK
    d = HEAD_DIM
    nblk = vt.shape[2]
    group = MOBA_GROUP
    items = [(j, g) for j in range(nblk) for g in range(-(-j // group))]
    assert items and _tiles(nblk, group)
    item_tile = jnp.asarray([j for j, _ in items], jnp.int32)
    item_group = jnp.asarray([g for _, g in items], jnp.int32)
    blocks = [((nblk, 2 * d, blk), BF16), ((1, seq, d), BF16), ((seq, LANES), BF16), ((nblk, d, blk), BF16),
              ((1, seq, d), BF16)]
    scratch = [((2, group * blk, blk), F32), ((nblk, d, blk), F32), ((nblk, 1, blk), F32), ((nblk, 1, blk), F32)]
    grid_spec = pltpu.PrefetchScalarGridSpec(
        num_scalar_prefetch=2,
        grid=(nbatch, MOBA_HEADS),
        in_specs=[pl.BlockSpec((1, 1, nblk, 2 * d, blk), lambda b, h, *_: (b, h, 0, 0, 0)),
                  pl.BlockSpec((1, seq, d), lambda b, h, *_: (b, 0, h)),
                  pl.BlockSpec((seq, LANES), lambda b, h, *_: (0, 0)),
                  pl.BlockSpec((1, 1, nblk, d, blk), lambda b, h, *_: (b, h, 0, 0, 0))],
        out_specs=pl.BlockSpec((1, seq, d), lambda b, h, *_: (b, 0, h)),
        scratch_shapes=[pltpu.VMEM(s, dt) for s, dt in scratch])
    return pl.pallas_call(
        _moba_flat_kernel,
        grid_spec=grid_spec,
        out_shape=jax.ShapeDtypeStruct((nbatch, seq, w), BF16),
        compiler_params=_params(("arbitrary", "arbitrary"), blocks, scratch),
        name="moba_attn",
    )(item_tile, item_group, q_aug, kb, mask_cols, vt)


def _gelu_tanh(x):
    c = np.sqrt(2.0 / np.pi).astype(np.float32)
    return x * (0.5 * (1.0 + jnp.tanh(c * (x + 0.044715 * (x * x * x)))))


def _gmlp_kernel(u_ref, v_ref, lg_ref, lb_ref, ws_ref, bst_ref, y_ref):
    rows = u_ref.shape[1]
    t = GMLP_CHUNK
    gd = GMLP_WIDTH // GMLP_GROUPS
    v = _gelu_tanh(v_ref[0])
    mu = jnp.mean(v, axis=-1, keepdims=True)
    vc = v - mu
    vln = vc * lax.rsqrt(jnp.mean(vc * vc, axis=-1, keepdims=True) + NORM_EPS) * lg_ref[...] + lb_ref[...]
    vb = vln.astype(BF16)
    row = lax.broadcasted_iota(jnp.int32, (t, t), 0)
    col = lax.broadcasted_iota(jnp.int32, (t, t), 1)
    for g in range(GMLP_GROUPS):
        wg = jnp.where(col <= row, ws_ref[g], 0.0).astype(BF16)
        bias = bst_ref[:, g:g + 1]
        cols = slice(g * gd, (g + 1) * gd)
        for c in range(rows // t):
            rs = slice(c * t, (c + 1) * t)
            mixed = jnp.dot(wg, vb[rs, cols], preferred_element_type=F32) + bias
            y_ref[0, rs, cols] = (_gelu_tanh(u_ref[0, rs, cols]) * mixed).astype(y_ref.dtype)


def _gmlp(z3, u_cb, v_cb, ln_g, ln_b, ws, bs, rows=512):
    nbatch, seq, _ = z3.shape
    w = GMLP_WIDTH
    t = GMLP_CHUNK
    blocks = [((1, rows, w), F32), ((1, rows, w), F32), ((GMLP_GROUPS, t, t), F32), ((1, rows, w), BF16)]
    return pl.pallas_call(
        _gmlp_kernel,
        grid=(nbatch, _tiles(seq, rows)),
        in_specs=[pl.BlockSpec((1, rows, w), lambda b, c: (b, c, u_cb)),
                  pl.BlockSpec((1, rows, w), lambda b, c: (b, c, v_cb)),
                  pl.BlockSpec((1, w), lambda b, c: (0, 0)),
                  pl.BlockSpec((1, w), lambda b, c: (0, 0)),
                  pl.BlockSpec((GMLP_GROUPS, t, t), lambda b, c: (0, 0, 0)),
                  pl.BlockSpec((t, GMLP_GROUPS), lambda b, c: (0, 0))],
        out_specs=pl.BlockSpec((1, rows, w), lambda b, c: (b, c, 0)),
        out_shape=jax.ShapeDtypeStruct((nbatch, seq, w), BF16),
        compiler_params=_params(("arbitrary", "arbitrary"), blocks),
        name="gmlp",
    )(z3, z3, ln_g.reshape(1, w), ln_b.reshape(1, w), ws, bs.T)


def _layer_weight_spec(layer, k, tn, index_of):
    return pl.BlockSpec((None, k, tn), lambda *idx: (layer,) + index_of(*idx))


def _merge_kernel(h_ref, wga_ref, wgb_ref, wgc_ref, ya_ref, yb_ref, yc_ref, wa_ref, wb_ref, wc_ref, o_ref):
    h = h_ref[...]

    def branch(wg_ref, y_ref, w_ref):
        gate = jax.nn.sigmoid(jnp.dot(h, wg_ref[...], preferred_element_type=F32))
        return gate * jnp.dot(y_ref[...], w_ref[...].astype(BF16), preferred_element_type=F32)

    merged = branch(wga_ref, ya_ref, wa_ref) + branch(wgb_ref, yb_ref, wb_ref) + branch(wgc_ref, yc_ref, wc_ref)
    o_ref[...] = merged.astype(o_ref.dtype)


def _merge(h, w_gate, ya, yb, yc, wa, wb, wc, layer, tm=1024, tn=512):
    m, d = h.shape
    nblk = _tiles(d, tn)
    ka, kb, kc = ya.shape[1], yb.shape[1], yc.shape[1]
    blocks = ([((tm, d), BF16)] + [((d, tn), BF16)] * 3
              + [((tm, ka), BF16), ((tm, kb), BF16), ((tm, kc), BF16)]
              + [((ka, tn), F32), ((kb, tn), F32), ((kc, tn), F32), ((tm, tn), BF16)])

    def gate_spec(branch):
        return pl.BlockSpec((None, d, tn), lambda i, j: (layer, 0, branch * nblk + j))

    def col(i, j):
        return (0, j)

    return pl.pallas_call(
        _merge_kernel,
        grid=(_tiles(m, tm), nblk),
        in_specs=[pl.BlockSpec((tm, d), lambda i, j: (i, 0)),
                  gate_spec(0), gate_spec(1), gate_spec(2),
                  pl.BlockSpec((tm, ka), lambda i, j: (i, 0)),
                  pl.BlockSpec((tm, kb), lambda i, j: (i, 0)),
                  pl.BlockSpec((tm, kc), lambda i, j: (i, 0)),
                  _layer_weight_spec(layer, ka, tn, col),
                  _layer_weight_spec(layer, kb, tn, col),
                  _layer_weight_spec(layer, kc, tn, col)],
        out_specs=pl.BlockSpec((tm, tn), lambda i, j: (i, j)),
        out_shape=jax.ShapeDtypeStruct((m, d), BF16),
        compiler_params=_params(("arbitrary", "arbitrary"), blocks),
        name="merge",
    )(h, w_gate, w_gate, w_gate, ya, yb, yc, wa, wb, wc)


def _residual_matmul_kernel(a_ref, w_ref, r_ref, o_ref):
    @pl.when(pl.program_id(2) == 0)
    def _():
        o_ref[...] = r_ref[...]

    o_ref[...] += jnp.dot(a_ref[...], w_ref[...].astype(BF16), preferred_element_type=F32)


def _residual_matmul(a, w, layer, res, tm=1024, tn=1024, tk=1024):
    m, k = a.shape
    n = w.shape[2]
    blocks = [((tm, tk), BF16), ((tk, tn), F32), ((tm, tn), F32), ((tm, tn), F32)]
    return pl.pallas_call(
        _residual_matmul_kernel,
        grid=(_tiles(m, tm), _tiles(n, tn), _tiles(k, tk)),
        in_specs=[pl.BlockSpec((tm, tk), lambda i, j, kk: (i, kk)),
                  _layer_weight_spec(layer, tk, tn, lambda i, j, kk: (kk, j)),
                  pl.BlockSpec((tm, tn), lambda i, j, kk: (i, j))],
        out_specs=pl.BlockSpec((tm, tn), lambda i, j, kk: (i, j)),
        out_shape=jax.ShapeDtypeStruct((m, n), F32),
        compiler_params=_params(("arbitrary", "arbitrary", "arbitrary"), blocks),
        name="residual_matmul",
    )(a, w, res)


def _relu2_matmul_kernel(x_ref, g_ref, w_ref, o_ref, h_sc):
    @pl.when(pl.program_id(1) == 0)
    def _():
        _rmsnorm_rows(x_ref, g_ref, h_sc)

    up = jnp.maximum(jnp.dot(h_sc[...], w_ref[...].astype(BF16), preferred_element_type=F32), 0.0)
    o_ref[...] = (up * up).astype(o_ref.dtype)


def _relu2_matmul(x, g, w, layer, tm=1024, tn=1024):
    m, k = x.shape
    n = w.shape[2]
    blocks = [((tm, k), F32), ((k, tn), F32), ((tm, tn), BF16)]
    scratch = [((tm, k), BF16)]
    return pl.pallas_call(
        _relu2_matmul_kernel,
        grid=(_tiles(m, tm), _tiles(n, tn)),
        in_specs=[pl.BlockSpec((tm, k), lambda i, j: (i, 0)),
                  pl.BlockSpec((1, k), lambda i, j: (0, 0)),
                  _layer_weight_spec(layer, k, tn, lambda i, j: (0, j))],
        out_specs=pl.BlockSpec((tm, tn), lambda i, j: (i, j)),
        out_shape=jax.ShapeDtypeStruct((m, n), BF16),
        scratch_shapes=[pltpu.VMEM(s, dt) for s, dt in scratch],
        compiler_params=_params(("arbitrary", "arbitrary"), blocks, scratch),
        name="mlp_up",
    )(x, g.reshape(1, k), w)


def _ple_kernel(x_ref, g_ref, wg_ref, p_ref, wp_ref, r_ref, o_ref, h_sc):
    @pl.when(pl.program_id(1) == 0)
    def _():
        _rmsnorm_rows(x_ref, g_ref, h_sc)

    gate = jax.nn.sigmoid(jnp.dot(h_sc[...], wg_ref[...].astype(BF16), preferred_element_type=F32))
    emb = jnp.dot(p_ref[...].astype(BF16), wp_ref[...].astype(BF16), preferred_element_type=F32)
    o_ref[...] = r_ref[...] + gate * emb


def _ple(x, g, w_gate, p, w_proj, layer, tm=1024, tn=512):
    m, d = x.shape
    pd = p.shape[2]
    blocks = [((tm, d), F32), ((d, tn), F32), ((tm, pd), F32), ((pd, tn), F32),
              ((tm, tn), F32), ((tm, tn), F32)]
    scratch = [((tm, d), BF16)]
    return pl.pallas_call(
        _ple_kernel,
        grid=(_tiles(m, tm), _tiles(d, tn)),
        in_specs=[pl.BlockSpec((tm, d), lambda i, j: (i, 0)),
                  pl.BlockSpec((1, d), lambda i, j: (0, 0)),
                  _layer_weight_spec(layer, d, tn, lambda i, j: (0, j)),
                  pl.BlockSpec((None, tm, pd), lambda i, j: (layer, i, 0)),
                  _layer_weight_spec(layer, pd, tn, lambda i, j: (0, j)),
                  pl.BlockSpec((tm, tn), lambda i, j: (i, j))],
        out_specs=pl.BlockSpec((tm, tn), lambda i, j: (i, j)),
        out_shape=jax.ShapeDtypeStruct((m, d), F32),
        scratch_shapes=[pltpu.VMEM(s, dt) for s, dt in scratch],
        compiler_params=_params(("arbitrary", "arbitrary"), blocks, scratch),
        name="ple",
    )(x, g.reshape(1, d), w_gate, p, w_proj, x)


def kernel(x, p, norm_mix_g, w_in, mlstm_gate_b, mlstm_norm_g, gmlp_norm_g, gmlp_norm_b, gmlp_ws, gmlp_bs,
           w_branch_a, w_branch_b, w_branch_c, w_out, norm_mlp_g, w_mlp_up, w_mlp_down, norm_ple_g,
           w_ple_gate, w_ple_proj, final_norm_g):
    nbatch, seq, d = x.shape
    depth = w_in.shape[0]
    m = nbatch * seq
    assert d == MLSTM_WIDTH + MOBA_WIDTH + GMLP_WIDTH

    qkvo_a = 4 * MLSTM_WIDTH
    gates_if = 2 * MLSTM_HEADS
    main_b = qkvo_a + gates_if
    main_cols = 3 * MOBA_WIDTH + 2 * GMLP_WIDTH
    gate_off = main_b + main_cols
    assert w_in.shape[2] == gate_off + N_BRANCHES * d

    moba_q_cb = qkvo_a // MOBA_WIDTH
    moba_k_cb = moba_q_cb + 1
    moba_v_cb = moba_k_cb + 1
    gmlp_u_cb = (qkvo_a + 3 * MOBA_WIDTH) // GMLP_WIDTH
    gmlp_v_cb = gmlp_u_cb + 1

    z_cols = qkvo_a + main_cols
    colscale = jnp.ones((1, z_cols), F32).at[:, MLSTM_WIDTH:2 * MLSTM_WIDTH].set(HEAD_DIM ** -0.5)
    rope_tables = _rope_tables(seq) + _rope_tables_t(seq)
    mask_cols = _moba_mask_columns(seq)

    xf = x.reshape(m, d)
    p_flat = p.reshape(depth, m, p.shape[-1])
    w_main, w_if, w_gate = _win_split(w_in, qkvo_a, gates_if, z_cols)
    for i in range(depth):
        z, zif, h = _inproj(xf, norm_mix_g[i], w_main, w_if, i, colscale)
        z3 = z.reshape(nbatch, seq, z_cols)
        zif3 = zif.reshape(nbatch, seq, GATE_LANES)

        ya = _mlstm(z3, zif3, mlstm_gate_b[i], mlstm_norm_g[i])
        yb = _moba_flat(*_moba_prep_t(z3, rope_tables, moba_q_cb, moba_k_cb, moba_v_cb), mask_cols)
        yc = _gmlp(z3, gmlp_u_cb, gmlp_v_cb, gmlp_norm_g[i], gmlp_norm_b[i], gmlp_ws[i], gmlp_bs[i])

        merged = _merge(h, w_gate, ya.reshape(m, -1), yb.reshape(m, -1), yc.reshape(m, -1),
                        w_branch_a, w_branch_b, w_branch_c, i)
        xf = _residual_matmul(merged, w_out, i, xf, tm=2048)

        hidden = _relu2_matmul(xf, norm_mlp_g[i], w_mlp_up, i)
        xf = _residual_matmul(hidden, w_mlp_down, i, xf, tm=2048)
        xf = _ple(xf, norm_ple_g[i], w_ple_gate, p_flat, w_ple_proj, i)

    return _rmsnorm(xf, final_norm_g, F32).reshape(nbatch, seq, d)
```

```python
import functools

import jax
import jax.numpy as jnp
import numpy as np
from jax import lax
from jax.experimental import pallas as pl
from jax.experimental.pallas import tpu as pltpu

F32 = jnp.float32
BF16 = jnp.bfloat16

HEAD_DIM = 128
MLSTM_HEADS = 4
MLSTM_WIDTH = MLSTM_HEADS * HEAD_DIM
MOBA_HEADS = 8
MOBA_WIDTH = MOBA_HEADS * HEAD_DIM
MOBA_BLOCK = 256
MOBA_TOPK = 3
ROPE_THETA = 500000.0
ROPE_DIM = HEAD_DIM // 4
GMLP_WIDTH = 512
GMLP_GROUPS = 4
GMLP_CHUNK = 128
N_BRANCHES = 3
NORM_EPS = 1e-6

LANES = 128
V7X_VMEM_BYTES = 64 * 1024 * 1024
VMEM_CEILING = V7X_VMEM_BYTES - 8 * 1024 * 1024

MLSTM_KERNEL_CHUNK = 128
STATE_ROWS = 16
GATE_LANES = LANES
MASK_BIG = 2.0 ** 100
MOBA_GROUP = 4

NT_DIMS = (((1,), (1,)), ((), ()))
TN_DIMS = (((0,), (0,)), ((), ()))


def _tiles(n, t):
    count, rest = divmod(n, t)
    assert rest == 0 and count > 0, (n, t)
    return count


def _nbytes(shape, dtype):
    return int(np.prod(shape)) * jnp.dtype(dtype).itemsize


def _params(semantics, blocks, scratch=()):
    need = 2 * sum(_nbytes(s, d) for s, d in blocks) + sum(_nbytes(s, d) for s, d in scratch)
    limit = min(VMEM_CEILING, need + need // 4 + 4 * 1024 * 1024)
    return pltpu.CompilerParams(dimension_semantics=semantics, vmem_limit_bytes=limit)


def _rmsnorm_kernel(x_ref, g_ref, o_ref):
    x = x_ref[...]
    y = x * lax.rsqrt(jnp.mean(x * x, axis=-1, keepdims=True) + NORM_EPS)
    o_ref[...] = (y * g_ref[...]).astype(o_ref.dtype)


def _rmsnorm(x, g, out_dtype, tm=512):
    m, d = x.shape
    return pl.pallas_call(
        _rmsnorm_kernel,
        grid=(_tiles(m, tm),),
        in_specs=[pl.BlockSpec((tm, d), lambda i: (i, 0)),
                  pl.BlockSpec((1, d), lambda i: (0, 0))],
        out_specs=pl.BlockSpec((tm, d), lambda i: (i, 0)),
        out_shape=jax.ShapeDtypeStruct((m, d), out_dtype),
        compiler_params=_params(("arbitrary",), [((tm, d), F32), ((tm, d), out_dtype)]),
        name="rmsnorm",
    )(x, g.reshape(1, d))


def _win_split_kernel(wm_ref, wif_ref, wg_ref, main_ref, if_ref, gate_ref):
    main_ref[...] = wm_ref[0].T.astype(BF16)
    gate_ref[...] = wg_ref[0].T.astype(BF16)
    _, n_if, k = wif_ref.shape
    padded = jnp.concatenate([wif_ref[0], jnp.zeros((GATE_LANES - n_if, k), F32)], axis=0)
    if_ref[...] = padded.T.astype(BF16)


def _win_split(w_in, head, n_if, main_cols, tn=512):
    depth, k, cols = w_in.shape
    gate_cols = cols - main_cols - n_if
    assert gate_cols == main_cols and head % tn == 0
    w_t = jnp.swapaxes(w_in, 1, 2)
    head_steps = head // tn
    blocks = [((tn, k), F32), ((n_if, k), F32), ((tn, k), F32), ((k, tn), BF16), ((k, GATE_LANES), BF16),
              ((k, tn), BF16)]

    def main_rows(l, r):
        return (l, pl.multiple_of(r * tn + jnp.where(r >= head_steps, n_if, 0), n_if), 0)

    def window(rows, start_of):
        return pl.BlockSpec((pl.Element(1), pl.Element(rows), pl.Element(k)), start_of)

    return pl.pallas_call(
        _win_split_kernel,
        grid=(depth, _tiles(main_cols, tn)),
        in_specs=[window(tn, main_rows),
                  window(n_if, lambda l, r: (l, head, 0)),
                  window(tn, lambda l, r: (l, pl.multiple_of(main_cols + n_if + r * tn, n_if), 0))],
        out_specs=[pl.BlockSpec((None, k, tn), lambda l, r: (l, 0, r)),
                   pl.BlockSpec((None, k, GATE_LANES), lambda l, r: (l, 0, 0)),
                   pl.BlockSpec((None, k, tn), lambda l, r: (l, 0, r))],
        out_shape=[jax.ShapeDtypeStruct((depth, k, main_cols), BF16),
                   jax.ShapeDtypeStruct((depth, k, GATE_LANES), BF16),
                   jax.ShapeDtypeStruct((depth, k, gate_cols), BF16)],
        compiler_params=_params(("arbitrary", "arbitrary"), blocks),
        name="win_split",
    )(w_t, w_t, w_t)


def _rmsnorm_rows(x_ref, g_ref, h_ref, rows=256):
    for r0 in range(0, x_ref.shape[0], rows):
        x = x_ref[r0:r0 + rows, :]
        y = x * lax.rsqrt(jnp.mean(x * x, axis=-1, keepdims=True) + NORM_EPS)
        h_ref[r0:r0 + rows, :] = (y * g_ref[...]).astype(h_ref.dtype)


def _inproj_kernel(x_ref, g_ref, w_ref, wif_ref, cs_ref, z_ref, zif_ref, h_ref):
    @pl.when(pl.program_id(1) == 0)
    def _():
        _rmsnorm_rows(x_ref, g_ref, h_ref)
        zif_ref[...] = jnp.dot(h_ref[...], wif_ref[...], preferred_element_type=F32)

    acc = jnp.dot(h_ref[...], w_ref[...], preferred_element_type=F32)
    z_ref[...] = acc * cs_ref[...]


def _inproj(x, g, w_main, w_if, layer, colscale, tm=1024, tn=1024):
    m, k = x.shape
    n = w_main.shape[2]
    blocks = [((tm, k), F32), ((k, tn), BF16), ((k, GATE_LANES), BF16), ((1, tn), F32),
              ((tm, tn), F32), ((tm, GATE_LANES), F32), ((tm, k), BF16)]
    return pl.pallas_call(
        _inproj_kernel,
        grid=(_tiles(m, tm), _tiles(n, tn)),
        in_specs=[pl.BlockSpec((tm, k), lambda i, j: (i, 0)),
                  pl.BlockSpec((1, k), lambda i, j: (0, 0)),
                  pl.BlockSpec((None, k, tn), lambda i, j: (layer, 0, j)),
                  pl.BlockSpec((None, k, GATE_LANES), lambda i, j: (layer, 0, 0)),
                  pl.BlockSpec((1, tn), lambda i, j: (0, j))],
        out_specs=[pl.BlockSpec((tm, tn), lambda i, j: (i, j)),
                   pl.BlockSpec((tm, GATE_LANES), lambda i, j: (i, 0)),
                   pl.BlockSpec((tm, k), lambda i, j: (i, 0))],
        out_shape=[jax.ShapeDtypeStruct((m, n), F32),
                   jax.ShapeDtypeStruct((m, GATE_LANES), F32),
                   jax.ShapeDtypeStruct((m, k), BF16)],
        compiler_params=_params(("arbitrary", "arbitrary"), blocks),
        name="in_proj",
    )(x, g.reshape(1, k), w_main, w_if, colscale)


def _log_sigmoid(x):
    return jnp.minimum(x, 0.0) - jnp.log1p(jnp.exp(-jnp.abs(x)))


def _exact_tril_matmul(tril, x):
    tril = tril.astype(BF16)
    hi = x.astype(BF16)
    rest = x - hi.astype(F32)
    mid = rest.astype(BF16)
    lo = (rest - mid.astype(F32)).astype(BF16)
    return (jnp.dot(tril, hi, preferred_element_type=F32) + jnp.dot(tril, mid, preferred_element_type=F32)
            + jnp.dot(tril, lo, preferred_element_type=F32))


def _mlstm_kernel(q_ref, k_ref, v_ref, o_ref, zif_ref, gb_ref, ng_ref, y_ref, c_sc, n_sc, m_sc):
    nbatch, chunk, _ = q_ref.shape
    heads = MLSTM_HEADS

    @pl.when(pl.program_id(0) == 0)
    def _():
        c_sc[...] = jnp.zeros_like(c_sc)
        n_sc[...] = jnp.zeros_like(n_sc)
        m_sc[...] = jnp.zeros_like(m_sc)

    row = lax.broadcasted_iota(jnp.int32, (chunk, chunk), 0)
    col = lax.broadcasted_iota(jnp.int32, (chunk, chunk), 1)
    causal_t = row <= col
    tril = jnp.where(col <= row, 1.0, 0.0).astype(F32)
    lane = lax.broadcasted_iota(jnp.int32, (chunk, GATE_LANES), 1)

    for b in range(nbatch):
        pre = zif_ref[b] + gb_ref[...]
        gates = jnp.where(lane < heads, pre, _log_sigmoid(pre))
        gcum = _exact_tril_matmul(tril, gates)
        gates_t = gates.T
        gcum_t = gcum.T
        for h in range(heads):
            s = b * heads + h
            sl = slice(h * HEAD_DIM, (h + 1) * HEAD_DIM)
            q = q_ref[b, :, sl]
            k = k_ref[b, :, sl]
            v = v_ref[b, :, sl]
            q_t = q.T
            qb = q.astype(BF16)
            kb = k.astype(BF16)
            q_tb = q_t.astype(BF16)
            v_t = v.T
            g_row = gcum_t[heads + h:heads + h + 1, :]
            i_row = gates_t[h:h + 1, :]
            ig_col = gates[:, h:h + 1] - gcum[:, heads + h:heads + h + 1]
            m_prev = m_sc[s][:, 0:1]
            c_prev = c_sc[s]
            n_prev = n_sc[s]

            log_w = jnp.where(causal_t, g_row + ig_col, -jnp.inf)
            log_a = g_row + m_prev
            m_row = jnp.maximum(jnp.max(log_w, axis=0, keepdims=True), log_a)
            qk = lax.dot_general(kb, qb, NT_DIMS, preferred_element_type=F32) * jnp.exp(log_w - m_row)
            a = jnp.exp(log_a - m_row)
            num = (jnp.dot(v_t.astype(BF16), qk.astype(BF16), preferred_element_type=F32)
                   + a * jnp.dot(c_prev.astype(BF16), q_tb, preferred_element_type=F32))
            n_dot_q = jnp.dot(n_prev.astype(BF16), q_tb, preferred_element_type=F32)[0:1]
            den = jnp.sum(qk, axis=0, keepdims=True) + a * n_dot_q
            h_out = num / jnp.maximum(jnp.abs(den), jnp.exp(-m_row))

            g_last = g_row[:, chunk - 1:chunk]
            log_u = g_last - g_row + i_row
            m_new = jnp.maximum(g_last + m_prev, jnp.max(log_u, axis=1, keepdims=True))
            decay = jnp.exp(g_last + m_prev - m_new)
            u = jnp.exp(log_u - m_new)
            c_sc[s] = decay * c_prev + jnp.dot((v_t * u).astype(BF16), kb, preferred_element_type=F32)
            u_rows = jnp.broadcast_to(u, (STATE_ROWS, chunk)).astype(BF16)
            n_sc[s] = decay * n_prev + jnp.dot(u_rows, kb, preferred_element_type=F32)
            m_sc[s] = jnp.broadcast_to(m_new, (1, LANES))

            yn = (h_out * lax.rsqrt(jnp.mean(h_out * h_out, axis=0, keepdims=True) + NORM_EPS)).T
            y_ref[b, :, sl] = (jax.nn.sigmoid(o_ref[b, :, sl]) * (yn * ng_ref[:, sl])).astype(y_ref.dtype)


def _mlstm(z3, zif3, gate_b, norm_g):
    nbatch, seq, _ = z3.shape
    chunk = MLSTM_KERNEL_CHUNK
    w = MLSTM_WIDTH
    streams = nbatch * MLSTM_HEADS
    gb = jnp.pad(gate_b, (0, GATE_LANES - gate_b.shape[0])).reshape(1, GATE_LANES)
    blocks = [((nbatch, chunk, w), F32)] * 4 + [((nbatch, chunk, GATE_LANES), F32),
                                               ((nbatch, chunk, w), BF16)]
    scratch = [((streams, HEAD_DIM, HEAD_DIM), F32), ((streams, STATE_ROWS, LANES), F32),
               ((streams, 1, LANES), F32)]

    def zcol(cb):
        return pl.BlockSpec((nbatch, chunk, w), lambda c: (0, c, cb))

    return pl.pallas_call(
        _mlstm_kernel,
        grid=(_tiles(seq, chunk),),
        in_specs=[zcol(0), zcol(1), zcol(2), zcol(3),
                  pl.BlockSpec((nbatch, chunk, GATE_LANES), lambda c: (0, c, 0)),
                  pl.BlockSpec((1, GATE_LANES), lambda c: (0, 0)),
                  pl.BlockSpec((1, w), lambda c: (0, 0))],
        out_specs=pl.BlockSpec((nbatch, chunk, w), lambda c: (0, c, 0)),
        out_shape=jax.ShapeDtypeStruct((nbatch, seq, w), BF16),
        scratch_shapes=[pltpu.VMEM(s, d) for s, d in scratch],
        compiler_params=_params(("arbitrary",), blocks, scratch),
        name="mlstm",
    )(z3, z3, z3, z3, zif3, gb, norm_g.reshape(1, w))


def _rope_tables(seq):
    half = ROPE_DIM // 2
    inv_freq = ROPE_THETA ** (-jnp.arange(0, ROPE_DIM, 2, dtype=F32) / ROPE_DIM)
    ang = jnp.arange(seq, dtype=F32)[:, None] * inv_freq[None, :]
    cos = jnp.cos(ang)
    sin = jnp.sin(ang)
    ones = jnp.ones((seq, HEAD_DIM - ROPE_DIM), F32)
    cos_tab = jnp.concatenate([cos, cos, ones], axis=1)
    sin_tab = jnp.concatenate([-sin, sin, 0.0 * ones], axis=1)
    assert cos_tab.shape == (seq, HEAD_DIM) and half * 2 == ROPE_DIM
    return cos_tab, sin_tab


def _rotary(t, cos, sin):
    half = ROPE_DIM // 2
    lane = lax.broadcasted_iota(jnp.int32, t.shape, 1)
    upper = pltpu.roll(t, HEAD_DIM - half, axis=1)
    lower = pltpu.roll(t, half, axis=1)
    partner = jnp.where(lane < half, upper, lower)
    return jnp.where(lane < ROPE_DIM, t * cos + partner * sin, t)


def _moba_prep_kernel(q_ref, k_ref, v_ref, cos_ref, sin_ref, qa_ref, kb_ref, vb_ref, km_sc):
    blk = q_ref.shape[1]
    d = HEAD_DIM
    j = pl.program_id(1)

    @pl.when(j == 0)
    def _():
        km_sc[...] = jnp.zeros_like(km_sc)

    cos = cos_ref[...]
    sin = sin_ref[...]
    lane = lax.broadcasted_iota(jnp.int32, (blk, LANES), 1)
    lane_f = lane.astype(F32)
    mean_row = lax.broadcasted_iota(jnp.int32, (LANES, d), 0)
    vb_ref[0] = v_ref[0].astype(BF16)
    for h in range(MOBA_HEADS):
        sl = slice(h * d, (h + 1) * d)
        qf = _rotary(q_ref[0, :, sl], cos, sin)
        kk = _rotary(k_ref[0, :, sl], cos, sin)
        kb_ref[0, :, sl] = kk.astype(BF16)

        gate = lax.dot_general(qf, km_sc[h], NT_DIMS, preferred_element_type=F32,
                               precision=lax.Precision.HIGHEST)
        gate = jnp.where(lane < j, gate, -jnp.inf)
        sel_m1 = jnp.full((blk, LANES), -1.0, F32)
        for _ in range(MOBA_TOPK):
            mx = jnp.max(gate, axis=1, keepdims=True)
            first = jnp.min(jnp.where(gate == mx, lane_f, float(LANES)), axis=1, keepdims=True)
            first = jnp.where(mx > -jnp.inf, first, -1.0)
            hit = lane_f == first
            sel_m1 = jnp.where(hit, 0.0, sel_m1)
            gate = jnp.where(hit, -jnp.inf, gate)
        qa_ref[0, :, 2 * h * d:(2 * h + 1) * d] = qf.astype(BF16)
        qa_ref[0, :, (2 * h + 1) * d:(2 * h + 2) * d] = sel_m1.astype(BF16)

        km_sc[h] = jnp.where(mean_row == j, jnp.mean(kk, axis=0, keepdims=True), km_sc[h])


def _moba_prep(z3, cos_tab, sin_tab, q_cb, k_cb, v_cb):
    nbatch, seq, _ = z3.shape
    blk = MOBA_BLOCK
    w = MOBA_WIDTH
    d = HEAD_DIM
    blocks = [((1, blk, w), F32)] * 3 + [((blk, d), F32)] * 2 + [((1, blk, 2 * w), BF16), ((1, blk, w), BF16),
                                                                  ((1, blk, w), BF16)]
    scratch = [((MOBA_HEADS, LANES, d), F32)]
    return pl.pallas_call(
        _moba_prep_kernel,
        grid=(nbatch, _tiles(seq, blk)),
        in_specs=[pl.BlockSpec((1, blk, w), lambda b, j: (b, j, q_cb)),
                  pl.BlockSpec((1, blk, w), lambda b, j: (b, j, k_cb)),
                  pl.BlockSpec((1, blk, w), lambda b, j: (b, j, v_cb)),
                  pl.BlockSpec((blk, d), lambda b, j: (j, 0)),
                  pl.BlockSpec((blk, d), lambda b, j: (j, 0))],
        out_specs=[pl.BlockSpec((1, blk, 2 * w), lambda b, j: (b, j, 0)),
                   pl.BlockSpec((1, blk, w), lambda b, j: (b, j, 0)),
                   pl.BlockSpec((1, blk, w), lambda b, j: (b, j, 0))],
        out_shape=[jax.ShapeDtypeStruct((nbatch, seq, 2 * w), BF16),
                   jax.ShapeDtypeStruct((nbatch, seq, w), BF16),
                   jax.ShapeDtypeStruct((nbatch, seq, w), BF16)],
        scratch_shapes=[pltpu.VMEM(s, dt) for s, dt in scratch],
        compiler_params=_params(("arbitrary", "arbitrary"), blocks, scratch),
        name="moba_prep",
    )(z3, z3, z3, cos_tab, sin_tab)


def _moba_attn_kernel(q_ref, k_ref, v_ref, o_ref):
    blk = q_ref.shape[1]
    d = HEAD_DIM
    nblk = k_ref.shape[1] // blk
    group = MOBA_GROUP
    j = pl.program_id(2)
    scale = d ** -0.5
    q_aug = q_ref[0]
    lane = lax.broadcasted_iota(jnp.int32, (blk, LANES), 1)

    def softmax_partial(s, v_rows):
        m = jnp.max(s, axis=1, keepdims=True)
        p = jnp.exp(s - m)
        return m, jnp.sum(p, axis=1, keepdims=True), jnp.dot(p.astype(BF16), v_rows, preferred_element_type=F32)

    def past_partial(n):
        c0 = pl.multiple_of(jnp.minimum(n, nblk - 1) * blk, blk)
        onehot = jnp.where(lane == n, MASK_BIG, 0.0).astype(BF16)
        k_aug = jnp.concatenate([k_ref[0, pl.ds(c0, blk), :], onehot], axis=1)
        s = lax.dot_general(q_aug, k_aug, NT_DIMS, preferred_element_type=F32) * scale
        return softmax_partial(s, v_ref[0, pl.ds(c0, blk), :])

    def merge(state, parts):
        m_run, l_run, acc_run = state
        m_new = m_run
        for m, _, _ in parts:
            m_new = jnp.maximum(m_new, m)
        w = jnp.exp(m_run - m_new)
        l_new = w * l_run
        acc = w * acc_run
        for m, l, a in parts:
            w = jnp.exp(m - m_new)
            l_new = l_new + w * l
            acc = acc + w * a
        return m_new, l_new, acc

    r0 = pl.multiple_of(j * blk, blk)
    row = lax.broadcasted_iota(jnp.int32, (blk, blk), 0)
    col = lax.broadcasted_iota(jnp.int32, (blk, blk), 1)
    s_own = lax.dot_general(q_aug[:, :d], k_ref[0, pl.ds(r0, blk), :], NT_DIMS,
                            preferred_element_type=F32) * scale
    s_own = jnp.where(col <= row, s_own, -jnp.inf)
    state = softmax_partial(s_own, v_ref[0, pl.ds(r0, blk), :])

    state = merge(state, [past_partial(n) for n in range(group)])
    ngroups = lax.div(j + (group - 1), group)

    def group_body(g, st):
        return merge(st, [past_partial(g * group + i) for i in range(group)])

    _, l_fin, acc = lax.fori_loop(1, ngroups, group_body, state)
    o_ref[0] = (acc / l_fin).astype(o_ref.dtype)


def _moba_attn(q_aug, kb, vb):
    nbatch, seq, w = kb.shape
    blk = MOBA_BLOCK
    d = HEAD_DIM
    blocks = [((1, blk, 2 * d), BF16), ((1, seq, d), BF16), ((1, seq, d), BF16), ((1, blk, d), BF16)]
    return pl.pallas_call(
        _moba_attn_kernel,
        grid=(nbatch, MOBA_HEADS, _tiles(seq, blk)),
        in_specs=[pl.BlockSpec((1, blk, 2 * d), lambda b, h, j: (b, j, h)),
                  pl.BlockSpec((1, seq, d), lambda b, h, j: (b, 0, h)),
                  pl.BlockSpec((1, seq, d), lambda b, h, j: (b, 0, h))],
        out_specs=pl.BlockSpec((1, blk, d), lambda b, h, j: (b, j, h)),
        out_shape=jax.ShapeDtypeStruct((nbatch, seq, w), BF16),
        compiler_params=_params(("arbitrary", "arbitrary", "arbitrary"), blocks),
        name="moba_attn",
    )(q_aug, kb, vb)


def _rope_tables_t(seq):
    inv_freq = ROPE_THETA ** (-jnp.arange(0, ROPE_DIM, 2, dtype=F32) / ROPE_DIM)
    ang = jnp.arange(seq, dtype=F32)[:, None] * inv_freq[None, :]
    cos = jnp.cos(ang).T
    sin = jnp.sin(ang).T
    return jnp.concatenate([cos, cos], axis=0), jnp.concatenate([-sin, sin], axis=0)


def _moba_prep_t_kernel(q_ref, k_ref, v_ref, cos_ref, sin_ref, cost_ref, sint_ref, qa_ref, kb_ref, vt_ref, km_sc):
    blk = q_ref.shape[1]
    d = HEAD_DIM
    half = ROPE_DIM // 2
    nsel = km_sc.shape[1]
    j = pl.program_id(1)

    @pl.when(j == 0)
    def _():
        km_sc[...] = jnp.zeros_like(km_sc)

    cos = cos_ref[...]
    sin = sin_ref[...]
    cos_t = cost_ref[...]
    sin_t = sint_ref[...]
    blk_id = lax.broadcasted_iota(jnp.int32, (nsel, blk), 0)
    blk_id_f = blk_id.astype(F32)
    mean_row = lax.broadcasted_iota(jnp.int32, (nsel, d), 0)
    for h in range(MOBA_HEADS):
        sl = slice(h * d, (h + 1) * d)
        q_t = q_ref[0, :, sl].T
        top = q_t[:ROPE_DIM]
        partner = jnp.concatenate([top[half:], top[:half]], axis=0)
        q_t = jnp.concatenate([top * cos_t + partner * sin_t, q_t[ROPE_DIM:]], axis=0)

        gate = jnp.dot(km_sc[h], q_t, preferred_element_type=F32,
                       precision=lax.Precision.HIGHEST)
        gate = jnp.where(blk_id < j, gate, -jnp.inf)
        sel_m1 = jnp.full((nsel, blk), -1.0, F32)
        for _ in range(MOBA_TOPK):
            mx = jnp.max(gate, axis=0, keepdims=True)
            first = jnp.min(jnp.where(gate == mx, blk_id_f, float(nsel)), axis=0, keepdims=True)
            first = jnp.where(mx > -jnp.inf, first, -1.0)
            hit = blk_id_f == first
            sel_m1 = jnp.where(hit, 0.0, sel_m1)
            gate = jnp.where(hit, -jnp.inf, gate)
        qa_ref[0, h, 0, 0:d, :] = q_t.astype(BF16)
        qa_ref[0, h, 0, d:d + nsel, :] = sel_m1.astype(BF16)
        qa_ref[0, h, 0, d + nsel:, :] = jnp.full((d - nsel, blk), -1.0, BF16)

        kk = _rotary(k_ref[0, :, sl], cos, sin)
        kb_ref[0, :, sl] = kk.astype(BF16)
        km_sc[h] = jnp.where(mean_row == j, jnp.mean(kk, axis=0, keepdims=True), km_sc[h])

        vt_ref[0, h, 0] = v_ref[0, :, sl].T.astype(BF16)


def _moba_prep_t(z3, tables, q_cb, k_cb, v_cb):
    nbatch, seq, _ = z3.shape
    blk = MOBA_BLOCK
    nblk = _tiles(seq, blk)
    nsel = -(-nblk // 16) * 16
    assert nsel <= HEAD_DIM
    w = MOBA_WIDTH
    d = HEAD_DIM
    hh = MOBA_HEADS
    cos_tab, sin_tab, cos_t, sin_t = tables
    blocks = ([((1, blk, w), F32)] * 3 + [((blk, d), F32)] * 2 + [((ROPE_DIM, blk), F32)] * 2
              + [((hh, 2 * d, blk), BF16), ((1, blk, w), BF16), ((hh, d, blk), BF16)])
    scratch = [((hh, nsel, d), F32)]
    return pl.pallas_call(
        _moba_prep_t_kernel,
        grid=(nbatch, nblk),
        in_specs=[pl.BlockSpec((1, blk, w), lambda b, j: (b, j, q_cb)),
                  pl.BlockSpec((1, blk, w), lambda b, j: (b, j, k_cb)),
                  pl.BlockSpec((1, blk, w), lambda b, j: (b, j, v_cb)),
                  pl.BlockSpec((blk, d), lambda b, j: (j, 0)),
                  pl.BlockSpec((blk, d), lambda b, j: (j, 0)),
                  pl.BlockSpec((ROPE_DIM, blk), lambda b, j: (0, j)),
                  pl.BlockSpec((ROPE_DIM, blk), lambda b, j: (0, j))],
        out_specs=[pl.BlockSpec((1, hh, 1, 2 * d, blk), lambda b, j: (b, 0, j, 0, 0)),
                   pl.BlockSpec((1, blk, w), lambda b, j: (b, j, 0)),
                   pl.BlockSpec((1, hh, 1, d, blk), lambda b, j: (b, 0, j, 0, 0))],
        out_shape=[jax.ShapeDtypeStruct((nbatch, hh, nblk, 2 * d, blk), BF16),
                   jax.ShapeDtypeStruct((nbatch, seq, w), BF16),
                   jax.ShapeDtypeStruct((nbatch, hh, nblk, d, blk), BF16)],
        scratch_shapes=[pltpu.VMEM(s, dt) for s, dt in scratch],
        compiler_params=_params(("arbitrary", "arbitrary"), blocks, scratch),
        name="moba_prep",
    )(z3, z3, z3, cos_tab, sin_tab, cos_t, sin_t)


def _moba_attn_t_kernel(q_ref, k_ref, mask_ref, vt_ref, vown_ref, o_ref, s_sc):
    blk = q_ref.shape[-1]
    d = HEAD_DIM
    gkeys = vt_ref.shape[-1]
    group = gkeys // blk
    last_group = vt_ref.shape[2] - 1
    j = pl.program_id(2)
    c_exp = (d ** -0.5) * np.log2(np.e).astype(np.float32)
    q_aug = q_ref[0, 0, 0]

    def group_scores(g, slot):
        c0 = pl.multiple_of(g * gkeys, gkeys)
        k_aug = jnp.concatenate([k_ref[0, pl.ds(c0, gkeys), :], mask_ref[pl.ds(c0, gkeys), :]], axis=1)
        s_t = jnp.dot(k_aug, q_aug, preferred_element_type=F32)
        s_sc[slot] = s_t
        return jnp.max(s_t, axis=0, keepdims=True)

    r0 = pl.multiple_of(j * blk, blk)
    key = lax.broadcasted_iota(jnp.int32, (blk, blk), 0)
    qry = lax.broadcasted_iota(jnp.int32, (blk, blk), 1)
    s_own = jnp.dot(k_ref[0, pl.ds(r0, blk), :], q_aug[:d], preferred_element_type=F32)
    s_own = jnp.where(key <= qry, s_own, -jnp.inf)
    m0 = jnp.max(s_own, axis=0, keepdims=True)
    p0 = jnp.exp2((s_own - m0) * c_exp)
    l0 = jnp.sum(p0, axis=0, keepdims=True)
    acc0 = jnp.dot(vown_ref[0, 0, 0], p0.astype(BF16), preferred_element_type=F32)

    def fold_group(g, state, slot):
        m_run, l_run, acc, s_max = state
        next_max = group_scores(jnp.minimum(g + 1, last_group), 1 - slot)
        m_new = jnp.maximum(m_run, s_max)
        alpha = jnp.exp2((m_run - m_new) * c_exp)
        p = jnp.exp2((s_sc[slot] - m_new) * c_exp)
        l_new = alpha * l_run + jnp.sum(p, axis=0, keepdims=True)
        acc = alpha * acc + jnp.dot(vt_ref[0, 0, g], p.astype(BF16), preferred_element_type=F32)
        return m_new, l_new, acc, next_max

    def group_body(g, state):
        return lax.cond(g % 2 == 0, lambda st: fold_group(g, st, 0), lambda st: fold_group(g, st, 1), state)

    ngroups = lax.div(j + (group - 1), group)
    _, l_fin, acc, _ = lax.fori_loop(0, ngroups, group_body, (m0, l0, acc0, group_scores(0, 0)))
    o_ref[0] = (acc / l_fin).T.astype(o_ref.dtype)


def _moba_mask_columns(seq):
    blk_id = jnp.arange(seq, dtype=jnp.int32)[:, None] // MOBA_BLOCK
    return jnp.where(jnp.arange(LANES, dtype=jnp.int32)[None, :] == blk_id, MASK_BIG, 0.0).astype(BF16)


def _moba_attn_t(q_aug, kb, vt, mask_cols):
    nbatch, seq, w = kb.shape
    blk = MOBA_BLOCK
    d = HEAD_DIM
    ngrp, gkeys = vt.shape[2], vt.shape[4]
    group = gkeys // blk
    blocks = [((2 * d, blk), BF16), ((1, seq, d), BF16), ((seq, LANES), BF16), ((ngrp, d, gkeys), BF16),
              ((d, blk), BF16), ((1, blk, d), BF16)]
    scratch = [((2, gkeys, blk), F32)]
    return pl.pallas_call(
        _moba_attn_t_kernel,
        grid=(nbatch, MOBA_HEADS, _tiles(seq, blk)),
        in_specs=[pl.BlockSpec((1, 1, 1, 2 * d, blk), lambda b, h, j: (b, h, j, 0, 0)),
                  pl.BlockSpec((1, seq, d), lambda b, h, j: (b, 0, h)),
                  pl.BlockSpec((seq, LANES), lambda b, h, j: (0, 0)),
                  pl.BlockSpec((1, 1, ngrp, d, gkeys), lambda b, h, j: (b, h, 0, 0, 0)),
                  pl.BlockSpec((1, 1, 1, d, blk), lambda b, h, j: (b, h, j // group, 0, j % group))],
        out_specs=pl.BlockSpec((1, blk, d), lambda b, h, j: (b, j, h)),
        out_shape=jax.ShapeDtypeStruct((nbatch, seq, w), BF16),
        scratch_shapes=[pltpu.VMEM(s, dt) for s, dt in scratch],
        compiler_params=_params(("arbitrary", "arbitrary", "arbitrary"), blocks, scratch),
        name="moba_attn",
    )(q_aug, kb, mask_cols, vt, vt)


def _pipelined(start, count, first, step):
    lo = start % 2

    def pair(t, carry):
        i = start + 2 * t
        return step(i + 1, 1 - lo, step(i, lo, carry))

    carry = lax.fori_loop(0, (count - start) // 2, pair, first)
    if (count - start) % 2:
        carry = step(count - 1, (count - 1) % 2, carry)
    return carry


def _moba_flat_kernel(tile_ref, group_ref, q_ref, k_ref, mask_ref, vt_ref, o_ref,
                      s_sc, p_sc, acc_sc, m_sc, l_sc):
    nblk, d, blk = vt_ref.shape[2:]
    group = MOBA_GROUP
    gkeys = group * blk
    n_items = tile_ref.shape[0]
    c_exp = (d ** -0.5) * np.log2(np.e).astype(np.float32)
    key = lax.broadcasted_iota(jnp.int32, (blk, blk), 0)
    qry = lax.broadcasted_iota(jnp.int32, (blk, blk), 1)

    def weighted_values(first_blk, p):
        acc = None
        for i in range(p.shape[0] // blk):
            part = jnp.dot(vt_ref[0, 0, first_blk + i], p[i * blk:(i + 1) * blk],
                           preferred_element_type=F32)
            acc = part if acc is None else acc + part
        return acc

    def own_scores(j, slot):
        r0 = pl.multiple_of(j * blk, blk)
        s_t = jnp.dot(k_ref[0, pl.ds(r0, blk), :], q_ref[0, 0, j, :d, :], preferred_element_type=F32)
        s_t = jnp.where(key <= qry, s_t, -jnp.inf)
        s_sc[slot, :blk, :] = s_t
        return jnp.max(s_t, axis=0, keepdims=True)

    def own_softmax(j, slot, s_max):
        next_max = own_scores(jnp.minimum(j + 1, nblk - 1), 1 - slot)
        p = jnp.exp2((s_sc[slot, :blk, :] - s_max) * c_exp)
        m_sc[j] = s_max
        l_sc[j] = jnp.sum(p, axis=0, keepdims=True)
        p_sc[slot] = p.astype(BF16)
        return next_max

    def own_values(j, slot):
        acc_sc[j] = weighted_values(j, p_sc[slot])

    def own_step(j, slot, s_max):
        own_values(j - 1, 1 - slot)
        return own_softmax(j, slot, s_max)

    _pipelined(1, nblk, own_softmax(0, 0, own_scores(0, 0)), own_step)
    own_values(nblk - 1, (nblk - 1) % 2)

    def group_scores(i, slot):
        c0 = pl.multiple_of(group_ref[i] * gkeys, gkeys)
        k_aug = jnp.concatenate([k_ref[0, pl.ds(c0, gkeys), :], mask_ref[pl.ds(c0, gkeys), :]], axis=1)
        s_t = jnp.dot(k_aug, q_ref[0, 0, tile_ref[i]], preferred_element_type=F32)
        s_sc[slot] = s_t
        return jnp.max(s_t, axis=0, keepdims=True)

    def group_step(i, slot, s_max):
        next_max = group_scores(jnp.minimum(i + 1, n_items - 1), 1 - slot)
        j = tile_ref[i]
        m_run = m_sc[j]
        m_new = jnp.maximum(m_run, s_max)
        alpha = jnp.exp2((m_run - m_new) * c_exp)
        p = jnp.exp2((s_sc[slot] - m_new) * c_exp)
        m_sc[j] = m_new
        l_sc[j] = alpha * l_sc[j] + jnp.sum(p, axis=0, keepdims=True)
        acc_sc[j] = alpha * acc_sc[j] + weighted_values(group_ref[i] * group, p.astype(BF16))
        return next_max

    _pipelined(0, n_items, group_scores(0, 0), group_step)

    def finish(j, carry):
        r0 = pl.multiple_of(j * blk, blk)
        o_ref[0, pl.ds(r0, blk), :] = (acc_sc[j] / l_sc[j]).T.astype(o_ref.dtype)
        return carry

    lax.fori_loop(0, nblk, finish, 0)


def _moba_flat(q_aug, kb, vt, mask_cols):
    nbatch, seq, w = kb.shape
    blk = MOBA_BLOCK
    d = HEAD_DIM
    nblk = vt.shape[2]
    group = MOBA_GROUP
    items = [(j, g) for j in range(nblk) for g in range(-(-j // group))]
    assert items and _tiles(nblk, group)
    if len(items) % 2:
        items.append((0, 0))
    item_tile = jnp.asarray([j for j, _ in items], jnp.int32)
    item_group = jnp.asarray([g for _, g in items], jnp.int32)
    blocks = [((nblk, 2 * d, blk), BF16), ((1, seq, d), BF16), ((seq, LANES), BF16), ((nblk, d, blk), BF16),
              ((1, seq, d), BF16)]
    scratch = [((2, group * blk, blk), F32), ((2, blk, blk), BF16), ((nblk, d, blk), F32),
               ((nblk, 1, blk), F32), ((nblk, 1, blk), F32)]
    grid_spec = pltpu.PrefetchScalarGridSpec(
        num_scalar_prefetch=2,
        grid=(nbatch, MOBA_HEADS),
        in_specs=[pl.BlockSpec((1, 1, nblk, 2 * d, blk), lambda b, h, *_: (b, h, 0, 0, 0)),
                  pl.BlockSpec((1, seq, d), lambda b, h, *_: (b, 0, h)),
                  pl.BlockSpec((seq, LANES), lambda b, h, *_: (0, 0)),
                  pl.BlockSpec((1, 1, nblk, d, blk), lambda b, h, *_: (b, h, 0, 0, 0))],
        out_specs=pl.BlockSpec((1, seq, d), lambda b, h, *_: (b, 0, h)),
        scratch_shapes=[pltpu.VMEM(s, dt) for s, dt in scratch])
    return pl.pallas_call(
        _moba_flat_kernel,
        grid_spec=grid_spec,
        out_shape=jax.ShapeDtypeStruct((nbatch, seq, w), BF16),
        compiler_params=_params(("arbitrary", "arbitrary"), blocks, scratch),
        name="moba_attn",
    )(item_tile, item_group, q_aug, kb, mask_cols, vt)


def _gelu_tanh(x):
    c = np.sqrt(2.0 / np.pi).astype(np.float32)
    return x * (0.5 * (1.0 + jnp.tanh(c * (x + 0.044715 * (x * x * x)))))


def _gmlp_kernel(u_ref, v_ref, lg_ref, lb_ref, ws_ref, bst_ref, y_ref):
    rows = u_ref.shape[1]
    t = GMLP_CHUNK
    gd = GMLP_WIDTH // GMLP_GROUPS
    v = _gelu_tanh(v_ref[0])
    mu = jnp.mean(v, axis=-1, keepdims=True)
    vc = v - mu
    vln = vc * lax.rsqrt(jnp.mean(vc * vc, axis=-1, keepdims=True) + NORM_EPS) * lg_ref[...] + lb_ref[...]
    vb = vln.astype(BF16)
    row = lax.broadcasted_iota(jnp.int32, (t, t), 0)
    col = lax.broadcasted_iota(jnp.int32, (t, t), 1)
    for g in range(GMLP_GROUPS):
        wg = jnp.where(col <= row, ws_ref[g], 0.0).astype(BF16)
        bias = bst_ref[:, g:g + 1]
        cols = slice(g * gd, (g + 1) * gd)
        for c in range(rows // t):
            rs = slice(c * t, (c + 1) * t)
            mixed = jnp.dot(wg, vb[rs, cols], preferred_element_type=F32) + bias
            y_ref[0, rs, cols] = (_gelu_tanh(u_ref[0, rs, cols]) * mixed).astype(y_ref.dtype)


def _gmlp(z3, u_cb, v_cb, ln_g, ln_b, ws, bs, rows=512):
    nbatch, seq, _ = z3.shape
    w = GMLP_WIDTH
    t = GMLP_CHUNK
    blocks = [((1, rows, w), F32), ((1, rows, w), F32), ((GMLP_GROUPS, t, t), F32), ((1, rows, w), BF16)]
    return pl.pallas_call(
        _gmlp_kernel,
        grid=(nbatch, _tiles(seq, rows)),
        in_specs=[pl.BlockSpec((1, rows, w), lambda b, c: (b, c, u_cb)),
                  pl.BlockSpec((1, rows, w), lambda b, c: (b, c, v_cb)),
                  pl.BlockSpec((1, w), lambda b, c: (0, 0)),
                  pl.BlockSpec((1, w), lambda b, c: (0, 0)),
                  pl.BlockSpec((GMLP_GROUPS, t, t), lambda b, c: (0, 0, 0)),
                  pl.BlockSpec((t, GMLP_GROUPS), lambda b, c: (0, 0))],
        out_specs=pl.BlockSpec((1, rows, w), lambda b, c: (b, c, 0)),
        out_shape=jax.ShapeDtypeStruct((nbatch, seq, w), BF16),
        compiler_params=_params(("arbitrary", "arbitrary"), blocks),
        name="gmlp",
    )(z3, z3, ln_g.reshape(1, w), ln_b.reshape(1, w), ws, bs.T)


def _layer_weight_spec(layer, k, tn, index_of):
    return pl.BlockSpec((None, k, tn), lambda *idx: (layer,) + index_of(*idx))


def _merge_kernel(h_ref, wga_ref, wgb_ref, wgc_ref, ya_ref, yb_ref, yc_ref, wa_ref, wb_ref, wc_ref, o_ref):
    h = h_ref[...]

    def branch(wg_ref, y_ref, w_ref):
        gate = jax.nn.sigmoid(jnp.dot(h, wg_ref[...], preferred_element_type=F32))
        return gate * jnp.dot(y_ref[...], w_ref[...].astype(BF16), preferred_element_type=F32)

    merged = branch(wga_ref, ya_ref, wa_ref) + branch(wgb_ref, yb_ref, wb_ref) + branch(wgc_ref, yc_ref, wc_ref)
    o_ref[...] = merged.astype(o_ref.dtype)


def _merge(h, w_gate, ya, yb, yc, wa, wb, wc, layer, tm=1024, tn=512):
    m, d = h.shape
    nblk = _tiles(d, tn)
    ka, kb, kc = ya.shape[1], yb.shape[1], yc.shape[1]
    blocks = ([((tm, d), BF16)] + [((d, tn), BF16)] * 3
              + [((tm, ka), BF16), ((tm, kb), BF16), ((tm, kc), BF16)]
              + [((ka, tn), F32), ((kb, tn), F32), ((kc, tn), F32), ((tm, tn), BF16)])

    def gate_spec(branch):
        return pl.BlockSpec((None, d, tn), lambda i, j: (layer, 0, branch * nblk + j))

    def col(i, j):
        return (0, j)

    return pl.pallas_call(
        _merge_kernel,
        grid=(_tiles(m, tm), nblk),
        in_specs=[pl.BlockSpec((tm, d), lambda i, j: (i, 0)),
                  gate_spec(0), gate_spec(1), gate_spec(2),
                  pl.BlockSpec((tm, ka), lambda i, j: (i, 0)),
                  pl.BlockSpec((tm, kb), lambda i, j: (i, 0)),
                  pl.BlockSpec((tm, kc), lambda i, j: (i, 0)),
                  _layer_weight_spec(layer, ka, tn, col),
                  _layer_weight_spec(layer, kb, tn, col),
                  _layer_weight_spec(layer, kc, tn, col)],
        out_specs=pl.BlockSpec((tm, tn), lambda i, j: (i, j)),
        out_shape=jax.ShapeDtypeStruct((m, d), BF16),
        compiler_params=_params(("arbitrary", "arbitrary"), blocks),
        name="merge",
    )(h, w_gate, w_gate, w_gate, ya, yb, yc, wa, wb, wc)


def _residual_matmul_kernel(a_ref, w_ref, r_ref, o_ref):
    @pl.when(pl.program_id(2) == 0)
    def _():
        o_ref[...] = r_ref[...]

    o_ref[...] += jnp.dot(a_ref[...], w_ref[...].astype(BF16), preferred_element_type=F32)


def _residual_matmul(a, w, layer, res, tm=1024, tn=1024, tk=1024):
    m, k = a.shape
    n = w.shape[2]
    blocks = [((tm, tk), BF16), ((tk, tn), F32), ((tm, tn), F32), ((tm, tn), F32)]
    return pl.pallas_call(
        _residual_matmul_kernel,
        grid=(_tiles(m, tm), _tiles(n, tn), _tiles(k, tk)),
        in_specs=[pl.BlockSpec((tm, tk), lambda i, j, kk: (i, kk)),
                  _layer_weight_spec(layer, tk, tn, lambda i, j, kk: (kk, j)),
                  pl.BlockSpec((tm, tn), lambda i, j, kk: (i, j))],
        out_specs=pl.BlockSpec((tm, tn), lambda i, j, kk: (i, j)),
        out_shape=jax.ShapeDtypeStruct((m, n), F32),
        compiler_params=_params(("arbitrary", "arbitrary", "arbitrary"), blocks),
        name="residual_matmul",
    )(a, w, res)


def _relu2_matmul_kernel(x_ref, g_ref, w_ref, o_ref, h_sc):
    @pl.when(pl.program_id(1) == 0)
    def _():
        _rmsnorm_rows(x_ref, g_ref, h_sc)

    up = jnp.maximum(jnp.dot(h_sc[...], w_ref[...].astype(BF16), preferred_element_type=F32), 0.0)
    o_ref[...] = (up * up).astype(o_ref.dtype)


def _relu2_matmul(x, g, w, layer, tm=1024, tn=1024):
    m, k = x.shape
    n = w.shape[2]
    blocks = [((tm, k), F32), ((k, tn), F32), ((tm, tn), BF16)]
    scratch = [((tm, k), BF16)]
    return pl.pallas_call(
        _relu2_matmul_kernel,
        grid=(_tiles(m, tm), _tiles(n, tn)),
        in_specs=[pl.BlockSpec((tm, k), lambda i, j: (i, 0)),
                  pl.BlockSpec((1, k), lambda i, j: (0, 0)),
                  _layer_weight_spec(layer, k, tn, lambda i, j: (0, j))],
        out_specs=pl.BlockSpec((tm, tn), lambda i, j: (i, j)),
        out_shape=jax.ShapeDtypeStruct((m, n), BF16),
        scratch_shapes=[pltpu.VMEM(s, dt) for s, dt in scratch],
        compiler_params=_params(("arbitrary", "arbitrary"), blocks, scratch),
        name="mlp_up",
    )(x, g.reshape(1, k), w)


def _ple_kernel(x_ref, g_ref, wg_ref, p_ref, wp_ref, o_ref, h_sc):
    j = pl.program_id(1)

    @pl.when(j == 0)
    def _():
        _rmsnorm_rows(x_ref, g_ref, h_sc)

    tn = o_ref.shape[1]
    gate = jax.nn.sigmoid(jnp.dot(h_sc[...], wg_ref[...].astype(BF16), preferred_element_type=F32))
    emb = jnp.dot(p_ref[...].astype(BF16), wp_ref[...].astype(BF16), preferred_element_type=F32)
    o_ref[...] = x_ref[:, pl.ds(pl.multiple_of(j * tn, tn), tn)] + gate * emb


def _ple(x, g, w_gate, p, w_proj, layer, tm=1024, tn=512):
    m, d = x.shape
    pd = p.shape[2]
    blocks = [((tm, d), F32), ((d, tn), F32), ((tm, pd), F32), ((pd, tn), F32), ((tm, tn), F32)]
    scratch = [((tm, d), BF16)]
    return pl.pallas_call(
        _ple_kernel,
        grid=(_tiles(m, tm), _tiles(d, tn)),
        in_specs=[pl.BlockSpec((tm, d), lambda i, j: (i, 0)),
                  pl.BlockSpec((1, d), lambda i, j: (0, 0)),
                  _layer_weight_spec(layer, d, tn, lambda i, j: (0, j)),
                  pl.BlockSpec((None, tm, pd), lambda i, j: (layer, i, 0)),
                  _layer_weight_spec(layer, pd, tn, lambda i, j: (0, j))],
        out_specs=pl.BlockSpec((tm, tn), lambda i, j: (i, j)),
        out_shape=jax.ShapeDtypeStruct((m, d), F32),
        scratch_shapes=[pltpu.VMEM(s, dt) for s, dt in scratch],
        compiler_params=_params(("arbitrary", "arbitrary"), blocks, scratch),
        name="ple",
    )(x, g.reshape(1, d), w_gate, p, w_proj)


def kernel(x, p, norm_mix_g, w_in, mlstm_gate_b, mlstm_norm_g, gmlp_norm_g, gmlp_norm_b, gmlp_ws, gmlp_bs,
           w_branch_a, w_branch_b, w_branch_c, w_out, norm_mlp_g, w_mlp_up, w_mlp_down, norm_ple_g,
           w_ple_gate, w_ple_proj, final_norm_g):
    nbatch, seq, d = x.shape
    depth = w_in.shape[0]
    m = nbatch * seq
    assert d == MLSTM_WIDTH + MOBA_WIDTH + GMLP_WIDTH

    qkvo_a = 4 * MLSTM_WIDTH
    gates_if = 2 * MLSTM_HEADS
    main_b = qkvo_a + gates_if
    main_cols = 3 * MOBA_WIDTH + 2 * GMLP_WIDTH
    gate_off = main_b + main_cols
    assert w_in.shape[2] == gate_off + N_BRANCHES * d

    moba_q_cb = qkvo_a // MOBA_WIDTH
    moba_k_cb = moba_q_cb + 1
    moba_v_cb = moba_k_cb + 1
    gmlp_u_cb = (qkvo_a + 3 * MOBA_WIDTH) // GMLP_WIDTH
    gmlp_v_cb = gmlp_u_cb + 1

    z_cols = qkvo_a + main_cols
    colscale = jnp.ones((1, z_cols), F32).at[:, MLSTM_WIDTH:2 * MLSTM_WIDTH].set(HEAD_DIM ** -0.5)
    rope_tables = _rope_tables(seq) + _rope_tables_t(seq)
    mask_cols = _moba_mask_columns(seq)

    xf = x.reshape(m, d)
    p_flat = p.reshape(depth, m, p.shape[-1])
    w_main, w_if, w_gate = _win_split(w_in, qkvo_a, gates_if, z_cols)
    for i in range(depth):
        z, zif, h = _inproj(xf, norm_mix_g[i], w_main, w_if, i, colscale)
        z3 = z.reshape(nbatch, seq, z_cols)
        zif3 = zif.reshape(nbatch, seq, GATE_LANES)

        ya = _mlstm(z3, zif3, mlstm_gate_b[i], mlstm_norm_g[i])
        yb = _moba_flat(*_moba_prep_t(z3, rope_tables, moba_q_cb, moba_k_cb, moba_v_cb), mask_cols)
        yc = _gmlp(z3, gmlp_u_cb, gmlp_v_cb, gmlp_norm_g[i], gmlp_norm_b[i], gmlp_ws[i], gmlp_bs[i])

        merged = _merge(h, w_gate, ya.reshape(m, -1), yb.reshape(m, -1), yc.reshape(m, -1),
                        w_branch_a, w_branch_b, w_branch_c, i)
        xf = _residual_matmul(merged, w_out, i, xf, tm=2048)

        hidden = _relu2_matmul(xf, norm_mlp_g[i], w_mlp_up, i)
        xf = _residual_matmul(hidden, w_mlp_down, i, xf, tm=2048)
        xf = _ple(xf, norm_ple_g[i], w_ple_gate, p_flat, w_ple_proj, i)

    return _rmsnorm(xf, final_norm_g, F32).reshape(nbatch, seq, d)
```

```python
import jax
import jax.numpy as jnp
import numpy as np
from jax import lax
from jax.experimental import pallas as pl
from jax.experimental.pallas import tpu as pltpu

F32 = jnp.float32
BF16 = jnp.bfloat16

HEAD_DIM = 128
MLSTM_HEADS = 4
MLSTM_WIDTH = MLSTM_HEADS * HEAD_DIM
MOBA_HEADS = 8
MOBA_WIDTH = MOBA_HEADS * HEAD_DIM
MOBA_BLOCK = 256
MOBA_TOPK = 3
ROPE_THETA = 500000.0
ROPE_DIM = HEAD_DIM // 4
GMLP_WIDTH = 512
GMLP_GROUPS = 4
GMLP_CHUNK = 128
N_BRANCHES = 3
NORM_EPS = 1e-6

LANES = 128
V7X_VMEM_BYTES = 64 * 1024 * 1024
VMEM_CEILING = V7X_VMEM_BYTES - 8 * 1024 * 1024

MLSTM_KERNEL_CHUNK = 128
STATE_ROWS = 16
GATE_LANES = LANES
MASK_BIG = 2.0 ** 100
MOBA_GROUP = 4

NT_DIMS = (((1,), (1,)), ((), ()))


def _tiles(n, t):
    count, rest = divmod(n, t)
    assert rest == 0 and count > 0, (n, t)
    return count


def _nbytes(shape, dtype):
    return int(np.prod(shape)) * jnp.dtype(dtype).itemsize


def _params(semantics, blocks, scratch=()):
    need = 2 * sum(_nbytes(s, d) for s, d in blocks) + sum(_nbytes(s, d) for s, d in scratch)
    limit = min(VMEM_CEILING, need + need // 4 + 4 * 1024 * 1024)
    return pltpu.CompilerParams(dimension_semantics=semantics, vmem_limit_bytes=limit)


def _rmsnorm_kernel(x_ref, g_ref, o_ref):
    x = x_ref[...]
    y = x * lax.rsqrt(jnp.mean(x * x, axis=-1, keepdims=True) + NORM_EPS)
    o_ref[...] = (y * g_ref[...]).astype(o_ref.dtype)


def _rmsnorm(x, g, out_dtype, tm=512):
    m, d = x.shape
    return pl.pallas_call(
        _rmsnorm_kernel,
        grid=(_tiles(m, tm),),
        in_specs=[pl.BlockSpec((tm, d), lambda i: (i, 0)),
                  pl.BlockSpec((1, d), lambda i: (0, 0))],
        out_specs=pl.BlockSpec((tm, d), lambda i: (i, 0)),
        out_shape=jax.ShapeDtypeStruct((m, d), out_dtype),
        compiler_params=_params(("arbitrary",), [((tm, d), F32), ((tm, d), out_dtype)]),
        name="rmsnorm",
    )(x, g.reshape(1, d))


def _win_split_kernel(wm_ref, wif_ref, wg_ref, main_ref, if_ref, gate_ref):
    main_ref[...] = wm_ref[0].T.astype(BF16)
    gate_ref[...] = wg_ref[0].T.astype(BF16)
    _, n_if, k = wif_ref.shape
    padded = jnp.concatenate([wif_ref[0], jnp.zeros((GATE_LANES - n_if, k), F32)], axis=0)
    if_ref[...] = padded.T.astype(BF16)


def _win_split(w_in, head, n_if, main_cols, tn=512):
    depth, k, cols = w_in.shape
    gate_cols = cols - main_cols - n_if
    assert gate_cols == main_cols and head % tn == 0
    w_t = jnp.swapaxes(w_in, 1, 2)
    head_steps = head // tn
    blocks = [((tn, k), F32), ((n_if, k), F32), ((tn, k), F32), ((k, tn), BF16), ((k, GATE_LANES), BF16),
              ((k, tn), BF16)]

    def main_rows(l, r):
        return (l, pl.multiple_of(r * tn + jnp.where(r >= head_steps, n_if, 0), n_if), 0)

    def window(rows, start_of):
        return pl.BlockSpec((pl.Element(1), pl.Element(rows), pl.Element(k)), start_of)

    return pl.pallas_call(
        _win_split_kernel,
        grid=(depth, _tiles(main_cols, tn)),
        in_specs=[window(tn, main_rows),
                  window(n_if, lambda l, r: (l, head, 0)),
                  window(tn, lambda l, r: (l, pl.multiple_of(main_cols + n_if + r * tn, n_if), 0))],
        out_specs=[pl.BlockSpec((None, k, tn), lambda l, r: (l, 0, r)),
                   pl.BlockSpec((None, k, GATE_LANES), lambda l, r: (l, 0, 0)),
                   pl.BlockSpec((None, k, tn), lambda l, r: (l, 0, r))],
        out_shape=[jax.ShapeDtypeStruct((depth, k, main_cols), BF16),
                   jax.ShapeDtypeStruct((depth, k, GATE_LANES), BF16),
                   jax.ShapeDtypeStruct((depth, k, gate_cols), BF16)],
        compiler_params=_params(("arbitrary", "arbitrary"), blocks),
        name="win_split",
    )(w_t, w_t, w_t)


def _rmsnorm_rows(x_ref, g_ref, h_ref, rows=256):
    for r0 in range(0, x_ref.shape[0], rows):
        x = x_ref[r0:r0 + rows, :]
        y = x * lax.rsqrt(jnp.mean(x * x, axis=-1, keepdims=True) + NORM_EPS)
        h_ref[r0:r0 + rows, :] = (y * g_ref[...]).astype(h_ref.dtype)


def _inproj_kernel(x_ref, g_ref, w_ref, wif_ref, cs_ref, z_ref, zif_ref, h_ref):
    @pl.when(pl.program_id(1) == 0)
    def _():
        _rmsnorm_rows(x_ref, g_ref, h_ref)
        zif_ref[...] = jnp.dot(h_ref[...], wif_ref[...], preferred_element_type=F32)

    acc = jnp.dot(h_ref[...], w_ref[...], preferred_element_type=F32)
    z_ref[...] = acc * cs_ref[...]


def _inproj(x, g, w_main, w_if, layer, colscale, tm=1024, tn=1024):
    m, k = x.shape
    n = w_main.shape[2]
    blocks = [((tm, k), F32), ((k, tn), BF16), ((k, GATE_LANES), BF16), ((1, tn), F32),
              ((tm, tn), F32), ((tm, GATE_LANES), F32), ((tm, k), BF16)]
    return pl.pallas_call(
        _inproj_kernel,
        grid=(_tiles(m, tm), _tiles(n, tn)),
        in_specs=[pl.BlockSpec((tm, k), lambda i, j: (i, 0)),
                  pl.BlockSpec((1, k), lambda i, j: (0, 0)),
                  pl.BlockSpec((None, k, tn), lambda i, j: (layer, 0, j)),
                  pl.BlockSpec((None, k, GATE_LANES), lambda i, j: (layer, 0, 0)),
                  pl.BlockSpec((1, tn), lambda i, j: (0, j))],
        out_specs=[pl.BlockSpec((tm, tn), lambda i, j: (i, j)),
                   pl.BlockSpec((tm, GATE_LANES), lambda i, j: (i, 0)),
                   pl.BlockSpec((tm, k), lambda i, j: (i, 0))],
        out_shape=[jax.ShapeDtypeStruct((m, n), F32),
                   jax.ShapeDtypeStruct((m, GATE_LANES), F32),
                   jax.ShapeDtypeStruct((m, k), BF16)],
        compiler_params=_params(("arbitrary", "arbitrary"), blocks),
        name="in_proj",
    )(x, g.reshape(1, k), w_main, w_if, colscale)


def _log_sigmoid(x):
    return jnp.minimum(x, 0.0) - jnp.log1p(jnp.exp(-jnp.abs(x)))


def _exact_tril_matmul(tril, x):
    tril = tril.astype(BF16)
    hi = x.astype(BF16)
    rest = x - hi.astype(F32)
    mid = rest.astype(BF16)
    lo = (rest - mid.astype(F32)).astype(BF16)
    return (jnp.dot(tril, hi, preferred_element_type=F32) + jnp.dot(tril, mid, preferred_element_type=F32)
            + jnp.dot(tril, lo, preferred_element_type=F32))


def _mlstm_kernel(q_ref, k_ref, v_ref, o_ref, zif_ref, gb_ref, ng_ref, y_ref, c_sc, n_sc, m_sc):
    nbatch, chunk, _ = q_ref.shape
    heads = MLSTM_HEADS

    @pl.when(pl.program_id(0) == 0)
    def _():
        c_sc[...] = jnp.zeros_like(c_sc)
        n_sc[...] = jnp.zeros_like(n_sc)
        m_sc[...] = jnp.zeros_like(m_sc)

    row = lax.broadcasted_iota(jnp.int32, (chunk, chunk), 0)
    col = lax.broadcasted_iota(jnp.int32, (chunk, chunk), 1)
    causal_t = row <= col
    tril = jnp.where(col <= row, 1.0, 0.0).astype(F32)
    lane = lax.broadcasted_iota(jnp.int32, (chunk, GATE_LANES), 1)

    for b in range(nbatch):
        pre = zif_ref[b] + gb_ref[...]
        gates = jnp.where(lane < heads, pre, _log_sigmoid(pre))
        gcum = _exact_tril_matmul(tril, gates)
        gates_t = gates.T
        gcum_t = gcum.T
        for h in range(heads):
            s = b * heads + h
            sl = slice(h * HEAD_DIM, (h + 1) * HEAD_DIM)
            q = q_ref[b, :, sl]
            k = k_ref[b, :, sl]
            v = v_ref[b, :, sl]
            q_t = q.T
            qb = q.astype(BF16)
            kb = k.astype(BF16)
            q_tb = q_t.astype(BF16)
            v_t = v.T
            g_row = gcum_t[heads + h:heads + h + 1, :]
            i_row = gates_t[h:h + 1, :]
            ig_col = gates[:, h:h + 1] - gcum[:, heads + h:heads + h + 1]
            m_prev = m_sc[s][:, 0:1]
            c_prev = c_sc[s]
            n_prev = n_sc[s]

            log_w = jnp.where(causal_t, g_row + ig_col, -jnp.inf)
            log_a = g_row + m_prev
            m_row = jnp.maximum(jnp.max(log_w, axis=0, keepdims=True), log_a)
            qk = lax.dot_general(kb, qb, NT_DIMS, preferred_element_type=F32) * jnp.exp(log_w - m_row)
            a = jnp.exp(log_a - m_row)
            num = (jnp.dot(v_t.astype(BF16), qk.astype(BF16), preferred_element_type=F32)
                   + a * jnp.dot(c_prev.astype(BF16), q_tb, preferred_element_type=F32))
            n_dot_q = jnp.dot(n_prev.astype(BF16), q_tb, preferred_element_type=F32)[0:1]
            den = jnp.sum(qk, axis=0, keepdims=True) + a * n_dot_q
            h_out = num / jnp.maximum(jnp.abs(den), jnp.exp(-m_row))

            g_last = g_row[:, chunk - 1:chunk]
            log_u = g_last - g_row + i_row
            m_new = jnp.maximum(g_last + m_prev, jnp.max(log_u, axis=1, keepdims=True))
            decay = jnp.exp(g_last + m_prev - m_new)
            u = jnp.exp(log_u - m_new)
            c_sc[s] = decay * c_prev + jnp.dot((v_t * u).astype(BF16), kb, preferred_element_type=F32)
            u_rows = jnp.broadcast_to(u, (STATE_ROWS, chunk)).astype(BF16)
            n_sc[s] = decay * n_prev + jnp.dot(u_rows, kb, preferred_element_type=F32)
            m_sc[s] = jnp.broadcast_to(m_new, (1, LANES))

            yn = (h_out * lax.rsqrt(jnp.mean(h_out * h_out, axis=0, keepdims=True) + NORM_EPS)).T
            y_ref[b, :, sl] = (jax.nn.sigmoid(o_ref[b, :, sl]) * (yn * ng_ref[:, sl])).astype(y_ref.dtype)


def _mlstm(z3, zif3, gate_b, norm_g):
    nbatch, seq, _ = z3.shape
    chunk = MLSTM_KERNEL_CHUNK
    w = MLSTM_WIDTH
    streams = nbatch * MLSTM_HEADS
    gb = jnp.pad(gate_b, (0, GATE_LANES - gate_b.shape[0])).reshape(1, GATE_LANES)
    blocks = [((nbatch, chunk, w), F32)] * 4 + [((nbatch, chunk, GATE_LANES), F32),
                                               ((nbatch, chunk, w), BF16)]
    scratch = [((streams, HEAD_DIM, HEAD_DIM), F32), ((streams, STATE_ROWS, LANES), F32),
               ((streams, 1, LANES), F32)]

    def zcol(cb):
        return pl.BlockSpec((nbatch, chunk, w), lambda c: (0, c, cb))

    return pl.pallas_call(
        _mlstm_kernel,
        grid=(_tiles(seq, chunk),),
        in_specs=[zcol(0), zcol(1), zcol(2), zcol(3),
                  pl.BlockSpec((nbatch, chunk, GATE_LANES), lambda c: (0, c, 0)),
                  pl.BlockSpec((1, GATE_LANES), lambda c: (0, 0)),
                  pl.BlockSpec((1, w), lambda c: (0, 0))],
        out_specs=pl.BlockSpec((nbatch, chunk, w), lambda c: (0, c, 0)),
        out_shape=jax.ShapeDtypeStruct((nbatch, seq, w), BF16),
        scratch_shapes=[pltpu.VMEM(s, d) for s, d in scratch],
        compiler_params=_params(("arbitrary",), blocks, scratch),
        name="mlstm",
    )(z3, z3, z3, z3, zif3, gb, norm_g.reshape(1, w))


def _rope_tables(seq):
    half = ROPE_DIM // 2
    inv_freq = ROPE_THETA ** (-jnp.arange(0, ROPE_DIM, 2, dtype=F32) / ROPE_DIM)
    ang = jnp.arange(seq, dtype=F32)[:, None] * inv_freq[None, :]
    cos = jnp.cos(ang)
    sin = jnp.sin(ang)
    ones = jnp.ones((seq, HEAD_DIM - ROPE_DIM), F32)
    cos_tab = jnp.concatenate([cos, cos, ones], axis=1)
    sin_tab = jnp.concatenate([-sin, sin, 0.0 * ones], axis=1)
    assert cos_tab.shape == (seq, HEAD_DIM) and half * 2 == ROPE_DIM
    return cos_tab, sin_tab


def _rotary(t, cos, sin):
    half = ROPE_DIM // 2
    lane = lax.broadcasted_iota(jnp.int32, t.shape, 1)
    upper = pltpu.roll(t, HEAD_DIM - half, axis=1)
    lower = pltpu.roll(t, half, axis=1)
    partner = jnp.where(lane < half, upper, lower)
    return jnp.where(lane < ROPE_DIM, t * cos + partner * sin, t)


def _rope_tables_t(seq):
    inv_freq = ROPE_THETA ** (-jnp.arange(0, ROPE_DIM, 2, dtype=F32) / ROPE_DIM)
    ang = jnp.arange(seq, dtype=F32)[:, None] * inv_freq[None, :]
    cos = jnp.cos(ang).T
    sin = jnp.sin(ang).T
    return jnp.concatenate([cos, cos], axis=0), jnp.concatenate([-sin, sin], axis=0)


def _moba_prep_t_kernel(q_ref, k_ref, v_ref, cos_ref, sin_ref, cost_ref, sint_ref, qa_ref, kb_ref, vt_ref, km_sc):
    blk = q_ref.shape[1]
    d = HEAD_DIM
    half = ROPE_DIM // 2
    nsel = km_sc.shape[1]
    j = pl.program_id(1)

    @pl.when(j == 0)
    def _():
        km_sc[...] = jnp.zeros_like(km_sc)

    cos = cos_ref[...]
    sin = sin_ref[...]
    cos_t = cost_ref[...]
    sin_t = sint_ref[...]
    blk_id = lax.broadcasted_iota(jnp.int32, (nsel, blk), 0)
    blk_id_f = blk_id.astype(F32)
    mean_row = lax.broadcasted_iota(jnp.int32, (nsel, d), 0)
    for h in range(MOBA_HEADS):
        sl = slice(h * d, (h + 1) * d)
        q_t = q_ref[0, :, sl].T
        top = q_t[:ROPE_DIM]
        partner = jnp.concatenate([top[half:], top[:half]], axis=0)
        q_t = jnp.concatenate([top * cos_t + partner * sin_t, q_t[ROPE_DIM:]], axis=0)

        gate = jnp.dot(km_sc[h], q_t, preferred_element_type=F32,
                       precision=lax.Precision.HIGHEST)
        gate = jnp.where(blk_id < j, gate, -jnp.inf)
        sel_m1 = jnp.full((nsel, blk), -1.0, F32)
        for _ in range(MOBA_TOPK):
            mx = jnp.max(gate, axis=0, keepdims=True)
            first = jnp.min(jnp.where(gate == mx, blk_id_f, float(nsel)), axis=0, keepdims=True)
            first = jnp.where(mx > -jnp.inf, first, -1.0)
            hit = blk_id_f == first
            sel_m1 = jnp.where(hit, 0.0, sel_m1)
            gate = jnp.where(hit, -jnp.inf, gate)
        qa_ref[0, h, 0, 0:d, :] = q_t.astype(BF16)
        qa_ref[0, h, 0, d:d + nsel, :] = sel_m1.astype(BF16)
        qa_ref[0, h, 0, d + nsel:, :] = jnp.full((d - nsel, blk), -1.0, BF16)

        kk = _rotary(k_ref[0, :, sl], cos, sin)
        kb_ref[0, :, sl] = kk.astype(BF16)
        km_sc[h] = jnp.where(mean_row == j, jnp.mean(kk, axis=0, keepdims=True), km_sc[h])

        vt_ref[0, h, 0] = v_ref[0, :, sl].T.astype(BF16)


def _moba_prep_t(z3, tables, q_cb, k_cb, v_cb):
    nbatch, seq, _ = z3.shape
    blk = MOBA_BLOCK
    nblk = _tiles(seq, blk)
    nsel = -(-nblk // 16) * 16
    assert nsel <= HEAD_DIM
    w = MOBA_WIDTH
    d = HEAD_DIM
    hh = MOBA_HEADS
    cos_tab, sin_tab, cos_t, sin_t = tables
    blocks = ([((1, blk, w), F32)] * 3 + [((blk, d), F32)] * 2 + [((ROPE_DIM, blk), F32)] * 2
              + [((hh, 2 * d, blk), BF16), ((1, blk, w), BF16), ((hh, d, blk), BF16)])
    scratch = [((hh, nsel, d), F32)]
    return pl.pallas_call(
        _moba_prep_t_kernel,
        grid=(nbatch, nblk),
        in_specs=[pl.BlockSpec((1, blk, w), lambda b, j: (b, j, q_cb)),
                  pl.BlockSpec((1, blk, w), lambda b, j: (b, j, k_cb)),
                  pl.BlockSpec((1, blk, w), lambda b, j: (b, j, v_cb)),
                  pl.BlockSpec((blk, d), lambda b, j: (j, 0)),
                  pl.BlockSpec((blk, d), lambda b, j: (j, 0)),
                  pl.BlockSpec((ROPE_DIM, blk), lambda b, j: (0, j)),
                  pl.BlockSpec((ROPE_DIM, blk), lambda b, j: (0, j))],
        out_specs=[pl.BlockSpec((1, hh, 1, 2 * d, blk), lambda b, j: (b, 0, j, 0, 0)),
                   pl.BlockSpec((1, blk, w), lambda b, j: (b, j, 0)),
                   pl.BlockSpec((1, hh, 1, d, blk), lambda b, j: (b, 0, j, 0, 0))],
        out_shape=[jax.ShapeDtypeStruct((nbatch, hh, nblk, 2 * d, blk), BF16),
                   jax.ShapeDtypeStruct((nbatch, seq, w), BF16),
                   jax.ShapeDtypeStruct((nbatch, hh, nblk, d, blk), BF16)],
        scratch_shapes=[pltpu.VMEM(s, dt) for s, dt in scratch],
        compiler_params=_params(("arbitrary", "arbitrary"), blocks, scratch),
        name="moba_prep",
    )(z3, z3, z3, cos_tab, sin_tab, cos_t, sin_t)


def _moba_mask_columns(seq):
    blk_id = jnp.arange(seq, dtype=jnp.int32)[:, None] // MOBA_BLOCK
    return jnp.where(jnp.arange(LANES, dtype=jnp.int32)[None, :] == blk_id, MASK_BIG, 0.0).astype(BF16)


def _pipelined(start, count, first, step):
    lo = start % 2

    def pair(t, carry):
        i = start + 2 * t
        return step(i + 1, 1 - lo, step(i, lo, carry))

    carry = lax.fori_loop(0, (count - start) // 2, pair, first)
    if (count - start) % 2:
        carry = step(count - 1, (count - 1) % 2, carry)
    return carry


def _moba_flat_kernel(tile_ref, group_ref, q_ref, k_ref, mask_ref, vt_ref, o_ref,
                      s_sc, p_sc, acc_sc, m_sc, l_sc):
    nblk, d, blk = vt_ref.shape[2:]
    group = MOBA_GROUP
    gkeys = group * blk
    n_items = tile_ref.shape[0]
    c_exp = (d ** -0.5) * np.log2(np.e).astype(np.float32)
    key = lax.broadcasted_iota(jnp.int32, (blk, blk), 0)
    qry = lax.broadcasted_iota(jnp.int32, (blk, blk), 1)

    def weighted_values(first_blk, p):
        acc = None
        for i in range(p.shape[0] // blk):
            part = jnp.dot(vt_ref[0, 0, first_blk + i], p[i * blk:(i + 1) * blk],
                           preferred_element_type=F32)
            acc = part if acc is None else acc + part
        return acc

    def own_scores(j, slot):
        r0 = pl.multiple_of(j * blk, blk)
        s_t = jnp.dot(k_ref[0, pl.ds(r0, blk), :], q_ref[0, 0, j, :d, :], preferred_element_type=F32)
        s_t = jnp.where(key <= qry, s_t, -jnp.inf)
        s_sc[slot, :blk, :] = s_t
        return jnp.max(s_t, axis=0, keepdims=True)

    def own_softmax(j, slot, s_max):
        next_max = own_scores(jnp.minimum(j + 1, nblk - 1), 1 - slot)
        p = jnp.exp2((s_sc[slot, :blk, :] - s_max) * c_exp)
        m_sc[j] = s_max
        l_sc[j] = jnp.sum(p, axis=0, keepdims=True)
        p_sc[slot] = p.astype(BF16)
        return next_max

    def own_values(j, slot):
        acc_sc[j] = weighted_values(j, p_sc[slot])

    def own_step(j, slot, s_max):
        own_values(j - 1, 1 - slot)
        return own_softmax(j, slot, s_max)

    _pipelined(1, nblk, own_softmax(0, 0, own_scores(0, 0)), own_step)
    own_values(nblk - 1, (nblk - 1) % 2)

    def group_scores(i, slot):
        c0 = pl.multiple_of(group_ref[i] * gkeys, gkeys)
        k_aug = jnp.concatenate([k_ref[0, pl.ds(c0, gkeys), :], mask_ref[pl.ds(c0, gkeys), :]], axis=1)
        s_t = jnp.dot(k_aug, q_ref[0, 0, tile_ref[i]], preferred_element_type=F32)
        s_sc[slot] = s_t
        return jnp.max(s_t, axis=0, keepdims=True)

    def group_step(i, slot, s_max):
        next_max = group_scores(jnp.minimum(i + 1, n_items - 1), 1 - slot)
        j = tile_ref[i]
        m_run = m_sc[j]
        m_new = jnp.maximum(m_run, s_max)
        alpha = jnp.exp2((m_run - m_new) * c_exp)
        p = jnp.exp2((s_sc[slot] - m_new) * c_exp)
        m_sc[j] = m_new
        l_sc[j] = alpha * l_sc[j] + jnp.sum(p, axis=0, keepdims=True)
        acc_sc[j] = alpha * acc_sc[j] + weighted_values(group_ref[i] * group, p.astype(BF16))
        return next_max

    _pipelined(0, n_items, group_scores(0, 0), group_step)

    def finish(j, carry):
        r0 = pl.multiple_of(j * blk, blk)
        o_ref[0, pl.ds(r0, blk), :] = (acc_sc[j] / l_sc[j]).T.astype(o_ref.dtype)
        return carry

    lax.fori_loop(0, nblk, finish, 0)


def _moba_flat(q_aug, kb, vt, mask_cols):
    nbatch, seq, w = kb.shape
    blk = MOBA_BLOCK
    d = HEAD_DIM
    nblk = vt.shape[2]
    group = MOBA_GROUP
    items = [(j, g) for j in range(nblk) for g in range(-(-j // group))]
    assert items and _tiles(nblk, group)
    item_tile = jnp.asarray([j for j, _ in items], jnp.int32)
    item_group = jnp.asarray([g for _, g in items], jnp.int32)
    blocks = [((nblk, 2 * d, blk), BF16), ((1, seq, d), BF16), ((seq, LANES), BF16), ((nblk, d, blk), BF16),
              ((1, seq, d), BF16)]
    scratch = [((2, group * blk, blk), F32), ((2, blk, blk), BF16), ((nblk, d, blk), F32),
               ((nblk, 1, blk), F32), ((nblk, 1, blk), F32)]
    grid_spec = pltpu.PrefetchScalarGridSpec(
        num_scalar_prefetch=2,
        grid=(nbatch, MOBA_HEADS),
        in_specs=[pl.BlockSpec((1, 1, nblk, 2 * d, blk), lambda b, h, *_: (b, h, 0, 0, 0)),
                  pl.BlockSpec((1, seq, d), lambda b, h, *_: (b, 0, h)),
                  pl.BlockSpec((seq, LANES), lambda b, h, *_: (0, 0)),
                  pl.BlockSpec((1, 1, nblk, d, blk), lambda b, h, *_: (b, h, 0, 0, 0))],
        out_specs=pl.BlockSpec((1, seq, d), lambda b, h, *_: (b, 0, h)),
        scratch_shapes=[pltpu.VMEM(s, dt) for s, dt in scratch])
    return pl.pallas_call(
        _moba_flat_kernel,
        grid_spec=grid_spec,
        out_shape=jax.ShapeDtypeStruct((nbatch, seq, w), BF16),
        compiler_params=_params(("arbitrary", "arbitrary"), blocks, scratch),
        name="moba_attn",
    )(item_tile, item_group, q_aug, kb, mask_cols, vt)


def _gelu_tanh(x):
    c = np.sqrt(2.0 / np.pi).astype(np.float32)
    return x * (0.5 * (1.0 + jnp.tanh(c * (x + 0.044715 * (x * x * x)))))


def _gmlp_kernel(u_ref, v_ref, lg_ref, lb_ref, ws_ref, bst_ref, y_ref):
    rows = u_ref.shape[1]
    t = GMLP_CHUNK
    gd = GMLP_WIDTH // GMLP_GROUPS
    v = _gelu_tanh(v_ref[0])
    mu = jnp.mean(v, axis=-1, keepdims=True)
    vc = v - mu
    vln = vc * lax.rsqrt(jnp.mean(vc * vc, axis=-1, keepdims=True) + NORM_EPS) * lg_ref[...] + lb_ref[...]
    vb = vln.astype(BF16)
    row = lax.broadcasted_iota(jnp.int32, (t, t), 0)
    col = lax.broadcasted_iota(jnp.int32, (t, t), 1)
    for g in range(GMLP_GROUPS):
        wg = jnp.where(col <= row, ws_ref[g], 0.0).astype(BF16)
        bias = bst_ref[:, g:g + 1]
        cols = slice(g * gd, (g + 1) * gd)
        for c in range(rows // t):
            rs = slice(c * t, (c + 1) * t)
            mixed = jnp.dot(wg, vb[rs, cols], preferred_element_type=F32) + bias
            y_ref[0, rs, cols] = (_gelu_tanh(u_ref[0, rs, cols]) * mixed).astype(y_ref.dtype)


def _gmlp(z3, u_cb, v_cb, ln_g, ln_b, ws, bs, rows=512):
    nbatch, seq, _ = z3.shape
    w = GMLP_WIDTH
    t = GMLP_CHUNK
    blocks = [((1, rows, w), F32), ((1, rows, w), F32), ((GMLP_GROUPS, t, t), F32), ((1, rows, w), BF16)]
    return pl.pallas_call(
        _gmlp_kernel,
        grid=(nbatch, _tiles(seq, rows)),
        in_specs=[pl.BlockSpec((1, rows, w), lambda b, c: (b, c, u_cb)),
                  pl.BlockSpec((1, rows, w), lambda b, c: (b, c, v_cb)),
                  pl.BlockSpec((1, w), lambda b, c: (0, 0)),
                  pl.BlockSpec((1, w), lambda b, c: (0, 0)),
                  pl.BlockSpec((GMLP_GROUPS, t, t), lambda b, c: (0, 0, 0)),
                  pl.BlockSpec((t, GMLP_GROUPS), lambda b, c: (0, 0))],
        out_specs=pl.BlockSpec((1, rows, w), lambda b, c: (b, c, 0)),
        out_shape=jax.ShapeDtypeStruct((nbatch, seq, w), BF16),
        compiler_params=_params(("arbitrary", "arbitrary"), blocks),
        name="gmlp",
    )(z3, z3, ln_g.reshape(1, w), ln_b.reshape(1, w), ws, bs.T)


def _layer_weight_spec(layer, k, tn, index_of):
    return pl.BlockSpec((None, k, tn), lambda *idx: (layer,) + index_of(*idx))


def _merge_kernel(h_ref, wga_ref, wgb_ref, wgc_ref, ya_ref, yb_ref, yc_ref, wa_ref, wb_ref, wc_ref, o_ref):
    h = h_ref[...]

    def branch(wg_ref, y_ref, w_ref):
        gate = jax.nn.sigmoid(jnp.dot(h, wg_ref[...], preferred_element_type=F32))
        return gate * jnp.dot(y_ref[...], w_ref[...].astype(BF16), preferred_element_type=F32)

    merged = branch(wga_ref, ya_ref, wa_ref) + branch(wgb_ref, yb_ref, wb_ref) + branch(wgc_ref, yc_ref, wc_ref)
    o_ref[...] = merged.astype(o_ref.dtype)


def _merge(h, w_gate, ya, yb, yc, wa, wb, wc, layer, tm=1024, tn=512):
    m, d = h.shape
    nblk = _tiles(d, tn)
    ka, kb, kc = ya.shape[1], yb.shape[1], yc.shape[1]
    blocks = ([((tm, d), BF16)] + [((d, tn), BF16)] * 3
              + [((tm, ka), BF16), ((tm, kb), BF16), ((tm, kc), BF16)]
              + [((ka, tn), F32), ((kb, tn), F32), ((kc, tn), F32), ((tm, tn), BF16)])

    def gate_spec(branch):
        return pl.BlockSpec((None, d, tn), lambda i, j: (layer, 0, branch * nblk + j))

    def col(i, j):
        return (0, j)

    return pl.pallas_call(
        _merge_kernel,
        grid=(_tiles(m, tm), nblk),
        in_specs=[pl.BlockSpec((tm, d), lambda i, j: (i, 0)),
                  gate_spec(0), gate_spec(1), gate_spec(2),
                  pl.BlockSpec((tm, ka), lambda i, j: (i, 0)),
                  pl.BlockSpec((tm, kb), lambda i, j: (i, 0)),
                  pl.BlockSpec((tm, kc), lambda i, j: (i, 0)),
                  _layer_weight_spec(layer, ka, tn, col),
                  _layer_weight_spec(layer, kb, tn, col),
                  _layer_weight_spec(layer, kc, tn, col)],
        out_specs=pl.BlockSpec((tm, tn), lambda i, j: (i, j)),
        out_shape=jax.ShapeDtypeStruct((m, d), BF16),
        compiler_params=_params(("arbitrary", "arbitrary"), blocks),
        name="merge",
    )(h, w_gate, w_gate, w_gate, ya, yb, yc, wa, wb, wc)


def _residual_matmul_kernel(a_ref, w_ref, r_ref, o_ref):
    @pl.when(pl.program_id(2) == 0)
    def _():
        o_ref[...] = r_ref[...]

    o_ref[...] += jnp.dot(a_ref[...], w_ref[...].astype(BF16), preferred_element_type=F32)


def _residual_matmul(a, w, layer, res, tm=1024, tn=1024, tk=1024):
    m, k = a.shape
    n = w.shape[2]
    blocks = [((tm, tk), BF16), ((tk, tn), F32), ((tm, tn), F32), ((tm, tn), F32)]
    return pl.pallas_call(
        _residual_matmul_kernel,
        grid=(_tiles(m, tm), _tiles(n, tn), _tiles(k, tk)),
        in_specs=[pl.BlockSpec((tm, tk), lambda i, j, kk: (i, kk)),
                  _layer_weight_spec(layer, tk, tn, lambda i, j, kk: (kk, j)),
                  pl.BlockSpec((tm, tn), lambda i, j, kk: (i, j))],
        out_specs=pl.BlockSpec((tm, tn), lambda i, j, kk: (i, j)),
        out_shape=jax.ShapeDtypeStruct((m, n), F32),
        compiler_params=_params(("arbitrary", "arbitrary", "arbitrary"), blocks),
        name="residual_matmul",
    )(a, w, res)


def _relu2_matmul_kernel(x_ref, g_ref, w_ref, o_ref, h_sc):
    @pl.when(pl.program_id(1) == 0)
    def _():
        _rmsnorm_rows(x_ref, g_ref, h_sc)

    up = jnp.maximum(jnp.dot(h_sc[...], w_ref[...].astype(BF16), preferred_element_type=F32), 0.0)
    o_ref[...] = (up * up).astype(o_ref.dtype)


def _relu2_matmul(x, g, w, layer, tm=1024, tn=1024):
    m, k = x.shape
    n = w.shape[2]
    blocks = [((tm, k), F32), ((k, tn), F32), ((tm, tn), BF16)]
    scratch = [((tm, k), BF16)]
    return pl.pallas_call(
        _relu2_matmul_kernel,
        grid=(_tiles(m, tm), _tiles(n, tn)),
        in_specs=[pl.BlockSpec((tm, k), lambda i, j: (i, 0)),
                  pl.BlockSpec((1, k), lambda i, j: (0, 0)),
                  _layer_weight_spec(layer, k, tn, lambda i, j: (0, j))],
        out_specs=pl.BlockSpec((tm, tn), lambda i, j: (i, j)),
        out_shape=jax.ShapeDtypeStruct((m, n), BF16),
        scratch_shapes=[pltpu.VMEM(s, dt) for s, dt in scratch],
        compiler_params=_params(("arbitrary", "arbitrary"), blocks, scratch),
        name="mlp_up",
    )(x, g.reshape(1, k), w)


def _ple_kernel(x_ref, g_ref, wg_ref, p_ref, wp_ref, o_ref, h_sc):
    j = pl.program_id(1)

    @pl.when(j == 0)
    def _():
        _rmsnorm_rows(x_ref, g_ref, h_sc)

    tn = o_ref.shape[1]
    gate = jax.nn.sigmoid(jnp.dot(h_sc[...], wg_ref[...].astype(BF16), preferred_element_type=F32))
    emb = jnp.dot(p_ref[...].astype(BF16), wp_ref[...].astype(BF16), preferred_element_type=F32)
    o_ref[...] = x_ref[:, pl.ds(pl.multiple_of(j * tn, tn), tn)] + gate * emb


def _ple(x, g, w_gate, p, w_proj, layer, tm=1024, tn=512):
    m, d = x.shape
    pd = p.shape[2]
    blocks = [((tm, d), F32), ((d, tn), F32), ((tm, pd), F32), ((pd, tn), F32), ((tm, tn), F32)]
    scratch = [((tm, d), BF16)]
    return pl.pallas_call(
        _ple_kernel,
        grid=(_tiles(m, tm), _tiles(d, tn)),
        in_specs=[pl.BlockSpec((tm, d), lambda i, j: (i, 0)),
                  pl.BlockSpec((1, d), lambda i, j: (0, 0)),
                  _layer_weight_spec(layer, d, tn, lambda i, j: (0, j)),
                  pl.BlockSpec((None, tm, pd), lambda i, j: (layer, i, 0)),
                  _layer_weight_spec(layer, pd, tn, lambda i, j: (0, j))],
        out_specs=pl.BlockSpec((tm, tn), lambda i, j: (i, j)),
        out_shape=jax.ShapeDtypeStruct((m, d), F32),
        scratch_shapes=[pltpu.VMEM(s, dt) for s, dt in scratch],
        compiler_params=_params(("arbitrary", "arbitrary"), blocks, scratch),
        name="ple",
    )(x, g.reshape(1, d), w_gate, p, w_proj)


def kernel(x, p, norm_mix_g, w_in, mlstm_gate_b, mlstm_norm_g, gmlp_norm_g, gmlp_norm_b, gmlp_ws, gmlp_bs,
           w_branch_a, w_branch_b, w_branch_c, w_out, norm_mlp_g, w_mlp_up, w_mlp_down, norm_ple_g,
           w_ple_gate, w_ple_proj, final_norm_g):
    nbatch, seq, d = x.shape
    depth = w_in.shape[0]
    m = nbatch * seq
    assert d == MLSTM_WIDTH + MOBA_WIDTH + GMLP_WIDTH

    qkvo_a = 4 * MLSTM_WIDTH
    gates_if = 2 * MLSTM_HEADS
    main_b = qkvo_a + gates_if
    main_cols = 3 * MOBA_WIDTH + 2 * GMLP_WIDTH
    gate_off = main_b + main_cols
    assert w_in.shape[2] == gate_off + N_BRANCHES * d

    moba_q_cb = qkvo_a // MOBA_WIDTH
    moba_k_cb = moba_q_cb + 1
    moba_v_cb = moba_k_cb + 1
    gmlp_u_cb = (qkvo_a + 3 * MOBA_WIDTH) // GMLP_WIDTH
    gmlp_v_cb = gmlp_u_cb + 1

    z_cols = qkvo_a + main_cols
    colscale = jnp.ones((1, z_cols), F32).at[:, MLSTM_WIDTH:2 * MLSTM_WIDTH].set(HEAD_DIM ** -0.5)
    rope_tables = _rope_tables(seq) + _rope_tables_t(seq)
    mask_cols = _moba_mask_columns(seq)

    xf = x.reshape(m, d)
    p_flat = p.reshape(depth, m, p.shape[-1])
    w_main, w_if, w_gate = _win_split(w_in, qkvo_a, gates_if, z_cols)
    for i in range(depth):
        z, zif, h = _inproj(xf, norm_mix_g[i], w_main, w_if, i, colscale)
        z3 = z.reshape(nbatch, seq, z_cols)
        zif3 = zif.reshape(nbatch, seq, GATE_LANES)

        ya = _mlstm(z3, zif3, mlstm_gate_b[i], mlstm_norm_g[i])
        yb = _moba_flat(*_moba_prep_t(z3, rope_tables, moba_q_cb, moba_k_cb, moba_v_cb), mask_cols)
        yc = _gmlp(z3, gmlp_u_cb, gmlp_v_cb, gmlp_norm_g[i], gmlp_norm_b[i], gmlp_ws[i], gmlp_bs[i])

        merged = _merge(h, w_gate, ya.reshape(m, -1), yb.reshape(m, -1), yc.reshape(m, -1),
                        w_branch_a, w_branch_b, w_branch_c, i)
        xf = _residual_matmul(merged, w_out, i, xf, tm=2048)

        hidden = _relu2_matmul(xf, norm_mlp_g[i], w_mlp_up, i)
        xf = _residual_matmul(hidden, w_mlp_down, i, xf, tm=2048)
        xf = _ple(xf, norm_ple_g[i], w_ple_gate, p_flat, w_ple_proj, i, tn=1024)

    return _rmsnorm(xf, final_norm_g, F32).reshape(nbatch, seq, d)
```

```python
import jax
import jax.numpy as jnp
import numpy as np
from jax import lax
from jax.experimental import pallas as pl
from jax.experimental.pallas import tpu as pltpu

F32 = jnp.float32
BF16 = jnp.bfloat16

HEAD_DIM = 128
MLSTM_HEADS = 4
MLSTM_WIDTH = MLSTM_HEADS * HEAD_DIM
MOBA_HEADS = 8
MOBA_WIDTH = MOBA_HEADS * HEAD_DIM
MOBA_BLOCK = 256
MOBA_TOPK = 3
ROPE_THETA = 500000.0
ROPE_DIM = HEAD_DIM // 4
GMLP_WIDTH = 512
GMLP_GROUPS = 4
GMLP_CHUNK = 128
N_BRANCHES = 3
NORM_EPS = 1e-6

LANES = 128
V7X_VMEM_BYTES = 64 * 1024 * 1024
VMEM_CEILING = V7X_VMEM_BYTES - 8 * 1024 * 1024

MLSTM_KERNEL_CHUNK = 128
STATE_ROWS = 16
GATE_LANES = LANES
MASK_BIG = 2.0 ** 100
BIAS_ROWS = 8
MOBA_GROUP = 4

NT_DIMS = (((1,), (1,)), ((), ()))


def _tiles(n, t):
    count, rest = divmod(n, t)
    assert rest == 0 and count > 0, (n, t)
    return count


def _nbytes(shape, dtype):
    return int(np.prod(shape)) * jnp.dtype(dtype).itemsize


def _params(semantics, blocks, scratch=()):
    need = 2 * sum(_nbytes(s, d) for s, d in blocks) + sum(_nbytes(s, d) for s, d in scratch)
    limit = min(VMEM_CEILING, need + need // 4 + 4 * 1024 * 1024)
    return pltpu.CompilerParams(dimension_semantics=semantics, vmem_limit_bytes=limit)


def _rmsnorm_kernel(x_ref, g_ref, o_ref):
    x = x_ref[...]
    y = x * lax.rsqrt(jnp.mean(x * x, axis=-1, keepdims=True) + NORM_EPS)
    o_ref[...] = (y * g_ref[...]).astype(o_ref.dtype)


def _rmsnorm(x, g, out_dtype, tm=512):
    m, d = x.shape
    return pl.pallas_call(
        _rmsnorm_kernel,
        grid=(_tiles(m, tm),),
        in_specs=[pl.BlockSpec((tm, d), lambda i: (i, 0)),
                  pl.BlockSpec((1, d), lambda i: (0, 0))],
        out_specs=pl.BlockSpec((tm, d), lambda i: (i, 0)),
        out_shape=jax.ShapeDtypeStruct((m, d), out_dtype),
        compiler_params=_params(("arbitrary",), [((tm, d), F32), ((tm, d), out_dtype)]),
        name="rmsnorm",
    )(x, g.reshape(1, d))


def _win_split_kernel(wm_ref, wif_ref, wg_ref, main_ref, if_ref, gate_ref):
    main_ref[...] = wm_ref[0].T.astype(BF16)
    gate_ref[...] = wg_ref[0].T.astype(BF16)
    _, n_if, k = wif_ref.shape
    padded = jnp.concatenate([wif_ref[0], jnp.zeros((GATE_LANES - n_if, k), F32)], axis=0)
    if_ref[...] = padded.T.astype(BF16)


def _win_split(w_in, head, n_if, main_cols, tn=512):
    depth, k, cols = w_in.shape
    gate_cols = cols - main_cols - n_if
    assert gate_cols == main_cols and head % tn == 0
    w_t = jnp.swapaxes(w_in, 1, 2)
    head_steps = head // tn
    blocks = [((tn, k), F32), ((n_if, k), F32), ((tn, k), F32), ((k, tn), BF16), ((k, GATE_LANES), BF16),
              ((k, tn), BF16)]

    def main_rows(l, r):
        return (l, pl.multiple_of(r * tn + jnp.where(r >= head_steps, n_if, 0), n_if), 0)

    def window(rows, start_of):
        return pl.BlockSpec((pl.Element(1), pl.Element(rows), pl.Element(k)), start_of)

    return pl.pallas_call(
        _win_split_kernel,
        grid=(depth, _tiles(main_cols, tn)),
        in_specs=[window(tn, main_rows),
                  window(n_if, lambda l, r: (l, head, 0)),
                  window(tn, lambda l, r: (l, pl.multiple_of(main_cols + n_if + r * tn, n_if), 0))],
        out_specs=[pl.BlockSpec((None, k, tn), lambda l, r: (l, 0, r)),
                   pl.BlockSpec((None, k, GATE_LANES), lambda l, r: (l, 0, 0)),
                   pl.BlockSpec((None, k, tn), lambda l, r: (l, 0, r))],
        out_shape=[jax.ShapeDtypeStruct((depth, k, main_cols), BF16),
                   jax.ShapeDtypeStruct((depth, k, GATE_LANES), BF16),
                   jax.ShapeDtypeStruct((depth, k, gate_cols), BF16)],
        compiler_params=_params(("arbitrary", "arbitrary"), blocks),
        name="win_split",
    )(w_t, w_t, w_t)


def _rmsnorm_rows(x_ref, g_ref, h_ref, rows=256):
    for r0 in range(0, x_ref.shape[0], rows):
        x = x_ref[r0:r0 + rows, :]
        y = x * lax.rsqrt(jnp.mean(x * x, axis=-1, keepdims=True) + NORM_EPS)
        h_ref[r0:r0 + rows, :] = (y * g_ref[...]).astype(h_ref.dtype)


def _inproj_kernel(x_ref, g_ref, w_ref, wif_ref, cs_ref, z_ref, zif_ref, h_ref):
    @pl.when(pl.program_id(1) == 0)
    def _():
        _rmsnorm_rows(x_ref, g_ref, h_ref)
        zif_ref[...] = jnp.dot(h_ref[...], wif_ref[...], preferred_element_type=F32)

    acc = jnp.dot(h_ref[...], w_ref[...], preferred_element_type=F32)
    z_ref[...] = acc * cs_ref[...]


def _inproj(x, g, w_main, w_if, layer, colscale, tm=1024, tn=1024):
    m, k = x.shape
    n = w_main.shape[2]
    blocks = [((tm, k), F32), ((k, tn), BF16), ((k, GATE_LANES), BF16), ((1, tn), F32),
              ((tm, tn), F32), ((tm, GATE_LANES), F32), ((tm, k), BF16)]
    return pl.pallas_call(
        _inproj_kernel,
        grid=(_tiles(m, tm), _tiles(n, tn)),
        in_specs=[pl.BlockSpec((tm, k), lambda i, j: (i, 0)),
                  pl.BlockSpec((1, k), lambda i, j: (0, 0)),
                  pl.BlockSpec((None, k, tn), lambda i, j: (layer, 0, j)),
                  pl.BlockSpec((None, k, GATE_LANES), lambda i, j: (layer, 0, 0)),
                  pl.BlockSpec((1, tn), lambda i, j: (0, j))],
        out_specs=[pl.BlockSpec((tm, tn), lambda i, j: (i, j)),
                   pl.BlockSpec((tm, GATE_LANES), lambda i, j: (i, 0)),
                   pl.BlockSpec((tm, k), lambda i, j: (i, 0))],
        out_shape=[jax.ShapeDtypeStruct((m, n), F32),
                   jax.ShapeDtypeStruct((m, GATE_LANES), F32),
                   jax.ShapeDtypeStruct((m, k), BF16)],
        compiler_params=_params(("arbitrary", "arbitrary"), blocks),
        name="in_proj",
    )(x, g.reshape(1, k), w_main, w_if, colscale)


def _log_sigmoid(x):
    return jnp.minimum(x, 0.0) - jnp.log1p(jnp.exp(-jnp.abs(x)))


def _exact_tril_matmul(tril, x):
    tril = tril.astype(BF16)
    hi = x.astype(BF16)
    rest = x - hi.astype(F32)
    mid = rest.astype(BF16)
    lo = (rest - mid.astype(F32)).astype(BF16)
    return (jnp.dot(tril, hi, preferred_element_type=F32) + jnp.dot(tril, mid, preferred_element_type=F32)
            + jnp.dot(tril, lo, preferred_element_type=F32))


def _mlstm_kernel(q_ref, k_ref, v_ref, o_ref, zif_ref, gb_ref, ng_ref, y_ref, c_sc, n_sc, m_sc):
    nbatch, chunk, _ = q_ref.shape
    heads = MLSTM_HEADS

    @pl.when(pl.program_id(0) == 0)
    def _():
        c_sc[...] = jnp.zeros_like(c_sc)
        n_sc[...] = jnp.zeros_like(n_sc)
        m_sc[...] = jnp.zeros_like(m_sc)

    row = lax.broadcasted_iota(jnp.int32, (chunk, chunk), 0)
    col = lax.broadcasted_iota(jnp.int32, (chunk, chunk), 1)
    causal_t = row <= col
    tril = jnp.where(col <= row, 1.0, 0.0).astype(F32)
    lane = lax.broadcasted_iota(jnp.int32, (chunk, GATE_LANES), 1)

    for b in range(nbatch):
        pre = zif_ref[b] + gb_ref[...]
        gates = jnp.where(lane < heads, pre, _log_sigmoid(pre))
        gcum = _exact_tril_matmul(tril, gates)
        gates_t = gates.T
        gcum_t = gcum.T
        for h in range(heads):
            s = b * heads + h
            sl = slice(h * HEAD_DIM, (h + 1) * HEAD_DIM)
            q = q_ref[b, :, sl]
            k = k_ref[b, :, sl]
            v = v_ref[b, :, sl]
            q_t = q.T
            qb = q.astype(BF16)
            kb = k.astype(BF16)
            q_tb = q_t.astype(BF16)
            v_t = v.T
            g_row = gcum_t[heads + h:heads + h + 1, :]
            i_row = gates_t[h:h + 1, :]
            ig_col = gates[:, h:h + 1] - gcum[:, heads + h:heads + h + 1]
            m_prev = m_sc[s][:, 0:1]
            c_prev = c_sc[s]
            n_prev = n_sc[s]

            log_w = jnp.where(causal_t, g_row + ig_col, -jnp.inf)
            log_a = g_row + m_prev
            m_row = jnp.maximum(jnp.max(log_w, axis=0, keepdims=True), log_a)
            qk = lax.dot_general(kb, qb, NT_DIMS, preferred_element_type=F32) * jnp.exp(log_w - m_row)
            a = jnp.exp(log_a - m_row)
            num = (jnp.dot(v_t.astype(BF16), qk.astype(BF16), preferred_element_type=F32)
                   + a * jnp.dot(c_prev.astype(BF16), q_tb, preferred_element_type=F32))
            n_dot_q = jnp.dot(n_prev.astype(BF16), q_tb, preferred_element_type=F32)[0:1]
            den = jnp.sum(qk, axis=0, keepdims=True) + a * n_dot_q
            h_out = num / jnp.maximum(jnp.abs(den), jnp.exp(-m_row))

            g_last = g_row[:, chunk - 1:chunk]
            log_u = g_last - g_row + i_row
            m_new = jnp.maximum(g_last + m_prev, jnp.max(log_u, axis=1, keepdims=True))
            decay = jnp.exp(g_last + m_prev - m_new)
            u = jnp.exp(log_u - m_new)
            c_sc[s] = decay * c_prev + jnp.dot((v_t * u).astype(BF16), kb, preferred_element_type=F32)
            u_rows = jnp.broadcast_to(u, (STATE_ROWS, chunk)).astype(BF16)
            n_sc[s] = decay * n_prev + jnp.dot(u_rows, kb, preferred_element_type=F32)
            m_sc[s] = jnp.broadcast_to(m_new, (1, LANES))

            yn = (h_out * lax.rsqrt(jnp.mean(h_out * h_out, axis=0, keepdims=True) + NORM_EPS)).T
            y_ref[b, :, sl] = (jax.nn.sigmoid(o_ref[b, :, sl]) * (yn * ng_ref[:, sl])).astype(y_ref.dtype)


def _mlstm(z3, zif3, gate_b, norm_g):
    nbatch, seq, _ = z3.shape
    chunk = MLSTM_KERNEL_CHUNK
    w = MLSTM_WIDTH
    streams = nbatch * MLSTM_HEADS
    gb = jnp.pad(gate_b, (0, GATE_LANES - gate_b.shape[0])).reshape(1, GATE_LANES)
    blocks = [((nbatch, chunk, w), F32)] * 4 + [((nbatch, chunk, GATE_LANES), F32),
                                               ((nbatch, chunk, w), BF16)]
    scratch = [((streams, HEAD_DIM, HEAD_DIM), F32), ((streams, STATE_ROWS, LANES), F32),
               ((streams, 1, LANES), F32)]

    def zcol(cb):
        return pl.BlockSpec((nbatch, chunk, w), lambda c: (0, c, cb))

    return pl.pallas_call(
        _mlstm_kernel,
        grid=(_tiles(seq, chunk),),
        in_specs=[zcol(0), zcol(1), zcol(2), zcol(3),
                  pl.BlockSpec((nbatch, chunk, GATE_LANES), lambda c: (0, c, 0)),
                  pl.BlockSpec((1, GATE_LANES), lambda c: (0, 0)),
                  pl.BlockSpec((1, w), lambda c: (0, 0))],
        out_specs=pl.BlockSpec((nbatch, chunk, w), lambda c: (0, c, 0)),
        out_shape=jax.ShapeDtypeStruct((nbatch, seq, w), BF16),
        scratch_shapes=[pltpu.VMEM(s, d) for s, d in scratch],
        compiler_params=_params(("arbitrary",), blocks, scratch),
        name="mlstm",
    )(z3, z3, z3, z3, zif3, gb, norm_g.reshape(1, w))


def _rope_tables(seq):
    half = ROPE_DIM // 2
    inv_freq = ROPE_THETA ** (-jnp.arange(0, ROPE_DIM, 2, dtype=F32) / ROPE_DIM)
    ang = jnp.arange(seq, dtype=F32)[:, None] * inv_freq[None, :]
    cos = jnp.cos(ang)
    sin = jnp.sin(ang)
    ones = jnp.ones((seq, HEAD_DIM - ROPE_DIM), F32)
    cos_tab = jnp.concatenate([cos, cos, ones], axis=1)
    sin_tab = jnp.concatenate([-sin, sin, 0.0 * ones], axis=1)
    assert cos_tab.shape == (seq, HEAD_DIM) and half * 2 == ROPE_DIM
    return cos_tab, sin_tab


def _rotary(t, cos, sin):
    half = ROPE_DIM // 2
    lane = lax.broadcasted_iota(jnp.int32, t.shape, 1)
    upper = pltpu.roll(t, HEAD_DIM - half, axis=1)
    lower = pltpu.roll(t, half, axis=1)
    partner = jnp.where(lane < half, upper, lower)
    return jnp.where(lane < ROPE_DIM, t * cos + partner * sin, t)


def _rope_tables_t(seq):
    inv_freq = ROPE_THETA ** (-jnp.arange(0, ROPE_DIM, 2, dtype=F32) / ROPE_DIM)
    ang = jnp.arange(seq, dtype=F32)[:, None] * inv_freq[None, :]
    cos = jnp.cos(ang).T
    sin = jnp.sin(ang).T
    return jnp.concatenate([cos, cos], axis=0), jnp.concatenate([-sin, sin], axis=0)


def _moba_prep_t_kernel(q_ref, k_ref, v_ref, cos_ref, sin_ref, cost_ref, sint_ref,
                        qa_ref, kb_ref, vt_ref, bias_ref, km_sc):
    blk = q_ref.shape[1]
    d = HEAD_DIM
    half = ROPE_DIM // 2
    nsel = km_sc.shape[1]
    j = pl.program_id(1)

    @pl.when(j == 0)
    def _():
        km_sc[...] = jnp.zeros_like(km_sc)

    cos = cos_ref[...]
    sin = sin_ref[...]
    cos_t = cost_ref[...]
    sin_t = sint_ref[...]
    blk_id = lax.broadcasted_iota(jnp.int32, (nsel, blk), 0)
    blk_id_f = blk_id.astype(F32)
    mean_row = lax.broadcasted_iota(jnp.int32, (nsel, d), 0)
    for h in range(MOBA_HEADS):
        sl = slice(h * d, (h + 1) * d)
        q_t = q_ref[0, :, sl].T
        top = q_t[:ROPE_DIM]
        partner = jnp.concatenate([top[half:], top[:half]], axis=0)
        q_t = jnp.concatenate([top * cos_t + partner * sin_t, q_t[ROPE_DIM:]], axis=0)

        gate = jnp.dot(km_sc[h], q_t, preferred_element_type=F32,
                       precision=lax.Precision.HIGHEST)
        gate = jnp.where(blk_id < j, gate, -jnp.inf)
        sel_m1 = jnp.full((nsel, blk), -1.0, F32)
        for _ in range(MOBA_TOPK):
            mx = jnp.max(gate, axis=0, keepdims=True)
            first = jnp.min(jnp.where(gate == mx, blk_id_f, float(nsel)), axis=0, keepdims=True)
            first = jnp.where(mx > -jnp.inf, first, -1.0)
            hit = blk_id_f == first
            sel_m1 = jnp.where(hit, 0.0, sel_m1)
            gate = jnp.where(hit, -jnp.inf, gate)
        qa_ref[0, h, 0] = q_t.astype(BF16)
        for g in range(bias_ref.shape[3] // BIAS_ROWS):
            rows = sel_m1[g * MOBA_GROUP:(g + 1) * MOBA_GROUP] * MASK_BIG
            pad = jnp.zeros((BIAS_ROWS - MOBA_GROUP, blk), F32)
            bias_ref[0, h, 0, g * BIAS_ROWS:(g + 1) * BIAS_ROWS, :] = jnp.concatenate([rows, pad], axis=0)

        kk = _rotary(k_ref[0, :, sl], cos, sin)
        kb_ref[0, :, sl] = kk.astype(BF16)
        km_sc[h] = jnp.where(mean_row == j, jnp.mean(kk, axis=0, keepdims=True), km_sc[h])

        vt_ref[0, h, 0] = v_ref[0, :, sl].T.astype(BF16)


def _moba_prep_t(z3, tables, q_cb, k_cb, v_cb):
    nbatch, seq, _ = z3.shape
    blk = MOBA_BLOCK
    nblk = _tiles(seq, blk)
    nsel = -(-nblk // BIAS_ROWS) * BIAS_ROWS
    bias_rows = _tiles(nblk, MOBA_GROUP) * BIAS_ROWS
    w = MOBA_WIDTH
    d = HEAD_DIM
    hh = MOBA_HEADS
    cos_tab, sin_tab, cos_t, sin_t = tables
    blocks = ([((1, blk, w), F32)] * 3 + [((blk, d), F32)] * 2 + [((ROPE_DIM, blk), F32)] * 2
              + [((hh, d, blk), BF16), ((1, blk, w), BF16), ((hh, d, blk), BF16), ((hh, bias_rows, blk), F32)])
    scratch = [((hh, nsel, d), F32)]
    return pl.pallas_call(
        _moba_prep_t_kernel,
        grid=(nbatch, nblk),
        in_specs=[pl.BlockSpec((1, blk, w), lambda b, j: (b, j, q_cb)),
                  pl.BlockSpec((1, blk, w), lambda b, j: (b, j, k_cb)),
                  pl.BlockSpec((1, blk, w), lambda b, j: (b, j, v_cb)),
                  pl.BlockSpec((blk, d), lambda b, j: (j, 0)),
                  pl.BlockSpec((blk, d), lambda b, j: (j, 0)),
                  pl.BlockSpec((ROPE_DIM, blk), lambda b, j: (0, j)),
                  pl.BlockSpec((ROPE_DIM, blk), lambda b, j: (0, j))],
        out_specs=[pl.BlockSpec((1, hh, 1, d, blk), lambda b, j: (b, 0, j, 0, 0)),
                   pl.BlockSpec((1, blk, w), lambda b, j: (b, j, 0)),
                   pl.BlockSpec((1, hh, 1, d, blk), lambda b, j: (b, 0, j, 0, 0)),
                   pl.BlockSpec((1, hh, 1, bias_rows, blk), lambda b, j: (b, 0, j, 0, 0))],
        out_shape=[jax.ShapeDtypeStruct((nbatch, hh, nblk, d, blk), BF16),
                   jax.ShapeDtypeStruct((nbatch, seq, w), BF16),
                   jax.ShapeDtypeStruct((nbatch, hh, nblk, d, blk), BF16),
                   jax.ShapeDtypeStruct((nbatch, hh, nblk, bias_rows, blk), F32)],
        scratch_shapes=[pltpu.VMEM(s, dt) for s, dt in scratch],
        compiler_params=_params(("arbitrary", "arbitrary"), blocks, scratch),
        name="moba_prep",
    )(z3, z3, z3, cos_tab, sin_tab, cos_t, sin_t)


def _pipelined(start, count, first, step):
    lo = start % 2

    def pair(t, carry):
        i = start + 2 * t
        return step(i + 1, 1 - lo, step(i, lo, carry))

    carry = lax.fori_loop(0, (count - start) // 2, pair, first)
    if (count - start) % 2:
        carry = step(count - 1, (count - 1) % 2, carry)
    return carry


def _moba_flat_kernel(tile_ref, group_ref, q_ref, k_ref, bias_ref, vt_ref, o_ref,
                      s_sc, p_sc, acc_sc, m_sc, l_sc):
    nblk, d, blk = vt_ref.shape[2:]
    group = MOBA_GROUP
    gkeys = group * blk
    n_items = tile_ref.shape[0]
    c_exp = (d ** -0.5) * np.log2(np.e).astype(np.float32)
    key = lax.broadcasted_iota(jnp.int32, (blk, blk), 0)
    qry = lax.broadcasted_iota(jnp.int32, (blk, blk), 1)

    def weighted_values(first_blk, p):
        acc = None
        for i in range(p.shape[0] // blk):
            part = jnp.dot(vt_ref[0, 0, first_blk + i], p[i * blk:(i + 1) * blk],
                           preferred_element_type=F32)
            acc = part if acc is None else acc + part
        return acc

    def own_scores(j, slot):
        r0 = pl.multiple_of(j * blk, blk)
        s_t = jnp.dot(k_ref[0, pl.ds(r0, blk), :], q_ref[0, 0, j], preferred_element_type=F32)
        s_t = jnp.where(key <= qry, s_t, -jnp.inf)
        s_sc[slot, :blk, :] = s_t
        return jnp.max(s_t, axis=0, keepdims=True)

    def own_softmax(j, slot, s_max):
        next_max = own_scores(jnp.minimum(j + 1, nblk - 1), 1 - slot)
        p = jnp.exp2((s_sc[slot, :blk, :] - s_max) * c_exp)
        m_sc[j] = s_max
        l_sc[j] = jnp.sum(p, axis=0, keepdims=True)
        p_sc[slot] = p.astype(BF16)
        return next_max

    def own_values(j, slot):
        acc_sc[j] = weighted_values(j, p_sc[slot])

    def own_step(j, slot, s_max):
        own_values(j - 1, 1 - slot)
        return own_softmax(j, slot, s_max)

    _pipelined(1, nblk, own_softmax(0, 0, own_scores(0, 0)), own_step)
    own_values(nblk - 1, (nblk - 1) % 2)

    def group_scores(i, slot):
        j = tile_ref[i]
        g = group_ref[i]
        c0 = pl.multiple_of(g * gkeys, gkeys)
        s_t = jnp.dot(k_ref[0, pl.ds(c0, gkeys), :], q_ref[0, 0, j], preferred_element_type=F32)
        bias = bias_ref[0, 0, j, pl.ds(pl.multiple_of(g * BIAS_ROWS, BIAS_ROWS), BIAS_ROWS), :]
        s_t = jnp.concatenate([s_t[n * blk:(n + 1) * blk] + bias[n:n + 1] for n in range(group)],
                              axis=0)
        s_sc[slot] = s_t
        return jnp.max(s_t, axis=0, keepdims=True)

    def group_step(i, slot, s_max):
        next_max = group_scores(jnp.minimum(i + 1, n_items - 1), 1 - slot)
        j = tile_ref[i]
        m_run = m_sc[j]
        m_new = jnp.maximum(m_run, s_max)
        alpha = jnp.exp2((m_run - m_new) * c_exp)
        p = jnp.exp2((s_sc[slot] - m_new) * c_exp)
        m_sc[j] = m_new
        l_sc[j] = alpha * l_sc[j] + jnp.sum(p, axis=0, keepdims=True)
        acc_sc[j] = alpha * acc_sc[j] + weighted_values(group_ref[i] * group, p.astype(BF16))
        return next_max

    _pipelined(0, n_items, group_scores(0, 0), group_step)

    def finish(j, carry):
        r0 = pl.multiple_of(j * blk, blk)
        o_ref[0, pl.ds(r0, blk), :] = (acc_sc[j] / l_sc[j]).T.astype(o_ref.dtype)
        return carry

    lax.fori_loop(0, nblk, finish, 0)


def _moba_flat(q_t, kb, vt, bias):
    nbatch, seq, w = kb.shape
    blk = MOBA_BLOCK
    d = HEAD_DIM
    nblk = vt.shape[2]
    group = MOBA_GROUP
    bias_rows = bias.shape[3]
    items = [(j, g) for j in range(nblk) for g in range(-(-j // group))]
    assert items and _tiles(nblk, group)
    item_tile = jnp.asarray([j for j, _ in items], jnp.int32)
    item_group = jnp.asarray([g for _, g in items], jnp.int32)
    blocks = [((nblk, d, blk), BF16), ((1, seq, d), BF16), ((nblk, bias_rows, blk), F32), ((nblk, d, blk), BF16),
              ((1, seq, d), BF16)]
    scratch = [((2, group * blk, blk), F32), ((2, blk, blk), BF16), ((nblk, d, blk), F32),
               ((nblk, 1, blk), F32), ((nblk, 1, blk), F32)]
    grid_spec = pltpu.PrefetchScalarGridSpec(
        num_scalar_prefetch=2,
        grid=(nbatch, MOBA_HEADS),
        in_specs=[pl.BlockSpec((1, 1, nblk, d, blk), lambda b, h, *_: (b, h, 0, 0, 0)),
                  pl.BlockSpec((1, seq, d), lambda b, h, *_: (b, 0, h)),
                  pl.BlockSpec((1, 1, nblk, bias_rows, blk), lambda b, h, *_: (b, h, 0, 0, 0)),
                  pl.BlockSpec((1, 1, nblk, d, blk), lambda b, h, *_: (b, h, 0, 0, 0))],
        out_specs=pl.BlockSpec((1, seq, d), lambda b, h, *_: (b, 0, h)),
        scratch_shapes=[pltpu.VMEM(s, dt) for s, dt in scratch])
    return pl.pallas_call(
        _moba_flat_kernel,
        grid_spec=grid_spec,
        out_shape=jax.ShapeDtypeStruct((nbatch, seq, w), BF16),
        compiler_params=_params(("arbitrary", "arbitrary"), blocks, scratch),
        name="moba_attn",
    )(item_tile, item_group, q_t, kb, bias, vt)


def _gelu_tanh(x):
    c = np.sqrt(2.0 / np.pi).astype(np.float32)
    return x * (0.5 * (1.0 + jnp.tanh(c * (x + 0.044715 * (x * x * x)))))


def _gmlp_kernel(u_ref, v_ref, lg_ref, lb_ref, ws_ref, bst_ref, y_ref):
    rows = u_ref.shape[1]
    t = GMLP_CHUNK
    gd = GMLP_WIDTH // GMLP_GROUPS
    v = _gelu_tanh(v_ref[0])
    mu = jnp.mean(v, axis=-1, keepdims=True)
    vc = v - mu
    vln = vc * lax.rsqrt(jnp.mean(vc * vc, axis=-1, keepdims=True) + NORM_EPS) * lg_ref[...] + lb_ref[...]
    vb = vln.astype(BF16)
    row = lax.broadcasted_iota(jnp.int32, (t, t), 0)
    col = lax.broadcasted_iota(jnp.int32, (t, t), 1)
    for g in range(GMLP_GROUPS):
        wg = jnp.where(col <= row, ws_ref[g], 0.0).astype(BF16)
        bias = bst_ref[:, g:g + 1]
        cols = slice(g * gd, (g + 1) * gd)
        for c in range(rows // t):
            rs = slice(c * t, (c + 1) * t)
            mixed = jnp.dot(wg, vb[rs, cols], preferred_element_type=F32) + bias
            y_ref[0, rs, cols] = (_gelu_tanh(u_ref[0, rs, cols]) * mixed).astype(y_ref.dtype)


def _gmlp(z3, u_cb, v_cb, ln_g, ln_b, ws, bs, rows=512):
    nbatch, seq, _ = z3.shape
    w = GMLP_WIDTH
    t = GMLP_CHUNK
    blocks = [((1, rows, w), F32), ((1, rows, w), F32), ((GMLP_GROUPS, t, t), F32), ((1, rows, w), BF16)]
    return pl.pallas_call(
        _gmlp_kernel,
        grid=(nbatch, _tiles(seq, rows)),
        in_specs=[pl.BlockSpec((1, rows, w), lambda b, c: (b, c, u_cb)),
                  pl.BlockSpec((1, rows, w), lambda b, c: (b, c, v_cb)),
                  pl.BlockSpec((1, w), lambda b, c: (0, 0)),
                  pl.BlockSpec((1, w), lambda b, c: (0, 0)),
                  pl.BlockSpec((GMLP_GROUPS, t, t), lambda b, c: (0, 0, 0)),
                  pl.BlockSpec((t, GMLP_GROUPS), lambda b, c: (0, 0))],
        out_specs=pl.BlockSpec((1, rows, w), lambda b, c: (b, c, 0)),
        out_shape=jax.ShapeDtypeStruct((nbatch, seq, w), BF16),
        compiler_params=_params(("arbitrary", "arbitrary"), blocks),
        name="gmlp",
    )(z3, z3, ln_g.reshape(1, w), ln_b.reshape(1, w), ws, bs.T)


def _layer_weight_spec(layer, k, tn, index_of):
    return pl.BlockSpec((None, k, tn), lambda *idx: (layer,) + index_of(*idx))


def _merge_kernel(h_ref, wga_ref, wgb_ref, wgc_ref, ya_ref, yb_ref, yc_ref, wa_ref, wb_ref, wc_ref, o_ref):
    h = h_ref[...]

    def branch(wg_ref, y_ref, w_ref):
        gate = jax.nn.sigmoid(jnp.dot(h, wg_ref[...], preferred_element_type=F32))
        return gate * jnp.dot(y_ref[...], w_ref[...].astype(BF16), preferred_element_type=F32)

    merged = branch(wga_ref, ya_ref, wa_ref) + branch(wgb_ref, yb_ref, wb_ref) + branch(wgc_ref, yc_ref, wc_ref)
    o_ref[...] = merged.astype(o_ref.dtype)


def _merge(h, w_gate, ya, yb, yc, wa, wb, wc, layer, tm=1024, tn=512):
    m, d = h.shape
    nblk = _tiles(d, tn)
    ka, kb, kc = ya.shape[1], yb.shape[1], yc.shape[1]
    blocks = ([((tm, d), BF16)] + [((d, tn), BF16)] * 3
              + [((tm, ka), BF16), ((tm, kb), BF16), ((tm, kc), BF16)]
              + [((ka, tn), F32), ((kb, tn), F32), ((kc, tn), F32), ((tm, tn), BF16)])

    def gate_spec(branch):
        return pl.BlockSpec((None, d, tn), lambda i, j: (layer, 0, branch * nblk + j))

    def col(i, j):
        return (0, j)

    return pl.pallas_call(
        _merge_kernel,
        grid=(_tiles(m, tm), nblk),
        in_specs=[pl.BlockSpec((tm, d), lambda i, j: (i, 0)),
                  gate_spec(0), gate_spec(1), gate_spec(2),
                  pl.BlockSpec((tm, ka), lambda i, j: (i, 0)),
                  pl.BlockSpec((tm, kb), lambda i, j: (i, 0)),
                  pl.BlockSpec((tm, kc), lambda i, j: (i, 0)),
                  _layer_weight_spec(layer, ka, tn, col),
                  _layer_weight_spec(layer, kb, tn, col),
                  _layer_weight_spec(layer, kc, tn, col)],
        out_specs=pl.BlockSpec((tm, tn), lambda i, j: (i, j)),
        out_shape=jax.ShapeDtypeStruct((m, d), BF16),
        compiler_params=_params(("arbitrary", "arbitrary"), blocks),
        name="merge",
    )(h, w_gate, w_gate, w_gate, ya, yb, yc, wa, wb, wc)


def _residual_matmul_kernel(a_ref, w_ref, r_ref, o_ref):
    @pl.when(pl.program_id(2) == 0)
    def _():
        o_ref[...] = r_ref[...]

    o_ref[...] += jnp.dot(a_ref[...], w_ref[...].astype(BF16), preferred_element_type=F32)


def _residual_matmul(a, w, layer, res, tm=1024, tn=1024, tk=1024):
    m, k = a.shape
    n = w.shape[2]
    blocks = [((tm, tk), BF16), ((tk, tn), F32), ((tm, tn), F32), ((tm, tn), F32)]
    return pl.pallas_call(
        _residual_matmul_kernel,
        grid=(_tiles(m, tm), _tiles(n, tn), _tiles(k, tk)),
        in_specs=[pl.BlockSpec((tm, tk), lambda i, j, kk: (i, kk)),
                  _layer_weight_spec(layer, tk, tn, lambda i, j, kk: (kk, j)),
                  pl.BlockSpec((tm, tn), lambda i, j, kk: (i, j))],
        out_specs=pl.BlockSpec((tm, tn), lambda i, j, kk: (i, j)),
        out_shape=jax.ShapeDtypeStruct((m, n), F32),
        compiler_params=_params(("arbitrary", "arbitrary", "arbitrary"), blocks),
        name="residual_matmul",
    )(a, w, res)


def _relu2_matmul_kernel(x_ref, g_ref, w_ref, o_ref, h_sc):
    @pl.when(pl.program_id(1) == 0)
    def _():
        _rmsnorm_rows(x_ref, g_ref, h_sc)

    up = jnp.maximum(jnp.dot(h_sc[...], w_ref[...].astype(BF16), preferred_element_type=F32), 0.0)
    o_ref[...] = (up * up).astype(o_ref.dtype)


def _relu2_matmul(x, g, w, layer, tm=1024, tn=1024):
    m, k = x.shape
    n = w.shape[2]
    blocks = [((tm, k), F32), ((k, tn), F32), ((tm, tn), BF16)]
    scratch = [((tm, k), BF16)]
    return pl.pallas_call(
        _relu2_matmul_kernel,
        grid=(_tiles(m, tm), _tiles(n, tn)),
        in_specs=[pl.BlockSpec((tm, k), lambda i, j: (i, 0)),
                  pl.BlockSpec((1, k), lambda i, j: (0, 0)),
                  _layer_weight_spec(layer, k, tn, lambda i, j: (0, j))],
        out_specs=pl.BlockSpec((tm, tn), lambda i, j: (i, j)),
        out_shape=jax.ShapeDtypeStruct((m, n), BF16),
        scratch_shapes=[pltpu.VMEM(s, dt) for s, dt in scratch],
        compiler_params=_params(("arbitrary", "arbitrary"), blocks, scratch),
        name="mlp_up",
    )(x, g.reshape(1, k), w)


def _cast_kernel(w_ref, o_ref):
    o_ref[...] = w_ref[...].astype(o_ref.dtype)


def _cast_bf16(w, rows=512):
    depth, k, n = w.shape
    spec = pl.BlockSpec((None, rows, n), lambda l, r: (l, r, 0))
    return pl.pallas_call(
        _cast_kernel,
        grid=(depth, _tiles(k, rows)),
        in_specs=[spec],
        out_specs=spec,
        out_shape=jax.ShapeDtypeStruct(w.shape, BF16),
        compiler_params=_params(("arbitrary", "arbitrary"), [((rows, n), F32), ((rows, n), BF16)]),
        name="cast_bf16",
    )(w)


def _ple_kernel(x_ref, g_ref, wg_ref, p_ref, wp_ref, o_ref, h_sc):
    j = pl.program_id(1)

    @pl.when(j == 0)
    def _():
        _rmsnorm_rows(x_ref, g_ref, h_sc)

    tn = o_ref.shape[1]
    gate = jax.nn.sigmoid(jnp.dot(h_sc[...], wg_ref[...].astype(BF16), preferred_element_type=F32))
    emb = jnp.dot(p_ref[...].astype(BF16), wp_ref[...].astype(BF16), preferred_element_type=F32)
    o_ref[...] = x_ref[:, pl.ds(pl.multiple_of(j * tn, tn), tn)] + gate * emb


def _ple(x, g, w_gate, p, w_proj, layer, tm=1024, tn=512):
    m, d = x.shape
    pd = p.shape[2]
    blocks = [((tm, d), F32), ((d, tn), w_gate.dtype), ((tm, pd), F32), ((pd, tn), F32), ((tm, tn), F32)]
    scratch = [((tm, d), BF16)]
    return pl.pallas_call(
        _ple_kernel,
        grid=(_tiles(m, tm), _tiles(d, tn)),
        in_specs=[pl.BlockSpec((tm, d), lambda i, j: (i, 0)),
                  pl.BlockSpec((1, d), lambda i, j: (0, 0)),
                  _layer_weight_spec(layer, d, tn, lambda i, j: (0, j)),
                  pl.BlockSpec((None, tm, pd), lambda i, j: (layer, i, 0)),
                  _layer_weight_spec(layer, pd, tn, lambda i, j: (0, j))],
        out_specs=pl.BlockSpec((tm, tn), lambda i, j: (i, j)),
        out_shape=jax.ShapeDtypeStruct((m, d), F32),
        scratch_shapes=[pltpu.VMEM(s, dt) for s, dt in scratch],
        compiler_params=_params(("arbitrary", "arbitrary"), blocks, scratch),
        name="ple",
    )(x, g.reshape(1, d), w_gate, p, w_proj)


def kernel(x, p, norm_mix_g, w_in, mlstm_gate_b, mlstm_norm_g, gmlp_norm_g, gmlp_norm_b, gmlp_ws, gmlp_bs,
           w_branch_a, w_branch_b, w_branch_c, w_out, norm_mlp_g, w_mlp_up, w_mlp_down, norm_ple_g,
           w_ple_gate, w_ple_proj, final_norm_g):
    nbatch, seq, d = x.shape
    depth = w_in.shape[0]
    m = nbatch * seq
    assert d == MLSTM_WIDTH + MOBA_WIDTH + GMLP_WIDTH

    qkvo_a = 4 * MLSTM_WIDTH
    gates_if = 2 * MLSTM_HEADS
    main_b = qkvo_a + gates_if
    main_cols = 3 * MOBA_WIDTH + 2 * GMLP_WIDTH
    gate_off = main_b + main_cols
    assert w_in.shape[2] == gate_off + N_BRANCHES * d

    moba_q_cb = qkvo_a // MOBA_WIDTH
    moba_k_cb = moba_q_cb + 1
    moba_v_cb = moba_k_cb + 1
    gmlp_u_cb = (qkvo_a + 3 * MOBA_WIDTH) // GMLP_WIDTH
    gmlp_v_cb = gmlp_u_cb + 1

    z_cols = qkvo_a + main_cols
    colscale = jnp.ones((1, z_cols), F32).at[:, MLSTM_WIDTH:2 * MLSTM_WIDTH].set(HEAD_DIM ** -0.5)
    rope_tables = _rope_tables(seq) + _rope_tables_t(seq)

    xf = x.reshape(m, d)
    p_flat = p.reshape(depth, m, p.shape[-1])
    w_main, w_if, w_gate = _win_split(w_in, qkvo_a, gates_if, z_cols)
    w_ple_gate_b = _cast_bf16(w_ple_gate)
    for i in range(depth):
        z, zif, h = _inproj(xf, norm_mix_g[i], w_main, w_if, i, colscale)
        z3 = z.reshape(nbatch, seq, z_cols)
        zif3 = zif.reshape(nbatch, seq, GATE_LANES)

        ya = _mlstm(z3, zif3, mlstm_gate_b[i], mlstm_norm_g[i])
        yb = _moba_flat(*_moba_prep_t(z3, rope_tables, moba_q_cb, moba_k_cb, moba_v_cb))
        yc = _gmlp(z3, gmlp_u_cb, gmlp_v_cb, gmlp_norm_g[i], gmlp_norm_b[i], gmlp_ws[i], gmlp_bs[i])

        merged = _merge(h, w_gate, ya.reshape(m, -1), yb.reshape(m, -1), yc.reshape(m, -1),
                        w_branch_a, w_branch_b, w_branch_c, i)
        xf = _residual_matmul(merged, w_out, i, xf, tm=2048)

        hidden = _relu2_matmul(xf, norm_mlp_g[i], w_mlp_up, i)
        xf = _residual_matmul(hidden, w_mlp_down, i, xf, tm=2048)
        xf = _ple(xf, norm_ple_g[i], w_ple_gate_b, p_flat, w_ple_proj, i, tn=1024)

    return _rmsnorm(xf, final_norm_g, F32).reshape(nbatch, seq, d)
```

```python
import functools

import jax
import jax.numpy as jnp
import numpy as np
from jax import lax
from jax.experimental import pallas as pl
from jax.experimental.pallas import tpu as pltpu

F32 = jnp.float32
BF16 = jnp.bfloat16

HEAD_DIM = 128
MLSTM_HEADS = 4
MLSTM_WIDTH = MLSTM_HEADS * HEAD_DIM
MOBA_HEADS = 8
MOBA_WIDTH = MOBA_HEADS * HEAD_DIM
MOBA_BLOCK = 256
MOBA_TOPK = 3
ROPE_THETA = 500000.0
ROPE_DIM = HEAD_DIM // 4
GMLP_WIDTH = 512
GMLP_GROUPS = 4
GMLP_CHUNK = 128
N_BRANCHES = 3
NORM_EPS = 1e-6

LANES = 128
V7X_VMEM_BYTES = 64 * 1024 * 1024
VMEM_CEILING = V7X_VMEM_BYTES - 8 * 1024 * 1024

MLSTM_KERNEL_CHUNK = 128
STATE_ROWS = 16
GATE_LANES = LANES
MASK_BIG = 2.0 ** 100
BIAS_ROWS = 8
MOBA_GROUP = 4

NT_DIMS = (((1,), (1,)), ((), ()))


def _tiles(n, t):
    count, rest = divmod(n, t)
    assert rest == 0 and count > 0, (n, t)
    return count


def _nbytes(shape, dtype):
    return int(np.prod(shape)) * jnp.dtype(dtype).itemsize


def _params(semantics, blocks, scratch=()):
    need = 2 * sum(_nbytes(s, d) for s, d in blocks) + sum(_nbytes(s, d) for s, d in scratch)
    limit = min(VMEM_CEILING, need + need // 4 + 4 * 1024 * 1024)
    return pltpu.CompilerParams(dimension_semantics=semantics, vmem_limit_bytes=limit)


def _win_split_kernel(wm_ref, wif_ref, wg_ref, main_ref, if_ref, gate_ref):
    main_ref[...] = wm_ref[0].T.astype(BF16)
    gate_ref[...] = wg_ref[0].T.astype(BF16)
    _, n_if, k = wif_ref.shape
    padded = jnp.concatenate([wif_ref[0], jnp.zeros((GATE_LANES - n_if, k), F32)], axis=0)
    if_ref[...] = padded.T.astype(BF16)


def _win_split(w_in, head, n_if, main_cols, tn=512):
    depth, k, cols = w_in.shape
    gate_cols = cols - main_cols - n_if
    assert gate_cols == main_cols and head % tn == 0
    w_t = jnp.swapaxes(w_in, 1, 2)
    head_steps = head // tn
    blocks = [((tn, k), F32), ((n_if, k), F32), ((tn, k), F32), ((k, tn), BF16), ((k, GATE_LANES), BF16),
              ((k, tn), BF16)]

    def main_rows(l, r):
        return (l, pl.multiple_of(r * tn + jnp.where(r >= head_steps, n_if, 0), n_if), 0)

    def window(rows, start_of):
        return pl.BlockSpec((pl.Element(1), pl.Element(rows), pl.Element(k)), start_of)

    return pl.pallas_call(
        _win_split_kernel,
        grid=(depth, _tiles(main_cols, tn)),
        in_specs=[window(tn, main_rows),
                  window(n_if, lambda l, r: (l, head, 0)),
                  window(tn, lambda l, r: (l, pl.multiple_of(main_cols + n_if + r * tn, n_if), 0))],
        out_specs=[pl.BlockSpec((None, k, tn), lambda l, r: (l, 0, r)),
                   pl.BlockSpec((None, k, GATE_LANES), lambda l, r: (l, 0, 0)),
                   pl.BlockSpec((None, k, tn), lambda l, r: (l, 0, r))],
        out_shape=[jax.ShapeDtypeStruct((depth, k, main_cols), BF16),
                   jax.ShapeDtypeStruct((depth, k, GATE_LANES), BF16),
                   jax.ShapeDtypeStruct((depth, k, gate_cols), BF16)],
        compiler_params=_params(("arbitrary", "arbitrary"), blocks),
        name="win_split",
    )(w_t, w_t, w_t)


def _rmsnorm_rows(x_ref, g_ref, h_ref, rows=256):
    for r0 in range(0, x_ref.shape[0], rows):
        x = x_ref[r0:r0 + rows, :]
        y = x * lax.rsqrt(jnp.mean(x * x, axis=-1, keepdims=True) + NORM_EPS)
        h_ref[r0:r0 + rows, :] = (y * g_ref[...]).astype(h_ref.dtype)


def _inproj_kernel(x_ref, g_ref, w_ref, wif_ref, cs_ref, z_ref, zif_ref, h_ref):
    @pl.when(pl.program_id(1) == 0)
    def _():
        _rmsnorm_rows(x_ref, g_ref, h_ref)
        zif_ref[...] = jnp.dot(h_ref[...], wif_ref[...], preferred_element_type=F32)

    acc = jnp.dot(h_ref[...], w_ref[...], preferred_element_type=F32)
    z_ref[...] = acc * cs_ref[...]


def _inproj(x, g, w_main, w_if, layer, colscale, tm=1024, tn=1024):
    m, k = x.shape
    n = w_main.shape[2]
    blocks = [((tm, k), F32), ((k, tn), BF16), ((k, GATE_LANES), BF16), ((1, tn), F32),
              ((tm, tn), F32), ((tm, GATE_LANES), F32), ((tm, k), BF16)]
    return pl.pallas_call(
        _inproj_kernel,
        grid=(_tiles(m, tm), _tiles(n, tn)),
        in_specs=[pl.BlockSpec((tm, k), lambda i, j: (i, 0)),
                  pl.BlockSpec((1, k), lambda i, j: (0, 0)),
                  pl.BlockSpec((None, k, tn), lambda i, j: (layer, 0, j)),
                  pl.BlockSpec((None, k, GATE_LANES), lambda i, j: (layer, 0, 0)),
                  pl.BlockSpec((1, tn), lambda i, j: (0, j))],
        out_specs=[pl.BlockSpec((tm, tn), lambda i, j: (i, j)),
                   pl.BlockSpec((tm, GATE_LANES), lambda i, j: (i, 0)),
                   pl.BlockSpec((tm, k), lambda i, j: (i, 0))],
        out_shape=[jax.ShapeDtypeStruct((m, n), F32),
                   jax.ShapeDtypeStruct((m, GATE_LANES), F32),
                   jax.ShapeDtypeStruct((m, k), BF16)],
        compiler_params=_params(("arbitrary", "arbitrary"), blocks),
        name="in_proj",
    )(x, g.reshape(1, k), w_main, w_if, colscale)


def _log_sigmoid(x):
    return jnp.minimum(x, 0.0) - jnp.log1p(jnp.exp(-jnp.abs(x)))


def _exact_tril_matmul(tril, x):
    tril = tril.astype(BF16)
    hi = x.astype(BF16)
    rest = x - hi.astype(F32)
    mid = rest.astype(BF16)
    lo = (rest - mid.astype(F32)).astype(BF16)
    return (jnp.dot(tril, hi, preferred_element_type=F32) + jnp.dot(tril, mid, preferred_element_type=F32)
            + jnp.dot(tril, lo, preferred_element_type=F32))


def _mlstm_kernel(q_ref, k_ref, v_ref, o_ref, zif_ref, gb_ref, ng_ref, y_ref, c_sc, n_sc, m_sc):
    nbatch, chunk, _ = q_ref.shape
    heads = MLSTM_HEADS

    @pl.when(pl.program_id(0) == 0)
    def _():
        c_sc[...] = jnp.zeros_like(c_sc)
        n_sc[...] = jnp.zeros_like(n_sc)
        m_sc[...] = jnp.zeros_like(m_sc)

    row = lax.broadcasted_iota(jnp.int32, (chunk, chunk), 0)
    col = lax.broadcasted_iota(jnp.int32, (chunk, chunk), 1)
    causal_t = row <= col
    tril = jnp.where(col <= row, 1.0, 0.0).astype(F32)
    lane = lax.broadcasted_iota(jnp.int32, (chunk, GATE_LANES), 1)

    for b in range(nbatch):
        pre = zif_ref[b] + gb_ref[...]
        gates = jnp.where(lane < heads, pre, _log_sigmoid(pre))
        gcum = _exact_tril_matmul(tril, gates)
        gates_t = gates.T
        gcum_t = gcum.T
        for h in range(heads):
            s = b * heads + h
            sl = slice(h * HEAD_DIM, (h + 1) * HEAD_DIM)
            q = q_ref[b, :, sl]
            k = k_ref[b, :, sl]
            v = v_ref[b, :, sl]
            q_t = q.T
            qb = q.astype(BF16)
            kb = k.astype(BF16)
            q_tb = q_t.astype(BF16)
            v_t = v.T
            g_row = gcum_t[heads + h:heads + h + 1, :]
            i_row = gates_t[h:h + 1, :]
            ig_col = gates[:, h:h + 1] - gcum[:, heads + h:heads + h + 1]
            m_prev = m_sc[s][:, 0:1]
            c_prev = c_sc[s]
            n_prev = n_sc[s]

            log_w = jnp.where(causal_t, g_row + ig_col, -jnp.inf)
            log_a = g_row + m_prev
            m_row = jnp.maximum(jnp.max(log_w, axis=0, keepdims=True), log_a)
            qk = lax.dot_general(kb, qb, NT_DIMS, preferred_element_type=F32) * jnp.exp(log_w - m_row)
            a = jnp.exp(log_a - m_row)
            num = (jnp.dot(v_t.astype(BF16), qk.astype(BF16), preferred_element_type=F32)
                   + a * jnp.dot(c_prev.astype(BF16), q_tb, preferred_element_type=F32))
            n_dot_q = jnp.dot(n_prev.astype(BF16), q_tb, preferred_element_type=F32)[0:1]
            den = jnp.sum(qk, axis=0, keepdims=True) + a * n_dot_q
            h_out = num / jnp.maximum(jnp.abs(den), jnp.exp(-m_row))

            g_last = g_row[:, chunk - 1:chunk]
            log_u = g_last - g_row + i_row
            m_new = jnp.maximum(g_last + m_prev, jnp.max(log_u, axis=1, keepdims=True))
            decay = jnp.exp(g_last + m_prev - m_new)
            u = jnp.exp(log_u - m_new)
            c_sc[s] = decay * c_prev + jnp.dot((v_t * u).astype(BF16), kb, preferred_element_type=F32)
            u_rows = jnp.broadcast_to(u, (STATE_ROWS, chunk)).astype(BF16)
            n_sc[s] = decay * n_prev + jnp.dot(u_rows, kb, preferred_element_type=F32)
            m_sc[s] = jnp.broadcast_to(m_new, (1, LANES))

            yn = (h_out * lax.rsqrt(jnp.mean(h_out * h_out, axis=0, keepdims=True) + NORM_EPS)).T
            y_ref[b, :, sl] = (jax.nn.sigmoid(o_ref[b, :, sl]) * (yn * ng_ref[:, sl])).astype(y_ref.dtype)


def _mlstm(z3, zif3, gate_b, norm_g):
    nbatch, seq, _ = z3.shape
    chunk = MLSTM_KERNEL_CHUNK
    w = MLSTM_WIDTH
    streams = nbatch * MLSTM_HEADS
    gb = jnp.pad(gate_b, (0, GATE_LANES - gate_b.shape[0])).reshape(1, GATE_LANES)
    blocks = [((nbatch, chunk, w), F32)] * 4 + [((nbatch, chunk, GATE_LANES), F32),
                                               ((nbatch, chunk, w), BF16)]
    scratch = [((streams, HEAD_DIM, HEAD_DIM), F32), ((streams, STATE_ROWS, LANES), F32),
               ((streams, 1, LANES), F32)]

    def zcol(cb):
        return pl.BlockSpec((nbatch, chunk, w), lambda c: (0, c, cb))

    return pl.pallas_call(
        _mlstm_kernel,
        grid=(_tiles(seq, chunk),),
        in_specs=[zcol(0), zcol(1), zcol(2), zcol(3),
                  pl.BlockSpec((nbatch, chunk, GATE_LANES), lambda c: (0, c, 0)),
                  pl.BlockSpec((1, GATE_LANES), lambda c: (0, 0)),
                  pl.BlockSpec((1, w), lambda c: (0, 0))],
        out_specs=pl.BlockSpec((nbatch, chunk, w), lambda c: (0, c, 0)),
        out_shape=jax.ShapeDtypeStruct((nbatch, seq, w), BF16),
        scratch_shapes=[pltpu.VMEM(s, d) for s, d in scratch],
        compiler_params=_params(("arbitrary",), blocks, scratch),
        name="mlstm",
    )(z3, z3, z3, z3, zif3, gb, norm_g.reshape(1, w))


def _rope_tables(seq):
    half = ROPE_DIM // 2
    inv_freq = ROPE_THETA ** (-jnp.arange(0, ROPE_DIM, 2, dtype=F32) / ROPE_DIM)
    ang = jnp.arange(seq, dtype=F32)[:, None] * inv_freq[None, :]
    cos = jnp.cos(ang)
    sin = jnp.sin(ang)
    ones = jnp.ones((seq, HEAD_DIM - ROPE_DIM), F32)
    cos_tab = jnp.concatenate([cos, cos, ones], axis=1)
    sin_tab = jnp.concatenate([-sin, sin, 0.0 * ones], axis=1)
    assert cos_tab.shape == (seq, HEAD_DIM) and half * 2 == ROPE_DIM
    return cos_tab, sin_tab


def _rotary(t, cos, sin):
    half = ROPE_DIM // 2
    lane = lax.broadcasted_iota(jnp.int32, t.shape, 1)
    upper = pltpu.roll(t, HEAD_DIM - half, axis=1)
    lower = pltpu.roll(t, half, axis=1)
    partner = jnp.where(lane < half, upper, lower)
    return jnp.where(lane < ROPE_DIM, t * cos + partner * sin, t)


def _rope_tables_t(seq):
    inv_freq = ROPE_THETA ** (-jnp.arange(0, ROPE_DIM, 2, dtype=F32) / ROPE_DIM)
    ang = jnp.arange(seq, dtype=F32)[:, None] * inv_freq[None, :]
    cos = jnp.cos(ang).T
    sin = jnp.sin(ang).T
    return jnp.concatenate([cos, cos], axis=0), jnp.concatenate([-sin, sin], axis=0)


def _moba_prep_t_kernel(q_ref, k_ref, v_ref, cos_ref, sin_ref, cost_ref, sint_ref,
                        qa_ref, kb_ref, vt_ref, bias_ref, km_sc):
    blk = q_ref.shape[1]
    d = HEAD_DIM
    half = ROPE_DIM // 2
    nsel = km_sc.shape[1]
    j = pl.program_id(1)

    @pl.when(j == 0)
    def _():
        km_sc[...] = jnp.zeros_like(km_sc)

    cos = cos_ref[...]
    sin = sin_ref[...]
    cos_t = cost_ref[...]
    sin_t = sint_ref[...]
    blk_id = lax.broadcasted_iota(jnp.int32, (nsel, blk), 0)
    blk_id_f = blk_id.astype(F32)
    mean_row = lax.broadcasted_iota(jnp.int32, (nsel, d), 0)
    for h in range(MOBA_HEADS):
        sl = slice(h * d, (h + 1) * d)
        q_t = q_ref[0, :, sl].T
        top = q_t[:ROPE_DIM]
        partner = jnp.concatenate([top[half:], top[:half]], axis=0)
        q_t = jnp.concatenate([top * cos_t + partner * sin_t, q_t[ROPE_DIM:]], axis=0)

        gate = jnp.dot(km_sc[h], q_t, preferred_element_type=F32,
                       precision=lax.Precision.HIGHEST)
        gate = jnp.where(blk_id < j, gate, -jnp.inf)
        sel_m1 = jnp.full((nsel, blk), -1.0, F32)
        for _ in range(MOBA_TOPK):
            mx = jnp.max(gate, axis=0, keepdims=True)
            first = jnp.min(jnp.where(gate == mx, blk_id_f, float(nsel)), axis=0, keepdims=True)
            first = jnp.where(mx > -jnp.inf, first, -1.0)
            hit = blk_id_f == first
            sel_m1 = jnp.where(hit, 0.0, sel_m1)
            gate = jnp.where(hit, -jnp.inf, gate)
        qa_ref[0, h, 0] = q_t.astype(BF16)
        for g in range(bias_ref.shape[3] // BIAS_ROWS):
            rows = sel_m1[g * MOBA_GROUP:(g + 1) * MOBA_GROUP] * MASK_BIG
            pad = jnp.zeros((BIAS_ROWS - MOBA_GROUP, blk), F32)
            bias_ref[0, h, 0, g * BIAS_ROWS:(g + 1) * BIAS_ROWS, :] = jnp.concatenate([rows, pad], axis=0)

        kk = _rotary(k_ref[0, :, sl], cos, sin)
        kb_ref[0, :, sl] = kk.astype(BF16)
        km_sc[h] = jnp.where(mean_row == j, jnp.mean(kk, axis=0, keepdims=True), km_sc[h])

        vt_ref[0, h, 0] = v_ref[0, :, sl].T.astype(BF16)


def _moba_prep_t(z3, tables, q_cb, k_cb, v_cb):
    nbatch, seq, _ = z3.shape
    blk = MOBA_BLOCK
    nblk = _tiles(seq, blk)
    nsel = -(-nblk // BIAS_ROWS) * BIAS_ROWS
    bias_rows = _tiles(nblk, MOBA_GROUP) * BIAS_ROWS
    w = MOBA_WIDTH
    d = HEAD_DIM
    hh = MOBA_HEADS
    cos_tab, sin_tab, cos_t, sin_t = tables
    blocks = ([((1, blk, w), F32)] * 3 + [((blk, d), F32)] * 2 + [((ROPE_DIM, blk), F32)] * 2
              + [((hh, d, blk), BF16), ((1, blk, w), BF16), ((hh, d, blk), BF16), ((hh, bias_rows, blk), F32)])
    scratch = [((hh, nsel, d), F32)]
    return pl.pallas_call(
        _moba_prep_t_kernel,
        grid=(nbatch, nblk),
        in_specs=[pl.BlockSpec((1, blk, w), lambda b, j: (b, j, q_cb)),
                  pl.BlockSpec((1, blk, w), lambda b, j: (b, j, k_cb)),
                  pl.BlockSpec((1, blk, w), lambda b, j: (b, j, v_cb)),
                  pl.BlockSpec((blk, d), lambda b, j: (j, 0)),
                  pl.BlockSpec((blk, d), lambda b, j: (j, 0)),
                  pl.BlockSpec((ROPE_DIM, blk), lambda b, j: (0, j)),
                  pl.BlockSpec((ROPE_DIM, blk), lambda b, j: (0, j))],
        out_specs=[pl.BlockSpec((1, hh, 1, d, blk), lambda b, j: (b, 0, j, 0, 0)),
                   pl.BlockSpec((1, blk, w), lambda b, j: (b, j, 0)),
                   pl.BlockSpec((1, hh, 1, d, blk), lambda b, j: (b, 0, j, 0, 0)),
                   pl.BlockSpec((1, hh, 1, bias_rows, blk), lambda b, j: (b, 0, j, 0, 0))],
        out_shape=[jax.ShapeDtypeStruct((nbatch, hh, nblk, d, blk), BF16),
                   jax.ShapeDtypeStruct((nbatch, seq, w), BF16),
                   jax.ShapeDtypeStruct((nbatch, hh, nblk, d, blk), BF16),
                   jax.ShapeDtypeStruct((nbatch, hh, nblk, bias_rows, blk), F32)],
        scratch_shapes=[pltpu.VMEM(s, dt) for s, dt in scratch],
        compiler_params=_params(("arbitrary", "arbitrary"), blocks, scratch),
        name="moba_prep",
    )(z3, z3, z3, cos_tab, sin_tab, cos_t, sin_t)


def _pipelined(start, count, first, step):
    lo = start % 2

    def pair(t, carry):
        i = start + 2 * t
        return step(i + 1, 1 - lo, step(i, lo, carry))

    carry = lax.fori_loop(0, (count - start) // 2, pair, first)
    if (count - start) % 2:
        carry = step(count - 1, (count - 1) % 2, carry)
    return carry


def _moba_flat_kernel(tile_ref, group_ref, q_ref, k_ref, bias_ref, vt_ref, o_ref,
                      s_sc, p_sc, acc_sc, m_sc, l_sc):
    nblk, d, blk = vt_ref.shape[2:]
    group = MOBA_GROUP
    gkeys = group * blk
    n_items = tile_ref.shape[0]
    c_exp = (d ** -0.5) * np.log2(np.e).astype(np.float32)
    key = lax.broadcasted_iota(jnp.int32, (blk, blk), 0)
    qry = lax.broadcasted_iota(jnp.int32, (blk, blk), 1)

    def weighted_values(first_blk, p):
        acc = None
        for i in range(p.shape[0] // blk):
            part = jnp.dot(vt_ref[0, 0, first_blk + i], p[i * blk:(i + 1) * blk],
                           preferred_element_type=F32)
            acc = part if acc is None else acc + part
        return acc

    def own_scores(j, slot):
        r0 = pl.multiple_of(j * blk, blk)
        s_t = jnp.dot(k_ref[0, pl.ds(r0, blk), :], q_ref[0, 0, j], preferred_element_type=F32)
        s_t = jnp.where(key <= qry, s_t, -jnp.inf)
        s_sc[slot, :blk, :] = s_t
        return jnp.max(s_t, axis=0, keepdims=True)

    def own_softmax(j, slot, s_max):
        next_max = own_scores(jnp.minimum(j + 1, nblk - 1), 1 - slot)
        p = jnp.exp2((s_sc[slot, :blk, :] - s_max) * c_exp)
        m_sc[j] = s_max
        l_sc[j] = jnp.sum(p, axis=0, keepdims=True)
        p_sc[slot] = p.astype(BF16)
        return next_max

    def own_values(j, slot):
        acc_sc[j] = weighted_values(j, p_sc[slot])

    def own_step(j, slot, s_max):
        own_values(j - 1, 1 - slot)
        return own_softmax(j, slot, s_max)

    _pipelined(1, nblk, own_softmax(0, 0, own_scores(0, 0)), own_step)
    own_values(nblk - 1, (nblk - 1) % 2)

    def group_scores(i, slot):
        j = tile_ref[i]
        g = group_ref[i]
        c0 = pl.multiple_of(g * gkeys, gkeys)
        s_t = jnp.dot(k_ref[0, pl.ds(c0, gkeys), :], q_ref[0, 0, j], preferred_element_type=F32)
        bias = bias_ref[0, 0, j, pl.ds(pl.multiple_of(g * BIAS_ROWS, BIAS_ROWS), BIAS_ROWS), :]
        s_t = jnp.concatenate([s_t[n * blk:(n + 1) * blk] + bias[n:n + 1] for n in range(group)],
                              axis=0)
        s_sc[slot] = s_t
        return jnp.max(s_t, axis=0, keepdims=True)

    def group_step(i, slot, s_max):
        next_max = group_scores(jnp.minimum(i + 1, n_items - 1), 1 - slot)
        j = tile_ref[i]
        m_run = m_sc[j]
        m_new = jnp.maximum(m_run, s_max)
        alpha = jnp.exp2((m_run - m_new) * c_exp)
        p = jnp.exp2((s_sc[slot] - m_new) * c_exp)
        m_sc[j] = m_new
        l_sc[j] = alpha * l_sc[j] + jnp.sum(p, axis=0, keepdims=True)
        acc_sc[j] = alpha * acc_sc[j] + weighted_values(group_ref[i] * group, p.astype(BF16))
        return next_max

    _pipelined(0, n_items, group_scores(0, 0), group_step)

    def finish(j, carry):
        r0 = pl.multiple_of(j * blk, blk)
        o_ref[0, pl.ds(r0, blk), :] = (acc_sc[j] / l_sc[j]).T.astype(o_ref.dtype)
        return carry

    lax.fori_loop(0, nblk, finish, 0)


def _moba_flat(q_t, kb, vt, bias):
    nbatch, seq, w = kb.shape
    blk = MOBA_BLOCK
    d = HEAD_DIM
    nblk = vt.shape[2]
    group = MOBA_GROUP
    bias_rows = bias.shape[3]
    items = [(j, g) for j in range(nblk) for g in range(-(-j // group))]
    assert items and _tiles(nblk, group)
    item_tile = jnp.asarray([j for j, _ in items], jnp.int32)
    item_group = jnp.asarray([g for _, g in items], jnp.int32)
    blocks = [((nblk, d, blk), BF16), ((1, seq, d), BF16), ((nblk, bias_rows, blk), F32), ((nblk, d, blk), BF16),
              ((1, seq, d), BF16)]
    scratch = [((2, group * blk, blk), F32), ((2, blk, blk), BF16), ((nblk, d, blk), F32),
               ((nblk, 1, blk), F32), ((nblk, 1, blk), F32)]
    grid_spec = pltpu.PrefetchScalarGridSpec(
        num_scalar_prefetch=2,
        grid=(nbatch, MOBA_HEADS),
        in_specs=[pl.BlockSpec((1, 1, nblk, d, blk), lambda b, h, *_: (b, h, 0, 0, 0)),
                  pl.BlockSpec((1, seq, d), lambda b, h, *_: (b, 0, h)),
                  pl.BlockSpec((1, 1, nblk, bias_rows, blk), lambda b, h, *_: (b, h, 0, 0, 0)),
                  pl.BlockSpec((1, 1, nblk, d, blk), lambda b, h, *_: (b, h, 0, 0, 0))],
        out_specs=pl.BlockSpec((1, seq, d), lambda b, h, *_: (b, 0, h)),
        scratch_shapes=[pltpu.VMEM(s, dt) for s, dt in scratch])
    return pl.pallas_call(
        _moba_flat_kernel,
        grid_spec=grid_spec,
        out_shape=jax.ShapeDtypeStruct((nbatch, seq, w), BF16),
        compiler_params=_params(("arbitrary", "arbitrary"), blocks, scratch),
        name="moba_attn",
    )(item_tile, item_group, q_t, kb, bias, vt)


def _gelu_tanh(x):
    c = np.sqrt(2.0 / np.pi).astype(np.float32)
    return x * (0.5 * (1.0 + jnp.tanh(c * (x + 0.044715 * (x * x * x)))))


def _gmlp_kernel(u_ref, v_ref, lg_ref, lb_ref, ws_ref, bst_ref, y_ref):
    rows = u_ref.shape[1]
    t = GMLP_CHUNK
    gd = GMLP_WIDTH // GMLP_GROUPS
    v = _gelu_tanh(v_ref[0])
    mu = jnp.mean(v, axis=-1, keepdims=True)
    vc = v - mu
    vln = vc * lax.rsqrt(jnp.mean(vc * vc, axis=-1, keepdims=True) + NORM_EPS) * lg_ref[...] + lb_ref[...]
    vb = vln.astype(BF16)
    row = lax.broadcasted_iota(jnp.int32, (t, t), 0)
    col = lax.broadcasted_iota(jnp.int32, (t, t), 1)
    for g in range(GMLP_GROUPS):
        wg = jnp.where(col <= row, ws_ref[g], 0.0).astype(BF16)
        bias = bst_ref[:, g:g + 1]
        cols = slice(g * gd, (g + 1) * gd)
        for c in range(rows // t):
            rs = slice(c * t, (c + 1) * t)
            mixed = jnp.dot(wg, vb[rs, cols], preferred_element_type=F32) + bias
            y_ref[0, rs, cols] = (_gelu_tanh(u_ref[0, rs, cols]) * mixed).astype(y_ref.dtype)


def _gmlp(z3, u_cb, v_cb, ln_g, ln_b, ws, bs, rows=512):
    nbatch, seq, _ = z3.shape
    w = GMLP_WIDTH
    t = GMLP_CHUNK
    blocks = [((1, rows, w), F32), ((1, rows, w), F32), ((GMLP_GROUPS, t, t), F32), ((1, rows, w), BF16)]
    return pl.pallas_call(
        _gmlp_kernel,
        grid=(nbatch, _tiles(seq, rows)),
        in_specs=[pl.BlockSpec((1, rows, w), lambda b, c: (b, c, u_cb)),
                  pl.BlockSpec((1, rows, w), lambda b, c: (b, c, v_cb)),
                  pl.BlockSpec((1, w), lambda b, c: (0, 0)),
                  pl.BlockSpec((1, w), lambda b, c: (0, 0)),
                  pl.BlockSpec((GMLP_GROUPS, t, t), lambda b, c: (0, 0, 0)),
                  pl.BlockSpec((t, GMLP_GROUPS), lambda b, c: (0, 0))],
        out_specs=pl.BlockSpec((1, rows, w), lambda b, c: (b, c, 0)),
        out_shape=jax.ShapeDtypeStruct((nbatch, seq, w), BF16),
        compiler_params=_params(("arbitrary", "arbitrary"), blocks),
        name="gmlp",
    )(z3, z3, ln_g.reshape(1, w), ln_b.reshape(1, w), ws, bs.T)


def _layer_weight_spec(layer, k, tn, index_of):
    return pl.BlockSpec((None, k, tn), lambda *idx: (layer,) + index_of(*idx))


def _merge_kernel(h_ref, wga_ref, wgb_ref, wgc_ref, ya_ref, yb_ref, yc_ref, wa_ref, wb_ref, wc_ref, o_ref):
    h = h_ref[...]

    def branch(wg_ref, y_ref, w_ref):
        gate = jax.nn.sigmoid(jnp.dot(h, wg_ref[...], preferred_element_type=F32))
        return gate * jnp.dot(y_ref[...], w_ref[...].astype(BF16), preferred_element_type=F32)

    merged = branch(wga_ref, ya_ref, wa_ref) + branch(wgb_ref, yb_ref, wb_ref) + branch(wgc_ref, yc_ref, wc_ref)
    o_ref[...] = merged.astype(o_ref.dtype)


def _merge(h, w_gate, ya, yb, yc, wa, wb, wc, layer, tm=1024, tn=512):
    m, d = h.shape
    nblk = _tiles(d, tn)
    ka, kb, kc = ya.shape[1], yb.shape[1], yc.shape[1]
    blocks = ([((tm, d), BF16)] + [((d, tn), BF16)] * 3
              + [((tm, ka), BF16), ((tm, kb), BF16), ((tm, kc), BF16)]
              + [((ka, tn), F32), ((kb, tn), F32), ((kc, tn), F32), ((tm, tn), BF16)])

    def gate_spec(branch):
        return pl.BlockSpec((None, d, tn), lambda i, j: (layer, 0, branch * nblk + j))

    def col(i, j):
        return (0, j)

    return pl.pallas_call(
        _merge_kernel,
        grid=(_tiles(m, tm), nblk),
        in_specs=[pl.BlockSpec((tm, d), lambda i, j: (i, 0)),
                  gate_spec(0), gate_spec(1), gate_spec(2),
                  pl.BlockSpec((tm, ka), lambda i, j: (i, 0)),
                  pl.BlockSpec((tm, kb), lambda i, j: (i, 0)),
                  pl.BlockSpec((tm, kc), lambda i, j: (i, 0)),
                  _layer_weight_spec(layer, ka, tn, col),
                  _layer_weight_spec(layer, kb, tn, col),
                  _layer_weight_spec(layer, kc, tn, col)],
        out_specs=pl.BlockSpec((tm, tn), lambda i, j: (i, j)),
        out_shape=jax.ShapeDtypeStruct((m, d), BF16),
        compiler_params=_params(("arbitrary", "arbitrary"), blocks),
        name="merge",
    )(h, w_gate, w_gate, w_gate, ya, yb, yc, wa, wb, wc)


def _residual_matmul_kernel(a_ref, w_ref, r_ref, o_ref):
    @pl.when(pl.program_id(2) == 0)
    def _():
        o_ref[...] = r_ref[...]

    o_ref[...] += jnp.dot(a_ref[...], w_ref[...].astype(BF16), preferred_element_type=F32)


def _residual_matmul(a, w, layer, res, tm=1024, tn=1024, tk=1024):
    m, k = a.shape
    n = w.shape[2]
    blocks = [((tm, tk), BF16), ((tk, tn), F32), ((tm, tn), F32), ((tm, tn), F32)]
    return pl.pallas_call(
        _residual_matmul_kernel,
        grid=(_tiles(m, tm), _tiles(n, tn), _tiles(k, tk)),
        in_specs=[pl.BlockSpec((tm, tk), lambda i, j, kk: (i, kk)),
                  _layer_weight_spec(layer, tk, tn, lambda i, j, kk: (kk, j)),
                  pl.BlockSpec((tm, tn), lambda i, j, kk: (i, j))],
        out_specs=pl.BlockSpec((tm, tn), lambda i, j, kk: (i, j)),
        out_shape=jax.ShapeDtypeStruct((m, n), F32),
        compiler_params=_params(("arbitrary", "arbitrary", "arbitrary"), blocks),
        name="residual_matmul",
    )(a, w, res)


def _relu2_matmul_kernel(x_ref, g_ref, w_ref, o_ref, h_sc):
    @pl.when(pl.program_id(1) == 0)
    def _():
        _rmsnorm_rows(x_ref, g_ref, h_sc)

    up = jnp.maximum(jnp.dot(h_sc[...], w_ref[...].astype(BF16), preferred_element_type=F32), 0.0)
    o_ref[...] = (up * up).astype(o_ref.dtype)


def _relu2_matmul(x, g, w, layer, tm=1024, tn=1024):
    m, k = x.shape
    n = w.shape[2]
    blocks = [((tm, k), F32), ((k, tn), F32), ((tm, tn), BF16)]
    scratch = [((tm, k), BF16)]
    return pl.pallas_call(
        _relu2_matmul_kernel,
        grid=(_tiles(m, tm), _tiles(n, tn)),
        in_specs=[pl.BlockSpec((tm, k), lambda i, j: (i, 0)),
                  pl.BlockSpec((1, k), lambda i, j: (0, 0)),
                  _layer_weight_spec(layer, k, tn, lambda i, j: (0, j))],
        out_specs=pl.BlockSpec((tm, tn), lambda i, j: (i, j)),
        out_shape=jax.ShapeDtypeStruct((m, n), BF16),
        scratch_shapes=[pltpu.VMEM(s, dt) for s, dt in scratch],
        compiler_params=_params(("arbitrary", "arbitrary"), blocks, scratch),
        name="mlp_up",
    )(x, g.reshape(1, k), w)


def _cast_kernel(w_ref, o_ref):
    o_ref[...] = w_ref[...].astype(o_ref.dtype)


def _cast_bf16(w, rows=512):
    depth, k, n = w.shape
    spec = pl.BlockSpec((None, rows, n), lambda l, r: (l, r, 0))
    return pl.pallas_call(
        _cast_kernel,
        grid=(depth, _tiles(k, rows)),
        in_specs=[spec],
        out_specs=spec,
        out_shape=jax.ShapeDtypeStruct(w.shape, BF16),
        compiler_params=_params(("arbitrary", "arbitrary"), [((rows, n), F32), ((rows, n), BF16)]),
        name="cast_bf16",
    )(w)


def _ple_kernel(x_ref, g_ref, wg_ref, p_ref, wp_ref, fg_ref, o_ref, h_sc, *, final_norm):
    j = pl.program_id(1)

    @pl.when(j == 0)
    def _():
        _rmsnorm_rows(x_ref, g_ref, h_sc)

    tn = o_ref.shape[1]
    gate = jax.nn.sigmoid(jnp.dot(h_sc[...], wg_ref[...].astype(BF16), preferred_element_type=F32))
    emb = jnp.dot(p_ref[...].astype(BF16), wp_ref[...].astype(BF16), preferred_element_type=F32)
    out = x_ref[:, pl.ds(pl.multiple_of(j * tn, tn), tn)] + gate * emb
    if final_norm:
        out = out * lax.rsqrt(jnp.mean(out * out, axis=-1, keepdims=True) + NORM_EPS) * fg_ref[...]
    o_ref[...] = out


def _ple(x, g, w_gate, p, w_proj, layer, final_g, final_norm, tm=1024, tn=1024):
    m, d = x.shape
    pd = p.shape[2]
    if final_norm:
        tm, tn = tm // 2, d
    blocks = [((tm, d), F32), ((d, tn), w_gate.dtype), ((tm, pd), F32), ((pd, tn), F32), ((tm, tn), F32)]
    scratch = [((tm, d), BF16)]
    return pl.pallas_call(
        functools.partial(_ple_kernel, final_norm=final_norm),
        grid=(_tiles(m, tm), _tiles(d, tn)),
        in_specs=[pl.BlockSpec((tm, d), lambda i, j: (i, 0)),
                  pl.BlockSpec((1, d), lambda i, j: (0, 0)),
                  _layer_weight_spec(layer, d, tn, lambda i, j: (0, j)),
                  pl.BlockSpec((None, tm, pd), lambda i, j: (layer, i, 0)),
                  _layer_weight_spec(layer, pd, tn, lambda i, j: (0, j)),
                  pl.BlockSpec((1, tn), lambda i, j: (0, j))],
        out_specs=pl.BlockSpec((tm, tn), lambda i, j: (i, j)),
        out_shape=jax.ShapeDtypeStruct((m, d), F32),
        scratch_shapes=[pltpu.VMEM(s, dt) for s, dt in scratch],
        compiler_params=_params(("arbitrary", "arbitrary"), blocks, scratch),
        name="ple",
    )(x, g.reshape(1, d), w_gate, p, w_proj, final_g.reshape(1, d))


def kernel(x, p, norm_mix_g, w_in, mlstm_gate_b, mlstm_norm_g, gmlp_norm_g, gmlp_norm_b, gmlp_ws, gmlp_bs,
           w_branch_a, w_branch_b, w_branch_c, w_out, norm_mlp_g, w_mlp_up, w_mlp_down, norm_ple_g,
           w_ple_gate, w_ple_proj, final_norm_g):
    nbatch, seq, d = x.shape
    depth = w_in.shape[0]
    m = nbatch * seq
    assert d == MLSTM_WIDTH + MOBA_WIDTH + GMLP_WIDTH

    qkvo_a = 4 * MLSTM_WIDTH
    gates_if = 2 * MLSTM_HEADS
    main_b = qkvo_a + gates_if
    main_cols = 3 * MOBA_WIDTH + 2 * GMLP_WIDTH
    gate_off = main_b + main_cols
    assert w_in.shape[2] == gate_off + N_BRANCHES * d

    moba_q_cb = qkvo_a // MOBA_WIDTH
    moba_k_cb = moba_q_cb + 1
    moba_v_cb = moba_k_cb + 1
    gmlp_u_cb = (qkvo_a + 3 * MOBA_WIDTH) // GMLP_WIDTH
    gmlp_v_cb = gmlp_u_cb + 1

    z_cols = qkvo_a + main_cols
    colscale = jnp.ones((1, z_cols), F32).at[:, MLSTM_WIDTH:2 * MLSTM_WIDTH].set(HEAD_DIM ** -0.5)
    rope_tables = _rope_tables(seq) + _rope_tables_t(seq)

    xf = x.reshape(m, d)
    p_flat = p.reshape(depth, m, p.shape[-1])
    w_main, w_if, w_gate = _win_split(w_in, qkvo_a, gates_if, z_cols)
    w_ple_gate_b = _cast_bf16(w_ple_gate)
    for i in range(depth):
        z, zif, h = _inproj(xf, norm_mix_g[i], w_main, w_if, i, colscale)
        z3 = z.reshape(nbatch, seq, z_cols)
        zif3 = zif.reshape(nbatch, seq, GATE_LANES)

        ya = _mlstm(z3, zif3, mlstm_gate_b[i], mlstm_norm_g[i])
        yb = _moba_flat(*_moba_prep_t(z3, rope_tables, moba_q_cb, moba_k_cb, moba_v_cb))
        yc = _gmlp(z3, gmlp_u_cb, gmlp_v_cb, gmlp_norm_g[i], gmlp_norm_b[i], gmlp_ws[i], gmlp_bs[i])

        merged = _merge(h, w_gate, ya.reshape(m, -1), yb.reshape(m, -1), yc.reshape(m, -1),
                        w_branch_a, w_branch_b, w_branch_c, i)
        xf = _residual_matmul(merged, w_out, i, xf, tm=2048)

        hidden = _relu2_matmul(xf, norm_mlp_g[i], w_mlp_up, i)
        xf = _residual_matmul(hidden, w_mlp_down, i, xf, tm=2048)
        xf = _ple(xf, norm_ple_g[i], w_ple_gate_b, p_flat, w_ple_proj, i, final_norm_g, i == depth - 1)

    return xf.reshape(nbatch, seq, d)
```

```python
import functools

import jax
import jax.numpy as jnp
import numpy as np
from jax import lax
from jax.experimental import pallas as pl
from jax.experimental.pallas import tpu as pltpu

F32 = jnp.float32
BF16 = jnp.bfloat16

HEAD_DIM = 128
MLSTM_HEADS = 4
MLSTM_WIDTH = MLSTM_HEADS * HEAD_DIM
MOBA_HEADS = 8
MOBA_WIDTH = MOBA_HEADS * HEAD_DIM
MOBA_BLOCK = 256
MOBA_TOPK = 3
ROPE_THETA = 500000.0
ROPE_DIM = HEAD_DIM // 4
GMLP_WIDTH = 512
GMLP_GROUPS = 4
GMLP_CHUNK = 128
N_BRANCHES = 3
NORM_EPS = 1e-6

LANES = 128
V7X_VMEM_BYTES = 64 * 1024 * 1024
VMEM_CEILING = V7X_VMEM_BYTES - 8 * 1024 * 1024

MLSTM_KERNEL_CHUNK = 128
STATE_ROWS = 16
GATE_LANES = LANES
MASK_BIG = 2.0 ** 100
BIAS_ROWS = 8
MOBA_GROUP = 4

NT_DIMS = (((1,), (1,)), ((), ()))


def _tiles(n, t):
    count, rest = divmod(n, t)
    assert rest == 0 and count > 0, (n, t)
    return count


def _nbytes(shape, dtype):
    return int(np.prod(shape)) * jnp.dtype(dtype).itemsize


def _params(semantics, blocks, scratch=()):
    need = 2 * sum(_nbytes(s, d) for s, d in blocks) + sum(_nbytes(s, d) for s, d in scratch)
    limit = min(VMEM_CEILING, need + need // 4 + 4 * 1024 * 1024)
    return pltpu.CompilerParams(dimension_semantics=semantics, vmem_limit_bytes=limit)


def _win_split_kernel(wm_ref, wif_ref, wg_ref, main_ref, if_ref, gate_ref):
    main_ref[...] = wm_ref[0].T.astype(BF16)
    gate_ref[...] = wg_ref[0].T.astype(BF16)
    _, n_if, k = wif_ref.shape
    padded = jnp.concatenate([wif_ref[0], jnp.zeros((GATE_LANES - n_if, k), F32)], axis=0)
    if_ref[...] = padded.T.astype(BF16)


def _win_split(w_in, head, n_if, main_cols, tn=512):
    depth, k, cols = w_in.shape
    gate_cols = cols - main_cols - n_if
    assert gate_cols == main_cols and head % tn == 0
    w_t = jnp.swapaxes(w_in, 1, 2)
    head_steps = head // tn
    blocks = [((tn, k), F32), ((n_if, k), F32), ((tn, k), F32), ((k, tn), BF16), ((k, GATE_LANES), BF16),
              ((k, tn), BF16)]

    def main_rows(l, r):
        return (l, pl.multiple_of(r * tn + jnp.where(r >= head_steps, n_if, 0), n_if), 0)

    def window(rows, start_of):
        return pl.BlockSpec((pl.Element(1), pl.Element(rows), pl.Element(k)), start_of)

    return pl.pallas_call(
        _win_split_kernel,
        grid=(depth, _tiles(main_cols, tn)),
        in_specs=[window(tn, main_rows),
                  window(n_if, lambda l, r: (l, head, 0)),
                  window(tn, lambda l, r: (l, pl.multiple_of(main_cols + n_if + r * tn, n_if), 0))],
        out_specs=[pl.BlockSpec((None, k, tn), lambda l, r: (l, 0, r)),
                   pl.BlockSpec((None, k, GATE_LANES), lambda l, r: (l, 0, 0)),
                   pl.BlockSpec((None, k, tn), lambda l, r: (l, 0, r))],
        out_shape=[jax.ShapeDtypeStruct((depth, k, main_cols), BF16),
                   jax.ShapeDtypeStruct((depth, k, GATE_LANES), BF16),
                   jax.ShapeDtypeStruct((depth, k, gate_cols), BF16)],
        compiler_params=_params(("arbitrary", "arbitrary"), blocks),
        name="win_split",
    )(w_t, w_t, w_t)


def _rmsnorm_rows(x_ref, g_ref, h_ref, rows=256):
    for r0 in range(0, x_ref.shape[0], rows):
        x = x_ref[r0:r0 + rows, :]
        y = x * lax.rsqrt(jnp.mean(x * x, axis=-1, keepdims=True) + NORM_EPS)
        h_ref[r0:r0 + rows, :] = (y * g_ref[...]).astype(h_ref.dtype)


def _inproj_kernel(x_ref, g_ref, w_ref, wif_ref, cs_ref, z_ref, zif_ref, h_ref):
    @pl.when(pl.program_id(1) == 0)
    def _():
        _rmsnorm_rows(x_ref, g_ref, h_ref)
        zif_ref[...] = jnp.dot(h_ref[...], wif_ref[...], preferred_element_type=F32)

    acc = jnp.dot(h_ref[...], w_ref[...], preferred_element_type=F32)
    z_ref[...] = acc * cs_ref[...]


def _inproj(x, g, w_main, w_if, layer, colscale, tm=1024, tn=1024):
    m, k = x.shape
    n = w_main.shape[2]
    blocks = [((tm, k), F32), ((k, tn), BF16), ((k, GATE_LANES), BF16), ((1, tn), F32),
              ((tm, tn), F32), ((tm, GATE_LANES), F32), ((tm, k), BF16)]
    return pl.pallas_call(
        _inproj_kernel,
        grid=(_tiles(m, tm), _tiles(n, tn)),
        in_specs=[pl.BlockSpec((tm, k), lambda i, j: (i, 0)),
                  pl.BlockSpec((1, k), lambda i, j: (0, 0)),
                  pl.BlockSpec((None, k, tn), lambda i, j: (layer, 0, j)),
                  pl.BlockSpec((None, k, GATE_LANES), lambda i, j: (layer, 0, 0)),
                  pl.BlockSpec((1, tn), lambda i, j: (0, j))],
        out_specs=[pl.BlockSpec((tm, tn), lambda i, j: (i, j)),
                   pl.BlockSpec((tm, GATE_LANES), lambda i, j: (i, 0)),
                   pl.BlockSpec((tm, k), lambda i, j: (i, 0))],
        out_shape=[jax.ShapeDtypeStruct((m, n), F32),
                   jax.ShapeDtypeStruct((m, GATE_LANES), F32),
                   jax.ShapeDtypeStruct((m, k), BF16)],
        compiler_params=_params(("arbitrary", "arbitrary"), blocks),
        name="in_proj",
    )(x, g.reshape(1, k), w_main, w_if, colscale)


def _log_sigmoid(x):
    return jnp.minimum(x, 0.0) - jnp.log1p(jnp.exp(-jnp.abs(x)))


def _exact_tril_matmul(tril, x):
    tril = tril.astype(BF16)
    hi = x.astype(BF16)
    rest = x - hi.astype(F32)
    mid = rest.astype(BF16)
    lo = (rest - mid.astype(F32)).astype(BF16)
    return (jnp.dot(tril, hi, preferred_element_type=F32) + jnp.dot(tril, mid, preferred_element_type=F32)
            + jnp.dot(tril, lo, preferred_element_type=F32))


def _mlstm_kernel(q_ref, k_ref, v_ref, o_ref, zif_ref, gb_ref, ng_ref, y_ref, c_sc, n_sc, m_sc):
    nbatch, chunk, _ = q_ref.shape
    heads = MLSTM_HEADS

    @pl.when(pl.program_id(0) == 0)
    def _():
        c_sc[...] = jnp.zeros_like(c_sc)
        n_sc[...] = jnp.zeros_like(n_sc)
        m_sc[...] = jnp.zeros_like(m_sc)

    row = lax.broadcasted_iota(jnp.int32, (chunk, chunk), 0)
    col = lax.broadcasted_iota(jnp.int32, (chunk, chunk), 1)
    causal_t = row <= col
    tril = jnp.where(col <= row, 1.0, 0.0).astype(F32)
    lane = lax.broadcasted_iota(jnp.int32, (chunk, GATE_LANES), 1)

    for b in range(nbatch):
        pre = zif_ref[b] + gb_ref[...]
        gates = jnp.where(lane < heads, pre, _log_sigmoid(pre))
        gcum = _exact_tril_matmul(tril, gates)
        gates_t = gates.T
        gcum_t = gcum.T
        for h in range(heads):
            s = b * heads + h
            sl = slice(h * HEAD_DIM, (h + 1) * HEAD_DIM)
            q = q_ref[b, :, sl]
            k = k_ref[b, :, sl]
            v = v_ref[b, :, sl]
            q_t = q.T
            qb = q.astype(BF16)
            kb = k.astype(BF16)
            q_tb = q_t.astype(BF16)
            v_t = v.T
            g_row = gcum_t[heads + h:heads + h + 1, :]
            i_row = gates_t[h:h + 1, :]
            ig_col = gates[:, h:h + 1] - gcum[:, heads + h:heads + h + 1]
            m_prev = m_sc[s][:, 0:1]
            c_prev = c_sc[s]
            n_prev = n_sc[s]

            log_w = jnp.where(causal_t, g_row + ig_col, -jnp.inf)
            log_a = g_row + m_prev
            m_row = jnp.maximum(jnp.max(log_w, axis=0, keepdims=True), log_a)
            qk = lax.dot_general(kb, qb, NT_DIMS, preferred_element_type=F32) * jnp.exp(log_w - m_row)
            a = jnp.exp(log_a - m_row)
            num = (jnp.dot(v_t.astype(BF16), qk.astype(BF16), preferred_element_type=F32)
                   + a * jnp.dot(c_prev.astype(BF16), q_tb, preferred_element_type=F32))
            n_dot_q = jnp.dot(n_prev.astype(BF16), q_tb, preferred_element_type=F32)[0:1]
            den = jnp.sum(qk, axis=0, keepdims=True) + a * n_dot_q
            h_out = num / jnp.maximum(jnp.abs(den), jnp.exp(-m_row))

            g_last = g_row[:, chunk - 1:chunk]
            log_u = g_last - g_row + i_row
            m_new = jnp.maximum(g_last + m_prev, jnp.max(log_u, axis=1, keepdims=True))
            decay = jnp.exp(g_last + m_prev - m_new)
            u = jnp.exp(log_u - m_new)
            c_sc[s] = decay * c_prev + jnp.dot((v_t * u).astype(BF16), kb, preferred_element_type=F32)
            u_rows = jnp.broadcast_to(u, (STATE_ROWS, chunk)).astype(BF16)
            n_sc[s] = decay * n_prev + jnp.dot(u_rows, kb, preferred_element_type=F32)
            m_sc[s] = jnp.broadcast_to(m_new, (1, LANES))

            yn = (h_out * lax.rsqrt(jnp.mean(h_out * h_out, axis=0, keepdims=True) + NORM_EPS)).T
            y_ref[b, :, sl] = (jax.nn.sigmoid(o_ref[b, :, sl]) * (yn * ng_ref[:, sl])).astype(y_ref.dtype)


def _mlstm(z3, zif3, gate_b, norm_g):
    nbatch, seq, _ = z3.shape
    chunk = MLSTM_KERNEL_CHUNK
    w = MLSTM_WIDTH
    streams = nbatch * MLSTM_HEADS
    gb = jnp.pad(gate_b, (0, GATE_LANES - gate_b.shape[0])).reshape(1, GATE_LANES)
    blocks = [((nbatch, chunk, w), F32)] * 4 + [((nbatch, chunk, GATE_LANES), F32),
                                               ((nbatch, chunk, w), BF16)]
    scratch = [((streams, HEAD_DIM, HEAD_DIM), F32), ((streams, STATE_ROWS, LANES), F32),
               ((streams, 1, LANES), F32)]

    def zcol(cb):
        return pl.BlockSpec((nbatch, chunk, w), lambda c: (0, c, cb))

    return pl.pallas_call(
        _mlstm_kernel,
        grid=(_tiles(seq, chunk),),
        in_specs=[zcol(0), zcol(1), zcol(2), zcol(3),
                  pl.BlockSpec((nbatch, chunk, GATE_LANES), lambda c: (0, c, 0)),
                  pl.BlockSpec((1, GATE_LANES), lambda c: (0, 0)),
                  pl.BlockSpec((1, w), lambda c: (0, 0))],
        out_specs=pl.BlockSpec((nbatch, chunk, w), lambda c: (0, c, 0)),
        out_shape=jax.ShapeDtypeStruct((nbatch, seq, w), BF16),
        scratch_shapes=[pltpu.VMEM(s, d) for s, d in scratch],
        compiler_params=_params(("arbitrary",), blocks, scratch),
        name="mlstm",
    )(z3, z3, z3, z3, zif3, gb, norm_g.reshape(1, w))


def _rope_tables(seq):
    half = ROPE_DIM // 2
    inv_freq = ROPE_THETA ** (-jnp.arange(0, ROPE_DIM, 2, dtype=F32) / ROPE_DIM)
    ang = jnp.arange(seq, dtype=F32)[:, None] * inv_freq[None, :]
    cos = jnp.cos(ang)
    sin = jnp.sin(ang)
    ones = jnp.ones((seq, HEAD_DIM - ROPE_DIM), F32)
    cos_tab = jnp.concatenate([cos, cos, ones], axis=1)
    sin_tab = jnp.concatenate([-sin, sin, 0.0 * ones], axis=1)
    assert cos_tab.shape == (seq, HEAD_DIM) and half * 2 == ROPE_DIM
    return cos_tab, sin_tab


def _rotary(t, cos, sin):
    half = ROPE_DIM // 2
    lane = lax.broadcasted_iota(jnp.int32, t.shape, 1)
    upper = pltpu.roll(t, HEAD_DIM - half, axis=1)
    lower = pltpu.roll(t, half, axis=1)
    partner = jnp.where(lane < half, upper, lower)
    return jnp.where(lane < ROPE_DIM, t * cos + partner * sin, t)


def _rope_tables_t(seq):
    inv_freq = ROPE_THETA ** (-jnp.arange(0, ROPE_DIM, 2, dtype=F32) / ROPE_DIM)
    ang = jnp.arange(seq, dtype=F32)[:, None] * inv_freq[None, :]
    cos = jnp.cos(ang).T
    sin = jnp.sin(ang).T
    return jnp.concatenate([cos, cos], axis=0), jnp.concatenate([-sin, sin], axis=0)


def _moba_prep_t_kernel(q_ref, k_ref, v_ref, cos_ref, sin_ref, cost_ref, sint_ref,
                        qa_ref, kb_ref, vt_ref, bias_ref, km_sc):
    blk = q_ref.shape[1]
    d = HEAD_DIM
    half = ROPE_DIM // 2
    nsel = km_sc.shape[1]
    j = pl.program_id(1)

    @pl.when(j == 0)
    def _():
        km_sc[...] = jnp.zeros_like(km_sc)

    cos = cos_ref[...]
    sin = sin_ref[...]
    cos_t = cost_ref[...]
    sin_t = sint_ref[...]
    blk_id = lax.broadcasted_iota(jnp.int32, (nsel, blk), 0)
    blk_id_f = blk_id.astype(F32)
    mean_row = lax.broadcasted_iota(jnp.int32, (nsel, d), 0)
    for h in range(MOBA_HEADS):
        sl = slice(h * d, (h + 1) * d)
        q_t = q_ref[0, :, sl].T
        top = q_t[:ROPE_DIM]
        partner = jnp.concatenate([top[half:], top[:half]], axis=0)
        q_t = jnp.concatenate([top * cos_t + partner * sin_t, q_t[ROPE_DIM:]], axis=0)

        gate = jnp.dot(km_sc[h], q_t, preferred_element_type=F32,
                       precision=lax.Precision.HIGHEST)
        gate = jnp.where(blk_id < j, gate, -jnp.inf)
        sel_m1 = jnp.full((nsel, blk), -1.0, F32)
        for _ in range(MOBA_TOPK):
            mx = jnp.max(gate, axis=0, keepdims=True)
            first = jnp.min(jnp.where(gate == mx, blk_id_f, float(nsel)), axis=0, keepdims=True)
            first = jnp.where(mx > -jnp.inf, first, -1.0)
            hit = blk_id_f == first
            sel_m1 = jnp.where(hit, 0.0, sel_m1)
            gate = jnp.where(hit, -jnp.inf, gate)
        qa_ref[0, h, 0] = q_t.astype(BF16)
        for g in range(bias_ref.shape[3] // BIAS_ROWS):
            rows = sel_m1[g * MOBA_GROUP:(g + 1) * MOBA_GROUP] * MASK_BIG
            pad = jnp.zeros((BIAS_ROWS - MOBA_GROUP, blk), F32)
            bias_ref[0, h, 0, g * BIAS_ROWS:(g + 1) * BIAS_ROWS, :] = jnp.concatenate([rows, pad], axis=0)

        kk = _rotary(k_ref[0, :, sl], cos, sin)
        kb_ref[0, :, sl] = kk.astype(BF16)
        km_sc[h] = jnp.where(mean_row == j, jnp.mean(kk, axis=0, keepdims=True), km_sc[h])

        vt_ref[0, h, 0] = v_ref[0, :, sl].T.astype(BF16)


def _moba_prep_t(z3, tables, q_cb, k_cb, v_cb):
    nbatch, seq, _ = z3.shape
    blk = MOBA_BLOCK
    nblk = _tiles(seq, blk)
    nsel = -(-nblk // BIAS_ROWS) * BIAS_ROWS
    bias_rows = _tiles(nblk, MOBA_GROUP) * BIAS_ROWS
    w = MOBA_WIDTH
    d = HEAD_DIM
    hh = MOBA_HEADS
    cos_tab, sin_tab, cos_t, sin_t = tables
    blocks = ([((1, blk, w), F32)] * 3 + [((blk, d), F32)] * 2 + [((ROPE_DIM, blk), F32)] * 2
              + [((hh, d, blk), BF16), ((1, blk, w), BF16), ((hh, d, blk), BF16), ((hh, bias_rows, blk), F32)])
    scratch = [((hh, nsel, d), F32)]
    return pl.pallas_call(
        _moba_prep_t_kernel,
        grid=(nbatch, nblk),
        in_specs=[pl.BlockSpec((1, blk, w), lambda b, j: (b, j, q_cb)),
                  pl.BlockSpec((1, blk, w), lambda b, j: (b, j, k_cb)),
                  pl.BlockSpec((1, blk, w), lambda b, j: (b, j, v_cb)),
                  pl.BlockSpec((blk, d), lambda b, j: (j, 0)),
                  pl.BlockSpec((blk, d), lambda b, j: (j, 0)),
                  pl.BlockSpec((ROPE_DIM, blk), lambda b, j: (0, j)),
                  pl.BlockSpec((ROPE_DIM, blk), lambda b, j: (0, j))],
        out_specs=[pl.BlockSpec((1, hh, 1, d, blk), lambda b, j: (b, 0, j, 0, 0)),
                   pl.BlockSpec((1, blk, w), lambda b, j: (b, j, 0)),
                   pl.BlockSpec((1, hh, 1, d, blk), lambda b, j: (b, 0, j, 0, 0)),
                   pl.BlockSpec((1, hh, 1, bias_rows, blk), lambda b, j: (b, 0, j, 0, 0))],
        out_shape=[jax.ShapeDtypeStruct((nbatch, hh, nblk, d, blk), BF16),
                   jax.ShapeDtypeStruct((nbatch, seq, w), BF16),
                   jax.ShapeDtypeStruct((nbatch, hh, nblk, d, blk), BF16),
                   jax.ShapeDtypeStruct((nbatch, hh, nblk, bias_rows, blk), F32)],
        scratch_shapes=[pltpu.VMEM(s, dt) for s, dt in scratch],
        compiler_params=_params(("arbitrary", "arbitrary"), blocks, scratch),
        name="moba_prep",
    )(z3, z3, z3, cos_tab, sin_tab, cos_t, sin_t)


def _pipelined(start, count, first, step):
    lo = start % 2

    def pair(t, carry):
        i = start + 2 * t
        return step(i + 1, 1 - lo, step(i, lo, carry))

    carry = lax.fori_loop(0, (count - start) // 2, pair, first)
    if (count - start) % 2:
        carry = step(count - 1, (count - 1) % 2, carry)
    return carry


def _moba_flat_kernel(tile_ref, group_ref, q_ref, k_ref, bias_ref, vt_ref, o_ref,
                      s_sc, p_sc, acc_sc, m_sc, l_sc):
    nblk, d, blk = vt_ref.shape[2:]
    group = MOBA_GROUP
    gkeys = group * blk
    n_items = tile_ref.shape[0]
    c_exp = (d ** -0.5) * np.log2(np.e).astype(np.float32)
    key = lax.broadcasted_iota(jnp.int32, (blk, blk), 0)
    qry = lax.broadcasted_iota(jnp.int32, (blk, blk), 1)

    def weighted_values(first_blk, p):
        acc = None
        for i in range(p.shape[0] // blk):
            part = jnp.dot(vt_ref[0, 0, first_blk + i], p[i * blk:(i + 1) * blk],
                           preferred_element_type=F32)
            acc = part if acc is None else acc + part
        return acc

    def own_scores(j, slot):
        r0 = pl.multiple_of(j * blk, blk)
        s_t = jnp.dot(k_ref[0, pl.ds(r0, blk), :], q_ref[0, 0, j], preferred_element_type=F32)
        s_t = jnp.where(key <= qry, s_t, -jnp.inf)
        s_sc[slot, :blk, :] = s_t
        return jnp.max(s_t, axis=0, keepdims=True)

    def own_softmax(j, slot, s_max):
        next_max = own_scores(jnp.minimum(j + 1, nblk - 1), 1 - slot)
        p = jnp.exp2((s_sc[slot, :blk, :] - s_max) * c_exp)
        m_sc[j] = s_max
        l_sc[j] = jnp.sum(p, axis=0, keepdims=True)
        p_sc[slot] = p.astype(BF16)
        return next_max

    def own_values(j, slot):
        acc_sc[j] = weighted_values(j, p_sc[slot])

    def own_step(j, slot, s_max):
        own_values(j - 1, 1 - slot)
        return own_softmax(j, slot, s_max)

    _pipelined(1, nblk, own_softmax(0, 0, own_scores(0, 0)), own_step)
    own_values(nblk - 1, (nblk - 1) % 2)

    def group_scores(i, slot):
        j = tile_ref[i]
        g = group_ref[i]
        c0 = pl.multiple_of(g * gkeys, gkeys)
        s_t = jnp.dot(k_ref[0, pl.ds(c0, gkeys), :], q_ref[0, 0, j], preferred_element_type=F32)
        bias = bias_ref[0, 0, j, pl.ds(pl.multiple_of(g * BIAS_ROWS, BIAS_ROWS), BIAS_ROWS), :]
        s_t = jnp.concatenate([s_t[n * blk:(n + 1) * blk] + bias[n:n + 1] for n in range(group)],
                              axis=0)
        s_sc[slot] = s_t
        return jnp.max(s_t, axis=0, keepdims=True)

    def group_step(i, slot, s_max):
        next_max = group_scores(jnp.minimum(i + 1, n_items - 1), 1 - slot)
        j = tile_ref[i]
        m_run = m_sc[j]
        m_new = jnp.maximum(m_run, s_max)
        alpha = jnp.exp2((m_run - m_new) * c_exp)
        p = jnp.exp2((s_sc[slot] - m_new) * c_exp)
        m_sc[j] = m_new
        l_sc[j] = alpha * l_sc[j] + jnp.sum(p, axis=0, keepdims=True)
        acc_sc[j] = alpha * acc_sc[j] + weighted_values(group_ref[i] * group, p.astype(BF16))
        return next_max

    _pipelined(0, n_items, group_scores(0, 0), group_step)

    def finish(j, carry):
        r0 = pl.multiple_of(j * blk, blk)
        o_ref[0, pl.ds(r0, blk), :] = (acc_sc[j] / l_sc[j]).T.astype(o_ref.dtype)
        return carry

    lax.fori_loop(0, nblk, finish, 0)


def _moba_flat(q_t, kb, vt, bias):
    nbatch, seq, w = kb.shape
    blk = MOBA_BLOCK
    d = HEAD_DIM
    nblk = vt.shape[2]
    group = MOBA_GROUP
    bias_rows = bias.shape[3]
    items = [(j, g) for j in range(nblk) for g in range(-(-j // group))]
    assert items and _tiles(nblk, group)
    item_tile = jnp.asarray([j for j, _ in items], jnp.int32)
    item_group = jnp.asarray([g for _, g in items], jnp.int32)
    blocks = [((nblk, d, blk), BF16), ((1, seq, d), BF16), ((nblk, bias_rows, blk), F32), ((nblk, d, blk), BF16),
              ((1, seq, d), BF16)]
    scratch = [((2, group * blk, blk), F32), ((2, blk, blk), BF16), ((nblk, d, blk), F32),
               ((nblk, 1, blk), F32), ((nblk, 1, blk), F32)]
    grid_spec = pltpu.PrefetchScalarGridSpec(
        num_scalar_prefetch=2,
        grid=(nbatch, MOBA_HEADS),
        in_specs=[pl.BlockSpec((1, 1, nblk, d, blk), lambda b, h, *_: (b, h, 0, 0, 0)),
                  pl.BlockSpec((1, seq, d), lambda b, h, *_: (b, 0, h)),
                  pl.BlockSpec((1, 1, nblk, bias_rows, blk), lambda b, h, *_: (b, h, 0, 0, 0)),
                  pl.BlockSpec((1, 1, nblk, d, blk), lambda b, h, *_: (b, h, 0, 0, 0))],
        out_specs=pl.BlockSpec((1, seq, d), lambda b, h, *_: (b, 0, h)),
        scratch_shapes=[pltpu.VMEM(s, dt) for s, dt in scratch])
    return pl.pallas_call(
        _moba_flat_kernel,
        grid_spec=grid_spec,
        out_shape=jax.ShapeDtypeStruct((nbatch, seq, w), BF16),
        compiler_params=_params(("arbitrary", "arbitrary"), blocks, scratch),
        name="moba_attn",
    )(item_tile, item_group, q_t, kb, bias, vt)


def _gelu_tanh(x):
    c = np.sqrt(2.0 / np.pi).astype(np.float32)
    return x * (0.5 * (1.0 + jnp.tanh(c * (x + 0.044715 * (x * x * x)))))


def _gmlp_kernel(u_ref, v_ref, lg_ref, lb_ref, ws_ref, bst_ref, y_ref):
    rows = u_ref.shape[1]
    t = GMLP_CHUNK
    gd = GMLP_WIDTH // GMLP_GROUPS
    v = _gelu_tanh(v_ref[0])
    mu = jnp.mean(v, axis=-1, keepdims=True)
    vc = v - mu
    vln = vc * lax.rsqrt(jnp.mean(vc * vc, axis=-1, keepdims=True) + NORM_EPS) * lg_ref[...] + lb_ref[...]
    vb = vln.astype(BF16)
    row = lax.broadcasted_iota(jnp.int32, (t, t), 0)
    col = lax.broadcasted_iota(jnp.int32, (t, t), 1)
    for g in range(GMLP_GROUPS):
        wg = jnp.where(col <= row, ws_ref[g], 0.0).astype(BF16)
        bias = bst_ref[:, g:g + 1]
        cols = slice(g * gd, (g + 1) * gd)
        for c in range(rows // t):
            rs = slice(c * t, (c + 1) * t)
            mixed = jnp.dot(wg, vb[rs, cols], preferred_element_type=F32) + bias
            y_ref[0, rs, cols] = (_gelu_tanh(u_ref[0, rs, cols]) * mixed).astype(y_ref.dtype)


def _gmlp(z3, u_cb, v_cb, ln_g, ln_b, ws, bs, rows=512):
    nbatch, seq, _ = z3.shape
    w = GMLP_WIDTH
    t = GMLP_CHUNK
    blocks = [((1, rows, w), F32), ((1, rows, w), F32), ((GMLP_GROUPS, t, t), F32), ((1, rows, w), BF16)]
    return pl.pallas_call(
        _gmlp_kernel,
        grid=(nbatch, _tiles(seq, rows)),
        in_specs=[pl.BlockSpec((1, rows, w), lambda b, c: (b, c, u_cb)),
                  pl.BlockSpec((1, rows, w), lambda b, c: (b, c, v_cb)),
                  pl.BlockSpec((1, w), lambda b, c: (0, 0)),
                  pl.BlockSpec((1, w), lambda b, c: (0, 0)),
                  pl.BlockSpec((GMLP_GROUPS, t, t), lambda b, c: (0, 0, 0)),
                  pl.BlockSpec((t, GMLP_GROUPS), lambda b, c: (0, 0))],
        out_specs=pl.BlockSpec((1, rows, w), lambda b, c: (b, c, 0)),
        out_shape=jax.ShapeDtypeStruct((nbatch, seq, w), BF16),
        compiler_params=_params(("arbitrary", "arbitrary"), blocks),
        name="gmlp",
    )(z3, z3, ln_g.reshape(1, w), ln_b.reshape(1, w), ws, bs.T)


def _layer_weight_spec(layer, k, tn, index_of):
    return pl.BlockSpec((None, k, tn), lambda *idx: (layer,) + index_of(*idx))


def _merge_kernel(h_ref, wga_ref, wgb_ref, wgc_ref, ya_ref, yb_ref, yc_ref, wa_ref, wb_ref, wc_ref, o_ref):
    h = h_ref[...]

    def branch(wg_ref, y_ref, w_ref):
        gate = jax.nn.sigmoid(jnp.dot(h, wg_ref[...], preferred_element_type=F32))
        return gate * jnp.dot(y_ref[...], w_ref[...].astype(BF16), preferred_element_type=F32)

    merged = branch(wga_ref, ya_ref, wa_ref) + branch(wgb_ref, yb_ref, wb_ref) + branch(wgc_ref, yc_ref, wc_ref)
    o_ref[...] = merged.astype(o_ref.dtype)


def _merge(h, w_gate, ya, yb, yc, wa, wb, wc, layer, tm=1024, tn=512):
    m, d = h.shape
    nblk = _tiles(d, tn)
    ka, kb, kc = ya.shape[1], yb.shape[1], yc.shape[1]
    blocks = ([((tm, d), BF16)] + [((d, tn), BF16)] * 3
              + [((tm, ka), BF16), ((tm, kb), BF16), ((tm, kc), BF16)]
              + [((ka, tn), F32), ((kb, tn), F32), ((kc, tn), F32), ((tm, tn), BF16)])

    def gate_spec(branch):
        return pl.BlockSpec((None, d, tn), lambda i, j: (layer, 0, branch * nblk + j))

    def col(i, j):
        return (0, j)

    return pl.pallas_call(
        _merge_kernel,
        grid=(_tiles(m, tm), nblk),
        in_specs=[pl.BlockSpec((tm, d), lambda i, j: (i, 0)),
                  gate_spec(0), gate_spec(1), gate_spec(2),
                  pl.BlockSpec((tm, ka), lambda i, j: (i, 0)),
                  pl.BlockSpec((tm, kb), lambda i, j: (i, 0)),
                  pl.BlockSpec((tm, kc), lambda i, j: (i, 0)),
                  _layer_weight_spec(layer, ka, tn, col),
                  _layer_weight_spec(layer, kb, tn, col),
                  _layer_weight_spec(layer, kc, tn, col)],
        out_specs=pl.BlockSpec((tm, tn), lambda i, j: (i, j)),
        out_shape=jax.ShapeDtypeStruct((m, d), BF16),
        compiler_params=_params(("arbitrary", "arbitrary"), blocks),
        name="merge",
    )(h, w_gate, w_gate, w_gate, ya, yb, yc, wa, wb, wc)


def _residual_matmul_kernel(a_ref, w_ref, r_ref, o_ref):
    @pl.when(pl.program_id(2) == 0)
    def _():
        o_ref[...] = r_ref[...]

    o_ref[...] += jnp.dot(a_ref[...], w_ref[...].astype(BF16), preferred_element_type=F32)


def _residual_matmul(a, w, layer, res, tm=1024, tn=1024, tk=1024):
    m, k = a.shape
    n = w.shape[2]
    blocks = [((tm, tk), BF16), ((tk, tn), F32), ((tm, tn), F32), ((tm, tn), F32)]
    return pl.pallas_call(
        _residual_matmul_kernel,
        grid=(_tiles(m, tm), _tiles(n, tn), _tiles(k, tk)),
        in_specs=[pl.BlockSpec((tm, tk), lambda i, j, kk: (i, kk)),
                  _layer_weight_spec(layer, tk, tn, lambda i, j, kk: (kk, j)),
                  pl.BlockSpec((tm, tn), lambda i, j, kk: (i, j))],
        out_specs=pl.BlockSpec((tm, tn), lambda i, j, kk: (i, j)),
        out_shape=jax.ShapeDtypeStruct((m, n), F32),
        compiler_params=_params(("arbitrary", "arbitrary", "arbitrary"), blocks),
        name="residual_matmul",
    )(a, w, res)


def _out_proj_kernel(a_ref, w_ref, r_ref, g_ref, x_ref, h_ref):
    x_ref[...] = r_ref[...] + jnp.dot(a_ref[...], w_ref[...].astype(BF16), preferred_element_type=F32)
    _rmsnorm_rows(x_ref, g_ref, h_ref)


def _out_proj(a, w, layer, res, g, tm=512):
    m, k = a.shape
    d = w.shape[2]
    blocks = [((tm, k), BF16), ((k, d), w.dtype), ((tm, d), F32), ((tm, d), F32), ((tm, d), BF16)]
    return pl.pallas_call(
        _out_proj_kernel,
        grid=(_tiles(m, tm),),
        in_specs=[pl.BlockSpec((tm, k), lambda i: (i, 0)),
                  _layer_weight_spec(layer, k, d, lambda i: (0, 0)),
                  pl.BlockSpec((tm, d), lambda i: (i, 0)),
                  pl.BlockSpec((1, d), lambda i: (0, 0))],
        out_specs=[pl.BlockSpec((tm, d), lambda i: (i, 0)),
                   pl.BlockSpec((tm, d), lambda i: (i, 0))],
        out_shape=[jax.ShapeDtypeStruct((m, d), F32),
                   jax.ShapeDtypeStruct((m, d), BF16)],
        compiler_params=_params(("arbitrary",), blocks),
        name="out_proj",
    )(a, w, res, g.reshape(1, d))


def _relu2_matmul_kernel(h_ref, w_ref, o_ref):
    up = jnp.maximum(jnp.dot(h_ref[...], w_ref[...].astype(BF16), preferred_element_type=F32), 0.0)
    o_ref[...] = (up * up).astype(o_ref.dtype)


def _relu2_matmul(h, w, layer, tm=2048, tn=1024):
    m, k = h.shape
    n = w.shape[2]
    blocks = [((tm, k), BF16), ((k, tn), F32), ((tm, tn), BF16)]
    return pl.pallas_call(
        _relu2_matmul_kernel,
        grid=(_tiles(m, tm), _tiles(n, tn)),
        in_specs=[pl.BlockSpec((tm, k), lambda i, j: (i, 0)),
                  _layer_weight_spec(layer, k, tn, lambda i, j: (0, j))],
        out_specs=pl.BlockSpec((tm, tn), lambda i, j: (i, j)),
        out_shape=jax.ShapeDtypeStruct((m, n), BF16),
        compiler_params=_params(("arbitrary", "arbitrary"), blocks),
        name="mlp_up",
    )(h, w)


def _cast_kernel(w_ref, o_ref):
    o_ref[...] = w_ref[...].astype(o_ref.dtype)


def _cast_bf16(w, rows=512):
    depth, k, n = w.shape
    spec = pl.BlockSpec((None, rows, n), lambda l, r: (l, r, 0))
    return pl.pallas_call(
        _cast_kernel,
        grid=(depth, _tiles(k, rows)),
        in_specs=[spec],
        out_specs=spec,
        out_shape=jax.ShapeDtypeStruct(w.shape, BF16),
        compiler_params=_params(("arbitrary", "arbitrary"), [((rows, n), F32), ((rows, n), BF16)]),
        name="cast_bf16",
    )(w)


def _ple_kernel(x_ref, g_ref, wg_ref, p_ref, wp_ref, fg_ref, o_ref, h_sc, *, final_norm):
    _rmsnorm_rows(x_ref, g_ref, h_sc)
    gate = jax.nn.sigmoid(jnp.dot(h_sc[...], wg_ref[...].astype(BF16), preferred_element_type=F32))
    emb = jnp.dot(p_ref[...].astype(BF16), wp_ref[...].astype(BF16), preferred_element_type=F32)
    out = x_ref[...] + gate * emb
    if final_norm:
        out = out * lax.rsqrt(jnp.mean(out * out, axis=-1, keepdims=True) + NORM_EPS) * fg_ref[...]
    o_ref[...] = out


def _ple(x, g, w_gate, p, w_proj, layer, final_g, final_norm, tm=512):
    m, d = x.shape
    pd = p.shape[2]
    blocks = [((tm, d), F32), ((d, d), w_gate.dtype), ((tm, pd), F32), ((pd, d), F32), ((tm, d), F32)]
    scratch = [((tm, d), BF16)]
    return pl.pallas_call(
        functools.partial(_ple_kernel, final_norm=final_norm),
        grid=(_tiles(m, tm),),
        in_specs=[pl.BlockSpec((tm, d), lambda i: (i, 0)),
                  pl.BlockSpec((1, d), lambda i: (0, 0)),
                  _layer_weight_spec(layer, d, d, lambda i: (0, 0)),
                  pl.BlockSpec((None, tm, pd), lambda i: (layer, i, 0)),
                  _layer_weight_spec(layer, pd, d, lambda i: (0, 0)),
                  pl.BlockSpec((1, d), lambda i: (0, 0))],
        out_specs=pl.BlockSpec((tm, d), lambda i: (i, 0)),
        out_shape=jax.ShapeDtypeStruct((m, d), F32),
        scratch_shapes=[pltpu.VMEM(s, dt) for s, dt in scratch],
        compiler_params=_params(("arbitrary",), blocks, scratch),
        name="ple",
    )(x, g.reshape(1, d), w_gate, p, w_proj, final_g.reshape(1, d))


def kernel(x, p, norm_mix_g, w_in, mlstm_gate_b, mlstm_norm_g, gmlp_norm_g, gmlp_norm_b, gmlp_ws, gmlp_bs,
           w_branch_a, w_branch_b, w_branch_c, w_out, norm_mlp_g, w_mlp_up, w_mlp_down, norm_ple_g,
           w_ple_gate, w_ple_proj, final_norm_g):
    nbatch, seq, d = x.shape
    depth = w_in.shape[0]
    m = nbatch * seq
    assert d == MLSTM_WIDTH + MOBA_WIDTH + GMLP_WIDTH

    qkvo_a = 4 * MLSTM_WIDTH
    gates_if = 2 * MLSTM_HEADS
    main_b = qkvo_a + gates_if
    main_cols = 3 * MOBA_WIDTH + 2 * GMLP_WIDTH
    gate_off = main_b + main_cols
    assert w_in.shape[2] == gate_off + N_BRANCHES * d

    moba_q_cb = qkvo_a // MOBA_WIDTH
    moba_k_cb = moba_q_cb + 1
    moba_v_cb = moba_k_cb + 1
    gmlp_u_cb = (qkvo_a + 3 * MOBA_WIDTH) // GMLP_WIDTH
    gmlp_v_cb = gmlp_u_cb + 1

    z_cols = qkvo_a + main_cols
    colscale = jnp.ones((1, z_cols), F32).at[:, MLSTM_WIDTH:2 * MLSTM_WIDTH].set(HEAD_DIM ** -0.5)
    rope_tables = _rope_tables(seq) + _rope_tables_t(seq)

    xf = x.reshape(m, d)
    p_flat = p.reshape(depth, m, p.shape[-1])
    w_main, w_if, w_gate = _win_split(w_in, qkvo_a, gates_if, z_cols)
    w_ple_gate_b = _cast_bf16(w_ple_gate)
    w_out_b = _cast_bf16(w_out)
    for i in range(depth):
        z, zif, h = _inproj(xf, norm_mix_g[i], w_main, w_if, i, colscale)
        z3 = z.reshape(nbatch, seq, z_cols)
        zif3 = zif.reshape(nbatch, seq, GATE_LANES)

        ya = _mlstm(z3, zif3, mlstm_gate_b[i], mlstm_norm_g[i])
        yb = _moba_flat(*_moba_prep_t(z3, rope_tables, moba_q_cb, moba_k_cb, moba_v_cb))
        yc = _gmlp(z3, gmlp_u_cb, gmlp_v_cb, gmlp_norm_g[i], gmlp_norm_b[i], gmlp_ws[i], gmlp_bs[i])

        merged = _merge(h, w_gate, ya.reshape(m, -1), yb.reshape(m, -1), yc.reshape(m, -1),
                        w_branch_a, w_branch_b, w_branch_c, i)
        xf, h2 = _out_proj(merged, w_out_b, i, xf, norm_mlp_g[i])
        hidden = _relu2_matmul(h2, w_mlp_up, i)
        xf = _residual_matmul(hidden, w_mlp_down, i, xf, tm=2048)
        xf = _ple(xf, norm_ple_g[i], w_ple_gate_b, p_flat, w_ple_proj, i, final_norm_g, i == depth - 1)

    return xf.reshape(nbatch, seq, d)
```

```python
import functools

import jax
import jax.numpy as jnp
import numpy as np
from jax import lax
from jax.experimental import pallas as pl
from jax.experimental.pallas import tpu as pltpu

F32 = jnp.float32
BF16 = jnp.bfloat16

HEAD_DIM = 128
MLSTM_HEADS = 4
MLSTM_WIDTH = MLSTM_HEADS * HEAD_DIM
MOBA_HEADS = 8
MOBA_WIDTH = MOBA_HEADS * HEAD_DIM
MOBA_BLOCK = 256
MOBA_TOPK = 3
ROPE_THETA = 500000.0
ROPE_DIM = HEAD_DIM // 4
GMLP_WIDTH = 512
GMLP_GROUPS = 4
GMLP_CHUNK = 128
N_BRANCHES = 3
NORM_EPS = 1e-6

LANES = 128
V7X_VMEM_BYTES = 64 * 1024 * 1024
VMEM_CEILING = V7X_VMEM_BYTES - 8 * 1024 * 1024

MLSTM_KERNEL_CHUNK = 128
MLSTM_CHUNKS_PER_STEP = 4
STATE_ROWS = 16
GATE_LANES = LANES
MASK_BIG = 2.0 ** 100
BIAS_ROWS = 8
MOBA_GROUP = 4

NT_DIMS = (((1,), (1,)), ((), ()))


def _tiles(n, t):
    count, rest = divmod(n, t)
    assert rest == 0 and count > 0, (n, t)
    return count


def _nbytes(shape, dtype):
    return int(np.prod(shape)) * jnp.dtype(dtype).itemsize


def _params(semantics, blocks, scratch=()):
    need = 2 * sum(_nbytes(s, d) for s, d in blocks) + sum(_nbytes(s, d) for s, d in scratch)
    limit = min(VMEM_CEILING, need + need // 4 + 4 * 1024 * 1024)
    return pltpu.CompilerParams(dimension_semantics=semantics, vmem_limit_bytes=limit)


def _win_split_kernel(wm_ref, wif_ref, wg_ref, main_ref, if_ref, gate_ref):
    main_ref[...] = wm_ref[0].T.astype(BF16)
    gate_ref[...] = wg_ref[0].T.astype(BF16)
    _, n_if, k = wif_ref.shape
    padded = jnp.concatenate([wif_ref[0], jnp.zeros((GATE_LANES - n_if, k), F32)], axis=0)
    if_ref[...] = padded.T.astype(BF16)


def _win_split(w_in, head, n_if, main_cols, tn=512):
    depth, k, cols = w_in.shape
    gate_cols = cols - main_cols - n_if
    assert gate_cols == main_cols and head % tn == 0
    w_t = jnp.swapaxes(w_in, 1, 2)
    head_steps = head // tn
    blocks = [((tn, k), F32), ((n_if, k), F32), ((tn, k), F32), ((k, tn), BF16), ((k, GATE_LANES), BF16),
              ((k, tn), BF16)]

    def main_rows(l, r):
        return (l, pl.multiple_of(r * tn + jnp.where(r >= head_steps, n_if, 0), n_if), 0)

    def window(rows, start_of):
        return pl.BlockSpec((pl.Element(1), pl.Element(rows), pl.Element(k)), start_of)

    return pl.pallas_call(
        _win_split_kernel,
        grid=(depth, _tiles(main_cols, tn)),
        in_specs=[window(tn, main_rows),
                  window(n_if, lambda l, r: (l, head, 0)),
                  window(tn, lambda l, r: (l, pl.multiple_of(main_cols + n_if + r * tn, n_if), 0))],
        out_specs=[pl.BlockSpec((None, k, tn), lambda l, r: (l, 0, r)),
                   pl.BlockSpec((None, k, GATE_LANES), lambda l, r: (l, 0, 0)),
                   pl.BlockSpec((None, k, tn), lambda l, r: (l, 0, r))],
        out_shape=[jax.ShapeDtypeStruct((depth, k, main_cols), BF16),
                   jax.ShapeDtypeStruct((depth, k, GATE_LANES), BF16),
                   jax.ShapeDtypeStruct((depth, k, gate_cols), BF16)],
        compiler_params=_params(("arbitrary", "arbitrary"), blocks),
        name="win_split",
    )(w_t, w_t, w_t)


def _rmsnorm_rows(x_ref, g_ref, h_ref, rows=256):
    for r0 in range(0, x_ref.shape[0], rows):
        x = x_ref[r0:r0 + rows, :]
        y = x * lax.rsqrt(jnp.mean(x * x, axis=-1, keepdims=True) + NORM_EPS)
        h_ref[r0:r0 + rows, :] = (y * g_ref[...]).astype(h_ref.dtype)


def _inproj_kernel(x_ref, g_ref, w_ref, wif_ref, cs_ref, z_ref, zif_ref, h_ref):
    @pl.when(pl.program_id(1) == 0)
    def _():
        _rmsnorm_rows(x_ref, g_ref, h_ref)
        zif_ref[...] = jnp.dot(h_ref[...], wif_ref[...], preferred_element_type=F32)

    acc = jnp.dot(h_ref[...], w_ref[...], preferred_element_type=F32)
    z_ref[...] = acc * cs_ref[...]


def _inproj(x, g, w_main, w_if, layer, colscale, tm=1024, tn=1024):
    m, k = x.shape
    n = w_main.shape[2]
    blocks = [((tm, k), F32), ((k, tn), BF16), ((k, GATE_LANES), BF16), ((1, tn), F32),
              ((tm, tn), F32), ((tm, GATE_LANES), F32), ((tm, k), BF16)]
    return pl.pallas_call(
        _inproj_kernel,
        grid=(_tiles(m, tm), _tiles(n, tn)),
        in_specs=[pl.BlockSpec((tm, k), lambda i, j: (i, 0)),
                  pl.BlockSpec((1, k), lambda i, j: (0, 0)),
                  pl.BlockSpec((None, k, tn), lambda i, j: (layer, 0, j)),
                  pl.BlockSpec((None, k, GATE_LANES), lambda i, j: (layer, 0, 0)),
                  pl.BlockSpec((1, tn), lambda i, j: (0, j))],
        out_specs=[pl.BlockSpec((tm, tn), lambda i, j: (i, j)),
                   pl.BlockSpec((tm, GATE_LANES), lambda i, j: (i, 0)),
                   pl.BlockSpec((tm, k), lambda i, j: (i, 0))],
        out_shape=[jax.ShapeDtypeStruct((m, n), F32),
                   jax.ShapeDtypeStruct((m, GATE_LANES), F32),
                   jax.ShapeDtypeStruct((m, k), BF16)],
        compiler_params=_params(("arbitrary", "arbitrary"), blocks),
        name="in_proj",
    )(x, g.reshape(1, k), w_main, w_if, colscale)


def _log_sigmoid(x):
    return jnp.minimum(x, 0.0) - jnp.log1p(jnp.exp(-jnp.abs(x)))


def _exact_tril_matmul(tril, x):
    tril = tril.astype(BF16)
    hi = x.astype(BF16)
    rest = x - hi.astype(F32)
    mid = rest.astype(BF16)
    lo = (rest - mid.astype(F32)).astype(BF16)
    return (jnp.dot(tril, hi, preferred_element_type=F32) + jnp.dot(tril, mid, preferred_element_type=F32)
            + jnp.dot(tril, lo, preferred_element_type=F32))


def _mlstm_kernel(q_ref, k_ref, v_ref, o_ref, zif_ref, gb_ref, ng_ref, y_ref, c_sc, n_sc, m_sc):
    @pl.when(pl.program_id(0) == 0)
    def _():
        c_sc[...] = jnp.zeros_like(c_sc)
        n_sc[...] = jnp.zeros_like(n_sc)
        m_sc[...] = jnp.zeros_like(m_sc)

    for c0 in range(0, q_ref.shape[1], MLSTM_KERNEL_CHUNK):
        rows = pl.ds(c0, MLSTM_KERNEL_CHUNK)
        _mlstm_chunk(q_ref.at[:, rows, :], k_ref.at[:, rows, :], v_ref.at[:, rows, :], o_ref.at[:, rows, :],
                     zif_ref.at[:, rows, :], gb_ref, ng_ref, y_ref.at[:, rows, :], c_sc, n_sc, m_sc)


def _mlstm_chunk(q_ref, k_ref, v_ref, o_ref, zif_ref, gb_ref, ng_ref, y_ref, c_sc, n_sc, m_sc):
    nbatch, chunk, _ = q_ref.shape
    heads = MLSTM_HEADS
    row = lax.broadcasted_iota(jnp.int32, (chunk, chunk), 0)
    col = lax.broadcasted_iota(jnp.int32, (chunk, chunk), 1)
    causal_t = row <= col
    tril = jnp.where(col <= row, 1.0, 0.0).astype(F32)
    lane = lax.broadcasted_iota(jnp.int32, (chunk, GATE_LANES), 1)

    for b in range(nbatch):
        pre = zif_ref[b] + gb_ref[...]
        gates = jnp.where(lane < heads, pre, _log_sigmoid(pre))
        gcum = _exact_tril_matmul(tril, gates)
        gates_t = gates.T
        gcum_t = gcum.T
        for h in range(heads):
            s = b * heads + h
            sl = slice(h * HEAD_DIM, (h + 1) * HEAD_DIM)
            q = q_ref[b, :, sl]
            k = k_ref[b, :, sl]
            v = v_ref[b, :, sl]
            q_t = q.T
            qb = q.astype(BF16)
            kb = k.astype(BF16)
            q_tb = q_t.astype(BF16)
            v_t = v.T
            g_row = gcum_t[heads + h:heads + h + 1, :]
            i_row = gates_t[h:h + 1, :]
            ig_col = gates[:, h:h + 1] - gcum[:, heads + h:heads + h + 1]
            m_prev = m_sc[s][:, 0:1]
            c_prev = c_sc[s]
            n_prev = n_sc[s]

            log_w = jnp.where(causal_t, g_row + ig_col, -jnp.inf)
            log_a = g_row + m_prev
            m_row = jnp.maximum(jnp.max(log_w, axis=0, keepdims=True), log_a)
            qk = lax.dot_general(kb, qb, NT_DIMS, preferred_element_type=F32) * jnp.exp(log_w - m_row)
            a = jnp.exp(log_a - m_row)
            num = (jnp.dot(v_t.astype(BF16), qk.astype(BF16), preferred_element_type=F32)
                   + a * jnp.dot(c_prev.astype(BF16), q_tb, preferred_element_type=F32))
            n_dot_q = jnp.dot(n_prev.astype(BF16), q_tb, preferred_element_type=F32)[0:1]
            den = jnp.sum(qk, axis=0, keepdims=True) + a * n_dot_q
            h_out = num / jnp.maximum(jnp.abs(den), jnp.exp(-m_row))

            g_last = g_row[:, chunk - 1:chunk]
            log_u = g_last - g_row + i_row
            m_new = jnp.maximum(g_last + m_prev, jnp.max(log_u, axis=1, keepdims=True))
            decay = jnp.exp(g_last + m_prev - m_new)
            u = jnp.exp(log_u - m_new)
            c_sc[s] = decay * c_prev + jnp.dot((v_t * u).astype(BF16), kb, preferred_element_type=F32)
            u_rows = jnp.broadcast_to(u, (STATE_ROWS, chunk)).astype(BF16)
            n_sc[s] = decay * n_prev + jnp.dot(u_rows, kb, preferred_element_type=F32)
            m_sc[s] = jnp.broadcast_to(m_new, (1, LANES))

            yn = (h_out * lax.rsqrt(jnp.mean(h_out * h_out, axis=0, keepdims=True) + NORM_EPS)).T
            y_ref[b, :, sl] = (jax.nn.sigmoid(o_ref[b, :, sl]) * (yn * ng_ref[:, sl])).astype(y_ref.dtype)


def _mlstm(z3, zif3, gate_b, norm_g):
    nbatch, seq, _ = z3.shape
    chunk = MLSTM_KERNEL_CHUNK * MLSTM_CHUNKS_PER_STEP
    w = MLSTM_WIDTH
    streams = nbatch * MLSTM_HEADS
    gb = jnp.pad(gate_b, (0, GATE_LANES - gate_b.shape[0])).reshape(1, GATE_LANES)
    blocks = [((nbatch, chunk, w), F32)] * 4 + [((nbatch, chunk, GATE_LANES), F32),
                                               ((nbatch, chunk, w), BF16)]
    scratch = [((streams, HEAD_DIM, HEAD_DIM), F32), ((streams, STATE_ROWS, LANES), F32),
               ((streams, 1, LANES), F32)]

    def zcol(cb):
        return pl.BlockSpec((nbatch, chunk, w), lambda c: (0, c, cb))

    return pl.pallas_call(
        _mlstm_kernel,
        grid=(_tiles(seq, chunk),),
        in_specs=[zcol(0), zcol(1), zcol(2), zcol(3),
                  pl.BlockSpec((nbatch, chunk, GATE_LANES), lambda c: (0, c, 0)),
                  pl.BlockSpec((1, GATE_LANES), lambda c: (0, 0)),
                  pl.BlockSpec((1, w), lambda c: (0, 0))],
        out_specs=pl.BlockSpec((nbatch, chunk, w), lambda c: (0, c, 0)),
        out_shape=jax.ShapeDtypeStruct((nbatch, seq, w), BF16),
        scratch_shapes=[pltpu.VMEM(s, d) for s, d in scratch],
        compiler_params=_params(("arbitrary",), blocks, scratch),
        name="mlstm",
    )(z3, z3, z3, z3, zif3, gb, norm_g.reshape(1, w))


def _rope_tables(seq):
    half = ROPE_DIM // 2
    inv_freq = ROPE_THETA ** (-jnp.arange(0, ROPE_DIM, 2, dtype=F32) / ROPE_DIM)
    ang = jnp.arange(seq, dtype=F32)[:, None] * inv_freq[None, :]
    cos = jnp.cos(ang)
    sin = jnp.sin(ang)
    ones = jnp.ones((seq, HEAD_DIM - ROPE_DIM), F32)
    cos_tab = jnp.concatenate([cos, cos, ones], axis=1)
    sin_tab = jnp.concatenate([-sin, sin, 0.0 * ones], axis=1)
    assert cos_tab.shape == (seq, HEAD_DIM) and half * 2 == ROPE_DIM
    return cos_tab, sin_tab


def _rotary(t, cos, sin):
    half = ROPE_DIM // 2
    lane = lax.broadcasted_iota(jnp.int32, t.shape, 1)
    upper = pltpu.roll(t, HEAD_DIM - half, axis=1)
    lower = pltpu.roll(t, half, axis=1)
    partner = jnp.where(lane < half, upper, lower)
    return jnp.where(lane < ROPE_DIM, t * cos + partner * sin, t)


def _rope_tables_t(seq):
    inv_freq = ROPE_THETA ** (-jnp.arange(0, ROPE_DIM, 2, dtype=F32) / ROPE_DIM)
    ang = jnp.arange(seq, dtype=F32)[:, None] * inv_freq[None, :]
    cos = jnp.cos(ang).T
    sin = jnp.sin(ang).T
    return jnp.concatenate([cos, cos], axis=0), jnp.concatenate([-sin, sin], axis=0)


def _moba_prep_t_kernel(q_ref, k_ref, v_ref, cos_ref, sin_ref, cost_ref, sint_ref,
                        qa_ref, kb_ref, vt_ref, bias_ref, km_sc):
    blk = q_ref.shape[1]
    d = HEAD_DIM
    half = ROPE_DIM // 2
    nsel = km_sc.shape[1]
    j = pl.program_id(1)

    @pl.when(j == 0)
    def _():
        km_sc[...] = jnp.zeros_like(km_sc)

    cos = cos_ref[...]
    sin = sin_ref[...]
    cos_t = cost_ref[...]
    sin_t = sint_ref[...]
    blk_id = lax.broadcasted_iota(jnp.int32, (nsel, blk), 0)
    blk_id_f = blk_id.astype(F32)
    mean_row = lax.broadcasted_iota(jnp.int32, (nsel, d), 0)
    for h in range(MOBA_HEADS):
        sl = slice(h * d, (h + 1) * d)
        q_t = q_ref[0, :, sl].T
        top = q_t[:ROPE_DIM]
        partner = jnp.concatenate([top[half:], top[:half]], axis=0)
        q_t = jnp.concatenate([top * cos_t + partner * sin_t, q_t[ROPE_DIM:]], axis=0)

        gate = jnp.dot(km_sc[h], q_t, preferred_element_type=F32,
                       precision=lax.Precision.HIGHEST)
        gate = jnp.where(blk_id < j, gate, -jnp.inf)
        sel_m1 = jnp.full((nsel, blk), -1.0, F32)
        for _ in range(MOBA_TOPK):
            mx = jnp.max(gate, axis=0, keepdims=True)
            first = jnp.min(jnp.where(gate == mx, blk_id_f, float(nsel)), axis=0, keepdims=True)
            first = jnp.where(mx > -jnp.inf, first, -1.0)
            hit = blk_id_f == first
            sel_m1 = jnp.where(hit, 0.0, sel_m1)
            gate = jnp.where(hit, -jnp.inf, gate)
        qa_ref[0, h, 0] = q_t.astype(BF16)
        for g in range(bias_ref.shape[3] // BIAS_ROWS):
            rows = sel_m1[g * MOBA_GROUP:(g + 1) * MOBA_GROUP] * MASK_BIG
            pad = jnp.zeros((BIAS_ROWS - MOBA_GROUP, blk), F32)
            bias_ref[0, h, 0, g * BIAS_ROWS:(g + 1) * BIAS_ROWS, :] = jnp.concatenate([rows, pad], axis=0)

        kk = _rotary(k_ref[0, :, sl], cos, sin)
        kb_ref[0, :, sl] = kk.astype(BF16)
        km_sc[h] = jnp.where(mean_row == j, jnp.mean(kk, axis=0, keepdims=True), km_sc[h])

        vt_ref[0, h, 0] = v_ref[0, :, sl].T.astype(BF16)


def _moba_prep_t(z3, tables, q_cb, k_cb, v_cb):
    nbatch, seq, _ = z3.shape
    blk = MOBA_BLOCK
    nblk = _tiles(seq, blk)
    nsel = -(-nblk // BIAS_ROWS) * BIAS_ROWS
    bias_rows = _tiles(nblk, MOBA_GROUP) * BIAS_ROWS
    w = MOBA_WIDTH
    d = HEAD_DIM
    hh = MOBA_HEADS
    cos_tab, sin_tab, cos_t, sin_t = tables
    blocks = ([((1, blk, w), F32)] * 3 + [((blk, d), F32)] * 2 + [((ROPE_DIM, blk), F32)] * 2
              + [((hh, d, blk), BF16), ((1, blk, w), BF16), ((hh, d, blk), BF16), ((hh, bias_rows, blk), F32)])
    scratch = [((hh, nsel, d), F32)]
    return pl.pallas_call(
        _moba_prep_t_kernel,
        grid=(nbatch, nblk),
        in_specs=[pl.BlockSpec((1, blk, w), lambda b, j: (b, j, q_cb)),
                  pl.BlockSpec((1, blk, w), lambda b, j: (b, j, k_cb)),
                  pl.BlockSpec((1, blk, w), lambda b, j: (b, j, v_cb)),
                  pl.BlockSpec((blk, d), lambda b, j: (j, 0)),
                  pl.BlockSpec((blk, d), lambda b, j: (j, 0)),
                  pl.BlockSpec((ROPE_DIM, blk), lambda b, j: (0, j)),
                  pl.BlockSpec((ROPE_DIM, blk), lambda b, j: (0, j))],
        out_specs=[pl.BlockSpec((1, hh, 1, d, blk), lambda b, j: (b, 0, j, 0, 0)),
                   pl.BlockSpec((1, blk, w), lambda b, j: (b, j, 0)),
                   pl.BlockSpec((1, hh, 1, d, blk), lambda b, j: (b, 0, j, 0, 0)),
                   pl.BlockSpec((1, hh, 1, bias_rows, blk), lambda b, j: (b, 0, j, 0, 0))],
        out_shape=[jax.ShapeDtypeStruct((nbatch, hh, nblk, d, blk), BF16),
                   jax.ShapeDtypeStruct((nbatch, seq, w), BF16),
                   jax.ShapeDtypeStruct((nbatch, hh, nblk, d, blk), BF16),
                   jax.ShapeDtypeStruct((nbatch, hh, nblk, bias_rows, blk), F32)],
        scratch_shapes=[pltpu.VMEM(s, dt) for s, dt in scratch],
        compiler_params=_params(("arbitrary", "arbitrary"), blocks, scratch),
        name="moba_prep",
    )(z3, z3, z3, cos_tab, sin_tab, cos_t, sin_t)


def _pipelined(start, count, first, step):
    lo = start % 2

    def pair(t, carry):
        i = start + 2 * t
        return step(i + 1, 1 - lo, step(i, lo, carry))

    carry = lax.fori_loop(0, (count - start) // 2, pair, first)
    if (count - start) % 2:
        carry = step(count - 1, (count - 1) % 2, carry)
    return carry


def _moba_flat_kernel(tile_ref, group_ref, q_ref, k_ref, bias_ref, vt_ref, o_ref,
                      s_sc, p_sc, acc_sc, m_sc, l_sc):
    nblk, d, blk = vt_ref.shape[2:]
    group = MOBA_GROUP
    gkeys = group * blk
    n_items = tile_ref.shape[0]
    c_exp = (d ** -0.5) * np.log2(np.e).astype(np.float32)
    key = lax.broadcasted_iota(jnp.int32, (blk, blk), 0)
    qry = lax.broadcasted_iota(jnp.int32, (blk, blk), 1)

    def weighted_values(first_blk, p):
        acc = None
        for i in range(p.shape[0] // blk):
            part = jnp.dot(vt_ref[0, 0, first_blk + i], p[i * blk:(i + 1) * blk],
                           preferred_element_type=F32)
            acc = part if acc is None else acc + part
        return acc

    def own_scores(j, slot):
        r0 = pl.multiple_of(j * blk, blk)
        s_t = jnp.dot(k_ref[0, pl.ds(r0, blk), :], q_ref[0, 0, j], preferred_element_type=F32)
        s_t = jnp.where(key <= qry, s_t, -jnp.inf)
        s_sc[slot, :blk, :] = s_t
        return jnp.max(s_t, axis=0, keepdims=True)

    def own_softmax(j, slot, s_max):
        next_max = own_scores(jnp.minimum(j + 1, nblk - 1), 1 - slot)
        p = jnp.exp2((s_sc[slot, :blk, :] - s_max) * c_exp)
        m_sc[j] = s_max
        l_sc[j] = jnp.sum(p, axis=0, keepdims=True)
        p_sc[slot] = p.astype(BF16)
        return next_max

    def own_values(j, slot):
        acc_sc[j] = weighted_values(j, p_sc[slot])

    def own_step(j, slot, s_max):
        own_values(j - 1, 1 - slot)
        return own_softmax(j, slot, s_max)

    _pipelined(1, nblk, own_softmax(0, 0, own_scores(0, 0)), own_step)
    own_values(nblk - 1, (nblk - 1) % 2)

    def group_scores(i, slot):
        j = tile_ref[i]
        g = group_ref[i]
        c0 = pl.multiple_of(g * gkeys, gkeys)
        s_t = jnp.dot(k_ref[0, pl.ds(c0, gkeys), :], q_ref[0, 0, j], preferred_element_type=F32)
        bias = bias_ref[0, 0, j, pl.ds(pl.multiple_of(g * BIAS_ROWS, BIAS_ROWS), BIAS_ROWS), :]
        s_t = jnp.concatenate([s_t[n * blk:(n + 1) * blk] + bias[n:n + 1] for n in range(group)],
                              axis=0)
        s_sc[slot] = s_t
        return jnp.max(s_t, axis=0, keepdims=True)

    def group_step(i, slot, s_max):
        next_max = group_scores(jnp.minimum(i + 1, n_items - 1), 1 - slot)
        j = tile_ref[i]
        m_run = m_sc[j]
        m_new = jnp.maximum(m_run, s_max)
        alpha = jnp.exp2((m_run - m_new) * c_exp)
        p = jnp.exp2((s_sc[slot] - m_new) * c_exp)
        m_sc[j] = m_new
        l_sc[j] = alpha * l_sc[j] + jnp.sum(p, axis=0, keepdims=True)
        acc_sc[j] = alpha * acc_sc[j] + weighted_values(group_ref[i] * group, p.astype(BF16))
        return next_max

    _pipelined(0, n_items, group_scores(0, 0), group_step)

    def finish(j, carry):
        r0 = pl.multiple_of(j * blk, blk)
        o_ref[0, pl.ds(r0, blk), :] = (acc_sc[j] / l_sc[j]).T.astype(o_ref.dtype)
        return carry

    lax.fori_loop(0, nblk, finish, 0)


def _moba_flat(q_t, kb, vt, bias):
    nbatch, seq, w = kb.shape
    blk = MOBA_BLOCK
    d = HEAD_DIM
    nblk = vt.shape[2]
    group = MOBA_GROUP
    bias_rows = bias.shape[3]
    items = [(j, g) for j in range(nblk) for g in range(-(-j // group))]
    assert items and _tiles(nblk, group)
    item_tile = jnp.asarray([j for j, _ in items], jnp.int32)
    item_group = jnp.asarray([g for _, g in items], jnp.int32)
    blocks = [((nblk, d, blk), BF16), ((1, seq, d), BF16), ((nblk, bias_rows, blk), F32), ((nblk, d, blk), BF16),
              ((1, seq, d), BF16)]
    scratch = [((2, group * blk, blk), F32), ((2, blk, blk), BF16), ((nblk, d, blk), F32),
               ((nblk, 1, blk), F32), ((nblk, 1, blk), F32)]
    grid_spec = pltpu.PrefetchScalarGridSpec(
        num_scalar_prefetch=2,
        grid=(nbatch, MOBA_HEADS),
        in_specs=[pl.BlockSpec((1, 1, nblk, d, blk), lambda b, h, *_: (b, h, 0, 0, 0)),
                  pl.BlockSpec((1, seq, d), lambda b, h, *_: (b, 0, h)),
                  pl.BlockSpec((1, 1, nblk, bias_rows, blk), lambda b, h, *_: (b, h, 0, 0, 0)),
                  pl.BlockSpec((1, 1, nblk, d, blk), lambda b, h, *_: (b, h, 0, 0, 0))],
        out_specs=pl.BlockSpec((1, seq, d), lambda b, h, *_: (b, 0, h)),
        scratch_shapes=[pltpu.VMEM(s, dt) for s, dt in scratch])
    return pl.pallas_call(
        _moba_flat_kernel,
        grid_spec=grid_spec,
        out_shape=jax.ShapeDtypeStruct((nbatch, seq, w), BF16),
        compiler_params=_params(("arbitrary", "arbitrary"), blocks, scratch),
        name="moba_attn",
    )(item_tile, item_group, q_t, kb, bias, vt)


def _gelu_tanh(x):
    c = np.sqrt(2.0 / np.pi).astype(np.float32)
    return x * (0.5 * (1.0 + jnp.tanh(c * (x + 0.044715 * (x * x * x)))))


def _gmlp_kernel(u_ref, v_ref, lg_ref, lb_ref, ws_ref, bst_ref, y_ref):
    rows = u_ref.shape[1]
    t = GMLP_CHUNK
    gd = GMLP_WIDTH // GMLP_GROUPS
    v = _gelu_tanh(v_ref[0])
    mu = jnp.mean(v, axis=-1, keepdims=True)
    vc = v - mu
    vln = vc * lax.rsqrt(jnp.mean(vc * vc, axis=-1, keepdims=True) + NORM_EPS) * lg_ref[...] + lb_ref[...]
    vb = vln.astype(BF16)
    row = lax.broadcasted_iota(jnp.int32, (t, t), 0)
    col = lax.broadcasted_iota(jnp.int32, (t, t), 1)
    for g in range(GMLP_GROUPS):
        wg = jnp.where(col <= row, ws_ref[g], 0.0).astype(BF16)
        bias = bst_ref[:, g:g + 1]
        cols = slice(g * gd, (g + 1) * gd)
        for c in range(rows // t):
            rs = slice(c * t, (c + 1) * t)
            mixed = jnp.dot(wg, vb[rs, cols], preferred_element_type=F32) + bias
            y_ref[0, rs, cols] = (_gelu_tanh(u_ref[0, rs, cols]) * mixed).astype(y_ref.dtype)


def _gmlp(z3, u_cb, v_cb, ln_g, ln_b, ws, bs, rows=1024):
    nbatch, seq, _ = z3.shape
    w = GMLP_WIDTH
    t = GMLP_CHUNK
    blocks = [((1, rows, w), F32), ((1, rows, w), F32), ((GMLP_GROUPS, t, t), F32), ((1, rows, w), BF16)]
    return pl.pallas_call(
        _gmlp_kernel,
        grid=(nbatch, _tiles(seq, rows)),
        in_specs=[pl.BlockSpec((1, rows, w), lambda b, c: (b, c, u_cb)),
                  pl.BlockSpec((1, rows, w), lambda b, c: (b, c, v_cb)),
                  pl.BlockSpec((1, w), lambda b, c: (0, 0)),
                  pl.BlockSpec((1, w), lambda b, c: (0, 0)),
                  pl.BlockSpec((GMLP_GROUPS, t, t), lambda b, c: (0, 0, 0)),
                  pl.BlockSpec((t, GMLP_GROUPS), lambda b, c: (0, 0))],
        out_specs=pl.BlockSpec((1, rows, w), lambda b, c: (b, c, 0)),
        out_shape=jax.ShapeDtypeStruct((nbatch, seq, w), BF16),
        compiler_params=_params(("arbitrary", "arbitrary"), blocks),
        name="gmlp",
    )(z3, z3, ln_g.reshape(1, w), ln_b.reshape(1, w), ws, bs.T)


def _layer_weight_spec(layer, k, tn, index_of):
    return pl.BlockSpec((None, k, tn), lambda *idx: (layer,) + index_of(*idx))


def _merge_kernel(h_ref, wga_ref, wgb_ref, wgc_ref, ya_ref, yb_ref, yc_ref, wa_ref, wb_ref, wc_ref, o_ref):
    h = h_ref[...]

    def branch(wg_ref, y_ref, w_ref):
        gate = jax.nn.sigmoid(jnp.dot(h, wg_ref[...], preferred_element_type=F32))
        return gate * jnp.dot(y_ref[...], w_ref[...].astype(BF16), preferred_element_type=F32)

    merged = branch(wga_ref, ya_ref, wa_ref) + branch(wgb_ref, yb_ref, wb_ref) + branch(wgc_ref, yc_ref, wc_ref)
    o_ref[...] = merged.astype(o_ref.dtype)


def _merge(h, w_gate, ya, yb, yc, wa, wb, wc, layer, tm=1024, tn=512):
    m, d = h.shape
    nblk = _tiles(d, tn)
    ka, kb, kc = ya.shape[1], yb.shape[1], yc.shape[1]
    blocks = ([((tm, d), BF16)] + [((d, tn), BF16)] * 3
              + [((tm, ka), BF16), ((tm, kb), BF16), ((tm, kc), BF16)]
              + [((ka, tn), F32), ((kb, tn), F32), ((kc, tn), F32), ((tm, tn), BF16)])

    def gate_spec(branch):
        return pl.BlockSpec((None, d, tn), lambda i, j: (layer, 0, branch * nblk + j))

    def col(i, j):
        return (0, j)

    return pl.pallas_call(
        _merge_kernel,
        grid=(_tiles(m, tm), nblk),
        in_specs=[pl.BlockSpec((tm, d), lambda i, j: (i, 0)),
                  gate_spec(0), gate_spec(1), gate_spec(2),
                  pl.BlockSpec((tm, ka), lambda i, j: (i, 0)),
                  pl.BlockSpec((tm, kb), lambda i, j: (i, 0)),
                  pl.BlockSpec((tm, kc), lambda i, j: (i, 0)),
                  _layer_weight_spec(layer, ka, tn, col),
                  _layer_weight_spec(layer, kb, tn, col),
                  _layer_weight_spec(layer, kc, tn, col)],
        out_specs=pl.BlockSpec((tm, tn), lambda i, j: (i, j)),
        out_shape=jax.ShapeDtypeStruct((m, d), BF16),
        compiler_params=_params(("arbitrary", "arbitrary"), blocks),
        name="merge",
    )(h, w_gate, w_gate, w_gate, ya, yb, yc, wa, wb, wc)


def _residual_matmul_kernel(a_ref, w_ref, r_ref, o_ref):
    @pl.when(pl.program_id(2) == 0)
    def _():
        o_ref[...] = r_ref[...]

    o_ref[...] += jnp.dot(a_ref[...], w_ref[...].astype(BF16), preferred_element_type=F32)


def _residual_matmul(a, w, layer, res, tm=1024, tn=1024, tk=1024):
    m, k = a.shape
    n = w.shape[2]
    blocks = [((tm, tk), BF16), ((tk, tn), F32), ((tm, tn), F32), ((tm, tn), F32)]
    return pl.pallas_call(
        _residual_matmul_kernel,
        grid=(_tiles(m, tm), _tiles(n, tn), _tiles(k, tk)),
        in_specs=[pl.BlockSpec((tm, tk), lambda i, j, kk: (i, kk)),
                  _layer_weight_spec(layer, tk, tn, lambda i, j, kk: (kk, j)),
                  pl.BlockSpec((tm, tn), lambda i, j, kk: (i, j))],
        out_specs=pl.BlockSpec((tm, tn), lambda i, j, kk: (i, j)),
        out_shape=jax.ShapeDtypeStruct((m, n), F32),
        compiler_params=_params(("arbitrary", "arbitrary", "arbitrary"), blocks),
        name="residual_matmul",
    )(a, w, res)


def _out_proj_kernel(a_ref, w_ref, r_ref, g_ref, x_ref, h_ref):
    x_ref[...] = r_ref[...] + jnp.dot(a_ref[...], w_ref[...].astype(BF16), preferred_element_type=F32)
    _rmsnorm_rows(x_ref, g_ref, h_ref)


def _out_proj(a, w, layer, res, g, tm=512):
    m, k = a.shape
    d = w.shape[2]
    blocks = [((tm, k), BF16), ((k, d), w.dtype), ((tm, d), F32), ((tm, d), F32), ((tm, d), BF16)]
    return pl.pallas_call(
        _out_proj_kernel,
        grid=(_tiles(m, tm),),
        in_specs=[pl.BlockSpec((tm, k), lambda i: (i, 0)),
                  _layer_weight_spec(layer, k, d, lambda i: (0, 0)),
                  pl.BlockSpec((tm, d), lambda i: (i, 0)),
                  pl.BlockSpec((1, d), lambda i: (0, 0))],
        out_specs=[pl.BlockSpec((tm, d), lambda i: (i, 0)),
                   pl.BlockSpec((tm, d), lambda i: (i, 0))],
        out_shape=[jax.ShapeDtypeStruct((m, d), F32),
                   jax.ShapeDtypeStruct((m, d), BF16)],
        compiler_params=_params(("arbitrary",), blocks),
        name="out_proj",
    )(a, w, res, g.reshape(1, d))


def _relu2_matmul_kernel(h_ref, w_ref, o_ref):
    up = jnp.maximum(jnp.dot(h_ref[...], w_ref[...].astype(BF16), preferred_element_type=F32), 0.0)
    o_ref[...] = (up * up).astype(o_ref.dtype)


def _relu2_matmul(h, w, layer, tm=2048, tn=1024):
    m, k = h.shape
    n = w.shape[2]
    blocks = [((tm, k), BF16), ((k, tn), F32), ((tm, tn), BF16)]
    return pl.pallas_call(
        _relu2_matmul_kernel,
        grid=(_tiles(m, tm), _tiles(n, tn)),
        in_specs=[pl.BlockSpec((tm, k), lambda i, j: (i, 0)),
                  _layer_weight_spec(layer, k, tn, lambda i, j: (0, j))],
        out_specs=pl.BlockSpec((tm, tn), lambda i, j: (i, j)),
        out_shape=jax.ShapeDtypeStruct((m, n), BF16),
        compiler_params=_params(("arbitrary", "arbitrary"), blocks),
        name="mlp_up",
    )(h, w)


def _cast_kernel(w_ref, o_ref):
    o_ref[...] = w_ref[...].astype(o_ref.dtype)


def _cast_bf16(w, rows=512):
    depth, k, n = w.shape
    spec = pl.BlockSpec((None, rows, n), lambda l, r: (l, r, 0))
    return pl.pallas_call(
        _cast_kernel,
        grid=(depth, _tiles(k, rows)),
        in_specs=[spec],
        out_specs=spec,
        out_shape=jax.ShapeDtypeStruct(w.shape, BF16),
        compiler_params=_params(("arbitrary", "arbitrary"), [((rows, n), F32), ((rows, n), BF16)]),
        name="cast_bf16",
    )(w)


def _ple_kernel(x_ref, g_ref, wg_ref, p_ref, wp_ref, fg_ref, o_ref, h_sc, *, final_norm):
    _rmsnorm_rows(x_ref, g_ref, h_sc)
    gate = jax.nn.sigmoid(jnp.dot(h_sc[...], wg_ref[...].astype(BF16), preferred_element_type=F32))
    emb = jnp.dot(p_ref[...].astype(BF16), wp_ref[...].astype(BF16), preferred_element_type=F32)
    out = x_ref[...] + gate * emb
    if final_norm:
        out = out * lax.rsqrt(jnp.mean(out * out, axis=-1, keepdims=True) + NORM_EPS) * fg_ref[...]
    o_ref[...] = out


def _ple(x, g, w_gate, p, w_proj, layer, final_g, final_norm, tm=512):
    m, d = x.shape
    pd = p.shape[2]
    blocks = [((tm, d), F32), ((d, d), w_gate.dtype), ((tm, pd), F32), ((pd, d), F32), ((tm, d), F32)]
    scratch = [((tm, d), BF16)]
    return pl.pallas_call(
        functools.partial(_ple_kernel, final_norm=final_norm),
        grid=(_tiles(m, tm),),
        in_specs=[pl.BlockSpec((tm, d), lambda i: (i, 0)),
                  pl.BlockSpec((1, d), lambda i: (0, 0)),
                  _layer_weight_spec(layer, d, d, lambda i: (0, 0)),
                  pl.BlockSpec((None, tm, pd), lambda i: (layer, i, 0)),
                  _layer_weight_spec(layer, pd, d, lambda i: (0, 0)),
                  pl.BlockSpec((1, d), lambda i: (0, 0))],
        out_specs=pl.BlockSpec((tm, d), lambda i: (i, 0)),
        out_shape=jax.ShapeDtypeStruct((m, d), F32),
        scratch_shapes=[pltpu.VMEM(s, dt) for s, dt in scratch],
        compiler_params=_params(("arbitrary",), blocks, scratch),
        name="ple",
    )(x, g.reshape(1, d), w_gate, p, w_proj, final_g.reshape(1, d))


def kernel(x, p, norm_mix_g, w_in, mlstm_gate_b, mlstm_norm_g, gmlp_norm_g, gmlp_norm_b, gmlp_ws, gmlp_bs,
           w_branch_a, w_branch_b, w_branch_c, w_out, norm_mlp_g, w_mlp_up, w_mlp_down, norm_ple_g,
           w_ple_gate, w_ple_proj, final_norm_g):
    nbatch, seq, d = x.shape
    depth = w_in.shape[0]
    m = nbatch * seq
    assert d == MLSTM_WIDTH + MOBA_WIDTH + GMLP_WIDTH

    qkvo_a = 4 * MLSTM_WIDTH
    gates_if = 2 * MLSTM_HEADS
    main_b = qkvo_a + gates_if
    main_cols = 3 * MOBA_WIDTH + 2 * GMLP_WIDTH
    gate_off = main_b + main_cols
    assert w_in.shape[2] == gate_off + N_BRANCHES * d

    moba_q_cb = qkvo_a // MOBA_WIDTH
    moba_k_cb = moba_q_cb + 1
    moba_v_cb = moba_k_cb + 1
    gmlp_u_cb = (qkvo_a + 3 * MOBA_WIDTH) // GMLP_WIDTH
    gmlp_v_cb = gmlp_u_cb + 1

    z_cols = qkvo_a + main_cols
    colscale = jnp.ones((1, z_cols), F32).at[:, MLSTM_WIDTH:2 * MLSTM_WIDTH].set(HEAD_DIM ** -0.5)
    rope_tables = _rope_tables(seq) + _rope_tables_t(seq)

    xf = x.reshape(m, d)
    p_flat = p.reshape(depth, m, p.shape[-1])
    w_main, w_if, w_gate = _win_split(w_in, qkvo_a, gates_if, z_cols)
    w_ple_gate_b = _cast_bf16(w_ple_gate)
    w_out_b = _cast_bf16(w_out)
    for i in range(depth):
        z, zif, h = _inproj(xf, norm_mix_g[i], w_main, w_if, i, colscale)
        z3 = z.reshape(nbatch, seq, z_cols)
        zif3 = zif.reshape(nbatch, seq, GATE_LANES)

        ya = _mlstm(z3, zif3, mlstm_gate_b[i], mlstm_norm_g[i])
        yb = _moba_flat(*_moba_prep_t(z3, rope_tables, moba_q_cb, moba_k_cb, moba_v_cb))
        yc = _gmlp(z3, gmlp_u_cb, gmlp_v_cb, gmlp_norm_g[i], gmlp_norm_b[i], gmlp_ws[i], gmlp_bs[i])

        merged = _merge(h, w_gate, ya.reshape(m, -1), yb.reshape(m, -1), yc.reshape(m, -1),
                        w_branch_a, w_branch_b, w_branch_c, i)
        xf, h2 = _out_proj(merged, w_out_b, i, xf, norm_mlp_g[i])
        hidden = _relu2_matmul(h2, w_mlp_up, i)
        xf = _residual_matmul(hidden, w_mlp_down, i, xf, tm=2048)
        xf = _ple(xf, norm_ple_g[i], w_ple_gate_b, p_flat, w_ple_proj, i, final_norm_g, i == depth - 1)

    return xf.reshape(nbatch, seq, d)
```

```python
import functools

import jax
import jax.numpy as jnp
import numpy as np
from jax import lax
from jax.experimental import pallas as pl
from jax.experimental.pallas import tpu as pltpu

F32 = jnp.float32
BF16 = jnp.bfloat16

HEAD_DIM = 128
MLSTM_HEADS = 4
MLSTM_WIDTH = MLSTM_HEADS * HEAD_DIM
MOBA_HEADS = 8
MOBA_WIDTH = MOBA_HEADS * HEAD_DIM
MOBA_BLOCK = 256
MOBA_TOPK = 3
ROPE_THETA = 500000.0
ROPE_DIM = HEAD_DIM // 4
GMLP_WIDTH = 512
GMLP_GROUPS = 4
GMLP_CHUNK = 128
N_BRANCHES = 3
NORM_EPS = 1e-6

LANES = 128
V7X_VMEM_BYTES = 64 * 1024 * 1024
VMEM_CEILING = V7X_VMEM_BYTES - 8 * 1024 * 1024

MLSTM_KERNEL_CHUNK = 128
MLSTM_CHUNKS_PER_STEP = 4
STATE_ROWS = 16
GATE_LANES = LANES
MASK_BIG = 2.0 ** 100
BIAS_ROWS = 8
MOBA_GROUP = 4

NT_DIMS = (((1,), (1,)), ((), ()))


def _tiles(n, t):
    count, rest = divmod(n, t)
    assert rest == 0 and count > 0, (n, t)
    return count


def _nbytes(shape, dtype):
    return int(np.prod(shape)) * jnp.dtype(dtype).itemsize


def _params(semantics, blocks, scratch=()):
    need = 2 * sum(_nbytes(s, d) for s, d in blocks) + sum(_nbytes(s, d) for s, d in scratch)
    limit = min(VMEM_CEILING, need + need // 4 + 4 * 1024 * 1024)
    return pltpu.CompilerParams(dimension_semantics=semantics, vmem_limit_bytes=limit)


def _row_window(rows, k, start_of):
    return pl.BlockSpec((pl.Element(1), pl.Element(rows), pl.Element(k)), start_of)


def _win_split_kernel(wm_ref, wif_ref, main_ref, if_ref):
    main_ref[...] = wm_ref[0].T.astype(BF16)
    _, n_if, k = wif_ref.shape
    padded = jnp.concatenate([wif_ref[0], jnp.zeros((GATE_LANES - n_if, k), F32)], axis=0)
    if_ref[...] = padded.T.astype(BF16)


def _win_split(w_in, head, n_if, main_cols, tn=512):
    depth, k, cols = w_in.shape
    assert head % tn == 0
    w_t = jnp.swapaxes(w_in, 1, 2)
    head_steps = head // tn
    blocks = [((tn, k), F32), ((n_if, k), F32), ((k, tn), BF16), ((k, GATE_LANES), BF16)]

    def main_rows(l, r):
        return (l, pl.multiple_of(r * tn + jnp.where(r >= head_steps, n_if, 0), n_if), 0)

    main, gates_if = pl.pallas_call(
        _win_split_kernel,
        grid=(depth, _tiles(main_cols, tn)),
        in_specs=[_row_window(tn, k, main_rows),
                  _row_window(n_if, k, lambda l, r: (l, head, 0))],
        out_specs=[pl.BlockSpec((None, k, tn), lambda l, r: (l, 0, r)),
                   pl.BlockSpec((None, k, GATE_LANES), lambda l, r: (l, 0, 0))],
        out_shape=[jax.ShapeDtypeStruct((depth, k, main_cols), BF16),
                   jax.ShapeDtypeStruct((depth, k, GATE_LANES), BF16)],
        compiler_params=_params(("arbitrary", "arbitrary"), blocks),
        name="win_split",
    )(w_t, w_t)
    return main, gates_if, w_t


def _rmsnorm_rows(x_ref, g_ref, h_ref, rows=256):
    for r0 in range(0, x_ref.shape[0], rows):
        x = x_ref[r0:r0 + rows, :]
        y = x * lax.rsqrt(jnp.mean(x * x, axis=-1, keepdims=True) + NORM_EPS)
        h_ref[r0:r0 + rows, :] = (y * g_ref[...]).astype(h_ref.dtype)


def _inproj_kernel(x_ref, g_ref, w_ref, wif_ref, cs_ref, z_ref, zif_ref, h_ref):
    @pl.when(pl.program_id(1) == 0)
    def _():
        _rmsnorm_rows(x_ref, g_ref, h_ref)
        zif_ref[...] = jnp.dot(h_ref[...], wif_ref[...], preferred_element_type=F32)

    acc = jnp.dot(h_ref[...], w_ref[...], preferred_element_type=F32)
    z_ref[...] = acc * cs_ref[...]


def _inproj(x, g, w_main, w_if, layer, colscale, tm=1024, tn=1024):
    m, k = x.shape
    n = w_main.shape[2]
    blocks = [((tm, k), F32), ((k, tn), BF16), ((k, GATE_LANES), BF16), ((1, tn), F32),
              ((tm, tn), F32), ((tm, GATE_LANES), F32), ((tm, k), BF16)]
    return pl.pallas_call(
        _inproj_kernel,
        grid=(_tiles(m, tm), _tiles(n, tn)),
        in_specs=[pl.BlockSpec((tm, k), lambda i, j: (i, 0)),
                  pl.BlockSpec((1, k), lambda i, j: (0, 0)),
                  pl.BlockSpec((None, k, tn), lambda i, j: (layer, 0, j)),
                  pl.BlockSpec((None, k, GATE_LANES), lambda i, j: (layer, 0, 0)),
                  pl.BlockSpec((1, tn), lambda i, j: (0, j))],
        out_specs=[pl.BlockSpec((tm, tn), lambda i, j: (i, j)),
                   pl.BlockSpec((tm, GATE_LANES), lambda i, j: (i, 0)),
                   pl.BlockSpec((tm, k), lambda i, j: (i, 0))],
        out_shape=[jax.ShapeDtypeStruct((m, n), F32),
                   jax.ShapeDtypeStruct((m, GATE_LANES), F32),
                   jax.ShapeDtypeStruct((m, k), BF16)],
        compiler_params=_params(("arbitrary", "arbitrary"), blocks),
        name="in_proj",
    )(x, g.reshape(1, k), w_main, w_if, colscale)


def _log_sigmoid(x):
    return jnp.minimum(x, 0.0) - jnp.log1p(jnp.exp(-jnp.abs(x)))


def _exact_tril_matmul(tril, x):
    tril = tril.astype(BF16)
    hi = x.astype(BF16)
    rest = x - hi.astype(F32)
    mid = rest.astype(BF16)
    lo = (rest - mid.astype(F32)).astype(BF16)
    return (jnp.dot(tril, hi, preferred_element_type=F32) + jnp.dot(tril, mid, preferred_element_type=F32)
            + jnp.dot(tril, lo, preferred_element_type=F32))


def _mlstm_kernel(q_ref, k_ref, v_ref, o_ref, zif_ref, gb_ref, ng_ref, y_ref, c_sc, n_sc, m_sc):
    @pl.when(pl.program_id(0) == 0)
    def _():
        c_sc[...] = jnp.zeros_like(c_sc)
        n_sc[...] = jnp.zeros_like(n_sc)
        m_sc[...] = jnp.zeros_like(m_sc)

    for c0 in range(0, q_ref.shape[1], MLSTM_KERNEL_CHUNK):
        rows = pl.ds(c0, MLSTM_KERNEL_CHUNK)
        _mlstm_chunk(q_ref.at[:, rows, :], k_ref.at[:, rows, :], v_ref.at[:, rows, :], o_ref.at[:, rows, :],
                     zif_ref.at[:, rows, :], gb_ref, ng_ref, y_ref.at[:, rows, :], c_sc, n_sc, m_sc)


def _mlstm_chunk(q_ref, k_ref, v_ref, o_ref, zif_ref, gb_ref, ng_ref, y_ref, c_sc, n_sc, m_sc):
    nbatch, chunk, _ = q_ref.shape
    heads = MLSTM_HEADS
    row = lax.broadcasted_iota(jnp.int32, (chunk, chunk), 0)
    col = lax.broadcasted_iota(jnp.int32, (chunk, chunk), 1)
    causal_t = row <= col
    tril = jnp.where(col <= row, 1.0, 0.0).astype(F32)
    lane = lax.broadcasted_iota(jnp.int32, (chunk, GATE_LANES), 1)

    for b in range(nbatch):
        pre = zif_ref[b] + gb_ref[...]
        gates = jnp.where(lane < heads, pre, _log_sigmoid(pre))
        gcum = _exact_tril_matmul(tril, gates)
        gates_t = gates.T
        gcum_t = gcum.T
        for h in range(heads):
            s = b * heads + h
            sl = slice(h * HEAD_DIM, (h + 1) * HEAD_DIM)
            q = q_ref[b, :, sl]
            k = k_ref[b, :, sl]
            v = v_ref[b, :, sl]
            q_t = q.T
            qb = q.astype(BF16)
            kb = k.astype(BF16)
            q_tb = q_t.astype(BF16)
            v_t = v.T
            g_row = gcum_t[heads + h:heads + h + 1, :]
            i_row = gates_t[h:h + 1, :]
            ig_col = gates[:, h:h + 1] - gcum[:, heads + h:heads + h + 1]
            m_prev = m_sc[s][:, 0:1]
            c_prev = c_sc[s]
            n_prev = n_sc[s]

            log_w = jnp.where(causal_t, g_row + ig_col, -jnp.inf)
            log_a = g_row + m_prev
            m_row = jnp.maximum(jnp.max(log_w, axis=0, keepdims=True), log_a)
            qk = lax.dot_general(kb, qb, NT_DIMS, preferred_element_type=F32) * jnp.exp(log_w - m_row)
            a = jnp.exp(log_a - m_row)
            num = (jnp.dot(v_t.astype(BF16), qk.astype(BF16), preferred_element_type=F32)
                   + a * jnp.dot(c_prev.astype(BF16), q_tb, preferred_element_type=F32))
            n_dot_q = jnp.dot(n_prev.astype(BF16), q_tb, preferred_element_type=F32)[0:1]
            den = jnp.sum(qk, axis=0, keepdims=True) + a * n_dot_q
            h_out = num / jnp.maximum(jnp.abs(den), jnp.exp(-m_row))

            g_last = g_row[:, chunk - 1:chunk]
            log_u = g_last - g_row + i_row
            m_new = jnp.maximum(g_last + m_prev, jnp.max(log_u, axis=1, keepdims=True))
            decay = jnp.exp(g_last + m_prev - m_new)
            u = jnp.exp(log_u - m_new)
            c_sc[s] = decay * c_prev + jnp.dot((v_t * u).astype(BF16), kb, preferred_element_type=F32)
            u_rows = jnp.broadcast_to(u, (STATE_ROWS, chunk)).astype(BF16)
            n_sc[s] = decay * n_prev + jnp.dot(u_rows, kb, preferred_element_type=F32)
            m_sc[s] = jnp.broadcast_to(m_new, (1, LANES))

            yn = (h_out * lax.rsqrt(jnp.mean(h_out * h_out, axis=0, keepdims=True) + NORM_EPS)).T
            y_ref[b, :, sl] = (jax.nn.sigmoid(o_ref[b, :, sl]) * (yn * ng_ref[:, sl])).astype(y_ref.dtype)


def _mlstm(z3, zif3, gate_b, norm_g):
    nbatch, seq, _ = z3.shape
    chunk = MLSTM_KERNEL_CHUNK * MLSTM_CHUNKS_PER_STEP
    w = MLSTM_WIDTH
    streams = nbatch * MLSTM_HEADS
    gb = jnp.pad(gate_b, (0, GATE_LANES - gate_b.shape[0])).reshape(1, GATE_LANES)
    blocks = [((nbatch, chunk, w), F32)] * 4 + [((nbatch, chunk, GATE_LANES), F32),
                                               ((nbatch, chunk, w), BF16)]
    scratch = [((streams, HEAD_DIM, HEAD_DIM), F32), ((streams, STATE_ROWS, LANES), F32),
               ((streams, 1, LANES), F32)]

    def zcol(cb):
        return pl.BlockSpec((nbatch, chunk, w), lambda c: (0, c, cb))

    return pl.pallas_call(
        _mlstm_kernel,
        grid=(_tiles(seq, chunk),),
        in_specs=[zcol(0), zcol(1), zcol(2), zcol(3),
                  pl.BlockSpec((nbatch, chunk, GATE_LANES), lambda c: (0, c, 0)),
                  pl.BlockSpec((1, GATE_LANES), lambda c: (0, 0)),
                  pl.BlockSpec((1, w), lambda c: (0, 0))],
        out_specs=pl.BlockSpec((nbatch, chunk, w), lambda c: (0, c, 0)),
        out_shape=jax.ShapeDtypeStruct((nbatch, seq, w), BF16),
        scratch_shapes=[pltpu.VMEM(s, d) for s, d in scratch],
        compiler_params=_params(("arbitrary",), blocks, scratch),
        name="mlstm",
    )(z3, z3, z3, z3, zif3, gb, norm_g.reshape(1, w))


def _rope_tables(seq):
    half = ROPE_DIM // 2
    inv_freq = ROPE_THETA ** (-jnp.arange(0, ROPE_DIM, 2, dtype=F32) / ROPE_DIM)
    ang = jnp.arange(seq, dtype=F32)[:, None] * inv_freq[None, :]
    cos = jnp.cos(ang)
    sin = jnp.sin(ang)
    ones = jnp.ones((seq, HEAD_DIM - ROPE_DIM), F32)
    cos_tab = jnp.concatenate([cos, cos, ones], axis=1)
    sin_tab = jnp.concatenate([-sin, sin, 0.0 * ones], axis=1)
    assert cos_tab.shape == (seq, HEAD_DIM) and half * 2 == ROPE_DIM
    return cos_tab, sin_tab


def _rotary(t, cos, sin):
    half = ROPE_DIM // 2
    lane = lax.broadcasted_iota(jnp.int32, t.shape, 1)
    upper = pltpu.roll(t, HEAD_DIM - half, axis=1)
    lower = pltpu.roll(t, half, axis=1)
    partner = jnp.where(lane < half, upper, lower)
    return jnp.where(lane < ROPE_DIM, t * cos + partner * sin, t)


def _rope_tables_t(seq):
    inv_freq = ROPE_THETA ** (-jnp.arange(0, ROPE_DIM, 2, dtype=F32) / ROPE_DIM)
    ang = jnp.arange(seq, dtype=F32)[:, None] * inv_freq[None, :]
    cos = jnp.cos(ang).T
    sin = jnp.sin(ang).T
    return jnp.concatenate([cos, cos], axis=0), jnp.concatenate([-sin, sin], axis=0)


def _moba_prep_t_kernel(q_ref, k_ref, v_ref, cos_ref, sin_ref, cost_ref, sint_ref,
                        qa_ref, kb_ref, vt_ref, bias_ref, km_sc):
    blk = q_ref.shape[1]
    d = HEAD_DIM
    half = ROPE_DIM // 2
    nsel = km_sc.shape[1]
    j = pl.program_id(1)

    @pl.when(j == 0)
    def _():
        km_sc[...] = jnp.zeros_like(km_sc)

    cos = cos_ref[...]
    sin = sin_ref[...]
    cos_t = cost_ref[...]
    sin_t = sint_ref[...]
    blk_id = lax.broadcasted_iota(jnp.int32, (nsel, blk), 0)
    blk_id_f = blk_id.astype(F32)
    mean_row = lax.broadcasted_iota(jnp.int32, (nsel, d), 0)
    for h in range(MOBA_HEADS):
        sl = slice(h * d, (h + 1) * d)
        q_t = q_ref[0, :, sl].T
        top = q_t[:ROPE_DIM]
        partner = jnp.concatenate([top[half:], top[:half]], axis=0)
        q_t = jnp.concatenate([top * cos_t + partner * sin_t, q_t[ROPE_DIM:]], axis=0)

        gate = jnp.dot(km_sc[h], q_t, preferred_element_type=F32,
                       precision=lax.Precision.HIGHEST)
        gate = jnp.where(blk_id < j, gate, -jnp.inf)
        sel_m1 = jnp.full((nsel, blk), -1.0, F32)
        for _ in range(MOBA_TOPK):
            mx = jnp.max(gate, axis=0, keepdims=True)
            first = jnp.min(jnp.where(gate == mx, blk_id_f, float(nsel)), axis=0, keepdims=True)
            first = jnp.where(mx > -jnp.inf, first, -1.0)
            hit = blk_id_f == first
            sel_m1 = jnp.where(hit, 0.0, sel_m1)
            gate = jnp.where(hit, -jnp.inf, gate)
        qa_ref[0, h, 0] = q_t.astype(BF16)
        for g in range(bias_ref.shape[3] // BIAS_ROWS):
            rows = sel_m1[g * MOBA_GROUP:(g + 1) * MOBA_GROUP] * MASK_BIG
            pad = jnp.zeros((BIAS_ROWS - MOBA_GROUP, blk), F32)
            bias_ref[0, h, 0, g * BIAS_ROWS:(g + 1) * BIAS_ROWS, :] = jnp.concatenate([rows, pad], axis=0)

        kk = _rotary(k_ref[0, :, sl], cos, sin)
        kb_ref[0, :, sl] = kk.astype(BF16)
        km_sc[h] = jnp.where(mean_row == j, jnp.mean(kk, axis=0, keepdims=True), km_sc[h])

        vt_ref[0, h, 0] = v_ref[0, :, sl].T.astype(BF16)


def _moba_prep_t(z3, tables, q_cb, k_cb, v_cb):
    nbatch, seq, _ = z3.shape
    blk = MOBA_BLOCK
    nblk = _tiles(seq, blk)
    nsel = -(-nblk // BIAS_ROWS) * BIAS_ROWS
    bias_rows = _tiles(nblk, MOBA_GROUP) * BIAS_ROWS
    w = MOBA_WIDTH
    d = HEAD_DIM
    hh = MOBA_HEADS
    cos_tab, sin_tab, cos_t, sin_t = tables
    blocks = ([((1, blk, w), F32)] * 3 + [((blk, d), F32)] * 2 + [((ROPE_DIM, blk), F32)] * 2
              + [((hh, d, blk), BF16), ((1, blk, w), BF16), ((hh, d, blk), BF16), ((hh, bias_rows, blk), F32)])
    scratch = [((hh, nsel, d), F32)]
    return pl.pallas_call(
        _moba_prep_t_kernel,
        grid=(nbatch, nblk),
        in_specs=[pl.BlockSpec((1, blk, w), lambda b, j: (b, j, q_cb)),
                  pl.BlockSpec((1, blk, w), lambda b, j: (b, j, k_cb)),
                  pl.BlockSpec((1, blk, w), lambda b, j: (b, j, v_cb)),
                  pl.BlockSpec((blk, d), lambda b, j: (j, 0)),
                  pl.BlockSpec((blk, d), lambda b, j: (j, 0)),
                  pl.BlockSpec((ROPE_DIM, blk), lambda b, j: (0, j)),
                  pl.BlockSpec((ROPE_DIM, blk), lambda b, j: (0, j))],
        out_specs=[pl.BlockSpec((1, hh, 1, d, blk), lambda b, j: (b, 0, j, 0, 0)),
                   pl.BlockSpec((1, blk, w), lambda b, j: (b, j, 0)),
                   pl.BlockSpec((1, hh, 1, d, blk), lambda b, j: (b, 0, j, 0, 0)),
                   pl.BlockSpec((1, hh, 1, bias_rows, blk), lambda b, j: (b, 0, j, 0, 0))],
        out_shape=[jax.ShapeDtypeStruct((nbatch, hh, nblk, d, blk), BF16),
                   jax.ShapeDtypeStruct((nbatch, seq, w), BF16),
                   jax.ShapeDtypeStruct((nbatch, hh, nblk, d, blk), BF16),
                   jax.ShapeDtypeStruct((nbatch, hh, nblk, bias_rows, blk), F32)],
        scratch_shapes=[pltpu.VMEM(s, dt) for s, dt in scratch],
        compiler_params=_params(("arbitrary", "arbitrary"), blocks, scratch),
        name="moba_prep",
    )(z3, z3, z3, cos_tab, sin_tab, cos_t, sin_t)


def _pipelined(start, count, first, step):
    lo = start % 2

    def pair(t, carry):
        i = start + 2 * t
        return step(i + 1, 1 - lo, step(i, lo, carry))

    carry = lax.fori_loop(0, (count - start) // 2, pair, first)
    if (count - start) % 2:
        carry = step(count - 1, (count - 1) % 2, carry)
    return carry


def _moba_flat_kernel(tile_ref, group_ref, q_ref, k_ref, bias_ref, vt_ref, o_ref,
                      s_sc, p_sc, acc_sc, m_sc, l_sc):
    nblk, d, blk = vt_ref.shape[2:]
    group = MOBA_GROUP
    gkeys = group * blk
    n_items = tile_ref.shape[0]
    c_exp = (d ** -0.5) * np.log2(np.e).astype(np.float32)
    key = lax.broadcasted_iota(jnp.int32, (blk, blk), 0)
    qry = lax.broadcasted_iota(jnp.int32, (blk, blk), 1)

    def weighted_values(first_blk, p):
        acc = None
        for i in range(p.shape[0] // blk):
            part = jnp.dot(vt_ref[0, 0, first_blk + i], p[i * blk:(i + 1) * blk],
                           preferred_element_type=F32)
            acc = part if acc is None else acc + part
        return acc

    def own_scores(j, slot):
        r0 = pl.multiple_of(j * blk, blk)
        s_t = jnp.dot(k_ref[0, pl.ds(r0, blk), :], q_ref[0, 0, j], preferred_element_type=F32)
        s_t = jnp.where(key <= qry, s_t, -jnp.inf)
        s_sc[slot, :blk, :] = s_t
        return jnp.max(s_t, axis=0, keepdims=True)

    def own_softmax(j, slot, s_max):
        next_max = own_scores(jnp.minimum(j + 1, nblk - 1), 1 - slot)
        p = jnp.exp2((s_sc[slot, :blk, :] - s_max) * c_exp)
        m_sc[j] = s_max
        l_sc[j] = jnp.sum(p, axis=0, keepdims=True)
        p_sc[slot] = p.astype(BF16)
        return next_max

    def own_values(j, slot):
        acc_sc[j] = weighted_values(j, p_sc[slot])

    def own_step(j, slot, s_max):
        own_values(j - 1, 1 - slot)
        return own_softmax(j, slot, s_max)

    _pipelined(1, nblk, own_softmax(0, 0, own_scores(0, 0)), own_step)
    own_values(nblk - 1, (nblk - 1) % 2)

    def group_scores(i, slot):
        j = tile_ref[i]
        g = group_ref[i]
        c0 = pl.multiple_of(g * gkeys, gkeys)
        s_t = jnp.dot(k_ref[0, pl.ds(c0, gkeys), :], q_ref[0, 0, j], preferred_element_type=F32)
        bias = bias_ref[0, 0, j, pl.ds(pl.multiple_of(g * BIAS_ROWS, BIAS_ROWS), BIAS_ROWS), :]
        s_t = jnp.concatenate([s_t[n * blk:(n + 1) * blk] + bias[n:n + 1] for n in range(group)],
                              axis=0)
        s_sc[slot] = s_t
        return jnp.max(s_t, axis=0, keepdims=True)

    def group_step(i, slot, s_max):
        next_max = group_scores(jnp.minimum(i + 1, n_items - 1), 1 - slot)
        j = tile_ref[i]
        m_run = m_sc[j]
        m_new = jnp.maximum(m_run, s_max)
        alpha = jnp.exp2((m_run - m_new) * c_exp)
        p = jnp.exp2((s_sc[slot] - m_new) * c_exp)
        m_sc[j] = m_new
        l_sc[j] = alpha * l_sc[j] + jnp.sum(p, axis=0, keepdims=True)
        acc_sc[j] = alpha * acc_sc[j] + weighted_values(group_ref[i] * group, p.astype(BF16))
        return next_max

    _pipelined(0, n_items, group_scores(0, 0), group_step)

    def finish(j, carry):
        r0 = pl.multiple_of(j * blk, blk)
        o_ref[0, pl.ds(r0, blk), :] = (acc_sc[j] / l_sc[j]).T.astype(o_ref.dtype)
        return carry

    lax.fori_loop(0, nblk, finish, 0)


def _moba_flat(q_t, kb, vt, bias):
    nbatch, seq, w = kb.shape
    blk = MOBA_BLOCK
    d = HEAD_DIM
    nblk = vt.shape[2]
    group = MOBA_GROUP
    bias_rows = bias.shape[3]
    items = [(j, g) for j in range(nblk) for g in range(-(-j // group))]
    assert items and _tiles(nblk, group)
    item_tile = jnp.asarray([j for j, _ in items], jnp.int32)
    item_group = jnp.asarray([g for _, g in items], jnp.int32)
    blocks = [((nblk, d, blk), BF16), ((1, seq, d), BF16), ((nblk, bias_rows, blk), F32), ((nblk, d, blk), BF16),
              ((1, seq, d), BF16)]
    scratch = [((2, group * blk, blk), F32), ((2, blk, blk), BF16), ((nblk, d, blk), F32),
               ((nblk, 1, blk), F32), ((nblk, 1, blk), F32)]
    grid_spec = pltpu.PrefetchScalarGridSpec(
        num_scalar_prefetch=2,
        grid=(nbatch, MOBA_HEADS),
        in_specs=[pl.BlockSpec((1, 1, nblk, d, blk), lambda b, h, *_: (b, h, 0, 0, 0)),
                  pl.BlockSpec((1, seq, d), lambda b, h, *_: (b, 0, h)),
                  pl.BlockSpec((1, 1, nblk, bias_rows, blk), lambda b, h, *_: (b, h, 0, 0, 0)),
                  pl.BlockSpec((1, 1, nblk, d, blk), lambda b, h, *_: (b, h, 0, 0, 0))],
        out_specs=pl.BlockSpec((1, seq, d), lambda b, h, *_: (b, 0, h)),
        scratch_shapes=[pltpu.VMEM(s, dt) for s, dt in scratch])
    return pl.pallas_call(
        _moba_flat_kernel,
        grid_spec=grid_spec,
        out_shape=jax.ShapeDtypeStruct((nbatch, seq, w), BF16),
        compiler_params=_params(("arbitrary", "arbitrary"), blocks, scratch),
        name="moba_attn",
    )(item_tile, item_group, q_t, kb, bias, vt)


def _gelu_tanh(x):
    c = np.sqrt(2.0 / np.pi).astype(np.float32)
    return x * (0.5 * (1.0 + jnp.tanh(c * (x + 0.044715 * (x * x * x)))))


def _gmlp_kernel(u_ref, v_ref, lg_ref, lb_ref, ws_ref, bst_ref, y_ref):
    rows = u_ref.shape[1]
    t = GMLP_CHUNK
    gd = GMLP_WIDTH // GMLP_GROUPS
    v = _gelu_tanh(v_ref[0])
    mu = jnp.mean(v, axis=-1, keepdims=True)
    vc = v - mu
    vln = vc * lax.rsqrt(jnp.mean(vc * vc, axis=-1, keepdims=True) + NORM_EPS) * lg_ref[...] + lb_ref[...]
    vb = vln.astype(BF16)
    row = lax.broadcasted_iota(jnp.int32, (t, t), 0)
    col = lax.broadcasted_iota(jnp.int32, (t, t), 1)
    for g in range(GMLP_GROUPS):
        wg = jnp.where(col <= row, ws_ref[g], 0.0).astype(BF16)
        bias = bst_ref[:, g:g + 1]
        cols = slice(g * gd, (g + 1) * gd)
        for c in range(rows // t):
            rs = slice(c * t, (c + 1) * t)
            mixed = jnp.dot(wg, vb[rs, cols], preferred_element_type=F32) + bias
            y_ref[0, rs, cols] = (_gelu_tanh(u_ref[0, rs, cols]) * mixed).astype(y_ref.dtype)


def _gmlp(z3, u_cb, v_cb, ln_g, ln_b, ws, bs, rows=1024):
    nbatch, seq, _ = z3.shape
    w = GMLP_WIDTH
    t = GMLP_CHUNK
    blocks = [((1, rows, w), F32), ((1, rows, w), F32), ((GMLP_GROUPS, t, t), F32), ((1, rows, w), BF16)]
    return pl.pallas_call(
        _gmlp_kernel,
        grid=(nbatch, _tiles(seq, rows)),
        in_specs=[pl.BlockSpec((1, rows, w), lambda b, c: (b, c, u_cb)),
                  pl.BlockSpec((1, rows, w), lambda b, c: (b, c, v_cb)),
                  pl.BlockSpec((1, w), lambda b, c: (0, 0)),
                  pl.BlockSpec((1, w), lambda b, c: (0, 0)),
                  pl.BlockSpec((GMLP_GROUPS, t, t), lambda b, c: (0, 0, 0)),
                  pl.BlockSpec((t, GMLP_GROUPS), lambda b, c: (0, 0))],
        out_specs=pl.BlockSpec((1, rows, w), lambda b, c: (b, c, 0)),
        out_shape=jax.ShapeDtypeStruct((nbatch, seq, w), BF16),
        compiler_params=_params(("arbitrary", "arbitrary"), blocks),
        name="gmlp",
    )(z3, z3, ln_g.reshape(1, w), ln_b.reshape(1, w), ws, bs.T)


def _layer_weight_spec(layer, k, tn, index_of):
    return pl.BlockSpec((None, k, tn), lambda *idx: (layer,) + index_of(*idx))


def _merge_kernel(h_ref, wga_ref, wgb_ref, wgc_ref, ya_ref, yb_ref, yc_ref, wa_ref, wb_ref, wc_ref, o_ref):
    h = h_ref[...]

    def branch(wg_ref, y_ref, w_ref):
        gate = jax.nn.sigmoid(lax.dot_general(h, wg_ref[0].astype(BF16), NT_DIMS, preferred_element_type=F32))
        return gate * jnp.dot(y_ref[...], w_ref[...].astype(BF16), preferred_element_type=F32)

    merged = branch(wga_ref, ya_ref, wa_ref) + branch(wgb_ref, yb_ref, wb_ref) + branch(wgc_ref, yc_ref, wc_ref)
    o_ref[...] = merged.astype(o_ref.dtype)


def _merge(h, w_t, gate_row0, ya, yb, yc, wa, wb, wc, layer, tm=1024, tn=256):
    m, d = h.shape
    nblk = _tiles(d, tn)
    ka, kb, kc = ya.shape[1], yb.shape[1], yc.shape[1]
    blocks = ([((tm, d), BF16)] + [((tn, d), F32)] * 3
              + [((tm, ka), BF16), ((tm, kb), BF16), ((tm, kc), BF16)]
              + [((ka, tn), F32), ((kb, tn), F32), ((kc, tn), F32), ((tm, tn), BF16)])

    def gate_spec(branch):
        return _row_window(tn, d, lambda i, j: (layer, pl.multiple_of(gate_row0 + branch * d + j * tn, 8), 0))

    def col(i, j):
        return (0, j)

    return pl.pallas_call(
        _merge_kernel,
        grid=(_tiles(m, tm), nblk),
        in_specs=[pl.BlockSpec((tm, d), lambda i, j: (i, 0)),
                  gate_spec(0), gate_spec(1), gate_spec(2),
                  pl.BlockSpec((tm, ka), lambda i, j: (i, 0)),
                  pl.BlockSpec((tm, kb), lambda i, j: (i, 0)),
                  pl.BlockSpec((tm, kc), lambda i, j: (i, 0)),
                  _layer_weight_spec(layer, ka, tn, col),
                  _layer_weight_spec(layer, kb, tn, col),
                  _layer_weight_spec(layer, kc, tn, col)],
        out_specs=pl.BlockSpec((tm, tn), lambda i, j: (i, j)),
        out_shape=jax.ShapeDtypeStruct((m, d), BF16),
        compiler_params=_params(("arbitrary", "arbitrary"), blocks),
        name="merge",
    )(h, w_t, w_t, w_t, ya, yb, yc, wa, wb, wc)


def _residual_matmul_kernel(a_ref, w_ref, r_ref, o_ref):
    @pl.when(pl.program_id(2) == 0)
    def _():
        o_ref[...] = r_ref[...]

    o_ref[...] += jnp.dot(a_ref[...], w_ref[...].astype(BF16), preferred_element_type=F32)


def _residual_matmul(a, w, layer, res, tm=1024, tn=1024, tk=1024):
    m, k = a.shape
    n = w.shape[2]
    blocks = [((tm, tk), BF16), ((tk, tn), F32), ((tm, tn), F32), ((tm, tn), F32)]
    return pl.pallas_call(
        _residual_matmul_kernel,
        grid=(_tiles(m, tm), _tiles(n, tn), _tiles(k, tk)),
        in_specs=[pl.BlockSpec((tm, tk), lambda i, j, kk: (i, kk)),
                  _layer_weight_spec(layer, tk, tn, lambda i, j, kk: (kk, j)),
                  pl.BlockSpec((tm, tn), lambda i, j, kk: (i, j))],
        out_specs=pl.BlockSpec((tm, tn), lambda i, j, kk: (i, j)),
        out_shape=jax.ShapeDtypeStruct((m, n), F32),
        compiler_params=_params(("arbitrary", "arbitrary", "arbitrary"), blocks),
        name="residual_matmul",
    )(a, w, res)


def _out_proj_kernel(a_ref, w_ref, r_ref, g_ref, x_ref, h_ref):
    x_ref[...] = r_ref[...] + jnp.dot(a_ref[...], w_ref[...].astype(BF16), preferred_element_type=F32)
    _rmsnorm_rows(x_ref, g_ref, h_ref)


def _out_proj(a, w, layer, res, g, tm=512):
    m, k = a.shape
    d = w.shape[2]
    blocks = [((tm, k), BF16), ((k, d), w.dtype), ((tm, d), F32), ((tm, d), F32), ((tm, d), BF16)]
    return pl.pallas_call(
        _out_proj_kernel,
        grid=(_tiles(m, tm),),
        in_specs=[pl.BlockSpec((tm, k), lambda i: (i, 0)),
                  _layer_weight_spec(layer, k, d, lambda i: (0, 0)),
                  pl.BlockSpec((tm, d), lambda i: (i, 0)),
                  pl.BlockSpec((1, d), lambda i: (0, 0))],
        out_specs=[pl.BlockSpec((tm, d), lambda i: (i, 0)),
                   pl.BlockSpec((tm, d), lambda i: (i, 0))],
        out_shape=[jax.ShapeDtypeStruct((m, d), F32),
                   jax.ShapeDtypeStruct((m, d), BF16)],
        compiler_params=_params(("arbitrary",), blocks),
        name="out_proj",
    )(a, w, res, g.reshape(1, d))


def _relu2_matmul_kernel(h_ref, w_ref, o_ref):
    up = jnp.maximum(jnp.dot(h_ref[...], w_ref[...].astype(BF16), preferred_element_type=F32), 0.0)
    o_ref[...] = (up * up).astype(o_ref.dtype)


def _relu2_matmul(h, w, layer, tm=2048, tn=1024):
    m, k = h.shape
    n = w.shape[2]
    blocks = [((tm, k), BF16), ((k, tn), F32), ((tm, tn), BF16)]
    return pl.pallas_call(
        _relu2_matmul_kernel,
        grid=(_tiles(m, tm), _tiles(n, tn)),
        in_specs=[pl.BlockSpec((tm, k), lambda i, j: (i, 0)),
                  _layer_weight_spec(layer, k, tn, lambda i, j: (0, j))],
        out_specs=pl.BlockSpec((tm, tn), lambda i, j: (i, j)),
        out_shape=jax.ShapeDtypeStruct((m, n), BF16),
        compiler_params=_params(("arbitrary", "arbitrary"), blocks),
        name="mlp_up",
    )(h, w)


def _cast_kernel(w_ref, o_ref):
    o_ref[...] = w_ref[...].astype(o_ref.dtype)


def _cast_bf16(w, rows=512):
    depth, k, n = w.shape
    spec = pl.BlockSpec((None, rows, n), lambda l, r: (l, r, 0))
    return pl.pallas_call(
        _cast_kernel,
        grid=(depth, _tiles(k, rows)),
        in_specs=[spec],
        out_specs=spec,
        out_shape=jax.ShapeDtypeStruct(w.shape, BF16),
        compiler_params=_params(("arbitrary", "arbitrary"), [((rows, n), F32), ((rows, n), BF16)]),
        name="cast_bf16",
    )(w)


def _ple_kernel(x_ref, g_ref, wg_ref, p_ref, wp_ref, fg_ref, o_ref, h_sc, *, final_norm):
    _rmsnorm_rows(x_ref, g_ref, h_sc)
    gate = jax.nn.sigmoid(jnp.dot(h_sc[...], wg_ref[...].astype(BF16), preferred_element_type=F32))
    emb = jnp.dot(p_ref[...].astype(BF16), wp_ref[...].astype(BF16), preferred_element_type=F32)
    out = x_ref[...] + gate * emb
    if final_norm:
        out = out * lax.rsqrt(jnp.mean(out * out, axis=-1, keepdims=True) + NORM_EPS) * fg_ref[...]
    o_ref[...] = out


def _ple(x, g, w_gate, p, w_proj, layer, final_g, final_norm, tm=512):
    m, d = x.shape
    pd = p.shape[2]
    blocks = [((tm, d), F32), ((d, d), w_gate.dtype), ((tm, pd), F32), ((pd, d), F32), ((tm, d), F32)]
    scratch = [((tm, d), BF16)]
    return pl.pallas_call(
        functools.partial(_ple_kernel, final_norm=final_norm),
        grid=(_tiles(m, tm),),
        in_specs=[pl.BlockSpec((tm, d), lambda i: (i, 0)),
                  pl.BlockSpec((1, d), lambda i: (0, 0)),
                  _layer_weight_spec(layer, d, d, lambda i: (0, 0)),
                  pl.BlockSpec((None, tm, pd), lambda i: (layer, i, 0)),
                  _layer_weight_spec(layer, pd, d, lambda i: (0, 0)),
                  pl.BlockSpec((1, d), lambda i: (0, 0))],
        out_specs=pl.BlockSpec((tm, d), lambda i: (i, 0)),
        out_shape=jax.ShapeDtypeStruct((m, d), F32),
        scratch_shapes=[pltpu.VMEM(s, dt) for s, dt in scratch],
        compiler_params=_params(("arbitrary",), blocks, scratch),
        name="ple",
    )(x, g.reshape(1, d), w_gate, p, w_proj, final_g.reshape(1, d))


def kernel(x, p, norm_mix_g, w_in, mlstm_gate_b, mlstm_norm_g, gmlp_norm_g, gmlp_norm_b, gmlp_ws, gmlp_bs,
           w_branch_a, w_branch_b, w_branch_c, w_out, norm_mlp_g, w_mlp_up, w_mlp_down, norm_ple_g,
           w_ple_gate, w_ple_proj, final_norm_g):
    nbatch, seq, d = x.shape
    depth = w_in.shape[0]
    m = nbatch * seq
    assert d == MLSTM_WIDTH + MOBA_WIDTH + GMLP_WIDTH

    qkvo_a = 4 * MLSTM_WIDTH
    gates_if = 2 * MLSTM_HEADS
    main_b = qkvo_a + gates_if
    main_cols = 3 * MOBA_WIDTH + 2 * GMLP_WIDTH
    gate_off = main_b + main_cols
    assert w_in.shape[2] == gate_off + N_BRANCHES * d

    moba_q_cb = qkvo_a // MOBA_WIDTH
    moba_k_cb = moba_q_cb + 1
    moba_v_cb = moba_k_cb + 1
    gmlp_u_cb = (qkvo_a + 3 * MOBA_WIDTH) // GMLP_WIDTH
    gmlp_v_cb = gmlp_u_cb + 1

    z_cols = qkvo_a + main_cols
    colscale = jnp.ones((1, z_cols), F32).at[:, MLSTM_WIDTH:2 * MLSTM_WIDTH].set(HEAD_DIM ** -0.5)
    rope_tables = _rope_tables(seq) + _rope_tables_t(seq)

    xf = x.reshape(m, d)
    p_flat = p.reshape(depth, m, p.shape[-1])
    w_main, w_if, w_t = _win_split(w_in, qkvo_a, gates_if, z_cols)
    w_ple_gate_b = _cast_bf16(w_ple_gate)
    w_out_b = _cast_bf16(w_out)
    for i in range(depth):
        z, zif, h = _inproj(xf, norm_mix_g[i], w_main, w_if, i, colscale)
        z3 = z.reshape(nbatch, seq, z_cols)
        zif3 = zif.reshape(nbatch, seq, GATE_LANES)

        ya = _mlstm(z3, zif3, mlstm_gate_b[i], mlstm_norm_g[i])
        yb = _moba_flat(*_moba_prep_t(z3, rope_tables, moba_q_cb, moba_k_cb, moba_v_cb))
        yc = _gmlp(z3, gmlp_u_cb, gmlp_v_cb, gmlp_norm_g[i], gmlp_norm_b[i], gmlp_ws[i], gmlp_bs[i])

        merged = _merge(h, w_t, gate_off, ya.reshape(m, -1), yb.reshape(m, -1), yc.reshape(m, -1),
                        w_branch_a, w_branch_b, w_branch_c, i)
        xf, h2 = _out_proj(merged, w_out_b, i, xf, norm_mlp_g[i])
        hidden = _relu2_matmul(h2, w_mlp_up, i)
        xf = _residual_matmul(hidden, w_mlp_down, i, xf, tm=2048)
        xf = _ple(xf, norm_ple_g[i], w_ple_gate_b, p_flat, w_ple_proj, i, final_norm_g, i == depth - 1)

    return xf.reshape(nbatch, seq, d)
```

```python
import functools

import jax
import jax.numpy as jnp
import numpy as np
from jax import lax
from jax.experimental import pallas as pl
from jax.experimental.pallas import tpu as pltpu

F32 = jnp.float32
BF16 = jnp.bfloat16

HEAD_DIM = 128
MLSTM_HEADS = 4
MLSTM_WIDTH = MLSTM_HEADS * HEAD_DIM
MOBA_HEADS = 8
MOBA_WIDTH = MOBA_HEADS * HEAD_DIM
MOBA_BLOCK = 256
MOBA_TOPK = 3
ROPE_THETA = 500000.0
ROPE_DIM = HEAD_DIM // 4
GMLP_WIDTH = 512
GMLP_GROUPS = 4
GMLP_CHUNK = 128
N_BRANCHES = 3
NORM_EPS = 1e-6

LANES = 128
V7X_VMEM_BYTES = 64 * 1024 * 1024
VMEM_CEILING = V7X_VMEM_BYTES - 8 * 1024 * 1024

MLSTM_KERNEL_CHUNK = 128
MLSTM_CHUNKS_PER_STEP = 4
STATE_ROWS = 16
GATE_LANES = LANES
MASK_BIG = 2.0 ** 100
BIAS_ROWS = 8
MOBA_GROUP = 4

NT_DIMS = (((1,), (1,)), ((), ()))


def _tiles(n, t):
    count, rest = divmod(n, t)
    assert rest == 0 and count > 0, (n, t)
    return count


def _nbytes(shape, dtype):
    return int(np.prod(shape)) * jnp.dtype(dtype).itemsize


def _params(semantics, blocks, scratch=()):
    need = 2 * sum(_nbytes(s, d) for s, d in blocks) + sum(_nbytes(s, d) for s, d in scratch)
    limit = min(VMEM_CEILING, need + need // 4 + 4 * 1024 * 1024)
    return pltpu.CompilerParams(dimension_semantics=semantics, vmem_limit_bytes=limit)


def _row_window(rows, k, start_of):
    return pl.BlockSpec((pl.Element(1), pl.Element(rows), pl.Element(k)), start_of)


def _win_split_kernel(wm_ref, wif_ref, main_ref, if_ref):
    main_ref[...] = wm_ref[0].T.astype(BF16)
    _, n_if, k = wif_ref.shape
    padded = jnp.concatenate([wif_ref[0], jnp.zeros((GATE_LANES - n_if, k), F32)], axis=0)
    if_ref[...] = padded.T.astype(BF16)


def _win_split(w_in, head, n_if, main_cols, tn=512):
    depth, k, cols = w_in.shape
    assert head % tn == 0
    w_t = jnp.swapaxes(w_in, 1, 2)
    head_steps = head // tn
    blocks = [((tn, k), F32), ((n_if, k), F32), ((k, tn), BF16), ((k, GATE_LANES), BF16)]

    def main_rows(l, r):
        return (l, pl.multiple_of(r * tn + jnp.where(r >= head_steps, n_if, 0), n_if), 0)

    main, gates_if = pl.pallas_call(
        _win_split_kernel,
        grid=(depth, _tiles(main_cols, tn)),
        in_specs=[_row_window(tn, k, main_rows),
                  _row_window(n_if, k, lambda l, r: (l, head, 0))],
        out_specs=[pl.BlockSpec((None, k, tn), lambda l, r: (l, 0, r)),
                   pl.BlockSpec((None, k, GATE_LANES), lambda l, r: (l, 0, 0))],
        out_shape=[jax.ShapeDtypeStruct((depth, k, main_cols), BF16),
                   jax.ShapeDtypeStruct((depth, k, GATE_LANES), BF16)],
        compiler_params=_params(("arbitrary", "arbitrary"), blocks),
        name="win_split",
    )(w_t, w_t)
    return main, gates_if, w_t


def _rmsnorm_rows(x_ref, g_ref, h_ref, rows=256):
    for r0 in range(0, x_ref.shape[0], rows):
        x = x_ref[r0:r0 + rows, :]
        y = x * lax.rsqrt(jnp.mean(x * x, axis=-1, keepdims=True) + NORM_EPS)
        h_ref[r0:r0 + rows, :] = (y * g_ref[...]).astype(h_ref.dtype)


def _inproj_kernel(x_ref, g_ref, w_ref, wif_ref, cs_ref, z_ref, zif_ref, h_ref):
    @pl.when(pl.program_id(1) == 0)
    def _():
        _rmsnorm_rows(x_ref, g_ref, h_ref)
        zif_ref[...] = jnp.dot(h_ref[...], wif_ref[...], preferred_element_type=F32)

    acc = jnp.dot(h_ref[...], w_ref[...], preferred_element_type=F32)
    z_ref[...] = acc * cs_ref[...]


def _inproj(x, g, w_main, w_if, layer, colscale, tm=1024, tn=1024):
    m, k = x.shape
    n = w_main.shape[2]
    blocks = [((tm, k), F32), ((k, tn), BF16), ((k, GATE_LANES), BF16), ((1, tn), F32),
              ((tm, tn), F32), ((tm, GATE_LANES), F32), ((tm, k), BF16)]
    return pl.pallas_call(
        _inproj_kernel,
        grid=(_tiles(m, tm), _tiles(n, tn)),
        in_specs=[pl.BlockSpec((tm, k), lambda i, j: (i, 0)),
                  pl.BlockSpec((1, k), lambda i, j: (0, 0)),
                  pl.BlockSpec((None, k, tn), lambda i, j: (layer, 0, j)),
                  pl.BlockSpec((None, k, GATE_LANES), lambda i, j: (layer, 0, 0)),
                  pl.BlockSpec((1, tn), lambda i, j: (0, j))],
        out_specs=[pl.BlockSpec((tm, tn), lambda i, j: (i, j)),
                   pl.BlockSpec((tm, GATE_LANES), lambda i, j: (i, 0)),
                   pl.BlockSpec((tm, k), lambda i, j: (i, 0))],
        out_shape=[jax.ShapeDtypeStruct((m, n), F32),
                   jax.ShapeDtypeStruct((m, GATE_LANES), F32),
                   jax.ShapeDtypeStruct((m, k), BF16)],
        compiler_params=_params(("arbitrary", "arbitrary"), blocks),
        name="in_proj",
    )(x, g.reshape(1, k), w_main, w_if, colscale)


def _log_sigmoid(x):
    return jnp.minimum(x, 0.0) - jnp.log1p(jnp.exp(-jnp.abs(x)))


def _exact_tril_matmul(tril, x):
    tril = tril.astype(BF16)
    hi = x.astype(BF16)
    rest = x - hi.astype(F32)
    mid = rest.astype(BF16)
    lo = (rest - mid.astype(F32)).astype(BF16)
    return (jnp.dot(tril, hi, preferred_element_type=F32) + jnp.dot(tril, mid, preferred_element_type=F32)
            + jnp.dot(tril, lo, preferred_element_type=F32))


def _mlstm_kernel(q_ref, k_ref, v_ref, o_ref, zif_ref, gb_ref, ng_ref, y_ref, c_sc, n_sc, m_sc):
    @pl.when(pl.program_id(0) == 0)
    def _():
        c_sc[...] = jnp.zeros_like(c_sc)
        n_sc[...] = jnp.zeros_like(n_sc)
        m_sc[...] = jnp.zeros_like(m_sc)

    for c0 in range(0, q_ref.shape[1], MLSTM_KERNEL_CHUNK):
        rows = pl.ds(c0, MLSTM_KERNEL_CHUNK)
        _mlstm_chunk(q_ref.at[:, rows, :], k_ref.at[:, rows, :], v_ref.at[:, rows, :], o_ref.at[:, rows, :],
                     zif_ref.at[:, rows, :], gb_ref, ng_ref, y_ref.at[:, rows, :], c_sc, n_sc, m_sc)


def _mlstm_chunk(q_ref, k_ref, v_ref, o_ref, zif_ref, gb_ref, ng_ref, y_ref, c_sc, n_sc, m_sc):
    nbatch, chunk, _ = q_ref.shape
    heads = MLSTM_HEADS
    row = lax.broadcasted_iota(jnp.int32, (chunk, chunk), 0)
    col = lax.broadcasted_iota(jnp.int32, (chunk, chunk), 1)
    causal_t = row <= col
    tril = jnp.where(col <= row, 1.0, 0.0).astype(F32)
    lane = lax.broadcasted_iota(jnp.int32, (chunk, GATE_LANES), 1)

    for b in range(nbatch):
        pre = zif_ref[b] + gb_ref[...]
        gates = jnp.where(lane < heads, pre, _log_sigmoid(pre))
        gcum = _exact_tril_matmul(tril, gates)
        gates_t = gates.T
        gcum_t = gcum.T
        for h in range(heads):
            s = b * heads + h
            sl = slice(h * HEAD_DIM, (h + 1) * HEAD_DIM)
            q = q_ref[b, :, sl]
            k = k_ref[b, :, sl]
            v = v_ref[b, :, sl]
            q_t = q.T
            qb = q.astype(BF16)
            kb = k.astype(BF16)
            q_tb = q_t.astype(BF16)
            v_t = v.T
            g_row = gcum_t[heads + h:heads + h + 1, :]
            i_row = gates_t[h:h + 1, :]
            ig_col = gates[:, h:h + 1] - gcum[:, heads + h:heads + h + 1]
            m_prev = m_sc[s][:, 0:1]
            c_prev = c_sc[s]
            n_prev = n_sc[s]

            log_w = jnp.where(causal_t, g_row + ig_col, -jnp.inf)
            log_a = g_row + m_prev
            m_row = jnp.maximum(jnp.max(log_w, axis=0, keepdims=True), log_a)
            qk = lax.dot_general(kb, qb, NT_DIMS, preferred_element_type=F32) * jnp.exp(log_w - m_row)
            a = jnp.exp(log_a - m_row)
            num = (jnp.dot(v_t.astype(BF16), qk.astype(BF16), preferred_element_type=F32)
                   + a * jnp.dot(c_prev.astype(BF16), q_tb, preferred_element_type=F32))
            n_dot_q = jnp.dot(n_prev.astype(BF16), q_tb, preferred_element_type=F32)[0:1]
            den = jnp.sum(qk, axis=0, keepdims=True) + a * n_dot_q
            h_out = num / jnp.maximum(jnp.abs(den), jnp.exp(-m_row))

            g_last = g_row[:, chunk - 1:chunk]
            log_u = g_last - g_row + i_row
            m_new = jnp.maximum(g_last + m_prev, jnp.max(log_u, axis=1, keepdims=True))
            decay = jnp.exp(g_last + m_prev - m_new)
            u = jnp.exp(log_u - m_new)
            c_sc[s] = decay * c_prev + jnp.dot((v_t * u).astype(BF16), kb, preferred_element_type=F32)
            u_rows = jnp.broadcast_to(u, (STATE_ROWS, chunk)).astype(BF16)
            n_sc[s] = decay * n_prev + jnp.dot(u_rows, kb, preferred_element_type=F32)
            m_sc[s] = jnp.broadcast_to(m_new, (1, LANES))

            yn = (h_out * lax.rsqrt(jnp.mean(h_out * h_out, axis=0, keepdims=True) + NORM_EPS)).T
            y_ref[b, :, sl] = (jax.nn.sigmoid(o_ref[b, :, sl]) * (yn * ng_ref[:, sl])).astype(y_ref.dtype)


def _mlstm(z3, zif3, gate_b, norm_g):
    nbatch, seq, _ = z3.shape
    chunk = MLSTM_KERNEL_CHUNK * MLSTM_CHUNKS_PER_STEP
    w = MLSTM_WIDTH
    streams = nbatch * MLSTM_HEADS
    gb = jnp.pad(gate_b, (0, GATE_LANES - gate_b.shape[0])).reshape(1, GATE_LANES)
    blocks = [((nbatch, chunk, w), F32)] * 4 + [((nbatch, chunk, GATE_LANES), F32),
                                               ((nbatch, chunk, w), BF16)]
    scratch = [((streams, HEAD_DIM, HEAD_DIM), F32), ((streams, STATE_ROWS, LANES), F32),
               ((streams, 1, LANES), F32)]

    def zcol(cb):
        return pl.BlockSpec((nbatch, chunk, w), lambda c: (0, c, cb))

    return pl.pallas_call(
        _mlstm_kernel,
        grid=(_tiles(seq, chunk),),
        in_specs=[zcol(0), zcol(1), zcol(2), zcol(3),
                  pl.BlockSpec((nbatch, chunk, GATE_LANES), lambda c: (0, c, 0)),
                  pl.BlockSpec((1, GATE_LANES), lambda c: (0, 0)),
                  pl.BlockSpec((1, w), lambda c: (0, 0))],
        out_specs=pl.BlockSpec((nbatch, chunk, w), lambda c: (0, c, 0)),
        out_shape=jax.ShapeDtypeStruct((nbatch, seq, w), BF16),
        scratch_shapes=[pltpu.VMEM(s, d) for s, d in scratch],
        compiler_params=_params(("arbitrary",), blocks, scratch),
        name="mlstm",
    )(z3, z3, z3, z3, zif3, gb, norm_g.reshape(1, w))


def _rope_tables(seq):
    half = ROPE_DIM // 2
    inv_freq = ROPE_THETA ** (-jnp.arange(0, ROPE_DIM, 2, dtype=F32) / ROPE_DIM)
    ang = jnp.arange(seq, dtype=F32)[:, None] * inv_freq[None, :]
    cos = jnp.cos(ang)
    sin = jnp.sin(ang)
    ones = jnp.ones((seq, HEAD_DIM - ROPE_DIM), F32)
    cos_tab = jnp.concatenate([cos, cos, ones], axis=1)
    sin_tab = jnp.concatenate([-sin, sin, 0.0 * ones], axis=1)
    assert cos_tab.shape == (seq, HEAD_DIM) and half * 2 == ROPE_DIM
    return cos_tab, sin_tab


def _rotary(t, cos, sin):
    half = ROPE_DIM // 2
    lane = lax.broadcasted_iota(jnp.int32, t.shape, 1)
    upper = pltpu.roll(t, HEAD_DIM - half, axis=1)
    lower = pltpu.roll(t, half, axis=1)
    partner = jnp.where(lane < half, upper, lower)
    return jnp.where(lane < ROPE_DIM, t * cos + partner * sin, t)


def _rope_tables_t(seq):
    inv_freq = ROPE_THETA ** (-jnp.arange(0, ROPE_DIM, 2, dtype=F32) / ROPE_DIM)
    ang = jnp.arange(seq, dtype=F32)[:, None] * inv_freq[None, :]
    cos = jnp.cos(ang).T
    sin = jnp.sin(ang).T
    return jnp.concatenate([cos, cos], axis=0), jnp.concatenate([-sin, sin], axis=0)


def _moba_prep_t_kernel(q_ref, k_ref, v_ref, cos_ref, sin_ref, cost_ref, sint_ref,
                        qa_ref, kb_ref, vt_ref, bias_ref, km_sc):
    blk = q_ref.shape[1]
    d = HEAD_DIM
    half = ROPE_DIM // 2
    nsel = km_sc.shape[1]
    j = pl.program_id(1)

    @pl.when(j == 0)
    def _():
        km_sc[...] = jnp.zeros_like(km_sc)

    cos = cos_ref[...]
    sin = sin_ref[...]
    cos_t = cost_ref[...]
    sin_t = sint_ref[...]
    blk_id = lax.broadcasted_iota(jnp.int32, (nsel, blk), 0)
    blk_id_f = blk_id.astype(F32)
    mean_row = lax.broadcasted_iota(jnp.int32, (nsel, d), 0)
    for h in range(MOBA_HEADS):
        sl = slice(h * d, (h + 1) * d)
        q_t = q_ref[0, :, sl].T
        top = q_t[:ROPE_DIM]
        partner = jnp.concatenate([top[half:], top[:half]], axis=0)
        q_t = jnp.concatenate([top * cos_t + partner * sin_t, q_t[ROPE_DIM:]], axis=0)

        gate = jnp.dot(km_sc[h], q_t, preferred_element_type=F32,
                       precision=lax.Precision.HIGHEST)
        gate = jnp.where(blk_id < j, gate, -jnp.inf)
        sel_m1 = jnp.full((nsel, blk), -1.0, F32)
        for _ in range(MOBA_TOPK):
            mx = jnp.max(gate, axis=0, keepdims=True)
            first = jnp.min(jnp.where(gate == mx, blk_id_f, float(nsel)), axis=0, keepdims=True)
            first = jnp.where(mx > -jnp.inf, first, -1.0)
            hit = blk_id_f == first
            sel_m1 = jnp.where(hit, 0.0, sel_m1)
            gate = jnp.where(hit, -jnp.inf, gate)
        qa_ref[0, h, 0] = q_t.astype(BF16)
        for g in range(bias_ref.shape[3] // BIAS_ROWS):
            rows = sel_m1[g * MOBA_GROUP:(g + 1) * MOBA_GROUP] * MASK_BIG
            pad = jnp.zeros((BIAS_ROWS - MOBA_GROUP, blk), F32)
            bias_ref[0, h, 0, g * BIAS_ROWS:(g + 1) * BIAS_ROWS, :] = jnp.concatenate([rows, pad], axis=0)

        kk = _rotary(k_ref[0, :, sl], cos, sin)
        kb_ref[0, :, sl] = kk.astype(BF16)
        km_sc[h] = jnp.where(mean_row == j, jnp.mean(kk, axis=0, keepdims=True), km_sc[h])

        vt_ref[0, h, 0] = v_ref[0, :, sl].T.astype(BF16)


def _moba_prep_t(z3, tables, q_cb, k_cb, v_cb):
    nbatch, seq, _ = z3.shape
    blk = MOBA_BLOCK
    nblk = _tiles(seq, blk)
    nsel = -(-nblk // BIAS_ROWS) * BIAS_ROWS
    bias_rows = _tiles(nblk, MOBA_GROUP) * BIAS_ROWS
    w = MOBA_WIDTH
    d = HEAD_DIM
    hh = MOBA_HEADS
    cos_tab, sin_tab, cos_t, sin_t = tables
    blocks = ([((1, blk, w), F32)] * 3 + [((blk, d), F32)] * 2 + [((ROPE_DIM, blk), F32)] * 2
              + [((hh, d, blk), BF16), ((1, blk, w), BF16), ((hh, d, blk), BF16), ((hh, bias_rows, blk), F32)])
    scratch = [((hh, nsel, d), F32)]
    return pl.pallas_call(
        _moba_prep_t_kernel,
        grid=(nbatch, nblk),
        in_specs=[pl.BlockSpec((1, blk, w), lambda b, j: (b, j, q_cb)),
                  pl.BlockSpec((1, blk, w), lambda b, j: (b, j, k_cb)),
                  pl.BlockSpec((1, blk, w), lambda b, j: (b, j, v_cb)),
                  pl.BlockSpec((blk, d), lambda b, j: (j, 0)),
                  pl.BlockSpec((blk, d), lambda b, j: (j, 0)),
                  pl.BlockSpec((ROPE_DIM, blk), lambda b, j: (0, j)),
                  pl.BlockSpec((ROPE_DIM, blk), lambda b, j: (0, j))],
        out_specs=[pl.BlockSpec((1, hh, 1, d, blk), lambda b, j: (b, 0, j, 0, 0)),
                   pl.BlockSpec((1, blk, w), lambda b, j: (b, j, 0)),
                   pl.BlockSpec((1, hh, 1, d, blk), lambda b, j: (b, 0, j, 0, 0)),
                   pl.BlockSpec((1, hh, 1, bias_rows, blk), lambda b, j: (b, 0, j, 0, 0))],
        out_shape=[jax.ShapeDtypeStruct((nbatch, hh, nblk, d, blk), BF16),
                   jax.ShapeDtypeStruct((nbatch, seq, w), BF16),
                   jax.ShapeDtypeStruct((nbatch, hh, nblk, d, blk), BF16),
                   jax.ShapeDtypeStruct((nbatch, hh, nblk, bias_rows, blk), F32)],
        scratch_shapes=[pltpu.VMEM(s, dt) for s, dt in scratch],
        compiler_params=_params(("arbitrary", "arbitrary"), blocks, scratch),
        name="moba_prep",
    )(z3, z3, z3, cos_tab, sin_tab, cos_t, sin_t)


def _pipelined(start, count, first, step):
    lo = start % 2

    def pair(t, carry):
        i = start + 2 * t
        return step(i + 1, 1 - lo, step(i, lo, carry))

    carry = lax.fori_loop(0, (count - start) // 2, pair, first)
    if (count - start) % 2:
        carry = step(count - 1, (count - 1) % 2, carry)
    return carry


def _moba_flat_kernel(tile_ref, group_ref, q_ref, k_ref, bias_ref, vt_ref, o_ref,
                      s_sc, p_sc, acc_sc, m_sc, l_sc):
    nblk, d, blk = vt_ref.shape[2:]
    group = MOBA_GROUP
    gkeys = group * blk
    n_items = tile_ref.shape[0]
    c_exp = (d ** -0.5) * np.log2(np.e).astype(np.float32)
    key = lax.broadcasted_iota(jnp.int32, (blk, blk), 0)
    qry = lax.broadcasted_iota(jnp.int32, (blk, blk), 1)

    def weighted_values(first_blk, p):
        acc = None
        for i in range(p.shape[0] // blk):
            part = jnp.dot(vt_ref[0, 0, first_blk + i], p[i * blk:(i + 1) * blk],
                           preferred_element_type=F32)
            acc = part if acc is None else acc + part
        return acc

    def own_scores(j, slot):
        r0 = pl.multiple_of(j * blk, blk)
        s_t = jnp.dot(k_ref[0, pl.ds(r0, blk), :], q_ref[0, 0, j], preferred_element_type=F32)
        s_t = jnp.where(key <= qry, s_t, -jnp.inf)
        s_sc[slot, :blk, :] = s_t
        return jnp.max(s_t, axis=0, keepdims=True)

    def own_softmax(j, slot, s_max):
        next_max = own_scores(jnp.minimum(j + 1, nblk - 1), 1 - slot)
        p = jnp.exp2((s_sc[slot, :blk, :] - s_max) * c_exp)
        m_sc[j] = s_max
        l_sc[j] = jnp.sum(p, axis=0, keepdims=True)
        p_sc[slot] = p.astype(BF16)
        return next_max

    def own_values(j, slot):
        acc_sc[j] = weighted_values(j, p_sc[slot])

    def own_step(j, slot, s_max):
        own_values(j - 1, 1 - slot)
        return own_softmax(j, slot, s_max)

    _pipelined(1, nblk, own_softmax(0, 0, own_scores(0, 0)), own_step)
    own_values(nblk - 1, (nblk - 1) % 2)

    def group_scores(i, slot):
        j = tile_ref[i]
        g = group_ref[i]
        c0 = pl.multiple_of(g * gkeys, gkeys)
        s_t = jnp.dot(k_ref[0, pl.ds(c0, gkeys), :], q_ref[0, 0, j], preferred_element_type=F32)
        bias = bias_ref[0, 0, j, pl.ds(pl.multiple_of(g * BIAS_ROWS, BIAS_ROWS), BIAS_ROWS), :]
        s_t = jnp.concatenate([s_t[n * blk:(n + 1) * blk] + bias[n:n + 1] for n in range(group)],
                              axis=0)
        s_sc[slot] = s_t
        return jnp.max(s_t, axis=0, keepdims=True)

    def group_step(i, slot, s_max):
        next_max = group_scores(jnp.minimum(i + 1, n_items - 1), 1 - slot)
        j = tile_ref[i]
        m_run = m_sc[j]
        m_new = jnp.maximum(m_run, s_max)
        alpha = jnp.exp2((m_run - m_new) * c_exp)
        p = jnp.exp2((s_sc[slot] - m_new) * c_exp)
        m_sc[j] = m_new
        l_sc[j] = alpha * l_sc[j] + jnp.sum(p, axis=0, keepdims=True)
        acc_sc[j] = alpha * acc_sc[j] + weighted_values(group_ref[i] * group, p.astype(BF16))
        return next_max

    _pipelined(0, n_items, group_scores(0, 0), group_step)

    def finish(j, carry):
        r0 = pl.multiple_of(j * blk, blk)
        o_ref[0, pl.ds(r0, blk), :] = (acc_sc[j] / l_sc[j]).T.astype(o_ref.dtype)
        return carry

    lax.fori_loop(0, nblk, finish, 0)


def _moba_flat(q_t, kb, vt, bias):
    nbatch, seq, w = kb.shape
    blk = MOBA_BLOCK
    d = HEAD_DIM
    nblk = vt.shape[2]
    group = MOBA_GROUP
    bias_rows = bias.shape[3]
    items = [(j, g) for j in range(nblk) for g in range(-(-j // group))]
    assert items and _tiles(nblk, group)
    item_tile = jnp.asarray([j for j, _ in items], jnp.int32)
    item_group = jnp.asarray([g for _, g in items], jnp.int32)
    blocks = [((nblk, d, blk), BF16), ((1, seq, d), BF16), ((nblk, bias_rows, blk), F32), ((nblk, d, blk), BF16),
              ((1, seq, d), BF16)]
    scratch = [((2, group * blk, blk), F32), ((2, blk, blk), BF16), ((nblk, d, blk), F32),
               ((nblk, 1, blk), F32), ((nblk, 1, blk), F32)]
    grid_spec = pltpu.PrefetchScalarGridSpec(
        num_scalar_prefetch=2,
        grid=(nbatch, MOBA_HEADS),
        in_specs=[pl.BlockSpec((1, 1, nblk, d, blk), lambda b, h, *_: (b, h, 0, 0, 0)),
                  pl.BlockSpec((1, seq, d), lambda b, h, *_: (b, 0, h)),
                  pl.BlockSpec((1, 1, nblk, bias_rows, blk), lambda b, h, *_: (b, h, 0, 0, 0)),
                  pl.BlockSpec((1, 1, nblk, d, blk), lambda b, h, *_: (b, h, 0, 0, 0))],
        out_specs=pl.BlockSpec((1, seq, d), lambda b, h, *_: (b, 0, h)),
        scratch_shapes=[pltpu.VMEM(s, dt) for s, dt in scratch])
    return pl.pallas_call(
        _moba_flat_kernel,
        grid_spec=grid_spec,
        out_shape=jax.ShapeDtypeStruct((nbatch, seq, w), BF16),
        compiler_params=_params(("arbitrary", "arbitrary"), blocks, scratch),
        name="moba_attn",
    )(item_tile, item_group, q_t, kb, bias, vt)


def _gelu_tanh(x):
    c = np.sqrt(2.0 / np.pi).astype(np.float32)
    return x * (0.5 * (1.0 + jnp.tanh(c * (x + 0.044715 * (x * x * x)))))


def _gmlp_kernel(u_ref, v_ref, lg_ref, lb_ref, ws_ref, bst_ref, y_ref):
    rows = u_ref.shape[1]
    t = GMLP_CHUNK
    gd = GMLP_WIDTH // GMLP_GROUPS
    v = _gelu_tanh(v_ref[0])
    mu = jnp.mean(v, axis=-1, keepdims=True)
    vc = v - mu
    vln = vc * lax.rsqrt(jnp.mean(vc * vc, axis=-1, keepdims=True) + NORM_EPS) * lg_ref[...] + lb_ref[...]
    vb = vln.astype(BF16)
    row = lax.broadcasted_iota(jnp.int32, (t, t), 0)
    col = lax.broadcasted_iota(jnp.int32, (t, t), 1)
    for g in range(GMLP_GROUPS):
        wg = jnp.where(col <= row, ws_ref[g], 0.0).astype(BF16)
        bias = bst_ref[:, g:g + 1]
        cols = slice(g * gd, (g + 1) * gd)
        for c in range(rows // t):
            rs = slice(c * t, (c + 1) * t)
            mixed = jnp.dot(wg, vb[rs, cols], preferred_element_type=F32) + bias
            y_ref[0, rs, cols] = (_gelu_tanh(u_ref[0, rs, cols]) * mixed).astype(y_ref.dtype)


def _gmlp(z3, u_cb, v_cb, ln_g, ln_b, ws, bs, rows=1024):
    nbatch, seq, _ = z3.shape
    w = GMLP_WIDTH
    t = GMLP_CHUNK
    blocks = [((1, rows, w), F32), ((1, rows, w), F32), ((GMLP_GROUPS, t, t), F32), ((1, rows, w), BF16)]
    return pl.pallas_call(
        _gmlp_kernel,
        grid=(nbatch, _tiles(seq, rows)),
        in_specs=[pl.BlockSpec((1, rows, w), lambda b, c: (b, c, u_cb)),
                  pl.BlockSpec((1, rows, w), lambda b, c: (b, c, v_cb)),
                  pl.BlockSpec((1, w), lambda b, c: (0, 0)),
                  pl.BlockSpec((1, w), lambda b, c: (0, 0)),
                  pl.BlockSpec((GMLP_GROUPS, t, t), lambda b, c: (0, 0, 0)),
                  pl.BlockSpec((t, GMLP_GROUPS), lambda b, c: (0, 0))],
        out_specs=pl.BlockSpec((1, rows, w), lambda b, c: (b, c, 0)),
        out_shape=jax.ShapeDtypeStruct((nbatch, seq, w), BF16),
        compiler_params=_params(("arbitrary", "arbitrary"), blocks),
        name="gmlp",
    )(z3, z3, ln_g.reshape(1, w), ln_b.reshape(1, w), ws, bs.T)


def _layer_weight_spec(layer, k, tn, index_of):
    return pl.BlockSpec((None, k, tn), lambda *idx: (layer,) + index_of(*idx))


def _merge_kernel(h_ref, wga_ref, wgb_ref, wgc_ref, ya_ref, yb_ref, yc_ref, wa_ref, wb_ref, wc_ref, o_ref):
    h = h_ref[...]

    def branch(wg_ref, y_ref, w_ref):
        gate = jax.nn.sigmoid(lax.dot_general(h, wg_ref[0].astype(BF16), NT_DIMS, preferred_element_type=F32))
        return gate * jnp.dot(y_ref[...], w_ref[...].astype(BF16), preferred_element_type=F32)

    merged = branch(wga_ref, ya_ref, wa_ref) + branch(wgb_ref, yb_ref, wb_ref) + branch(wgc_ref, yc_ref, wc_ref)
    o_ref[...] = merged.astype(o_ref.dtype)


def _merge(h, w_t, gate_row0, ya, yb, yc, wa, wb, wc, layer, tm=1024, tn=256):
    m, d = h.shape
    nblk = _tiles(d, tn)
    ka, kb, kc = ya.shape[1], yb.shape[1], yc.shape[1]
    blocks = ([((tm, d), BF16)] + [((tn, d), F32)] * 3
              + [((tm, ka), BF16), ((tm, kb), BF16), ((tm, kc), BF16)]
              + [((ka, tn), F32), ((kb, tn), F32), ((kc, tn), F32), ((tm, tn), BF16)])

    def gate_spec(branch):
        return _row_window(tn, d, lambda i, j: (layer, pl.multiple_of(gate_row0 + branch * d + j * tn, 8), 0))

    def col(i, j):
        return (0, j)

    return pl.pallas_call(
        _merge_kernel,
        grid=(_tiles(m, tm), nblk),
        in_specs=[pl.BlockSpec((tm, d), lambda i, j: (i, 0)),
                  gate_spec(0), gate_spec(1), gate_spec(2),
                  pl.BlockSpec((tm, ka), lambda i, j: (i, 0)),
                  pl.BlockSpec((tm, kb), lambda i, j: (i, 0)),
                  pl.BlockSpec((tm, kc), lambda i, j: (i, 0)),
                  _layer_weight_spec(layer, ka, tn, col),
                  _layer_weight_spec(layer, kb, tn, col),
                  _layer_weight_spec(layer, kc, tn, col)],
        out_specs=pl.BlockSpec((tm, tn), lambda i, j: (i, j)),
        out_shape=jax.ShapeDtypeStruct((m, d), BF16),
        compiler_params=_params(("arbitrary", "arbitrary"), blocks),
        name="merge",
    )(h, w_t, w_t, w_t, ya, yb, yc, wa, wb, wc)


def _down_proj_kernel(a_ref, w_ref, o_ref):
    def part():
        return jnp.dot(a_ref[...], w_ref[...].astype(BF16), preferred_element_type=F32)

    @pl.when(pl.program_id(2) == 0)
    def _():
        o_ref[...] = part()

    @pl.when(pl.program_id(2) > 0)
    def _():
        o_ref[...] += part()


def _down_proj(a, w, layer, tm=2048, tn=1024, tk=1024):
    m, k = a.shape
    n = w.shape[2]
    blocks = [((tm, tk), BF16), ((tk, tn), F32), ((tm, tn), F32)]
    return pl.pallas_call(
        _down_proj_kernel,
        grid=(_tiles(m, tm), _tiles(n, tn), _tiles(k, tk)),
        in_specs=[pl.BlockSpec((tm, tk), lambda i, j, kk: (i, kk)),
                  _layer_weight_spec(layer, tk, tn, lambda i, j, kk: (kk, j))],
        out_specs=pl.BlockSpec((tm, tn), lambda i, j, kk: (i, j)),
        out_shape=jax.ShapeDtypeStruct((m, n), F32),
        compiler_params=_params(("arbitrary", "arbitrary", "arbitrary"), blocks),
        name="down_proj",
    )(a, w)


def _out_proj_kernel(a_ref, w_ref, r_ref, g_ref, x_ref, h_ref):
    x_ref[...] = r_ref[...] + jnp.dot(a_ref[...], w_ref[...].astype(BF16), preferred_element_type=F32)
    _rmsnorm_rows(x_ref, g_ref, h_ref)


def _out_proj(a, w, layer, res, g, tm=512):
    m, k = a.shape
    d = w.shape[2]
    blocks = [((tm, k), BF16), ((k, d), w.dtype), ((tm, d), F32), ((tm, d), F32), ((tm, d), BF16)]
    return pl.pallas_call(
        _out_proj_kernel,
        grid=(_tiles(m, tm),),
        in_specs=[pl.BlockSpec((tm, k), lambda i: (i, 0)),
                  _layer_weight_spec(layer, k, d, lambda i: (0, 0)),
                  pl.BlockSpec((tm, d), lambda i: (i, 0)),
                  pl.BlockSpec((1, d), lambda i: (0, 0))],
        out_specs=[pl.BlockSpec((tm, d), lambda i: (i, 0)),
                   pl.BlockSpec((tm, d), lambda i: (i, 0))],
        out_shape=[jax.ShapeDtypeStruct((m, d), F32),
                   jax.ShapeDtypeStruct((m, d), BF16)],
        compiler_params=_params(("arbitrary",), blocks),
        name="out_proj",
    )(a, w, res, g.reshape(1, d))


def _relu2_matmul_kernel(h_ref, w_ref, o_ref):
    up = jnp.maximum(jnp.dot(h_ref[...], w_ref[...].astype(BF16), preferred_element_type=F32), 0.0)
    o_ref[...] = (up * up).astype(o_ref.dtype)


def _relu2_matmul(h, w, layer, tm=2048, tn=1024):
    m, k = h.shape
    n = w.shape[2]
    blocks = [((tm, k), BF16), ((k, tn), F32), ((tm, tn), BF16)]
    return pl.pallas_call(
        _relu2_matmul_kernel,
        grid=(_tiles(m, tm), _tiles(n, tn)),
        in_specs=[pl.BlockSpec((tm, k), lambda i, j: (i, 0)),
                  _layer_weight_spec(layer, k, tn, lambda i, j: (0, j))],
        out_specs=pl.BlockSpec((tm, tn), lambda i, j: (i, j)),
        out_shape=jax.ShapeDtypeStruct((m, n), BF16),
        compiler_params=_params(("arbitrary", "arbitrary"), blocks),
        name="mlp_up",
    )(h, w)


def _cast_kernel(w_ref, o_ref):
    o_ref[...] = w_ref[...].astype(o_ref.dtype)


def _cast_bf16(w, rows=512):
    depth, k, n = w.shape
    spec = pl.BlockSpec((None, rows, n), lambda l, r: (l, r, 0))
    return pl.pallas_call(
        _cast_kernel,
        grid=(depth, _tiles(k, rows)),
        in_specs=[spec],
        out_specs=spec,
        out_shape=jax.ShapeDtypeStruct(w.shape, BF16),
        compiler_params=_params(("arbitrary", "arbitrary"), [((rows, n), F32), ((rows, n), BF16)]),
        name="cast_bf16",
    )(w)


def _ple_kernel(x_ref, dx_ref, g_ref, wg_ref, p_ref, wp_ref, fg_ref, o_ref, h_sc, *, final_norm):
    o_ref[...] = x_ref[...] + dx_ref[...]
    _rmsnorm_rows(o_ref, g_ref, h_sc)
    gate = jax.nn.sigmoid(jnp.dot(h_sc[...], wg_ref[...].astype(BF16), preferred_element_type=F32))
    emb = jnp.dot(p_ref[...].astype(BF16), wp_ref[...].astype(BF16), preferred_element_type=F32)
    out = o_ref[...] + gate * emb
    if final_norm:
        out = out * lax.rsqrt(jnp.mean(out * out, axis=-1, keepdims=True) + NORM_EPS) * fg_ref[...]
    o_ref[...] = out


def _ple(x, dx, g, w_gate, p, w_proj, layer, final_g, final_norm, tm=512):
    m, d = x.shape
    pd = p.shape[2]
    blocks = [((tm, d), F32), ((tm, d), F32), ((d, d), w_gate.dtype), ((tm, pd), F32), ((pd, d), F32),
              ((tm, d), F32)]
    scratch = [((tm, d), BF16)]
    return pl.pallas_call(
        functools.partial(_ple_kernel, final_norm=final_norm),
        grid=(_tiles(m, tm),),
        in_specs=[pl.BlockSpec((tm, d), lambda i: (i, 0)),
                  pl.BlockSpec((tm, d), lambda i: (i, 0)),
                  pl.BlockSpec((1, d), lambda i: (0, 0)),
                  _layer_weight_spec(layer, d, d, lambda i: (0, 0)),
                  pl.BlockSpec((None, tm, pd), lambda i: (layer, i, 0)),
                  _layer_weight_spec(layer, pd, d, lambda i: (0, 0)),
                  pl.BlockSpec((1, d), lambda i: (0, 0))],
        out_specs=pl.BlockSpec((tm, d), lambda i: (i, 0)),
        out_shape=jax.ShapeDtypeStruct((m, d), F32),
        scratch_shapes=[pltpu.VMEM(s, dt) for s, dt in scratch],
        compiler_params=_params(("arbitrary",), blocks, scratch),
        name="ple",
    )(x, dx, g.reshape(1, d), w_gate, p, w_proj, final_g.reshape(1, d))


def kernel(x, p, norm_mix_g, w_in, mlstm_gate_b, mlstm_norm_g, gmlp_norm_g, gmlp_norm_b, gmlp_ws, gmlp_bs,
           w_branch_a, w_branch_b, w_branch_c, w_out, norm_mlp_g, w_mlp_up, w_mlp_down, norm_ple_g,
           w_ple_gate, w_ple_proj, final_norm_g):
    nbatch, seq, d = x.shape
    depth = w_in.shape[0]
    m = nbatch * seq
    assert d == MLSTM_WIDTH + MOBA_WIDTH + GMLP_WIDTH

    qkvo_a = 4 * MLSTM_WIDTH
    gates_if = 2 * MLSTM_HEADS
    main_b = qkvo_a + gates_if
    main_cols = 3 * MOBA_WIDTH + 2 * GMLP_WIDTH
    gate_off = main_b + main_cols
    assert w_in.shape[2] == gate_off + N_BRANCHES * d

    moba_q_cb = qkvo_a // MOBA_WIDTH
    moba_k_cb = moba_q_cb + 1
    moba_v_cb = moba_k_cb + 1
    gmlp_u_cb = (qkvo_a + 3 * MOBA_WIDTH) // GMLP_WIDTH
    gmlp_v_cb = gmlp_u_cb + 1

    z_cols = qkvo_a + main_cols
    colscale = jnp.ones((1, z_cols), F32).at[:, MLSTM_WIDTH:2 * MLSTM_WIDTH].set(HEAD_DIM ** -0.5)
    rope_tables = _rope_tables(seq) + _rope_tables_t(seq)

    xf = x.reshape(m, d)
    p_flat = p.reshape(depth, m, p.shape[-1])
    w_main, w_if, w_t = _win_split(w_in, qkvo_a, gates_if, z_cols)
    w_ple_gate_b = _cast_bf16(w_ple_gate)
    w_out_b = _cast_bf16(w_out)
    for i in range(depth):
        z, zif, h = _inproj(xf, norm_mix_g[i], w_main, w_if, i, colscale)
        z3 = z.reshape(nbatch, seq, z_cols)
        zif3 = zif.reshape(nbatch, seq, GATE_LANES)

        ya = _mlstm(z3, zif3, mlstm_gate_b[i], mlstm_norm_g[i])
        yb = _moba_flat(*_moba_prep_t(z3, rope_tables, moba_q_cb, moba_k_cb, moba_v_cb))
        yc = _gmlp(z3, gmlp_u_cb, gmlp_v_cb, gmlp_norm_g[i], gmlp_norm_b[i], gmlp_ws[i], gmlp_bs[i])

        merged = _merge(h, w_t, gate_off, ya.reshape(m, -1), yb.reshape(m, -1), yc.reshape(m, -1),
                        w_branch_a, w_branch_b, w_branch_c, i)
        xf, h2 = _out_proj(merged, w_out_b, i, xf, norm_mlp_g[i])
        hidden = _relu2_matmul(h2, w_mlp_up, i)
        mlp_out = _down_proj(hidden, w_mlp_down, i)
        xf = _ple(xf, mlp_out, norm_ple_g[i], w_ple_gate_b, p_flat, w_ple_proj, i, final_norm_g, i == depth - 1)

    return xf.reshape(nbatch, seq, d)
```

```python
import functools

import jax
import jax.numpy as jnp
import numpy as np
from jax import lax
from jax.experimental import pallas as pl
from jax.experimental.pallas import tpu as pltpu

F32 = jnp.float32
BF16 = jnp.bfloat16

HEAD_DIM = 128
MLSTM_HEADS = 4
MLSTM_WIDTH = MLSTM_HEADS * HEAD_DIM
MOBA_HEADS = 8
MOBA_WIDTH = MOBA_HEADS * HEAD_DIM
MOBA_BLOCK = 256
MOBA_TOPK = 3
ROPE_THETA = 500000.0
ROPE_DIM = HEAD_DIM // 4
GMLP_WIDTH = 512
GMLP_GROUPS = 4
GMLP_CHUNK = 128
N_BRANCHES = 3
NORM_EPS = 1e-6

LANES = 128
V7X_VMEM_BYTES = 64 * 1024 * 1024
VMEM_CEILING = V7X_VMEM_BYTES - 8 * 1024 * 1024

MLSTM_KERNEL_CHUNK = 128
MLSTM_CHUNKS_PER_STEP = 4
STATE_ROWS = 16
GATE_LANES = LANES
MASK_BIG = 2.0 ** 100
BIAS_ROWS = 8
MOBA_GROUP = 4

NT_DIMS = (((1,), (1,)), ((), ()))
TN_DIMS = (((0,), (0,)), ((), ()))


def _tiles(n, t):
    count, rest = divmod(n, t)
    assert rest == 0 and count > 0, (n, t)
    return count


def _nbytes(shape, dtype):
    return int(np.prod(shape)) * jnp.dtype(dtype).itemsize


def _params(semantics, blocks, scratch=()):
    need = 2 * sum(_nbytes(s, d) for s, d in blocks) + sum(_nbytes(s, d) for s, d in scratch)
    limit = min(VMEM_CEILING, need + need // 4 + 4 * 1024 * 1024)
    return pltpu.CompilerParams(dimension_semantics=semantics, vmem_limit_bytes=limit)


def _row_window(rows, k, start_of):
    return pl.BlockSpec((pl.Element(1), pl.Element(rows), pl.Element(k)), start_of)


def _win_split_kernel(wm_ref, wif_ref, main_ref, if_ref):
    main_ref[...] = wm_ref[0].T.astype(BF16)
    _, n_if, k = wif_ref.shape
    padded = jnp.concatenate([wif_ref[0], jnp.zeros((GATE_LANES - n_if, k), F32)], axis=0)
    if_ref[...] = padded.T.astype(BF16)


def _win_split(w_in, head, n_if, main_cols, tn=512):
    depth, k, cols = w_in.shape
    assert head % tn == 0
    w_t = jnp.swapaxes(w_in, 1, 2)
    head_steps = head // tn
    blocks = [((tn, k), F32), ((n_if, k), F32), ((k, tn), BF16), ((k, GATE_LANES), BF16)]

    def main_rows(l, r):
        return (l, pl.multiple_of(r * tn + jnp.where(r >= head_steps, n_if, 0), n_if), 0)

    main, gates_if = pl.pallas_call(
        _win_split_kernel,
        grid=(depth, _tiles(main_cols, tn)),
        in_specs=[_row_window(tn, k, main_rows),
                  _row_window(n_if, k, lambda l, r: (l, head, 0))],
        out_specs=[pl.BlockSpec((None, k, tn), lambda l, r: (l, 0, r)),
                   pl.BlockSpec((None, k, GATE_LANES), lambda l, r: (l, 0, 0))],
        out_shape=[jax.ShapeDtypeStruct((depth, k, main_cols), BF16),
                   jax.ShapeDtypeStruct((depth, k, GATE_LANES), BF16)],
        compiler_params=_params(("arbitrary", "arbitrary"), blocks),
        name="win_split",
    )(w_t, w_t)
    return main, gates_if, w_t


def _rmsnorm_rows(x_ref, g_ref, h_ref, rows=256):
    for r0 in range(0, x_ref.shape[0], rows):
        x = x_ref[r0:r0 + rows, :]
        y = x * lax.rsqrt(jnp.mean(x * x, axis=-1, keepdims=True) + NORM_EPS)
        h_ref[r0:r0 + rows, :] = (y * g_ref[...]).astype(h_ref.dtype)


def _inproj_kernel(x_ref, g_ref, w_ref, wif_ref, cs_ref, z_ref, zif_ref, h_ref):
    @pl.when(pl.program_id(1) == 0)
    def _():
        _rmsnorm_rows(x_ref, g_ref, h_ref)
        zif_ref[...] = jnp.dot(h_ref[...], wif_ref[...], preferred_element_type=F32)

    acc = jnp.dot(h_ref[...], w_ref[...], preferred_element_type=F32)
    z_ref[...] = acc * cs_ref[...]


def _inproj(x, g, w_main, w_if, layer, colscale, tm=1024, tn=1024):
    m, k = x.shape
    n = w_main.shape[2]
    blocks = [((tm, k), F32), ((k, tn), BF16), ((k, GATE_LANES), BF16), ((1, tn), F32),
              ((tm, tn), F32), ((tm, GATE_LANES), F32), ((tm, k), BF16)]
    return pl.pallas_call(
        _inproj_kernel,
        grid=(_tiles(m, tm), _tiles(n, tn)),
        in_specs=[pl.BlockSpec((tm, k), lambda i, j: (i, 0)),
                  pl.BlockSpec((1, k), lambda i, j: (0, 0)),
                  pl.BlockSpec((None, k, tn), lambda i, j: (layer, 0, j)),
                  pl.BlockSpec((None, k, GATE_LANES), lambda i, j: (layer, 0, 0)),
                  pl.BlockSpec((1, tn), lambda i, j: (0, j))],
        out_specs=[pl.BlockSpec((tm, tn), lambda i, j: (i, j)),
                   pl.BlockSpec((tm, GATE_LANES), lambda i, j: (i, 0)),
                   pl.BlockSpec((tm, k), lambda i, j: (i, 0))],
        out_shape=[jax.ShapeDtypeStruct((m, n), F32),
                   jax.ShapeDtypeStruct((m, GATE_LANES), F32),
                   jax.ShapeDtypeStruct((m, k), BF16)],
        compiler_params=_params(("arbitrary", "arbitrary"), blocks),
        name="in_proj",
    )(x, g.reshape(1, k), w_main, w_if, colscale)


def _log_sigmoid(x):
    return jnp.minimum(x, 0.0) - jnp.log1p(jnp.exp(-jnp.abs(x)))


def _exact_tril_matmul(tril, x):
    tril = tril.astype(BF16)
    hi = x.astype(BF16)
    rest = x - hi.astype(F32)
    mid = rest.astype(BF16)
    lo = (rest - mid.astype(F32)).astype(BF16)
    return (jnp.dot(tril, hi, preferred_element_type=F32) + jnp.dot(tril, mid, preferred_element_type=F32)
            + jnp.dot(tril, lo, preferred_element_type=F32))


def _mlstm_kernel(q_ref, k_ref, v_ref, o_ref, zif_ref, gb_ref, ng_ref, y_ref, c_sc, n_sc, m_sc):
    @pl.when(pl.program_id(0) == 0)
    def _():
        c_sc[...] = jnp.zeros_like(c_sc)
        n_sc[...] = jnp.zeros_like(n_sc)
        m_sc[...] = jnp.zeros_like(m_sc)

    for c0 in range(0, q_ref.shape[1], MLSTM_KERNEL_CHUNK):
        rows = pl.ds(c0, MLSTM_KERNEL_CHUNK)
        _mlstm_chunk(q_ref.at[:, rows, :], k_ref.at[:, rows, :], v_ref.at[:, rows, :], o_ref.at[:, rows, :],
                     zif_ref.at[:, rows, :], gb_ref, ng_ref, y_ref.at[:, rows, :], c_sc, n_sc, m_sc)


def _mlstm_chunk(q_ref, k_ref, v_ref, o_ref, zif_ref, gb_ref, ng_ref, y_ref, c_sc, n_sc, m_sc):
    nbatch, chunk, _ = q_ref.shape
    heads = MLSTM_HEADS
    row = lax.broadcasted_iota(jnp.int32, (chunk, chunk), 0)
    col = lax.broadcasted_iota(jnp.int32, (chunk, chunk), 1)
    causal_t = row <= col
    tril = jnp.where(col <= row, 1.0, 0.0).astype(F32)
    lane = lax.broadcasted_iota(jnp.int32, (chunk, GATE_LANES), 1)

    for b in range(nbatch):
        pre = zif_ref[b] + gb_ref[...]
        gates = jnp.where(lane < heads, pre, _log_sigmoid(pre))
        gcum = _exact_tril_matmul(tril, gates)
        gcum_t = gcum.T
        for h in range(heads):
            s = b * heads + h
            sl = slice(h * HEAD_DIM, (h + 1) * HEAD_DIM)
            q = q_ref[b, :, sl]
            k = k_ref[b, :, sl]
            v = v_ref[b, :, sl]
            qb = q.astype(BF16)
            kb = k.astype(BF16)
            vb = v.astype(BF16)
            g_row = gcum_t[heads + h:heads + h + 1, :]
            ig_col = gates[:, h:h + 1] - gcum[:, heads + h:heads + h + 1]
            m_prev = m_sc[s][:, 0:1]
            c_prev = c_sc[s]
            n_prev = n_sc[s]

            log_w = jnp.where(causal_t, g_row + ig_col, -jnp.inf)
            log_a = g_row + m_prev
            m_row = jnp.maximum(jnp.max(log_w, axis=0, keepdims=True), log_a)
            qk = lax.dot_general(kb, qb, NT_DIMS, preferred_element_type=F32) * jnp.exp(log_w - m_row)
            a = jnp.exp(log_a - m_row)
            num = (lax.dot_general(vb, qk.astype(BF16), TN_DIMS, preferred_element_type=F32)
                   + a * lax.dot_general(c_prev.astype(BF16), qb, NT_DIMS,
                                         preferred_element_type=F32))
            n_dot_q = lax.dot_general(n_prev.astype(BF16), qb, NT_DIMS, preferred_element_type=F32)[0:1]
            den = jnp.sum(qk, axis=0, keepdims=True) + a * n_dot_q
            h_out = num / jnp.maximum(jnp.abs(den), jnp.exp(-m_row))

            g_last = g_row[:, chunk - 1:chunk]
            m_new = jnp.maximum(g_last + m_prev, g_last + jnp.max(ig_col, axis=0, keepdims=True))
            decay = jnp.exp(g_last + m_prev - m_new)
            uk = (k * jnp.exp(g_last + ig_col - m_new)).astype(BF16)
            c_sc[s] = decay * c_prev + lax.dot_general(vb, uk, TN_DIMS, preferred_element_type=F32)
            n_sc[s] = decay * n_prev + jnp.dot(jnp.ones((STATE_ROWS, chunk), BF16), uk,
                                               preferred_element_type=F32)
            m_sc[s] = jnp.broadcast_to(m_new, (1, LANES))

            yn = (h_out * lax.rsqrt(jnp.mean(h_out * h_out, axis=0, keepdims=True) + NORM_EPS)).T
            y_ref[b, :, sl] = (jax.nn.sigmoid(o_ref[b, :, sl]) * (yn * ng_ref[:, sl])).astype(y_ref.dtype)


def _mlstm(z3, zif3, gate_b, norm_g):
    nbatch, seq, _ = z3.shape
    chunk = MLSTM_KERNEL_CHUNK * MLSTM_CHUNKS_PER_STEP
    w = MLSTM_WIDTH
    streams = nbatch * MLSTM_HEADS
    gb = jnp.pad(gate_b, (0, GATE_LANES - gate_b.shape[0])).reshape(1, GATE_LANES)
    blocks = [((nbatch, chunk, w), F32)] * 4 + [((nbatch, chunk, GATE_LANES), F32),
                                               ((nbatch, chunk, w), BF16)]
    scratch = [((streams, HEAD_DIM, HEAD_DIM), F32), ((streams, STATE_ROWS, LANES), F32),
               ((streams, 1, LANES), F32)]

    def zcol(cb):
        return pl.BlockSpec((nbatch, chunk, w), lambda c: (0, c, cb))

    return pl.pallas_call(
        _mlstm_kernel,
        grid=(_tiles(seq, chunk),),
        in_specs=[zcol(0), zcol(1), zcol(2), zcol(3),
                  pl.BlockSpec((nbatch, chunk, GATE_LANES), lambda c: (0, c, 0)),
                  pl.BlockSpec((1, GATE_LANES), lambda c: (0, 0)),
                  pl.BlockSpec((1, w), lambda c: (0, 0))],
        out_specs=pl.BlockSpec((nbatch, chunk, w), lambda c: (0, c, 0)),
        out_shape=jax.ShapeDtypeStruct((nbatch, seq, w), BF16),
        scratch_shapes=[pltpu.VMEM(s, d) for s, d in scratch],
        compiler_params=_params(("arbitrary",), blocks, scratch),
        name="mlstm",
    )(z3, z3, z3, z3, zif3, gb, norm_g.reshape(1, w))


def _rope_tables(seq):
    half = ROPE_DIM // 2
    inv_freq = ROPE_THETA ** (-jnp.arange(0, ROPE_DIM, 2, dtype=F32) / ROPE_DIM)
    ang = jnp.arange(seq, dtype=F32)[:, None] * inv_freq[None, :]
    cos = jnp.cos(ang)
    sin = jnp.sin(ang)
    ones = jnp.ones((seq, HEAD_DIM - ROPE_DIM), F32)
    cos_tab = jnp.concatenate([cos, cos, ones], axis=1)
    sin_tab = jnp.concatenate([-sin, sin, 0.0 * ones], axis=1)
    assert cos_tab.shape == (seq, HEAD_DIM) and half * 2 == ROPE_DIM
    return cos_tab, sin_tab


def _rotary(t, cos, sin):
    half = ROPE_DIM // 2
    lane = lax.broadcasted_iota(jnp.int32, t.shape, 1)
    upper = pltpu.roll(t, HEAD_DIM - half, axis=1)
    lower = pltpu.roll(t, half, axis=1)
    partner = jnp.where(lane < half, upper, lower)
    return jnp.where(lane < ROPE_DIM, t * cos + partner * sin, t)


def _rope_tables_t(seq):
    inv_freq = ROPE_THETA ** (-jnp.arange(0, ROPE_DIM, 2, dtype=F32) / ROPE_DIM)
    ang = jnp.arange(seq, dtype=F32)[:, None] * inv_freq[None, :]
    cos = jnp.cos(ang).T
    sin = jnp.sin(ang).T
    return jnp.concatenate([cos, cos], axis=0), jnp.concatenate([-sin, sin], axis=0)


def _moba_prep_t_kernel(q_ref, k_ref, v_ref, cos_ref, sin_ref, cost_ref, sint_ref,
                        qa_ref, kb_ref, vt_ref, bias_ref, km_sc):
    blk = q_ref.shape[1]
    d = HEAD_DIM
    half = ROPE_DIM // 2
    nsel = km_sc.shape[1]
    j = pl.program_id(1)

    @pl.when(j == 0)
    def _():
        km_sc[...] = jnp.zeros_like(km_sc)

    cos = cos_ref[...]
    sin = sin_ref[...]
    cos_t = cost_ref[...]
    sin_t = sint_ref[...]
    blk_id = lax.broadcasted_iota(jnp.int32, (nsel, blk), 0)
    blk_id_f = blk_id.astype(F32)
    mean_row = lax.broadcasted_iota(jnp.int32, (nsel, d), 0)
    for h in range(MOBA_HEADS):
        sl = slice(h * d, (h + 1) * d)
        q_t = q_ref[0, :, sl].T
        top = q_t[:ROPE_DIM]
        partner = jnp.concatenate([top[half:], top[:half]], axis=0)
        q_t = jnp.concatenate([top * cos_t + partner * sin_t, q_t[ROPE_DIM:]], axis=0)

        gate = jnp.dot(km_sc[h], q_t, preferred_element_type=F32,
                       precision=lax.Precision.HIGHEST)
        gate = jnp.where(blk_id < j, gate, -jnp.inf)
        sel_m1 = jnp.full((nsel, blk), -1.0, F32)
        for _ in range(MOBA_TOPK):
            mx = jnp.max(gate, axis=0, keepdims=True)
            first = jnp.min(jnp.where(gate == mx, blk_id_f, float(nsel)), axis=0, keepdims=True)
            first = jnp.where(mx > -jnp.inf, first, -1.0)
            hit = blk_id_f == first
            sel_m1 = jnp.where(hit, 0.0, sel_m1)
            gate = jnp.where(hit, -jnp.inf, gate)
        qa_ref[0, h, 0] = q_t.astype(BF16)
        for g in range(bias_ref.shape[3] // BIAS_ROWS):
            rows = sel_m1[g * MOBA_GROUP:(g + 1) * MOBA_GROUP] * MASK_BIG
            pad = jnp.zeros((BIAS_ROWS - MOBA_GROUP, blk), F32)
            bias_ref[0, h, 0, g * BIAS_ROWS:(g + 1) * BIAS_ROWS, :] = jnp.concatenate([rows, pad], axis=0)

        kk = _rotary(k_ref[0, :, sl], cos, sin)
        kb_ref[0, :, sl] = kk.astype(BF16)
        km_sc[h] = jnp.where(mean_row == j, jnp.mean(kk, axis=0, keepdims=True), km_sc[h])

        vt_ref[0, h, 0] = v_ref[0, :, sl].T.astype(BF16)


def _moba_prep_t(z3, tables, q_cb, k_cb, v_cb):
    nbatch, seq, _ = z3.shape
    blk = MOBA_BLOCK
    nblk = _tiles(seq, blk)
    nsel = -(-nblk // BIAS_ROWS) * BIAS_ROWS
    bias_rows = _tiles(nblk, MOBA_GROUP) * BIAS_ROWS
    w = MOBA_WIDTH
    d = HEAD_DIM
    hh = MOBA_HEADS
    cos_tab, sin_tab, cos_t, sin_t = tables
    blocks = ([((1, blk, w), F32)] * 3 + [((blk, d), F32)] * 2 + [((ROPE_DIM, blk), F32)] * 2
              + [((hh, d, blk), BF16), ((1, blk, w), BF16), ((hh, d, blk), BF16), ((hh, bias_rows, blk), F32)])
    scratch = [((hh, nsel, d), F32)]
    return pl.pallas_call(
        _moba_prep_t_kernel,
        grid=(nbatch, nblk),
        in_specs=[pl.BlockSpec((1, blk, w), lambda b, j: (b, j, q_cb)),
                  pl.BlockSpec((1, blk, w), lambda b, j: (b, j, k_cb)),
                  pl.BlockSpec((1, blk, w), lambda b, j: (b, j, v_cb)),
                  pl.BlockSpec((blk, d), lambda b, j: (j, 0)),
                  pl.BlockSpec((blk, d), lambda b, j: (j, 0)),
                  pl.BlockSpec((ROPE_DIM, blk), lambda b, j: (0, j)),
                  pl.BlockSpec((ROPE_DIM, blk), lambda b, j: (0, j))],
        out_specs=[pl.BlockSpec((1, hh, 1, d, blk), lambda b, j: (b, 0, j, 0, 0)),
                   pl.BlockSpec((1, blk, w), lambda b, j: (b, j, 0)),
                   pl.BlockSpec((1, hh, 1, d, blk), lambda b, j: (b, 0, j, 0, 0)),
                   pl.BlockSpec((1, hh, 1, bias_rows, blk), lambda b, j: (b, 0, j, 0, 0))],
        out_shape=[jax.ShapeDtypeStruct((nbatch, hh, nblk, d, blk), BF16),
                   jax.ShapeDtypeStruct((nbatch, seq, w), BF16),
                   jax.ShapeDtypeStruct((nbatch, hh, nblk, d, blk), BF16),
                   jax.ShapeDtypeStruct((nbatch, hh, nblk, bias_rows, blk), F32)],
        scratch_shapes=[pltpu.VMEM(s, dt) for s, dt in scratch],
        compiler_params=_params(("arbitrary", "arbitrary"), blocks, scratch),
        name="moba_prep",
    )(z3, z3, z3, cos_tab, sin_tab, cos_t, sin_t)


def _pipelined(start, count, first, step):
    lo = start % 2

    def pair(t, carry):
        i = start + 2 * t
        return step(i + 1, 1 - lo, step(i, lo, carry))

    carry = lax.fori_loop(0, (count - start) // 2, pair, first)
    if (count - start) % 2:
        carry = step(count - 1, (count - 1) % 2, carry)
    return carry


def _moba_flat_kernel(tile_ref, group_ref, q_ref, k_ref, bias_ref, vt_ref, o_ref,
                      s_sc, p_sc, acc_sc, m_sc, l_sc):
    nblk, d, blk = vt_ref.shape[2:]
    group = MOBA_GROUP
    gkeys = group * blk
    n_items = tile_ref.shape[0]
    c_exp = (d ** -0.5) * np.log2(np.e).astype(np.float32)
    key = lax.broadcasted_iota(jnp.int32, (blk, blk), 0)
    qry = lax.broadcasted_iota(jnp.int32, (blk, blk), 1)

    def weighted_values(first_blk, p):
        acc = None
        for i in range(p.shape[0] // blk):
            part = jnp.dot(vt_ref[0, 0, first_blk + i], p[i * blk:(i + 1) * blk],
                           preferred_element_type=F32)
            acc = part if acc is None else acc + part
        return acc

    def own_scores(j, slot):
        r0 = pl.multiple_of(j * blk, blk)
        s_t = jnp.dot(k_ref[0, pl.ds(r0, blk), :], q_ref[0, 0, j], preferred_element_type=F32)
        s_t = jnp.where(key <= qry, s_t, -jnp.inf)
        s_sc[slot, :blk, :] = s_t
        return jnp.max(s_t, axis=0, keepdims=True)

    def own_softmax(j, slot, s_max):
        next_max = own_scores(jnp.minimum(j + 1, nblk - 1), 1 - slot)
        p = jnp.exp2((s_sc[slot, :blk, :] - s_max) * c_exp)
        m_sc[j] = s_max
        l_sc[j] = jnp.sum(p, axis=0, keepdims=True)
        p_sc[slot] = p.astype(BF16)
        return next_max

    def own_values(j, slot):
        acc_sc[j] = weighted_values(j, p_sc[slot])

    def own_step(j, slot, s_max):
        own_values(j - 1, 1 - slot)
        return own_softmax(j, slot, s_max)

    _pipelined(1, nblk, own_softmax(0, 0, own_scores(0, 0)), own_step)
    own_values(nblk - 1, (nblk - 1) % 2)

    def group_scores(i, slot):
        j = tile_ref[i]
        g = group_ref[i]
        c0 = pl.multiple_of(g * gkeys, gkeys)
        s_t = jnp.dot(k_ref[0, pl.ds(c0, gkeys), :], q_ref[0, 0, j], preferred_element_type=F32)
        bias = bias_ref[0, 0, j, pl.ds(pl.multiple_of(g * BIAS_ROWS, BIAS_ROWS), BIAS_ROWS), :]
        s_t = jnp.concatenate([s_t[n * blk:(n + 1) * blk] + bias[n:n + 1] for n in range(group)],
                              axis=0)
        s_sc[slot] = s_t
        return jnp.max(s_t, axis=0, keepdims=True)

    def group_step(i, slot, s_max):
        next_max = group_scores(jnp.minimum(i + 1, n_items - 1), 1 - slot)
        j = tile_ref[i]
        m_run = m_sc[j]
        m_new = jnp.maximum(m_run, s_max)
        alpha = jnp.exp2((m_run - m_new) * c_exp)
        p = jnp.exp2((s_sc[slot] - m_new) * c_exp)
        m_sc[j] = m_new
        l_sc[j] = alpha * l_sc[j] + jnp.sum(p, axis=0, keepdims=True)
        acc_sc[j] = alpha * acc_sc[j] + weighted_values(group_ref[i] * group, p.astype(BF16))
        return next_max

    _pipelined(0, n_items, group_scores(0, 0), group_step)

    def finish(j, carry):
        r0 = pl.multiple_of(j * blk, blk)
        o_ref[0, pl.ds(r0, blk), :] = (acc_sc[j] / l_sc[j]).T.astype(o_ref.dtype)
        return carry

    lax.fori_loop(0, nblk, finish, 0)


def _moba_flat(q_t, kb, vt, bias):
    nbatch, seq, w = kb.shape
    blk = MOBA_BLOCK
    d = HEAD_DIM
    nblk = vt.shape[2]
    group = MOBA_GROUP
    bias_rows = bias.shape[3]
    items = [(j, g) for j in range(nblk) for g in range(-(-j // group))]
    assert items and _tiles(nblk, group)
    item_tile = jnp.asarray([j for j, _ in items], jnp.int32)
    item_group = jnp.asarray([g for _, g in items], jnp.int32)
    blocks = [((nblk, d, blk), BF16), ((1, seq, d), BF16), ((nblk, bias_rows, blk), F32), ((nblk, d, blk), BF16),
              ((1, seq, d), BF16)]
    scratch = [((2, group * blk, blk), F32), ((2, blk, blk), BF16), ((nblk, d, blk), F32),
               ((nblk, 1, blk), F32), ((nblk, 1, blk), F32)]
    grid_spec = pltpu.PrefetchScalarGridSpec(
        num_scalar_prefetch=2,
        grid=(nbatch, MOBA_HEADS),
        in_specs=[pl.BlockSpec((1, 1, nblk, d, blk), lambda b, h, *_: (b, h, 0, 0, 0)),
                  pl.BlockSpec((1, seq, d), lambda b, h, *_: (b, 0, h)),
                  pl.BlockSpec((1, 1, nblk, bias_rows, blk), lambda b, h, *_: (b, h, 0, 0, 0)),
                  pl.BlockSpec((1, 1, nblk, d, blk), lambda b, h, *_: (b, h, 0, 0, 0))],
        out_specs=pl.BlockSpec((1, seq, d), lambda b, h, *_: (b, 0, h)),
        scratch_shapes=[pltpu.VMEM(s, dt) for s, dt in scratch])
    return pl.pallas_call(
        _moba_flat_kernel,
        grid_spec=grid_spec,
        out_shape=jax.ShapeDtypeStruct((nbatch, seq, w), BF16),
        compiler_params=_params(("arbitrary", "arbitrary"), blocks, scratch),
        name="moba_attn",
    )(item_tile, item_group, q_t, kb, bias, vt)


def _gelu_tanh(x):
    c = np.sqrt(2.0 / np.pi).astype(np.float32)
    return x * (0.5 * (1.0 + jnp.tanh(c * (x + 0.044715 * (x * x * x)))))


def _gmlp_kernel(u_ref, v_ref, lg_ref, lb_ref, ws_ref, bst_ref, y_ref):
    rows = u_ref.shape[1]
    t = GMLP_CHUNK
    gd = GMLP_WIDTH // GMLP_GROUPS
    v = _gelu_tanh(v_ref[0])
    mu = jnp.mean(v, axis=-1, keepdims=True)
    vc = v - mu
    vln = vc * lax.rsqrt(jnp.mean(vc * vc, axis=-1, keepdims=True) + NORM_EPS) * lg_ref[...] + lb_ref[...]
    vb = vln.astype(BF16)
    row = lax.broadcasted_iota(jnp.int32, (t, t), 0)
    col = lax.broadcasted_iota(jnp.int32, (t, t), 1)
    for g in range(GMLP_GROUPS):
        wg = jnp.where(col <= row, ws_ref[g], 0.0).astype(BF16)
        bias = bst_ref[:, g:g + 1]
        cols = slice(g * gd, (g + 1) * gd)
        for c in range(rows // t):
            rs = slice(c * t, (c + 1) * t)
            mixed = jnp.dot(wg, vb[rs, cols], preferred_element_type=F32) + bias
            y_ref[0, rs, cols] = (_gelu_tanh(u_ref[0, rs, cols]) * mixed).astype(y_ref.dtype)


def _gmlp(z3, u_cb, v_cb, ln_g, ln_b, ws, bs, rows=1024):
    nbatch, seq, _ = z3.shape
    w = GMLP_WIDTH
    t = GMLP_CHUNK
    blocks = [((1, rows, w), F32), ((1, rows, w), F32), ((GMLP_GROUPS, t, t), F32), ((1, rows, w), BF16)]
    return pl.pallas_call(
        _gmlp_kernel,
        grid=(nbatch, _tiles(seq, rows)),
        in_specs=[pl.BlockSpec((1, rows, w), lambda b, c: (b, c, u_cb)),
                  pl.BlockSpec((1, rows, w), lambda b, c: (b, c, v_cb)),
                  pl.BlockSpec((1, w), lambda b, c: (0, 0)),
                  pl.BlockSpec((1, w), lambda b, c: (0, 0)),
                  pl.BlockSpec((GMLP_GROUPS, t, t), lambda b, c: (0, 0, 0)),
                  pl.BlockSpec((t, GMLP_GROUPS), lambda b, c: (0, 0))],
        out_specs=pl.BlockSpec((1, rows, w), lambda b, c: (b, c, 0)),
        out_shape=jax.ShapeDtypeStruct((nbatch, seq, w), BF16),
        compiler_params=_params(("arbitrary", "arbitrary"), blocks),
        name="gmlp",
    )(z3, z3, ln_g.reshape(1, w), ln_b.reshape(1, w), ws, bs.T)


def _layer_weight_spec(layer, k, tn, index_of):
    return pl.BlockSpec((None, k, tn), lambda *idx: (layer,) + index_of(*idx))


def _merge_kernel(h_ref, wga_ref, wgb_ref, wgc_ref, ya_ref, yb_ref, yc_ref, wa_ref, wb_ref, wc_ref, o_ref):
    h = h_ref[...]

    def branch(wg_ref, y_ref, w_ref):
        gate = jax.nn.sigmoid(lax.dot_general(h, wg_ref[0].astype(BF16), NT_DIMS, preferred_element_type=F32))
        return gate * jnp.dot(y_ref[...], w_ref[...].astype(BF16), preferred_element_type=F32)

    merged = branch(wga_ref, ya_ref, wa_ref) + branch(wgb_ref, yb_ref, wb_ref) + branch(wgc_ref, yc_ref, wc_ref)
    o_ref[...] = merged.astype(o_ref.dtype)


def _merge(h, w_t, gate_row0, ya, yb, yc, wa, wb, wc, layer, tm=1024, tn=256):
    m, d = h.shape
    nblk = _tiles(d, tn)
    ka, kb, kc = ya.shape[1], yb.shape[1], yc.shape[1]
    blocks = ([((tm, d), BF16)] + [((tn, d), F32)] * 3
              + [((tm, ka), BF16), ((tm, kb), BF16), ((tm, kc), BF16)]
              + [((ka, tn), F32), ((kb, tn), F32), ((kc, tn), F32), ((tm, tn), BF16)])

    def gate_spec(branch):
        return _row_window(tn, d, lambda i, j: (layer, pl.multiple_of(gate_row0 + branch * d + j * tn, 8), 0))

    def col(i, j):
        return (0, j)

    return pl.pallas_call(
        _merge_kernel,
        grid=(_tiles(m, tm), nblk),
        in_specs=[pl.BlockSpec((tm, d), lambda i, j: (i, 0)),
                  gate_spec(0), gate_spec(1), gate_spec(2),
                  pl.BlockSpec((tm, ka), lambda i, j: (i, 0)),
                  pl.BlockSpec((tm, kb), lambda i, j: (i, 0)),
                  pl.BlockSpec((tm, kc), lambda i, j: (i, 0)),
                  _layer_weight_spec(layer, ka, tn, col),
                  _layer_weight_spec(layer, kb, tn, col),
                  _layer_weight_spec(layer, kc, tn, col)],
        out_specs=pl.BlockSpec((tm, tn), lambda i, j: (i, j)),
        out_shape=jax.ShapeDtypeStruct((m, d), BF16),
        compiler_params=_params(("arbitrary", "arbitrary"), blocks),
        name="merge",
    )(h, w_t, w_t, w_t, ya, yb, yc, wa, wb, wc)


def _residual_matmul_kernel(a_ref, w_ref, r_ref, o_ref):
    @pl.when(pl.program_id(2) == 0)
    def _():
        o_ref[...] = r_ref[...]

    o_ref[...] += jnp.dot(a_ref[...], w_ref[...].astype(BF16), preferred_element_type=F32)


def _residual_matmul(a, w, layer, res, tm=1024, tn=1024, tk=1024):
    m, k = a.shape
    n = w.shape[2]
    blocks = [((tm, tk), BF16), ((tk, tn), F32), ((tm, tn), F32), ((tm, tn), F32)]
    return pl.pallas_call(
        _residual_matmul_kernel,
        grid=(_tiles(m, tm), _tiles(n, tn), _tiles(k, tk)),
        in_specs=[pl.BlockSpec((tm, tk), lambda i, j, kk: (i, kk)),
                  _layer_weight_spec(layer, tk, tn, lambda i, j, kk: (kk, j)),
                  pl.BlockSpec((tm, tn), lambda i, j, kk: (i, j))],
        out_specs=pl.BlockSpec((tm, tn), lambda i, j, kk: (i, j)),
        out_shape=jax.ShapeDtypeStruct((m, n), F32),
        compiler_params=_params(("arbitrary", "arbitrary", "arbitrary"), blocks),
        name="residual_matmul",
    )(a, w, res)


def _out_proj_kernel(a_ref, w_ref, r_ref, g_ref, x_ref, h_ref):
    x_ref[...] = r_ref[...] + jnp.dot(a_ref[...], w_ref[...].astype(BF16), preferred_element_type=F32)
    _rmsnorm_rows(x_ref, g_ref, h_ref)


def _out_proj(a, w, layer, res, g, tm=512):
    m, k = a.shape
    d = w.shape[2]
    blocks = [((tm, k), BF16), ((k, d), w.dtype), ((tm, d), F32), ((tm, d), F32), ((tm, d), BF16)]
    return pl.pallas_call(
        _out_proj_kernel,
        grid=(_tiles(m, tm),),
        in_specs=[pl.BlockSpec((tm, k), lambda i: (i, 0)),
                  _layer_weight_spec(layer, k, d, lambda i: (0, 0)),
                  pl.BlockSpec((tm, d), lambda i: (i, 0)),
                  pl.BlockSpec((1, d), lambda i: (0, 0))],
        out_specs=[pl.BlockSpec((tm, d), lambda i: (i, 0)),
                   pl.BlockSpec((tm, d), lambda i: (i, 0))],
        out_shape=[jax.ShapeDtypeStruct((m, d), F32),
                   jax.ShapeDtypeStruct((m, d), BF16)],
        compiler_params=_params(("arbitrary",), blocks),
        name="out_proj",
    )(a, w, res, g.reshape(1, d))


def _relu2_matmul_kernel(h_ref, w_ref, o_ref):
    up = jnp.maximum(jnp.dot(h_ref[...], w_ref[...].astype(BF16), preferred_element_type=F32), 0.0)
    o_ref[...] = (up * up).astype(o_ref.dtype)


def _relu2_matmul(h, w, layer, tm=2048, tn=1024):
    m, k = h.shape
    n = w.shape[2]
    blocks = [((tm, k), BF16), ((k, tn), F32), ((tm, tn), BF16)]
    return pl.pallas_call(
        _relu2_matmul_kernel,
        grid=(_tiles(m, tm), _tiles(n, tn)),
        in_specs=[pl.BlockSpec((tm, k), lambda i, j: (i, 0)),
                  _layer_weight_spec(layer, k, tn, lambda i, j: (0, j))],
        out_specs=pl.BlockSpec((tm, tn), lambda i, j: (i, j)),
        out_shape=jax.ShapeDtypeStruct((m, n), BF16),
        compiler_params=_params(("arbitrary", "arbitrary"), blocks),
        name="mlp_up",
    )(h, w)


def _cast_kernel(w_ref, o_ref):
    o_ref[...] = w_ref[...].astype(o_ref.dtype)


def _cast_bf16(w, rows=512):
    depth, k, n = w.shape
    spec = pl.BlockSpec((None, rows, n), lambda l, r: (l, r, 0))
    return pl.pallas_call(
        _cast_kernel,
        grid=(depth, _tiles(k, rows)),
        in_specs=[spec],
        out_specs=spec,
        out_shape=jax.ShapeDtypeStruct(w.shape, BF16),
        compiler_params=_params(("arbitrary", "arbitrary"), [((rows, n), F32), ((rows, n), BF16)]),
        name="cast_bf16",
    )(w)


def _ple_kernel(x_ref, g_ref, wg_ref, p_ref, wp_ref, fg_ref, o_ref, h_sc, *, final_norm):
    _rmsnorm_rows(x_ref, g_ref, h_sc)
    gate = jax.nn.sigmoid(jnp.dot(h_sc[...], wg_ref[...].astype(BF16), preferred_element_type=F32))
    emb = jnp.dot(p_ref[...].astype(BF16), wp_ref[...].astype(BF16), preferred_element_type=F32)
    out = x_ref[...] + gate * emb
    if final_norm:
        out = out * lax.rsqrt(jnp.mean(out * out, axis=-1, keepdims=True) + NORM_EPS) * fg_ref[...]
    o_ref[...] = out


def _ple(x, g, w_gate, p, w_proj, layer, final_g, final_norm, tm=512):
    m, d = x.shape
    pd = p.shape[2]
    blocks = [((tm, d), F32), ((d, d), w_gate.dtype), ((tm, pd), F32), ((pd, d), F32), ((tm, d), F32)]
    scratch = [((tm, d), BF16)]
    return pl.pallas_call(
        functools.partial(_ple_kernel, final_norm=final_norm),
        grid=(_tiles(m, tm),),
        in_specs=[pl.BlockSpec((tm, d), lambda i: (i, 0)),
                  pl.BlockSpec((1, d), lambda i: (0, 0)),
                  _layer_weight_spec(layer, d, d, lambda i: (0, 0)),
                  pl.BlockSpec((None, tm, pd), lambda i: (layer, i, 0)),
                  _layer_weight_spec(layer, pd, d, lambda i: (0, 0)),
                  pl.BlockSpec((1, d), lambda i: (0, 0))],
        out_specs=pl.BlockSpec((tm, d), lambda i: (i, 0)),
        out_shape=jax.ShapeDtypeStruct((m, d), F32),
        scratch_shapes=[pltpu.VMEM(s, dt) for s, dt in scratch],
        compiler_params=_params(("arbitrary",), blocks, scratch),
        name="ple",
    )(x, g.reshape(1, d), w_gate, p, w_proj, final_g.reshape(1, d))


def kernel(x, p, norm_mix_g, w_in, mlstm_gate_b, mlstm_norm_g, gmlp_norm_g, gmlp_norm_b, gmlp_ws, gmlp_bs,
           w_branch_a, w_branch_b, w_branch_c, w_out, norm_mlp_g, w_mlp_up, w_mlp_down, norm_ple_g,
           w_ple_gate, w_ple_proj, final_norm_g):
    nbatch, seq, d = x.shape
    depth = w_in.shape[0]
    m = nbatch * seq
    assert d == MLSTM_WIDTH + MOBA_WIDTH + GMLP_WIDTH

    qkvo_a = 4 * MLSTM_WIDTH
    gates_if = 2 * MLSTM_HEADS
    main_b = qkvo_a + gates_if
    main_cols = 3 * MOBA_WIDTH + 2 * GMLP_WIDTH
    gate_off = main_b + main_cols
    assert w_in.shape[2] == gate_off + N_BRANCHES * d

    moba_q_cb = qkvo_a // MOBA_WIDTH
    moba_k_cb = moba_q_cb + 1
    moba_v_cb = moba_k_cb + 1
    gmlp_u_cb = (qkvo_a + 3 * MOBA_WIDTH) // GMLP_WIDTH
    gmlp_v_cb = gmlp_u_cb + 1

    z_cols = qkvo_a + main_cols
    colscale = jnp.ones((1, z_cols), F32).at[:, MLSTM_WIDTH:2 * MLSTM_WIDTH].set(HEAD_DIM ** -0.5)
    rope_tables = _rope_tables(seq) + _rope_tables_t(seq)

    xf = x.reshape(m, d)
    p_flat = p.reshape(depth, m, p.shape[-1])
    w_main, w_if, w_t = _win_split(w_in, qkvo_a, gates_if, z_cols)
    w_ple_gate_b = _cast_bf16(w_ple_gate)
    w_out_b = _cast_bf16(w_out)
    for i in range(depth):
        z, zif, h = _inproj(xf, norm_mix_g[i], w_main, w_if, i, colscale)
        z3 = z.reshape(nbatch, seq, z_cols)
        zif3 = zif.reshape(nbatch, seq, GATE_LANES)

        ya = _mlstm(z3, zif3, mlstm_gate_b[i], mlstm_norm_g[i])
        yb = _moba_flat(*_moba_prep_t(z3, rope_tables, moba_q_cb, moba_k_cb, moba_v_cb))
        yc = _gmlp(z3, gmlp_u_cb, gmlp_v_cb, gmlp_norm_g[i], gmlp_norm_b[i], gmlp_ws[i], gmlp_bs[i])

        merged = _merge(h, w_t, gate_off, ya.reshape(m, -1), yb.reshape(m, -1), yc.reshape(m, -1),
                        w_branch_a, w_branch_b, w_branch_c, i)
        xf, h2 = _out_proj(merged, w_out_b, i, xf, norm_mlp_g[i])
        hidden = _relu2_matmul(h2, w_mlp_up, i)
        xf = _residual_matmul(hidden, w_mlp_down, i, xf, tm=2048)
        xf = _ple(xf, norm_ple_g[i], w_ple_gate_b, p_flat, w_ple_proj, i, final_norm_g, i == depth - 1)

    return xf.reshape(nbatch, seq, d)
```

```python
import functools

import jax
import jax.numpy as jnp
import numpy as np
from jax import lax
from jax.experimental import pallas as pl
from jax.experimental.pallas import tpu as pltpu

F32 = jnp.float32
BF16 = jnp.bfloat16

HEAD_DIM = 128
MLSTM_HEADS = 4
MLSTM_WIDTH = MLSTM_HEADS * HEAD_DIM
MOBA_HEADS = 8
MOBA_WIDTH = MOBA_HEADS * HEAD_DIM
MOBA_BLOCK = 256
MOBA_TOPK = 3
ROPE_THETA = 500000.0
ROPE_DIM = HEAD_DIM // 4
GMLP_WIDTH = 512
GMLP_GROUPS = 4
GMLP_CHUNK = 128
N_BRANCHES = 3
NORM_EPS = 1e-6

LANES = 128
V7X_VMEM_BYTES = 64 * 1024 * 1024
VMEM_CEILING = V7X_VMEM_BYTES - 8 * 1024 * 1024

MLSTM_KERNEL_CHUNK = 128
MLSTM_CHUNKS_PER_STEP = 4
STATE_ROWS = 16
GATE_LANES = LANES
MASK_BIG = 2.0 ** 100
BIAS_ROWS = 8
MOBA_GROUP = 4

NT_DIMS = (((1,), (1,)), ((), ()))
TN_DIMS = (((0,), (0,)), ((), ()))


def _tiles(n, t):
    count, rest = divmod(n, t)
    assert rest == 0 and count > 0, (n, t)
    return count


def _nbytes(shape, dtype):
    return int(np.prod(shape)) * jnp.dtype(dtype).itemsize


def _params(semantics, blocks, scratch=()):
    need = 2 * sum(_nbytes(s, d) for s, d in blocks) + sum(_nbytes(s, d) for s, d in scratch)
    limit = min(VMEM_CEILING, need + need // 4 + 4 * 1024 * 1024)
    return pltpu.CompilerParams(dimension_semantics=semantics, vmem_limit_bytes=limit)


def _row_window(rows, k, start_of):
    return pl.BlockSpec((pl.Element(1), pl.Element(rows), pl.Element(k)), start_of)


def _win_split_kernel(wm_ref, wif_ref, main_ref, if_ref):
    main_ref[...] = wm_ref[0].T.astype(BF16)
    _, n_if, k = wif_ref.shape
    padded = jnp.concatenate([wif_ref[0], jnp.zeros((GATE_LANES - n_if, k), F32)], axis=0)
    if_ref[...] = padded.T.astype(BF16)


def _win_split(w_in, head, n_if, main_cols, tn=512):
    depth, k, cols = w_in.shape
    assert head % tn == 0
    w_t = jnp.swapaxes(w_in, 1, 2)
    head_steps = head // tn
    blocks = [((tn, k), F32), ((n_if, k), F32), ((k, tn), BF16), ((k, GATE_LANES), BF16)]

    def main_rows(l, r):
        return (l, pl.multiple_of(r * tn + jnp.where(r >= head_steps, n_if, 0), n_if), 0)

    main, gates_if = pl.pallas_call(
        _win_split_kernel,
        grid=(depth, _tiles(main_cols, tn)),
        in_specs=[_row_window(tn, k, main_rows),
                  _row_window(n_if, k, lambda l, r: (l, head, 0))],
        out_specs=[pl.BlockSpec((None, k, tn), lambda l, r: (l, 0, r)),
                   pl.BlockSpec((None, k, GATE_LANES), lambda l, r: (l, 0, 0))],
        out_shape=[jax.ShapeDtypeStruct((depth, k, main_cols), BF16),
                   jax.ShapeDtypeStruct((depth, k, GATE_LANES), BF16)],
        compiler_params=_params(("arbitrary", "arbitrary"), blocks),
        name="win_split",
    )(w_t, w_t)
    return main, gates_if, w_t


def _rmsnorm_rows(x_ref, g_ref, h_ref, rows=256):
    for r0 in range(0, x_ref.shape[0], rows):
        x = x_ref[r0:r0 + rows, :]
        y = x * lax.rsqrt(jnp.mean(x * x, axis=-1, keepdims=True) + NORM_EPS)
        h_ref[r0:r0 + rows, :] = (y * g_ref[...]).astype(h_ref.dtype)


def _inproj_kernel(x_ref, g_ref, w_ref, wif_ref, cs_ref, z_ref, zif_ref, h_ref):
    @pl.when(pl.program_id(1) == 0)
    def _():
        _rmsnorm_rows(x_ref, g_ref, h_ref)
        zif_ref[...] = jnp.dot(h_ref[...], wif_ref[...], preferred_element_type=F32)

    acc = jnp.dot(h_ref[...], w_ref[...], preferred_element_type=F32)
    z_ref[...] = acc * cs_ref[...]


def _inproj(x, g, w_main, w_if, layer, colscale, tm=1024, tn=1024):
    m, k = x.shape
    n = w_main.shape[2]
    blocks = [((tm, k), F32), ((k, tn), BF16), ((k, GATE_LANES), BF16), ((1, tn), F32),
              ((tm, tn), F32), ((tm, GATE_LANES), F32), ((tm, k), BF16)]
    return pl.pallas_call(
        _inproj_kernel,
        grid=(_tiles(m, tm), _tiles(n, tn)),
        in_specs=[pl.BlockSpec((tm, k), lambda i, j: (i, 0)),
                  pl.BlockSpec((1, k), lambda i, j: (0, 0)),
                  pl.BlockSpec((None, k, tn), lambda i, j: (layer, 0, j)),
                  pl.BlockSpec((None, k, GATE_LANES), lambda i, j: (layer, 0, 0)),
                  pl.BlockSpec((1, tn), lambda i, j: (0, j))],
        out_specs=[pl.BlockSpec((tm, tn), lambda i, j: (i, j)),
                   pl.BlockSpec((tm, GATE_LANES), lambda i, j: (i, 0)),
                   pl.BlockSpec((tm, k), lambda i, j: (i, 0))],
        out_shape=[jax.ShapeDtypeStruct((m, n), F32),
                   jax.ShapeDtypeStruct((m, GATE_LANES), F32),
                   jax.ShapeDtypeStruct((m, k), BF16)],
        compiler_params=_params(("arbitrary", "arbitrary"), blocks),
        name="in_proj",
    )(x, g.reshape(1, k), w_main, w_if, colscale)


def _log_sigmoid(x):
    return jnp.minimum(x, 0.0) - jnp.log1p(jnp.exp(-jnp.abs(x)))


def _exact_tril_matmul(tril, x):
    tril = tril.astype(BF16)
    hi = x.astype(BF16)
    rest = x - hi.astype(F32)
    mid = rest.astype(BF16)
    lo = (rest - mid.astype(F32)).astype(BF16)
    return (jnp.dot(tril, hi, preferred_element_type=F32) + jnp.dot(tril, mid, preferred_element_type=F32)
            + jnp.dot(tril, lo, preferred_element_type=F32))


def _mlstm_kernel(q_ref, k_ref, v_ref, o_ref, zif_ref, gb_ref, ng_ref, y_ref, c_sc, n_sc, m_sc):
    @pl.when(pl.program_id(0) == 0)
    def _():
        c_sc[...] = jnp.zeros_like(c_sc)
        n_sc[...] = jnp.zeros_like(n_sc)
        m_sc[...] = jnp.zeros_like(m_sc)

    for c0 in range(0, q_ref.shape[1], MLSTM_KERNEL_CHUNK):
        rows = pl.ds(c0, MLSTM_KERNEL_CHUNK)
        _mlstm_chunk(q_ref.at[:, rows, :], k_ref.at[:, rows, :], v_ref.at[:, rows, :], o_ref.at[:, rows, :],
                     zif_ref.at[:, rows, :], gb_ref, ng_ref, y_ref.at[:, rows, :], c_sc, n_sc, m_sc)


def _mlstm_chunk(q_ref, k_ref, v_ref, o_ref, zif_ref, gb_ref, ng_ref, y_ref, c_sc, n_sc, m_sc):
    nbatch, chunk, _ = q_ref.shape
    heads = MLSTM_HEADS
    row = lax.broadcasted_iota(jnp.int32, (chunk, chunk), 0)
    col = lax.broadcasted_iota(jnp.int32, (chunk, chunk), 1)
    causal_t = row <= col
    tril = jnp.where(col <= row, 1.0, 0.0).astype(F32)
    lane = lax.broadcasted_iota(jnp.int32, (chunk, GATE_LANES), 1)

    for b in range(nbatch):
        pre = zif_ref[b] + gb_ref[...]
        gates = jnp.where(lane < heads, pre, _log_sigmoid(pre))
        gcum = _exact_tril_matmul(tril, gates)
        gcum_t = gcum.T
        for h in range(heads):
            s = b * heads + h
            sl = slice(h * HEAD_DIM, (h + 1) * HEAD_DIM)
            q = q_ref[b, :, sl]
            k = k_ref[b, :, sl]
            v = v_ref[b, :, sl]
            qb = q.astype(BF16)
            kb = k.astype(BF16)
            vb = v.astype(BF16)
            g_row = gcum_t[heads + h:heads + h + 1, :]
            ig_col = gates[:, h:h + 1] - gcum[:, heads + h:heads + h + 1]
            m_prev = m_sc[s][:, 0:1]
            c_prev = c_sc[s]
            n_prev = n_sc[s]

            log_w = jnp.where(causal_t, g_row + ig_col, -jnp.inf)
            log_a = g_row + m_prev
            m_row = jnp.maximum(jnp.max(log_w, axis=0, keepdims=True), log_a)
            qk = lax.dot_general(kb, qb, NT_DIMS, preferred_element_type=F32) * jnp.exp(log_w - m_row)
            a = jnp.exp(log_a - m_row)
            num = (lax.dot_general(vb, qk.astype(BF16), TN_DIMS, preferred_element_type=F32)
                   + a * lax.dot_general(c_prev.astype(BF16), qb, NT_DIMS,
                                         preferred_element_type=F32))
            n_dot_q = lax.dot_general(n_prev.astype(BF16), qb, NT_DIMS, preferred_element_type=F32)[0:1]
            den = jnp.sum(qk, axis=0, keepdims=True) + a * n_dot_q
            h_out = num / jnp.maximum(jnp.abs(den), jnp.exp(-m_row))

            g_last = g_row[:, chunk - 1:chunk]
            m_new = jnp.maximum(g_last + m_prev, g_last + jnp.max(ig_col, axis=0, keepdims=True))
            decay = jnp.exp(g_last + m_prev - m_new)
            uk = (k * jnp.exp(g_last + ig_col - m_new)).astype(BF16)
            c_sc[s] = decay * c_prev + lax.dot_general(vb, uk, TN_DIMS, preferred_element_type=F32)
            n_sc[s] = decay * n_prev + jnp.dot(jnp.ones((STATE_ROWS, chunk), BF16), uk,
                                               preferred_element_type=F32)
            m_sc[s] = jnp.broadcast_to(m_new, (1, LANES))

            yn = (h_out * lax.rsqrt(jnp.mean(h_out * h_out, axis=0, keepdims=True) + NORM_EPS)).T
            y_ref[b, :, sl] = (jax.nn.sigmoid(o_ref[b, :, sl]) * (yn * ng_ref[:, sl])).astype(y_ref.dtype)


def _mlstm(z3, zif3, gate_b, norm_g):
    nbatch, seq, _ = z3.shape
    chunk = MLSTM_KERNEL_CHUNK * MLSTM_CHUNKS_PER_STEP
    w = MLSTM_WIDTH
    streams = nbatch * MLSTM_HEADS
    gb = jnp.pad(gate_b, (0, GATE_LANES - gate_b.shape[0])).reshape(1, GATE_LANES)
    blocks = [((nbatch, chunk, w), F32)] * 4 + [((nbatch, chunk, GATE_LANES), F32),
                                               ((nbatch, chunk, w), BF16)]
    scratch = [((streams, HEAD_DIM, HEAD_DIM), F32), ((streams, STATE_ROWS, LANES), F32),
               ((streams, 1, LANES), F32)]

    def zcol(cb):
        return pl.BlockSpec((nbatch, chunk, w), lambda c: (0, c, cb))

    return pl.pallas_call(
        _mlstm_kernel,
        grid=(_tiles(seq, chunk),),
        in_specs=[zcol(0), zcol(1), zcol(2), zcol(3),
                  pl.BlockSpec((nbatch, chunk, GATE_LANES), lambda c: (0, c, 0)),
                  pl.BlockSpec((1, GATE_LANES), lambda c: (0, 0)),
                  pl.BlockSpec((1, w), lambda c: (0, 0))],
        out_specs=pl.BlockSpec((nbatch, chunk, w), lambda c: (0, c, 0)),
        out_shape=jax.ShapeDtypeStruct((nbatch, seq, w), BF16),
        scratch_shapes=[pltpu.VMEM(s, d) for s, d in scratch],
        compiler_params=_params(("arbitrary",), blocks, scratch),
        name="mlstm",
    )(z3, z3, z3, z3, zif3, gb, norm_g.reshape(1, w))


def _rope_tables(seq):
    half = ROPE_DIM // 2
    inv_freq = ROPE_THETA ** (-jnp.arange(0, ROPE_DIM, 2, dtype=F32) / ROPE_DIM)
    ang = jnp.arange(seq, dtype=F32)[:, None] * inv_freq[None, :]
    cos = jnp.cos(ang)
    sin = jnp.sin(ang)
    ones = jnp.ones((seq, HEAD_DIM - ROPE_DIM), F32)
    cos_tab = jnp.concatenate([cos, cos, ones], axis=1)
    sin_tab = jnp.concatenate([-sin, sin, 0.0 * ones], axis=1)
    assert cos_tab.shape == (seq, HEAD_DIM) and half * 2 == ROPE_DIM
    return cos_tab, sin_tab


def _rotary(t, cos, sin):
    half = ROPE_DIM // 2
    lane = lax.broadcasted_iota(jnp.int32, t.shape, 1)
    upper = pltpu.roll(t, HEAD_DIM - half, axis=1)
    lower = pltpu.roll(t, half, axis=1)
    partner = jnp.where(lane < half, upper, lower)
    return jnp.where(lane < ROPE_DIM, t * cos + partner * sin, t)


def _rope_tables_t(seq):
    inv_freq = ROPE_THETA ** (-jnp.arange(0, ROPE_DIM, 2, dtype=F32) / ROPE_DIM)
    ang = jnp.arange(seq, dtype=F32)[:, None] * inv_freq[None, :]
    cos = jnp.cos(ang).T
    sin = jnp.sin(ang).T
    return jnp.concatenate([cos, cos], axis=0), jnp.concatenate([-sin, sin], axis=0)


def _moba_prep_t_kernel(q_ref, k_ref, v_ref, cos_ref, sin_ref, cost_ref, sint_ref,
                        qa_ref, kb_ref, vt_ref, bias_ref, km_sc):
    blk = q_ref.shape[1]
    d = HEAD_DIM
    half = ROPE_DIM // 2
    nsel = km_sc.shape[1]
    j = pl.program_id(1)

    @pl.when(j == 0)
    def _():
        km_sc[...] = jnp.zeros_like(km_sc)

    cos = cos_ref[...]
    sin = sin_ref[...]
    cos_t = cost_ref[...]
    sin_t = sint_ref[...]
    blk_id = lax.broadcasted_iota(jnp.int32, (nsel, blk), 0)
    blk_id_f = blk_id.astype(F32)
    mean_row = lax.broadcasted_iota(jnp.int32, (nsel, d), 0)
    for h in range(MOBA_HEADS):
        sl = slice(h * d, (h + 1) * d)
        q_t = q_ref[0, :, sl].T
        top = q_t[:ROPE_DIM]
        partner = jnp.concatenate([top[half:], top[:half]], axis=0)
        q_t = jnp.concatenate([top * cos_t + partner * sin_t, q_t[ROPE_DIM:]], axis=0)

        gate = jnp.dot(km_sc[h], q_t, preferred_element_type=F32,
                       precision=lax.Precision.HIGHEST)
        gate = jnp.where(blk_id < j, gate, -jnp.inf)
        sel_m1 = jnp.full((nsel, blk), -1.0, F32)
        for _ in range(MOBA_TOPK):
            mx = jnp.max(gate, axis=0, keepdims=True)
            first = jnp.min(jnp.where(gate == mx, blk_id_f, float(nsel)), axis=0, keepdims=True)
            first = jnp.where(mx > -jnp.inf, first, -1.0)
            hit = blk_id_f == first
            sel_m1 = jnp.where(hit, 0.0, sel_m1)
            gate = jnp.where(hit, -jnp.inf, gate)
        qa_ref[0, h, 0] = q_t.astype(BF16)
        for g in range(bias_ref.shape[3] // BIAS_ROWS):
            rows = sel_m1[g * MOBA_GROUP:(g + 1) * MOBA_GROUP] * MASK_BIG
            pad = jnp.zeros((BIAS_ROWS - MOBA_GROUP, blk), F32)
            bias_ref[0, h, 0, g * BIAS_ROWS:(g + 1) * BIAS_ROWS, :] = jnp.concatenate([rows, pad], axis=0)

        kk = _rotary(k_ref[0, :, sl], cos, sin)
        kb_ref[0, :, sl] = kk.astype(BF16)
        km_sc[h] = jnp.where(mean_row == j, jnp.mean(kk, axis=0, keepdims=True), km_sc[h])

        vt_ref[0, h, 0] = v_ref[0, :, sl].T.astype(BF16)


def _moba_prep_t(z3, tables, q_cb, k_cb, v_cb):
    nbatch, seq, _ = z3.shape
    blk = MOBA_BLOCK
    nblk = _tiles(seq, blk)
    nsel = -(-nblk // BIAS_ROWS) * BIAS_ROWS
    bias_rows = _tiles(nblk, MOBA_GROUP) * BIAS_ROWS
    w = MOBA_WIDTH
    d = HEAD_DIM
    hh = MOBA_HEADS
    cos_tab, sin_tab, cos_t, sin_t = tables
    blocks = ([((1, blk, w), F32)] * 3 + [((blk, d), F32)] * 2 + [((ROPE_DIM, blk), F32)] * 2
              + [((hh, d, blk), BF16), ((1, blk, w), BF16), ((hh, d, blk), BF16), ((hh, bias_rows, blk), F32)])
    scratch = [((hh, nsel, d), F32)]
    return pl.pallas_call(
        _moba_prep_t_kernel,
        grid=(nbatch, nblk),
        in_specs=[pl.BlockSpec((1, blk, w), lambda b, j: (b, j, q_cb)),
                  pl.BlockSpec((1, blk, w), lambda b, j: (b, j, k_cb)),
                  pl.BlockSpec((1, blk, w), lambda b, j: (b, j, v_cb)),
                  pl.BlockSpec((blk, d), lambda b, j: (j, 0)),
                  pl.BlockSpec((blk, d), lambda b, j: (j, 0)),
                  pl.BlockSpec((ROPE_DIM, blk), lambda b, j: (0, j)),
                  pl.BlockSpec((ROPE_DIM, blk), lambda b, j: (0, j))],
        out_specs=[pl.BlockSpec((1, hh, 1, d, blk), lambda b, j: (b, 0, j, 0, 0)),
                   pl.BlockSpec((1, blk, w), lambda b, j: (b, j, 0)),
                   pl.BlockSpec((1, hh, 1, d, blk), lambda b, j: (b, 0, j, 0, 0)),
                   pl.BlockSpec((1, hh, 1, bias_rows, blk), lambda b, j: (b, 0, j, 0, 0))],
        out_shape=[jax.ShapeDtypeStruct((nbatch, hh, nblk, d, blk), BF16),
                   jax.ShapeDtypeStruct((nbatch, seq, w), BF16),
                   jax.ShapeDtypeStruct((nbatch, hh, nblk, d, blk), BF16),
                   jax.ShapeDtypeStruct((nbatch, hh, nblk, bias_rows, blk), F32)],
        scratch_shapes=[pltpu.VMEM(s, dt) for s, dt in scratch],
        compiler_params=_params(("arbitrary", "arbitrary"), blocks, scratch),
        name="moba_prep",
    )(z3, z3, z3, cos_tab, sin_tab, cos_t, sin_t)


def _pipelined(start, count, first, step, unroll=2):
    assert unroll % 2 == 0
    trips, rest = divmod(count - start, unroll)

    def body(t, carry):
        for e in range(unroll):
            carry = step(start + unroll * t + e, (start + e) % 2, carry)
        return carry

    carry = lax.fori_loop(0, trips, body, first)
    for i in range(count - rest, count):
        carry = step(i, i % 2, carry)
    return carry


def _moba_flat_kernel(tile_ref, group_ref, q_ref, k_ref, bias_ref, vt_ref, o_ref,
                      s_sc, p_sc, acc_sc, m_sc, l_sc):
    nblk, d, blk = vt_ref.shape[2:]
    group = MOBA_GROUP
    gkeys = group * blk
    n_items = tile_ref.shape[0]
    c_exp = (d ** -0.5) * np.log2(np.e).astype(np.float32)
    key = lax.broadcasted_iota(jnp.int32, (blk, blk), 0)
    qry = lax.broadcasted_iota(jnp.int32, (blk, blk), 1)

    def weighted_values(first_blk, p):
        acc = None
        for i in range(p.shape[0] // blk):
            part = jnp.dot(vt_ref[0, 0, first_blk + i], p[i * blk:(i + 1) * blk],
                           preferred_element_type=F32)
            acc = part if acc is None else acc + part
        return acc

    def own_scores(j, slot):
        r0 = pl.multiple_of(j * blk, blk)
        s_t = jnp.dot(k_ref[0, pl.ds(r0, blk), :], q_ref[0, 0, j], preferred_element_type=F32)
        s_t = jnp.where(key <= qry, s_t, -jnp.inf)
        s_sc[slot, :blk, :] = s_t
        return jnp.max(s_t, axis=0, keepdims=True)

    def own_softmax(j, slot, s_max):
        next_max = own_scores(jnp.minimum(j + 1, nblk - 1), 1 - slot)
        p = jnp.exp2((s_sc[slot, :blk, :] - s_max) * c_exp)
        m_sc[j] = s_max
        l_sc[j] = jnp.sum(p, axis=0, keepdims=True)
        p_sc[slot] = p.astype(BF16)
        return next_max

    def own_values(j, slot):
        acc_sc[j] = weighted_values(j, p_sc[slot])

    def own_step(j, slot, s_max):
        own_values(j - 1, 1 - slot)
        return own_softmax(j, slot, s_max)

    _pipelined(1, nblk, own_softmax(0, 0, own_scores(0, 0)), own_step)
    own_values(nblk - 1, (nblk - 1) % 2)

    def group_scores(i, slot):
        j = tile_ref[i]
        g = group_ref[i]
        c0 = pl.multiple_of(g * gkeys, gkeys)
        s_t = jnp.dot(k_ref[0, pl.ds(c0, gkeys), :], q_ref[0, 0, j], preferred_element_type=F32)
        bias = bias_ref[0, 0, j, pl.ds(pl.multiple_of(g * BIAS_ROWS, BIAS_ROWS), BIAS_ROWS), :]
        s_t = jnp.concatenate([s_t[n * blk:(n + 1) * blk] + bias[n:n + 1] for n in range(group)],
                              axis=0)
        s_sc[slot] = s_t
        return jnp.max(s_t, axis=0, keepdims=True)

    def group_step(i, slot, s_max):
        next_max = group_scores(jnp.minimum(i + 1, n_items - 1), 1 - slot)
        j = tile_ref[i]
        m_run = m_sc[j]
        m_new = jnp.maximum(m_run, s_max)
        alpha = jnp.exp2((m_run - m_new) * c_exp)
        p = jnp.exp2((s_sc[slot] - m_new) * c_exp)
        m_sc[j] = m_new
        l_sc[j] = alpha * l_sc[j] + jnp.sum(p, axis=0, keepdims=True)
        acc_sc[j] = alpha * acc_sc[j] + weighted_values(group_ref[i] * group, p.astype(BF16))
        return next_max

    _pipelined(0, n_items, group_scores(0, 0), group_step, unroll=12)

    for j in range(nblk):
        o_ref[0, j * blk:(j + 1) * blk, :] = (acc_sc[j] / l_sc[j]).T.astype(o_ref.dtype)


def _moba_flat(q_t, kb, vt, bias):
    nbatch, seq, w = kb.shape
    blk = MOBA_BLOCK
    d = HEAD_DIM
    nblk = vt.shape[2]
    group = MOBA_GROUP
    bias_rows = bias.shape[3]
    items = [(j, g) for j in range(nblk) for g in range(-(-j // group))]
    assert items and _tiles(nblk, group)
    item_tile = jnp.asarray([j for j, _ in items], jnp.int32)
    item_group = jnp.asarray([g for _, g in items], jnp.int32)
    blocks = [((nblk, d, blk), BF16), ((1, seq, d), BF16), ((nblk, bias_rows, blk), F32), ((nblk, d, blk), BF16),
              ((1, seq, d), BF16)]
    scratch = [((2, group * blk, blk), F32), ((2, blk, blk), BF16), ((nblk, d, blk), F32),
               ((nblk, 1, blk), F32), ((nblk, 1, blk), F32)]
    grid_spec = pltpu.PrefetchScalarGridSpec(
        num_scalar_prefetch=2,
        grid=(nbatch, MOBA_HEADS),
        in_specs=[pl.BlockSpec((1, 1, nblk, d, blk), lambda b, h, *_: (b, h, 0, 0, 0)),
                  pl.BlockSpec((1, seq, d), lambda b, h, *_: (b, 0, h)),
                  pl.BlockSpec((1, 1, nblk, bias_rows, blk), lambda b, h, *_: (b, h, 0, 0, 0)),
                  pl.BlockSpec((1, 1, nblk, d, blk), lambda b, h, *_: (b, h, 0, 0, 0))],
        out_specs=pl.BlockSpec((1, seq, d), lambda b, h, *_: (b, 0, h)),
        scratch_shapes=[pltpu.VMEM(s, dt) for s, dt in scratch])
    return pl.pallas_call(
        _moba_flat_kernel,
        grid_spec=grid_spec,
        out_shape=jax.ShapeDtypeStruct((nbatch, seq, w), BF16),
        compiler_params=_params(("arbitrary", "arbitrary"), blocks, scratch),
        name="moba_attn",
    )(item_tile, item_group, q_t, kb, bias, vt)


def _gelu_tanh(x):
    c = np.sqrt(2.0 / np.pi).astype(np.float32)
    return x * (0.5 * (1.0 + jnp.tanh(c * (x + 0.044715 * (x * x * x)))))


def _gmlp_kernel(u_ref, v_ref, lg_ref, lb_ref, ws_ref, bst_ref, y_ref):
    rows = u_ref.shape[1]
    t = GMLP_CHUNK
    gd = GMLP_WIDTH // GMLP_GROUPS
    v = _gelu_tanh(v_ref[0])
    mu = jnp.mean(v, axis=-1, keepdims=True)
    vc = v - mu
    vln = vc * lax.rsqrt(jnp.mean(vc * vc, axis=-1, keepdims=True) + NORM_EPS) * lg_ref[...] + lb_ref[...]
    vb = vln.astype(BF16)
    row = lax.broadcasted_iota(jnp.int32, (t, t), 0)
    col = lax.broadcasted_iota(jnp.int32, (t, t), 1)
    for g in range(GMLP_GROUPS):
        wg = jnp.where(col <= row, ws_ref[g], 0.0).astype(BF16)
        bias = bst_ref[:, g:g + 1]
        cols = slice(g * gd, (g + 1) * gd)
        for c in range(rows // t):
            rs = slice(c * t, (c + 1) * t)
            mixed = jnp.dot(wg, vb[rs, cols], preferred_element_type=F32) + bias
            y_ref[0, rs, cols] = (_gelu_tanh(u_ref[0, rs, cols]) * mixed).astype(y_ref.dtype)


def _gmlp(z3, u_cb, v_cb, ln_g, ln_b, ws, bs, rows=1024):
    nbatch, seq, _ = z3.shape
    w = GMLP_WIDTH
    t = GMLP_CHUNK
    blocks = [((1, rows, w), F32), ((1, rows, w), F32), ((GMLP_GROUPS, t, t), F32), ((1, rows, w), BF16)]
    return pl.pallas_call(
        _gmlp_kernel,
        grid=(nbatch, _tiles(seq, rows)),
        in_specs=[pl.BlockSpec((1, rows, w), lambda b, c: (b, c, u_cb)),
                  pl.BlockSpec((1, rows, w), lambda b, c: (b, c, v_cb)),
                  pl.BlockSpec((1, w), lambda b, c: (0, 0)),
                  pl.BlockSpec((1, w), lambda b, c: (0, 0)),
                  pl.BlockSpec((GMLP_GROUPS, t, t), lambda b, c: (0, 0, 0)),
                  pl.BlockSpec((t, GMLP_GROUPS), lambda b, c: (0, 0))],
        out_specs=pl.BlockSpec((1, rows, w), lambda b, c: (b, c, 0)),
        out_shape=jax.ShapeDtypeStruct((nbatch, seq, w), BF16),
        compiler_params=_params(("arbitrary", "arbitrary"), blocks),
        name="gmlp",
    )(z3, z3, ln_g.reshape(1, w), ln_b.reshape(1, w), ws, bs.T)


def _layer_weight_spec(layer, k, tn, index_of):
    return pl.BlockSpec((None, k, tn), lambda *idx: (layer,) + index_of(*idx))


def _merge_kernel(h_ref, wga_ref, wgb_ref, wgc_ref, ya_ref, yb_ref, yc_ref, wa_ref, wb_ref, wc_ref, o_ref):
    h = h_ref[...]

    def branch(wg_ref, y_ref, w_ref):
        gate = jax.nn.sigmoid(lax.dot_general(h, wg_ref[0].astype(BF16), NT_DIMS, preferred_element_type=F32))
        return gate * jnp.dot(y_ref[...], w_ref[...].astype(BF16), preferred_element_type=F32)

    merged = branch(wga_ref, ya_ref, wa_ref) + branch(wgb_ref, yb_ref, wb_ref) + branch(wgc_ref, yc_ref, wc_ref)
    o_ref[...] = merged.astype(o_ref.dtype)


def _merge(h, w_t, gate_row0, ya, yb, yc, wa, wb, wc, layer, tm=1024, tn=256):
    m, d = h.shape
    nblk = _tiles(d, tn)
    ka, kb, kc = ya.shape[1], yb.shape[1], yc.shape[1]
    blocks = ([((tm, d), BF16)] + [((tn, d), F32)] * 3
              + [((tm, ka), BF16), ((tm, kb), BF16), ((tm, kc), BF16)]
              + [((ka, tn), F32), ((kb, tn), F32), ((kc, tn), F32), ((tm, tn), BF16)])

    def gate_spec(branch):
        return _row_window(tn, d, lambda i, j: (layer, pl.multiple_of(gate_row0 + branch * d + j * tn, 8), 0))

    def col(i, j):
        return (0, j)

    return pl.pallas_call(
        _merge_kernel,
        grid=(_tiles(m, tm), nblk),
        in_specs=[pl.BlockSpec((tm, d), lambda i, j: (i, 0)),
                  gate_spec(0), gate_spec(1), gate_spec(2),
                  pl.BlockSpec((tm, ka), lambda i, j: (i, 0)),
                  pl.BlockSpec((tm, kb), lambda i, j: (i, 0)),
                  pl.BlockSpec((tm, kc), lambda i, j: (i, 0)),
                  _layer_weight_spec(layer, ka, tn, col),
                  _layer_weight_spec(layer, kb, tn, col),
                  _layer_weight_spec(layer, kc, tn, col)],
        out_specs=pl.BlockSpec((tm, tn), lambda i, j: (i, j)),
        out_shape=jax.ShapeDtypeStruct((m, d), BF16),
        compiler_params=_params(("arbitrary", "arbitrary"), blocks),
        name="merge",
    )(h, w_t, w_t, w_t, ya, yb, yc, wa, wb, wc)


def _residual_matmul_kernel(a_ref, w_ref, r_ref, o_ref):
    @pl.when(pl.program_id(2) == 0)
    def _():
        o_ref[...] = r_ref[...]

    o_ref[...] += jnp.dot(a_ref[...], w_ref[...].astype(BF16), preferred_element_type=F32)


def _residual_matmul(a, w, layer, res, tm=1024, tn=1024, tk=1024):
    m, k = a.shape
    n = w.shape[2]
    blocks = [((tm, tk), BF16), ((tk, tn), F32), ((tm, tn), F32), ((tm, tn), F32)]
    return pl.pallas_call(
        _residual_matmul_kernel,
        grid=(_tiles(m, tm), _tiles(n, tn), _tiles(k, tk)),
        in_specs=[pl.BlockSpec((tm, tk), lambda i, j, kk: (i, kk)),
                  _layer_weight_spec(layer, tk, tn, lambda i, j, kk: (kk, j)),
                  pl.BlockSpec((tm, tn), lambda i, j, kk: (i, j))],
        out_specs=pl.BlockSpec((tm, tn), lambda i, j, kk: (i, j)),
        out_shape=jax.ShapeDtypeStruct((m, n), F32),
        compiler_params=_params(("arbitrary", "arbitrary", "arbitrary"), blocks),
        name="residual_matmul",
    )(a, w, res)


def _out_proj_kernel(a_ref, w_ref, r_ref, g_ref, x_ref, h_ref):
    x_ref[...] = r_ref[...] + jnp.dot(a_ref[...], w_ref[...].astype(BF16), preferred_element_type=F32)
    _rmsnorm_rows(x_ref, g_ref, h_ref)


def _out_proj(a, w, layer, res, g, tm=512):
    m, k = a.shape
    d = w.shape[2]
    blocks = [((tm, k), BF16), ((k, d), w.dtype), ((tm, d), F32), ((tm, d), F32), ((tm, d), BF16)]
    return pl.pallas_call(
        _out_proj_kernel,
        grid=(_tiles(m, tm),),
        in_specs=[pl.BlockSpec((tm, k), lambda i: (i, 0)),
                  _layer_weight_spec(layer, k, d, lambda i: (0, 0)),
                  pl.BlockSpec((tm, d), lambda i: (i, 0)),
                  pl.BlockSpec((1, d), lambda i: (0, 0))],
        out_specs=[pl.BlockSpec((tm, d), lambda i: (i, 0)),
                   pl.BlockSpec((tm, d), lambda i: (i, 0))],
        out_shape=[jax.ShapeDtypeStruct((m, d), F32),
                   jax.ShapeDtypeStruct((m, d), BF16)],
        compiler_params=_params(("arbitrary",), blocks),
        name="out_proj",
    )(a, w, res, g.reshape(1, d))


def _relu2_matmul_kernel(h_ref, w_ref, o_ref):
    up = jnp.maximum(jnp.dot(h_ref[...], w_ref[...].astype(BF16), preferred_element_type=F32), 0.0)
    o_ref[...] = (up * up).astype(o_ref.dtype)


def _relu2_matmul(h, w, layer, tm=2048, tn=1024):
    m, k = h.shape
    n = w.shape[2]
    blocks = [((tm, k), BF16), ((k, tn), F32), ((tm, tn), BF16)]
    return pl.pallas_call(
        _relu2_matmul_kernel,
        grid=(_tiles(m, tm), _tiles(n, tn)),
        in_specs=[pl.BlockSpec((tm, k), lambda i, j: (i, 0)),
                  _layer_weight_spec(layer, k, tn, lambda i, j: (0, j))],
        out_specs=pl.BlockSpec((tm, tn), lambda i, j: (i, j)),
        out_shape=jax.ShapeDtypeStruct((m, n), BF16),
        compiler_params=_params(("arbitrary", "arbitrary"), blocks),
        name="mlp_up",
    )(h, w)


def _cast_kernel(w_ref, o_ref):
    o_ref[...] = w_ref[...].astype(o_ref.dtype)


def _cast_bf16(w, rows=512):
    depth, k, n = w.shape
    spec = pl.BlockSpec((None, rows, n), lambda l, r: (l, r, 0))
    return pl.pallas_call(
        _cast_kernel,
        grid=(depth, _tiles(k, rows)),
        in_specs=[spec],
        out_specs=spec,
        out_shape=jax.ShapeDtypeStruct(w.shape, BF16),
        compiler_params=_params(("arbitrary", "arbitrary"), [((rows, n), F32), ((rows, n), BF16)]),
        name="cast_bf16",
    )(w)


def _ple_kernel(x_ref, g_ref, wg_ref, p_ref, wp_ref, fg_ref, o_ref, h_sc, *, final_norm):
    _rmsnorm_rows(x_ref, g_ref, h_sc)
    gate = jax.nn.sigmoid(jnp.dot(h_sc[...], wg_ref[...].astype(BF16), preferred_element_type=F32))
    emb = jnp.dot(p_ref[...].astype(BF16), wp_ref[...].astype(BF16), preferred_element_type=F32)
    out = x_ref[...] + gate * emb
    if final_norm:
        out = out * lax.rsqrt(jnp.mean(out * out, axis=-1, keepdims=True) + NORM_EPS) * fg_ref[...]
    o_ref[...] = out


def _ple(x, g, w_gate, p, w_proj, layer, final_g, final_norm, tm=512):
    m, d = x.shape
    pd = p.shape[2]
    blocks = [((tm, d), F32), ((d, d), w_gate.dtype), ((tm, pd), F32), ((pd, d), F32), ((tm, d), F32)]
    scratch = [((tm, d), BF16)]
    return pl.pallas_call(
        functools.partial(_ple_kernel, final_norm=final_norm),
        grid=(_tiles(m, tm),),
        in_specs=[pl.BlockSpec((tm, d), lambda i: (i, 0)),
                  pl.BlockSpec((1, d), lambda i: (0, 0)),
                  _layer_weight_spec(layer, d, d, lambda i: (0, 0)),
                  pl.BlockSpec((None, tm, pd), lambda i: (layer, i, 0)),
                  _layer_weight_spec(layer, pd, d, lambda i: (0, 0)),
                  pl.BlockSpec((1, d), lambda i: (0, 0))],
        out_specs=pl.BlockSpec((tm, d), lambda i: (i, 0)),
        out_shape=jax.ShapeDtypeStruct((m, d), F32),
        scratch_shapes=[pltpu.VMEM(s, dt) for s, dt in scratch],
        compiler_params=_params(("arbitrary",), blocks, scratch),
        name="ple",
    )(x, g.reshape(1, d), w_gate, p, w_proj, final_g.reshape(1, d))


def kernel(x, p, norm_mix_g, w_in, mlstm_gate_b, mlstm_norm_g, gmlp_norm_g, gmlp_norm_b, gmlp_ws, gmlp_bs,
           w_branch_a, w_branch_b, w_branch_c, w_out, norm_mlp_g, w_mlp_up, w_mlp_down, norm_ple_g,
           w_ple_gate, w_ple_proj, final_norm_g):
    nbatch, seq, d = x.shape
    depth = w_in.shape[0]
    m = nbatch * seq
    assert d == MLSTM_WIDTH + MOBA_WIDTH + GMLP_WIDTH

    qkvo_a = 4 * MLSTM_WIDTH
    gates_if = 2 * MLSTM_HEADS
    main_b = qkvo_a + gates_if
    main_cols = 3 * MOBA_WIDTH + 2 * GMLP_WIDTH
    gate_off = main_b + main_cols
    assert w_in.shape[2] == gate_off + N_BRANCHES * d

    moba_q_cb = qkvo_a // MOBA_WIDTH
    moba_k_cb = moba_q_cb + 1
    moba_v_cb = moba_k_cb + 1
    gmlp_u_cb = (qkvo_a + 3 * MOBA_WIDTH) // GMLP_WIDTH
    gmlp_v_cb = gmlp_u_cb + 1

    z_cols = qkvo_a + main_cols
    colscale = jnp.ones((1, z_cols), F32).at[:, MLSTM_WIDTH:2 * MLSTM_WIDTH].set(HEAD_DIM ** -0.5)
    rope_tables = _rope_tables(seq) + _rope_tables_t(seq)

    xf = x.reshape(m, d)
    p_flat = p.reshape(depth, m, p.shape[-1])
    w_main, w_if, w_t = _win_split(w_in, qkvo_a, gates_if, z_cols)
    w_ple_gate_b = _cast_bf16(w_ple_gate)
    w_out_b = _cast_bf16(w_out)
    for i in range(depth):
        z, zif, h = _inproj(xf, norm_mix_g[i], w_main, w_if, i, colscale)
        z3 = z.reshape(nbatch, seq, z_cols)
        zif3 = zif.reshape(nbatch, seq, GATE_LANES)

        ya = _mlstm(z3, zif3, mlstm_gate_b[i], mlstm_norm_g[i])
        yb = _moba_flat(*_moba_prep_t(z3, rope_tables, moba_q_cb, moba_k_cb, moba_v_cb))
        yc = _gmlp(z3, gmlp_u_cb, gmlp_v_cb, gmlp_norm_g[i], gmlp_norm_b[i], gmlp_ws[i], gmlp_bs[i])

        merged = _merge(h, w_t, gate_off, ya.reshape(m, -1), yb.reshape(m, -1), yc.reshape(m, -1),
                        w_branch_a, w_branch_b, w_branch_c, i)
        xf, h2 = _out_proj(merged, w_out_b, i, xf, norm_mlp_g[i])
        hidden = _relu2_matmul(h2, w_mlp_up, i)
        xf = _residual_matmul(hidden, w_mlp_down, i, xf, tm=2048)
        xf = _ple(xf, norm_ple_g[i], w_ple_gate_b, p_flat, w_ple_proj, i, final_norm_g, i == depth - 1)

    return xf.reshape(nbatch, seq, d)
```

```python
import functools

import jax
import jax.numpy as jnp
import numpy as np
from jax import lax
from jax.experimental import pallas as pl
from jax.experimental.pallas import tpu as pltpu

F32 = jnp.float32
BF16 = jnp.bfloat16

HEAD_DIM = 128
MLSTM_HEADS = 4
MLSTM_WIDTH = MLSTM_HEADS * HEAD_DIM
MOBA_HEADS = 8
MOBA_WIDTH = MOBA_HEADS * HEAD_DIM
MOBA_BLOCK = 256
MOBA_TOPK = 3
ROPE_THETA = 500000.0
ROPE_DIM = HEAD_DIM // 4
GMLP_WIDTH = 512
GMLP_GROUPS = 4
GMLP_CHUNK = 128
N_BRANCHES = 3
NORM_EPS = 1e-6

LANES = 128
V7X_VMEM_BYTES = 64 * 1024 * 1024
VMEM_CEILING = V7X_VMEM_BYTES - 8 * 1024 * 1024

MLSTM_KERNEL_CHUNK = 128
MLSTM_CHUNKS_PER_STEP = 4
STATE_ROWS = 16
GATE_LANES = LANES
MASK_BIG = 2.0 ** 100
BIAS_ROWS = 8
MOBA_GROUP = 4

NT_DIMS = (((1,), (1,)), ((), ()))
TN_DIMS = (((0,), (0,)), ((), ()))


def _tiles(n, t):
    count, rest = divmod(n, t)
    assert rest == 0 and count > 0, (n, t)
    return count


def _nbytes(shape, dtype):
    return int(np.prod(shape)) * jnp.dtype(dtype).itemsize


def _params(semantics, blocks, scratch=()):
    need = 2 * sum(_nbytes(s, d) for s, d in blocks) + sum(_nbytes(s, d) for s, d in scratch)
    limit = min(VMEM_CEILING, need + need // 4 + 4 * 1024 * 1024)
    return pltpu.CompilerParams(dimension_semantics=semantics, vmem_limit_bytes=limit)


def _row_window(rows, k, start_of):
    return pl.BlockSpec((pl.Element(1), pl.Element(rows), pl.Element(k)), start_of)


def _win_split_kernel(wm_ref, wif_ref, main_ref, if_ref):
    main_ref[...] = wm_ref[0].T.astype(BF16)
    _, n_if, k = wif_ref.shape
    padded = jnp.concatenate([wif_ref[0], jnp.zeros((GATE_LANES - n_if, k), F32)], axis=0)
    if_ref[...] = padded.T.astype(BF16)


def _win_split(w_in, head, n_if, main_cols, tn=512):
    depth, k, cols = w_in.shape
    assert head % tn == 0
    w_t = jnp.swapaxes(w_in, 1, 2)
    head_steps = head // tn
    blocks = [((tn, k), F32), ((n_if, k), F32), ((k, tn), BF16), ((k, GATE_LANES), BF16)]

    def main_rows(l, r):
        return (l, pl.multiple_of(r * tn + jnp.where(r >= head_steps, n_if, 0), n_if), 0)

    main, gates_if = pl.pallas_call(
        _win_split_kernel,
        grid=(depth, _tiles(main_cols, tn)),
        in_specs=[_row_window(tn, k, main_rows),
                  _row_window(n_if, k, lambda l, r: (l, head, 0))],
        out_specs=[pl.BlockSpec((None, k, tn), lambda l, r: (l, 0, r)),
                   pl.BlockSpec((None, k, GATE_LANES), lambda l, r: (l, 0, 0))],
        out_shape=[jax.ShapeDtypeStruct((depth, k, main_cols), BF16),
                   jax.ShapeDtypeStruct((depth, k, GATE_LANES), BF16)],
        compiler_params=_params(("arbitrary", "arbitrary"), blocks),
        name="win_split",
    )(w_t, w_t)
    return main, gates_if, w_t


def _rmsnorm_rows(x_ref, g_ref, h_ref, rows=256):
    for r0 in range(0, x_ref.shape[0], rows):
        x = x_ref[r0:r0 + rows, :]
        y = x * lax.rsqrt(jnp.mean(x * x, axis=-1, keepdims=True) + NORM_EPS)
        h_ref[r0:r0 + rows, :] = (y * g_ref[...]).astype(h_ref.dtype)


def _inproj_kernel(x_ref, g_ref, w_ref, wif_ref, cs_ref, z_ref, zif_ref, h_ref):
    @pl.when(pl.program_id(1) == 0)
    def _():
        _rmsnorm_rows(x_ref, g_ref, h_ref)
        zif_ref[...] = jnp.dot(h_ref[...], wif_ref[...], preferred_element_type=F32)

    acc = jnp.dot(h_ref[...], w_ref[...], preferred_element_type=F32)
    z_ref[...] = acc * cs_ref[...]


def _inproj(x, g, w_main, w_if, layer, colscale, tm=1024, tn=1024):
    m, k = x.shape
    n = w_main.shape[2]
    blocks = [((tm, k), F32), ((k, tn), BF16), ((k, GATE_LANES), BF16), ((1, tn), F32),
              ((tm, tn), F32), ((tm, GATE_LANES), F32), ((tm, k), BF16)]
    return pl.pallas_call(
        _inproj_kernel,
        grid=(_tiles(m, tm), _tiles(n, tn)),
        in_specs=[pl.BlockSpec((tm, k), lambda i, j: (i, 0)),
                  pl.BlockSpec((1, k), lambda i, j: (0, 0)),
                  pl.BlockSpec((None, k, tn), lambda i, j: (layer, 0, j)),
                  pl.BlockSpec((None, k, GATE_LANES), lambda i, j: (layer, 0, 0)),
                  pl.BlockSpec((1, tn), lambda i, j: (0, j))],
        out_specs=[pl.BlockSpec((tm, tn), lambda i, j: (i, j)),
                   pl.BlockSpec((tm, GATE_LANES), lambda i, j: (i, 0)),
                   pl.BlockSpec((tm, k), lambda i, j: (i, 0))],
        out_shape=[jax.ShapeDtypeStruct((m, n), F32),
                   jax.ShapeDtypeStruct((m, GATE_LANES), F32),
                   jax.ShapeDtypeStruct((m, k), BF16)],
        compiler_params=_params(("arbitrary", "arbitrary"), blocks),
        name="in_proj",
    )(x, g.reshape(1, k), w_main, w_if, colscale)


def _log_sigmoid(x):
    return jnp.minimum(x, 0.0) - jnp.log1p(jnp.exp(-jnp.abs(x)))


def _exact_tril_matmul(tril, x):
    tril = tril.astype(BF16)
    hi = x.astype(BF16)
    rest = x - hi.astype(F32)
    mid = rest.astype(BF16)
    lo = (rest - mid.astype(F32)).astype(BF16)
    return (jnp.dot(tril, hi, preferred_element_type=F32) + jnp.dot(tril, mid, preferred_element_type=F32)
            + jnp.dot(tril, lo, preferred_element_type=F32))


def _mlstm_kernel(q_ref, k_ref, v_ref, o_ref, zif_ref, gb_ref, ng_ref, y_ref, c_sc, n_sc, m_sc):
    @pl.when(pl.program_id(0) == 0)
    def _():
        c_sc[...] = jnp.zeros_like(c_sc)
        n_sc[...] = jnp.zeros_like(n_sc)
        m_sc[...] = jnp.zeros_like(m_sc)

    for c0 in range(0, q_ref.shape[1], MLSTM_KERNEL_CHUNK):
        rows = pl.ds(c0, MLSTM_KERNEL_CHUNK)
        _mlstm_chunk(q_ref.at[:, rows, :], k_ref.at[:, rows, :], v_ref.at[:, rows, :], o_ref.at[:, rows, :],
                     zif_ref.at[:, rows, :], gb_ref, ng_ref, y_ref.at[:, rows, :], c_sc, n_sc, m_sc)


def _mlstm_chunk(q_ref, k_ref, v_ref, o_ref, zif_ref, gb_ref, ng_ref, y_ref, c_sc, n_sc, m_sc):
    nbatch, chunk, _ = q_ref.shape
    heads = MLSTM_HEADS
    row = lax.broadcasted_iota(jnp.int32, (chunk, chunk), 0)
    col = lax.broadcasted_iota(jnp.int32, (chunk, chunk), 1)
    causal_t = row <= col
    tril = jnp.where(col <= row, 1.0, 0.0).astype(F32)
    lane = lax.broadcasted_iota(jnp.int32, (chunk, GATE_LANES), 1)

    for b in range(nbatch):
        pre = zif_ref[b] + gb_ref[...]
        gates = jnp.where(lane < heads, pre, _log_sigmoid(pre))
        gcum = _exact_tril_matmul(tril, gates)
        gcum_t = gcum.T
        for h in range(heads):
            s = b * heads + h
            sl = slice(h * HEAD_DIM, (h + 1) * HEAD_DIM)
            q = q_ref[b, :, sl]
            k = k_ref[b, :, sl]
            v = v_ref[b, :, sl]
            qb = q.astype(BF16)
            kb = k.astype(BF16)
            vb = v.astype(BF16)
            g_row = gcum_t[heads + h:heads + h + 1, :]
            ig_col = gates[:, h:h + 1] - gcum[:, heads + h:heads + h + 1]
            m_prev = m_sc[s][:, 0:1]
            c_prev = c_sc[s]
            n_prev = n_sc[s]

            log_w = jnp.where(causal_t, g_row + ig_col, -jnp.inf)
            log_a = g_row + m_prev
            m_row = jnp.maximum(jnp.max(log_w, axis=0, keepdims=True), log_a)
            qk = lax.dot_general(kb, qb, NT_DIMS, preferred_element_type=F32) * jnp.exp(log_w - m_row)
            a = jnp.exp(log_a - m_row)
            num = (lax.dot_general(vb, qk.astype(BF16), TN_DIMS, preferred_element_type=F32)
                   + a * lax.dot_general(c_prev.astype(BF16), qb, NT_DIMS,
                                         preferred_element_type=F32))
            n_dot_q = lax.dot_general(n_prev.astype(BF16), qb, NT_DIMS, preferred_element_type=F32)[0:1]
            den = jnp.sum(qk, axis=0, keepdims=True) + a * n_dot_q
            h_out = num / jnp.maximum(jnp.abs(den), jnp.exp(-m_row))

            g_last = g_row[:, chunk - 1:chunk]
            m_new = jnp.maximum(g_last + m_prev, g_last + jnp.max(ig_col, axis=0, keepdims=True))
            decay = jnp.exp(g_last + m_prev - m_new)
            uk = (k * jnp.exp(g_last + ig_col - m_new)).astype(BF16)
            c_sc[s] = decay * c_prev + lax.dot_general(vb, uk, TN_DIMS, preferred_element_type=F32)
            n_sc[s] = decay * n_prev + jnp.dot(jnp.ones((STATE_ROWS, chunk), BF16), uk,
                                               preferred_element_type=F32)
            m_sc[s] = jnp.broadcast_to(m_new, (1, LANES))

            yn = (h_out * lax.rsqrt(jnp.mean(h_out * h_out, axis=0, keepdims=True) + NORM_EPS)).T
            y_ref[b, :, sl] = (jax.nn.sigmoid(o_ref[b, :, sl]) * (yn * ng_ref[:, sl])).astype(y_ref.dtype)


def _mlstm(z3, zif3, gate_b, norm_g):
    nbatch, seq, _ = z3.shape
    chunk = MLSTM_KERNEL_CHUNK * MLSTM_CHUNKS_PER_STEP
    w = MLSTM_WIDTH
    streams = nbatch * MLSTM_HEADS
    gb = jnp.pad(gate_b, (0, GATE_LANES - gate_b.shape[0])).reshape(1, GATE_LANES)
    blocks = [((nbatch, chunk, w), F32)] * 4 + [((nbatch, chunk, GATE_LANES), F32),
                                               ((nbatch, chunk, w), BF16)]
    scratch = [((streams, HEAD_DIM, HEAD_DIM), F32), ((streams, STATE_ROWS, LANES), F32),
               ((streams, 1, LANES), F32)]

    def zcol(cb):
        return pl.BlockSpec((nbatch, chunk, w), lambda c: (0, c, cb))

    return pl.pallas_call(
        _mlstm_kernel,
        grid=(_tiles(seq, chunk),),
        in_specs=[zcol(0), zcol(1), zcol(2), zcol(3),
                  pl.BlockSpec((nbatch, chunk, GATE_LANES), lambda c: (0, c, 0)),
                  pl.BlockSpec((1, GATE_LANES), lambda c: (0, 0)),
                  pl.BlockSpec((1, w), lambda c: (0, 0))],
        out_specs=pl.BlockSpec((nbatch, chunk, w), lambda c: (0, c, 0)),
        out_shape=jax.ShapeDtypeStruct((nbatch, seq, w), BF16),
        scratch_shapes=[pltpu.VMEM(s, d) for s, d in scratch],
        compiler_params=_params(("arbitrary",), blocks, scratch),
        name="mlstm",
    )(z3, z3, z3, z3, zif3, gb, norm_g.reshape(1, w))


def _rope_tables(seq):
    half = ROPE_DIM // 2
    inv_freq = ROPE_THETA ** (-jnp.arange(0, ROPE_DIM, 2, dtype=F32) / ROPE_DIM)
    ang = jnp.arange(seq, dtype=F32)[:, None] * inv_freq[None, :]
    cos = jnp.cos(ang)
    sin = jnp.sin(ang)
    ones = jnp.ones((seq, HEAD_DIM - ROPE_DIM), F32)
    cos_tab = jnp.concatenate([cos, cos, ones], axis=1)
    sin_tab = jnp.concatenate([-sin, sin, 0.0 * ones], axis=1)
    assert cos_tab.shape == (seq, HEAD_DIM) and half * 2 == ROPE_DIM
    return cos_tab, sin_tab


def _rotary(t, cos, sin):
    half = ROPE_DIM // 2
    lane = lax.broadcasted_iota(jnp.int32, t.shape, 1)
    upper = pltpu.roll(t, HEAD_DIM - half, axis=1)
    lower = pltpu.roll(t, half, axis=1)
    partner = jnp.where(lane < half, upper, lower)
    return jnp.where(lane < ROPE_DIM, t * cos + partner * sin, t)


def _rope_tables_t(seq):
    inv_freq = ROPE_THETA ** (-jnp.arange(0, ROPE_DIM, 2, dtype=F32) / ROPE_DIM)
    ang = jnp.arange(seq, dtype=F32)[:, None] * inv_freq[None, :]
    cos = jnp.cos(ang).T
    sin = jnp.sin(ang).T
    return jnp.concatenate([cos, cos], axis=0), jnp.concatenate([-sin, sin], axis=0)


def _moba_prep_t_kernel(q_ref, k_ref, v_ref, cos_ref, sin_ref, cost_ref, sint_ref,
                        qa_ref, kb_ref, vt_ref, bias_ref, km_sc):
    blk = q_ref.shape[1]
    d = HEAD_DIM
    half = ROPE_DIM // 2
    nsel = km_sc.shape[1]
    j = pl.program_id(1)

    @pl.when(j == 0)
    def _():
        km_sc[...] = jnp.zeros_like(km_sc)

    cos = cos_ref[...]
    sin = sin_ref[...]
    cos_t = cost_ref[...]
    sin_t = sint_ref[...]
    blk_id = lax.broadcasted_iota(jnp.int32, (nsel, blk), 0)
    blk_id_f = blk_id.astype(F32)
    mean_row = lax.broadcasted_iota(jnp.int32, (nsel, d), 0)
    for h in range(MOBA_HEADS):
        sl = slice(h * d, (h + 1) * d)
        q_t = q_ref[0, :, sl].T
        top = q_t[:ROPE_DIM]
        partner = jnp.concatenate([top[half:], top[:half]], axis=0)
        q_t = jnp.concatenate([top * cos_t + partner * sin_t, q_t[ROPE_DIM:]], axis=0)

        gate = jnp.dot(km_sc[h], q_t, preferred_element_type=F32,
                       precision=lax.Precision.HIGHEST)
        gate = jnp.where(blk_id < j, gate, -jnp.inf)
        sel_m1 = jnp.full((nsel, blk), -1.0, F32)
        for _ in range(MOBA_TOPK):
            mx = jnp.max(gate, axis=0, keepdims=True)
            first = jnp.min(jnp.where(gate == mx, blk_id_f, float(nsel)), axis=0, keepdims=True)
            first = jnp.where(mx > -jnp.inf, first, -1.0)
            hit = blk_id_f == first
            sel_m1 = jnp.where(hit, 0.0, sel_m1)
            gate = jnp.where(hit, -jnp.inf, gate)
        qa_ref[0, h, 0] = q_t.astype(BF16)
        for g in range(bias_ref.shape[3] // BIAS_ROWS):
            rows = sel_m1[g * MOBA_GROUP:(g + 1) * MOBA_GROUP] * MASK_BIG
            pad = jnp.zeros((BIAS_ROWS - MOBA_GROUP, blk), F32)
            bias_ref[0, h, 0, g * BIAS_ROWS:(g + 1) * BIAS_ROWS, :] = jnp.concatenate([rows, pad], axis=0)

        kk = _rotary(k_ref[0, :, sl], cos, sin)
        kb_ref[0, :, sl] = kk.astype(BF16)
        km_sc[h] = jnp.where(mean_row == j, jnp.mean(kk, axis=0, keepdims=True), km_sc[h])

        vt_ref[0, h, 0] = v_ref[0, :, sl].T.astype(BF16)


def _moba_prep_t(z3, tables, q_cb, k_cb, v_cb):
    nbatch, seq, _ = z3.shape
    blk = MOBA_BLOCK
    nblk = _tiles(seq, blk)
    nsel = -(-nblk // BIAS_ROWS) * BIAS_ROWS
    bias_rows = _tiles(nblk, MOBA_GROUP) * BIAS_ROWS
    w = MOBA_WIDTH
    d = HEAD_DIM
    hh = MOBA_HEADS
    cos_tab, sin_tab, cos_t, sin_t = tables
    blocks = ([((1, blk, w), F32)] * 3 + [((blk, d), F32)] * 2 + [((ROPE_DIM, blk), F32)] * 2
              + [((hh, d, blk), BF16), ((1, blk, w), BF16), ((hh, d, blk), BF16), ((hh, bias_rows, blk), F32)])
    scratch = [((hh, nsel, d), F32)]
    return pl.pallas_call(
        _moba_prep_t_kernel,
        grid=(nbatch, nblk),
        in_specs=[pl.BlockSpec((1, blk, w), lambda b, j: (b, j, q_cb)),
                  pl.BlockSpec((1, blk, w), lambda b, j: (b, j, k_cb)),
                  pl.BlockSpec((1, blk, w), lambda b, j: (b, j, v_cb)),
                  pl.BlockSpec((blk, d), lambda b, j: (j, 0)),
                  pl.BlockSpec((blk, d), lambda b, j: (j, 0)),
                  pl.BlockSpec((ROPE_DIM, blk), lambda b, j: (0, j)),
                  pl.BlockSpec((ROPE_DIM, blk), lambda b, j: (0, j))],
        out_specs=[pl.BlockSpec((1, hh, 1, d, blk), lambda b, j: (b, 0, j, 0, 0)),
                   pl.BlockSpec((1, blk, w), lambda b, j: (b, j, 0)),
                   pl.BlockSpec((1, hh, 1, d, blk), lambda b, j: (b, 0, j, 0, 0)),
                   pl.BlockSpec((1, hh, 1, bias_rows, blk), lambda b, j: (b, 0, j, 0, 0))],
        out_shape=[jax.ShapeDtypeStruct((nbatch, hh, nblk, d, blk), BF16),
                   jax.ShapeDtypeStruct((nbatch, seq, w), BF16),
                   jax.ShapeDtypeStruct((nbatch, hh, nblk, d, blk), BF16),
                   jax.ShapeDtypeStruct((nbatch, hh, nblk, bias_rows, blk), F32)],
        scratch_shapes=[pltpu.VMEM(s, dt) for s, dt in scratch],
        compiler_params=_params(("arbitrary", "arbitrary"), blocks, scratch),
        name="moba_prep",
    )(z3, z3, z3, cos_tab, sin_tab, cos_t, sin_t)


def _pipelined(start, count, first, step, unroll=2):
    assert unroll % 2 == 0
    trips, rest = divmod(count - start, unroll)

    def body(t, carry):
        for e in range(unroll):
            carry = step(start + unroll * t + e, (start + e) % 2, carry)
        return carry

    carry = lax.fori_loop(0, trips, body, first)
    for i in range(count - rest, count):
        carry = step(i, i % 2, carry)
    return carry


def _moba_flat_kernel(tile_ref, group_ref, q_ref, k_ref, bias_ref, vt_ref, o_ref,
                      s_sc, p_sc, acc_sc, m_sc, l_sc):
    nblk, d, blk = vt_ref.shape[2:]
    group = MOBA_GROUP
    gkeys = group * blk
    n_items = tile_ref.shape[0]
    c_exp = (d ** -0.5) * np.log2(np.e).astype(np.float32)
    key = lax.broadcasted_iota(jnp.int32, (blk, blk), 0)
    qry = lax.broadcasted_iota(jnp.int32, (blk, blk), 1)

    def weighted_values(first_blk, p):
        acc = None
        for i in range(p.shape[0] // blk):
            part = jnp.dot(vt_ref[0, 0, first_blk + i], p[i * blk:(i + 1) * blk],
                           preferred_element_type=F32)
            acc = part if acc is None else acc + part
        return acc

    def own_scores(j, slot):
        r0 = pl.multiple_of(j * blk, blk)
        s_t = jnp.dot(k_ref[0, pl.ds(r0, blk), :], q_ref[0, 0, j], preferred_element_type=F32)
        s_t = jnp.where(key <= qry, s_t, -jnp.inf)
        s_sc[slot, :blk, :] = s_t
        return jnp.max(s_t, axis=0, keepdims=True)

    def own_softmax(j, slot, s_max):
        next_max = own_scores(jnp.minimum(j + 1, nblk - 1), 1 - slot)
        p = jnp.exp2((s_sc[slot, :blk, :] - s_max) * c_exp)
        m_sc[j] = s_max
        l_sc[j] = jnp.sum(p, axis=0, keepdims=True)
        p_sc[slot] = p.astype(BF16)
        return next_max

    def own_values(j, slot):
        acc_sc[j] = weighted_values(j, p_sc[slot])

    def own_step(j, slot, s_max):
        own_values(j - 1, 1 - slot)
        return own_softmax(j, slot, s_max)

    _pipelined(1, nblk, own_softmax(0, 0, own_scores(0, 0)), own_step, unroll=2 * (nblk // 2))
    own_values(nblk - 1, (nblk - 1) % 2)

    def group_scores(i, slot):
        j = tile_ref[i]
        g = group_ref[i]
        c0 = pl.multiple_of(g * gkeys, gkeys)
        s_t = jnp.dot(k_ref[0, pl.ds(c0, gkeys), :], q_ref[0, 0, j], preferred_element_type=F32)
        bias = bias_ref[0, 0, j, pl.ds(pl.multiple_of(g * BIAS_ROWS, BIAS_ROWS), BIAS_ROWS), :]
        s_t = jnp.concatenate([s_t[n * blk:(n + 1) * blk] + bias[n:n + 1] for n in range(group)],
                              axis=0)
        s_sc[slot] = s_t
        return jnp.max(s_t, axis=0, keepdims=True)

    def group_step(i, slot, s_max):
        next_max = group_scores(jnp.minimum(i + 1, n_items - 1), 1 - slot)
        j = tile_ref[i]
        m_run = m_sc[j]
        m_new = jnp.maximum(m_run, s_max)
        alpha = jnp.exp2((m_run - m_new) * c_exp)
        p = jnp.exp2((s_sc[slot] - m_new) * c_exp)
        m_sc[j] = m_new
        l_sc[j] = alpha * l_sc[j] + jnp.sum(p, axis=0, keepdims=True)
        acc_sc[j] = alpha * acc_sc[j] + weighted_values(group_ref[i] * group, p.astype(BF16))
        return next_max

    _pipelined(0, n_items, group_scores(0, 0), group_step, unroll=12)

    for j in range(nblk):
        o_ref[0, j * blk:(j + 1) * blk, :] = (acc_sc[j] / l_sc[j]).T.astype(o_ref.dtype)


def _moba_flat(q_t, kb, vt, bias):
    nbatch, seq, w = kb.shape
    blk = MOBA_BLOCK
    d = HEAD_DIM
    nblk = vt.shape[2]
    group = MOBA_GROUP
    bias_rows = bias.shape[3]
    items = [(j, g) for j in range(nblk) for g in range(-(-j // group))]
    assert items and _tiles(nblk, group)
    item_tile = jnp.asarray([j for j, _ in items], jnp.int32)
    item_group = jnp.asarray([g for _, g in items], jnp.int32)
    blocks = [((nblk, d, blk), BF16), ((1, seq, d), BF16), ((nblk, bias_rows, blk), F32), ((nblk, d, blk), BF16),
              ((1, seq, d), BF16)]
    scratch = [((2, group * blk, blk), F32), ((2, blk, blk), BF16), ((nblk, d, blk), F32),
               ((nblk, 1, blk), F32), ((nblk, 1, blk), F32)]
    grid_spec = pltpu.PrefetchScalarGridSpec(
        num_scalar_prefetch=2,
        grid=(nbatch, MOBA_HEADS),
        in_specs=[pl.BlockSpec((1, 1, nblk, d, blk), lambda b, h, *_: (b, h, 0, 0, 0)),
                  pl.BlockSpec((1, seq, d), lambda b, h, *_: (b, 0, h)),
                  pl.BlockSpec((1, 1, nblk, bias_rows, blk), lambda b, h, *_: (b, h, 0, 0, 0)),
                  pl.BlockSpec((1, 1, nblk, d, blk), lambda b, h, *_: (b, h, 0, 0, 0))],
        out_specs=pl.BlockSpec((1, seq, d), lambda b, h, *_: (b, 0, h)),
        scratch_shapes=[pltpu.VMEM(s, dt) for s, dt in scratch])
    return pl.pallas_call(
        _moba_flat_kernel,
        grid_spec=grid_spec,
        out_shape=jax.ShapeDtypeStruct((nbatch, seq, w), BF16),
        compiler_params=_params(("arbitrary", "arbitrary"), blocks, scratch),
        name="moba_attn",
    )(item_tile, item_group, q_t, kb, bias, vt)


def _gelu_tanh(x):
    c = np.sqrt(2.0 / np.pi).astype(np.float32)
    return x * (0.5 * (1.0 + jnp.tanh(c * (x + 0.044715 * (x * x * x)))))


def _gmlp_kernel(u_ref, v_ref, lg_ref, lb_ref, ws_ref, bst_ref, y_ref):
    rows = u_ref.shape[1]
    t = GMLP_CHUNK
    gd = GMLP_WIDTH // GMLP_GROUPS
    v = _gelu_tanh(v_ref[0])
    mu = jnp.mean(v, axis=-1, keepdims=True)
    vc = v - mu
    vln = vc * lax.rsqrt(jnp.mean(vc * vc, axis=-1, keepdims=True) + NORM_EPS) * lg_ref[...] + lb_ref[...]
    vb = vln.astype(BF16)
    row = lax.broadcasted_iota(jnp.int32, (t, t), 0)
    col = lax.broadcasted_iota(jnp.int32, (t, t), 1)
    for g in range(GMLP_GROUPS):
        wg = jnp.where(col <= row, ws_ref[g], 0.0).astype(BF16)
        bias = bst_ref[:, g:g + 1]
        cols = slice(g * gd, (g + 1) * gd)
        for c in range(rows // t):
            rs = slice(c * t, (c + 1) * t)
            mixed = jnp.dot(wg, vb[rs, cols], preferred_element_type=F32) + bias
            y_ref[0, rs, cols] = (_gelu_tanh(u_ref[0, rs, cols]) * mixed).astype(y_ref.dtype)


def _gmlp(z3, u_cb, v_cb, ln_g, ln_b, ws, bs, rows=1024):
    nbatch, seq, _ = z3.shape
    w = GMLP_WIDTH
    t = GMLP_CHUNK
    blocks = [((1, rows, w), F32), ((1, rows, w), F32), ((GMLP_GROUPS, t, t), F32), ((1, rows, w), BF16)]
    return pl.pallas_call(
        _gmlp_kernel,
        grid=(nbatch, _tiles(seq, rows)),
        in_specs=[pl.BlockSpec((1, rows, w), lambda b, c: (b, c, u_cb)),
                  pl.BlockSpec((1, rows, w), lambda b, c: (b, c, v_cb)),
                  pl.BlockSpec((1, w), lambda b, c: (0, 0)),
                  pl.BlockSpec((1, w), lambda b, c: (0, 0)),
                  pl.BlockSpec((GMLP_GROUPS, t, t), lambda b, c: (0, 0, 0)),
                  pl.BlockSpec((t, GMLP_GROUPS), lambda b, c: (0, 0))],
        out_specs=pl.BlockSpec((1, rows, w), lambda b, c: (b, c, 0)),
        out_shape=jax.ShapeDtypeStruct((nbatch, seq, w), BF16),
        compiler_params=_params(("arbitrary", "arbitrary"), blocks),
        name="gmlp",
    )(z3, z3, ln_g.reshape(1, w), ln_b.reshape(1, w), ws, bs.T)


def _layer_weight_spec(layer, k, tn, index_of):
    return pl.BlockSpec((None, k, tn), lambda *idx: (layer,) + index_of(*idx))


def _merge_kernel(h_ref, wga_ref, wgb_ref, wgc_ref, ya_ref, yb_ref, yc_ref, wa_ref, wb_ref, wc_ref, o_ref):
    h = h_ref[...]

    def branch(wg_ref, y_ref, w_ref):
        gate = jax.nn.sigmoid(lax.dot_general(h, wg_ref[0].astype(BF16), NT_DIMS, preferred_element_type=F32))
        return gate * jnp.dot(y_ref[...], w_ref[...].astype(BF16), preferred_element_type=F32)

    merged = branch(wga_ref, ya_ref, wa_ref) + branch(wgb_ref, yb_ref, wb_ref) + branch(wgc_ref, yc_ref, wc_ref)
    o_ref[...] = merged.astype(o_ref.dtype)


def _merge(h, w_t, gate_row0, ya, yb, yc, wa, wb, wc, layer, tm=1024, tn=256):
    m, d = h.shape
    nblk = _tiles(d, tn)
    ka, kb, kc = ya.shape[1], yb.shape[1], yc.shape[1]
    blocks = ([((tm, d), BF16)] + [((tn, d), F32)] * 3
              + [((tm, ka), BF16), ((tm, kb), BF16), ((tm, kc), BF16)]
              + [((ka, tn), F32), ((kb, tn), F32), ((kc, tn), F32), ((tm, tn), BF16)])

    def gate_spec(branch):
        return _row_window(tn, d, lambda i, j: (layer, pl.multiple_of(gate_row0 + branch * d + j * tn, 8), 0))

    def col(i, j):
        return (0, j)

    return pl.pallas_call(
        _merge_kernel,
        grid=(_tiles(m, tm), nblk),
        in_specs=[pl.BlockSpec((tm, d), lambda i, j: (i, 0)),
                  gate_spec(0), gate_spec(1), gate_spec(2),
                  pl.BlockSpec((tm, ka), lambda i, j: (i, 0)),
                  pl.BlockSpec((tm, kb), lambda i, j: (i, 0)),
                  pl.BlockSpec((tm, kc), lambda i, j: (i, 0)),
                  _layer_weight_spec(layer, ka, tn, col),
                  _layer_weight_spec(layer, kb, tn, col),
                  _layer_weight_spec(layer, kc, tn, col)],
        out_specs=pl.BlockSpec((tm, tn), lambda i, j: (i, j)),
        out_shape=jax.ShapeDtypeStruct((m, d), BF16),
        compiler_params=_params(("arbitrary", "arbitrary"), blocks),
        name="merge",
    )(h, w_t, w_t, w_t, ya, yb, yc, wa, wb, wc)


def _residual_matmul_kernel(a_ref, w_ref, r_ref, o_ref):
    @pl.when(pl.program_id(2) == 0)
    def _():
        o_ref[...] = r_ref[...]

    o_ref[...] += jnp.dot(a_ref[...], w_ref[...].astype(BF16), preferred_element_type=F32)


def _residual_matmul(a, w, layer, res, tm=1024, tn=1024, tk=1024):
    m, k = a.shape
    n = w.shape[2]
    blocks = [((tm, tk), BF16), ((tk, tn), F32), ((tm, tn), F32), ((tm, tn), F32)]
    return pl.pallas_call(
        _residual_matmul_kernel,
        grid=(_tiles(m, tm), _tiles(n, tn), _tiles(k, tk)),
        in_specs=[pl.BlockSpec((tm, tk), lambda i, j, kk: (i, kk)),
                  _layer_weight_spec(layer, tk, tn, lambda i, j, kk: (kk, j)),
                  pl.BlockSpec((tm, tn), lambda i, j, kk: (i, j))],
        out_specs=pl.BlockSpec((tm, tn), lambda i, j, kk: (i, j)),
        out_shape=jax.ShapeDtypeStruct((m, n), F32),
        compiler_params=_params(("arbitrary", "arbitrary", "arbitrary"), blocks),
        name="residual_matmul",
    )(a, w, res)


def _out_proj_kernel(a_ref, w_ref, r_ref, g_ref, x_ref, h_ref):
    x_ref[...] = r_ref[...] + jnp.dot(a_ref[...], w_ref[...].astype(BF16), preferred_element_type=F32)
    _rmsnorm_rows(x_ref, g_ref, h_ref)


def _out_proj(a, w, layer, res, g, tm=512):
    m, k = a.shape
    d = w.shape[2]
    blocks = [((tm, k), BF16), ((k, d), w.dtype), ((tm, d), F32), ((tm, d), F32), ((tm, d), BF16)]
    return pl.pallas_call(
        _out_proj_kernel,
        grid=(_tiles(m, tm),),
        in_specs=[pl.BlockSpec((tm, k), lambda i: (i, 0)),
                  _layer_weight_spec(layer, k, d, lambda i: (0, 0)),
                  pl.BlockSpec((tm, d), lambda i: (i, 0)),
                  pl.BlockSpec((1, d), lambda i: (0, 0))],
        out_specs=[pl.BlockSpec((tm, d), lambda i: (i, 0)),
                   pl.BlockSpec((tm, d), lambda i: (i, 0))],
        out_shape=[jax.ShapeDtypeStruct((m, d), F32),
                   jax.ShapeDtypeStruct((m, d), BF16)],
        compiler_params=_params(("arbitrary",), blocks),
        name="out_proj",
    )(a, w, res, g.reshape(1, d))


def _relu2_matmul_kernel(h_ref, w_ref, o_ref):
    up = jnp.maximum(jnp.dot(h_ref[...], w_ref[...].astype(BF16), preferred_element_type=F32), 0.0)
    o_ref[...] = (up * up).astype(o_ref.dtype)


def _relu2_matmul(h, w, layer, tm=2048, tn=1024):
    m, k = h.shape
    n = w.shape[2]
    blocks = [((tm, k), BF16), ((k, tn), F32), ((tm, tn), BF16)]
    return pl.pallas_call(
        _relu2_matmul_kernel,
        grid=(_tiles(m, tm), _tiles(n, tn)),
        in_specs=[pl.BlockSpec((tm, k), lambda i, j: (i, 0)),
                  _layer_weight_spec(layer, k, tn, lambda i, j: (0, j))],
        out_specs=pl.BlockSpec((tm, tn), lambda i, j: (i, j)),
        out_shape=jax.ShapeDtypeStruct((m, n), BF16),
        compiler_params=_params(("arbitrary", "arbitrary"), blocks),
        name="mlp_up",
    )(h, w)


def _cast_kernel(w_ref, o_ref):
    o_ref[...] = w_ref[...].astype(o_ref.dtype)


def _cast_bf16(w, rows=512):
    depth, k, n = w.shape
    spec = pl.BlockSpec((None, rows, n), lambda l, r: (l, r, 0))
    return pl.pallas_call(
        _cast_kernel,
        grid=(depth, _tiles(k, rows)),
        in_specs=[spec],
        out_specs=spec,
        out_shape=jax.ShapeDtypeStruct(w.shape, BF16),
        compiler_params=_params(("arbitrary", "arbitrary"), [((rows, n), F32), ((rows, n), BF16)]),
        name="cast_bf16",
    )(w)


def _ple_kernel(x_ref, g_ref, wg_ref, p_ref, wp_ref, fg_ref, o_ref, h_sc, *, final_norm):
    _rmsnorm_rows(x_ref, g_ref, h_sc)
    gate = jax.nn.sigmoid(jnp.dot(h_sc[...], wg_ref[...].astype(BF16), preferred_element_type=F32))
    emb = jnp.dot(p_ref[...].astype(BF16), wp_ref[...].astype(BF16), preferred_element_type=F32)
    out = x_ref[...] + gate * emb
    if final_norm:
        out = out * lax.rsqrt(jnp.mean(out * out, axis=-1, keepdims=True) + NORM_EPS) * fg_ref[...]
    o_ref[...] = out


def _ple(x, g, w_gate, p, w_proj, layer, final_g, final_norm, tm=512):
    m, d = x.shape
    pd = p.shape[2]
    blocks = [((tm, d), F32), ((d, d), w_gate.dtype), ((tm, pd), F32), ((pd, d), F32), ((tm, d), F32)]
    scratch = [((tm, d), BF16)]
    return pl.pallas_call(
        functools.partial(_ple_kernel, final_norm=final_norm),
        grid=(_tiles(m, tm),),
        in_specs=[pl.BlockSpec((tm, d), lambda i: (i, 0)),
                  pl.BlockSpec((1, d), lambda i: (0, 0)),
                  _layer_weight_spec(layer, d, d, lambda i: (0, 0)),
                  pl.BlockSpec((None, tm, pd), lambda i: (layer, i, 0)),
                  _layer_weight_spec(layer, pd, d, lambda i: (0, 0)),
                  pl.BlockSpec((1, d), lambda i: (0, 0))],
        out_specs=pl.BlockSpec((tm, d), lambda i: (i, 0)),
        out_shape=jax.ShapeDtypeStruct((m, d), F32),
        scratch_shapes=[pltpu.VMEM(s, dt) for s, dt in scratch],
        compiler_params=_params(("arbitrary",), blocks, scratch),
        name="ple",
    )(x, g.reshape(1, d), w_gate, p, w_proj, final_g.reshape(1, d))


def kernel(x, p, norm_mix_g, w_in, mlstm_gate_b, mlstm_norm_g, gmlp_norm_g, gmlp_norm_b, gmlp_ws, gmlp_bs,
           w_branch_a, w_branch_b, w_branch_c, w_out, norm_mlp_g, w_mlp_up, w_mlp_down, norm_ple_g,
           w_ple_gate, w_ple_proj, final_norm_g):
    nbatch, seq, d = x.shape
    depth = w_in.shape[0]
    m = nbatch * seq
    assert d == MLSTM_WIDTH + MOBA_WIDTH + GMLP_WIDTH

    qkvo_a = 4 * MLSTM_WIDTH
    gates_if = 2 * MLSTM_HEADS
    main_b = qkvo_a + gates_if
    main_cols = 3 * MOBA_WIDTH + 2 * GMLP_WIDTH
    gate_off = main_b + main_cols
    assert w_in.shape[2] == gate_off + N_BRANCHES * d

    moba_q_cb = qkvo_a // MOBA_WIDTH
    moba_k_cb = moba_q_cb + 1
    moba_v_cb = moba_k_cb + 1
    gmlp_u_cb = (qkvo_a + 3 * MOBA_WIDTH) // GMLP_WIDTH
    gmlp_v_cb = gmlp_u_cb + 1

    z_cols = qkvo_a + main_cols
    colscale = jnp.ones((1, z_cols), F32).at[:, MLSTM_WIDTH:2 * MLSTM_WIDTH].set(HEAD_DIM ** -0.5)
    rope_tables = _rope_tables(seq) + _rope_tables_t(seq)

    xf = x.reshape(m, d)
    p_flat = p.reshape(depth, m, p.shape[-1])
    w_main, w_if, w_t = _win_split(w_in, qkvo_a, gates_if, z_cols)
    w_ple_gate_b = _cast_bf16(w_ple_gate)
    w_out_b = _cast_bf16(w_out)
    for i in range(depth):
        z, zif, h = _inproj(xf, norm_mix_g[i], w_main, w_if, i, colscale)
        z3 = z.reshape(nbatch, seq, z_cols)
        zif3 = zif.reshape(nbatch, seq, GATE_LANES)

        ya = _mlstm(z3, zif3, mlstm_gate_b[i], mlstm_norm_g[i])
        yb = _moba_flat(*_moba_prep_t(z3, rope_tables, moba_q_cb, moba_k_cb, moba_v_cb))
        yc = _gmlp(z3, gmlp_u_cb, gmlp_v_cb, gmlp_norm_g[i], gmlp_norm_b[i], gmlp_ws[i], gmlp_bs[i])

        merged = _merge(h, w_t, gate_off, ya.reshape(m, -1), yb.reshape(m, -1), yc.reshape(m, -1),
                        w_branch_a, w_branch_b, w_branch_c, i)
        xf, h2 = _out_proj(merged, w_out_b, i, xf, norm_mlp_g[i])
        hidden = _relu2_matmul(h2, w_mlp_up, i)
        xf = _residual_matmul(hidden, w_mlp_down, i, xf, tm=2048)
        xf = _ple(xf, norm_ple_g[i], w_ple_gate_b, p_flat, w_ple_proj, i, final_norm_g, i == depth - 1)

    return xf.reshape(nbatch, seq, d)
```

```python
import functools

import jax
import jax.numpy as jnp
import numpy as np
from jax import lax
from jax.experimental import pallas as pl
from jax.experimental.pallas import tpu as pltpu

F32 = jnp.float32
BF16 = jnp.bfloat16

HEAD_DIM = 128
MLSTM_HEADS = 4
MLSTM_WIDTH = MLSTM_HEADS * HEAD_DIM
MOBA_HEADS = 8
MOBA_WIDTH = MOBA_HEADS * HEAD_DIM
MOBA_BLOCK = 256
MOBA_TOPK = 3
ROPE_THETA = 500000.0
ROPE_DIM = HEAD_DIM // 4
GMLP_WIDTH = 512
GMLP_GROUPS = 4
GMLP_CHUNK = 128
N_BRANCHES = 3
NORM_EPS = 1e-6

LANES = 128
V7X_VMEM_BYTES = 64 * 1024 * 1024
VMEM_CEILING = V7X_VMEM_BYTES - 8 * 1024 * 1024

MLSTM_KERNEL_CHUNK = 128
MLSTM_CHUNKS_PER_STEP = 4
STATE_ROWS = 16
GATE_LANES = LANES
MASK_BIG = 2.0 ** 100
BIAS_ROWS = 8
MOBA_GROUP = 4

NT_DIMS = (((1,), (1,)), ((), ()))
TN_DIMS = (((0,), (0,)), ((), ()))


def _tiles(n, t):
    count, rest = divmod(n, t)
    assert rest == 0 and count > 0, (n, t)
    return count


def _nbytes(shape, dtype):
    return int(np.prod(shape)) * jnp.dtype(dtype).itemsize


def _params(semantics, blocks, scratch=()):
    need = 2 * sum(_nbytes(s, d) for s, d in blocks) + sum(_nbytes(s, d) for s, d in scratch)
    limit = min(VMEM_CEILING, need + need // 4 + 4 * 1024 * 1024)
    return pltpu.CompilerParams(dimension_semantics=semantics, vmem_limit_bytes=limit)


def _row_window(rows, k, start_of):
    return pl.BlockSpec((pl.Element(1), pl.Element(rows), pl.Element(k)), start_of)


def _win_split_kernel(wm_ref, wif_ref, main_ref, if_ref):
    main_ref[...] = wm_ref[0].T.astype(BF16)
    _, n_if, k = wif_ref.shape
    padded = jnp.concatenate([wif_ref[0], jnp.zeros((GATE_LANES - n_if, k), F32)], axis=0)
    if_ref[...] = padded.T.astype(BF16)


def _win_split(w_in, head, n_if, main_cols, tn=512):
    depth, k, cols = w_in.shape
    assert head % tn == 0
    w_t = jnp.swapaxes(w_in, 1, 2)
    head_steps = head // tn
    blocks = [((tn, k), F32), ((n_if, k), F32), ((k, tn), BF16), ((k, GATE_LANES), BF16)]

    def main_rows(l, r):
        return (l, pl.multiple_of(r * tn + jnp.where(r >= head_steps, n_if, 0), n_if), 0)

    main, gates_if = pl.pallas_call(
        _win_split_kernel,
        grid=(depth, _tiles(main_cols, tn)),
        in_specs=[_row_window(tn, k, main_rows),
                  _row_window(n_if, k, lambda l, r: (l, head, 0))],
        out_specs=[pl.BlockSpec((None, k, tn), lambda l, r: (l, 0, r)),
                   pl.BlockSpec((None, k, GATE_LANES), lambda l, r: (l, 0, 0))],
        out_shape=[jax.ShapeDtypeStruct((depth, k, main_cols), BF16),
                   jax.ShapeDtypeStruct((depth, k, GATE_LANES), BF16)],
        compiler_params=_params(("arbitrary", "arbitrary"), blocks),
        name="win_split",
    )(w_t, w_t)
    return main, gates_if, w_t


def _rmsnorm_rows(x_ref, g_ref, h_ref, rows=256):
    for r0 in range(0, x_ref.shape[0], rows):
        x = x_ref[r0:r0 + rows, :]
        y = x * lax.rsqrt(jnp.mean(x * x, axis=-1, keepdims=True) + NORM_EPS)
        h_ref[r0:r0 + rows, :] = (y * g_ref[...]).astype(h_ref.dtype)


def _inproj_kernel(x_ref, g_ref, w_ref, wif_ref, cs_ref, z_ref, zif_ref, h_ref):
    @pl.when(pl.program_id(1) == 0)
    def _():
        _rmsnorm_rows(x_ref, g_ref, h_ref)
        zif_ref[...] = jnp.dot(h_ref[...], wif_ref[...], preferred_element_type=F32)

    acc = jnp.dot(h_ref[...], w_ref[...], preferred_element_type=F32)
    z_ref[...] = acc * cs_ref[...]


def _inproj(x, g, w_main, w_if, layer, colscale, tm=1024, tn=1024):
    m, k = x.shape
    n = w_main.shape[2]
    blocks = [((tm, k), F32), ((k, tn), BF16), ((k, GATE_LANES), BF16), ((1, tn), F32),
              ((tm, tn), F32), ((tm, GATE_LANES), F32), ((tm, k), BF16)]
    return pl.pallas_call(
        _inproj_kernel,
        grid=(_tiles(m, tm), _tiles(n, tn)),
        in_specs=[pl.BlockSpec((tm, k), lambda i, j: (i, 0)),
                  pl.BlockSpec((1, k), lambda i, j: (0, 0)),
                  pl.BlockSpec((None, k, tn), lambda i, j: (layer, 0, j)),
                  pl.BlockSpec((None, k, GATE_LANES), lambda i, j: (layer, 0, 0)),
                  pl.BlockSpec((1, tn), lambda i, j: (0, j))],
        out_specs=[pl.BlockSpec((tm, tn), lambda i, j: (i, j)),
                   pl.BlockSpec((tm, GATE_LANES), lambda i, j: (i, 0)),
                   pl.BlockSpec((tm, k), lambda i, j: (i, 0))],
        out_shape=[jax.ShapeDtypeStruct((m, n), F32),
                   jax.ShapeDtypeStruct((m, GATE_LANES), F32),
                   jax.ShapeDtypeStruct((m, k), BF16)],
        compiler_params=_params(("arbitrary", "arbitrary"), blocks),
        name="in_proj",
    )(x, g.reshape(1, k), w_main, w_if, colscale)


def _log_sigmoid(x):
    return jnp.minimum(x, 0.0) - jnp.log1p(jnp.exp(-jnp.abs(x)))


def _exact_tril_matmul(tril, x):
    tril = tril.astype(BF16)
    hi = x.astype(BF16)
    rest = x - hi.astype(F32)
    mid = rest.astype(BF16)
    lo = (rest - mid.astype(F32)).astype(BF16)
    return (jnp.dot(tril, hi, preferred_element_type=F32) + jnp.dot(tril, mid, preferred_element_type=F32)
            + jnp.dot(tril, lo, preferred_element_type=F32))


def _mlstm_kernel(q_ref, k_ref, v_ref, o_ref, zif_ref, gb_ref, ng_ref, y_ref, c_sc, n_sc, m_sc):
    @pl.when(pl.program_id(0) == 0)
    def _():
        c_sc[...] = jnp.zeros_like(c_sc)
        n_sc[...] = jnp.zeros_like(n_sc)
        m_sc[...] = jnp.zeros_like(m_sc)

    for c0 in range(0, q_ref.shape[1], MLSTM_KERNEL_CHUNK):
        rows = pl.ds(c0, MLSTM_KERNEL_CHUNK)
        _mlstm_chunk(q_ref.at[:, rows, :], k_ref.at[:, rows, :], v_ref.at[:, rows, :], o_ref.at[:, rows, :],
                     zif_ref.at[:, rows, :], gb_ref, ng_ref, y_ref.at[:, rows, :], c_sc, n_sc, m_sc)


def _mlstm_chunk(q_ref, k_ref, v_ref, o_ref, zif_ref, gb_ref, ng_ref, y_ref, c_sc, n_sc, m_sc):
    nbatch, chunk, _ = q_ref.shape
    heads = MLSTM_HEADS
    row = lax.broadcasted_iota(jnp.int32, (chunk, chunk), 0)
    col = lax.broadcasted_iota(jnp.int32, (chunk, chunk), 1)
    causal_t = row <= col
    tril = jnp.where(col <= row, 1.0, 0.0).astype(F32)
    lane = lax.broadcasted_iota(jnp.int32, (chunk, GATE_LANES), 1)

    for b in range(nbatch):
        pre = zif_ref[b] + gb_ref[...]
        gates = jnp.where(lane < heads, pre, _log_sigmoid(pre))
        gcum = _exact_tril_matmul(tril, gates)
        gcum_t = gcum.T
        for h in range(heads):
            s = b * heads + h
            sl = slice(h * HEAD_DIM, (h + 1) * HEAD_DIM)
            q = q_ref[b, :, sl]
            k = k_ref[b, :, sl]
            v = v_ref[b, :, sl]
            qb = q.astype(BF16)
            kb = k.astype(BF16)
            vb = v.astype(BF16)
            g_row = gcum_t[heads + h:heads + h + 1, :]
            ig_col = gates[:, h:h + 1] - gcum[:, heads + h:heads + h + 1]
            m_prev = m_sc[s][:, 0:1]
            c_prev = c_sc[s]
            n_prev = n_sc[s]

            log_w = jnp.where(causal_t, g_row + ig_col, -jnp.inf)
            log_a = g_row + m_prev
            m_row = jnp.maximum(jnp.max(log_w, axis=0, keepdims=True), log_a)
            qk = lax.dot_general(kb, qb, NT_DIMS, preferred_element_type=F32) * jnp.exp(log_w - m_row)
            a = jnp.exp(log_a - m_row)
            num = (lax.dot_general(vb, qk.astype(BF16), TN_DIMS, preferred_element_type=F32)
                   + a * lax.dot_general(c_prev.astype(BF16), qb, NT_DIMS,
                                         preferred_element_type=F32))
            n_dot_q = lax.dot_general(n_prev.astype(BF16), qb, NT_DIMS, preferred_element_type=F32)[0:1]
            den = jnp.sum(qk, axis=0, keepdims=True) + a * n_dot_q
            h_out = num / jnp.maximum(jnp.abs(den), jnp.exp(-m_row))

            g_last = g_row[:, chunk - 1:chunk]
            m_new = jnp.maximum(g_last + m_prev, g_last + jnp.max(ig_col, axis=0, keepdims=True))
            decay = jnp.exp(g_last + m_prev - m_new)
            uk = (k * jnp.exp(g_last + ig_col - m_new)).astype(BF16)
            c_sc[s] = decay * c_prev + lax.dot_general(vb, uk, TN_DIMS, preferred_element_type=F32)
            n_sc[s] = decay * n_prev + jnp.dot(jnp.ones((STATE_ROWS, chunk), BF16), uk,
                                               preferred_element_type=F32)
            m_sc[s] = jnp.broadcast_to(m_new, (1, LANES))

            yn = (h_out * lax.rsqrt(jnp.mean(h_out * h_out, axis=0, keepdims=True) + NORM_EPS)).T
            y_ref[b, :, sl] = (jax.nn.sigmoid(o_ref[b, :, sl]) * (yn * ng_ref[:, sl])).astype(y_ref.dtype)


def _mlstm(z3, zif3, gate_b, norm_g):
    nbatch, seq, _ = z3.shape
    chunk = MLSTM_KERNEL_CHUNK * MLSTM_CHUNKS_PER_STEP
    w = MLSTM_WIDTH
    streams = nbatch * MLSTM_HEADS
    gb = jnp.pad(gate_b, (0, GATE_LANES - gate_b.shape[0])).reshape(1, GATE_LANES)
    blocks = [((nbatch, chunk, w), F32)] * 4 + [((nbatch, chunk, GATE_LANES), F32),
                                               ((nbatch, chunk, w), BF16)]
    scratch = [((streams, HEAD_DIM, HEAD_DIM), F32), ((streams, STATE_ROWS, LANES), F32),
               ((streams, 1, LANES), F32)]

    def zcol(cb):
        return pl.BlockSpec((nbatch, chunk, w), lambda c: (0, c, cb))

    return pl.pallas_call(
        _mlstm_kernel,
        grid=(_tiles(seq, chunk),),
        in_specs=[zcol(0), zcol(1), zcol(2), zcol(3),
                  pl.BlockSpec((nbatch, chunk, GATE_LANES), lambda c: (0, c, 0)),
                  pl.BlockSpec((1, GATE_LANES), lambda c: (0, 0)),
                  pl.BlockSpec((1, w), lambda c: (0, 0))],
        out_specs=pl.BlockSpec((nbatch, chunk, w), lambda c: (0, c, 0)),
        out_shape=jax.ShapeDtypeStruct((nbatch, seq, w), BF16),
        scratch_shapes=[pltpu.VMEM(s, d) for s, d in scratch],
        compiler_params=_params(("arbitrary",), blocks, scratch),
        name="mlstm",
    )(z3, z3, z3, z3, zif3, gb, norm_g.reshape(1, w))


def _rope_tables(seq):
    half = ROPE_DIM // 2
    inv_freq = ROPE_THETA ** (-jnp.arange(0, ROPE_DIM, 2, dtype=F32) / ROPE_DIM)
    ang = jnp.arange(seq, dtype=F32)[:, None] * inv_freq[None, :]
    cos = jnp.cos(ang)
    sin = jnp.sin(ang)
    ones = jnp.ones((seq, HEAD_DIM - ROPE_DIM), F32)
    cos_tab = jnp.concatenate([cos, cos, ones], axis=1)
    sin_tab = jnp.concatenate([-sin, sin, 0.0 * ones], axis=1)
    assert cos_tab.shape == (seq, HEAD_DIM) and half * 2 == ROPE_DIM
    return cos_tab, sin_tab


def _rotary(t, cos, sin):
    half = ROPE_DIM // 2
    lane = lax.broadcasted_iota(jnp.int32, t.shape, 1)
    upper = pltpu.roll(t, HEAD_DIM - half, axis=1)
    lower = pltpu.roll(t, half, axis=1)
    partner = jnp.where(lane < half, upper, lower)
    return jnp.where(lane < ROPE_DIM, t * cos + partner * sin, t)


def _rope_tables_t(seq):
    inv_freq = ROPE_THETA ** (-jnp.arange(0, ROPE_DIM, 2, dtype=F32) / ROPE_DIM)
    ang = jnp.arange(seq, dtype=F32)[:, None] * inv_freq[None, :]
    cos = jnp.cos(ang).T
    sin = jnp.sin(ang).T
    return jnp.concatenate([cos, cos], axis=0), jnp.concatenate([-sin, sin], axis=0)


def _moba_prep_t_kernel(q_ref, k_ref, v_ref, cos_ref, sin_ref, cost_ref, sint_ref,
                        qa_ref, kb_ref, vt_ref, bias_ref, km_sc):
    blk = q_ref.shape[1]
    d = HEAD_DIM
    half = ROPE_DIM // 2
    nsel = km_sc.shape[1]
    j = pl.program_id(1)

    @pl.when(j == 0)
    def _():
        km_sc[...] = jnp.zeros_like(km_sc)

    cos = cos_ref[...]
    sin = sin_ref[...]
    cos_t = cost_ref[...]
    sin_t = sint_ref[...]
    blk_id = lax.broadcasted_iota(jnp.int32, (nsel, blk), 0)
    blk_id_f = blk_id.astype(F32)
    mean_row = lax.broadcasted_iota(jnp.int32, (nsel, d), 0)
    for h in range(MOBA_HEADS):
        sl = slice(h * d, (h + 1) * d)
        q_t = q_ref[0, :, sl].T
        top = q_t[:ROPE_DIM]
        partner = jnp.concatenate([top[half:], top[:half]], axis=0)
        q_t = jnp.concatenate([top * cos_t + partner * sin_t, q_t[ROPE_DIM:]], axis=0)

        gate = jnp.dot(km_sc[h], q_t, preferred_element_type=F32,
                       precision=lax.Precision.HIGHEST)
        gate = jnp.where(blk_id < j, gate, -jnp.inf)
        sel_m1 = jnp.full((nsel, blk), -1.0, F32)
        for _ in range(MOBA_TOPK):
            mx = jnp.max(gate, axis=0, keepdims=True)
            first = jnp.min(jnp.where(gate == mx, blk_id_f, float(nsel)), axis=0, keepdims=True)
            first = jnp.where(mx > -jnp.inf, first, -1.0)
            hit = blk_id_f == first
            sel_m1 = jnp.where(hit, 0.0, sel_m1)
            gate = jnp.where(hit, -jnp.inf, gate)
        qa_ref[0, h, 0] = q_t.astype(BF16)
        for g in range(bias_ref.shape[3] // BIAS_ROWS):
            rows = sel_m1[g * MOBA_GROUP:(g + 1) * MOBA_GROUP] * MASK_BIG
            pad = jnp.zeros((BIAS_ROWS - MOBA_GROUP, blk), F32)
            bias_ref[0, h, 0, g * BIAS_ROWS:(g + 1) * BIAS_ROWS, :] = jnp.concatenate([rows, pad], axis=0)

        kk = _rotary(k_ref[0, :, sl], cos, sin)
        kb_ref[0, :, sl] = kk.astype(BF16)
        km_sc[h] = jnp.where(mean_row == j, jnp.mean(kk, axis=0, keepdims=True), km_sc[h])

        vt_ref[0, h, 0] = v_ref[0, :, sl].T.astype(BF16)


def _moba_prep_t(z3, tables, q_cb, k_cb, v_cb):
    nbatch, seq, _ = z3.shape
    blk = MOBA_BLOCK
    nblk = _tiles(seq, blk)
    nsel = -(-nblk // BIAS_ROWS) * BIAS_ROWS
    bias_rows = _tiles(nblk, MOBA_GROUP) * BIAS_ROWS
    w = MOBA_WIDTH
    d = HEAD_DIM
    hh = MOBA_HEADS
    cos_tab, sin_tab, cos_t, sin_t = tables
    blocks = ([((1, blk, w), F32)] * 3 + [((blk, d), F32)] * 2 + [((ROPE_DIM, blk), F32)] * 2
              + [((hh, d, blk), BF16), ((1, blk, w), BF16), ((hh, d, blk), BF16), ((hh, bias_rows, blk), F32)])
    scratch = [((hh, nsel, d), F32)]
    return pl.pallas_call(
        _moba_prep_t_kernel,
        grid=(nbatch, nblk),
        in_specs=[pl.BlockSpec((1, blk, w), lambda b, j: (b, j, q_cb)),
                  pl.BlockSpec((1, blk, w), lambda b, j: (b, j, k_cb)),
                  pl.BlockSpec((1, blk, w), lambda b, j: (b, j, v_cb)),
                  pl.BlockSpec((blk, d), lambda b, j: (j, 0)),
                  pl.BlockSpec((blk, d), lambda b, j: (j, 0)),
                  pl.BlockSpec((ROPE_DIM, blk), lambda b, j: (0, j)),
                  pl.BlockSpec((ROPE_DIM, blk), lambda b, j: (0, j))],
        out_specs=[pl.BlockSpec((1, hh, 1, d, blk), lambda b, j: (b, 0, j, 0, 0)),
                   pl.BlockSpec((1, blk, w), lambda b, j: (b, j, 0)),
                   pl.BlockSpec((1, hh, 1, d, blk), lambda b, j: (b, 0, j, 0, 0)),
                   pl.BlockSpec((1, hh, 1, bias_rows, blk), lambda b, j: (b, 0, j, 0, 0))],
        out_shape=[jax.ShapeDtypeStruct((nbatch, hh, nblk, d, blk), BF16),
                   jax.ShapeDtypeStruct((nbatch, seq, w), BF16),
                   jax.ShapeDtypeStruct((nbatch, hh, nblk, d, blk), BF16),
                   jax.ShapeDtypeStruct((nbatch, hh, nblk, bias_rows, blk), F32)],
        scratch_shapes=[pltpu.VMEM(s, dt) for s, dt in scratch],
        compiler_params=_params(("arbitrary", "arbitrary"), blocks, scratch),
        name="moba_prep",
    )(z3, z3, z3, cos_tab, sin_tab, cos_t, sin_t)


def _pipelined(start, count, first, step, unroll=2):
    assert unroll % 2 == 0
    trips, rest = divmod(count - start, unroll)

    def body(t, carry):
        for e in range(unroll):
            carry = step(start + unroll * t + e, (start + e) % 2, carry)
        return carry

    carry = lax.fori_loop(0, trips, body, first)
    for i in range(count - rest, count):
        carry = step(i, i % 2, carry)
    return carry


def _moba_flat_kernel(tile_ref, group_ref, q_ref, k_ref, bias_ref, vt_ref, o_ref,
                      s_sc, p_sc, acc_sc, m_sc, l_sc):
    nblk, d, blk = vt_ref.shape[2:]
    group = MOBA_GROUP
    gkeys = group * blk
    n_items = tile_ref.shape[0]
    c_exp = (d ** -0.5) * np.log2(np.e).astype(np.float32)
    key = lax.broadcasted_iota(jnp.int32, (blk, blk), 0)
    qry = lax.broadcasted_iota(jnp.int32, (blk, blk), 1)

    def weighted_values(first_blk, p):
        acc = None
        for i in range(p.shape[0] // blk):
            part = jnp.dot(vt_ref[0, 0, first_blk + i], p[i * blk:(i + 1) * blk],
                           preferred_element_type=F32)
            acc = part if acc is None else acc + part
        return acc

    def own_scores(j, slot):
        r0 = pl.multiple_of(j * blk, blk)
        s_t = jnp.dot(k_ref[0, pl.ds(r0, blk), :], q_ref[0, 0, j], preferred_element_type=F32)
        s_t = jnp.where(key <= qry, s_t, -jnp.inf)
        s_sc[slot, :blk, :] = s_t
        return jnp.max(s_t, axis=0, keepdims=True)

    def own_softmax(j, slot, s_max):
        next_max = own_scores(jnp.minimum(j + 1, nblk - 1), 1 - slot)
        p = jnp.exp2((s_sc[slot, :blk, :] - s_max) * c_exp)
        m_sc[j] = s_max
        l_sc[j] = jnp.sum(p, axis=0, keepdims=True)
        p_sc[slot] = p.astype(BF16)
        return next_max

    def own_values(j, slot):
        acc_sc[j] = weighted_values(j, p_sc[slot])

    def own_step(j, slot, s_max):
        own_values(j - 1, 1 - slot)
        return own_softmax(j, slot, s_max)

    _pipelined(1, nblk, own_softmax(0, 0, own_scores(0, 0)), own_step)
    own_values(nblk - 1, (nblk - 1) % 2)

    def group_scores(i, slot):
        j = tile_ref[i]
        g = group_ref[i]
        c0 = pl.multiple_of(g * gkeys, gkeys)
        s_t = jnp.dot(k_ref[0, pl.ds(c0, gkeys), :], q_ref[0, 0, j], preferred_element_type=F32)
        bias = bias_ref[0, 0, j, pl.ds(pl.multiple_of(g * BIAS_ROWS, BIAS_ROWS), BIAS_ROWS), :]
        s_t = jnp.concatenate([s_t[n * blk:(n + 1) * blk] + bias[n:n + 1] for n in range(group)],
                              axis=0)
        s_sc[slot] = s_t
        return jnp.max(s_t, axis=0, keepdims=True)

    def group_step(i, slot, s_max):
        next_max = group_scores(jnp.minimum(i + 1, n_items - 1), 1 - slot)
        j = tile_ref[i]
        m_run = m_sc[j]
        m_new = jnp.maximum(m_run, s_max)
        alpha = jnp.exp2((m_run - m_new) * c_exp)
        p = jnp.exp2((s_sc[slot] - m_new) * c_exp)
        m_sc[j] = m_new
        l_sc[j] = alpha * l_sc[j] + jnp.sum(p, axis=0, keepdims=True)
        acc_sc[j] = alpha * acc_sc[j] + weighted_values(group_ref[i] * group, p.astype(BF16))
        return next_max

    _pipelined(0, n_items, group_scores(0, 0), group_step, unroll=12)

    for j in range(nblk):
        o_ref[0, j * blk:(j + 1) * blk, :] = (acc_sc[j] / l_sc[j]).T.astype(o_ref.dtype)


def _moba_flat(q_t, kb, vt, bias):
    nbatch, seq, w = kb.shape
    blk = MOBA_BLOCK
    d = HEAD_DIM
    nblk = vt.shape[2]
    group = MOBA_GROUP
    bias_rows = bias.shape[3]
    items = [(j, g) for j in range(nblk) for g in range(-(-j // group))]
    assert items and _tiles(nblk, group)
    item_tile = jnp.asarray([j for j, _ in items], jnp.int32)
    item_group = jnp.asarray([g for _, g in items], jnp.int32)
    blocks = [((nblk, d, blk), BF16), ((1, seq, d), BF16), ((nblk, bias_rows, blk), F32), ((nblk, d, blk), BF16),
              ((1, seq, d), BF16)]
    scratch = [((2, group * blk, blk), F32), ((2, blk, blk), BF16), ((nblk, d, blk), F32),
               ((nblk, 1, blk), F32), ((nblk, 1, blk), F32)]
    grid_spec = pltpu.PrefetchScalarGridSpec(
        num_scalar_prefetch=2,
        grid=(nbatch, MOBA_HEADS),
        in_specs=[pl.BlockSpec((1, 1, nblk, d, blk), lambda b, h, *_: (b, h, 0, 0, 0)),
                  pl.BlockSpec((1, seq, d), lambda b, h, *_: (b, 0, h)),
                  pl.BlockSpec((1, 1, nblk, bias_rows, blk), lambda b, h, *_: (b, h, 0, 0, 0)),
                  pl.BlockSpec((1, 1, nblk, d, blk), lambda b, h, *_: (b, h, 0, 0, 0))],
        out_specs=pl.BlockSpec((1, seq, d), lambda b, h, *_: (b, 0, h)),
        scratch_shapes=[pltpu.VMEM(s, dt) for s, dt in scratch])
    return pl.pallas_call(
        _moba_flat_kernel,
        grid_spec=grid_spec,
        out_shape=jax.ShapeDtypeStruct((nbatch, seq, w), BF16),
        compiler_params=_params(("arbitrary", "arbitrary"), blocks, scratch),
        name="moba_attn",
    )(item_tile, item_group, q_t, kb, bias, vt)


def _gelu_tanh(x):
    c = np.sqrt(2.0 / np.pi).astype(np.float32)
    return x * (0.5 * (1.0 + jnp.tanh(c * (x + 0.044715 * (x * x * x)))))


def _gmlp_kernel(u_ref, v_ref, lg_ref, lb_ref, ws_ref, bst_ref, y_ref):
    rows = u_ref.shape[1]
    t = GMLP_CHUNK
    gd = GMLP_WIDTH // GMLP_GROUPS
    v = _gelu_tanh(v_ref[0])
    mu = jnp.mean(v, axis=-1, keepdims=True)
    vc = v - mu
    vln = vc * lax.rsqrt(jnp.mean(vc * vc, axis=-1, keepdims=True) + NORM_EPS) * lg_ref[...] + lb_ref[...]
    vb = vln.astype(BF16)
    row = lax.broadcasted_iota(jnp.int32, (t, t), 0)
    col = lax.broadcasted_iota(jnp.int32, (t, t), 1)
    for g in range(GMLP_GROUPS):
        wg = jnp.where(col <= row, ws_ref[g], 0.0).astype(BF16)
        bias = bst_ref[:, g:g + 1]
        cols = slice(g * gd, (g + 1) * gd)
        for c in range(rows // t):
            rs = slice(c * t, (c + 1) * t)
            mixed = jnp.dot(wg, vb[rs, cols], preferred_element_type=F32) + bias
            y_ref[0, rs, cols] = (_gelu_tanh(u_ref[0, rs, cols]) * mixed).astype(y_ref.dtype)


def _gmlp(z3, u_cb, v_cb, ln_g, ln_b, ws, bs, rows=1024):
    nbatch, seq, _ = z3.shape
    w = GMLP_WIDTH
    t = GMLP_CHUNK
    blocks = [((1, rows, w), F32), ((1, rows, w), F32), ((GMLP_GROUPS, t, t), F32), ((1, rows, w), BF16)]
    return pl.pallas_call(
        _gmlp_kernel,
        grid=(nbatch, _tiles(seq, rows)),
        in_specs=[pl.BlockSpec((1, rows, w), lambda b, c: (b, c, u_cb)),
                  pl.BlockSpec((1, rows, w), lambda b, c: (b, c, v_cb)),
                  pl.BlockSpec((1, w), lambda b, c: (0, 0)),
                  pl.BlockSpec((1, w), lambda b, c: (0, 0)),
                  pl.BlockSpec((GMLP_GROUPS, t, t), lambda b, c: (0, 0, 0)),
                  pl.BlockSpec((t, GMLP_GROUPS), lambda b, c: (0, 0))],
        out_specs=pl.BlockSpec((1, rows, w), lambda b, c: (b, c, 0)),
        out_shape=jax.ShapeDtypeStruct((nbatch, seq, w), BF16),
        compiler_params=_params(("arbitrary", "arbitrary"), blocks),
        name="gmlp",
    )(z3, z3, ln_g.reshape(1, w), ln_b.reshape(1, w), ws, bs.T)


def _layer_weight_spec(layer, k, tn, index_of):
    return pl.BlockSpec((None, k, tn), lambda *idx: (layer,) + index_of(*idx))


def _merge_kernel(h_ref, wga_ref, wgb_ref, wgc_ref, ya_ref, yb_ref, yc_ref, wa_ref, wb_ref, wc_ref, o_ref):
    h = h_ref[...]

    def branch(wg_ref, y_ref, w_ref):
        gate = jax.nn.sigmoid(lax.dot_general(h, wg_ref[0].astype(BF16), NT_DIMS, preferred_element_type=F32))
        return gate * jnp.dot(y_ref[...], w_ref[...].astype(BF16), preferred_element_type=F32)

    merged = branch(wga_ref, ya_ref, wa_ref) + branch(wgb_ref, yb_ref, wb_ref) + branch(wgc_ref, yc_ref, wc_ref)
    o_ref[...] = merged.astype(o_ref.dtype)


def _merge(h, w_t, gate_row0, ya, yb, yc, wa, wb, wc, layer, tm=1024, tn=256):
    m, d = h.shape
    nblk = _tiles(d, tn)
    ka, kb, kc = ya.shape[1], yb.shape[1], yc.shape[1]
    blocks = ([((tm, d), BF16)] + [((tn, d), F32)] * 3
              + [((tm, ka), BF16), ((tm, kb), BF16), ((tm, kc), BF16)]
              + [((ka, tn), F32), ((kb, tn), F32), ((kc, tn), F32), ((tm, tn), BF16)])

    def gate_spec(branch):
        return _row_window(tn, d, lambda i, j: (layer, pl.multiple_of(gate_row0 + branch * d + j * tn, 8), 0))

    def col(i, j):
        return (0, j)

    return pl.pallas_call(
        _merge_kernel,
        grid=(_tiles(m, tm), nblk),
        in_specs=[pl.BlockSpec((tm, d), lambda i, j: (i, 0)),
                  gate_spec(0), gate_spec(1), gate_spec(2),
                  pl.BlockSpec((tm, ka), lambda i, j: (i, 0)),
                  pl.BlockSpec((tm, kb), lambda i, j: (i, 0)),
                  pl.BlockSpec((tm, kc), lambda i, j: (i, 0)),
                  _layer_weight_spec(layer, ka, tn, col),
                  _layer_weight_spec(layer, kb, tn, col),
                  _layer_weight_spec(layer, kc, tn, col)],
        out_specs=pl.BlockSpec((tm, tn), lambda i, j: (i, j)),
        out_shape=jax.ShapeDtypeStruct((m, d), BF16),
        compiler_params=_params(("arbitrary", "arbitrary"), blocks),
        name="merge",
    )(h, w_t, w_t, w_t, ya, yb, yc, wa, wb, wc)


def _residual_matmul_kernel(a_ref, w_ref, r_ref, o_ref):
    @pl.when(pl.program_id(2) == 0)
    def _():
        o_ref[...] = r_ref[...]

    o_ref[...] += jnp.dot(a_ref[...], w_ref[...].astype(BF16), preferred_element_type=F32)


def _residual_matmul(a, w, layer, res, tm=1024, tn=1024, tk=1024):
    m, k = a.shape
    n = w.shape[2]
    blocks = [((tm, tk), BF16), ((tk, tn), F32), ((tm, tn), F32), ((tm, tn), F32)]
    return pl.pallas_call(
        _residual_matmul_kernel,
        grid=(_tiles(m, tm), _tiles(n, tn), _tiles(k, tk)),
        in_specs=[pl.BlockSpec((tm, tk), lambda i, j, kk: (i, kk)),
                  _layer_weight_spec(layer, tk, tn, lambda i, j, kk: (kk, j)),
                  pl.BlockSpec((tm, tn), lambda i, j, kk: (i, j))],
        out_specs=pl.BlockSpec((tm, tn), lambda i, j, kk: (i, j)),
        out_shape=jax.ShapeDtypeStruct((m, n), F32),
        compiler_params=_params(("arbitrary", "arbitrary", "arbitrary"), blocks),
        name="residual_matmul",
    )(a, w, res)


def _out_proj_kernel(a_ref, w_ref, r_ref, g_ref, x_ref, h_ref):
    x_ref[...] = r_ref[...] + jnp.dot(a_ref[...], w_ref[...].astype(BF16), preferred_element_type=F32)
    _rmsnorm_rows(x_ref, g_ref, h_ref)


def _out_proj(a, w, layer, res, g, tm=512):
    m, k = a.shape
    d = w.shape[2]
    blocks = [((tm, k), BF16), ((k, d), w.dtype), ((tm, d), F32), ((tm, d), F32), ((tm, d), BF16)]
    return pl.pallas_call(
        _out_proj_kernel,
        grid=(_tiles(m, tm),),
        in_specs=[pl.BlockSpec((tm, k), lambda i: (i, 0)),
                  _layer_weight_spec(layer, k, d, lambda i: (0, 0)),
                  pl.BlockSpec((tm, d), lambda i: (i, 0)),
                  pl.BlockSpec((1, d), lambda i: (0, 0))],
        out_specs=[pl.BlockSpec((tm, d), lambda i: (i, 0)),
                   pl.BlockSpec((tm, d), lambda i: (i, 0))],
        out_shape=[jax.ShapeDtypeStruct((m, d), F32),
                   jax.ShapeDtypeStruct((m, d), BF16)],
        compiler_params=_params(("arbitrary",), blocks),
        name="out_proj",
    )(a, w, res, g.reshape(1, d))


def _relu2_matmul_kernel(h_ref, w_ref, o_ref):
    up = jnp.maximum(jnp.dot(h_ref[...], w_ref[...].astype(BF16), preferred_element_type=F32), 0.0)
    o_ref[...] = (up * up).astype(o_ref.dtype)


def _relu2_matmul(h, w, layer, tm=2048, tn=1024):
    m, k = h.shape
    n = w.shape[2]
    blocks = [((tm, k), BF16), ((k, tn), F32), ((tm, tn), BF16)]
    return pl.pallas_call(
        _relu2_matmul_kernel,
        grid=(_tiles(m, tm), _tiles(n, tn)),
        in_specs=[pl.BlockSpec((tm, k), lambda i, j: (i, 0)),
                  _layer_weight_spec(layer, k, tn, lambda i, j: (0, j))],
        out_specs=pl.BlockSpec((tm, tn), lambda i, j: (i, j)),
        out_shape=jax.ShapeDtypeStruct((m, n), BF16),
        compiler_params=_params(("arbitrary", "arbitrary"), blocks),
        name="mlp_up",
    )(h, w)


def _mlp_fused_kernel(h_ref, wu_ref, wd_ref, o_ref):
    def part():
        up = jnp.maximum(jnp.dot(h_ref[...], wu_ref[...].astype(BF16), preferred_element_type=F32), 0.0)
        return jnp.dot((up * up).astype(BF16), wd_ref[...].astype(BF16), preferred_element_type=F32)

    @pl.when(pl.program_id(1) == 0)
    def _():
        o_ref[...] = part()

    @pl.when(pl.program_id(1) > 0)
    def _():
        o_ref[...] += part()


def _mlp_fused(h, w_up, w_down, layer, tm=1024, tf=512):
    m, k = h.shape
    ff = w_up.shape[2]
    d = w_down.shape[2]
    blocks = [((tm, k), BF16), ((k, tf), F32), ((tf, d), F32), ((tm, d), F32)]
    return pl.pallas_call(
        _mlp_fused_kernel,
        grid=(_tiles(m, tm), _tiles(ff, tf)),
        in_specs=[pl.BlockSpec((tm, k), lambda i, f: (i, 0)),
                  _layer_weight_spec(layer, k, tf, lambda i, f: (0, f)),
                  _layer_weight_spec(layer, tf, d, lambda i, f: (f, 0))],
        out_specs=pl.BlockSpec((tm, d), lambda i, f: (i, 0)),
        out_shape=jax.ShapeDtypeStruct((m, d), F32),
        compiler_params=_params(("arbitrary", "arbitrary"), blocks),
        name="mlp_fused",
    )(h, w_up, w_down)


def _cast_kernel(w_ref, o_ref):
    o_ref[...] = w_ref[...].astype(o_ref.dtype)


def _cast_bf16(w, rows=512):
    depth, k, n = w.shape
    spec = pl.BlockSpec((None, rows, n), lambda l, r: (l, r, 0))
    return pl.pallas_call(
        _cast_kernel,
        grid=(depth, _tiles(k, rows)),
        in_specs=[spec],
        out_specs=spec,
        out_shape=jax.ShapeDtypeStruct(w.shape, BF16),
        compiler_params=_params(("arbitrary", "arbitrary"), [((rows, n), F32), ((rows, n), BF16)]),
        name="cast_bf16",
    )(w)


def _ple_kernel(x_ref, dx_ref, g_ref, wg_ref, p_ref, wp_ref, fg_ref, o_ref, h_sc, *, final_norm):
    o_ref[...] = x_ref[...] + dx_ref[...]
    _rmsnorm_rows(o_ref, g_ref, h_sc)
    gate = jax.nn.sigmoid(jnp.dot(h_sc[...], wg_ref[...].astype(BF16), preferred_element_type=F32))
    emb = jnp.dot(p_ref[...].astype(BF16), wp_ref[...].astype(BF16), preferred_element_type=F32)
    out = o_ref[...] + gate * emb
    if final_norm:
        out = out * lax.rsqrt(jnp.mean(out * out, axis=-1, keepdims=True) + NORM_EPS) * fg_ref[...]
    o_ref[...] = out


def _ple(x, dx, g, w_gate, p, w_proj, layer, final_g, final_norm, tm=512):
    m, d = x.shape
    pd = p.shape[2]
    blocks = [((tm, d), F32), ((tm, d), F32), ((d, d), w_gate.dtype), ((tm, pd), F32), ((pd, d), F32),
              ((tm, d), F32)]
    scratch = [((tm, d), BF16)]
    return pl.pallas_call(
        functools.partial(_ple_kernel, final_norm=final_norm),
        grid=(_tiles(m, tm),),
        in_specs=[pl.BlockSpec((tm, d), lambda i: (i, 0)),
                  pl.BlockSpec((tm, d), lambda i: (i, 0)),
                  pl.BlockSpec((1, d), lambda i: (0, 0)),
                  _layer_weight_spec(layer, d, d, lambda i: (0, 0)),
                  pl.BlockSpec((None, tm, pd), lambda i: (layer, i, 0)),
                  _layer_weight_spec(layer, pd, d, lambda i: (0, 0)),
                  pl.BlockSpec((1, d), lambda i: (0, 0))],
        out_specs=pl.BlockSpec((tm, d), lambda i: (i, 0)),
        out_shape=jax.ShapeDtypeStruct((m, d), F32),
        scratch_shapes=[pltpu.VMEM(s, dt) for s, dt in scratch],
        compiler_params=_params(("arbitrary",), blocks, scratch),
        name="ple",
    )(x, dx, g.reshape(1, d), w_gate, p, w_proj, final_g.reshape(1, d))


def kernel(x, p, norm_mix_g, w_in, mlstm_gate_b, mlstm_norm_g, gmlp_norm_g, gmlp_norm_b, gmlp_ws, gmlp_bs,
           w_branch_a, w_branch_b, w_branch_c, w_out, norm_mlp_g, w_mlp_up, w_mlp_down, norm_ple_g,
           w_ple_gate, w_ple_proj, final_norm_g):
    nbatch, seq, d = x.shape
    depth = w_in.shape[0]
    m = nbatch * seq
    assert d == MLSTM_WIDTH + MOBA_WIDTH + GMLP_WIDTH

    qkvo_a = 4 * MLSTM_WIDTH
    gates_if = 2 * MLSTM_HEADS
    main_b = qkvo_a + gates_if
    main_cols = 3 * MOBA_WIDTH + 2 * GMLP_WIDTH
    gate_off = main_b + main_cols
    assert w_in.shape[2] == gate_off + N_BRANCHES * d

    moba_q_cb = qkvo_a // MOBA_WIDTH
    moba_k_cb = moba_q_cb + 1
    moba_v_cb = moba_k_cb + 1
    gmlp_u_cb = (qkvo_a + 3 * MOBA_WIDTH) // GMLP_WIDTH
    gmlp_v_cb = gmlp_u_cb + 1

    z_cols = qkvo_a + main_cols
    colscale = jnp.ones((1, z_cols), F32).at[:, MLSTM_WIDTH:2 * MLSTM_WIDTH].set(HEAD_DIM ** -0.5)
    rope_tables = _rope_tables(seq) + _rope_tables_t(seq)

    xf = x.reshape(m, d)
    p_flat = p.reshape(depth, m, p.shape[-1])
    w_main, w_if, w_t = _win_split(w_in, qkvo_a, gates_if, z_cols)
    w_ple_gate_b = _cast_bf16(w_ple_gate)
    w_out_b = _cast_bf16(w_out)
    for i in range(depth):
        z, zif, h = _inproj(xf, norm_mix_g[i], w_main, w_if, i, colscale)
        z3 = z.reshape(nbatch, seq, z_cols)
        zif3 = zif.reshape(nbatch, seq, GATE_LANES)

        ya = _mlstm(z3, zif3, mlstm_gate_b[i], mlstm_norm_g[i])
        yb = _moba_flat(*_moba_prep_t(z3, rope_tables, moba_q_cb, moba_k_cb, moba_v_cb))
        yc = _gmlp(z3, gmlp_u_cb, gmlp_v_cb, gmlp_norm_g[i], gmlp_norm_b[i], gmlp_ws[i], gmlp_bs[i])

        merged = _merge(h, w_t, gate_off, ya.reshape(m, -1), yb.reshape(m, -1), yc.reshape(m, -1),
                        w_branch_a, w_branch_b, w_branch_c, i)
        xf, h2 = _out_proj(merged, w_out_b, i, xf, norm_mlp_g[i])
        mlp_out = _mlp_fused(h2, w_mlp_up, w_mlp_down, i)
        xf = _ple(xf, mlp_out, norm_ple_g[i], w_ple_gate_b, p_flat, w_ple_proj, i, final_norm_g, i == depth - 1)

    return xf.reshape(nbatch, seq, d)
```
